```python
import math
import jax, jax.numpy as jnp
from jax import lax
import numpy as np

D_MODEL = 1024
BATCH = 8
SEQ = 8192
DEPTH = 2

GRID_W = 64
HEAD_DIM = 64
ROT_DIM = HEAD_DIM // 4
ROPE_THETA = 500000.0
MIX_WIDTH = D_MODEL
A_HEADS = MIX_WIDTH // (2 * HEAD_DIM)
B_HEADS = MIX_WIDTH // (2 * HEAD_DIM)
DILATED_BRANCHES = ((128, 1), (512, 4), (2048, 16))
A_QBLOCK = 128
NA_ROWS = 8
NA_COLS = 16
NA_QCOLS = 16
LRU_WIDTH = MIX_WIDTH // 2
LRU_BLOCKS = 8
LRU_BW = LRU_WIDTH // LRU_BLOCKS
LRU_C = 8.0
CONV_WIDTH = 4
CONV_PAD = (2, 1)
HGRN_HEADS = 4
HGRN_DK = (MIX_WIDTH // 2) // HGRN_HEADS
HGRN_DV = HGRN_DK
HGRN_CHUNK = 64
MEM_LEN = 256
XA_HEADS = 4
XA_DH = D_MODEL // XA_HEADS
D_FF = 4 * D_MODEL
N_EVEN = (DEPTH + 1) // 2
N_ODD = DEPTH // 2
EPS = 1e-6
EV_IN = 3 * (A_HEADS + B_HEADS) * HEAD_DIM
HG_K = HGRN_HEADS * HGRN_DK
HG_V = HGRN_HEADS * HGRN_DV
OD_IN = 2 * LRU_WIDTH + 3 * HG_K + 2 * HG_V
OD_OUT = LRU_WIDTH + HG_V

kernel_name = 'hybrid_dilated_na_rglru_hgrn2_encoder'


def _rms_norm(x, g):
    xf = x.astype(jnp.float32)
    y = xf * lax.rsqrt(jnp.mean(xf * xf, axis=-1, keepdims=True) + EPS)
    return (y * g.astype(jnp.float32)).astype(x.dtype)


def _partial_rope(t):
    s_len = t.shape[1]
    half = ROT_DIM // 2
    inv = jnp.asarray(ROPE_THETA ** (-np.arange(half) * 2.0 / ROT_DIM), jnp.float32)
    ang = jnp.arange(s_len, dtype=jnp.float32)[:, None] * inv[None, :]
    cos = jnp.cos(ang)[None, :, None, :]
    sin = jnp.sin(ang)[None, :, None, :]
    tf = t.astype(jnp.float32)
    t1, t2, rest = tf[..., :half], tf[..., half:ROT_DIM], tf[..., ROT_DIM:]
    return jnp.concatenate([t1 * cos - t2 * sin, t2 * cos + t1 * sin, rest], axis=-1).astype(t.dtype)


def _dilated_branch(q, k, v, window, dil):
    bn, s_len, nh, dh = q.shape
    L = s_len // dil
    P = window // (2 * dil)
    qb_len = math.gcd(L, A_QBLOCK)
    nb = L // qb_len
    kl = qb_len + 2 * P

    def sub(t):
        return t.reshape(bn, L, dil, nh, dh).transpose(0, 2, 1, 3, 4)

    qs, ks, vs = sub(q), sub(k), sub(v)
    pad = ((0, 0), (0, 0), (P, P), (0, 0), (0, 0))
    idx = np.arange(nb)[:, None] * qb_len + np.arange(kl)[None, :]
    kb = jnp.pad(ks, pad)[:, :, idx]
    vb = jnp.pad(vs, pad)[:, :, idx]
    qb = qs.reshape(bn, dil, nb, qb_len, nh, dh)
    s = jnp.einsum('brnqhc,brnkhc->brnhqk', qb, kb).astype(jnp.float32)
    rel = np.arange(kl)[None, :] - P - np.arange(qb_len)[:, None]
    kpos = np.arange(nb)[:, None, None] * qb_len + np.arange(kl)[None, None, :] - P
    valid = (np.abs(rel) <= P)[None] & (kpos >= 0) & (kpos < L)
    s = jnp.where(valid[None, None, :, None], s, -jnp.inf)
    m = jnp.max(s, axis=-1, keepdims=True)
    p = jnp.exp(s - m)
    den = jnp.sum(p, axis=-1, keepdims=True)
    o = jnp.einsum('brnhqk,brnkhc->brnqhc', (p / den).astype(v.dtype), vb)
    lse = (m + jnp.log(den))[..., 0]
    o = o.reshape(bn, dil, L, nh, dh).transpose(0, 2, 1, 3, 4).reshape(bn, s_len, nh, dh)
    lse = lse.transpose(0, 1, 2, 4, 3).reshape(bn, dil, L, nh).transpose(0, 2, 1, 3).reshape(bn, s_len, nh)
    return o, lse


def _dilated_mixture_attention(q, k, v):
    outs, lses = [], []
    for window, dil in DILATED_BRANCHES:
        o, l = _dilated_branch(q, k, v, window, dil)
        outs.append(o)
        lses.append(l)
    w = jax.nn.softmax(jnp.stack(lses, axis=0), axis=0)
    o = jnp.sum(w[..., None] * jnp.stack(outs, axis=0).astype(jnp.float32), axis=0)
    return o.astype(q.dtype)


def _neighborhood_attention(q, k, v, rpb):
    bn, s_len, nh, dh = q.shape
    rows = s_len // GRID_W
    kh = min(NA_ROWS, rows)
    ncb = GRID_W // NA_QCOLS
    kcols = NA_QCOLS + NA_COLS

    def grid(t):
        return t.reshape(bn, rows, GRID_W, nh, dh)

    qg, kg, vg = grid(q), grid(k), grid(v)
    row_start = np.clip(np.arange(rows) - kh // 2, 0, rows - kh)
    row_bias = row_start[:, None] + np.arange(kh)[None, :] - np.arange(rows)[:, None] + NA_ROWS - 1
    cb_start = np.clip(np.arange(ncb) * NA_QCOLS - NA_COLS // 2, 0, GRID_W - kcols)
    key_col = cb_start[:, None] + np.arange(kcols)[None, :]
    q_col = np.arange(ncb)[:, None] * NA_QCOLS + np.arange(NA_QCOLS)[None, :]
    q_col_start = np.clip(q_col - NA_COLS // 2, 0, GRID_W - NA_COLS)
    kc = key_col[:, None, :]
    col_valid = (kc >= q_col_start[..., None]) & (kc < q_col_start[..., None] + NA_COLS)
    col_bias = np.clip(kc - q_col[..., None], 1 - NA_COLS, NA_COLS - 1) + NA_COLS - 1
    mask = col_valid[None, None, :, :, None, :]

    def one_row(args):
        q_row, r0, rb = args
        k_rows = lax.dynamic_slice_in_dim(kg, r0, kh, axis=1)[:, :, key_col]
        v_rows = lax.dynamic_slice_in_dim(vg, r0, kh, axis=1)[:, :, key_col]
        qb = q_row.reshape(bn, ncb, NA_QCOLS, nh, dh)
        s = jnp.einsum('bjuhc,bijvhc->bhjuiv', qb, k_rows).astype(jnp.float32)
        bias = rpb[:, rb[None, None, :, None], col_bias[:, :, None, :]].astype(jnp.float32)
        s = jnp.where(mask, s + bias[None], -jnp.inf)
        p = jax.nn.softmax(s.reshape(bn, nh, ncb, NA_QCOLS, kh * kcols), axis=-1)
        p = p.reshape(bn, nh, ncb, NA_QCOLS, kh, kcols).astype(v.dtype)
        o = jnp.einsum('bhjuiv,bijvhc->bjuhc', p, v_rows)
        return o.reshape(bn, GRID_W, nh, dh)

    out = lax.map(one_row, (qg.transpose(1, 0, 2, 3, 4),
                            jnp.asarray(row_start, jnp.int32),
                            jnp.asarray(row_bias, jnp.int32)))
    return out.transpose(1, 0, 2, 3, 4).reshape(bn, s_len, nh, dh)


def _even_mixer(h, w_in, w_out, rpb):
    bn, s_len, _ = h.shape
    a_w, b_w = A_HEADS * HEAD_DIM, B_HEADS * HEAD_DIM
    proj = h @ w_in
    qa, ka, va, qb, kb, vb = jnp.split(proj, np.cumsum([a_w, a_w, a_w, b_w, b_w]).tolist(), axis=-1)
    scale = HEAD_DIM ** -0.5

    def heads(t, n):
        return t.reshape(bn, s_len, n, HEAD_DIM)

    y_a = _dilated_mixture_attention(_partial_rope(heads(qa, A_HEADS)) * scale,
                                     _partial_rope(heads(ka, A_HEADS)), heads(va, A_HEADS))
    y_b = _neighborhood_attention(heads(qb, B_HEADS) * scale, heads(kb, B_HEADS), heads(vb, B_HEADS), rpb)
    y = jnp.concatenate([y_a.reshape(bn, s_len, a_w), y_b.reshape(bn, s_len, b_w)], axis=-1)
    return y @ w_out


def _depthwise_conv(u, w, b):
    c = u.shape[-1]
    y = lax.conv_general_dilated(u, w[:, None, :].astype(u.dtype), window_strides=(1,),
                                 padding=[CONV_PAD], dimension_numbers=('NWC', 'WIO', 'NWC'),
                                 feature_group_count=c)
    return y + b.astype(u.dtype)


def _lin_combine(e1, e2):
    a1, b1 = e1
    a2, b2 = e2
    return a1 * a2, a2 * b1 + b2


def _rg_lru(u, wa, ba, wx, bx, lam, reverse):
    bn, s_len, c = u.shape
    ub = u.reshape(bn, s_len, LRU_BLOCKS, LRU_BW)
    r = jax.nn.sigmoid(jnp.einsum('bsnc,ncd->bsnd', ub, wa.astype(jnp.float32)).reshape(bn, s_len, c)
                       + ba.astype(jnp.float32))
    i = jax.nn.sigmoid(jnp.einsum('bsnc,ncd->bsnd', ub, wx.astype(jnp.float32)).reshape(bn, s_len, c)
                       + bx.astype(jnp.float32))
    log_a = -LRU_C * r * jax.nn.softplus(-lam.astype(jnp.float32))
    a = jnp.exp(log_a)
    b = jnp.sqrt(-jnp.expm1(2.0 * log_a)) * (i * u)
    _, hs = lax.associative_scan(_lin_combine, (a, b), reverse=reverse, axis=1)
    return hs


def _gla_chunk_scan(q, k, v, logf):
    bn, s_len, nh, dk = q.shape
    dv = v.shape[-1]
    cl = math.gcd(s_len, HGRN_CHUNK)
    nc = s_len // cl

    def chunks(t):
        return t.reshape(bn, nc, cl, nh, t.shape[-1]).transpose(1, 0, 3, 2, 4)

    tri = np.tril(np.ones((cl, cl), dtype=bool))[:, :, None]

    def step(state, xs):
        qc, kc, vc, gc = xs
        G = jnp.cumsum(gc, axis=2)
        o_inter = jnp.einsum('bhtd,bhdv->bhtv', qc * jnp.exp(G), state)
        diff = G[:, :, :, None, :] - G[:, :, None, :, :]
        decay = jnp.exp(jnp.where(tri, diff, -jnp.inf))
        att = jnp.einsum('bhtd,bhsd,bhtsd->bhts', qc, kc, decay)
        o_intra = jnp.einsum('bhts,bhsv->bhtv', att, vc)
        g_last = G[:, :, -1:, :]
        new_state = (state * jnp.exp(g_last[:, :, 0, :, None])
                     + jnp.einsum('bhsd,bhsv->bhdv', kc * jnp.exp(g_last - G), vc))
        return new_state, o_inter + o_intra

    init = jnp.zeros((bn, nh, dk, dv), jnp.float32)
    _, o = lax.scan(step, init, (chunks(q), chunks(k), chunks(v), chunks(logf)))
    return o.transpose(1, 0, 3, 2, 4).reshape(bn, s_len, nh, dv)


def _hgrn2_gates(f_logit, lb):
    bn, s_len, _ = f_logit.shape
    fl = f_logit.astype(jnp.float32).reshape(bn, s_len, HGRN_HEADS, HGRN_DK)
    lb = lb.astype(jnp.float32).reshape(HGRN_HEADS, HGRN_DK)
    log_f = jnp.logaddexp(jnp.log(lb), jnp.log1p(-lb) + jax.nn.log_sigmoid(fl))
    key = (1.0 - lb) * jax.nn.sigmoid(-fl)
    return key, log_f


def _odd_mixer(h, w_in, w_out, conv_w, conv_b, wa, ba, wx, bx, lam, lb_f, lb_b, gnorm_g):
    bn, s_len, _ = h.shape
    proj = h @ w_in
    sizes = (LRU_WIDTH, LRU_WIDTH, HG_K, HG_K, HG_K, HG_V)
    u, gate, q, f_fw, f_bw, i_in, g_out = jnp.split(proj, np.cumsum(sizes).tolist(), axis=-1)
    u = _depthwise_conv(u, conv_w, conv_b).astype(jnp.float32)
    h_fw = _rg_lru(u, wa[0], ba[0], wx[0], bx[0], lam[0], False)
    h_bw = _rg_lru(u, wa[1], ba[1], wx[1], bx[1], lam[1], True)
    y_c = (h_fw + h_bw) * jax.nn.gelu(gate.astype(jnp.float32))
    qh = jax.nn.silu(q.astype(jnp.float32)).reshape(bn, s_len, HGRN_HEADS, HGRN_DK)
    vh = i_in.astype(jnp.float32).reshape(bn, s_len, HGRN_HEADS, HGRN_DV)
    k_fw, lf_fw = _hgrn2_gates(f_fw, lb_f)
    k_bw, lf_bw = _hgrn2_gates(f_bw, lb_b)

    def flip(t):
        return jnp.flip(t, axis=1)

    o = (_gla_chunk_scan(qh, k_fw, vh, lf_fw)
         + flip(_gla_chunk_scan(flip(qh), flip(k_bw), flip(vh), flip(lf_bw))))
    o = o * lax.rsqrt(jnp.mean(o * o, axis=-1, keepdims=True) + EPS) * gnorm_g.astype(jnp.float32)
    y_d = o.reshape(bn, s_len, HG_V) * jax.nn.silu(g_out.astype(jnp.float32))
    y = jnp.concatenate([y_c, y_d], axis=-1).astype(h.dtype)
    return y @ w_out


def _memory_cross_attention(h, mem_n, wq, wkv, wo):
    bn, s_len, _ = h.shape
    m_len = mem_n.shape[1]
    q = (h @ wq).reshape(bn, s_len, XA_HEADS, XA_DH) * (XA_DH ** -0.5)
    k, v = jnp.split(mem_n @ wkv, 2, axis=-1)
    k = k.reshape(bn, m_len, XA_HEADS, XA_DH)
    v = v.reshape(bn, m_len, XA_HEADS, XA_DH)
    p = jax.nn.softmax(jnp.einsum('bshc,bmhc->bhsm', q, k).astype(jnp.float32), axis=-1)
    o = jnp.einsum('bhsm,bmhc->bshc', p.astype(v.dtype), v).reshape(bn, s_len, D_MODEL)
    return o @ wo


def _sq_relu_mlp(h, w1, w2):
    return jnp.square(jax.nn.relu(h @ w1)) @ w2


def setup_inputs(seed: int = 0) -> dict:
    key = jax.random.key(seed)
    ks = jax.random.split(key, 32)
    f32 = jnp.float32

    def w(k, shape, fan_in):
        return jax.random.normal(k, shape, f32) * (fan_in ** -0.5)

    def gain(k, shape):
        return 1.0 + 0.05 * jax.random.normal(k, shape, f32)

    def small(k, shape, s):
        return s * jax.random.normal(k, shape, f32)

    a0 = jax.random.uniform(ks[18], (N_ODD, 2, LRU_WIDTH), f32, 0.9, 0.999)
    s0 = a0 ** (1.0 / LRU_C)
    lru_lambda = jnp.log(s0) - jnp.log1p(-s0)
    return {
        'x': jax.random.normal(ks[0], (BATCH, SEQ, D_MODEL), f32),
        'mem': jax.random.normal(ks[1], (BATCH, MEM_LEN, D_MODEL), f32),
        'norm_mix_g': gain(ks[2], (DEPTH, D_MODEL)),
        'norm_xa_g': gain(ks[3], (DEPTH, D_MODEL)),
        'norm_mem_g': gain(ks[4], (DEPTH, D_MODEL)),
        'norm_mlp_g': gain(ks[5], (DEPTH, D_MODEL)),
        'final_norm_g': gain(ks[6], (D_MODEL,)),
        'ev_w_in': w(ks[7], (N_EVEN, D_MODEL, EV_IN), D_MODEL),
        'ev_w_out': w(ks[8], (N_EVEN, (A_HEADS + B_HEADS) * HEAD_DIM, D_MODEL), (A_HEADS + B_HEADS) * HEAD_DIM),
        'na_rpb': small(ks[9], (N_EVEN, B_HEADS, 2 * NA_ROWS - 1, 2 * NA_COLS - 1), 0.2),
        'od_w_in': w(ks[10], (N_ODD, D_MODEL, OD_IN), D_MODEL),
        'od_w_out': w(ks[11], (N_ODD, OD_OUT, D_MODEL), OD_OUT),
        'conv_w': w(ks[12], (N_ODD, CONV_WIDTH, LRU_WIDTH), CONV_WIDTH),
        'conv_b': small(ks[13], (N_ODD, LRU_WIDTH), 0.02),
        'lru_wa': w(ks[14], (N_ODD, 2, LRU_BLOCKS, LRU_BW, LRU_BW), LRU_BW),
        'lru_ba': small(ks[15], (N_ODD, 2, LRU_WIDTH), 0.1),
        'lru_wx': w(ks[16], (N_ODD, 2, LRU_BLOCKS, LRU_BW, LRU_BW), LRU_BW),
        'lru_bx': small(ks[17], (N_ODD, 2, LRU_WIDTH), 0.1),
        'lru_lambda': lru_lambda,
        'hgrn_lb_logits': small(ks[19], (DEPTH, 2, HG_K), 0.5),
        'hgrn_norm_g': gain(ks[20], (N_ODD, HGRN_DV)),
        'xa_wq': w(ks[21], (DEPTH, D_MODEL, XA_HEADS * XA_DH), D_MODEL),
        'xa_wkv': w(ks[22], (DEPTH, D_MODEL, 2 * XA_HEADS * XA_DH), D_MODEL),
        'xa_wo': w(ks[23], (DEPTH, XA_HEADS * XA_DH, D_MODEL), XA_HEADS * XA_DH),
        'mlp_w1': w(ks[24], (DEPTH, D_MODEL, D_FF), D_MODEL),
        'mlp_w2': w(ks[25], (DEPTH, D_FF, D_MODEL), D_FF),
    }


def reference(x, mem, norm_mix_g, norm_xa_g, norm_mem_g, norm_mlp_g, final_norm_g,
              ev_w_in, ev_w_out, na_rpb, od_w_in, od_w_out, conv_w, conv_b,
              lru_wa, lru_ba, lru_wx, lru_bx, lru_lambda, hgrn_lb_logits, hgrn_norm_g,
              xa_wq, xa_wkv, xa_wo, mlp_w1, mlp_w2):
    p_lb = jax.nn.softmax(hgrn_lb_logits.astype(jnp.float32), axis=0)
    lower_bounds = jnp.cumsum(p_lb, axis=0) - p_lb[0:1]
    for layer in range(DEPTH):
        h = _rms_norm(x, norm_mix_g[layer])
        if layer % 2 == 0:
            e = layer // 2
            x = x + _even_mixer(h, ev_w_in[e], ev_w_out[e], na_rpb[e])
        else:
            o = layer // 2
            x = x + _odd_mixer(h, od_w_in[o], od_w_out[o], conv_w[o], conv_b[o],
                               lru_wa[o], lru_ba[o], lru_wx[o], lru_bx[o], lru_lambda[o],
                               lower_bounds[layer, 0], lower_bounds[layer, 1], hgrn_norm_g[o])
        x = x + _memory_cross_attention(_rms_norm(x, norm_xa_g[layer]), _rms_norm(mem, norm_mem_g[layer]),
                                        xa_wq[layer], xa_wkv[layer], xa_wo[layer])
        x = x + _sq_relu_mlp(_rms_norm(x, norm_mlp_g[layer]), mlp_w1[layer], mlp_w2[layer])
    return _rms_norm(x, final_norm_g)
```

```python
import functools
import math

import jax
import jax.numpy as jnp
import numpy as np
from jax import lax
from jax.experimental import pallas as pl
from jax.experimental.pallas import tpu as pltpu

F32 = jnp.float32
BF16 = jnp.bfloat16

D_MODEL = 1024
HEAD_DIM = 64
ROT_DIM = 16
ROPE_THETA = 500000.0
N_HEADS_A = 8
N_HEADS_B = 8
GROUP_W = N_HEADS_A * HEAD_DIM
DILATIONS = (1, 4, 16)
WIN_HALF = 64
GRID_W = 64
NA_ROWS = 8
NA_COLS = 16
LRU_W = 512
LRU_BLOCKS = 8
LRU_C = 8.0
HG_HEADS = 4
HG_DK = 128
XA_HEADS = 4
XA_DH = 256
D_FF = 4096
EPS = 1e-6

LANES = 128
V7X_VMEM_LIMIT_BYTES = 56 * 1024 * 1024

ROW_TILE = 512
ATT_Q_BLOCK = 512
ATT_SUB = 128
NA_ROW_GROUP = 8
LRU_T = 128
HG_CHUNK = 128
HG_T = 512


def _cparams(sem):
    return pltpu.CompilerParams(dimension_semantics=sem, vmem_limit_bytes=V7X_VMEM_LIMIT_BYTES)


def _rms(x, g):
    return x * lax.rsqrt(jnp.mean(x * x, axis=-1, keepdims=True) + EPS) * g


def _nt_dot(a, b):
    return lax.dot_general(a, b, (((1,), (1,)), ((), ())), preferred_element_type=F32)


def _dot(a, b):
    return jnp.dot(a, b, preferred_element_type=F32)


def _head_selectors():
    lane = lax.broadcasted_iota(jnp.int32, (1, LANES), 1)
    low = (lane < HEAD_DIM).astype(F32)
    return low.astype(BF16), (1.0 - low).astype(BF16)


def _ev_proj_kernel(x_ref, g_ref, w_ref, rc_ref, rs1_ref, rs2_ref, o_ref):
    h = _rms(x_ref[...], g_ref[...]).astype(BF16)
    scale = HEAD_DIM ** -0.5
    for c in range(6):
        acc = _dot(h, w_ref[:, c * GROUP_W:(c + 1) * GROUP_W])
        if c in (0, 1):
            rc, rs1, rs2 = rc_ref[...], rs1_ref[...], rs2_ref[...]
            parts = []
            for j in range(GROUP_W // LANES):
                t = acc[:, j * LANES:(j + 1) * LANES]
                t = t * rc + pltpu.roll(t, LANES - ROT_DIM // 2, 1) * rs1 + pltpu.roll(t, ROT_DIM // 2, 1) * rs2
                parts.append(t)
            acc = jnp.concatenate(parts, axis=-1)
        if c in (0, 3):
            acc = acc * scale
        o_ref[:, c * GROUP_W:(c + 1) * GROUP_W] = acc.astype(BF16)


def _rope_tables(s_len):
    half = ROT_DIM // 2
    inv = jnp.asarray(ROPE_THETA ** (-np.arange(half) * 2.0 / ROT_DIM), F32)
    ang = jnp.arange(s_len, dtype=F32)[:, None] * inv[None, :]
    cos, sin = jnp.cos(ang), jnp.sin(ang)
    ones = jnp.ones((s_len, HEAD_DIM - ROT_DIM), F32)
    zeros = jnp.zeros((s_len, HEAD_DIM - ROT_DIM), F32)
    zh = jnp.zeros((s_len, half), F32)
    rc = jnp.concatenate([cos, cos, ones], axis=-1)
    rs1 = jnp.concatenate([-sin, zh, zeros], axis=-1)
    rs2 = jnp.concatenate([zh, sin, zeros], axis=-1)
    rep = LANES // HEAD_DIM
    return jnp.tile(rc, (1, rep)), jnp.tile(rs1, (1, rep)), jnp.tile(rs2, (1, rep))


def _ev_proj(x, g, w_bf16):
    bn, s_len, d = x.shape
    tm = min(ROW_TILE, s_len)
    n_out = w_bf16.shape[1]
    rc, rs1, rs2 = _rope_tables(s_len)
    tbl_spec = pl.BlockSpec((tm, LANES), lambda b, i: (i, 0))
    return pl.pallas_call(
        _ev_proj_kernel,
        out_shape=jax.ShapeDtypeStruct((bn, s_len, n_out), BF16),
        grid=(bn, s_len // tm),
        in_specs=[
            pl.BlockSpec((None, tm, d), lambda b, i: (b, i, 0)),
            pl.BlockSpec((1, d), lambda b, i: (0, 0)),
            pl.BlockSpec((d, n_out), lambda b, i: (0, 0)),
            tbl_spec, tbl_spec, tbl_spec,
        ],
        out_specs=pl.BlockSpec((None, tm, n_out), lambda b, i: (b, i, 0)),
        compiler_params=_cparams(("parallel", "parallel")),
        name="ev_proj",
    )(x, g.reshape(1, d), w_bf16, rc, rs1, rs2)


def _dilated_kernel(q_ref, kl_ref, kc_ref, kr_ref, vl_ref, vc_ref, vr_ref, o_ref, lse_ref,
                    kext, vext, *, lq, l_total):
    i = pl.program_id(2)
    kext[0:WIN_HALF, :] = kl_ref[...]
    kext[WIN_HALF:WIN_HALF + lq, :] = kc_ref[...]
    kext[WIN_HALF + lq:, :] = kr_ref[...]
    vext[0:WIN_HALF, :] = vl_ref[...]
    vext[WIN_HALF:WIN_HALF + lq, :] = vc_ref[...]
    vext[WIN_HALF + lq:, :] = vr_ref[...]

    wk = ATT_SUB + 2 * WIN_HALF
    qi = lax.broadcasted_iota(jnp.int32, (ATT_SUB, wk), 0)
    ci = lax.broadcasted_iota(jnp.int32, (ATT_SUB, wk), 1)
    band = (ci - qi >= 0) & (ci - qi <= 2 * WIN_HALF)
    lane = lax.broadcasted_iota(jnp.int32, (ATT_SUB, LANES), 1)
    low_half = lane < HEAD_DIM
    head_sel = _head_selectors()

    for j in range(lq // ATT_SUB):
        base = i * lq + j * ATT_SUB - WIN_HALF
        valid = band & (ci + base >= 0) & (ci + base < l_total)
        bias = jnp.where(valid, 0.0, -jnp.inf).astype(F32)
        rows = slice(j * ATT_SUB, (j + 1) * ATT_SUB)
        wrows = slice(j * ATT_SUB, j * ATT_SUB + wk)
        lse_tile = jnp.zeros((ATT_SUB, LANES), F32)
        for hp in range(GROUP_W // LANES):
            cols = slice(hp * LANES, (hp + 1) * LANES)
            qp = q_ref[rows, cols]
            kp = kext[wrows, cols]
            vp = vext[wrows, cols]
            outs = []
            for which in range(2):
                s = _nt_dot(qp * head_sel[which], kp) + bias
                m = jnp.max(s, axis=-1, keepdims=True)
                p = jnp.exp(s - m)
                den = jnp.sum(p, axis=-1, keepdims=True)
                o = _dot(p.astype(BF16), vp) * (1.0 / den)
                outs.append(o)
                lse = m + jnp.log(den)
                lse_tile = jnp.where(lane == 2 * hp + which, lse, lse_tile)
            o_ref[rows, cols] = jnp.where(low_half, outs[0], outs[1]).astype(BF16)
        lse_ref[rows, :] = lse_tile


def _dilated_branch(qkv, dil):
    bn, s_len, width = qkv.shape
    n_groups = width // GROUP_W
    l_total = s_len // dil
    lq = min(ATT_Q_BLOCK, l_total)
    nblk = l_total // lq
    per = lq // WIN_HALF
    n_halo = l_total // WIN_HALF
    view = qkv.reshape(bn, l_total, dil * width)

    def cur(g):
        return pl.BlockSpec((None, lq, GROUP_W), lambda b, r, i: (b, i, r * n_groups + g))

    def left(g):
        return pl.BlockSpec((None, WIN_HALF, GROUP_W),
                            lambda b, r, i: (b, jnp.maximum(i * per - 1, 0), r * n_groups + g))

    def right(g):
        return pl.BlockSpec((None, WIN_HALF, GROUP_W),
                            lambda b, r, i: (b, jnp.minimum((i + 1) * per, n_halo - 1), r * n_groups + g))

    o, lse = pl.pallas_call(
        functools.partial(_dilated_kernel, lq=lq, l_total=l_total),
        out_shape=(jax.ShapeDtypeStruct((bn, l_total, dil * GROUP_W), BF16),
                   jax.ShapeDtypeStruct((bn, l_total, dil * LANES), F32)),
        grid=(bn, dil, nblk),
        in_specs=[cur(0), left(1), cur(1), right(1), left(2), cur(2), right(2)],
        out_specs=(pl.BlockSpec((None, lq, GROUP_W), lambda b, r, i: (b, i, r)),
                   pl.BlockSpec((None, lq, LANES), lambda b, r, i: (b, i, r))),
        scratch_shapes=[pltpu.VMEM((lq + 2 * WIN_HALF, GROUP_W), BF16),
                        pltpu.VMEM((lq + 2 * WIN_HALF, GROUP_W), BF16)],
        compiler_params=_cparams(("parallel", "parallel", "parallel")),
        name=f"dilated_attn_d{dil}",
    )(view, view, view, view, view, view, view)
    return o.reshape(bn, s_len, GROUP_W), lse.reshape(bn, s_len, LANES)


NA_QT = NA_ROW_GROUP * GRID_W
NA_KT = 2 * NA_QT
NA_PIECE = NA_KT // 4


def _na_tables(rpb, rows):
    n_groups = rows // NA_ROW_GROUP
    rb_all, cb_all, valid_all = [], [], []
    for i0, w0 in ((0, 0), (NA_ROW_GROUP, NA_ROW_GROUP - 4), (rows - NA_ROW_GROUP, rows - 2 * NA_ROW_GROUP)):
        a = np.arange(NA_ROW_GROUP)
        qrow = (i0 + a)[:, None, None, None]
        qc = np.arange(GRID_W)[None, :, None, None]
        krow = (w0 + np.arange(2 * NA_ROW_GROUP))[None, None, :, None]
        kc = np.arange(GRID_W)[None, None, None, :]
        r0 = np.clip(qrow - NA_ROWS // 2, 0, rows - NA_ROWS)
        row_ok = (krow >= r0) & (krow < r0 + NA_ROWS)
        c0 = np.clip(qc - NA_COLS // 2, 0, GRID_W - NA_COLS)
        col_ok = (kc >= c0) & (kc < c0 + NA_COLS)
        rb = np.clip(krow - qrow + NA_ROWS - 1, 0, 2 * NA_ROWS - 2)
        cb = np.clip(kc - qc, 1 - NA_COLS, NA_COLS - 1) + NA_COLS - 1
        shape = (NA_ROW_GROUP, GRID_W, 2 * NA_ROW_GROUP, GRID_W)
        rb_all.append(np.broadcast_to(rb, shape).reshape(NA_QT, NA_KT))
        cb_all.append(np.broadcast_to(cb, shape).reshape(NA_QT, NA_KT))
        valid_all.append(np.broadcast_to(row_ok & col_ok, shape).reshape(NA_QT, NA_KT))
    del n_groups
    rb = np.stack(rb_all)
    cb = np.stack(cb_all)
    valid = np.stack(valid_all)
    bias = rpb.astype(F32)[:, rb, cb]
    bias = jnp.where(valid[None], bias, -jnp.inf)
    return jnp.transpose(bias, (1, 0, 2, 3))


def _na_kernel(q_ref, k0, k1, k2, k3, v0, v1, v2, v3, tb_ref, o_ref):
    ks = (k0, k1, k2, k3)
    vs = (v0, v1, v2, v3)
    lane = lax.broadcasted_iota(jnp.int32, (ATT_SUB, LANES), 1)
    low_half = lane < HEAD_DIM
    head_sel = _head_selectors()
    for hp in range(GROUP_W // LANES):
        cols = slice(hp * LANES, (hp + 1) * LANES)
        for qb in range(NA_QT // ATT_SUB):
            rows = slice(qb * ATT_SUB, (qb + 1) * ATT_SUB)
            qp = q_ref[rows, cols]
            outs = []
            for which in range(2):
                qm = qp * head_sel[which]
                s = jnp.concatenate([_nt_dot(qm, kr[:, cols]) for kr in ks], axis=-1)
                s = s + tb_ref[2 * hp + which, rows, :]
                m = jnp.max(s, axis=-1, keepdims=True)
                p = jnp.exp(s - m)
                den = jnp.sum(p, axis=-1, keepdims=True)
                pb = p.astype(BF16)
                o = _dot(pb[:, 0:NA_PIECE], vs[0][:, cols])
                for t in range(1, 4):
                    o = o + _dot(pb[:, t * NA_PIECE:(t + 1) * NA_PIECE], vs[t][:, cols])
                outs.append(o * (1.0 / den))
            o_ref[rows, cols] = jnp.where(low_half, outs[0], outs[1]).astype(BF16)


def _neighborhood(qkv, rpb):
    bn, s_len, _ = qkv.shape
    rows = s_len // GRID_W
    n_groups = rows // NA_ROW_GROUP
    n_pieces = s_len // NA_PIECE
    tables = _na_tables(rpb, rows)

    def piece(g, t):
        return pl.BlockSpec((None, NA_PIECE, GROUP_W),
                            lambda b, i: (b, jnp.clip(2 * i - 1, 0, n_pieces - 4) + t, g))

    def variant(b, i):
        return (jnp.where(i == 0, 0, jnp.where(i == n_groups - 1, 2, 1)), 0, 0, 0)

    return pl.pallas_call(
        _na_kernel,
        out_shape=jax.ShapeDtypeStruct((bn, s_len, GROUP_W), BF16),
        grid=(bn, n_groups),
        in_specs=[pl.BlockSpec((None, NA_QT, GROUP_W), lambda b, i: (b, i, 3))]
        + [piece(4, t) for t in range(4)] + [piece(5, t) for t in range(4)]
        + [pl.BlockSpec((None, N_HEADS_B, NA_QT, NA_KT), variant)],
        out_specs=pl.BlockSpec((None, NA_QT, GROUP_W), lambda b, i: (b, i, 0)),
        compiler_params=_cparams(("parallel", "arbitrary")),
        name="neighborhood_attn",
    )(qkv, *([qkv] * 8), tables)


def _ev_out_kernel(o1, o2, o3, l1, l2, l3, yb_ref, ex_ref, w_ref, x_ref, out_ref):
    la, lb, lc = l1[...], l2[...], l3[...]
    m = jnp.maximum(jnp.maximum(la, lb), lc)
    ea, eb, ec = jnp.exp(la - m), jnp.exp(lb - m), jnp.exp(lc - m)
    inv = 1.0 / (ea + eb + ec)
    ex = ex_ref[...]

    def expand(w):
        hi = w.astype(BF16)
        lo = (w - hi.astype(F32)).astype(BF16)
        return _dot(hi, ex) + _dot(lo, ex)

    ya = (expand(ea * inv) * o1[...].astype(F32) + expand(eb * inv) * o2[...].astype(F32)
          + expand(ec * inv) * o3[...].astype(F32))
    y = _dot(ya.astype(BF16), w_ref[0:GROUP_W, :]) + _dot(yb_ref[...], w_ref[GROUP_W:, :])
    out_ref[...] = x_ref[...] + y


def _ev_out(x, outs, lses, yb, w_bf16):
    bn, s_len, d = x.shape
    tm = min(ROW_TILE, s_len)
    ex = np.zeros((LANES, GROUP_W), np.float32)
    for h in range(N_HEADS_A):
        ex[h, h * HEAD_DIM:(h + 1) * HEAD_DIM] = 1.0
    o_spec = pl.BlockSpec((None, tm, GROUP_W), lambda b, i: (b, i, 0))
    l_spec = pl.BlockSpec((None, tm, LANES), lambda b, i: (b, i, 0))
    x_spec = pl.BlockSpec((None, tm, d), lambda b, i: (b, i, 0))
    return pl.pallas_call(
        _ev_out_kernel,
        out_shape=jax.ShapeDtypeStruct(x.shape, F32),
        grid=(bn, s_len // tm),
        in_specs=[o_spec, o_spec, o_spec, l_spec, l_spec, l_spec, o_spec,
                  pl.BlockSpec((LANES, GROUP_W), lambda b, i: (0, 0)),
                  pl.BlockSpec((2 * GROUP_W, d), lambda b, i: (0, 0)),
                  x_spec],
        out_specs=x_spec,
        compiler_params=_cparams(("parallel", "parallel")),
        name="ev_out",
    )(*outs, *lses, yb, jnp.asarray(ex, BF16), w_bf16, x)


def _norm_matmul_kernel(x_ref, g_ref, w_ref, o_ref):
    o_ref[...] = _dot(_rms(x_ref[...], g_ref[...]).astype(BF16), w_ref[...]).astype(o_ref.dtype)


def _mem_kv(mem, g, w_bf16):
    bn, m_len, d = mem.shape
    n_out = w_bf16.shape[1]
    return pl.pallas_call(
        _norm_matmul_kernel,
        out_shape=jax.ShapeDtypeStruct((bn, m_len, n_out), BF16),
        grid=(bn,),
        in_specs=[pl.BlockSpec((None, m_len, d), lambda b: (b, 0, 0)),
                  pl.BlockSpec((1, d), lambda b: (0, 0)),
                  pl.BlockSpec((d, n_out), lambda b: (0, 0))],
        out_specs=pl.BlockSpec((None, m_len, n_out), lambda b: (b, 0, 0)),
        compiler_params=_cparams(("parallel",)),
        name="mem_kv",
    )(mem, g.reshape(1, d), w_bf16)


def _xattn_kernel(x_ref, g_ref, wq_ref, kv_ref, wo_ref, out_ref):
    x = x_ref[...]
    q = (_dot(_rms(x, g_ref[...]).astype(BF16), wq_ref[...]) * (XA_DH ** -0.5)).astype(BF16)
    heads = []
    for hd in range(XA_HEADS):
        cols = slice(hd * XA_DH, (hd + 1) * XA_DH)
        s = _nt_dot(q[:, cols], kv_ref[:, cols])
        m = jnp.max(s, axis=-1, keepdims=True)
        p = jnp.exp(s - m)
        den = jnp.sum(p, axis=-1, keepdims=True)
        vcols = slice(D_MODEL + hd * XA_DH, D_MODEL + (hd + 1) * XA_DH)
        heads.append((_dot(p.astype(BF16), kv_ref[:, vcols]) * (1.0 / den)).astype(BF16))
    o = jnp.concatenate(heads, axis=-1)
    out_ref[...] = x + _dot(o, wo_ref[...])


def _xattn(x, g, wq, kv, wo):
    bn, s_len, d = x.shape
    tm = min(ROW_TILE, s_len)
    m_len = kv.shape[1]
    x_spec = pl.BlockSpec((None, tm, d), lambda b, i: (b, i, 0))
    return pl.pallas_call(
        _xattn_kernel,
        out_shape=jax.ShapeDtypeStruct(x.shape, F32),
        grid=(bn, s_len // tm),
        in_specs=[x_spec,
                  pl.BlockSpec((1, d), lambda b, i: (0, 0)),
                  pl.BlockSpec((d, d), lambda b, i: (0, 0)),
                  pl.BlockSpec((None, m_len, 2 * d), lambda b, i: (b, 0, 0)),
                  pl.BlockSpec((d, d), lambda b, i: (0, 0))],
        out_specs=x_spec,
        compiler_params=_cparams(("parallel", "parallel")),
        name="mem_xattn",
    )(x, g.reshape(1, d), wq, kv, wo)


def _mlp_kernel(x_ref, g_ref, w1_ref, w2_ref, gf_ref, out_ref, *, final_norm):
    x = x_ref[...]
    h = _rms(x, g_ref[...]).astype(BF16)
    acc = x
    for c in range(D_FF // D_MODEL):
        cols = slice(c * D_MODEL, (c + 1) * D_MODEL)
        a = jnp.maximum(_dot(h, w1_ref[:, cols]), 0.0)
        acc = acc + _dot((a * a).astype(BF16), w2_ref[cols, :])
    out_ref[...] = _rms(acc, gf_ref[...]) if final_norm else acc


def _mlp(x, g, w1, w2, g_final=None):
    bn, s_len, d = x.shape
    tm = min(ROW_TILE, s_len)
    final_norm = g_final is not None
    gf = (g_final if final_norm else g).reshape(1, d)
    x_spec = pl.BlockSpec((None, tm, d), lambda b, i: (b, i, 0))
    vec = pl.BlockSpec((1, d), lambda b, i: (0, 0))
    return pl.pallas_call(
        functools.partial(_mlp_kernel, final_norm=final_norm),
        out_shape=jax.ShapeDtypeStruct(x.shape, F32),
        grid=(bn, s_len // tm),
        in_specs=[x_spec, vec,
                  pl.BlockSpec((d, D_FF), lambda b, i: (0, 0)),
                  pl.BlockSpec((D_FF, d), lambda b, i: (0, 0)),
                  vec],
        out_specs=x_spec,
        compiler_params=_cparams(("parallel", "parallel")),
        name="mlp_final" if final_norm else "mlp",
    )(x, g.reshape(1, d), w1, w2, gf)


def _od_proj_kernel(x_ref, g_ref, w_ref, ug_ref, hg_ref):
    h = _rms(x_ref[...], g_ref[...]).astype(BF16)
    n_ug = 2 * LRU_W
    for c in range(n_ug // GROUP_W):
        cols = slice(c * GROUP_W, (c + 1) * GROUP_W)
        ug_ref[:, cols] = _dot(h, w_ref[:, cols])
    for c in range(hg_ref.shape[-1] // GROUP_W):
        hg_ref[:, c * GROUP_W:(c + 1) * GROUP_W] = _dot(h, w_ref[:, n_ug + c * GROUP_W:n_ug + (c + 1) * GROUP_W])


def _od_proj(x, g, w_bf16):
    bn, s_len, d = x.shape
    tm = min(ROW_TILE, s_len)
    n_ug = 2 * LRU_W
    n_hg = w_bf16.shape[1] - n_ug
    return pl.pallas_call(
        _od_proj_kernel,
        out_shape=(jax.ShapeDtypeStruct((s_len, bn * n_ug), F32),
                   jax.ShapeDtypeStruct((bn, s_len, n_hg), F32)),
        grid=(bn, s_len // tm),
        in_specs=[pl.BlockSpec((None, tm, d), lambda b, i: (b, i, 0)),
                  pl.BlockSpec((1, d), lambda b, i: (0, 0)),
                  pl.BlockSpec((d, n_ug + n_hg), lambda b, i: (0, 0))],
        out_specs=(pl.BlockSpec((tm, n_ug), lambda b, i: (i, b)),
                   pl.BlockSpec((None, tm, n_hg), lambda b, i: (b, i, 0))),
        compiler_params=_cparams(("parallel", "parallel")),
        name="od_proj",
    )(x, g.reshape(1, d), w_bf16)


def _gelu_tanh(x):
    return 0.5 * x * (1.0 + jnp.tanh(math.sqrt(2.0 / math.pi) * (x + 0.044715 * (x * x * x))))


def _rglru_kernel(*refs, reverse, n_chunks):
    if reverse:
        (ul_ref, u_ref, ur_ref, gate_ref, hf_ref, cw_ref, cb_ref, wa_ref, ba_ref, wx_ref, bx_ref, sp_ref,
         out_ref, ext, a_s, b_s, h_s, carry) = refs
    else:
        (ul_ref, u_ref, ur_ref, cw_ref, cb_ref, wa_ref, ba_ref, wx_ref, bx_ref, sp_ref,
         out_ref, ext, a_s, b_s, carry) = refs
        h_s = out_ref
    step = pl.program_id(0)
    chunk = (n_chunks - 1 - step) if reverse else step
    t_len, bn, c = u_ref.shape

    @pl.when(step == 0)
    def _():
        carry[...] = jnp.zeros_like(carry)

    ext[0:2] = jnp.where(chunk == 0, 0.0, ul_ref[...])
    ext[2:t_len + 2] = u_ref[...]
    ext[t_len + 2:t_len + 3] = jnp.where(chunk == n_chunks - 1, 0.0, ur_ref[...])
    uc = cb_ref[...].reshape(1, 1, c)
    for j in range(4):
        uc = uc + cw_ref[j:j + 1, :].reshape(1, 1, c) * ext[j:j + t_len]

    u2 = uc.reshape(t_len * bn, c)
    ub = u2.astype(BF16)
    half = c // 2

    def gate(w_ref, b_ref):
        z = jnp.concatenate([_dot(ub[:, :half], w_ref[0]), _dot(ub[:, half:], w_ref[1])], axis=-1)
        return jax.nn.sigmoid(z + b_ref[...])

    r = gate(wa_ref, ba_ref)
    ig = gate(wx_ref, bx_ref)
    log_a = (-LRU_C) * r * sp_ref[...]
    a = jnp.exp(log_a)
    b = jnp.sqrt(1.0 - jnp.exp(2.0 * log_a)) * (ig * u2)
    a_s[...] = a.reshape(t_len, bn, c)
    b_s[...] = b.reshape(t_len, bn, c)

    def body(k, h):
        t = (t_len - 1 - k) if reverse else k
        h = a_s[t] * h + b_s[t]
        h_s[t] = h
        return h

    carry[...] = lax.fori_loop(0, t_len, body, carry[...], unroll=8)

    if reverse:
        out_ref[...] = ((hf_ref[...] + h_s[...]) * _gelu_tanh(gate_ref[...])).astype(out_ref.dtype)


def _rglru(ug, hf, conv_w, conv_b, wa, ba, wx, bx, lam, bn, *, reverse):
    s_len = ug.shape[0]
    c = LRU_W
    ug3 = ug.reshape(s_len, bn, 2 * c)
    t_len = min(LRU_T, s_len)
    n_chunks = s_len // t_len

    def ck(i):
        return (n_chunks - 1 - i) if reverse else i

    half = c // 2
    nb = LRU_BLOCKS // 2

    def dense_halves(w):
        w = w.astype(F32).reshape(2, nb, c // LRU_BLOCKS, c // LRU_BLOCKS)
        eye = jnp.eye(nb, dtype=F32)
        return jnp.einsum('gnij,nm->gnimj', w, eye).reshape(2, half, half).astype(BF16)

    softplus_neg_lam = jax.nn.softplus(-lam.astype(F32)).reshape(1, c)
    vec = pl.BlockSpec((1, c), lambda i: (0, 0))
    wspec = pl.BlockSpec((2, half, half), lambda i: (0, 0, 0))
    blk = pl.BlockSpec((t_len, bn, c), lambda i: (ck(i), 0, 0))
    in_specs = [
        pl.BlockSpec((2, bn, c), lambda i: (jnp.maximum(ck(i) * (t_len // 2) - 1, 0), 0, 0)),
        blk,
        pl.BlockSpec((1, bn, c), lambda i: (jnp.minimum((ck(i) + 1) * t_len, s_len - 1), 0, 0)),
    ]
    args = [ug3, ug3, ug3]
    scratch = [pltpu.VMEM((t_len + 3, bn, c), F32), pltpu.VMEM((t_len, bn, c), F32),
               pltpu.VMEM((t_len, bn, c), F32)]
    if reverse:
        in_specs += [pl.BlockSpec((t_len, bn, c), lambda i: (ck(i), 0, 1)), blk]
        args += [ug3, hf]
        scratch.append(pltpu.VMEM((t_len, bn, c), F32))
    scratch.append(pltpu.VMEM((bn, c), F32))
    in_specs += [pl.BlockSpec((4, c), lambda i: (0, 0)), vec, wspec, vec, wspec, vec, vec]
    args += [conv_w.astype(F32), conv_b.astype(F32).reshape(1, c), dense_halves(wa),
             ba.astype(F32).reshape(1, c), dense_halves(wx), bx.astype(F32).reshape(1, c), softplus_neg_lam]
    return pl.pallas_call(
        functools.partial(_rglru_kernel, reverse=reverse, n_chunks=n_chunks),
        out_shape=jax.ShapeDtypeStruct((s_len, bn, c), BF16 if reverse else F32),
        grid=(n_chunks,),
        in_specs=in_specs,
        out_specs=blk,
        scratch_shapes=scratch,
        compiler_params=_cparams(("arbitrary",)),
        name="rglru_bw" if reverse else "rglru_fw",
    )(*args)


def _hg_matrices(c, reverse):
    n_levels = int(math.log2(c))
    t = np.arange(c)[:, None]
    r = np.arange(c)[None, :]
    mats, masks, upper = [], [], []
    if not reverse:
        mats.append(r <= t)
        mats.append(r > t)
    else:
        mats.append(r >= t)
        mats.append(r < t)
    for lev in range(n_levels):
        half = c >> (lev + 1)
        parent = 2 * half
        start = (np.arange(c) // parent) * parent
        mid = (start + half)[:, None]
        later = (np.arange(c) % parent >= half)[:, None]
        if not reverse:
            m = np.where(later, (r >= mid) & (r <= t), (r > t) & (r < mid))
            is_q = later
        else:
            m = np.where(later, (r >= mid) & (r < t), (r >= t) & (r < mid))
            is_q = ~later
        mats.append(m)
        same_parent = (start[:, None] == start[None, :])
        masks.append(same_parent & is_q & (~is_q).T)
        upper.append(np.broadcast_to(is_q, (c, HG_DK)))
    return (np.concatenate(mats, axis=0).astype(np.float32), np.stack(masks).astype(np.float32),
            np.stack(upper).astype(np.float32))


def _split3(x):
    hi = x.astype(BF16)
    r1 = x - hi.astype(F32)
    mid = r1.astype(BF16)
    lo = (r1 - mid.astype(F32)).astype(BF16)
    return hi, mid, lo


def _hgrn_kernel(*refs, reverse, n_steps, n_levels):
    if reverse:
        (q_ref, f_ref, v_ref, go_ref, of_ref, lb_ref, gn_ref, mat_ref, mask_ref, isq_ref, out_ref, state) = refs
    else:
        (q_ref, f_ref, v_ref, lb_ref, mat_ref, mask_ref, isq_ref, out_ref, state) = refs
    step = pl.program_id(1)
    c = HG_CHUNK
    t_len = q_ref.shape[0]
    n_chunks = t_len // c

    @pl.when(step == 0)
    def _():
        state[...] = jnp.zeros_like(state)

    def chunk_body(ci, carry):
        ch = (n_chunks - 1 - ci) if reverse else ci
        rows = pl.ds(pl.multiple_of(ch * c, c), c)
        for hd in range(HG_HEADS):
            cols = slice(hd * HG_DK, (hd + 1) * HG_DK)
            fl = f_ref[rows, cols]
            e = jnp.exp(-jnp.abs(fl))
            log_sig = jnp.minimum(fl, 0.0) - jnp.log(1.0 + e)
            la = lb_ref[0:1, cols]
            lbb = lb_ref[1:2, cols] + log_sig
            g = jnp.maximum(la, lbb) + jnp.log(1.0 + jnp.exp(-jnp.abs(la - lbb)))
            kk = lb_ref[2:3, cols] * (jnp.where(fl > 0.0, e, 1.0) / (1.0 + e))
            qv = q_ref[rows, cols]
            qq = qv * jax.nn.sigmoid(qv)
            v = v_ref[rows, cols]
            vb = v.astype(BF16)

            ps = _dot(mat_ref[...], jnp.concatenate(_split3(g), axis=-1))
            ps = jnp.minimum(ps[:, 0:HG_DK] + ps[:, HG_DK:2 * HG_DK] + ps[:, 2 * HG_DK:], 0.0)

            def psum(idx):
                return ps[idx * c:(idx + 1) * c, :]

            st = state[hd]
            g_in = psum(0)
            edge = (c - 1) if not reverse else 0
            qi = (qq * jnp.exp(g_in)).astype(BF16)
            o = _nt_dot(qi, st.astype(BF16))
            ks = (kk * jnp.exp(psum(1))).astype(BF16)
            state[hd] = st * jnp.exp(g_in[edge:edge + 1, :]) + _dot(v.T.astype(BF16), ks)

            att = jnp.zeros((c, c), F32)
            for lev in range(n_levels):
                xl = (jnp.where(isq_ref[lev] > 0.5, qq, kk) * jnp.exp(psum(2 + lev))).astype(BF16)
                att = att + jnp.where(mask_ref[lev] > 0.5, _nt_dot(xl, xl), 0.0)
            o = o + _dot(att.astype(BF16), vb) + jnp.sum(qq * kk, axis=-1, keepdims=True) * v

            if reverse:
                o = o + of_ref[rows, cols]
                o = o * lax.rsqrt(jnp.mean(o * o, axis=-1, keepdims=True) + EPS) * gn_ref[...]
                gv = go_ref[rows, cols]
                out_ref[rows, cols] = (o * (gv * jax.nn.sigmoid(gv))).astype(out_ref.dtype)
            else:
                out_ref[rows, cols] = o
        return carry

    lax.fori_loop(0, n_chunks, chunk_body, 0)


def _hgrn(hg, o_fw, lb, gnorm_g, *, reverse):
    bn, s_len, _ = hg.shape
    w = HG_HEADS * HG_DK
    t_len = min(HG_T, s_len)
    n_steps = s_len // t_len
    mats, masks, isq = _hg_matrices(HG_CHUNK, reverse)
    n_levels = masks.shape[0]
    lbf = lb.astype(F32)
    lb_tab = jnp.stack([jnp.log(lbf), jnp.log1p(-lbf), 1.0 - lbf])

    def blk(g):
        return pl.BlockSpec((None, t_len, w), lambda b, i: (b, (n_steps - 1 - i) if reverse else i, g))

    def const(shape):
        return pl.BlockSpec(shape, lambda b, i: (0,) * len(shape))

    in_specs = [blk(0), blk(2 if reverse else 1), blk(3)]
    args = [hg, hg, hg]
    if reverse:
        in_specs += [blk(4), blk(0)]
        args += [hg, o_fw]
    in_specs.append(const((3, w)))
    args.append(lb_tab)
    if reverse:
        in_specs.append(const((1, HG_DK)))
        args.append(gnorm_g.astype(F32).reshape(1, HG_DK))
    in_specs += [const(mats.shape), const(masks.shape), const(isq.shape)]
    args += [jnp.asarray(mats, BF16), jnp.asarray(masks, F32), jnp.asarray(isq, F32)]
    return pl.pallas_call(
        functools.partial(_hgrn_kernel, reverse=reverse, n_steps=n_steps, n_levels=n_levels),
        out_shape=jax.ShapeDtypeStruct((bn, s_len, w), BF16 if reverse else F32),
        grid=(bn, n_steps),
        in_specs=in_specs,
        out_specs=blk(0),
        scratch_shapes=[pltpu.VMEM((HG_HEADS, HG_DK, HG_DK), F32)],
        compiler_params=_cparams(("parallel", "arbitrary")),
        name="hgrn_bw" if reverse else "hgrn_fw",
    )(*args)


def _od_out_kernel(yc_ref, yd_ref, w_ref, x_ref, out_ref):
    half = yc_ref.shape[-1]
    out_ref[...] = x_ref[...] + _dot(yc_ref[...], w_ref[0:half, :]) + _dot(yd_ref[...], w_ref[half:, :])


def _od_out(x, yc_tm, yd, w_bf16):
    bn, s_len, d = x.shape
    tm = min(ROW_TILE, s_len)
    c = yd.shape[-1]
    x_spec = pl.BlockSpec((None, tm, d), lambda b, i: (b, i, 0))
    return pl.pallas_call(
        _od_out_kernel,
        out_shape=jax.ShapeDtypeStruct(x.shape, F32),
        grid=(bn, s_len // tm),
        in_specs=[pl.BlockSpec((tm, c), lambda b, i: (i, b)),
                  pl.BlockSpec((None, tm, c), lambda b, i: (b, i, 0)),
                  pl.BlockSpec((2 * c, d), lambda b, i: (0, 0)),
                  x_spec],
        out_specs=x_spec,
        compiler_params=_cparams(("parallel", "parallel")),
        name="od_out",
    )(yc_tm, yd, w_bf16, x)


def _even_layer(x, g, w_in, w_out, rpb):
    qkv = _ev_proj(x, g, w_in.astype(BF16))
    outs, lses = [], []
    for dil in DILATIONS:
        o, lse = _dilated_branch(qkv, dil)
        outs.append(o)
        lses.append(lse)
    yb = _neighborhood(qkv, rpb)
    return _ev_out(x, outs, lses, yb, w_out.astype(BF16))


def _odd_layer(x, g, w_in, w_out, conv_w, conv_b, wa, ba, wx, bx, lam, lb_f, lb_b, gnorm_g):
    bn, s_len, _ = x.shape
    ug, hg = _od_proj(x, g, w_in.astype(BF16))
    h_fw = _rglru(ug, None, conv_w, conv_b, wa[0], ba[0], wx[0], bx[0], lam[0], bn, reverse=False)
    yc = _rglru(ug, h_fw, conv_w, conv_b, wa[1], ba[1], wx[1], bx[1], lam[1], bn, reverse=True)
    o_fw = _hgrn(hg, None, lb_f, gnorm_g, reverse=False)
    yd = _hgrn(hg, o_fw, lb_b, gnorm_g, reverse=True)
    return _od_out(x, yc.reshape(s_len, bn * LRU_W), yd, w_out.astype(BF16))


def kernel(x, mem, norm_mix_g, norm_xa_g, norm_mem_g, norm_mlp_g, final_norm_g, ev_w_in, ev_w_out, na_rpb,
           od_w_in, od_w_out, conv_w, conv_b, lru_wa, lru_ba, lru_wx, lru_bx, lru_lambda, hgrn_lb_logits,
           hgrn_norm_g, xa_wq, xa_wkv, xa_wo, mlp_w1, mlp_w2):
    depth = norm_mix_g.shape[0]
    p_lb = jax.nn.softmax(hgrn_lb_logits.astype(F32), axis=0)
    lower_bounds = jnp.cumsum(p_lb, axis=0) - p_lb[0:1]
    for layer in range(depth):
        if layer % 2 == 0:
            e = layer // 2
            x = _even_layer(x, norm_mix_g[layer], ev_w_in[e], ev_w_out[e], na_rpb[e])
        else:
            o = layer // 2
            x = _odd_layer(x, norm_mix_g[layer], od_w_in[o], od_w_out[o], conv_w[o], conv_b[o],
                           lru_wa[o], lru_ba[o], lru_wx[o], lru_bx[o], lru_lambda[o],
                           lower_bounds[layer, 0], lower_bounds[layer, 1], hgrn_norm_g[o])
        kv = _mem_kv(mem, norm_mem_g[layer], xa_wkv[layer].astype(BF16))
        x = _xattn(x, norm_xa_g[layer], xa_wq[layer].astype(BF16), kv, xa_wo[layer].astype(BF16))
        x = _mlp(x, norm_mlp_g[layer], mlp_w1[layer].astype(BF16), mlp_w2[layer].astype(BF16),
                 g_final=final_norm_g if layer == depth - 1 else None)
    return x
```

```python
import functools
import math

import jax
import jax.numpy as jnp
import numpy as np
from jax import lax
from jax.experimental import pallas as pl
from jax.experimental.pallas import tpu as pltpu

F32 = jnp.float32
BF16 = jnp.bfloat16

D_MODEL = 1024
HEAD_DIM = 64
ROT_DIM = 16
ROPE_THETA = 500000.0
N_HEADS_A = 8
N_HEADS_B = 8
GROUP_W = N_HEADS_A * HEAD_DIM
DILATIONS = (1, 4, 16)
WIN_HALF = 64
GRID_W = 64
NA_ROWS = 8
NA_COLS = 16
LRU_W = 512
LRU_BLOCKS = 8
LRU_C = 8.0
HG_HEADS = 4
HG_DK = 128
XA_HEADS = 4
XA_DH = 256
D_FF = 4096
EPS = 1e-6

LANES = 128
V7X_VMEM_LIMIT_BYTES = 56 * 1024 * 1024

ROW_TILE = 512
ATT_Q_BLOCK = 512
ATT_SUB = 128
NA_ROW_GROUP = 8
LRU_T = 128
HG_CHUNK = 128
HG_T = 512


def _cparams(sem):
    return pltpu.CompilerParams(dimension_semantics=sem, vmem_limit_bytes=V7X_VMEM_LIMIT_BYTES)


def _rms(x, g):
    return x * lax.rsqrt(jnp.mean(x * x, axis=-1, keepdims=True) + EPS) * g


def _nt_dot(a, b):
    return lax.dot_general(a, b, (((1,), (1,)), ((), ())), preferred_element_type=F32)


def _dot(a, b):
    return jnp.dot(a, b, preferred_element_type=F32)


def _head_selectors():
    lane = lax.broadcasted_iota(jnp.int32, (1, LANES), 1)
    low = (lane < HEAD_DIM).astype(F32)
    return low.astype(BF16), (1.0 - low).astype(BF16)


def _ev_proj_kernel(x_ref, g_ref, w_ref, rc_ref, rs1_ref, rs2_ref, o_ref, o4_ref, o16_ref, slab):
    h = _rms(x_ref[...], g_ref[...]).astype(BF16)
    scale = HEAD_DIM ** -0.5
    tm = x_ref.shape[0]
    n_slabs = GROUP_W // LANES
    for c in range(6):
        acc = _dot(h, w_ref[:, c * GROUP_W:(c + 1) * GROUP_W])
        if c in (0, 1):
            rc, rs1, rs2 = rc_ref[...], rs1_ref[...], rs2_ref[...]
            parts = []
            for j in range(n_slabs):
                t = acc[:, j * LANES:(j + 1) * LANES]
                t = t * rc + pltpu.roll(t, LANES - ROT_DIM // 2, 1) * rs1 + pltpu.roll(t, ROT_DIM // 2, 1) * rs2
                parts.append(t)
            acc = jnp.concatenate(parts, axis=-1)
        if c in (0, 3):
            acc = acc * scale
        o_ref[:, c * GROUP_W:(c + 1) * GROUP_W] = acc.astype(BF16)
        if c < 3:
            for j in range(n_slabs):
                slab[j] = acc[:, j * LANES:(j + 1) * LANES]
            for dil, dst in ((DILATIONS[1], o4_ref), (DILATIONS[2], o16_ref)):
                for r in range(dil):
                    for j in range(n_slabs):
                        col = r * 3 * GROUP_W + c * GROUP_W + j * LANES
                        dst[:, col:col + LANES] = slab[j, pl.ds(r, tm // dil, stride=dil), :].astype(BF16)


def _rope_tables(s_len):
    half = ROT_DIM // 2
    inv = jnp.asarray(ROPE_THETA ** (-np.arange(half) * 2.0 / ROT_DIM), F32)
    ang = jnp.arange(s_len, dtype=F32)[:, None] * inv[None, :]
    cos, sin = jnp.cos(ang), jnp.sin(ang)
    ones = jnp.ones((s_len, HEAD_DIM - ROT_DIM), F32)
    zeros = jnp.zeros((s_len, HEAD_DIM - ROT_DIM), F32)
    zh = jnp.zeros((s_len, half), F32)
    rc = jnp.concatenate([cos, cos, ones], axis=-1)
    rs1 = jnp.concatenate([-sin, zh, zeros], axis=-1)
    rs2 = jnp.concatenate([zh, sin, zeros], axis=-1)
    rep = LANES // HEAD_DIM
    return jnp.tile(rc, (1, rep)), jnp.tile(rs1, (1, rep)), jnp.tile(rs2, (1, rep))


def _ev_proj(x, g, w_bf16):
    bn, s_len, d = x.shape
    tm = min(ROW_TILE, s_len)
    n_out = w_bf16.shape[1]
    rc, rs1, rs2 = _rope_tables(s_len)
    tbl_spec = pl.BlockSpec((tm, LANES), lambda b, i: (i, 0))
    wa = 3 * GROUP_W
    d4, d16 = DILATIONS[1], DILATIONS[2]
    return pl.pallas_call(
        _ev_proj_kernel,
        out_shape=(jax.ShapeDtypeStruct((bn, s_len, n_out), BF16),
                   jax.ShapeDtypeStruct((bn, s_len // d4, d4 * wa), BF16),
                   jax.ShapeDtypeStruct((bn, s_len // d16, d16 * wa), BF16)),
        grid=(bn, s_len // tm),
        in_specs=[
            pl.BlockSpec((None, tm, d), lambda b, i: (b, i, 0)),
            pl.BlockSpec((1, d), lambda b, i: (0, 0)),
            pl.BlockSpec((d, n_out), lambda b, i: (0, 0)),
            tbl_spec, tbl_spec, tbl_spec,
        ],
        out_specs=(pl.BlockSpec((None, tm, n_out), lambda b, i: (b, i, 0)),
                   pl.BlockSpec((None, tm // d4, d4 * wa), lambda b, i: (b, i, 0)),
                   pl.BlockSpec((None, tm // d16, d16 * wa), lambda b, i: (b, i, 0))),
        scratch_shapes=[pltpu.VMEM((GROUP_W // LANES, tm, LANES), F32)],
        compiler_params=_cparams(("parallel", "parallel")),
        name="ev_proj",
    )(x, g.reshape(1, d), w_bf16, rc, rs1, rs2)


def _dilated_kernel(q_ref, kl_ref, kc_ref, kr_ref, vl_ref, vc_ref, vr_ref, o_ref, lse_ref,
                    kext, vext, *, lq, l_total):
    i = pl.program_id(2)
    kext[0:WIN_HALF, :] = kl_ref[...]
    kext[WIN_HALF:WIN_HALF + lq, :] = kc_ref[...]
    kext[WIN_HALF + lq:, :] = kr_ref[...]
    vext[0:WIN_HALF, :] = vl_ref[...]
    vext[WIN_HALF:WIN_HALF + lq, :] = vc_ref[...]
    vext[WIN_HALF + lq:, :] = vr_ref[...]

    wk = ATT_SUB + 2 * WIN_HALF
    qi = lax.broadcasted_iota(jnp.int32, (ATT_SUB, wk), 0)
    ci = lax.broadcasted_iota(jnp.int32, (ATT_SUB, wk), 1)
    band = (ci - qi >= 0) & (ci - qi <= 2 * WIN_HALF)
    lane = lax.broadcasted_iota(jnp.int32, (ATT_SUB, LANES), 1)
    low_half = lane < HEAD_DIM
    head_sel = _head_selectors()

    for j in range(lq // ATT_SUB):
        base = i * lq + j * ATT_SUB - WIN_HALF
        valid = band & (ci + base >= 0) & (ci + base < l_total)
        bias = jnp.where(valid, 0.0, -jnp.inf).astype(F32)
        rows = slice(j * ATT_SUB, (j + 1) * ATT_SUB)
        wrows = slice(j * ATT_SUB, j * ATT_SUB + wk)
        lse_tile = jnp.zeros((ATT_SUB, LANES), F32)
        for hp in range(GROUP_W // LANES):
            cols = slice(hp * LANES, (hp + 1) * LANES)
            qp = q_ref[rows, cols]
            kp = kext[wrows, cols]
            vp = vext[wrows, cols]
            outs = []
            for which in range(2):
                s = _nt_dot(qp * head_sel[which], kp) + bias
                m = jnp.max(s, axis=-1, keepdims=True)
                p = jnp.exp(s - m)
                den = jnp.sum(p, axis=-1, keepdims=True)
                o = _dot(p.astype(BF16), vp) * (1.0 / den)
                outs.append(o)
                lse = m + jnp.log(den)
                lse_tile = jnp.where(lane == 2 * hp + which, lse, lse_tile)
            o_ref[rows, cols] = jnp.where(low_half, outs[0], outs[1]).astype(BF16)
        lse_ref[rows, :] = lse_tile


def _dilated_branch(view, dil):
    bn, l_total, width = view.shape
    n_groups = width // (dil * GROUP_W)
    lq = min(ATT_Q_BLOCK, l_total)
    nblk = l_total // lq
    per = lq // WIN_HALF
    n_halo = l_total // WIN_HALF

    def cur(g):
        return pl.BlockSpec((None, lq, GROUP_W), lambda b, r, i: (b, i, r * n_groups + g))

    def left(g):
        return pl.BlockSpec((None, WIN_HALF, GROUP_W),
                            lambda b, r, i: (b, jnp.maximum(i * per - 1, 0), r * n_groups + g))

    def right(g):
        return pl.BlockSpec((None, WIN_HALF, GROUP_W),
                            lambda b, r, i: (b, jnp.minimum((i + 1) * per, n_halo - 1), r * n_groups + g))

    o, lse = pl.pallas_call(
        functools.partial(_dilated_kernel, lq=lq, l_total=l_total),
        out_shape=(jax.ShapeDtypeStruct((bn, l_total, dil * GROUP_W), BF16),
                   jax.ShapeDtypeStruct((bn, l_total, dil * LANES), F32)),
        grid=(bn, dil, nblk),
        in_specs=[cur(0), left(1), cur(1), right(1), left(2), cur(2), right(2)],
        out_specs=(pl.BlockSpec((None, lq, GROUP_W), lambda b, r, i: (b, i, r)),
                   pl.BlockSpec((None, lq, LANES), lambda b, r, i: (b, i, r))),
        scratch_shapes=[pltpu.VMEM((lq + 2 * WIN_HALF, GROUP_W), BF16),
                        pltpu.VMEM((lq + 2 * WIN_HALF, GROUP_W), BF16)],
        compiler_params=_cparams(("parallel", "parallel", "parallel")),
        name=f"dilated_attn_d{dil}",
    )(view, view, view, view, view, view, view)
    return o, lse


NA_QT = NA_ROW_GROUP * GRID_W
NA_KT = 2 * NA_QT
NA_PIECE = NA_KT // 4


def _na_tables(rpb, rows):
    rpb = rpb.astype(F32)
    nh = rpb.shape[0]
    ext = GRID_W - NA_COLS
    padded = jnp.concatenate([jnp.repeat(rpb[..., :1], ext, axis=-1), rpb,
                              jnp.repeat(rpb[..., -1:], ext, axis=-1)], axis=-1)
    t1 = jnp.stack([padded[..., GRID_W - 1 - qc:2 * GRID_W - 1 - qc] for qc in range(GRID_W)], axis=2)
    qc = np.arange(GRID_W)[:, None]
    kc = np.arange(GRID_W)[None, :]
    c0 = np.clip(qc - NA_COLS // 2, 0, GRID_W - NA_COLS)
    t1 = jnp.where((kc >= c0) & (kc < c0 + NA_COLS), t1, -jnp.inf)
    neg = jnp.full((nh, GRID_W, GRID_W), -jnp.inf, F32)
    variants = []
    for i0, w0 in ((0, 0), (NA_ROW_GROUP, NA_ROW_GROUP - 4), (rows - NA_ROW_GROUP, rows - 2 * NA_ROW_GROUP)):
        q_rows = []
        for a in range(NA_ROW_GROUP):
            qrow = i0 + a
            r0 = min(max(qrow - NA_ROWS // 2, 0), rows - NA_ROWS)
            tiles = []
            for jj in range(2 * NA_ROW_GROUP):
                krow = w0 + jj
                tiles.append(t1[:, krow - qrow + NA_ROWS - 1] if r0 <= krow < r0 + NA_ROWS else neg)
            q_rows.append(jnp.concatenate(tiles, axis=-1))
        variants.append(jnp.concatenate(q_rows, axis=1))
    return jnp.stack(variants)


def _na_kernel(q_ref, k0, k1, k2, k3, v0, v1, v2, v3, tb_ref, o_ref):
    ks = (k0, k1, k2, k3)
    vs = (v0, v1, v2, v3)
    lane = lax.broadcasted_iota(jnp.int32, (ATT_SUB, LANES), 1)
    low_half = lane < HEAD_DIM
    head_sel = _head_selectors()
    for hp in range(GROUP_W // LANES):
        cols = slice(hp * LANES, (hp + 1) * LANES)
        for qb in range(NA_QT // ATT_SUB):
            rows = slice(qb * ATT_SUB, (qb + 1) * ATT_SUB)
            qp = q_ref[rows, cols]
            outs = []
            for which in range(2):
                qm = qp * head_sel[which]
                s = jnp.concatenate([_nt_dot(qm, kr[:, cols]) for kr in ks], axis=-1)
                s = s + tb_ref[2 * hp + which, rows, :]
                m = jnp.max(s, axis=-1, keepdims=True)
                p = jnp.exp(s - m)
                den = jnp.sum(p, axis=-1, keepdims=True)
                pb = p.astype(BF16)
                o = _dot(pb[:, 0:NA_PIECE], vs[0][:, cols])
                for t in range(1, 4):
                    o = o + _dot(pb[:, t * NA_PIECE:(t + 1) * NA_PIECE], vs[t][:, cols])
                outs.append(o * (1.0 / den))
            o_ref[rows, cols] = jnp.where(low_half, outs[0], outs[1]).astype(BF16)


def _neighborhood(qkv, rpb):
    bn, s_len, _ = qkv.shape
    rows = s_len // GRID_W
    n_groups = rows // NA_ROW_GROUP
    n_pieces = s_len // NA_PIECE
    tables = _na_tables(rpb, rows)

    def piece(g, t):
        return pl.BlockSpec((None, NA_PIECE, GROUP_W),
                            lambda b, i: (b, jnp.clip(2 * i - 1, 0, n_pieces - 4) + t, g))

    def variant(b, i):
        return (jnp.where(i == 0, 0, jnp.where(i == n_groups - 1, 2, 1)), 0, 0, 0)

    return pl.pallas_call(
        _na_kernel,
        out_shape=jax.ShapeDtypeStruct((bn, s_len, GROUP_W), BF16),
        grid=(bn, n_groups),
        in_specs=[pl.BlockSpec((None, NA_QT, GROUP_W), lambda b, i: (b, i, 3))]
        + [piece(4, t) for t in range(4)] + [piece(5, t) for t in range(4)]
        + [pl.BlockSpec((None, N_HEADS_B, NA_QT, NA_KT), variant)],
        out_specs=pl.BlockSpec((None, NA_QT, GROUP_W), lambda b, i: (b, i, 0)),
        compiler_params=_cparams(("parallel", "arbitrary")),
        name="neighborhood_attn",
    )(qkv, *([qkv] * 8), tables)


def _ev_out_kernel(o1, o2v, o3v, l1, l2v, l3v, yb_ref, ex_ref, w_ref, x_ref, out_ref, so2, so3, sl2, sl3):
    tm = x_ref.shape[0]
    n_slabs = GROUP_W // LANES
    for dil, ov, lv, so, sl in ((DILATIONS[1], o2v, l2v, so2, sl2), (DILATIONS[2], o3v, l3v, so3, sl3)):
        for r in range(dil):
            dst = pl.ds(r, tm // dil, stride=dil)
            for j in range(n_slabs):
                col = r * GROUP_W + j * LANES
                so[j, dst, :] = ov[:, col:col + LANES].astype(F32)
            sl[0, dst, :] = lv[:, r * LANES:(r + 1) * LANES]
    o2 = jnp.concatenate([so2[j] for j in range(n_slabs)], axis=-1)
    o3 = jnp.concatenate([so3[j] for j in range(n_slabs)], axis=-1)
    la, lb, lc = l1[...], sl2[0], sl3[0]
    m = jnp.maximum(jnp.maximum(la, lb), lc)
    ea, eb, ec = jnp.exp(la - m), jnp.exp(lb - m), jnp.exp(lc - m)
    inv = 1.0 / (ea + eb + ec)
    ex = ex_ref[...]

    def expand(w):
        hi = w.astype(BF16)
        lo = (w - hi.astype(F32)).astype(BF16)
        return _dot(hi, ex) + _dot(lo, ex)

    ya = expand(ea * inv) * o1[...].astype(F32) + expand(eb * inv) * o2 + expand(ec * inv) * o3
    y = _dot(ya.astype(BF16), w_ref[0:GROUP_W, :]) + _dot(yb_ref[...], w_ref[GROUP_W:, :])
    out_ref[...] = x_ref[...] + y


def _ev_out(x, outs, lses, yb, w_bf16):
    bn, s_len, d = x.shape
    tm = min(ROW_TILE, s_len)
    ex = np.zeros((LANES, GROUP_W), np.float32)
    for h in range(N_HEADS_A):
        ex[h, h * HEAD_DIM:(h + 1) * HEAD_DIM] = 1.0
    def o_spec(dil):
        return pl.BlockSpec((None, tm // dil, dil * GROUP_W), lambda b, i: (b, i, 0))

    def l_spec(dil):
        return pl.BlockSpec((None, tm // dil, dil * LANES), lambda b, i: (b, i, 0))

    x_spec = pl.BlockSpec((None, tm, d), lambda b, i: (b, i, 0))
    n_slabs = GROUP_W // LANES
    return pl.pallas_call(
        _ev_out_kernel,
        out_shape=jax.ShapeDtypeStruct(x.shape, F32),
        grid=(bn, s_len // tm),
        in_specs=[o_spec(d_) for d_ in DILATIONS] + [l_spec(d_) for d_ in DILATIONS]
        + [o_spec(1),
           pl.BlockSpec((LANES, GROUP_W), lambda b, i: (0, 0)),
           pl.BlockSpec((2 * GROUP_W, d), lambda b, i: (0, 0)),
           x_spec],
        out_specs=x_spec,
        scratch_shapes=[pltpu.VMEM((n_slabs, tm, LANES), F32), pltpu.VMEM((n_slabs, tm, LANES), F32),
                        pltpu.VMEM((1, tm, LANES), F32), pltpu.VMEM((1, tm, LANES), F32)],
        compiler_params=_cparams(("parallel", "parallel")),
        name="ev_out",
    )(*outs, *lses, yb, jnp.asarray(ex, BF16), w_bf16, x)


def _norm_matmul_kernel(x_ref, g_ref, w_ref, o_ref):
    o_ref[...] = _dot(_rms(x_ref[...], g_ref[...]).astype(BF16), w_ref[...]).astype(o_ref.dtype)


def _mem_kv(mem, g, w_bf16):
    bn, m_len, d = mem.shape
    n_out = w_bf16.shape[1]
    return pl.pallas_call(
        _norm_matmul_kernel,
        out_shape=jax.ShapeDtypeStruct((bn, m_len, n_out), BF16),
        grid=(bn,),
        in_specs=[pl.BlockSpec((None, m_len, d), lambda b: (b, 0, 0)),
                  pl.BlockSpec((1, d), lambda b: (0, 0)),
                  pl.BlockSpec((d, n_out), lambda b: (0, 0))],
        out_specs=pl.BlockSpec((None, m_len, n_out), lambda b: (b, 0, 0)),
        compiler_params=_cparams(("parallel",)),
        name="mem_kv",
    )(mem, g.reshape(1, d), w_bf16)


def _xattn_kernel(x_ref, g_ref, wq_ref, kv_ref, wo_ref, out_ref):
    x = x_ref[...]
    q = (_dot(_rms(x, g_ref[...]).astype(BF16), wq_ref[...]) * (XA_DH ** -0.5)).astype(BF16)
    heads = []
    for hd in range(XA_HEADS):
        cols = slice(hd * XA_DH, (hd + 1) * XA_DH)
        s = _nt_dot(q[:, cols], kv_ref[:, cols])
        m = jnp.max(s, axis=-1, keepdims=True)
        p = jnp.exp(s - m)
        den = jnp.sum(p, axis=-1, keepdims=True)
        vcols = slice(D_MODEL + hd * XA_DH, D_MODEL + (hd + 1) * XA_DH)
        heads.append((_dot(p.astype(BF16), kv_ref[:, vcols]) * (1.0 / den)).astype(BF16))
    o = jnp.concatenate(heads, axis=-1)
    out_ref[...] = x + _dot(o, wo_ref[...])


def _xattn(x, g, wq, kv, wo):
    bn, s_len, d = x.shape
    tm = min(ROW_TILE, s_len)
    m_len = kv.shape[1]
    x_spec = pl.BlockSpec((None, tm, d), lambda b, i: (b, i, 0))
    return pl.pallas_call(
        _xattn_kernel,
        out_shape=jax.ShapeDtypeStruct(x.shape, F32),
        grid=(bn, s_len // tm),
        in_specs=[x_spec,
                  pl.BlockSpec((1, d), lambda b, i: (0, 0)),
                  pl.BlockSpec((d, d), lambda b, i: (0, 0)),
                  pl.BlockSpec((None, m_len, 2 * d), lambda b, i: (b, 0, 0)),
                  pl.BlockSpec((d, d), lambda b, i: (0, 0))],
        out_specs=x_spec,
        compiler_params=_cparams(("parallel", "parallel")),
        name="mem_xattn",
    )(x, g.reshape(1, d), wq, kv, wo)


def _mlp_kernel(x_ref, g_ref, w1_ref, w2_ref, gf_ref, out_ref, *, final_norm):
    x = x_ref[...]
    h = _rms(x, g_ref[...]).astype(BF16)
    acc = x
    for c in range(D_FF // D_MODEL):
        cols = slice(c * D_MODEL, (c + 1) * D_MODEL)
        a = jnp.maximum(_dot(h, w1_ref[:, cols]), 0.0)
        acc = acc + _dot((a * a).astype(BF16), w2_ref[cols, :])
    out_ref[...] = _rms(acc, gf_ref[...]) if final_norm else acc


def _mlp(x, g, w1, w2, g_final=None):
    bn, s_len, d = x.shape
    tm = min(ROW_TILE, s_len)
    final_norm = g_final is not None
    gf = (g_final if final_norm else g).reshape(1, d)
    x_spec = pl.BlockSpec((None, tm, d), lambda b, i: (b, i, 0))
    vec = pl.BlockSpec((1, d), lambda b, i: (0, 0))
    return pl.pallas_call(
        functools.partial(_mlp_kernel, final_norm=final_norm),
        out_shape=jax.ShapeDtypeStruct(x.shape, F32),
        grid=(bn, s_len // tm),
        in_specs=[x_spec, vec,
                  pl.BlockSpec((d, D_FF), lambda b, i: (0, 0)),
                  pl.BlockSpec((D_FF, d), lambda b, i: (0, 0)),
                  vec],
        out_specs=x_spec,
        compiler_params=_cparams(("parallel", "parallel")),
        name="mlp_final" if final_norm else "mlp",
    )(x, g.reshape(1, d), w1, w2, gf)


def _od_proj_kernel(x_ref, g_ref, w_ref, ug_ref, hg_ref):
    h = _rms(x_ref[...], g_ref[...]).astype(BF16)
    n_ug = 2 * LRU_W
    for c in range(n_ug // GROUP_W):
        cols = slice(c * GROUP_W, (c + 1) * GROUP_W)
        ug_ref[:, cols] = _dot(h, w_ref[:, cols])
    for c in range(hg_ref.shape[-1] // GROUP_W):
        hg_ref[:, c * GROUP_W:(c + 1) * GROUP_W] = _dot(h, w_ref[:, n_ug + c * GROUP_W:n_ug + (c + 1) * GROUP_W])


def _od_proj(x, g, w_bf16):
    bn, s_len, d = x.shape
    tm = min(ROW_TILE, s_len)
    n_ug = 2 * LRU_W
    n_hg = w_bf16.shape[1] - n_ug
    return pl.pallas_call(
        _od_proj_kernel,
        out_shape=(jax.ShapeDtypeStruct((s_len, bn * n_ug), F32),
                   jax.ShapeDtypeStruct((bn, s_len, n_hg), F32)),
        grid=(bn, s_len // tm),
        in_specs=[pl.BlockSpec((None, tm, d), lambda b, i: (b, i, 0)),
                  pl.BlockSpec((1, d), lambda b, i: (0, 0)),
                  pl.BlockSpec((d, n_ug + n_hg), lambda b, i: (0, 0))],
        out_specs=(pl.BlockSpec((tm, n_ug), lambda b, i: (i, b)),
                   pl.BlockSpec((None, tm, n_hg), lambda b, i: (b, i, 0))),
        compiler_params=_cparams(("parallel", "parallel")),
        name="od_proj",
    )(x, g.reshape(1, d), w_bf16)


def _gelu_tanh(x):
    return 0.5 * x * (1.0 + jnp.tanh(math.sqrt(2.0 / math.pi) * (x + 0.044715 * (x * x * x))))


def _rglru_kernel(*refs, reverse, n_chunks):
    if reverse:
        (ul_ref, u_ref, ur_ref, gate_ref, hf_ref, cw_ref, cb_ref, wa_ref, ba_ref, wx_ref, bx_ref, sp_ref,
         out_ref, ext, a_s, b_s, h_s, carry) = refs
    else:
        (ul_ref, u_ref, ur_ref, cw_ref, cb_ref, wa_ref, ba_ref, wx_ref, bx_ref, sp_ref,
         out_ref, ext, a_s, b_s, carry) = refs
        h_s = out_ref
    step = pl.program_id(0)
    chunk = (n_chunks - 1 - step) if reverse else step
    t_len, bn, c = u_ref.shape

    @pl.when(step == 0)
    def _():
        carry[...] = jnp.zeros_like(carry)

    ext[0:2] = jnp.where(chunk == 0, 0.0, ul_ref[...])
    ext[2:t_len + 2] = u_ref[...]
    ext[t_len + 2:t_len + 3] = jnp.where(chunk == n_chunks - 1, 0.0, ur_ref[...])
    uc = cb_ref[...].reshape(1, 1, c)
    for j in range(4):
        uc = uc + cw_ref[j:j + 1, :].reshape(1, 1, c) * ext[j:j + t_len]

    u2 = uc.reshape(t_len * bn, c)
    ub = u2.astype(BF16)
    half = c // 2

    def gate(w_ref, b_ref):
        z = jnp.concatenate([_dot(ub[:, :half], w_ref[0]), _dot(ub[:, half:], w_ref[1])], axis=-1)
        return jax.nn.sigmoid(z + b_ref[...])

    r = gate(wa_ref, ba_ref)
    ig = gate(wx_ref, bx_ref)
    log_a = (-LRU_C) * r * sp_ref[...]
    a = jnp.exp(log_a)
    b = jnp.sqrt(1.0 - jnp.exp(2.0 * log_a)) * (ig * u2)
    a_s[...] = a.reshape(t_len, bn, c)
    b_s[...] = b.reshape(t_len, bn, c)

    def body(k, h):
        t = (t_len - 1 - k) if reverse else k
        h = a_s[t] * h + b_s[t]
        h_s[t] = h
        return h

    carry[...] = lax.fori_loop(0, t_len, body, carry[...], unroll=8)

    if reverse:
        out_ref[...] = ((hf_ref[...] + h_s[...]) * _gelu_tanh(gate_ref[...])).astype(out_ref.dtype)


def _rglru(ug, hf, conv_w, conv_b, wa, ba, wx, bx, lam, bn, *, reverse):
    s_len = ug.shape[0]
    c = LRU_W
    ug3 = ug.reshape(s_len, bn, 2 * c)
    t_len = min(LRU_T, s_len)
    n_chunks = s_len // t_len

    def ck(i):
        return (n_chunks - 1 - i) if reverse else i

    half = c // 2
    nb = LRU_BLOCKS // 2

    def dense_halves(w):
        w = w.astype(F32).reshape(2, nb, c // LRU_BLOCKS, c // LRU_BLOCKS)
        eye = jnp.eye(nb, dtype=F32)
        return jnp.einsum('gnij,nm->gnimj', w, eye).reshape(2, half, half).astype(BF16)

    softplus_neg_lam = jax.nn.softplus(-lam.astype(F32)).reshape(1, c)
    vec = pl.BlockSpec((1, c), lambda i: (0, 0))
    wspec = pl.BlockSpec((2, half, half), lambda i: (0, 0, 0))
    blk = pl.BlockSpec((t_len, bn, c), lambda i: (ck(i), 0, 0))
    in_specs = [
        pl.BlockSpec((2, bn, c), lambda i: (jnp.maximum(ck(i) * (t_len // 2) - 1, 0), 0, 0)),
        blk,
        pl.BlockSpec((1, bn, c), lambda i: (jnp.minimum((ck(i) + 1) * t_len, s_len - 1), 0, 0)),
    ]
    args = [ug3, ug3, ug3]
    scratch = [pltpu.VMEM((t_len + 3, bn, c), F32), pltpu.VMEM((t_len, bn, c), F32),
               pltpu.VMEM((t_len, bn, c), F32)]
    if reverse:
        in_specs += [pl.BlockSpec((t_len, bn, c), lambda i: (ck(i), 0, 1)), blk]
        args += [ug3, hf]
        scratch.append(pltpu.VMEM((t_len, bn, c), F32))
    scratch.append(pltpu.VMEM((bn, c), F32))
    in_specs += [pl.BlockSpec((4, c), lambda i: (0, 0)), vec, wspec, vec, wspec, vec, vec]
    args += [conv_w.astype(F32), conv_b.astype(F32).reshape(1, c), dense_halves(wa),
             ba.astype(F32).reshape(1, c), dense_halves(wx), bx.astype(F32).reshape(1, c), softplus_neg_lam]
    return pl.pallas_call(
        functools.partial(_rglru_kernel, reverse=reverse, n_chunks=n_chunks),
        out_shape=jax.ShapeDtypeStruct((s_len, bn, c), BF16 if reverse else F32),
        grid=(n_chunks,),
        in_specs=in_specs,
        out_specs=blk,
        scratch_shapes=scratch,
        compiler_params=_cparams(("arbitrary",)),
        name="rglru_bw" if reverse else "rglru_fw",
    )(*args)


def _hg_matrices(c, reverse):
    n_levels = int(math.log2(c))
    t = np.arange(c)[:, None]
    r = np.arange(c)[None, :]
    mats, masks, upper = [], [], []
    if not reverse:
        mats.append(r <= t)
        mats.append(r > t)
    else:
        mats.append(r >= t)
        mats.append(r < t)
    for lev in range(n_levels):
        half = c >> (lev + 1)
        parent = 2 * half
        start = (np.arange(c) // parent) * parent
        mid = (start + half)[:, None]
        later = (np.arange(c) % parent >= half)[:, None]
        if not reverse:
            m = np.where(later, (r >= mid) & (r <= t), (r > t) & (r < mid))
            is_q = later
        else:
            m = np.where(later, (r >= mid) & (r < t), (r >= t) & (r < mid))
            is_q = ~later
        mats.append(m)
        same_parent = (start[:, None] == start[None, :])
        masks.append(same_parent & is_q & (~is_q).T)
        upper.append(np.broadcast_to(is_q, (c, HG_DK)))
    return (np.concatenate(mats, axis=0).astype(np.float32), np.stack(masks).astype(np.float32),
            np.stack(upper).astype(np.float32))


def _split3(x):
    hi = x.astype(BF16)
    r1 = x - hi.astype(F32)
    mid = r1.astype(BF16)
    lo = (r1 - mid.astype(F32)).astype(BF16)
    return hi, mid, lo


def _hgrn_kernel(*refs, reverse, n_steps, n_levels):
    if reverse:
        (q_ref, f_ref, v_ref, go_ref, of_ref, lb_ref, gn_ref, mat_ref, mask_ref, isq_ref, out_ref, state) = refs
    else:
        (q_ref, f_ref, v_ref, lb_ref, mat_ref, mask_ref, isq_ref, out_ref, state) = refs
    step = pl.program_id(1)
    c = HG_CHUNK
    t_len = q_ref.shape[0]
    n_chunks = t_len // c

    @pl.when(step == 0)
    def _():
        state[...] = jnp.zeros_like(state)

    def chunk_body(ci, carry):
        ch = (n_chunks - 1 - ci) if reverse else ci
        rows = pl.ds(pl.multiple_of(ch * c, c), c)
        for hd in range(HG_HEADS):
            cols = slice(hd * HG_DK, (hd + 1) * HG_DK)
            fl = f_ref[rows, cols]
            e = jnp.exp(-jnp.abs(fl))
            log_sig = jnp.minimum(fl, 0.0) - jnp.log(1.0 + e)
            la = lb_ref[0:1, cols]
            lbb = lb_ref[1:2, cols] + log_sig
            g = jnp.maximum(la, lbb) + jnp.log(1.0 + jnp.exp(-jnp.abs(la - lbb)))
            kk = lb_ref[2:3, cols] * (jnp.where(fl > 0.0, e, 1.0) / (1.0 + e))
            qv = q_ref[rows, cols]
            qq = qv * jax.nn.sigmoid(qv)
            v = v_ref[rows, cols]
            vb = v.astype(BF16)

            ps = _dot(mat_ref[...], jnp.concatenate(_split3(g), axis=-1))
            ps = jnp.minimum(ps[:, 0:HG_DK] + ps[:, HG_DK:2 * HG_DK] + ps[:, 2 * HG_DK:], 0.0)

            def psum(idx):
                return ps[idx * c:(idx + 1) * c, :]

            st = state[hd]
            g_in = psum(0)
            edge = (c - 1) if not reverse else 0
            qi = (qq * jnp.exp(g_in)).astype(BF16)
            o = _nt_dot(qi, st.astype(BF16))
            ks = (kk * jnp.exp(psum(1))).astype(BF16)
            state[hd] = st * jnp.exp(g_in[edge:edge + 1, :]) + _dot(v.T.astype(BF16), ks)

            att = jnp.zeros((c, c), F32)
            for lev in range(n_levels):
                xl = (jnp.where(isq_ref[lev] > 0.5, qq, kk) * jnp.exp(psum(2 + lev))).astype(BF16)
                att = att + jnp.where(mask_ref[lev] > 0.5, _nt_dot(xl, xl), 0.0)
            o = o + _dot(att.astype(BF16), vb) + jnp.sum(qq * kk, axis=-1, keepdims=True) * v

            if reverse:
                o = o + of_ref[rows, cols]
                o = o * lax.rsqrt(jnp.mean(o * o, axis=-1, keepdims=True) + EPS) * gn_ref[...]
                gv = go_ref[rows, cols]
                out_ref[rows, cols] = (o * (gv * jax.nn.sigmoid(gv))).astype(out_ref.dtype)
            else:
                out_ref[rows, cols] = o
        return carry

    lax.fori_loop(0, n_chunks, chunk_body, 0)


def _hgrn(hg, o_fw, lb, gnorm_g, *, reverse):
    bn, s_len, _ = hg.shape
    w = HG_HEADS * HG_DK
    t_len = min(HG_T, s_len)
    n_steps = s_len // t_len
    mats, masks, isq = _hg_matrices(HG_CHUNK, reverse)
    n_levels = masks.shape[0]
    lbf = lb.astype(F32)
    lb_tab = jnp.stack([jnp.log(lbf), jnp.log1p(-lbf), 1.0 - lbf])

    def blk(g):
        return pl.BlockSpec((None, t_len, w), lambda b, i: (b, (n_steps - 1 - i) if reverse else i, g))

    def const(shape):
        return pl.BlockSpec(shape, lambda b, i: (0,) * len(shape))

    in_specs = [blk(0), blk(2 if reverse else 1), blk(3)]
    args = [hg, hg, hg]
    if reverse:
        in_specs += [blk(4), blk(0)]
        args += [hg, o_fw]
    in_specs.append(const((3, w)))
    args.append(lb_tab)
    if reverse:
        in_specs.append(const((1, HG_DK)))
        args.append(gnorm_g.astype(F32).reshape(1, HG_DK))
    in_specs += [const(mats.shape), const(masks.shape), const(isq.shape)]
    args += [jnp.asarray(mats, BF16), jnp.asarray(masks, F32), jnp.asarray(isq, F32)]
    return pl.pallas_call(
        functools.partial(_hgrn_kernel, reverse=reverse, n_steps=n_steps, n_levels=n_levels),
        out_shape=jax.ShapeDtypeStruct((bn, s_len, w), BF16 if reverse else F32),
        grid=(bn, n_steps),
        in_specs=in_specs,
        out_specs=blk(0),
        scratch_shapes=[pltpu.VMEM((HG_HEADS, HG_DK, HG_DK), F32)],
        compiler_params=_cparams(("parallel", "arbitrary")),
        name="hgrn_bw" if reverse else "hgrn_fw",
    )(*args)


def _od_out_kernel(yc_ref, yd_ref, w_ref, x_ref, out_ref):
    half = yc_ref.shape[-1]
    out_ref[...] = x_ref[...] + _dot(yc_ref[...], w_ref[0:half, :]) + _dot(yd_ref[...], w_ref[half:, :])


def _od_out(x, yc_tm, yd, w_bf16):
    bn, s_len, d = x.shape
    tm = min(ROW_TILE, s_len)
    c = yd.shape[-1]
    x_spec = pl.BlockSpec((None, tm, d), lambda b, i: (b, i, 0))
    return pl.pallas_call(
        _od_out_kernel,
        out_shape=jax.ShapeDtypeStruct(x.shape, F32),
        grid=(bn, s_len // tm),
        in_specs=[pl.BlockSpec((tm, c), lambda b, i: (i, b)),
                  pl.BlockSpec((None, tm, c), lambda b, i: (b, i, 0)),
                  pl.BlockSpec((2 * c, d), lambda b, i: (0, 0)),
                  x_spec],
        out_specs=x_spec,
        compiler_params=_cparams(("parallel", "parallel")),
        name="od_out",
    )(yc_tm, yd, w_bf16, x)


def _even_layer(x, g, w_in, w_out, rpb):
    qkv, view4, view16 = _ev_proj(x, g, w_in.astype(BF16))
    outs, lses = [], []
    for dil, view in zip(DILATIONS, (qkv, view4, view16)):
        o, lse = _dilated_branch(view, dil)
        outs.append(o)
        lses.append(lse)
    yb = _neighborhood(qkv, rpb)
    return _ev_out(x, outs, lses, yb, w_out.astype(BF16))


def _odd_layer(x, g, w_in, w_out, conv_w, conv_b, wa, ba, wx, bx, lam, lb_f, lb_b, gnorm_g):
    bn, s_len, _ = x.shape
    ug, hg = _od_proj(x, g, w_in.astype(BF16))
    h_fw = _rglru(ug, None, conv_w, conv_b, wa[0], ba[0], wx[0], bx[0], lam[0], bn, reverse=False)
    yc = _rglru(ug, h_fw, conv_w, conv_b, wa[1], ba[1], wx[1], bx[1], lam[1], bn, reverse=True)
    o_fw = _hgrn(hg, None, lb_f, gnorm_g, reverse=False)
    yd = _hgrn(hg, o_fw, lb_b, gnorm_g, reverse=True)
    return _od_out(x, yc.reshape(s_len, bn * LRU_W), yd, w_out.astype(BF16))


def kernel(x, mem, norm_mix_g, norm_xa_g, norm_mem_g, norm_mlp_g, final_norm_g, ev_w_in, ev_w_out, na_rpb,
           od_w_in, od_w_out, conv_w, conv_b, lru_wa, lru_ba, lru_wx, lru_bx, lru_lambda, hgrn_lb_logits,
           hgrn_norm_g, xa_wq, xa_wkv, xa_wo, mlp_w1, mlp_w2):
    depth = norm_mix_g.shape[0]
    p_lb = jax.nn.softmax(hgrn_lb_logits.astype(F32), axis=0)
    lower_bounds = jnp.cumsum(p_lb, axis=0) - p_lb[0:1]
    for layer in range(depth):
        if layer % 2 == 0:
            e = layer // 2
            x = _even_layer(x, norm_mix_g[layer], ev_w_in[e], ev_w_out[e], na_rpb[e])
        else:
            o = layer // 2
            x = _odd_layer(x, norm_mix_g[layer], od_w_in[o], od_w_out[o], conv_w[o], conv_b[o],
                           lru_wa[o], lru_ba[o], lru_wx[o], lru_bx[o], lru_lambda[o],
                           lower_bounds[layer, 0], lower_bounds[layer, 1], hgrn_norm_g[o])
        kv = _mem_kv(mem, norm_mem_g[layer], xa_wkv[layer].astype(BF16))
        x = _xattn(x, norm_xa_g[layer], xa_wq[layer].astype(BF16), kv, xa_wo[layer].astype(BF16))
        x = _mlp(x, norm_mlp_g[layer], mlp_w1[layer].astype(BF16), mlp_w2[layer].astype(BF16),
                 g_final=final_norm_g if layer == depth - 1 else None)
    return x
```

```python
import functools
import math

import jax
import jax.numpy as jnp
import numpy as np
from jax import lax
from jax.experimental import pallas as pl
from jax.experimental.pallas import tpu as pltpu

F32 = jnp.float32
BF16 = jnp.bfloat16

D_MODEL = 1024
HEAD_DIM = 64
ROT_DIM = 16
ROPE_THETA = 500000.0
N_HEADS_A = 8
N_HEADS_B = 8
GROUP_W = N_HEADS_A * HEAD_DIM
DILATIONS = (1, 4, 16)
WIN_HALF = 64
GRID_W = 64
NA_ROWS = 8
NA_COLS = 16
LRU_W = 512
LRU_BLOCKS = 8
LRU_C = 8.0
HG_HEADS = 4
HG_DK = 128
XA_HEADS = 4
XA_DH = 256
D_FF = 4096
EPS = 1e-6

LANES = 128
SUBLANES_BF16 = 16
V7X_VMEM_LIMIT_BYTES = 56 * 1024 * 1024

ROW_TILE = 512
ATT_Q_BLOCK = 512
ATT_SUB = 128
NA_ROW_GROUP = 8
LRU_T = 128
HG_CHUNK = 128
HG_T = 512


def _cparams(sem):
    return pltpu.CompilerParams(dimension_semantics=sem, vmem_limit_bytes=V7X_VMEM_LIMIT_BYTES)


def _rms(x, g):
    return x * lax.rsqrt(jnp.mean(x * x, axis=-1, keepdims=True) + EPS) * g


def _nt_dot(a, b):
    return lax.dot_general(a, b, (((1,), (1,)), ((), ())), preferred_element_type=F32)


def _dot(a, b):
    return jnp.dot(a, b, preferred_element_type=F32)


def _head_selectors():
    lane = lax.broadcasted_iota(jnp.int32, (1, LANES), 1)
    low = (lane < HEAD_DIM).astype(F32)
    return low.astype(BF16), (1.0 - low).astype(BF16)


def _ev_proj_kernel(x_ref, g_ref, w_ref, rc_ref, rs1_ref, rs2_ref, o_ref, o4_ref, o16_ref, slab):
    h = _rms(x_ref[...], g_ref[...]).astype(BF16)
    scale = HEAD_DIM ** -0.5
    tm = x_ref.shape[0]
    n_slabs = GROUP_W // LANES
    for c in range(6):
        acc = _dot(h, w_ref[:, c * GROUP_W:(c + 1) * GROUP_W])
        if c in (0, 1):
            rc, rs1, rs2 = rc_ref[...], rs1_ref[...], rs2_ref[...]
            parts = []
            for j in range(n_slabs):
                t = acc[:, j * LANES:(j + 1) * LANES]
                t = t * rc + pltpu.roll(t, LANES - ROT_DIM // 2, 1) * rs1 + pltpu.roll(t, ROT_DIM // 2, 1) * rs2
                parts.append(t)
            acc = jnp.concatenate(parts, axis=-1)
        if c in (0, 3):
            acc = acc * scale
        o_ref[:, c * GROUP_W:(c + 1) * GROUP_W] = acc.astype(BF16)
        if c < 3:
            for j in range(n_slabs):
                slab[j] = acc[:, j * LANES:(j + 1) * LANES]
            for dil, dst in ((DILATIONS[1], o4_ref), (DILATIONS[2], o16_ref)):
                for r in range(dil):
                    for j in range(n_slabs):
                        col = r * 3 * GROUP_W + c * GROUP_W + j * LANES
                        dst[:, col:col + LANES] = slab[j, pl.ds(r, tm // dil, stride=dil), :].astype(BF16)


def _rope_tables(s_len):
    half = ROT_DIM // 2
    inv = jnp.asarray(ROPE_THETA ** (-np.arange(half) * 2.0 / ROT_DIM), F32)
    ang = jnp.arange(s_len, dtype=F32)[:, None] * inv[None, :]
    cos, sin = jnp.cos(ang), jnp.sin(ang)
    ones = jnp.ones((s_len, HEAD_DIM - ROT_DIM), F32)
    zeros = jnp.zeros((s_len, HEAD_DIM - ROT_DIM), F32)
    zh = jnp.zeros((s_len, half), F32)
    rc = jnp.concatenate([cos, cos, ones], axis=-1)
    rs1 = jnp.concatenate([-sin, zh, zeros], axis=-1)
    rs2 = jnp.concatenate([zh, sin, zeros], axis=-1)
    rep = LANES // HEAD_DIM
    return jnp.tile(rc, (1, rep)), jnp.tile(rs1, (1, rep)), jnp.tile(rs2, (1, rep))


def _ev_proj(x, g, w_bf16):
    bn, s_len, d = x.shape
    tm = min(ROW_TILE, s_len)
    n_out = w_bf16.shape[1]
    rc, rs1, rs2 = _rope_tables(s_len)
    tbl_spec = pl.BlockSpec((tm, LANES), lambda b, i: (i, 0))
    wa = 3 * GROUP_W
    d4, d16 = DILATIONS[1], DILATIONS[2]
    return pl.pallas_call(
        _ev_proj_kernel,
        out_shape=(jax.ShapeDtypeStruct((bn, s_len, n_out), BF16),
                   jax.ShapeDtypeStruct((bn, s_len // d4, d4 * wa), BF16),
                   jax.ShapeDtypeStruct((bn, s_len // d16, d16 * wa), BF16)),
        grid=(bn, s_len // tm),
        in_specs=[
            pl.BlockSpec((None, tm, d), lambda b, i: (b, i, 0)),
            pl.BlockSpec((1, d), lambda b, i: (0, 0)),
            pl.BlockSpec((d, n_out), lambda b, i: (0, 0)),
            tbl_spec, tbl_spec, tbl_spec,
        ],
        out_specs=(pl.BlockSpec((None, tm, n_out), lambda b, i: (b, i, 0)),
                   pl.BlockSpec((None, tm // d4, d4 * wa), lambda b, i: (b, i, 0)),
                   pl.BlockSpec((None, tm // d16, d16 * wa), lambda b, i: (b, i, 0))),
        scratch_shapes=[pltpu.VMEM((GROUP_W // LANES, tm, LANES), F32)],
        compiler_params=_cparams(("parallel", "parallel")),
        name="ev_proj",
    )(x, g.reshape(1, d), w_bf16, rc, rs1, rs2)


def _dilated_kernel(q_ref, kl_ref, kc_ref, kr_ref, vl_ref, vc_ref, vr_ref, o_ref, lse_ref,
                    kext, vext, *, lq, l_total):
    i = pl.program_id(2)
    kext[0:WIN_HALF, :] = kl_ref[...]
    kext[WIN_HALF:WIN_HALF + lq, :] = kc_ref[...]
    kext[WIN_HALF + lq:, :] = kr_ref[...]
    vext[0:WIN_HALF, :] = vl_ref[...]
    vext[WIN_HALF:WIN_HALF + lq, :] = vc_ref[...]
    vext[WIN_HALF + lq:, :] = vr_ref[...]

    wk = ATT_SUB + 2 * WIN_HALF
    qi = lax.broadcasted_iota(jnp.int32, (ATT_SUB, wk), 0)
    ci = lax.broadcasted_iota(jnp.int32, (ATT_SUB, wk), 1)
    band = (ci - qi >= 0) & (ci - qi <= 2 * WIN_HALF)
    lane = lax.broadcasted_iota(jnp.int32, (ATT_SUB, LANES), 1)
    low_half = lane < HEAD_DIM
    head_sel = _head_selectors()

    for j in range(lq // ATT_SUB):
        base = i * lq + j * ATT_SUB - WIN_HALF
        valid = band & (ci + base >= 0) & (ci + base < l_total)
        bias = jnp.where(valid, 0.0, -jnp.inf).astype(F32)
        rows = slice(j * ATT_SUB, (j + 1) * ATT_SUB)
        wrows = slice(j * ATT_SUB, j * ATT_SUB + wk)
        lse_tile = jnp.zeros((ATT_SUB, LANES), F32)
        for hp in range(GROUP_W // LANES):
            cols = slice(hp * LANES, (hp + 1) * LANES)
            qp = q_ref[rows, cols]
            kp = kext[wrows, cols]
            vp = vext[wrows, cols]
            outs = []
            for which in range(2):
                s = _nt_dot(qp * head_sel[which], kp) + bias
                m = jnp.max(s, axis=-1, keepdims=True)
                p = jnp.exp(s - m)
                den = jnp.sum(p, axis=-1, keepdims=True)
                o = _dot(p.astype(BF16), vp) * (1.0 / den)
                outs.append(o)
                lse = m + jnp.log(den)
                lse_tile = jnp.where(lane == 2 * hp + which, lse, lse_tile)
            o_ref[rows, cols] = jnp.where(low_half, outs[0], outs[1]).astype(BF16)
        lse_ref[rows, :] = lse_tile


def _dilated_branch(view, dil):
    bn, l_total, width = view.shape
    n_groups = width // (dil * GROUP_W)
    lq = min(ATT_Q_BLOCK, l_total)
    nblk = l_total // lq
    per = lq // WIN_HALF
    n_halo = l_total // WIN_HALF

    def cur(g):
        return pl.BlockSpec((None, lq, GROUP_W), lambda b, r, i: (b, i, r * n_groups + g))

    def left(g):
        return pl.BlockSpec((None, WIN_HALF, GROUP_W),
                            lambda b, r, i: (b, jnp.maximum(i * per - 1, 0), r * n_groups + g))

    def right(g):
        return pl.BlockSpec((None, WIN_HALF, GROUP_W),
                            lambda b, r, i: (b, jnp.minimum((i + 1) * per, n_halo - 1), r * n_groups + g))

    o, lse = pl.pallas_call(
        functools.partial(_dilated_kernel, lq=lq, l_total=l_total),
        out_shape=(jax.ShapeDtypeStruct((bn, l_total, dil * GROUP_W), BF16),
                   jax.ShapeDtypeStruct((bn, l_total, dil * LANES), F32)),
        grid=(bn, dil, nblk),
        in_specs=[cur(0), left(1), cur(1), right(1), left(2), cur(2), right(2)],
        out_specs=(pl.BlockSpec((None, lq, GROUP_W), lambda b, r, i: (b, i, r)),
                   pl.BlockSpec((None, lq, LANES), lambda b, r, i: (b, i, r))),
        scratch_shapes=[pltpu.VMEM((lq + 2 * WIN_HALF, GROUP_W), BF16),
                        pltpu.VMEM((lq + 2 * WIN_HALF, GROUP_W), BF16)],
        compiler_params=_cparams(("parallel", "parallel", "parallel")),
        name=f"dilated_attn_d{dil}",
    )(view, view, view, view, view, view, view)
    return o, lse


NA_QT = NA_ROW_GROUP * GRID_W
NA_KT = 2 * NA_QT
NA_KW = NA_ROWS * GRID_W


def _na_tables(rpb):
    rpb = rpb.astype(F32)
    nh = rpb.shape[0]
    ext = GRID_W - NA_COLS
    padded = jnp.concatenate([jnp.repeat(rpb[..., :1], ext, axis=-1), rpb,
                              jnp.repeat(rpb[..., -1:], ext, axis=-1)], axis=-1)
    t1 = jnp.stack([padded[..., GRID_W - 1 - qc:2 * GRID_W - 1 - qc] for qc in range(GRID_W)], axis=2)
    qc = np.arange(GRID_W)[:, None]
    kc = np.arange(GRID_W)[None, :]
    c0 = np.clip(qc - NA_COLS // 2, 0, GRID_W - NA_COLS)
    t1 = jnp.where((kc >= c0) & (kc < c0 + NA_COLS), t1, -jnp.inf)
    per_delta = [jnp.concatenate([t1[:, kr - delta + NA_ROWS - 1] for kr in range(NA_ROWS)], axis=-1)
                 for delta in range(NA_ROWS)]
    return jnp.stack(per_delta).reshape(NA_ROWS, nh // 2, 2 * GRID_W, NA_KW)


def _na_kernel(q_ref, k_ref, v_ref, tb_ref, o_ref, s_scr, p_scr, inv_scr, *, rows):
    i0 = pl.program_id(1) * NA_ROW_GROUP
    w0 = jnp.clip(i0 - NA_ROWS // 2, 0, rows - 2 * NA_ROW_GROUP)
    lane = lax.broadcasted_iota(jnp.int32, (GRID_W, LANES), 1)
    low_half = lane < HEAD_DIM
    sel_lo, sel_hi = _head_selectors()
    n_hp = GROUP_W // LANES

    def key_rows(a):
        r0 = jnp.clip(i0 + a - NA_ROWS // 2, 0, rows - NA_ROWS)
        return pl.ds(pl.multiple_of((r0 - w0) * GRID_W, GRID_W), NA_KW), i0 + a - r0

    for a in range(NA_ROW_GROUP):
        krows, delta = key_rows(a)
        qrows = slice(a * GRID_W, (a + 1) * GRID_W)
        for hp in range(n_hp):
            cols = slice(hp * LANES, (hp + 1) * LANES)
            qp = q_ref[qrows, cols]
            q2 = jnp.concatenate([qp * sel_lo, qp * sel_hi], axis=0)
            s_scr[a * n_hp + hp] = _nt_dot(q2, k_ref[0, krows, cols]) + tb_ref[delta, hp]
    for t in range(NA_ROW_GROUP * n_hp):
        s = s_scr[t]
        p = jnp.exp(s - jnp.max(s, axis=-1, keepdims=True))
        inv_scr[t] = jnp.broadcast_to(1.0 / jnp.sum(p, axis=-1, keepdims=True), (2 * GRID_W, LANES))
        p_scr[t] = p.astype(BF16)
    for a in range(NA_ROW_GROUP):
        krows, _ = key_rows(a)
        qrows = slice(a * GRID_W, (a + 1) * GRID_W)
        for hp in range(n_hp):
            cols = slice(hp * LANES, (hp + 1) * LANES)
            t = a * n_hp + hp
            o = _dot(p_scr[t], v_ref[0, krows, cols]) * inv_scr[t]
            o_ref[qrows, cols] = jnp.where(low_half, o[0:GRID_W], o[GRID_W:]).astype(BF16)


def _neighborhood(qkv, rpb):
    bn, s_len, _ = qkv.shape
    rows = s_len // GRID_W
    n_groups = rows // NA_ROW_GROUP
    tables = _na_tables(rpb)

    def window(g):
        return pl.BlockSpec(
            (pl.Element(1), pl.Element(NA_KT), pl.Element(GROUP_W)),
            lambda b, i: (b, jnp.clip(i * NA_ROW_GROUP - NA_ROWS // 2, 0, rows - 2 * NA_ROW_GROUP) * GRID_W,
                          g * GROUP_W))

    return pl.pallas_call(
        functools.partial(_na_kernel, rows=rows),
        out_shape=jax.ShapeDtypeStruct((bn, s_len, GROUP_W), BF16),
        grid=(bn, n_groups),
        in_specs=[pl.BlockSpec((None, NA_QT, GROUP_W), lambda b, i: (b, i, 3)), window(4), window(5),
                  pl.BlockSpec(tables.shape, lambda b, i: (0, 0, 0, 0))],
        out_specs=pl.BlockSpec((None, NA_QT, GROUP_W), lambda b, i: (b, i, 0)),
        scratch_shapes=[pltpu.VMEM((NA_ROW_GROUP * GROUP_W // LANES, 2 * GRID_W, NA_KW), F32),
                        pltpu.VMEM((NA_ROW_GROUP * GROUP_W // LANES, 2 * GRID_W, NA_KW), BF16),
                        pltpu.VMEM((NA_ROW_GROUP * GROUP_W // LANES, 2 * GRID_W, LANES), F32)],
        compiler_params=_cparams(("parallel", "parallel")),
        name="neighborhood_attn",
    )(qkv, qkv, qkv, tables)


def _ev_out_kernel(o1, o2v, o3v, l1, l2v, l3v, yb_ref, ex_ref, w_ref, x_ref, out_ref, so2, so3, sl2, sl3):
    tm = x_ref.shape[0]
    n_slabs = GROUP_W // LANES
    for dil, ov, lv, so, sl in ((DILATIONS[1], o2v, l2v, so2, sl2), (DILATIONS[2], o3v, l3v, so3, sl3)):
        for r in range(dil):
            dst = pl.ds(r, tm // dil, stride=dil)
            for j in range(n_slabs):
                col = r * GROUP_W + j * LANES
                so[j, dst, :] = ov[:, col:col + LANES].astype(F32)
            sl[0, dst, :] = lv[:, r * LANES:(r + 1) * LANES]
    o2 = jnp.concatenate([so2[j] for j in range(n_slabs)], axis=-1)
    o3 = jnp.concatenate([so3[j] for j in range(n_slabs)], axis=-1)
    la, lb, lc = l1[...], sl2[0], sl3[0]
    m = jnp.maximum(jnp.maximum(la, lb), lc)
    ea, eb, ec = jnp.exp(la - m), jnp.exp(lb - m), jnp.exp(lc - m)
    inv = 1.0 / (ea + eb + ec)
    ex = ex_ref[...]

    def expand(w):
        hi = w.astype(BF16)
        lo = (w - hi.astype(F32)).astype(BF16)
        return _dot(hi, ex) + _dot(lo, ex)

    ya = expand(ea * inv) * o1[...].astype(F32) + expand(eb * inv) * o2 + expand(ec * inv) * o3
    y = _dot(ya.astype(BF16), w_ref[0:GROUP_W, :]) + _dot(yb_ref[...], w_ref[GROUP_W:, :])
    out_ref[...] = x_ref[...] + y


def _ev_out(x, outs, lses, yb, w_bf16):
    bn, s_len, d = x.shape
    tm = min(ROW_TILE, s_len)
    ex = np.zeros((LANES, GROUP_W), np.float32)
    for h in range(N_HEADS_A):
        ex[h, h * HEAD_DIM:(h + 1) * HEAD_DIM] = 1.0
    def o_spec(dil):
        return pl.BlockSpec((None, tm // dil, dil * GROUP_W), lambda b, i: (b, i, 0))

    def l_spec(dil):
        return pl.BlockSpec((None, tm // dil, dil * LANES), lambda b, i: (b, i, 0))

    x_spec = pl.BlockSpec((None, tm, d), lambda b, i: (b, i, 0))
    n_slabs = GROUP_W // LANES
    return pl.pallas_call(
        _ev_out_kernel,
        out_shape=jax.ShapeDtypeStruct(x.shape, F32),
        grid=(bn, s_len // tm),
        in_specs=[o_spec(d_) for d_ in DILATIONS] + [l_spec(d_) for d_ in DILATIONS]
        + [o_spec(1),
           pl.BlockSpec((LANES, GROUP_W), lambda b, i: (0, 0)),
           pl.BlockSpec((2 * GROUP_W, d), lambda b, i: (0, 0)),
           x_spec],
        out_specs=x_spec,
        scratch_shapes=[pltpu.VMEM((n_slabs, tm, LANES), F32), pltpu.VMEM((n_slabs, tm, LANES), F32),
                        pltpu.VMEM((1, tm, LANES), F32), pltpu.VMEM((1, tm, LANES), F32)],
        compiler_params=_cparams(("parallel", "parallel")),
        name="ev_out",
    )(*outs, *lses, yb, jnp.asarray(ex, BF16), w_bf16, x)


def _norm_matmul_kernel(x_ref, g_ref, w_ref, o_ref):
    o_ref[...] = _dot(_rms(x_ref[...], g_ref[...]).astype(BF16), w_ref[...]).astype(o_ref.dtype)


def _mem_kv(mem, g, w_bf16):
    bn, m_len, d = mem.shape
    n_out = w_bf16.shape[1]
    return pl.pallas_call(
        _norm_matmul_kernel,
        out_shape=jax.ShapeDtypeStruct((bn, m_len, n_out), BF16),
        grid=(bn,),
        in_specs=[pl.BlockSpec((None, m_len, d), lambda b: (b, 0, 0)),
                  pl.BlockSpec((1, d), lambda b: (0, 0)),
                  pl.BlockSpec((d, n_out), lambda b: (0, 0))],
        out_specs=pl.BlockSpec((None, m_len, n_out), lambda b: (b, 0, 0)),
        compiler_params=_cparams(("parallel",)),
        name="mem_kv",
    )(mem, g.reshape(1, d), w_bf16)


def _xattn_kernel(x_ref, g_ref, wq_ref, kv_ref, wo_ref, out_ref):
    x = x_ref[...]
    q = (_dot(_rms(x, g_ref[...]).astype(BF16), wq_ref[...]) * (XA_DH ** -0.5)).astype(BF16)
    heads = []
    for hd in range(XA_HEADS):
        cols = slice(hd * XA_DH, (hd + 1) * XA_DH)
        s = _nt_dot(q[:, cols], kv_ref[:, cols])
        m = jnp.max(s, axis=-1, keepdims=True)
        p = jnp.exp(s - m)
        den = jnp.sum(p, axis=-1, keepdims=True)
        vcols = slice(D_MODEL + hd * XA_DH, D_MODEL + (hd + 1) * XA_DH)
        heads.append((_dot(p.astype(BF16), kv_ref[:, vcols]) * (1.0 / den)).astype(BF16))
    o = jnp.concatenate(heads, axis=-1)
    out_ref[...] = x + _dot(o, wo_ref[...])


def _xattn(x, g, wq, kv, wo):
    bn, s_len, d = x.shape
    tm = min(ROW_TILE, s_len)
    m_len = kv.shape[1]
    x_spec = pl.BlockSpec((None, tm, d), lambda b, i: (b, i, 0))
    return pl.pallas_call(
        _xattn_kernel,
        out_shape=jax.ShapeDtypeStruct(x.shape, F32),
        grid=(bn, s_len // tm),
        in_specs=[x_spec,
                  pl.BlockSpec((1, d), lambda b, i: (0, 0)),
                  pl.BlockSpec((d, d), lambda b, i: (0, 0)),
                  pl.BlockSpec((None, m_len, 2 * d), lambda b, i: (b, 0, 0)),
                  pl.BlockSpec((d, d), lambda b, i: (0, 0))],
        out_specs=x_spec,
        compiler_params=_cparams(("parallel", "parallel")),
        name="mem_xattn",
    )(x, g.reshape(1, d), wq, kv, wo)


def _mlp_kernel(x_ref, g_ref, w1_ref, w2_ref, gf_ref, out_ref, *, final_norm):
    x = x_ref[...]
    h = _rms(x, g_ref[...]).astype(BF16)
    acc = x
    for c in range(D_FF // D_MODEL):
        cols = slice(c * D_MODEL, (c + 1) * D_MODEL)
        a = jnp.maximum(_dot(h, w1_ref[:, cols]), 0.0)
        acc = acc + _dot((a * a).astype(BF16), w2_ref[cols, :])
    out_ref[...] = _rms(acc, gf_ref[...]) if final_norm else acc


def _mlp(x, g, w1, w2, g_final=None):
    bn, s_len, d = x.shape
    tm = min(ROW_TILE, s_len)
    final_norm = g_final is not None
    gf = (g_final if final_norm else g).reshape(1, d)
    x_spec = pl.BlockSpec((None, tm, d), lambda b, i: (b, i, 0))
    vec = pl.BlockSpec((1, d), lambda b, i: (0, 0))
    return pl.pallas_call(
        functools.partial(_mlp_kernel, final_norm=final_norm),
        out_shape=jax.ShapeDtypeStruct(x.shape, F32),
        grid=(bn, s_len // tm),
        in_specs=[x_spec, vec,
                  pl.BlockSpec((d, D_FF), lambda b, i: (0, 0)),
                  pl.BlockSpec((D_FF, d), lambda b, i: (0, 0)),
                  vec],
        out_specs=x_spec,
        compiler_params=_cparams(("parallel", "parallel")),
        name="mlp_final" if final_norm else "mlp",
    )(x, g.reshape(1, d), w1, w2, gf)


def _od_proj_kernel(x_ref, g_ref, w_ref, ug_ref, hg_ref):
    h = _rms(x_ref[...], g_ref[...]).astype(BF16)
    n_ug = 2 * LRU_W
    for c in range(n_ug // GROUP_W):
        cols = slice(c * GROUP_W, (c + 1) * GROUP_W)
        ug_ref[:, cols] = _dot(h, w_ref[:, cols])
    for c in range(hg_ref.shape[-1] // GROUP_W):
        hg_ref[:, c * GROUP_W:(c + 1) * GROUP_W] = _dot(h, w_ref[:, n_ug + c * GROUP_W:n_ug + (c + 1) * GROUP_W])


def _od_proj(x, g, w_bf16):
    bn, s_len, d = x.shape
    tm = min(ROW_TILE, s_len)
    n_ug = 2 * LRU_W
    n_hg = w_bf16.shape[1] - n_ug
    return pl.pallas_call(
        _od_proj_kernel,
        out_shape=(jax.ShapeDtypeStruct((s_len, bn * n_ug), F32),
                   jax.ShapeDtypeStruct((bn, s_len, n_hg), F32)),
        grid=(bn, s_len // tm),
        in_specs=[pl.BlockSpec((None, tm, d), lambda b, i: (b, i, 0)),
                  pl.BlockSpec((1, d), lambda b, i: (0, 0)),
                  pl.BlockSpec((d, n_ug + n_hg), lambda b, i: (0, 0))],
        out_specs=(pl.BlockSpec((tm, n_ug), lambda b, i: (i, b)),
                   pl.BlockSpec((None, tm, n_hg), lambda b, i: (b, i, 0))),
        compiler_params=_cparams(("parallel", "parallel")),
        name="od_proj",
    )(x, g.reshape(1, d), w_bf16)


def _gelu_tanh(x):
    return 0.5 * x * (1.0 + jnp.tanh(math.sqrt(2.0 / math.pi) * (x + 0.044715 * (x * x * x))))


def _rglru_kernel(*refs, reverse, n_chunks):
    if reverse:
        (ul_ref, u_ref, ur_ref, gate_ref, hf_ref, cw_ref, cb_ref, wa_ref, ba_ref, wx_ref, bx_ref, sp_ref,
         out_ref, ext, a_s, b_s, h_s, carry) = refs
    else:
        (ul_ref, u_ref, ur_ref, cw_ref, cb_ref, wa_ref, ba_ref, wx_ref, bx_ref, sp_ref,
         out_ref, ext, a_s, b_s, carry) = refs
        h_s = out_ref
    step = pl.program_id(0)
    chunk = (n_chunks - 1 - step) if reverse else step
    t_len, bn, c = u_ref.shape

    @pl.when(step == 0)
    def _():
        carry[...] = jnp.zeros_like(carry)

    ext[0:2] = jnp.where(chunk == 0, 0.0, ul_ref[...])
    ext[2:t_len + 2] = u_ref[...]
    ext[t_len + 2:t_len + 3] = jnp.where(chunk == n_chunks - 1, 0.0, ur_ref[...])
    uc = cb_ref[...].reshape(1, 1, c)
    for j in range(4):
        uc = uc + cw_ref[j:j + 1, :].reshape(1, 1, c) * ext[j:j + t_len]

    u2 = uc.reshape(t_len * bn, c)
    ub = u2.astype(BF16)
    half = c // 2

    def gate(w_ref, b_ref):
        z = jnp.concatenate([_dot(ub[:, :half], w_ref[0]), _dot(ub[:, half:], w_ref[1])], axis=-1)
        return jax.nn.sigmoid(z + b_ref[...])

    r = gate(wa_ref, ba_ref)
    ig = gate(wx_ref, bx_ref)
    log_a = (-LRU_C) * r * sp_ref[...]
    a = jnp.exp(log_a)
    b = jnp.sqrt(1.0 - jnp.exp(2.0 * log_a)) * (ig * u2)
    a_s[...] = a.reshape(t_len, bn, c)
    b_s[...] = b.reshape(t_len, bn, c)

    def body(k, h):
        t = (t_len - 1 - k) if reverse else k
        h = a_s[t] * h + b_s[t]
        h_s[t] = h
        return h

    carry[...] = lax.fori_loop(0, t_len, body, carry[...], unroll=8)

    if reverse:
        out_ref[...] = ((hf_ref[...] + h_s[...]) * _gelu_tanh(gate_ref[...])).astype(out_ref.dtype)


def _rglru(ug, hf, conv_w, conv_b, wa, ba, wx, bx, lam, bn, *, reverse):
    s_len = ug.shape[0]
    c = LRU_W
    ug3 = ug.reshape(s_len, bn, 2 * c)
    t_len = min(LRU_T, s_len)
    n_chunks = s_len // t_len

    def ck(i):
        return (n_chunks - 1 - i) if reverse else i

    half = c // 2
    nb = LRU_BLOCKS // 2

    def dense_halves(w):
        w = w.astype(F32).reshape(2, nb, c // LRU_BLOCKS, c // LRU_BLOCKS)
        eye = jnp.eye(nb, dtype=F32)
        return jnp.einsum('gnij,nm->gnimj', w, eye).reshape(2, half, half).astype(BF16)

    softplus_neg_lam = jax.nn.softplus(-lam.astype(F32)).reshape(1, c)
    vec = pl.BlockSpec((1, c), lambda i: (0, 0))
    wspec = pl.BlockSpec((2, half, half), lambda i: (0, 0, 0))
    blk = pl.BlockSpec((t_len, bn, c), lambda i: (ck(i), 0, 0))
    in_specs = [
        pl.BlockSpec((2, bn, c), lambda i: (jnp.maximum(ck(i) * (t_len // 2) - 1, 0), 0, 0)),
        blk,
        pl.BlockSpec((1, bn, c), lambda i: (jnp.minimum((ck(i) + 1) * t_len, s_len - 1), 0, 0)),
    ]
    args = [ug3, ug3, ug3]
    scratch = [pltpu.VMEM((t_len + 3, bn, c), F32), pltpu.VMEM((t_len, bn, c), F32),
               pltpu.VMEM((t_len, bn, c), F32)]
    if reverse:
        in_specs += [pl.BlockSpec((t_len, bn, c), lambda i: (ck(i), 0, 1)), blk]
        args += [ug3, hf]
        scratch.append(pltpu.VMEM((t_len, bn, c), F32))
    scratch.append(pltpu.VMEM((bn, c), F32))
    in_specs += [pl.BlockSpec((4, c), lambda i: (0, 0)), vec, wspec, vec, wspec, vec, vec]
    args += [conv_w.astype(F32), conv_b.astype(F32).reshape(1, c), dense_halves(wa),
             ba.astype(F32).reshape(1, c), dense_halves(wx), bx.astype(F32).reshape(1, c), softplus_neg_lam]
    return pl.pallas_call(
        functools.partial(_rglru_kernel, reverse=reverse, n_chunks=n_chunks),
        out_shape=jax.ShapeDtypeStruct((s_len, bn, c), BF16 if reverse else F32),
        grid=(n_chunks,),
        in_specs=in_specs,
        out_specs=blk,
        scratch_shapes=scratch,
        compiler_params=_cparams(("arbitrary",)),
        name="rglru_bw" if reverse else "rglru_fw",
    )(*args)


def _hg_matrices(c, reverse):
    n_levels = int(math.log2(c))
    t = np.arange(c)[:, None]
    r = np.arange(c)[None, :]
    mats, masks, upper = [], [], []
    if not reverse:
        mats.append(r <= t)
        mats.append(r > t)
    else:
        mats.append(r >= t)
        mats.append(r < t)
    for lev in range(n_levels):
        half = c >> (lev + 1)
        parent = 2 * half
        start = (np.arange(c) // parent) * parent
        mid = (start + half)[:, None]
        later = (np.arange(c) % parent >= half)[:, None]
        if not reverse:
            m = np.where(later, (r >= mid) & (r <= t), (r > t) & (r < mid))
            is_q = later
        else:
            m = np.where(later, (r >= mid) & (r < t), (r >= t) & (r < mid))
            is_q = ~later
        mats.append(m)
        same_parent = (start[:, None] == start[None, :])
        masks.append(same_parent & is_q & (~is_q).T)
        upper.append(np.broadcast_to(is_q, (c, HG_DK)))
    mat = np.concatenate(mats, axis=0).astype(np.float32)
    return (np.concatenate([mat, mat], axis=1), np.stack(masks).astype(np.float32),
            np.stack(upper).astype(np.float32))


def _hgrn_kernel(*refs, reverse, n_steps, n_levels):
    if reverse:
        (q_ref, f_ref, v_ref, go_ref, of_ref, lb_ref, gn_ref, mat_ref, mask_ref, isq_ref, out_ref, state) = refs
    else:
        (q_ref, f_ref, v_ref, lb_ref, mat_ref, mask_ref, isq_ref, out_ref, state) = refs
    step = pl.program_id(1)
    c = HG_CHUNK
    t_len = q_ref.shape[0]
    n_chunks = t_len // c

    @pl.when(step == 0)
    def _():
        state[...] = jnp.zeros_like(state)

    def chunk_body(ci, carry):
        ch = (n_chunks - 1 - ci) if reverse else ci
        rows = pl.ds(pl.multiple_of(ch * c, c), c)
        decs, kks = [], []
        for hp in range(HG_HEADS // 2):
            cols2 = slice(2 * hp * HG_DK, 2 * (hp + 1) * HG_DK)
            fl = f_ref[rows, cols2]
            e = jnp.exp(-jnp.abs(fl))
            r = 1.0 / (1.0 + e)
            er = e * r
            pos = fl > 0.0
            lb = lb_ref[0:1, cols2]
            oml = lb_ref[1:2, cols2]
            g2 = jnp.log2(lb + oml * jnp.where(pos, r, er))
            kks.append(oml * jnp.where(pos, er, r))
            g_hi = g2.astype(BF16)
            g_lo = (g2 - g_hi.astype(F32)).astype(BF16)
            decs.append(jnp.exp2(_dot(mat_ref[...], jnp.concatenate([g_hi, g_lo], axis=0))))

        for hd in range(HG_HEADS):
            cols = slice(hd * HG_DK, (hd + 1) * HG_DK)
            pcols = slice((hd % 2) * HG_DK, (hd % 2 + 1) * HG_DK)
            kk = kks[hd // 2][:, pcols]
            dec = decs[hd // 2]
            qv = q_ref[rows, cols]
            qq = qv * jax.nn.sigmoid(qv)
            v = v_ref[rows, cols]
            vb = v.astype(BF16)
            qb, kb = qq.astype(BF16), kk.astype(BF16)

            def decay(idx):
                return dec[idx * c:(idx + 1) * c, pcols]

            st = state[hd]
            d_in = decay(0)
            edge = (c - 1) if not reverse else 0
            o = _nt_dot((qq * d_in).astype(BF16), st.astype(BF16))
            ks = (kk * decay(1)).astype(BF16)
            state[hd] = st * d_in[edge:edge + 1, :] + _dot(v.T.astype(BF16), ks)

            att = jnp.zeros((c, c), F32)
            for lev in range(n_levels):
                half = c >> (lev + 1)
                if half >= SUBLANES_BF16:
                    first_is_q = bool(reverse)
                    base = jnp.concatenate(
                        [(qb if (blk % 2 == 1) != first_is_q else kb)[blk * half:(blk + 1) * half]
                         for blk in range(c // half)], axis=0)
                else:
                    base = jnp.where(isq_ref[lev] > 0.5, qb, kb)
                xl = base * decay(2 + lev).astype(BF16)
                att = att + mask_ref[lev] * _nt_dot(xl, xl)
            o = o + _dot(att.astype(BF16), vb) + jnp.sum(qq * kk, axis=-1, keepdims=True) * v

            if reverse:
                o = o + of_ref[rows, cols]
                o = o * lax.rsqrt(jnp.mean(o * o, axis=-1, keepdims=True) + EPS) * gn_ref[...]
                gv = go_ref[rows, cols]
                out_ref[rows, cols] = (o * (gv * jax.nn.sigmoid(gv))).astype(out_ref.dtype)
            else:
                out_ref[rows, cols] = o
        return carry

    lax.fori_loop(0, n_chunks, chunk_body, 0)


def _hgrn(hg, o_fw, lb, gnorm_g, *, reverse):
    bn, s_len, _ = hg.shape
    w = HG_HEADS * HG_DK
    t_len = min(HG_T, s_len)
    n_steps = s_len // t_len
    mats, masks, isq = _hg_matrices(HG_CHUNK, reverse)
    n_levels = masks.shape[0]
    lbf = lb.astype(F32)
    lb_tab = jnp.stack([lbf, 1.0 - lbf])

    def blk(g):
        return pl.BlockSpec((None, t_len, w), lambda b, i: (b, (n_steps - 1 - i) if reverse else i, g))

    def const(shape):
        return pl.BlockSpec(shape, lambda b, i: (0,) * len(shape))

    in_specs = [blk(0), blk(2 if reverse else 1), blk(3)]
    args = [hg, hg, hg]
    if reverse:
        in_specs += [blk(4), blk(0)]
        args += [hg, o_fw]
    in_specs.append(const((2, w)))
    args.append(lb_tab)
    if reverse:
        in_specs.append(const((1, HG_DK)))
        args.append(gnorm_g.astype(F32).reshape(1, HG_DK))
    in_specs += [const(mats.shape), const(masks.shape), const(isq.shape)]
    args += [jnp.asarray(mats, BF16), jnp.asarray(masks, F32), jnp.asarray(isq, BF16)]
    return pl.pallas_call(
        functools.partial(_hgrn_kernel, reverse=reverse, n_steps=n_steps, n_levels=n_levels),
        out_shape=jax.ShapeDtypeStruct((bn, s_len, w), BF16 if reverse else F32),
        grid=(bn, n_steps),
        in_specs=in_specs,
        out_specs=blk(0),
        scratch_shapes=[pltpu.VMEM((HG_HEADS, HG_DK, HG_DK), F32)],
        compiler_params=_cparams(("parallel", "arbitrary")),
        name="hgrn_bw" if reverse else "hgrn_fw",
    )(*args)


def _od_out_kernel(yc_ref, yd_ref, w_ref, x_ref, out_ref):
    half = yc_ref.shape[-1]
    out_ref[...] = x_ref[...] + _dot(yc_ref[...], w_ref[0:half, :]) + _dot(yd_ref[...], w_ref[half:, :])


def _od_out(x, yc_tm, yd, w_bf16):
    bn, s_len, d = x.shape
    tm = min(ROW_TILE, s_len)
    c = yd.shape[-1]
    x_spec = pl.BlockSpec((None, tm, d), lambda b, i: (b, i, 0))
    return pl.pallas_call(
        _od_out_kernel,
        out_shape=jax.ShapeDtypeStruct(x.shape, F32),
        grid=(bn, s_len // tm),
        in_specs=[pl.BlockSpec((tm, c), lambda b, i: (i, b)),
                  pl.BlockSpec((None, tm, c), lambda b, i: (b, i, 0)),
                  pl.BlockSpec((2 * c, d), lambda b, i: (0, 0)),
                  x_spec],
        out_specs=x_spec,
        compiler_params=_cparams(("parallel", "parallel")),
        name="od_out",
    )(yc_tm, yd, w_bf16, x)


def _even_layer(x, g, w_in, w_out, rpb):
    qkv, view4, view16 = _ev_proj(x, g, w_in.astype(BF16))
    outs, lses = [], []
    for dil, view in zip(DILATIONS, (qkv, view4, view16)):
        o, lse = _dilated_branch(view, dil)
        outs.append(o)
        lses.append(lse)
    yb = _neighborhood(qkv, rpb)
    return _ev_out(x, outs, lses, yb, w_out.astype(BF16))


def _odd_layer(x, g, w_in, w_out, conv_w, conv_b, wa, ba, wx, bx, lam, lb_f, lb_b, gnorm_g):
    bn, s_len, _ = x.shape
    ug, hg = _od_proj(x, g, w_in.astype(BF16))
    h_fw = _rglru(ug, None, conv_w, conv_b, wa[0], ba[0], wx[0], bx[0], lam[0], bn, reverse=False)
    yc = _rglru(ug, h_fw, conv_w, conv_b, wa[1], ba[1], wx[1], bx[1], lam[1], bn, reverse=True)
    o_fw = _hgrn(hg, None, lb_f, gnorm_g, reverse=False)
    yd = _hgrn(hg, o_fw, lb_b, gnorm_g, reverse=True)
    return _od_out(x, yc.reshape(s_len, bn * LRU_W), yd, w_out.astype(BF16))


def kernel(x, mem, norm_mix_g, norm_xa_g, norm_mem_g, norm_mlp_g, final_norm_g, ev_w_in, ev_w_out, na_rpb,
           od_w_in, od_w_out, conv_w, conv_b, lru_wa, lru_ba, lru_wx, lru_bx, lru_lambda, hgrn_lb_logits,
           hgrn_norm_g, xa_wq, xa_wkv, xa_wo, mlp_w1, mlp_w2):
    depth = norm_mix_g.shape[0]
    p_lb = jax.nn.softmax(hgrn_lb_logits.astype(F32), axis=0)
    lower_bounds = jnp.cumsum(p_lb, axis=0) - p_lb[0:1]
    for layer in range(depth):
        if layer % 2 == 0:
            e = layer // 2
            x = _even_layer(x, norm_mix_g[layer], ev_w_in[e], ev_w_out[e], na_rpb[e])
        else:
            o = layer // 2
            x = _odd_layer(x, norm_mix_g[layer], od_w_in[o], od_w_out[o], conv_w[o], conv_b[o],
                           lru_wa[o], lru_ba[o], lru_wx[o], lru_bx[o], lru_lambda[o],
                           lower_bounds[layer, 0], lower_bounds[layer, 1], hgrn_norm_g[o])
        kv = _mem_kv(mem, norm_mem_g[layer], xa_wkv[layer].astype(BF16))
        x = _xattn(x, norm_xa_g[layer], xa_wq[layer].astype(BF16), kv, xa_wo[layer].astype(BF16))
        x = _mlp(x, norm_mlp_g[layer], mlp_w1[layer].astype(BF16), mlp_w2[layer].astype(BF16),
                 g_final=final_norm_g if layer == depth - 1 else None)
    return x
```

```python
import functools
import math

import jax
import jax.numpy as jnp
import numpy as np
from jax import lax
from jax.experimental import pallas as pl
from jax.experimental.pallas import tpu as pltpu

F32 = jnp.float32
BF16 = jnp.bfloat16

D_MODEL = 1024
HEAD_DIM = 64
ROT_DIM = 16
ROPE_THETA = 500000.0
N_HEADS_A = 8
N_HEADS_B = 8
GROUP_W = N_HEADS_A * HEAD_DIM
DILATIONS = (1, 4, 16)
WIN_HALF = 64
GRID_W = 64
NA_ROWS = 8
NA_COLS = 16
LRU_W = 512
LRU_BLOCKS = 8
LRU_C = 8.0
HG_HEADS = 4
HG_DK = 128
XA_HEADS = 4
XA_DH = 256
D_FF = 4096
EPS = 1e-6

LANES = 128
SUBLANES_BF16 = 16
V7X_VMEM_LIMIT_BYTES = 56 * 1024 * 1024

ROW_TILE = 512
ATT_Q_BLOCK = 512
ATT_SUB = 128
NA_ROW_GROUP = 8
LRU_T = 128
HG_CHUNK = 128
HG_T = 512


def _cparams(sem):
    return pltpu.CompilerParams(dimension_semantics=sem, vmem_limit_bytes=V7X_VMEM_LIMIT_BYTES)


def _rms(x, g):
    return x * lax.rsqrt(jnp.mean(x * x, axis=-1, keepdims=True) + EPS) * g


def _nt_dot(a, b):
    return lax.dot_general(a, b, (((1,), (1,)), ((), ())), preferred_element_type=F32)


def _dot(a, b):
    return jnp.dot(a, b, preferred_element_type=F32)


def _head_selectors():
    lane = lax.broadcasted_iota(jnp.int32, (1, LANES), 1)
    low = (lane < HEAD_DIM).astype(F32)
    return low.astype(BF16), (1.0 - low).astype(BF16)


def _ev_proj_kernel(x_ref, g_ref, w_ref, rc_ref, rs1_ref, rs2_ref, o_ref, o4_ref, o16_ref, slab):
    h = _rms(x_ref[...], g_ref[...]).astype(BF16)
    scale = HEAD_DIM ** -0.5
    tm = x_ref.shape[0]
    n_slabs = GROUP_W // LANES
    for c in range(6):
        acc = _dot(h, w_ref[:, c * GROUP_W:(c + 1) * GROUP_W])
        if c in (0, 1):
            rc, rs1, rs2 = rc_ref[...], rs1_ref[...], rs2_ref[...]
            parts = []
            for j in range(n_slabs):
                t = acc[:, j * LANES:(j + 1) * LANES]
                t = t * rc + pltpu.roll(t, LANES - ROT_DIM // 2, 1) * rs1 + pltpu.roll(t, ROT_DIM // 2, 1) * rs2
                parts.append(t)
            acc = jnp.concatenate(parts, axis=-1)
        if c in (0, 3):
            acc = acc * scale
        o_ref[:, c * GROUP_W:(c + 1) * GROUP_W] = acc.astype(BF16)
        if c < 3:
            for j in range(n_slabs):
                slab[j] = acc[:, j * LANES:(j + 1) * LANES]
            for dil, dst in ((DILATIONS[1], o4_ref), (DILATIONS[2], o16_ref)):
                for r in range(dil):
                    for j in range(n_slabs):
                        col = r * 3 * GROUP_W + c * GROUP_W + j * LANES
                        dst[:, col:col + LANES] = slab[j, pl.ds(r, tm // dil, stride=dil), :].astype(BF16)


def _rope_tables(s_len):
    half = ROT_DIM // 2
    inv = jnp.asarray(ROPE_THETA ** (-np.arange(half) * 2.0 / ROT_DIM), F32)
    ang = jnp.arange(s_len, dtype=F32)[:, None] * inv[None, :]
    cos, sin = jnp.cos(ang), jnp.sin(ang)
    ones = jnp.ones((s_len, HEAD_DIM - ROT_DIM), F32)
    zeros = jnp.zeros((s_len, HEAD_DIM - ROT_DIM), F32)
    zh = jnp.zeros((s_len, half), F32)
    rc = jnp.concatenate([cos, cos, ones], axis=-1)
    rs1 = jnp.concatenate([-sin, zh, zeros], axis=-1)
    rs2 = jnp.concatenate([zh, sin, zeros], axis=-1)
    rep = LANES // HEAD_DIM
    return jnp.tile(rc, (1, rep)), jnp.tile(rs1, (1, rep)), jnp.tile(rs2, (1, rep))


def _ev_proj(x, g, w_bf16):
    bn, s_len, d = x.shape
    tm = min(ROW_TILE, s_len)
    n_out = w_bf16.shape[1]
    rc, rs1, rs2 = _rope_tables(s_len)
    tbl_spec = pl.BlockSpec((tm, LANES), lambda b, i: (i, 0))
    wa = 3 * GROUP_W
    d4, d16 = DILATIONS[1], DILATIONS[2]
    return pl.pallas_call(
        _ev_proj_kernel,
        out_shape=(jax.ShapeDtypeStruct((bn, s_len, n_out), BF16),
                   jax.ShapeDtypeStruct((bn, s_len // d4, d4 * wa), BF16),
                   jax.ShapeDtypeStruct((bn, s_len // d16, d16 * wa), BF16)),
        grid=(bn, s_len // tm),
        in_specs=[
            pl.BlockSpec((None, tm, d), lambda b, i: (b, i, 0)),
            pl.BlockSpec((1, d), lambda b, i: (0, 0)),
            pl.BlockSpec((d, n_out), lambda b, i: (0, 0)),
            tbl_spec, tbl_spec, tbl_spec,
        ],
        out_specs=(pl.BlockSpec((None, tm, n_out), lambda b, i: (b, i, 0)),
                   pl.BlockSpec((None, tm // d4, d4 * wa), lambda b, i: (b, i, 0)),
                   pl.BlockSpec((None, tm // d16, d16 * wa), lambda b, i: (b, i, 0))),
        scratch_shapes=[pltpu.VMEM((GROUP_W // LANES, tm, LANES), F32)],
        compiler_params=_cparams(("parallel", "parallel")),
        name="ev_proj",
    )(x, g.reshape(1, d), w_bf16, rc, rs1, rs2)


def _dilated_kernel(q_ref, kl_ref, kc_ref, kr_ref, vl_ref, vc_ref, vr_ref, o_ref, lse_ref,
                    kext, vext, s_scr, p_scr, inv_scr, *, lq, l_total):
    i = pl.program_id(2)
    kext[0:WIN_HALF, :] = kl_ref[...]
    kext[WIN_HALF:WIN_HALF + lq, :] = kc_ref[...]
    kext[WIN_HALF + lq:, :] = kr_ref[...]
    vext[0:WIN_HALF, :] = vl_ref[...]
    vext[WIN_HALF:WIN_HALF + lq, :] = vc_ref[...]
    vext[WIN_HALF + lq:, :] = vr_ref[...]

    wk = ATT_SUB + 2 * WIN_HALF
    qi = lax.broadcasted_iota(jnp.int32, (ATT_SUB, wk), 0)
    ci = lax.broadcasted_iota(jnp.int32, (ATT_SUB, wk), 1)
    band = (ci - qi >= 0) & (ci - qi <= 2 * WIN_HALF)
    lane = lax.broadcasted_iota(jnp.int32, (ATT_SUB, LANES), 1)
    low_half = lane < HEAD_DIM
    sel_lo, sel_hi = _head_selectors()
    n_sub = lq // ATT_SUB
    n_hp = GROUP_W // LANES

    for j in range(n_sub):
        base = i * lq + j * ATT_SUB - WIN_HALF
        valid = band & (ci + base >= 0) & (ci + base < l_total)
        bias = jnp.where(valid, 0.0, -jnp.inf).astype(F32)
        bias2 = jnp.concatenate([bias, bias], axis=0)
        rows = slice(j * ATT_SUB, (j + 1) * ATT_SUB)
        wrows = slice(j * ATT_SUB, j * ATT_SUB + wk)
        for hp in range(n_hp):
            cols = slice(hp * LANES, (hp + 1) * LANES)
            qp = q_ref[rows, cols]
            q2 = jnp.concatenate([qp * sel_lo, qp * sel_hi], axis=0)
            s_scr[j * n_hp + hp] = _nt_dot(q2, kext[wrows, cols]) + bias2
    for j in range(n_sub):
        lse_tile = jnp.zeros((ATT_SUB, LANES), F32)
        for hp in range(n_hp):
            t = j * n_hp + hp
            s = s_scr[t]
            m = jnp.max(s, axis=-1, keepdims=True)
            p = jnp.exp(s - m)
            den = jnp.sum(p, axis=-1, keepdims=True)
            p_scr[t] = p.astype(BF16)
            inv_scr[t] = jnp.broadcast_to(1.0 / den, (2 * ATT_SUB, LANES))
            lse = m + jnp.log(den)
            lse_tile = jnp.where(lane == 2 * hp, lse[0:ATT_SUB],
                                 jnp.where(lane == 2 * hp + 1, lse[ATT_SUB:], lse_tile))
        lse_ref[j * ATT_SUB:(j + 1) * ATT_SUB, :] = lse_tile
    for j in range(n_sub):
        rows = slice(j * ATT_SUB, (j + 1) * ATT_SUB)
        wrows = slice(j * ATT_SUB, j * ATT_SUB + wk)
        for hp in range(n_hp):
            cols = slice(hp * LANES, (hp + 1) * LANES)
            t = j * n_hp + hp
            o = _dot(p_scr[t], vext[wrows, cols]) * inv_scr[t]
            o_ref[rows, cols] = jnp.where(low_half, o[0:ATT_SUB], o[ATT_SUB:]).astype(BF16)


def _dilated_branch(view, dil):
    bn, l_total, width = view.shape
    n_groups = width // (dil * GROUP_W)
    lq = min(ATT_Q_BLOCK, l_total)
    nblk = l_total // lq
    per = lq // WIN_HALF
    n_halo = l_total // WIN_HALF
    n_units = (lq // ATT_SUB) * (GROUP_W // LANES)

    def cur(g):
        return pl.BlockSpec((None, lq, GROUP_W), lambda b, r, i: (b, i, r * n_groups + g))

    def left(g):
        return pl.BlockSpec((None, WIN_HALF, GROUP_W),
                            lambda b, r, i: (b, jnp.maximum(i * per - 1, 0), r * n_groups + g))

    def right(g):
        return pl.BlockSpec((None, WIN_HALF, GROUP_W),
                            lambda b, r, i: (b, jnp.minimum((i + 1) * per, n_halo - 1), r * n_groups + g))

    o, lse = pl.pallas_call(
        functools.partial(_dilated_kernel, lq=lq, l_total=l_total),
        out_shape=(jax.ShapeDtypeStruct((bn, l_total, dil * GROUP_W), BF16),
                   jax.ShapeDtypeStruct((bn, l_total, dil * LANES), F32)),
        grid=(bn, dil, nblk),
        in_specs=[cur(0), left(1), cur(1), right(1), left(2), cur(2), right(2)],
        out_specs=(pl.BlockSpec((None, lq, GROUP_W), lambda b, r, i: (b, i, r)),
                   pl.BlockSpec((None, lq, LANES), lambda b, r, i: (b, i, r))),
        scratch_shapes=[pltpu.VMEM((lq + 2 * WIN_HALF, GROUP_W), BF16),
                        pltpu.VMEM((lq + 2 * WIN_HALF, GROUP_W), BF16),
                        pltpu.VMEM((n_units, 2 * ATT_SUB, ATT_SUB + 2 * WIN_HALF), F32),
                        pltpu.VMEM((n_units, 2 * ATT_SUB, ATT_SUB + 2 * WIN_HALF), BF16),
                        pltpu.VMEM((n_units, 2 * ATT_SUB, LANES), F32)],
        compiler_params=_cparams(("parallel", "parallel", "parallel")),
        name=f"dilated_attn_d{dil}",
    )(view, view, view, view, view, view, view)
    return o, lse


NA_QT = NA_ROW_GROUP * GRID_W
NA_KT = 2 * NA_QT
NA_KW = NA_ROWS * GRID_W


def _na_tables(rpb):
    rpb = rpb.astype(F32)
    nh = rpb.shape[0]
    ext = GRID_W - NA_COLS
    padded = jnp.concatenate([jnp.repeat(rpb[..., :1], ext, axis=-1), rpb,
                              jnp.repeat(rpb[..., -1:], ext, axis=-1)], axis=-1)
    t1 = jnp.stack([padded[..., GRID_W - 1 - qc:2 * GRID_W - 1 - qc] for qc in range(GRID_W)], axis=2)
    qc = np.arange(GRID_W)[:, None]
    kc = np.arange(GRID_W)[None, :]
    c0 = np.clip(qc - NA_COLS // 2, 0, GRID_W - NA_COLS)
    t1 = jnp.where((kc >= c0) & (kc < c0 + NA_COLS), t1, -jnp.inf)
    per_delta = [jnp.concatenate([t1[:, kr - delta + NA_ROWS - 1] for kr in range(NA_ROWS)], axis=-1)
                 for delta in range(NA_ROWS)]
    return jnp.stack(per_delta).reshape(NA_ROWS, nh // 2, 2 * GRID_W, NA_KW)


def _na_kernel(q_ref, k_ref, v_ref, tb_ref, o_ref, s_scr, p_scr, inv_scr, *, rows):
    i0 = pl.program_id(1) * NA_ROW_GROUP
    w0 = jnp.clip(i0 - NA_ROWS // 2, 0, rows - 2 * NA_ROW_GROUP)
    lane = lax.broadcasted_iota(jnp.int32, (GRID_W, LANES), 1)
    low_half = lane < HEAD_DIM
    sel_lo, sel_hi = _head_selectors()
    n_hp = GROUP_W // LANES

    def key_rows(a):
        r0 = jnp.clip(i0 + a - NA_ROWS // 2, 0, rows - NA_ROWS)
        return pl.ds(pl.multiple_of((r0 - w0) * GRID_W, GRID_W), NA_KW), i0 + a - r0

    for a in range(NA_ROW_GROUP):
        krows, delta = key_rows(a)
        qrows = slice(a * GRID_W, (a + 1) * GRID_W)
        for hp in range(n_hp):
            cols = slice(hp * LANES, (hp + 1) * LANES)
            qp = q_ref[qrows, cols]
            q2 = jnp.concatenate([qp * sel_lo, qp * sel_hi], axis=0)
            s_scr[a * n_hp + hp] = _nt_dot(q2, k_ref[0, krows, cols]) + tb_ref[delta, hp]
    for t in range(NA_ROW_GROUP * n_hp):
        s = s_scr[t]
        p = jnp.exp(s - jnp.max(s, axis=-1, keepdims=True))
        inv_scr[t] = jnp.broadcast_to(1.0 / jnp.sum(p, axis=-1, keepdims=True), (2 * GRID_W, LANES))
        p_scr[t] = p.astype(BF16)
    for a in range(NA_ROW_GROUP):
        krows, _ = key_rows(a)
        qrows = slice(a * GRID_W, (a + 1) * GRID_W)
        for hp in range(n_hp):
            cols = slice(hp * LANES, (hp + 1) * LANES)
            t = a * n_hp + hp
            o = _dot(p_scr[t], v_ref[0, krows, cols]) * inv_scr[t]
            o_ref[qrows, cols] = jnp.where(low_half, o[0:GRID_W], o[GRID_W:]).astype(BF16)


def _neighborhood(qkv, rpb):
    bn, s_len, _ = qkv.shape
    rows = s_len // GRID_W
    n_groups = rows // NA_ROW_GROUP
    tables = _na_tables(rpb)

    def window(g):
        return pl.BlockSpec(
            (pl.Element(1), pl.Element(NA_KT), pl.Element(GROUP_W)),
            lambda b, i: (b, jnp.clip(i * NA_ROW_GROUP - NA_ROWS // 2, 0, rows - 2 * NA_ROW_GROUP) * GRID_W,
                          g * GROUP_W))

    return pl.pallas_call(
        functools.partial(_na_kernel, rows=rows),
        out_shape=jax.ShapeDtypeStruct((bn, s_len, GROUP_W), BF16),
        grid=(bn, n_groups),
        in_specs=[pl.BlockSpec((None, NA_QT, GROUP_W), lambda b, i: (b, i, 3)), window(4), window(5),
                  pl.BlockSpec(tables.shape, lambda b, i: (0, 0, 0, 0))],
        out_specs=pl.BlockSpec((None, NA_QT, GROUP_W), lambda b, i: (b, i, 0)),
        scratch_shapes=[pltpu.VMEM((NA_ROW_GROUP * GROUP_W // LANES, 2 * GRID_W, NA_KW), F32),
                        pltpu.VMEM((NA_ROW_GROUP * GROUP_W // LANES, 2 * GRID_W, NA_KW), BF16),
                        pltpu.VMEM((NA_ROW_GROUP * GROUP_W // LANES, 2 * GRID_W, LANES), F32)],
        compiler_params=_cparams(("parallel", "parallel")),
        name="neighborhood_attn",
    )(qkv, qkv, qkv, tables)


def _ev_out_kernel(o1, o2v, o3v, l1, l2v, l3v, yb_ref, ex_ref, w_ref, x_ref, out_ref, so2, so3, sl2, sl3):
    tm = x_ref.shape[0]
    n_slabs = GROUP_W // LANES
    for dil, ov, lv, so, sl in ((DILATIONS[1], o2v, l2v, so2, sl2), (DILATIONS[2], o3v, l3v, so3, sl3)):
        for r in range(dil):
            dst = pl.ds(r, tm // dil, stride=dil)
            for j in range(n_slabs):
                col = r * GROUP_W + j * LANES
                so[j, dst, :] = ov[:, col:col + LANES].astype(F32)
            sl[0, dst, :] = lv[:, r * LANES:(r + 1) * LANES]
    o2 = jnp.concatenate([so2[j] for j in range(n_slabs)], axis=-1)
    o3 = jnp.concatenate([so3[j] for j in range(n_slabs)], axis=-1)
    la, lb, lc = l1[...], sl2[0], sl3[0]
    m = jnp.maximum(jnp.maximum(la, lb), lc)
    ea, eb, ec = jnp.exp(la - m), jnp.exp(lb - m), jnp.exp(lc - m)
    inv = 1.0 / (ea + eb + ec)
    ex = ex_ref[...]

    def expand(w):
        hi = w.astype(BF16)
        lo = (w - hi.astype(F32)).astype(BF16)
        return _dot(hi, ex) + _dot(lo, ex)

    ya = expand(ea * inv) * o1[...].astype(F32) + expand(eb * inv) * o2 + expand(ec * inv) * o3
    y = _dot(ya.astype(BF16), w_ref[0:GROUP_W, :]) + _dot(yb_ref[...], w_ref[GROUP_W:, :])
    out_ref[...] = x_ref[...] + y


def _ev_out(x, outs, lses, yb, w_bf16):
    bn, s_len, d = x.shape
    tm = min(ROW_TILE, s_len)
    ex = np.zeros((LANES, GROUP_W), np.float32)
    for h in range(N_HEADS_A):
        ex[h, h * HEAD_DIM:(h + 1) * HEAD_DIM] = 1.0
    def o_spec(dil):
        return pl.BlockSpec((None, tm // dil, dil * GROUP_W), lambda b, i: (b, i, 0))

    def l_spec(dil):
        return pl.BlockSpec((None, tm // dil, dil * LANES), lambda b, i: (b, i, 0))

    x_spec = pl.BlockSpec((None, tm, d), lambda b, i: (b, i, 0))
    n_slabs = GROUP_W // LANES
    return pl.pallas_call(
        _ev_out_kernel,
        out_shape=jax.ShapeDtypeStruct(x.shape, F32),
        grid=(bn, s_len // tm),
        in_specs=[o_spec(d_) for d_ in DILATIONS] + [l_spec(d_) for d_ in DILATIONS]
        + [o_spec(1),
           pl.BlockSpec((LANES, GROUP_W), lambda b, i: (0, 0)),
           pl.BlockSpec((2 * GROUP_W, d), lambda b, i: (0, 0)),
           x_spec],
        out_specs=x_spec,
        scratch_shapes=[pltpu.VMEM((n_slabs, tm, LANES), F32), pltpu.VMEM((n_slabs, tm, LANES), F32),
                        pltpu.VMEM((1, tm, LANES), F32), pltpu.VMEM((1, tm, LANES), F32)],
        compiler_params=_cparams(("parallel", "parallel")),
        name="ev_out",
    )(*outs, *lses, yb, jnp.asarray(ex, BF16), w_bf16, x)


def _norm_matmul_kernel(x_ref, g_ref, w_ref, o_ref):
    o_ref[...] = _dot(_rms(x_ref[...], g_ref[...]).astype(BF16), w_ref[...]).astype(o_ref.dtype)


def _mem_kv(mem, g, w_bf16):
    bn, m_len, d = mem.shape
    n_out = w_bf16.shape[1]
    return pl.pallas_call(
        _norm_matmul_kernel,
        out_shape=jax.ShapeDtypeStruct((bn, m_len, n_out), BF16),
        grid=(bn,),
        in_specs=[pl.BlockSpec((None, m_len, d), lambda b: (b, 0, 0)),
                  pl.BlockSpec((1, d), lambda b: (0, 0)),
                  pl.BlockSpec((d, n_out), lambda b: (0, 0))],
        out_specs=pl.BlockSpec((None, m_len, n_out), lambda b: (b, 0, 0)),
        compiler_params=_cparams(("parallel",)),
        name="mem_kv",
    )(mem, g.reshape(1, d), w_bf16)


def _xattn_kernel(x_ref, g_ref, wq_ref, kv_ref, wo_ref, out_ref):
    x = x_ref[...]
    q = (_dot(_rms(x, g_ref[...]).astype(BF16), wq_ref[...]) * (XA_DH ** -0.5)).astype(BF16)
    heads = []
    for hd in range(XA_HEADS):
        cols = slice(hd * XA_DH, (hd + 1) * XA_DH)
        s = _nt_dot(q[:, cols], kv_ref[:, cols])
        m = jnp.max(s, axis=-1, keepdims=True)
        p = jnp.exp(s - m)
        den = jnp.sum(p, axis=-1, keepdims=True)
        vcols = slice(D_MODEL + hd * XA_DH, D_MODEL + (hd + 1) * XA_DH)
        heads.append((_dot(p.astype(BF16), kv_ref[:, vcols]) * (1.0 / den)).astype(BF16))
    o = jnp.concatenate(heads, axis=-1)
    out_ref[...] = x + _dot(o, wo_ref[...])


def _xattn(x, g, wq, kv, wo):
    bn, s_len, d = x.shape
    tm = min(ROW_TILE, s_len)
    m_len = kv.shape[1]
    x_spec = pl.BlockSpec((None, tm, d), lambda b, i: (b, i, 0))
    return pl.pallas_call(
        _xattn_kernel,
        out_shape=jax.ShapeDtypeStruct(x.shape, F32),
        grid=(bn, s_len // tm),
        in_specs=[x_spec,
                  pl.BlockSpec((1, d), lambda b, i: (0, 0)),
                  pl.BlockSpec((d, d), lambda b, i: (0, 0)),
                  pl.BlockSpec((None, m_len, 2 * d), lambda b, i: (b, 0, 0)),
                  pl.BlockSpec((d, d), lambda b, i: (0, 0))],
        out_specs=x_spec,
        compiler_params=_cparams(("parallel", "parallel")),
        name="mem_xattn",
    )(x, g.reshape(1, d), wq, kv, wo)


def _mlp_kernel(x_ref, g_ref, w1_ref, w2_ref, gf_ref, out_ref, *, final_norm):
    x = x_ref[...]
    h = _rms(x, g_ref[...]).astype(BF16)
    acc = x
    for c in range(D_FF // D_MODEL):
        cols = slice(c * D_MODEL, (c + 1) * D_MODEL)
        a = jnp.maximum(_dot(h, w1_ref[:, cols]), 0.0)
        acc = acc + _dot((a * a).astype(BF16), w2_ref[cols, :])
    out_ref[...] = _rms(acc, gf_ref[...]) if final_norm else acc


def _mlp(x, g, w1, w2, g_final=None):
    bn, s_len, d = x.shape
    tm = min(ROW_TILE, s_len)
    final_norm = g_final is not None
    gf = (g_final if final_norm else g).reshape(1, d)
    x_spec = pl.BlockSpec((None, tm, d), lambda b, i: (b, i, 0))
    vec = pl.BlockSpec((1, d), lambda b, i: (0, 0))
    return pl.pallas_call(
        functools.partial(_mlp_kernel, final_norm=final_norm),
        out_shape=jax.ShapeDtypeStruct(x.shape, F32),
        grid=(bn, s_len // tm),
        in_specs=[x_spec, vec,
                  pl.BlockSpec((d, D_FF), lambda b, i: (0, 0)),
                  pl.BlockSpec((D_FF, d), lambda b, i: (0, 0)),
                  vec],
        out_specs=x_spec,
        compiler_params=_cparams(("parallel", "parallel")),
        name="mlp_final" if final_norm else "mlp",
    )(x, g.reshape(1, d), w1, w2, gf)


def _od_proj_kernel(x_ref, g_ref, w_ref, ug_ref, hg_ref, slab):
    bn, tq, d = x_ref.shape
    h = _rms(x_ref[...].reshape(bn * tq, d), g_ref[...]).astype(BF16)
    n_ug = 2 * LRU_W
    n_slabs = GROUP_W // LANES
    for c in range(n_ug // GROUP_W):
        acc = _dot(h, w_ref[:, c * GROUP_W:(c + 1) * GROUP_W])
        for b in range(bn):
            for j in range(n_slabs):
                slab[j, pl.ds(b, tq, stride=bn), :] = acc[b * tq:(b + 1) * tq, j * LANES:(j + 1) * LANES]
        for j in range(n_slabs):
            ug_ref[:, c * GROUP_W + j * LANES:c * GROUP_W + (j + 1) * LANES] = slab[j]
    for c in range(hg_ref.shape[-1] // GROUP_W):
        acc = _dot(h, w_ref[:, n_ug + c * GROUP_W:n_ug + (c + 1) * GROUP_W])
        hg_ref[:, :, c * GROUP_W:(c + 1) * GROUP_W] = acc.reshape(bn, tq, GROUP_W)


def _od_proj(x, g, w_bf16):
    bn, s_len, d = x.shape
    tq = min(ROW_TILE // bn, s_len)
    n_ug = 2 * LRU_W
    n_hg = w_bf16.shape[1] - n_ug
    ug, hg = pl.pallas_call(
        _od_proj_kernel,
        out_shape=(jax.ShapeDtypeStruct((s_len * bn, n_ug), F32),
                   jax.ShapeDtypeStruct((bn, s_len, n_hg), F32)),
        grid=(s_len // tq,),
        in_specs=[pl.BlockSpec((bn, tq, d), lambda i: (0, i, 0)),
                  pl.BlockSpec((1, d), lambda i: (0, 0)),
                  pl.BlockSpec((d, n_ug + n_hg), lambda i: (0, 0))],
        out_specs=(pl.BlockSpec((tq * bn, n_ug), lambda i: (i, 0)),
                   pl.BlockSpec((bn, tq, n_hg), lambda i: (0, i, 0))),
        scratch_shapes=[pltpu.VMEM((GROUP_W // LANES, tq * bn, LANES), F32)],
        compiler_params=_cparams(("parallel",)),
        name="od_proj",
    )(x, g.reshape(1, d), w_bf16)
    return ug.reshape(s_len, bn, n_ug), hg


def _gelu_tanh(x):
    return 0.5 * x * (1.0 + jnp.tanh(math.sqrt(2.0 / math.pi) * (x + 0.044715 * (x * x * x))))


def _rglru_kernel(*refs, reverse, n_chunks):
    if reverse:
        (ul_ref, u_ref, ur_ref, gate_ref, hf_ref, cw_ref, cb_ref, wa_ref, ba_ref, wx_ref, bx_ref, sp_ref,
         out_ref, ext, a_s, b_s, h_s, carry) = refs
    else:
        (ul_ref, u_ref, ur_ref, cw_ref, cb_ref, wa_ref, ba_ref, wx_ref, bx_ref, sp_ref,
         out_ref, ext, a_s, b_s, carry) = refs
        h_s = out_ref
    step = pl.program_id(0)
    chunk = (n_chunks - 1 - step) if reverse else step
    t_len, bn, c = u_ref.shape

    @pl.when(step == 0)
    def _():
        carry[...] = jnp.zeros_like(carry)

    ext[0:2] = jnp.where(chunk == 0, 0.0, ul_ref[...])
    ext[2:t_len + 2] = u_ref[...]
    ext[t_len + 2:t_len + 3] = jnp.where(chunk == n_chunks - 1, 0.0, ur_ref[...])
    uc = cb_ref[...].reshape(1, 1, c)
    for j in range(4):
        uc = uc + cw_ref[j:j + 1, :].reshape(1, 1, c) * ext[j:j + t_len]

    u2 = uc.reshape(t_len * bn, c)
    ub = u2.astype(BF16)
    half = c // 2

    def gate(w_ref, b_ref):
        z = jnp.concatenate([_dot(ub[:, :half], w_ref[0]), _dot(ub[:, half:], w_ref[1])], axis=-1)
        return jax.nn.sigmoid(z + b_ref[...])

    r = gate(wa_ref, ba_ref)
    ig = gate(wx_ref, bx_ref)
    log_a = (-LRU_C) * r * sp_ref[...]
    a = jnp.exp(log_a)
    b = jnp.sqrt(1.0 - jnp.exp(2.0 * log_a)) * (ig * u2)
    a_s[...] = a.reshape(t_len, bn, c)
    b_s[...] = b.reshape(t_len, bn, c)

    def body(k, h):
        t = (t_len - 1 - k) if reverse else k
        h = a_s[t] * h + b_s[t]
        h_s[t] = h
        return h

    carry[...] = lax.fori_loop(0, t_len, body, carry[...], unroll=8)

    if reverse:
        out_ref[...] = ((hf_ref[...] + h_s[...]) * _gelu_tanh(gate_ref[...])).astype(out_ref.dtype)


def _rglru(ug, hf, conv_w, conv_b, wa, ba, wx, bx, lam, bn, *, reverse):
    s_len = ug.shape[0]
    c = LRU_W
    ug3 = ug
    t_len = min(LRU_T, s_len)
    n_chunks = s_len // t_len

    def ck(i):
        return (n_chunks - 1 - i) if reverse else i

    half = c // 2
    nb = LRU_BLOCKS // 2

    def dense_halves(w):
        w = w.astype(F32).reshape(2, nb, c // LRU_BLOCKS, c // LRU_BLOCKS)
        eye = jnp.eye(nb, dtype=F32)
        return jnp.einsum('gnij,nm->gnimj', w, eye).reshape(2, half, half).astype(BF16)

    softplus_neg_lam = jax.nn.softplus(-lam.astype(F32)).reshape(1, c)
    vec = pl.BlockSpec((1, c), lambda i: (0, 0))
    wspec = pl.BlockSpec((2, half, half), lambda i: (0, 0, 0))
    blk = pl.BlockSpec((t_len, bn, c), lambda i: (ck(i), 0, 0))
    in_specs = [
        pl.BlockSpec((2, bn, c), lambda i: (jnp.maximum(ck(i) * (t_len // 2) - 1, 0), 0, 0)),
        blk,
        pl.BlockSpec((1, bn, c), lambda i: (jnp.minimum((ck(i) + 1) * t_len, s_len - 1), 0, 0)),
    ]
    args = [ug3, ug3, ug3]
    scratch = [pltpu.VMEM((t_len + 3, bn, c), F32), pltpu.VMEM((t_len, bn, c), F32),
               pltpu.VMEM((t_len, bn, c), F32)]
    if reverse:
        in_specs += [pl.BlockSpec((t_len, bn, c), lambda i: (ck(i), 0, 1)), blk]
        args += [ug3, hf]
        scratch.append(pltpu.VMEM((t_len, bn, c), F32))
    scratch.append(pltpu.VMEM((bn, c), F32))
    in_specs += [pl.BlockSpec((4, c), lambda i: (0, 0)), vec, wspec, vec, wspec, vec, vec]
    args += [conv_w.astype(F32), conv_b.astype(F32).reshape(1, c), dense_halves(wa),
             ba.astype(F32).reshape(1, c), dense_halves(wx), bx.astype(F32).reshape(1, c), softplus_neg_lam]
    return pl.pallas_call(
        functools.partial(_rglru_kernel, reverse=reverse, n_chunks=n_chunks),
        out_shape=jax.ShapeDtypeStruct((s_len, bn, c), BF16 if reverse else F32),
        grid=(n_chunks,),
        in_specs=in_specs,
        out_specs=blk,
        scratch_shapes=scratch,
        compiler_params=_cparams(("arbitrary",)),
        name="rglru_bw" if reverse else "rglru_fw",
    )(*args)


def _hg_matrices(c, reverse):
    n_levels = int(math.log2(c))
    t = np.arange(c)[:, None]
    r = np.arange(c)[None, :]
    mats, masks, upper = [], [], []
    if not reverse:
        mats.append(r <= t)
        mats.append(r > t)
    else:
        mats.append(r >= t)
        mats.append(r < t)
    for lev in range(n_levels):
        half = c >> (lev + 1)
        parent = 2 * half
        start = (np.arange(c) // parent) * parent
        mid = (start + half)[:, None]
        later = (np.arange(c) % parent >= half)[:, None]
        if not reverse:
            m = np.where(later, (r >= mid) & (r <= t), (r > t) & (r < mid))
            is_q = later
        else:
            m = np.where(later, (r >= mid) & (r < t), (r >= t) & (r < mid))
            is_q = ~later
        mats.append(m)
        same_parent = (start[:, None] == start[None, :])
        masks.append(same_parent & is_q & (~is_q).T)
        upper.append(np.broadcast_to(is_q, (c, HG_DK)))
    mat = np.concatenate(mats, axis=0).astype(np.float32)
    return (np.concatenate([mat, mat], axis=1), np.stack(masks).astype(np.float32),
            np.stack(upper).astype(np.float32))


def _hgrn_kernel(*refs, reverse, n_steps, n_levels):
    if reverse:
        (q_ref, f_ref, v_ref, go_ref, of_ref, lb_ref, gn_ref, mat_ref, mask_ref, isq_ref, out_ref,
         state, x_scr, a_scr, aux_scr, dec_scr, kk_scr) = refs
    else:
        (q_ref, f_ref, v_ref, lb_ref, mat_ref, mask_ref, isq_ref, out_ref,
         state, x_scr, a_scr, aux_scr, dec_scr, kk_scr) = refs
    step = pl.program_id(1)
    c = HG_CHUNK
    t_len = q_ref.shape[0]
    n_chunks = t_len // c

    @pl.when(step == 0)
    def _():
        state[...] = jnp.zeros_like(state)

    def chunk_body(ci, carry):
        ch = (n_chunks - 1 - ci) if reverse else ci
        rows = slice(ch * c, (ch + 1) * c)
        for hp in range(HG_HEADS // 2):
            cols2 = slice(2 * hp * HG_DK, 2 * (hp + 1) * HG_DK)
            fl = f_ref[rows, cols2]
            e = jnp.exp(-jnp.abs(fl))
            r = 1.0 / (1.0 + e)
            er = e * r
            pos = fl > 0.0
            lb = lb_ref[0:1, cols2]
            oml = lb_ref[1:2, cols2]
            g2 = jnp.log2(lb + oml * jnp.where(pos, r, er))
            kk_scr[hp] = oml * jnp.where(pos, er, r)
            g_hi = g2.astype(BF16)
            g_lo = (g2 - g_hi.astype(F32)).astype(BF16)
            dec_scr[hp] = jnp.exp2(_dot(mat_ref[...], jnp.concatenate([g_hi, g_lo], axis=0)))

        edge = (c - 1) if not reverse else 0
        for hd in range(HG_HEADS):
            cols = slice(hd * HG_DK, (hd + 1) * HG_DK)
            pcols = slice((hd % 2) * HG_DK, (hd % 2 + 1) * HG_DK)
            kk = kk_scr[hd // 2, :, pcols]
            dec = dec_scr.at[hd // 2]
            qv = q_ref[rows, cols]
            qq = qv * jax.nn.sigmoid(qv)
            qb, kb = qq.astype(BF16), kk.astype(BF16)
            d_in = dec[0:c, pcols]
            x_scr[hd, n_levels] = (qq * d_in).astype(BF16)
            x_scr[hd, n_levels + 1] = (kk * dec[c:2 * c, pcols]).astype(BF16)
            aux_scr[hd, 0] = jnp.broadcast_to(d_in[edge:edge + 1, :], (c, HG_DK))
            aux_scr[hd, 1] = jnp.broadcast_to(jnp.sum(qq * kk, axis=-1, keepdims=True), (c, HG_DK))
            for lev in range(n_levels):
                half = c >> (lev + 1)
                if half >= SUBLANES_BF16:
                    first_is_q = bool(reverse)
                    base = jnp.concatenate(
                        [(qb if (blk % 2 == 1) != first_is_q else kb)[blk * half:(blk + 1) * half]
                         for blk in range(c // half)], axis=0)
                else:
                    base = jnp.where(isq_ref[lev] > 0.5, qb, kb)
                x_scr[hd, lev] = base * dec[(2 + lev) * c:(3 + lev) * c, pcols].astype(BF16)

        for hd in range(HG_HEADS):
            cols = slice(hd * HG_DK, (hd + 1) * HG_DK)
            for lev in range(n_levels):
                xl = x_scr[hd, lev]
                a_scr[hd, lev] = _nt_dot(xl, xl)
            st = state[hd]
            a_scr[hd, n_levels] = _nt_dot(x_scr[hd, n_levels], st.astype(BF16))
            state[hd] = st * aux_scr[hd, 0] + _dot(v_ref[rows, cols].T.astype(BF16), x_scr[hd, n_levels + 1])

        for hd in range(HG_HEADS):
            att = mask_ref[0] * a_scr[hd, 0]
            for lev in range(1, n_levels):
                att = att + mask_ref[lev] * a_scr[hd, lev]
            x_scr[hd, 0] = att.astype(BF16)

        for hd in range(HG_HEADS):
            cols = slice(hd * HG_DK, (hd + 1) * HG_DK)
            v = v_ref[rows, cols]
            o = a_scr[hd, n_levels] + _dot(x_scr[hd, 0], v.astype(BF16)) + aux_scr[hd, 1] * v

            if reverse:
                o = o + of_ref[rows, cols]
                o = o * lax.rsqrt(jnp.mean(o * o, axis=-1, keepdims=True) + EPS) * gn_ref[...]
                gv = go_ref[rows, cols]
                out_ref[rows, cols] = (o * (gv * jax.nn.sigmoid(gv))).astype(out_ref.dtype)
            else:
                out_ref[rows, cols] = o
        return carry

    for ci in range(n_chunks):
        chunk_body(ci, 0)


def _hgrn(hg, o_fw, lb, gnorm_g, *, reverse):
    bn, s_len, _ = hg.shape
    w = HG_HEADS * HG_DK
    t_len = min(HG_T, s_len)
    n_steps = s_len // t_len
    mats, masks, isq = _hg_matrices(HG_CHUNK, reverse)
    n_levels = masks.shape[0]
    lbf = lb.astype(F32)
    lb_tab = jnp.stack([lbf, 1.0 - lbf])

    def blk(g):
        return pl.BlockSpec((None, t_len, w), lambda b, i: (b, (n_steps - 1 - i) if reverse else i, g))

    def const(shape):
        return pl.BlockSpec(shape, lambda b, i: (0,) * len(shape))

    in_specs = [blk(0), blk(2 if reverse else 1), blk(3)]
    args = [hg, hg, hg]
    if reverse:
        in_specs += [blk(4), blk(0)]
        args += [hg, o_fw]
    in_specs.append(const((2, w)))
    args.append(lb_tab)
    if reverse:
        in_specs.append(const((1, HG_DK)))
        args.append(gnorm_g.astype(F32).reshape(1, HG_DK))
    in_specs += [const(mats.shape), const(masks.shape), const(isq.shape)]
    args += [jnp.asarray(mats, BF16), jnp.asarray(masks, F32), jnp.asarray(isq, BF16)]
    return pl.pallas_call(
        functools.partial(_hgrn_kernel, reverse=reverse, n_steps=n_steps, n_levels=n_levels),
        out_shape=jax.ShapeDtypeStruct((bn, s_len, w), BF16 if reverse else F32),
        grid=(bn, n_steps),
        in_specs=in_specs,
        out_specs=blk(0),
        scratch_shapes=[pltpu.VMEM((HG_HEADS, HG_DK, HG_DK), F32),
                        pltpu.VMEM((HG_HEADS, n_levels + 2, HG_CHUNK, HG_DK), BF16),
                        pltpu.VMEM((HG_HEADS, n_levels + 1, HG_CHUNK, HG_CHUNK), F32),
                        pltpu.VMEM((HG_HEADS, 2, HG_CHUNK, HG_DK), F32),
                        pltpu.VMEM((HG_HEADS // 2, (n_levels + 2) * HG_CHUNK, 2 * HG_DK), F32),
                        pltpu.VMEM((HG_HEADS // 2, HG_CHUNK, 2 * HG_DK), F32)],
        compiler_params=_cparams(("parallel", "arbitrary")),
        name="hgrn_bw" if reverse else "hgrn_fw",
    )(*args)


def _od_out_kernel(yc_ref, yd_ref, w_ref, x_ref, out_ref):
    half = yc_ref.shape[-1]
    out_ref[...] = x_ref[...] + _dot(yc_ref[...], w_ref[0:half, :]) + _dot(yd_ref[...], w_ref[half:, :])


def _od_out(x, yc_tm, yd, w_bf16):
    bn, s_len, d = x.shape
    tm = min(ROW_TILE, s_len)
    c = yd.shape[-1]
    x_spec = pl.BlockSpec((None, tm, d), lambda b, i: (b, i, 0))
    return pl.pallas_call(
        _od_out_kernel,
        out_shape=jax.ShapeDtypeStruct(x.shape, F32),
        grid=(bn, s_len // tm),
        in_specs=[pl.BlockSpec((tm, c), lambda b, i: (i, b)),
                  pl.BlockSpec((None, tm, c), lambda b, i: (b, i, 0)),
                  pl.BlockSpec((2 * c, d), lambda b, i: (0, 0)),
                  x_spec],
        out_specs=x_spec,
        compiler_params=_cparams(("parallel", "parallel")),
        name="od_out",
    )(yc_tm, yd, w_bf16, x)


def _even_layer(x, g, w_in, w_out, rpb):
    qkv, view4, view16 = _ev_proj(x, g, w_in.astype(BF16))
    outs, lses = [], []
    for dil, view in zip(DILATIONS, (qkv, view4, view16)):
        o, lse = _dilated_branch(view, dil)
        outs.append(o)
        lses.append(lse)
    yb = _neighborhood(qkv, rpb)
    return _ev_out(x, outs, lses, yb, w_out.astype(BF16))


def _odd_layer(x, g, w_in, w_out, conv_w, conv_b, wa, ba, wx, bx, lam, lb_f, lb_b, gnorm_g):
    bn, s_len, _ = x.shape
    ug, hg = _od_proj(x, g, w_in.astype(BF16))
    h_fw = _rglru(ug, None, conv_w, conv_b, wa[0], ba[0], wx[0], bx[0], lam[0], bn, reverse=False)
    yc = _rglru(ug, h_fw, conv_w, conv_b, wa[1], ba[1], wx[1], bx[1], lam[1], bn, reverse=True)
    o_fw = _hgrn(hg, None, lb_f, gnorm_g, reverse=False)
    yd = _hgrn(hg, o_fw, lb_b, gnorm_g, reverse=True)
    return _od_out(x, yc.reshape(s_len, bn * LRU_W), yd, w_out.astype(BF16))


def kernel(x, mem, norm_mix_g, norm_xa_g, norm_mem_g, norm_mlp_g, final_norm_g, ev_w_in, ev_w_out, na_rpb,
           od_w_in, od_w_out, conv_w, conv_b, lru_wa, lru_ba, lru_wx, lru_bx, lru_lambda, hgrn_lb_logits,
           hgrn_norm_g, xa_wq, xa_wkv, xa_wo, mlp_w1, mlp_w2):
    depth = norm_mix_g.shape[0]
    p_lb = jax.nn.softmax(hgrn_lb_logits.astype(F32), axis=0)
    lower_bounds = jnp.cumsum(p_lb, axis=0) - p_lb[0:1]
    for layer in range(depth):
        if layer % 2 == 0:
            e = layer // 2
            x = _even_layer(x, norm_mix_g[layer], ev_w_in[e], ev_w_out[e], na_rpb[e])
        else:
            o = layer // 2
            x = _odd_layer(x, norm_mix_g[layer], od_w_in[o], od_w_out[o], conv_w[o], conv_b[o],
                           lru_wa[o], lru_ba[o], lru_wx[o], lru_bx[o], lru_lambda[o],
                           lower_bounds[layer, 0], lower_bounds[layer, 1], hgrn_norm_g[o])
        kv = _mem_kv(mem, norm_mem_g[layer], xa_wkv[layer].astype(BF16))
        x = _xattn(x, norm_xa_g[layer], xa_wq[layer].astype(BF16), kv, xa_wo[layer].astype(BF16))
        x = _mlp(x, norm_mlp_g[layer], mlp_w1[layer].astype(BF16), mlp_w2[layer].astype(BF16),
                 g_final=final_norm_g if layer == depth - 1 else None)
    return x
```

```python
import functools
import math

import jax
import jax.numpy as jnp
import numpy as np
from jax import lax
from jax.experimental import pallas as pl
from jax.experimental.pallas import tpu as pltpu

F32 = jnp.float32
BF16 = jnp.bfloat16

D_MODEL = 1024
HEAD_DIM = 64
ROT_DIM = 16
ROPE_THETA = 500000.0
N_HEADS_A = 8
N_HEADS_B = 8
GROUP_W = N_HEADS_A * HEAD_DIM
DILATIONS = (1, 4, 16)
WIN_HALF = 64
GRID_W = 64
NA_ROWS = 8
NA_COLS = 16
LRU_W = 512
LRU_BLOCKS = 8
LRU_C = 8.0
HG_HEADS = 4
HG_DK = 128
XA_HEADS = 4
XA_DH = 256
D_FF = 4096
EPS = 1e-6
LOG2E = math.log2(math.e)

LANES = 128
SUBLANES_BF16 = 16
V7X_VMEM_LIMIT_BYTES = 56 * 1024 * 1024

ROW_TILE = 512
ROW_TILE_WIDE = 1024
ATT_Q_BLOCK = 512
ATT_SUB = 128
NA_ROW_GROUP = 8
LRU_T = 128
HG_CHUNK = 128
HG_T = 512


def _cparams(sem):
    return pltpu.CompilerParams(dimension_semantics=sem, vmem_limit_bytes=V7X_VMEM_LIMIT_BYTES)


def _resident(shape, index_map):
    return pl.BlockSpec(shape, index_map, pipeline_mode=pl.Buffered(1))


def _rms(x, g):
    return x * lax.rsqrt(jnp.mean(x * x, axis=-1, keepdims=True) + EPS) * g


def _nt_dot(a, b):
    return lax.dot_general(a, b, (((1,), (1,)), ((), ())), preferred_element_type=F32)


def _dot(a, b):
    return jnp.dot(a, b, preferred_element_type=F32)


def _head_selectors():
    lane = lax.broadcasted_iota(jnp.int32, (1, LANES), 1)
    low = (lane < HEAD_DIM).astype(F32)
    return low.astype(BF16), (1.0 - low).astype(BF16)


def _ev_proj_kernel(x_ref, g_ref, w_ref, rc_ref, rs1_ref, rs2_ref, o_ref, o4_ref, o16_ref, slab):
    h = _rms(x_ref[...], g_ref[...]).astype(BF16)
    scale = HEAD_DIM ** -0.5 * LOG2E
    tm = x_ref.shape[0]
    n_slabs = GROUP_W // LANES
    for c in range(6):
        acc = _dot(h, w_ref[:, c * GROUP_W:(c + 1) * GROUP_W])
        if c in (0, 1):
            rc, rs1, rs2 = rc_ref[...], rs1_ref[...], rs2_ref[...]
            parts = []
            for j in range(n_slabs):
                t = acc[:, j * LANES:(j + 1) * LANES]
                t = t * rc + pltpu.roll(t, LANES - ROT_DIM // 2, 1) * rs1 + pltpu.roll(t, ROT_DIM // 2, 1) * rs2
                parts.append(t)
            acc = jnp.concatenate(parts, axis=-1)
        if c in (0, 3):
            acc = acc * scale
        o_ref[:, c * GROUP_W:(c + 1) * GROUP_W] = acc.astype(BF16)
        if c < 3:
            for j in range(n_slabs):
                slab[j] = acc[:, j * LANES:(j + 1) * LANES]
            for dil, dst in ((DILATIONS[1], o4_ref), (DILATIONS[2], o16_ref)):
                for r in range(dil):
                    for j in range(n_slabs):
                        col = r * 3 * GROUP_W + c * GROUP_W + j * LANES
                        dst[:, col:col + LANES] = slab[j, pl.ds(r, tm // dil, stride=dil), :].astype(BF16)


def _rope_tables(s_len):
    half = ROT_DIM // 2
    inv = jnp.asarray(ROPE_THETA ** (-np.arange(half) * 2.0 / ROT_DIM), F32)
    ang = jnp.arange(s_len, dtype=F32)[:, None] * inv[None, :]
    cos, sin = jnp.cos(ang), jnp.sin(ang)
    ones = jnp.ones((s_len, HEAD_DIM - ROT_DIM), F32)
    zeros = jnp.zeros((s_len, HEAD_DIM - ROT_DIM), F32)
    zh = jnp.zeros((s_len, half), F32)
    rc = jnp.concatenate([cos, cos, ones], axis=-1)
    rs1 = jnp.concatenate([-sin, zh, zeros], axis=-1)
    rs2 = jnp.concatenate([zh, sin, zeros], axis=-1)
    rep = LANES // HEAD_DIM
    return jnp.tile(rc, (1, rep)), jnp.tile(rs1, (1, rep)), jnp.tile(rs2, (1, rep))


def _ev_proj(x, g, w_bf16):
    bn, s_len, d = x.shape
    tm = min(ROW_TILE, s_len)
    n_out = w_bf16.shape[1]
    rc, rs1, rs2 = _rope_tables(s_len)
    tbl_spec = pl.BlockSpec((tm, LANES), lambda b, i: (i, 0))
    wa = 3 * GROUP_W
    d4, d16 = DILATIONS[1], DILATIONS[2]
    return pl.pallas_call(
        _ev_proj_kernel,
        out_shape=(jax.ShapeDtypeStruct((bn, s_len, n_out), BF16),
                   jax.ShapeDtypeStruct((bn, s_len // d4, d4 * wa), BF16),
                   jax.ShapeDtypeStruct((bn, s_len // d16, d16 * wa), BF16)),
        grid=(bn, s_len // tm),
        in_specs=[
            pl.BlockSpec((None, tm, d), lambda b, i: (b, i, 0)),
            pl.BlockSpec((1, d), lambda b, i: (0, 0)),
            pl.BlockSpec((d, n_out), lambda b, i: (0, 0)),
            tbl_spec, tbl_spec, tbl_spec,
        ],
        out_specs=(pl.BlockSpec((None, tm, n_out), lambda b, i: (b, i, 0)),
                   pl.BlockSpec((None, tm // d4, d4 * wa), lambda b, i: (b, i, 0)),
                   pl.BlockSpec((None, tm // d16, d16 * wa), lambda b, i: (b, i, 0))),
        scratch_shapes=[pltpu.VMEM((GROUP_W // LANES, tm, LANES), F32)],
        compiler_params=_cparams(("parallel", "parallel")),
        name="ev_proj",
    )(x, g.reshape(1, d), w_bf16, rc, rs1, rs2)


def _dilated_kernel(q_ref, kl_ref, kc_ref, kr_ref, vl_ref, vc_ref, vr_ref, o_ref, max_ref, den_ref,
                    kext, vext, s_scr, p_scr, *, lq, l_total):
    i = pl.program_id(2)
    kext[0:WIN_HALF, :] = kl_ref[...]
    kext[WIN_HALF:WIN_HALF + lq, :] = kc_ref[...]
    kext[WIN_HALF + lq:, :] = kr_ref[...]
    vext[0:WIN_HALF, :] = vl_ref[...]
    vext[WIN_HALF:WIN_HALF + lq, :] = vc_ref[...]
    vext[WIN_HALF + lq:, :] = vr_ref[...]

    wk = ATT_SUB + 2 * WIN_HALF
    qi = lax.broadcasted_iota(jnp.int32, (ATT_SUB, wk), 0)
    ci = lax.broadcasted_iota(jnp.int32, (ATT_SUB, wk), 1)
    band_bias = jnp.where((ci - qi >= 0) & (ci - qi <= 2 * WIN_HALF), 0.0, -jnp.inf).astype(F32)
    crow = lax.broadcasted_iota(jnp.int32, (1, wk), 1)
    lane = lax.broadcasted_iota(jnp.int32, (ATT_SUB, LANES), 1)
    low_half = lane < HEAD_DIM
    sel_lo, sel_hi = _head_selectors()
    n_sub = lq // ATT_SUB
    n_hp = GROUP_W // LANES

    for j in range(n_sub):
        base = i * lq + j * ATT_SUB - WIN_HALF
        in_seq = (crow + base >= 0) & (crow + base < l_total)
        bias = band_bias + jnp.where(in_seq, 0.0, -jnp.inf).astype(F32)
        bias2 = jnp.concatenate([bias, bias], axis=0)
        rows = slice(j * ATT_SUB, (j + 1) * ATT_SUB)
        wrows = slice(j * ATT_SUB, j * ATT_SUB + wk)
        for hp in range(n_hp):
            cols = slice(hp * LANES, (hp + 1) * LANES)
            qp = q_ref[rows, cols]
            q2 = jnp.concatenate([qp * sel_lo, qp * sel_hi], axis=0)
            s_scr[j * n_hp + hp] = _nt_dot(q2, kext[wrows, cols]) + bias2
    for j in range(n_sub):
        m_tile = jnp.zeros((ATT_SUB, LANES), F32)
        den_tile = jnp.ones((ATT_SUB, LANES), F32)
        for hp in range(n_hp):
            t = j * n_hp + hp
            s = s_scr[t]
            m = jnp.max(s, axis=-1, keepdims=True)
            p = jnp.exp2(s - m)
            den = jnp.sum(p, axis=-1, keepdims=True)
            p_scr[t] = p.astype(BF16)
            is_lo, is_hi = lane == 2 * hp, lane == 2 * hp + 1
            m_tile = jnp.where(is_lo, m[0:ATT_SUB], jnp.where(is_hi, m[ATT_SUB:], m_tile))
            den_tile = jnp.where(is_lo, den[0:ATT_SUB], jnp.where(is_hi, den[ATT_SUB:], den_tile))
        max_ref[j * ATT_SUB:(j + 1) * ATT_SUB, :] = m_tile
        den_ref[j * ATT_SUB:(j + 1) * ATT_SUB, :] = den_tile
    for j in range(n_sub):
        rows = slice(j * ATT_SUB, (j + 1) * ATT_SUB)
        wrows = slice(j * ATT_SUB, j * ATT_SUB + wk)
        for hp in range(n_hp):
            cols = slice(hp * LANES, (hp + 1) * LANES)
            o = _dot(p_scr[j * n_hp + hp], vext[wrows, cols])
            o_ref[rows, cols] = jnp.where(low_half, o[0:ATT_SUB], o[ATT_SUB:]).astype(BF16)


def _dilated_branch(view, dil):
    bn, l_total, width = view.shape
    n_groups = width // (dil * GROUP_W)
    lq = min(ATT_Q_BLOCK, l_total)
    nblk = l_total // lq
    per = lq // WIN_HALF
    n_halo = l_total // WIN_HALF
    n_units = (lq // ATT_SUB) * (GROUP_W // LANES)

    def cur(g):
        return pl.BlockSpec((None, lq, GROUP_W), lambda b, r, i: (b, i, r * n_groups + g))

    def left(g):
        return pl.BlockSpec((None, WIN_HALF, GROUP_W),
                            lambda b, r, i: (b, jnp.maximum(i * per - 1, 0), r * n_groups + g))

    def right(g):
        return pl.BlockSpec((None, WIN_HALF, GROUP_W),
                            lambda b, r, i: (b, jnp.minimum((i + 1) * per, n_halo - 1), r * n_groups + g))

    stat_shape = jax.ShapeDtypeStruct((bn, l_total, dil * LANES), F32)
    stat_spec = pl.BlockSpec((None, lq, LANES), lambda b, r, i: (b, i, r))
    return pl.pallas_call(
        functools.partial(_dilated_kernel, lq=lq, l_total=l_total),
        out_shape=(jax.ShapeDtypeStruct((bn, l_total, dil * GROUP_W), BF16), stat_shape, stat_shape),
        grid=(bn, dil, nblk),
        in_specs=[cur(0), left(1), cur(1), right(1), left(2), cur(2), right(2)],
        out_specs=(pl.BlockSpec((None, lq, GROUP_W), lambda b, r, i: (b, i, r)), stat_spec, stat_spec),
        scratch_shapes=[pltpu.VMEM((lq + 2 * WIN_HALF, GROUP_W), BF16),
                        pltpu.VMEM((lq + 2 * WIN_HALF, GROUP_W), BF16),
                        pltpu.VMEM((n_units, 2 * ATT_SUB, ATT_SUB + 2 * WIN_HALF), F32),
                        pltpu.VMEM((n_units, 2 * ATT_SUB, ATT_SUB + 2 * WIN_HALF), BF16)],
        compiler_params=_cparams(("parallel", "parallel", "parallel")),
        name=f"dilated_attn_d{dil}",
    )(view, view, view, view, view, view, view)


NA_QT = NA_ROW_GROUP * GRID_W
NA_KT = 2 * NA_QT
NA_KW = NA_ROWS * GRID_W


def _na_tables(rpb):
    rpb = rpb.astype(F32)
    nh = rpb.shape[0]
    ext = GRID_W - NA_COLS
    padded = jnp.concatenate([jnp.repeat(rpb[..., :1], ext, axis=-1), rpb,
                              jnp.repeat(rpb[..., -1:], ext, axis=-1)], axis=-1)
    t1 = jnp.stack([padded[..., GRID_W - 1 - qc:2 * GRID_W - 1 - qc] for qc in range(GRID_W)], axis=2)
    qc = np.arange(GRID_W)[:, None]
    kc = np.arange(GRID_W)[None, :]
    c0 = np.clip(qc - NA_COLS // 2, 0, GRID_W - NA_COLS)
    t1 = jnp.where((kc >= c0) & (kc < c0 + NA_COLS), t1 * LOG2E, -jnp.inf)
    per_delta = [jnp.concatenate([t1[:, kr - delta + NA_ROWS - 1] for kr in range(NA_ROWS)], axis=-1)
                 for delta in range(NA_ROWS)]
    return jnp.stack(per_delta).reshape(NA_ROWS, nh // 2, 2 * GRID_W, NA_KW)


def _na_kernel(q_ref, k_ref, v_ref, tb_ref, o_ref, s_scr, p_scr, inv_scr, *, rows):
    i0 = pl.program_id(1) * NA_ROW_GROUP
    w0 = jnp.clip(i0 - NA_ROWS // 2, 0, rows - 2 * NA_ROW_GROUP)
    lane = lax.broadcasted_iota(jnp.int32, (GRID_W, LANES), 1)
    low_half = lane < HEAD_DIM
    sel_lo, sel_hi = _head_selectors()
    n_hp = GROUP_W // LANES

    def key_rows(a):
        r0 = jnp.clip(i0 + a - NA_ROWS // 2, 0, rows - NA_ROWS)
        return pl.ds(pl.multiple_of((r0 - w0) * GRID_W, GRID_W), NA_KW), i0 + a - r0

    for a in range(NA_ROW_GROUP):
        krows, delta = key_rows(a)
        qrows = slice(a * GRID_W, (a + 1) * GRID_W)
        for hp in range(n_hp):
            cols = slice(hp * LANES, (hp + 1) * LANES)
            qp = q_ref[qrows, cols]
            q2 = jnp.concatenate([qp * sel_lo, qp * sel_hi], axis=0)
            s_scr[a * n_hp + hp] = _nt_dot(q2, k_ref[0, krows, cols]) + tb_ref[delta, hp]
    for t in range(NA_ROW_GROUP * n_hp):
        s = s_scr[t]
        p = jnp.exp2(s - jnp.max(s, axis=-1, keepdims=True))
        inv_scr[t] = jnp.broadcast_to(1.0 / jnp.sum(p, axis=-1, keepdims=True), (2 * GRID_W, LANES))
        p_scr[t] = p.astype(BF16)
    for a in range(NA_ROW_GROUP):
        krows, _ = key_rows(a)
        qrows = slice(a * GRID_W, (a + 1) * GRID_W)
        for hp in range(n_hp):
            cols = slice(hp * LANES, (hp + 1) * LANES)
            t = a * n_hp + hp
            o = _dot(p_scr[t], v_ref[0, krows, cols]) * inv_scr[t]
            o_ref[qrows, cols] = jnp.where(low_half, o[0:GRID_W], o[GRID_W:]).astype(BF16)


def _neighborhood(qkv, rpb):
    bn, s_len, _ = qkv.shape
    rows = s_len // GRID_W
    n_groups = rows // NA_ROW_GROUP
    tables = _na_tables(rpb)

    def window(g):
        return pl.BlockSpec(
            (pl.Element(1), pl.Element(NA_KT), pl.Element(GROUP_W)),
            lambda b, i: (b, jnp.clip(i * NA_ROW_GROUP - NA_ROWS // 2, 0, rows - 2 * NA_ROW_GROUP) * GRID_W,
                          g * GROUP_W))

    return pl.pallas_call(
        functools.partial(_na_kernel, rows=rows),
        out_shape=jax.ShapeDtypeStruct((bn, s_len, GROUP_W), BF16),
        grid=(bn, n_groups),
        in_specs=[pl.BlockSpec((None, NA_QT, GROUP_W), lambda b, i: (b, i, 3)), window(4), window(5),
                  pl.BlockSpec(tables.shape, lambda b, i: (0, 0, 0, 0))],
        out_specs=pl.BlockSpec((None, NA_QT, GROUP_W), lambda b, i: (b, i, 0)),
        scratch_shapes=[pltpu.VMEM((NA_ROW_GROUP * GROUP_W // LANES, 2 * GRID_W, NA_KW), F32),
                        pltpu.VMEM((NA_ROW_GROUP * GROUP_W // LANES, 2 * GRID_W, NA_KW), BF16),
                        pltpu.VMEM((NA_ROW_GROUP * GROUP_W // LANES, 2 * GRID_W, LANES), F32)],
        compiler_params=_cparams(("parallel", "parallel")),
        name="neighborhood_attn",
    )(qkv, qkv, qkv, tables)


def _ev_out_kernel(o1, o2v, o3v, m1, m2v, m3v, d1, d2v, d3v, yb_ref, ex_ref, w_ref, x_ref, out_ref,
                   so2, so3, st2, st3):
    tm = x_ref.shape[0]
    n_slabs = GROUP_W // LANES
    for dil, ov, mv, dv, so, st in ((DILATIONS[1], o2v, m2v, d2v, so2, st2), (DILATIONS[2], o3v, m3v, d3v, so3, st3)):
        for r in range(dil):
            dst = pl.ds(r, tm // dil, stride=dil)
            for j in range(n_slabs):
                col = r * GROUP_W + j * LANES
                so[j, dst, :] = ov[:, col:col + LANES].astype(F32)
            st[0, dst, :] = mv[:, r * LANES:(r + 1) * LANES]
            st[1, dst, :] = dv[:, r * LANES:(r + 1) * LANES]
    o2 = jnp.concatenate([so2[j] for j in range(n_slabs)], axis=-1)
    o3 = jnp.concatenate([so3[j] for j in range(n_slabs)], axis=-1)
    ma, mb, mc = m1[...], st2[0], st3[0]
    m = jnp.maximum(jnp.maximum(ma, mb), mc)
    ea, eb, ec = jnp.exp2(ma - m), jnp.exp2(mb - m), jnp.exp2(mc - m)
    inv = 1.0 / (ea * d1[...] + eb * st2[1] + ec * st3[1])
    ex = ex_ref[...]

    def expand(w):
        hi = w.astype(BF16)
        lo = (w - hi.astype(F32)).astype(BF16)
        return _dot(hi, ex) + _dot(lo, ex)

    ya = expand(ea * inv) * o1[...].astype(F32) + expand(eb * inv) * o2 + expand(ec * inv) * o3
    y = _dot(ya.astype(BF16), w_ref[0:GROUP_W, :]) + _dot(yb_ref[...], w_ref[GROUP_W:, :])
    out_ref[...] = x_ref[...] + y


def _ev_out(x, outs, maxes, dens, yb, w_bf16):
    bn, s_len, d = x.shape
    tm = min(ROW_TILE, s_len)
    ex = np.zeros((LANES, GROUP_W), np.float32)
    for h in range(N_HEADS_A):
        ex[h, h * HEAD_DIM:(h + 1) * HEAD_DIM] = 1.0
    def o_spec(dil):
        return pl.BlockSpec((None, tm // dil, dil * GROUP_W), lambda b, i: (b, i, 0))

    def l_spec(dil):
        return pl.BlockSpec((None, tm // dil, dil * LANES), lambda b, i: (b, i, 0))

    x_spec = pl.BlockSpec((None, tm, d), lambda b, i: (b, i, 0))
    n_slabs = GROUP_W // LANES
    return pl.pallas_call(
        _ev_out_kernel,
        out_shape=jax.ShapeDtypeStruct(x.shape, F32),
        grid=(bn, s_len // tm),
        in_specs=[o_spec(d_) for d_ in DILATIONS] + 2 * [l_spec(d_) for d_ in DILATIONS]
        + [o_spec(1),
           pl.BlockSpec((LANES, GROUP_W), lambda b, i: (0, 0)),
           pl.BlockSpec((2 * GROUP_W, d), lambda b, i: (0, 0)),
           x_spec],
        out_specs=x_spec,
        scratch_shapes=[pltpu.VMEM((n_slabs, tm, LANES), F32), pltpu.VMEM((n_slabs, tm, LANES), F32),
                        pltpu.VMEM((2, tm, LANES), F32), pltpu.VMEM((2, tm, LANES), F32)],
        compiler_params=_cparams(("parallel", "parallel")),
        name="ev_out",
    )(*outs, *maxes, *dens, yb, jnp.asarray(ex, BF16), w_bf16, x)


def _norm_matmul_kernel(x_ref, g_ref, w_ref, o_ref):
    o_ref[...] = _dot(_rms(x_ref[...], g_ref[...]).astype(BF16), w_ref[...]).astype(o_ref.dtype)


def _mem_kv(mem, g, w_bf16):
    bn, m_len, d = mem.shape
    n_out = w_bf16.shape[1]
    return pl.pallas_call(
        _norm_matmul_kernel,
        out_shape=jax.ShapeDtypeStruct((bn, m_len, n_out), BF16),
        grid=(bn,),
        in_specs=[pl.BlockSpec((None, m_len, d), lambda b: (b, 0, 0)),
                  pl.BlockSpec((1, d), lambda b: (0, 0)),
                  pl.BlockSpec((d, n_out), lambda b: (0, 0))],
        out_specs=pl.BlockSpec((None, m_len, n_out), lambda b: (b, 0, 0)),
        compiler_params=_cparams(("parallel",)),
        name="mem_kv",
    )(mem, g.reshape(1, d), w_bf16)


def _xattn_kernel(x_ref, g_ref, wq_ref, kv_ref, wo_ref, out_ref):
    x = x_ref[...]
    q = (_dot(_rms(x, g_ref[...]).astype(BF16), wq_ref[...]) * (XA_DH ** -0.5 * LOG2E)).astype(BF16)
    heads = []
    for hd in range(XA_HEADS):
        cols = slice(hd * XA_DH, (hd + 1) * XA_DH)
        s = _nt_dot(q[:, cols], kv_ref[:, cols])
        m = jnp.max(s, axis=-1, keepdims=True)
        p = jnp.exp2(s - m)
        den = jnp.sum(p, axis=-1, keepdims=True)
        vcols = slice(D_MODEL + hd * XA_DH, D_MODEL + (hd + 1) * XA_DH)
        heads.append((_dot(p.astype(BF16), kv_ref[:, vcols]) * (1.0 / den)).astype(BF16))
    o = jnp.concatenate(heads, axis=-1)
    out_ref[...] = x + _dot(o, wo_ref[...])


def _xattn(x, g, wq, kv, wo):
    bn, s_len, d = x.shape
    tm = min(ROW_TILE_WIDE, s_len)
    m_len = kv.shape[1]
    x_spec = pl.BlockSpec((None, tm, d), lambda b, i: (b, i, 0))
    return pl.pallas_call(
        _xattn_kernel,
        out_shape=jax.ShapeDtypeStruct(x.shape, F32),
        grid=(bn, s_len // tm),
        in_specs=[x_spec,
                  pl.BlockSpec((1, d), lambda b, i: (0, 0)),
                  pl.BlockSpec((d, d), lambda b, i: (0, 0)),
                  pl.BlockSpec((None, m_len, 2 * d), lambda b, i: (b, 0, 0)),
                  pl.BlockSpec((d, d), lambda b, i: (0, 0))],
        out_specs=x_spec,
        compiler_params=_cparams(("parallel", "parallel")),
        name="mem_xattn",
    )(x, g.reshape(1, d), wq, kv, wo)


def _mlp_kernel(x_ref, g_ref, w1_ref, w2_ref, gf_ref, out_ref, *, final_norm):
    x = x_ref[...]
    h = _rms(x, g_ref[...]).astype(BF16)
    acc = x
    for c in range(D_FF // D_MODEL):
        cols = slice(c * D_MODEL, (c + 1) * D_MODEL)
        a = jnp.maximum(_dot(h, w1_ref[:, cols]), 0.0)
        acc = acc + _dot((a * a).astype(BF16), w2_ref[cols, :])
    out_ref[...] = _rms(acc, gf_ref[...]) if final_norm else acc


def _mlp(x, g, w1, w2, g_final=None):
    bn, s_len, d = x.shape
    tm = min(ROW_TILE_WIDE, s_len)
    final_norm = g_final is not None
    gf = (g_final if final_norm else g).reshape(1, d)
    x_spec = pl.BlockSpec((None, tm, d), lambda b, i: (b, i, 0))
    vec = pl.BlockSpec((1, d), lambda b, i: (0, 0))
    return pl.pallas_call(
        functools.partial(_mlp_kernel, final_norm=final_norm),
        out_shape=jax.ShapeDtypeStruct(x.shape, F32),
        grid=(bn, s_len // tm),
        in_specs=[x_spec, vec,
                  _resident((d, D_FF), lambda b, i: (0, 0)),
                  _resident((D_FF, d), lambda b, i: (0, 0)),
                  vec],
        out_specs=x_spec,
        compiler_params=_cparams(("parallel", "parallel")),
        name="mlp_final" if final_norm else "mlp",
    )(x, g.reshape(1, d), w1, w2, gf)


def _od_proj_kernel(x_ref, g_ref, w_ref, ug_ref, hg_ref, slab):
    bn, tq, d = x_ref.shape
    h = _rms(x_ref[...].reshape(bn * tq, d), g_ref[...]).astype(BF16)
    n_ug = 2 * LRU_W
    n_slabs = GROUP_W // LANES
    for c in range(n_ug // GROUP_W):
        acc = _dot(h, w_ref[:, c * GROUP_W:(c + 1) * GROUP_W])
        for b in range(bn):
            for j in range(n_slabs):
                slab[j, pl.ds(b, tq, stride=bn), :] = acc[b * tq:(b + 1) * tq, j * LANES:(j + 1) * LANES]
        for j in range(n_slabs):
            ug_ref[:, c * GROUP_W + j * LANES:c * GROUP_W + (j + 1) * LANES] = slab[j]
    for c in range(hg_ref.shape[-1] // GROUP_W):
        acc = _dot(h, w_ref[:, n_ug + c * GROUP_W:n_ug + (c + 1) * GROUP_W])
        hg_ref[:, :, c * GROUP_W:(c + 1) * GROUP_W] = acc.reshape(bn, tq, GROUP_W)


def _od_proj(x, g, w_bf16):
    bn, s_len, d = x.shape
    tq = min(ROW_TILE // bn, s_len)
    n_ug = 2 * LRU_W
    n_hg = w_bf16.shape[1] - n_ug
    ug, hg = pl.pallas_call(
        _od_proj_kernel,
        out_shape=(jax.ShapeDtypeStruct((s_len * bn, n_ug), F32),
                   jax.ShapeDtypeStruct((bn, s_len, n_hg), F32)),
        grid=(s_len // tq,),
        in_specs=[pl.BlockSpec((bn, tq, d), lambda i: (0, i, 0)),
                  pl.BlockSpec((1, d), lambda i: (0, 0)),
                  pl.BlockSpec((d, n_ug + n_hg), lambda i: (0, 0))],
        out_specs=(pl.BlockSpec((tq * bn, n_ug), lambda i: (i, 0)),
                   pl.BlockSpec((bn, tq, n_hg), lambda i: (0, i, 0))),
        scratch_shapes=[pltpu.VMEM((GROUP_W // LANES, tq * bn, LANES), F32)],
        compiler_params=_cparams(("parallel",)),
        name="od_proj",
    )(x, g.reshape(1, d), w_bf16)
    return ug.reshape(s_len, bn, n_ug), hg


def _gelu_tanh(x):
    return 0.5 * x * (1.0 + jnp.tanh(math.sqrt(2.0 / math.pi) * (x + 0.044715 * (x * x * x))))


def _rglru_kernel(*refs, reverse, n_chunks):
    if reverse:
        (ul_ref, u_ref, ur_ref, gate_ref, hf_ref, cw_ref, cb_ref, wa_ref, ba_ref, wx_ref, bx_ref, sp_ref,
         out_ref, ext, a_s, b_s, h_s, carry) = refs
    else:
        (ul_ref, u_ref, ur_ref, cw_ref, cb_ref, wa_ref, ba_ref, wx_ref, bx_ref, sp_ref,
         out_ref, ext, a_s, b_s, carry) = refs
        h_s = out_ref
    step = pl.program_id(0)
    chunk = (n_chunks - 1 - step) if reverse else step
    t_len, bn, c = u_ref.shape

    @pl.when(step == 0)
    def _():
        carry[...] = jnp.zeros_like(carry)

    ext[0:2] = jnp.where(chunk == 0, 0.0, ul_ref[...])
    ext[2:t_len + 2] = u_ref[...]
    ext[t_len + 2:t_len + 3] = jnp.where(chunk == n_chunks - 1, 0.0, ur_ref[...])
    uc = cb_ref[...].reshape(1, 1, c)
    for j in range(4):
        uc = uc + cw_ref[j:j + 1, :].reshape(1, 1, c) * ext[j:j + t_len]

    u2 = uc.reshape(t_len * bn, c)
    ub = u2.astype(BF16)
    half = c // 2

    def gate(w_ref, b_ref):
        z = jnp.concatenate([_dot(ub[:, :half], w_ref[0]), _dot(ub[:, half:], w_ref[1])], axis=-1)
        return jax.nn.sigmoid(z + b_ref[...])

    r = gate(wa_ref, ba_ref)
    ig = gate(wx_ref, bx_ref)
    log_a = (-LRU_C) * r * sp_ref[...]
    a = jnp.exp(log_a)
    b = jnp.sqrt(1.0 - jnp.exp(2.0 * log_a)) * (ig * u2)
    a_s[...] = a.reshape(t_len, bn, c)
    b_s[...] = b.reshape(t_len, bn, c)

    def body(k, h):
        t = (t_len - 1 - k) if reverse else k
        h = a_s[t] * h + b_s[t]
        h_s[t] = h
        return h

    carry[...] = lax.fori_loop(0, t_len, body, carry[...], unroll=8)

    if reverse:
        out_ref[...] = ((hf_ref[...] + h_s[...]) * _gelu_tanh(gate_ref[...])).astype(out_ref.dtype)


def _rglru(ug, hf, conv_w, conv_b, wa, ba, wx, bx, lam, bn, *, reverse):
    s_len = ug.shape[0]
    c = LRU_W
    ug3 = ug
    t_len = min(LRU_T, s_len)
    n_chunks = s_len // t_len

    def ck(i):
        return (n_chunks - 1 - i) if reverse else i

    half = c // 2
    nb = LRU_BLOCKS // 2

    def dense_halves(w):
        w = w.astype(F32).reshape(2, nb, c // LRU_BLOCKS, c // LRU_BLOCKS)
        eye = jnp.eye(nb, dtype=F32)
        return jnp.einsum('gnij,nm->gnimj', w, eye).reshape(2, half, half).astype(BF16)

    softplus_neg_lam = jax.nn.softplus(-lam.astype(F32)).reshape(1, c)
    vec = pl.BlockSpec((1, c), lambda i: (0, 0))
    wspec = pl.BlockSpec((2, half, half), lambda i: (0, 0, 0))
    blk = pl.BlockSpec((t_len, bn, c), lambda i: (ck(i), 0, 0))
    in_specs = [
        pl.BlockSpec((2, bn, c), lambda i: (jnp.maximum(ck(i) * (t_len // 2) - 1, 0), 0, 0)),
        blk,
        pl.BlockSpec((1, bn, c), lambda i: (jnp.minimum((ck(i) + 1) * t_len, s_len - 1), 0, 0)),
    ]
    args = [ug3, ug3, ug3]
    scratch = [pltpu.VMEM((t_len + 3, bn, c), F32), pltpu.VMEM((t_len, bn, c), F32),
               pltpu.VMEM((t_len, bn, c), F32)]
    if reverse:
        in_specs += [pl.BlockSpec((t_len, bn, c), lambda i: (ck(i), 0, 1)), blk]
        args += [ug3, hf]
        scratch.append(pltpu.VMEM((t_len, bn, c), F32))
    scratch.append(pltpu.VMEM((bn, c), F32))
    in_specs += [pl.BlockSpec((4, c), lambda i: (0, 0)), vec, wspec, vec, wspec, vec, vec]
    args += [conv_w.astype(F32), conv_b.astype(F32).reshape(1, c), dense_halves(wa),
             ba.astype(F32).reshape(1, c), dense_halves(wx), bx.astype(F32).reshape(1, c), softplus_neg_lam]
    return pl.pallas_call(
        functools.partial(_rglru_kernel, reverse=reverse, n_chunks=n_chunks),
        out_shape=jax.ShapeDtypeStruct((s_len, bn, c), BF16 if reverse else F32),
        grid=(n_chunks,),
        in_specs=in_specs,
        out_specs=blk,
        scratch_shapes=scratch,
        compiler_params=_cparams(("arbitrary",)),
        name="rglru_bw" if reverse else "rglru_fw",
    )(*args)


def _hg_matrices(c, reverse):
    n_levels = int(math.log2(c))
    t = np.arange(c)[:, None]
    r = np.arange(c)[None, :]
    mats, masks, upper = [], [], []
    if not reverse:
        mats.append(r <= t)
        mats.append(r > t)
    else:
        mats.append(r >= t)
        mats.append(r < t)
    for lev in range(n_levels):
        half = c >> (lev + 1)
        parent = 2 * half
        start = (np.arange(c) // parent) * parent
        mid = (start + half)[:, None]
        later = (np.arange(c) % parent >= half)[:, None]
        if not reverse:
            m = np.where(later, (r >= mid) & (r <= t), (r > t) & (r < mid))
            is_q = later
        else:
            m = np.where(later, (r >= mid) & (r < t), (r >= t) & (r < mid))
            is_q = ~later
        mats.append(m)
        same_parent = (start[:, None] == start[None, :])
        masks.append(same_parent & is_q & (~is_q).T)
        upper.append(np.broadcast_to(is_q, (c, HG_DK)))
    mat = np.concatenate(mats, axis=0).astype(np.float32)
    return (np.concatenate([mat, mat], axis=1), np.stack(masks).astype(np.float32),
            np.stack(upper).astype(np.float32))


def _hgrn_kernel(*refs, reverse, n_steps, n_levels):
    if reverse:
        (q_ref, f_ref, v_ref, go_ref, of_ref, lb_ref, gn_ref, mat_ref, mask_ref, isq_ref, out_ref,
         state, x_scr, a_scr, aux_scr, dec_scr, kk_scr) = refs
    else:
        (q_ref, f_ref, v_ref, lb_ref, mat_ref, mask_ref, isq_ref, out_ref,
         state, x_scr, a_scr, aux_scr, dec_scr, kk_scr) = refs
    step = pl.program_id(1)
    c = HG_CHUNK
    t_len = q_ref.shape[0]
    n_chunks = t_len // c

    @pl.when(step == 0)
    def _():
        state[...] = jnp.zeros_like(state)

    def chunk_body(ci, carry):
        ch = (n_chunks - 1 - ci) if reverse else ci
        rows = slice(ch * c, (ch + 1) * c)
        for hp in range(HG_HEADS // 2):
            cols2 = slice(2 * hp * HG_DK, 2 * (hp + 1) * HG_DK)
            fl = f_ref[rows, cols2]
            e = jnp.exp(-jnp.abs(fl))
            r = 1.0 / (1.0 + e)
            er = e * r
            pos = fl > 0.0
            lb = lb_ref[0:1, cols2]
            oml = lb_ref[1:2, cols2]
            g2 = jnp.log2(lb + oml * jnp.where(pos, r, er))
            kk_scr[hp] = oml * jnp.where(pos, er, r)
            g_hi = g2.astype(BF16)
            g_lo = (g2 - g_hi.astype(F32)).astype(BF16)
            dec_scr[hp] = jnp.exp2(_dot(mat_ref[...], jnp.concatenate([g_hi, g_lo], axis=0)))

        edge = (c - 1) if not reverse else 0
        for hd in range(HG_HEADS):
            cols = slice(hd * HG_DK, (hd + 1) * HG_DK)
            pcols = slice((hd % 2) * HG_DK, (hd % 2 + 1) * HG_DK)
            kk = kk_scr[hd // 2, :, pcols]
            dec = dec_scr.at[hd // 2]
            qv = q_ref[rows, cols]
            qq = qv * jax.nn.sigmoid(qv)
            qb, kb = qq.astype(BF16), kk.astype(BF16)
            d_in = dec[0:c, pcols]
            x_scr[hd, n_levels] = (qq * d_in).astype(BF16)
            x_scr[hd, n_levels + 1] = (kk * dec[c:2 * c, pcols]).astype(BF16)
            aux_scr[hd, 0] = jnp.broadcast_to(d_in[edge:edge + 1, :], (c, HG_DK))
            aux_scr[hd, 1] = jnp.broadcast_to(jnp.sum(qq * kk, axis=-1, keepdims=True), (c, HG_DK))
            for lev in range(n_levels):
                half = c >> (lev + 1)
                if half >= SUBLANES_BF16:
                    first_is_q = bool(reverse)
                    base = jnp.concatenate(
                        [(qb if (blk % 2 == 1) != first_is_q else kb)[blk * half:(blk + 1) * half]
                         for blk in range(c // half)], axis=0)
                else:
                    base = jnp.where(isq_ref[lev] > 0.5, qb, kb)
                x_scr[hd, lev] = base * dec[(2 + lev) * c:(3 + lev) * c, pcols].astype(BF16)

        for hd in range(HG_HEADS):
            cols = slice(hd * HG_DK, (hd + 1) * HG_DK)
            for lev in range(n_levels):
                xl = x_scr[hd, lev]
                a_scr[hd, lev] = _nt_dot(xl, xl)
            st = state[hd]
            a_scr[hd, n_levels] = _nt_dot(x_scr[hd, n_levels], st.astype(BF16))
            state[hd] = st * aux_scr[hd, 0] + _dot(v_ref[rows, cols].T.astype(BF16), x_scr[hd, n_levels + 1])

        for hd in range(HG_HEADS):
            att = mask_ref[0] * a_scr[hd, 0]
            for lev in range(1, n_levels):
                att = att + mask_ref[lev] * a_scr[hd, lev]
            x_scr[hd, 0] = att.astype(BF16)

        for hd in range(HG_HEADS):
            cols = slice(hd * HG_DK, (hd + 1) * HG_DK)
            v = v_ref[rows, cols]
            o = a_scr[hd, n_levels] + _dot(x_scr[hd, 0], v.astype(BF16)) + aux_scr[hd, 1] * v

            if reverse:
                o = o + of_ref[rows, cols]
                o = o * lax.rsqrt(jnp.mean(o * o, axis=-1, keepdims=True) + EPS) * gn_ref[...]
                gv = go_ref[rows, cols]
                out_ref[rows, cols] = (o * (gv * jax.nn.sigmoid(gv))).astype(out_ref.dtype)
            else:
                out_ref[rows, cols] = o
        return carry

    for ci in range(n_chunks):
        chunk_body(ci, 0)


def _hgrn(hg, o_fw, lb, gnorm_g, *, reverse):
    bn, s_len, _ = hg.shape
    w = HG_HEADS * HG_DK
    t_len = min(HG_T, s_len)
    n_steps = s_len // t_len
    mats, masks, isq = _hg_matrices(HG_CHUNK, reverse)
    n_levels = masks.shape[0]
    lbf = lb.astype(F32)
    lb_tab = jnp.stack([lbf, 1.0 - lbf])

    def blk(g):
        return pl.BlockSpec((None, t_len, w), lambda b, i: (b, (n_steps - 1 - i) if reverse else i, g))

    def const(shape):
        return pl.BlockSpec(shape, lambda b, i: (0,) * len(shape))

    in_specs = [blk(0), blk(2 if reverse else 1), blk(3)]
    args = [hg, hg, hg]
    if reverse:
        in_specs += [blk(4), blk(0)]
        args += [hg, o_fw]
    in_specs.append(const((2, w)))
    args.append(lb_tab)
    if reverse:
        in_specs.append(const((1, HG_DK)))
        args.append(gnorm_g.astype(F32).reshape(1, HG_DK))
    in_specs += [const(mats.shape), const(masks.shape), const(isq.shape)]
    args += [jnp.asarray(mats, BF16), jnp.asarray(masks, F32), jnp.asarray(isq, BF16)]
    return pl.pallas_call(
        functools.partial(_hgrn_kernel, reverse=reverse, n_steps=n_steps, n_levels=n_levels),
        out_shape=jax.ShapeDtypeStruct((bn, s_len, w), BF16 if reverse else F32),
        grid=(bn, n_steps),
        in_specs=in_specs,
        out_specs=blk(0),
        scratch_shapes=[pltpu.VMEM((HG_HEADS, HG_DK, HG_DK), F32),
                        pltpu.VMEM((HG_HEADS, n_levels + 2, HG_CHUNK, HG_DK), BF16),
                        pltpu.VMEM((HG_HEADS, n_levels + 1, HG_CHUNK, HG_CHUNK), F32),
                        pltpu.VMEM((HG_HEADS, 2, HG_CHUNK, HG_DK), F32),
                        pltpu.VMEM((HG_HEADS // 2, (n_levels + 2) * HG_CHUNK, 2 * HG_DK), F32),
                        pltpu.VMEM((HG_HEADS // 2, HG_CHUNK, 2 * HG_DK), F32)],
        compiler_params=_cparams(("parallel", "arbitrary")),
        name="hgrn_bw" if reverse else "hgrn_fw",
    )(*args)


def _od_out_kernel(yc_ref, yd_ref, w_ref, x_ref, out_ref):
    half = yc_ref.shape[-1]
    out_ref[...] = x_ref[...] + _dot(yc_ref[...], w_ref[0:half, :]) + _dot(yd_ref[...], w_ref[half:, :])


def _od_out(x, yc_tm, yd, w_bf16):
    bn, s_len, d = x.shape
    tm = min(ROW_TILE_WIDE, s_len)
    c = yd.shape[-1]
    x_spec = pl.BlockSpec((None, tm, d), lambda b, i: (b, i, 0))
    return pl.pallas_call(
        _od_out_kernel,
        out_shape=jax.ShapeDtypeStruct(x.shape, F32),
        grid=(bn, s_len // tm),
        in_specs=[pl.BlockSpec((tm, c), lambda b, i: (i, b)),
                  pl.BlockSpec((None, tm, c), lambda b, i: (b, i, 0)),
                  pl.BlockSpec((2 * c, d), lambda b, i: (0, 0)),
                  x_spec],
        out_specs=x_spec,
        compiler_params=_cparams(("parallel", "parallel")),
        name="od_out",
    )(yc_tm, yd, w_bf16, x)


def _even_layer(x, g, w_in, w_out, rpb):
    qkv, view4, view16 = _ev_proj(x, g, w_in.astype(BF16))
    outs, maxes, dens = zip(*[_dilated_branch(view, dil) for dil, view in zip(DILATIONS, (qkv, view4, view16))])
    yb = _neighborhood(qkv, rpb)
    return _ev_out(x, outs, maxes, dens, yb, w_out.astype(BF16))


def _odd_layer(x, g, w_in, w_out, conv_w, conv_b, wa, ba, wx, bx, lam, lb_f, lb_b, gnorm_g):
    bn, s_len, _ = x.shape
    ug, hg = _od_proj(x, g, w_in.astype(BF16))
    h_fw = _rglru(ug, None, conv_w, conv_b, wa[0], ba[0], wx[0], bx[0], lam[0], bn, reverse=False)
    yc = _rglru(ug, h_fw, conv_w, conv_b, wa[1], ba[1], wx[1], bx[1], lam[1], bn, reverse=True)
    o_fw = _hgrn(hg, None, lb_f, gnorm_g, reverse=False)
    yd = _hgrn(hg, o_fw, lb_b, gnorm_g, reverse=True)
    return _od_out(x, yc.reshape(s_len, bn * LRU_W), yd, w_out.astype(BF16))


def kernel(x, mem, norm_mix_g, norm_xa_g, norm_mem_g, norm_mlp_g, final_norm_g, ev_w_in, ev_w_out, na_rpb,
           od_w_in, od_w_out, conv_w, conv_b, lru_wa, lru_ba, lru_wx, lru_bx, lru_lambda, hgrn_lb_logits,
           hgrn_norm_g, xa_wq, xa_wkv, xa_wo, mlp_w1, mlp_w2):
    depth = norm_mix_g.shape[0]
    p_lb = jax.nn.softmax(hgrn_lb_logits.astype(F32), axis=0)
    lower_bounds = jnp.cumsum(p_lb, axis=0) - p_lb[0:1]
    for layer in range(depth):
        if layer % 2 == 0:
            e = layer // 2
            x = _even_layer(x, norm_mix_g[layer], ev_w_in[e], ev_w_out[e], na_rpb[e])
        else:
            o = layer // 2
            x = _odd_layer(x, norm_mix_g[layer], od_w_in[o], od_w_out[o], conv_w[o], conv_b[o],
                           lru_wa[o], lru_ba[o], lru_wx[o], lru_bx[o], lru_lambda[o],
                           lower_bounds[layer, 0], lower_bounds[layer, 1], hgrn_norm_g[o])
        kv = _mem_kv(mem, norm_mem_g[layer], xa_wkv[layer].astype(BF16))
        x = _xattn(x, norm_xa_g[layer], xa_wq[layer].astype(BF16), kv, xa_wo[layer].astype(BF16))
        x = _mlp(x, norm_mlp_g[layer], mlp_w1[layer].astype(BF16), mlp_w2[layer].astype(BF16),
                 g_final=final_norm_g if layer == depth - 1 else None)
    return x
```

```python
import functools
import math

import jax
import jax.numpy as jnp
import numpy as np
from jax import lax
from jax.experimental import pallas as pl
from jax.experimental.pallas import tpu as pltpu

F32 = jnp.float32
BF16 = jnp.bfloat16

D_MODEL = 1024
HEAD_DIM = 64
ROT_DIM = 16
ROPE_THETA = 500000.0
N_HEADS_A = 8
N_HEADS_B = 8
GROUP_W = N_HEADS_A * HEAD_DIM
DILATIONS = (1, 4, 16)
WIN_HALF = 64
GRID_W = 64
NA_ROWS = 8
NA_COLS = 16
LRU_W = 512
LRU_BLOCKS = 8
LRU_C = 8.0
HG_HEADS = 4
HG_DK = 128
XA_HEADS = 4
XA_DH = 256
D_FF = 4096
EPS = 1e-6
LOG2E = math.log2(math.e)

LANES = 128
SUBLANES_BF16 = 16
V7X_VMEM_LIMIT_BYTES = 56 * 1024 * 1024

ROW_TILE = 512
ROW_TILE_WIDE = 1024
ATT_Q_BLOCK = 512
ATT_SUB = 128
NA_ROW_GROUP = 8
LRU_T = 128
HG_CHUNK = 128
HG_T = 512


def _cparams(sem):
    return pltpu.CompilerParams(dimension_semantics=sem, vmem_limit_bytes=V7X_VMEM_LIMIT_BYTES)


def _resident(shape, index_map):
    return pl.BlockSpec(shape, index_map, pipeline_mode=pl.Buffered(1))


def _rms(x, g):
    return x * lax.rsqrt(jnp.mean(x * x, axis=-1, keepdims=True) + EPS) * g


def _nt_dot(a, b):
    return lax.dot_general(a, b, (((1,), (1,)), ((), ())), preferred_element_type=F32)


def _dot(a, b):
    return jnp.dot(a, b, preferred_element_type=F32)


def _head_selectors():
    lane = lax.broadcasted_iota(jnp.int32, (1, LANES), 1)
    low = (lane < HEAD_DIM).astype(F32)
    return low.astype(BF16), (1.0 - low).astype(BF16)


def _ev_proj_kernel(x_ref, g_ref, w_ref, rc_ref, rs1_ref, rs2_ref, o_ref, o4_ref, o16_ref, slab):
    h = _rms(x_ref[...], g_ref[...]).astype(BF16)
    scale = HEAD_DIM ** -0.5 * LOG2E
    tm = x_ref.shape[0]
    n_slabs = GROUP_W // LANES
    for c in range(6):
        acc = _dot(h, w_ref[:, c * GROUP_W:(c + 1) * GROUP_W])
        if c in (0, 1):
            rc, rs1, rs2 = rc_ref[...], rs1_ref[...], rs2_ref[...]
            parts = []
            for j in range(n_slabs):
                t = acc[:, j * LANES:(j + 1) * LANES]
                t = t * rc + pltpu.roll(t, LANES - ROT_DIM // 2, 1) * rs1 + pltpu.roll(t, ROT_DIM // 2, 1) * rs2
                parts.append(t)
            acc = jnp.concatenate(parts, axis=-1)
        if c in (0, 3):
            acc = acc * scale
        o_ref[:, c * GROUP_W:(c + 1) * GROUP_W] = acc.astype(BF16)
        if c < 3:
            for j in range(n_slabs):
                slab[j] = acc[:, j * LANES:(j + 1) * LANES]
            for dil, dst in ((DILATIONS[1], o4_ref), (DILATIONS[2], o16_ref)):
                for r in range(dil):
                    for j in range(n_slabs):
                        col = r * 3 * GROUP_W + c * GROUP_W + j * LANES
                        dst[:, col:col + LANES] = slab[j, pl.ds(r, tm // dil, stride=dil), :].astype(BF16)


def _rope_tables(s_len):
    half = ROT_DIM // 2
    inv = jnp.asarray(ROPE_THETA ** (-np.arange(half) * 2.0 / ROT_DIM), F32)
    ang = jnp.arange(s_len, dtype=F32)[:, None] * inv[None, :]
    cos, sin = jnp.cos(ang), jnp.sin(ang)
    ones = jnp.ones((s_len, HEAD_DIM - ROT_DIM), F32)
    zeros = jnp.zeros((s_len, HEAD_DIM - ROT_DIM), F32)
    zh = jnp.zeros((s_len, half), F32)
    rc = jnp.concatenate([cos, cos, ones], axis=-1)
    rs1 = jnp.concatenate([-sin, zh, zeros], axis=-1)
    rs2 = jnp.concatenate([zh, sin, zeros], axis=-1)
    rep = LANES // HEAD_DIM
    return jnp.tile(rc, (1, rep)), jnp.tile(rs1, (1, rep)), jnp.tile(rs2, (1, rep))


def _ev_proj(x, g, w_bf16):
    bn, s_len, d = x.shape
    tm = min(ROW_TILE, s_len)
    n_out = w_bf16.shape[1]
    rc, rs1, rs2 = _rope_tables(s_len)
    tbl_spec = pl.BlockSpec((tm, LANES), lambda b, i: (i, 0))
    wa = 3 * GROUP_W
    d4, d16 = DILATIONS[1], DILATIONS[2]
    return pl.pallas_call(
        _ev_proj_kernel,
        out_shape=(jax.ShapeDtypeStruct((bn, s_len, n_out), BF16),
                   jax.ShapeDtypeStruct((bn, s_len // d4, d4 * wa), BF16),
                   jax.ShapeDtypeStruct((bn, s_len // d16, d16 * wa), BF16)),
        grid=(bn, s_len // tm),
        in_specs=[
            pl.BlockSpec((None, tm, d), lambda b, i: (b, i, 0)),
            pl.BlockSpec((1, d), lambda b, i: (0, 0)),
            pl.BlockSpec((d, n_out), lambda b, i: (0, 0)),
            tbl_spec, tbl_spec, tbl_spec,
        ],
        out_specs=(pl.BlockSpec((None, tm, n_out), lambda b, i: (b, i, 0)),
                   pl.BlockSpec((None, tm // d4, d4 * wa), lambda b, i: (b, i, 0)),
                   pl.BlockSpec((None, tm // d16, d16 * wa), lambda b, i: (b, i, 0))),
        scratch_shapes=[pltpu.VMEM((GROUP_W // LANES, tm, LANES), F32)],
        compiler_params=_cparams(("parallel", "parallel")),
        name="ev_proj",
    )(x, g.reshape(1, d), w_bf16, rc, rs1, rs2)


def _dilated_kernel(q_ref, kl_ref, kc_ref, kr_ref, vl_ref, vc_ref, vr_ref, o_ref, max_ref, den_ref,
                    kext, vext, s_scr, p_scr, *, lq, l_total):
    i = pl.program_id(2)
    kext[0:WIN_HALF, :] = kl_ref[...]
    kext[WIN_HALF:WIN_HALF + lq, :] = kc_ref[...]
    kext[WIN_HALF + lq:, :] = kr_ref[...]
    vext[0:WIN_HALF, :] = vl_ref[...]
    vext[WIN_HALF:WIN_HALF + lq, :] = vc_ref[...]
    vext[WIN_HALF + lq:, :] = vr_ref[...]

    wk = ATT_SUB + 2 * WIN_HALF
    qi = lax.broadcasted_iota(jnp.int32, (ATT_SUB, wk), 0)
    ci = lax.broadcasted_iota(jnp.int32, (ATT_SUB, wk), 1)
    band_bias = jnp.where((ci - qi >= 0) & (ci - qi <= 2 * WIN_HALF), 0.0, -jnp.inf).astype(F32)
    crow = lax.broadcasted_iota(jnp.int32, (1, wk), 1)
    lane = lax.broadcasted_iota(jnp.int32, (ATT_SUB, LANES), 1)
    low_half = lane < HEAD_DIM
    sel_lo, sel_hi = _head_selectors()
    n_sub = lq // ATT_SUB
    n_hp = GROUP_W // LANES

    for j in range(n_sub):
        base = i * lq + j * ATT_SUB - WIN_HALF
        in_seq = (crow + base >= 0) & (crow + base < l_total)
        bias = band_bias + jnp.where(in_seq, 0.0, -jnp.inf).astype(F32)
        bias2 = jnp.concatenate([bias, bias], axis=0)
        rows = slice(j * ATT_SUB, (j + 1) * ATT_SUB)
        wrows = slice(j * ATT_SUB, j * ATT_SUB + wk)
        for hp in range(n_hp):
            cols = slice(hp * LANES, (hp + 1) * LANES)
            qp = q_ref[rows, cols]
            q2 = jnp.concatenate([qp * sel_lo, qp * sel_hi], axis=0)
            s_scr[j * n_hp + hp] = _nt_dot(q2, kext[wrows, cols]) + bias2
    for j in range(n_sub):
        m_tile = jnp.zeros((ATT_SUB, LANES), F32)
        den_tile = jnp.ones((ATT_SUB, LANES), F32)
        for hp in range(n_hp):
            t = j * n_hp + hp
            s = s_scr[t]
            m = jnp.max(s, axis=-1, keepdims=True)
            p = jnp.exp2(s - m)
            den = jnp.sum(p, axis=-1, keepdims=True)
            p_scr[t] = p.astype(BF16)
            is_lo, is_hi = lane == 2 * hp, lane == 2 * hp + 1
            m_tile = jnp.where(is_lo, m[0:ATT_SUB], jnp.where(is_hi, m[ATT_SUB:], m_tile))
            den_tile = jnp.where(is_lo, den[0:ATT_SUB], jnp.where(is_hi, den[ATT_SUB:], den_tile))
        max_ref[j * ATT_SUB:(j + 1) * ATT_SUB, :] = m_tile
        den_ref[j * ATT_SUB:(j + 1) * ATT_SUB, :] = den_tile
    for j in range(n_sub):
        rows = slice(j * ATT_SUB, (j + 1) * ATT_SUB)
        wrows = slice(j * ATT_SUB, j * ATT_SUB + wk)
        for hp in range(n_hp):
            cols = slice(hp * LANES, (hp + 1) * LANES)
            o = _dot(p_scr[j * n_hp + hp], vext[wrows, cols])
            o_ref[rows, cols] = jnp.where(low_half, o[0:ATT_SUB], o[ATT_SUB:]).astype(BF16)


def _dilated_branch(view, dil):
    bn, l_total, width = view.shape
    n_groups = width // (dil * GROUP_W)
    lq = min(ATT_Q_BLOCK, l_total)
    nblk = l_total // lq
    per = lq // WIN_HALF
    n_halo = l_total // WIN_HALF
    n_units = (lq // ATT_SUB) * (GROUP_W // LANES)

    def cur(g):
        return pl.BlockSpec((None, lq, GROUP_W), lambda b, r, i: (b, i, r * n_groups + g))

    def left(g):
        return pl.BlockSpec((None, WIN_HALF, GROUP_W),
                            lambda b, r, i: (b, jnp.maximum(i * per - 1, 0), r * n_groups + g))

    def right(g):
        return pl.BlockSpec((None, WIN_HALF, GROUP_W),
                            lambda b, r, i: (b, jnp.minimum((i + 1) * per, n_halo - 1), r * n_groups + g))

    stat_shape = jax.ShapeDtypeStruct((bn, l_total, dil * LANES), F32)
    stat_spec = pl.BlockSpec((None, lq, LANES), lambda b, r, i: (b, i, r))
    return pl.pallas_call(
        functools.partial(_dilated_kernel, lq=lq, l_total=l_total),
        out_shape=(jax.ShapeDtypeStruct((bn, l_total, dil * GROUP_W), BF16), stat_shape, stat_shape),
        grid=(bn, dil, nblk),
        in_specs=[cur(0), left(1), cur(1), right(1), left(2), cur(2), right(2)],
        out_specs=(pl.BlockSpec((None, lq, GROUP_W), lambda b, r, i: (b, i, r)), stat_spec, stat_spec),
        scratch_shapes=[pltpu.VMEM((lq + 2 * WIN_HALF, GROUP_W), BF16),
                        pltpu.VMEM((lq + 2 * WIN_HALF, GROUP_W), BF16),
                        pltpu.VMEM((n_units, 2 * ATT_SUB, ATT_SUB + 2 * WIN_HALF), F32),
                        pltpu.VMEM((n_units, 2 * ATT_SUB, ATT_SUB + 2 * WIN_HALF), BF16)],
        compiler_params=_cparams(("parallel", "parallel", "parallel")),
        name=f"dilated_attn_d{dil}",
    )(view, view, view, view, view, view, view)


NA_QT = NA_ROW_GROUP * GRID_W
NA_KT = 2 * NA_QT
NA_KW = NA_ROWS * GRID_W


def _na_tables(rpb):
    rpb = rpb.astype(F32)
    nh = rpb.shape[0]
    ext = GRID_W - NA_COLS
    padded = jnp.concatenate([jnp.repeat(rpb[..., :1], ext, axis=-1), rpb,
                              jnp.repeat(rpb[..., -1:], ext, axis=-1)], axis=-1)
    t1 = jnp.stack([padded[..., GRID_W - 1 - qc:2 * GRID_W - 1 - qc] for qc in range(GRID_W)], axis=2)
    qc = np.arange(GRID_W)[:, None]
    kc = np.arange(GRID_W)[None, :]
    c0 = np.clip(qc - NA_COLS // 2, 0, GRID_W - NA_COLS)
    t1 = jnp.where((kc >= c0) & (kc < c0 + NA_COLS), t1 * LOG2E, -jnp.inf)
    per_delta = [jnp.concatenate([t1[:, kr - delta + NA_ROWS - 1] for kr in range(NA_ROWS)], axis=-1)
                 for delta in range(NA_ROWS)]
    return jnp.stack(per_delta).reshape(NA_ROWS, nh // 2, 2 * GRID_W, NA_KW)


def _na_kernel(q_ref, k_ref, v_ref, tb_ref, o_ref, s_scr, p_scr, inv_scr, *, rows):
    i0 = pl.program_id(1) * NA_ROW_GROUP
    w0 = jnp.clip(i0 - NA_ROWS // 2, 0, rows - 2 * NA_ROW_GROUP)
    lane = lax.broadcasted_iota(jnp.int32, (GRID_W, LANES), 1)
    low_half = lane < HEAD_DIM
    sel_lo, sel_hi = _head_selectors()
    n_hp = GROUP_W // LANES

    def key_rows(a):
        r0 = jnp.clip(i0 + a - NA_ROWS // 2, 0, rows - NA_ROWS)
        return pl.ds(pl.multiple_of((r0 - w0) * GRID_W, GRID_W), NA_KW), i0 + a - r0

    for a in range(NA_ROW_GROUP):
        krows, delta = key_rows(a)
        qrows = slice(a * GRID_W, (a + 1) * GRID_W)
        for hp in range(n_hp):
            cols = slice(hp * LANES, (hp + 1) * LANES)
            qp = q_ref[qrows, cols]
            q2 = jnp.concatenate([qp * sel_lo, qp * sel_hi], axis=0)
            s_scr[a * n_hp + hp] = _nt_dot(q2, k_ref[0, krows, cols]) + tb_ref[delta, hp]
    for t in range(NA_ROW_GROUP * n_hp):
        s = s_scr[t]
        p = jnp.exp2(s - jnp.max(s, axis=-1, keepdims=True))
        inv_scr[t] = jnp.broadcast_to(1.0 / jnp.sum(p, axis=-1, keepdims=True), (2 * GRID_W, LANES))
        p_scr[t] = p.astype(BF16)
    for a in range(NA_ROW_GROUP):
        krows, _ = key_rows(a)
        qrows = slice(a * GRID_W, (a + 1) * GRID_W)
        for hp in range(n_hp):
            cols = slice(hp * LANES, (hp + 1) * LANES)
            t = a * n_hp + hp
            o = _dot(p_scr[t], v_ref[0, krows, cols]) * inv_scr[t]
            o_ref[qrows, cols] = jnp.where(low_half, o[0:GRID_W], o[GRID_W:]).astype(BF16)


def _neighborhood(qkv, rpb):
    bn, s_len, _ = qkv.shape
    rows = s_len // GRID_W
    n_groups = rows // NA_ROW_GROUP
    tables = _na_tables(rpb)

    def window(g):
        return pl.BlockSpec(
            (pl.Element(1), pl.Element(NA_KT), pl.Element(GROUP_W)),
            lambda b, i: (b, jnp.clip(i * NA_ROW_GROUP - NA_ROWS // 2, 0, rows - 2 * NA_ROW_GROUP) * GRID_W,
                          g * GROUP_W))

    return pl.pallas_call(
        functools.partial(_na_kernel, rows=rows),
        out_shape=jax.ShapeDtypeStruct((bn, s_len, GROUP_W), BF16),
        grid=(bn, n_groups),
        in_specs=[pl.BlockSpec((None, NA_QT, GROUP_W), lambda b, i: (b, i, 3)), window(4), window(5),
                  pl.BlockSpec(tables.shape, lambda b, i: (0, 0, 0, 0))],
        out_specs=pl.BlockSpec((None, NA_QT, GROUP_W), lambda b, i: (b, i, 0)),
        scratch_shapes=[pltpu.VMEM((NA_ROW_GROUP * GROUP_W // LANES, 2 * GRID_W, NA_KW), F32),
                        pltpu.VMEM((NA_ROW_GROUP * GROUP_W // LANES, 2 * GRID_W, NA_KW), BF16),
                        pltpu.VMEM((NA_ROW_GROUP * GROUP_W // LANES, 2 * GRID_W, LANES), F32)],
        compiler_params=_cparams(("parallel", "parallel")),
        name="neighborhood_attn",
    )(qkv, qkv, qkv, tables)


def _ev_out_kernel(o1, o2v, o3v, m1, m2v, m3v, d1, d2v, d3v, yb_ref, ex_ref, w_ref, x_ref, out_ref,
                   so2, so3, st2, st3):
    tm = x_ref.shape[0]
    n_slabs = GROUP_W // LANES
    for dil, ov, mv, dv, so, st in ((DILATIONS[1], o2v, m2v, d2v, so2, st2), (DILATIONS[2], o3v, m3v, d3v, so3, st3)):
        for r in range(dil):
            dst = pl.ds(r, tm // dil, stride=dil)
            for j in range(n_slabs):
                col = r * GROUP_W + j * LANES
                so[j, dst, :] = ov[:, col:col + LANES].astype(F32)
            st[0, dst, :] = mv[:, r * LANES:(r + 1) * LANES]
            st[1, dst, :] = dv[:, r * LANES:(r + 1) * LANES]
    o2 = jnp.concatenate([so2[j] for j in range(n_slabs)], axis=-1)
    o3 = jnp.concatenate([so3[j] for j in range(n_slabs)], axis=-1)
    ma, mb, mc = m1[...], st2[0], st3[0]
    m = jnp.maximum(jnp.maximum(ma, mb), mc)
    ea, eb, ec = jnp.exp2(ma - m), jnp.exp2(mb - m), jnp.exp2(mc - m)
    inv = 1.0 / (ea * d1[...] + eb * st2[1] + ec * st3[1])
    ex = ex_ref[...]

    def expand(w):
        hi = w.astype(BF16)
        lo = (w - hi.astype(F32)).astype(BF16)
        return _dot(hi, ex) + _dot(lo, ex)

    ya = expand(ea * inv) * o1[...].astype(F32) + expand(eb * inv) * o2 + expand(ec * inv) * o3
    y = _dot(ya.astype(BF16), w_ref[0:GROUP_W, :]) + _dot(yb_ref[...], w_ref[GROUP_W:, :])
    out_ref[...] = x_ref[...] + y


def _ev_out(x, outs, maxes, dens, yb, w_bf16):
    bn, s_len, d = x.shape
    tm = min(ROW_TILE, s_len)
    ex = np.zeros((LANES, GROUP_W), np.float32)
    for h in range(N_HEADS_A):
        ex[h, h * HEAD_DIM:(h + 1) * HEAD_DIM] = 1.0
    def o_spec(dil):
        return pl.BlockSpec((None, tm // dil, dil * GROUP_W), lambda b, i: (b, i, 0))

    def l_spec(dil):
        return pl.BlockSpec((None, tm // dil, dil * LANES), lambda b, i: (b, i, 0))

    x_spec = pl.BlockSpec((None, tm, d), lambda b, i: (b, i, 0))
    n_slabs = GROUP_W // LANES
    return pl.pallas_call(
        _ev_out_kernel,
        out_shape=jax.ShapeDtypeStruct(x.shape, F32),
        grid=(bn, s_len // tm),
        in_specs=[o_spec(d_) for d_ in DILATIONS] + 2 * [l_spec(d_) for d_ in DILATIONS]
        + [o_spec(1),
           pl.BlockSpec((LANES, GROUP_W), lambda b, i: (0, 0)),
           pl.BlockSpec((2 * GROUP_W, d), lambda b, i: (0, 0)),
           x_spec],
        out_specs=x_spec,
        scratch_shapes=[pltpu.VMEM((n_slabs, tm, LANES), F32), pltpu.VMEM((n_slabs, tm, LANES), F32),
                        pltpu.VMEM((2, tm, LANES), F32), pltpu.VMEM((2, tm, LANES), F32)],
        compiler_params=_cparams(("parallel", "parallel")),
        name="ev_out",
    )(*outs, *maxes, *dens, yb, jnp.asarray(ex, BF16), w_bf16, x)


def _norm_matmul_kernel(x_ref, g_ref, w_ref, o_ref):
    o_ref[...] = _dot(_rms(x_ref[...], g_ref[...]).astype(BF16), w_ref[...]).astype(o_ref.dtype)


def _mem_kv(mem, g, w_bf16):
    bn, m_len, d = mem.shape
    n_out = w_bf16.shape[1]
    return pl.pallas_call(
        _norm_matmul_kernel,
        out_shape=jax.ShapeDtypeStruct((bn, m_len, n_out), BF16),
        grid=(bn,),
        in_specs=[pl.BlockSpec((None, m_len, d), lambda b: (b, 0, 0)),
                  pl.BlockSpec((1, d), lambda b: (0, 0)),
                  pl.BlockSpec((d, n_out), lambda b: (0, 0))],
        out_specs=pl.BlockSpec((None, m_len, n_out), lambda b: (b, 0, 0)),
        compiler_params=_cparams(("parallel",)),
        name="mem_kv",
    )(mem, g.reshape(1, d), w_bf16)


def _xattn_kernel(x_ref, g_ref, wq_ref, kv_ref, wo_ref, out_ref):
    x = x_ref[...]
    q = (_dot(_rms(x, g_ref[...]).astype(BF16), wq_ref[...]) * (XA_DH ** -0.5 * LOG2E)).astype(BF16)
    heads = []
    for hd in range(XA_HEADS):
        cols = slice(hd * XA_DH, (hd + 1) * XA_DH)
        s = _nt_dot(q[:, cols], kv_ref[:, cols])
        m = jnp.max(s, axis=-1, keepdims=True)
        p = jnp.exp2(s - m)
        den = jnp.sum(p, axis=-1, keepdims=True)
        vcols = slice(D_MODEL + hd * XA_DH, D_MODEL + (hd + 1) * XA_DH)
        heads.append((_dot(p.astype(BF16), kv_ref[:, vcols]) * (1.0 / den)).astype(BF16))
    o = jnp.concatenate(heads, axis=-1)
    out_ref[...] = x + _dot(o, wo_ref[...])


def _xattn(x, g, wq, kv, wo):
    bn, s_len, d = x.shape
    tm = min(ROW_TILE_WIDE, s_len)
    m_len = kv.shape[1]
    x_spec = pl.BlockSpec((None, tm, d), lambda b, i: (b, i, 0))
    return pl.pallas_call(
        _xattn_kernel,
        out_shape=jax.ShapeDtypeStruct(x.shape, F32),
        grid=(bn, s_len // tm),
        in_specs=[x_spec,
                  pl.BlockSpec((1, d), lambda b, i: (0, 0)),
                  pl.BlockSpec((d, d), lambda b, i: (0, 0)),
                  pl.BlockSpec((None, m_len, 2 * d), lambda b, i: (b, 0, 0)),
                  pl.BlockSpec((d, d), lambda b, i: (0, 0))],
        out_specs=x_spec,
        compiler_params=_cparams(("parallel", "parallel")),
        name="mem_xattn",
    )(x, g.reshape(1, d), wq, kv, wo)


def _mlp_kernel(x_ref, g_ref, w1_ref, w2_ref, gf_ref, out_ref, *, final_norm):
    x = x_ref[...]
    h = _rms(x, g_ref[...]).astype(BF16)
    acc = x
    for c in range(D_FF // D_MODEL):
        cols = slice(c * D_MODEL, (c + 1) * D_MODEL)
        a = jnp.maximum(_dot(h, w1_ref[:, cols]), 0.0)
        acc = acc + _dot((a * a).astype(BF16), w2_ref[cols, :])
    out_ref[...] = _rms(acc, gf_ref[...]) if final_norm else acc


def _mlp(x, g, w1, w2, g_final=None):
    bn, s_len, d = x.shape
    tm = min(ROW_TILE_WIDE, s_len)
    final_norm = g_final is not None
    gf = (g_final if final_norm else g).reshape(1, d)
    x_spec = pl.BlockSpec((None, tm, d), lambda b, i: (b, i, 0))
    vec = pl.BlockSpec((1, d), lambda b, i: (0, 0))
    return pl.pallas_call(
        functools.partial(_mlp_kernel, final_norm=final_norm),
        out_shape=jax.ShapeDtypeStruct(x.shape, F32),
        grid=(bn, s_len // tm),
        in_specs=[x_spec, vec,
                  _resident((d, D_FF), lambda b, i: (0, 0)),
                  _resident((D_FF, d), lambda b, i: (0, 0)),
                  vec],
        out_specs=x_spec,
        compiler_params=_cparams(("parallel", "parallel")),
        name="mlp_final" if final_norm else "mlp",
    )(x, g.reshape(1, d), w1, w2, gf)


def _od_proj_kernel(x_ref, g_ref, w_ref, ug_ref, *rest):
    hg_refs, slab = rest[:-1], rest[-1]
    bn, tq, d = x_ref.shape
    h = _rms(x_ref[...].reshape(bn * tq, d), g_ref[...]).astype(BF16)
    n_ug = 2 * LRU_W
    n_slabs = GROUP_W // LANES
    for c in range(n_ug // GROUP_W):
        acc = _dot(h, w_ref[:, c * GROUP_W:(c + 1) * GROUP_W])
        for b in range(bn):
            for j in range(n_slabs):
                slab[j, pl.ds(b, tq, stride=bn), :] = acc[b * tq:(b + 1) * tq, j * LANES:(j + 1) * LANES]
        for j in range(n_slabs):
            ug_ref[:, c * GROUP_W + j * LANES:c * GROUP_W + (j + 1) * LANES] = slab[j]
    for c, hg_ref in enumerate(hg_refs):
        acc = _dot(h, w_ref[:, n_ug + c * GROUP_W:n_ug + (c + 1) * GROUP_W])
        hg_ref[...] = acc.reshape(bn, tq, GROUP_W).astype(hg_ref.dtype)


HG_INPUT_DTYPES = (("q", BF16), ("f_fw", F32), ("f_bw", F32), ("i_in", BF16), ("g_out", BF16))


def _od_proj(x, g, w_bf16):
    bn, s_len, d = x.shape
    tq = min(ROW_TILE // bn, s_len)
    n_ug = 2 * LRU_W
    n_hg = w_bf16.shape[1] - n_ug
    assert n_hg == len(HG_INPUT_DTYPES) * GROUP_W
    hg_spec = pl.BlockSpec((bn, tq, GROUP_W), lambda i: (0, i, 0))
    ug, *hg = pl.pallas_call(
        _od_proj_kernel,
        out_shape=[jax.ShapeDtypeStruct((s_len * bn, n_ug), F32)]
        + [jax.ShapeDtypeStruct((bn, s_len, GROUP_W), dt) for _, dt in HG_INPUT_DTYPES],
        grid=(s_len // tq,),
        in_specs=[pl.BlockSpec((bn, tq, d), lambda i: (0, i, 0)),
                  pl.BlockSpec((1, d), lambda i: (0, 0)),
                  pl.BlockSpec((d, n_ug + n_hg), lambda i: (0, 0))],
        out_specs=[pl.BlockSpec((tq * bn, n_ug), lambda i: (i, 0))] + [hg_spec] * len(HG_INPUT_DTYPES),
        scratch_shapes=[pltpu.VMEM((GROUP_W // LANES, tq * bn, LANES), F32)],
        compiler_params=_cparams(("parallel",)),
        name="od_proj",
    )(x, g.reshape(1, d), w_bf16)
    return ug.reshape(s_len, bn, n_ug), dict(zip([n for n, _ in HG_INPUT_DTYPES], hg))


def _gelu_tanh(x):
    return 0.5 * x * (1.0 + jnp.tanh(math.sqrt(2.0 / math.pi) * (x + 0.044715 * (x * x * x))))


def _rglru_kernel(*refs, reverse, n_chunks):
    if reverse:
        (ul_ref, u_ref, ur_ref, gate_ref, hf_ref, cw_ref, cb_ref, wa_ref, ba_ref, wx_ref, bx_ref, sp_ref,
         out_ref, ext, a_s, b_s, h_s, carry) = refs
    else:
        (ul_ref, u_ref, ur_ref, cw_ref, cb_ref, wa_ref, ba_ref, wx_ref, bx_ref, sp_ref,
         out_ref, ext, a_s, b_s, carry) = refs
        h_s = out_ref
    step = pl.program_id(0)
    chunk = (n_chunks - 1 - step) if reverse else step
    t_len, bn, c = u_ref.shape

    @pl.when(step == 0)
    def _():
        carry[...] = jnp.zeros_like(carry)

    ext[0:2] = jnp.where(chunk == 0, 0.0, ul_ref[...])
    ext[2:t_len + 2] = u_ref[...]
    ext[t_len + 2:t_len + 3] = jnp.where(chunk == n_chunks - 1, 0.0, ur_ref[...])
    uc = cb_ref[...].reshape(1, 1, c)
    for j in range(4):
        uc = uc + cw_ref[j:j + 1, :].reshape(1, 1, c) * ext[j:j + t_len]

    u2 = uc.reshape(t_len * bn, c)
    ub = u2.astype(BF16)
    half = c // 2

    def gate(w_ref, b_ref):
        z = jnp.concatenate([_dot(ub[:, :half], w_ref[0]), _dot(ub[:, half:], w_ref[1])], axis=-1)
        return 0.5 * jnp.tanh(0.5 * (z + b_ref[...])) + 0.5

    r = gate(wa_ref, ba_ref)
    ig = gate(wx_ref, bx_ref)
    a = jnp.exp((-LRU_C) * r * sp_ref[...])
    b = jnp.sqrt(1.0 - a * a) * (ig * u2)
    a_s[...] = a.reshape(t_len, bn, c)
    b_s[...] = b.reshape(t_len, bn, c)

    def body(k, h):
        t = (t_len - 1 - k) if reverse else k
        h = a_s[t] * h + b_s[t]
        h_s[t] = h
        return h

    carry[...] = lax.fori_loop(0, t_len, body, carry[...], unroll=8)

    if reverse:
        out_ref[...] = ((hf_ref[...] + h_s[...]) * _gelu_tanh(gate_ref[...])).astype(out_ref.dtype)


def _rglru(ug, hf, conv_w, conv_b, wa, ba, wx, bx, lam, bn, *, reverse):
    s_len = ug.shape[0]
    c = LRU_W
    ug3 = ug
    t_len = min(LRU_T, s_len)
    n_chunks = s_len // t_len

    def ck(i):
        return (n_chunks - 1 - i) if reverse else i

    half = c // 2
    nb = LRU_BLOCKS // 2

    def dense_halves(w):
        w = w.astype(F32).reshape(2, nb, c // LRU_BLOCKS, c // LRU_BLOCKS)
        eye = jnp.eye(nb, dtype=F32)
        return jnp.einsum('gnij,nm->gnimj', w, eye).reshape(2, half, half).astype(BF16)

    softplus_neg_lam = jax.nn.softplus(-lam.astype(F32)).reshape(1, c)
    vec = pl.BlockSpec((1, c), lambda i: (0, 0))
    wspec = pl.BlockSpec((2, half, half), lambda i: (0, 0, 0))
    blk = pl.BlockSpec((t_len, bn, c), lambda i: (ck(i), 0, 0))
    in_specs = [
        pl.BlockSpec((2, bn, c), lambda i: (jnp.maximum(ck(i) * (t_len // 2) - 1, 0), 0, 0)),
        blk,
        pl.BlockSpec((1, bn, c), lambda i: (jnp.minimum((ck(i) + 1) * t_len, s_len - 1), 0, 0)),
    ]
    args = [ug3, ug3, ug3]
    scratch = [pltpu.VMEM((t_len + 3, bn, c), F32), pltpu.VMEM((t_len, bn, c), F32),
               pltpu.VMEM((t_len, bn, c), F32)]
    if reverse:
        in_specs += [pl.BlockSpec((t_len, bn, c), lambda i: (ck(i), 0, 1)), blk]
        args += [ug3, hf]
        scratch.append(pltpu.VMEM((t_len, bn, c), F32))
    scratch.append(pltpu.VMEM((bn, c), F32))
    in_specs += [pl.BlockSpec((4, c), lambda i: (0, 0)), vec, wspec, vec, wspec, vec, vec]
    args += [conv_w.astype(F32), conv_b.astype(F32).reshape(1, c), dense_halves(wa),
             ba.astype(F32).reshape(1, c), dense_halves(wx), bx.astype(F32).reshape(1, c), softplus_neg_lam]
    return pl.pallas_call(
        functools.partial(_rglru_kernel, reverse=reverse, n_chunks=n_chunks),
        out_shape=jax.ShapeDtypeStruct((s_len, bn, c), BF16 if reverse else F32),
        grid=(n_chunks,),
        in_specs=in_specs,
        out_specs=blk,
        scratch_shapes=scratch,
        compiler_params=_cparams(("arbitrary",)),
        name="rglru_bw" if reverse else "rglru_fw",
    )(*args)


def _hg_matrices(c, reverse):
    n_levels = int(math.log2(c))
    t = np.arange(c)[:, None]
    r = np.arange(c)[None, :]
    mats, masks, upper = [], [], []
    if not reverse:
        mats.append(r <= t)
        mats.append(r > t)
    else:
        mats.append(r >= t)
        mats.append(r < t)
    for lev in range(n_levels):
        half = c >> (lev + 1)
        parent = 2 * half
        start = (np.arange(c) // parent) * parent
        mid = (start + half)[:, None]
        later = (np.arange(c) % parent >= half)[:, None]
        if not reverse:
            m = np.where(later, (r >= mid) & (r <= t), (r > t) & (r < mid))
            is_q = later
        else:
            m = np.where(later, (r >= mid) & (r < t), (r >= t) & (r < mid))
            is_q = ~later
        mats.append(m)
        same_parent = (start[:, None] == start[None, :])
        masks.append(same_parent & is_q & (~is_q).T)
        upper.append(np.broadcast_to(is_q, (c, HG_DK)))
    mat = np.concatenate(mats, axis=0).astype(np.float32)
    return (np.concatenate([mat, mat], axis=1), np.stack(masks).astype(np.float32),
            np.stack(upper).astype(np.float32))


def _hgrn_kernel(*refs, reverse, n_steps, n_levels):
    if reverse:
        (q_ref, f_ref, v_ref, go_ref, of_ref, lb_ref, gn_ref, mat_ref, mask_ref, isq_ref, out_ref,
         state, x_scr, a_scr, aux_scr, dec_scr, kk_scr) = refs
    else:
        (q_ref, f_ref, v_ref, lb_ref, mat_ref, mask_ref, isq_ref, out_ref,
         state, x_scr, a_scr, aux_scr, dec_scr, kk_scr) = refs
    step = pl.program_id(1)
    c = HG_CHUNK
    t_len = q_ref.shape[0]
    n_chunks = t_len // c

    @pl.when(step == 0)
    def _():
        state[...] = jnp.zeros_like(state)

    def chunk_body(ci, carry):
        ch = (n_chunks - 1 - ci) if reverse else ci
        rows = slice(ch * c, (ch + 1) * c)
        for hp in range(HG_HEADS // 2):
            cols2 = slice(2 * hp * HG_DK, 2 * (hp + 1) * HG_DK)
            fl = f_ref[rows, cols2]
            e = jnp.exp(-jnp.abs(fl))
            r = 1.0 / (1.0 + e)
            er = e * r
            pos = fl > 0.0
            lb = lb_ref[0:1, cols2]
            oml = lb_ref[1:2, cols2]
            g2 = jnp.log2(lb + oml * jnp.where(pos, r, er))
            kk_scr[hp] = oml * jnp.where(pos, er, r)
            g_hi = g2.astype(BF16)
            g_lo = (g2 - g_hi.astype(F32)).astype(BF16)
            dec_scr[hp] = jnp.exp2(_dot(mat_ref[...], jnp.concatenate([g_hi, g_lo], axis=0)))

        edge = (c - 1) if not reverse else 0
        for hd in range(HG_HEADS):
            cols = slice(hd * HG_DK, (hd + 1) * HG_DK)
            pcols = slice((hd % 2) * HG_DK, (hd % 2 + 1) * HG_DK)
            kk = kk_scr[hd // 2, :, pcols]
            dec = dec_scr.at[hd // 2]
            qv = q_ref[rows, cols].astype(F32)
            qq = qv * jax.nn.sigmoid(qv)
            qb, kb = qq.astype(BF16), kk.astype(BF16)
            d_in = dec[0:c, pcols]
            x_scr[hd, n_levels] = (qq * d_in).astype(BF16)
            x_scr[hd, n_levels + 1] = (kk * dec[c:2 * c, pcols]).astype(BF16)
            aux_scr[hd, 0] = jnp.broadcast_to(d_in[edge:edge + 1, :], (c, HG_DK))
            aux_scr[hd, 1] = jnp.broadcast_to(jnp.sum(qq * kk, axis=-1, keepdims=True), (c, HG_DK))
            for lev in range(n_levels):
                half = c >> (lev + 1)
                if half >= SUBLANES_BF16:
                    first_is_q = bool(reverse)
                    base = jnp.concatenate(
                        [(qb if (blk % 2 == 1) != first_is_q else kb)[blk * half:(blk + 1) * half]
                         for blk in range(c // half)], axis=0)
                else:
                    base = jnp.where(isq_ref[lev] > 0.5, qb, kb)
                x_scr[hd, lev] = base * dec[(2 + lev) * c:(3 + lev) * c, pcols].astype(BF16)

        for hd in range(HG_HEADS):
            cols = slice(hd * HG_DK, (hd + 1) * HG_DK)
            for lev in range(n_levels):
                xl = x_scr[hd, lev]
                a_scr[hd, lev] = _nt_dot(xl, xl)
            st = state[hd]
            a_scr[hd, n_levels] = _nt_dot(x_scr[hd, n_levels], st.astype(BF16))
            vt = v_ref[rows, cols].astype(F32).T.astype(BF16)
            state[hd] = st * aux_scr[hd, 0] + _dot(vt, x_scr[hd, n_levels + 1])

        for hd in range(HG_HEADS):
            att = mask_ref[0] * a_scr[hd, 0]
            for lev in range(1, n_levels):
                att = att + mask_ref[lev] * a_scr[hd, lev]
            x_scr[hd, 0] = att.astype(BF16)

        for hd in range(HG_HEADS):
            cols = slice(hd * HG_DK, (hd + 1) * HG_DK)
            vb = v_ref[rows, cols]
            o = a_scr[hd, n_levels] + _dot(x_scr[hd, 0], vb) + aux_scr[hd, 1] * vb.astype(F32)

            if reverse:
                o = o + of_ref[rows, cols]
                o = o * lax.rsqrt(jnp.mean(o * o, axis=-1, keepdims=True) + EPS) * gn_ref[...]
                gv = go_ref[rows, cols].astype(F32)
                out_ref[rows, cols] = (o * (gv * jax.nn.sigmoid(gv))).astype(out_ref.dtype)
            else:
                out_ref[rows, cols] = o
        return carry

    for ci in range(n_chunks):
        chunk_body(ci, 0)


def _hgrn(hg, o_fw, lb, gnorm_g, *, reverse):
    bn, s_len, _ = hg["q"].shape
    w = HG_HEADS * HG_DK
    t_len = min(HG_T, s_len)
    n_steps = s_len // t_len
    mats, masks, isq = _hg_matrices(HG_CHUNK, reverse)
    n_levels = masks.shape[0]
    lbf = lb.astype(F32)
    lb_tab = jnp.stack([lbf, 1.0 - lbf])

    blk = pl.BlockSpec((None, t_len, w), lambda b, i: (b, (n_steps - 1 - i) if reverse else i, 0))

    def const(shape):
        return pl.BlockSpec(shape, lambda b, i: (0,) * len(shape))

    in_specs = [blk, blk, blk]
    args = [hg["q"], hg["f_bw" if reverse else "f_fw"], hg["i_in"]]
    if reverse:
        in_specs += [blk, blk]
        args += [hg["g_out"], o_fw]
    in_specs.append(const((2, w)))
    args.append(lb_tab)
    if reverse:
        in_specs.append(const((1, HG_DK)))
        args.append(gnorm_g.astype(F32).reshape(1, HG_DK))
    in_specs += [const(mats.shape), const(masks.shape), const(isq.shape)]
    args += [jnp.asarray(mats, BF16), jnp.asarray(masks, F32), jnp.asarray(isq, BF16)]
    return pl.pallas_call(
        functools.partial(_hgrn_kernel, reverse=reverse, n_steps=n_steps, n_levels=n_levels),
        out_shape=jax.ShapeDtypeStruct((bn, s_len, w), BF16 if reverse else F32),
        grid=(bn, n_steps),
        in_specs=in_specs,
        out_specs=blk,
        scratch_shapes=[pltpu.VMEM((HG_HEADS, HG_DK, HG_DK), F32),
                        pltpu.VMEM((HG_HEADS, n_levels + 2, HG_CHUNK, HG_DK), BF16),
                        pltpu.VMEM((HG_HEADS, n_levels + 1, HG_CHUNK, HG_CHUNK), F32),
                        pltpu.VMEM((HG_HEADS, 2, HG_CHUNK, HG_DK), F32),
                        pltpu.VMEM((HG_HEADS // 2, (n_levels + 2) * HG_CHUNK, 2 * HG_DK), F32),
                        pltpu.VMEM((HG_HEADS // 2, HG_CHUNK, 2 * HG_DK), F32)],
        compiler_params=_cparams(("parallel", "arbitrary")),
        name="hgrn_bw" if reverse else "hgrn_fw",
    )(*args)


def _od_out_kernel(yc_ref, yd_ref, w_ref, x_ref, out_ref):
    half = yc_ref.shape[-1]
    out_ref[...] = x_ref[...] + _dot(yc_ref[...], w_ref[0:half, :]) + _dot(yd_ref[...], w_ref[half:, :])


def _od_out(x, yc_tm, yd, w_bf16):
    bn, s_len, d = x.shape
    tm = min(ROW_TILE_WIDE, s_len)
    c = yd.shape[-1]
    x_spec = pl.BlockSpec((None, tm, d), lambda b, i: (b, i, 0))
    return pl.pallas_call(
        _od_out_kernel,
        out_shape=jax.ShapeDtypeStruct(x.shape, F32),
        grid=(bn, s_len // tm),
        in_specs=[pl.BlockSpec((tm, c), lambda b, i: (i, b)),
                  pl.BlockSpec((None, tm, c), lambda b, i: (b, i, 0)),
                  pl.BlockSpec((2 * c, d), lambda b, i: (0, 0)),
                  x_spec],
        out_specs=x_spec,
        compiler_params=_cparams(("parallel", "parallel")),
        name="od_out",
    )(yc_tm, yd, w_bf16, x)


def _even_layer(x, g, w_in, w_out, rpb):
    qkv, view4, view16 = _ev_proj(x, g, w_in.astype(BF16))
    outs, maxes, dens = zip(*[_dilated_branch(view, dil) for dil, view in zip(DILATIONS, (qkv, view4, view16))])
    yb = _neighborhood(qkv, rpb)
    return _ev_out(x, outs, maxes, dens, yb, w_out.astype(BF16))


def _odd_layer(x, g, w_in, w_out, conv_w, conv_b, wa, ba, wx, bx, lam, lb_f, lb_b, gnorm_g):
    bn, s_len, _ = x.shape
    ug, hg = _od_proj(x, g, w_in.astype(BF16))
    h_fw = _rglru(ug, None, conv_w, conv_b, wa[0], ba[0], wx[0], bx[0], lam[0], bn, reverse=False)
    yc = _rglru(ug, h_fw, conv_w, conv_b, wa[1], ba[1], wx[1], bx[1], lam[1], bn, reverse=True)
    o_fw = _hgrn(hg, None, lb_f, gnorm_g, reverse=False)
    yd = _hgrn(hg, o_fw, lb_b, gnorm_g, reverse=True)
    return _od_out(x, yc.reshape(s_len, bn * LRU_W), yd, w_out.astype(BF16))


def kernel(x, mem, norm_mix_g, norm_xa_g, norm_mem_g, norm_mlp_g, final_norm_g, ev_w_in, ev_w_out, na_rpb,
           od_w_in, od_w_out, conv_w, conv_b, lru_wa, lru_ba, lru_wx, lru_bx, lru_lambda, hgrn_lb_logits,
           hgrn_norm_g, xa_wq, xa_wkv, xa_wo, mlp_w1, mlp_w2):
    depth = norm_mix_g.shape[0]
    p_lb = jax.nn.softmax(hgrn_lb_logits.astype(F32), axis=0)
    lower_bounds = jnp.cumsum(p_lb, axis=0) - p_lb[0:1]
    for layer in range(depth):
        if layer % 2 == 0:
            e = layer // 2
            x = _even_layer(x, norm_mix_g[layer], ev_w_in[e], ev_w_out[e], na_rpb[e])
        else:
            o = layer // 2
            x = _odd_layer(x, norm_mix_g[layer], od_w_in[o], od_w_out[o], conv_w[o], conv_b[o],
                           lru_wa[o], lru_ba[o], lru_wx[o], lru_bx[o], lru_lambda[o],
                           lower_bounds[layer, 0], lower_bounds[layer, 1], hgrn_norm_g[o])
        kv = _mem_kv(mem, norm_mem_g[layer], xa_wkv[layer].astype(BF16))
        x = _xattn(x, norm_xa_g[layer], xa_wq[layer].astype(BF16), kv, xa_wo[layer].astype(BF16))
        x = _mlp(x, norm_mlp_g[layer], mlp_w1[layer].astype(BF16), mlp_w2[layer].astype(BF16),
                 g_final=final_norm_g if layer == depth - 1 else None)
    return x
```

```python
import functools
import math

import jax
import jax.numpy as jnp
import numpy as np
from jax import lax
from jax.experimental import pallas as pl
from jax.experimental.pallas import tpu as pltpu

F32 = jnp.float32
BF16 = jnp.bfloat16

D_MODEL = 1024
HEAD_DIM = 64
ROT_DIM = 16
ROPE_THETA = 500000.0
N_HEADS_A = 8
N_HEADS_B = 8
GROUP_W = N_HEADS_A * HEAD_DIM
DILATIONS = (1, 4, 16)
WIN_HALF = 64
GRID_W = 64
NA_ROWS = 8
NA_COLS = 16
LRU_W = 512
LRU_BLOCKS = 8
LRU_C = 8.0
HG_HEADS = 4
HG_DK = 128
XA_HEADS = 4
XA_DH = 256
D_FF = 4096
EPS = 1e-6
LOG2E = math.log2(math.e)

LANES = 128
SUBLANES_BF16 = 16
V7X_VMEM_LIMIT_BYTES = 56 * 1024 * 1024

ROW_TILE = 512
ROW_TILE_WIDE = 1024
ATT_Q_BLOCK = 512
ATT_SUB = 128
NA_ROW_GROUP = 8
LRU_T = 128
HG_CHUNK = 128
HG_T = 512


def _cparams(sem):
    return pltpu.CompilerParams(dimension_semantics=sem, vmem_limit_bytes=V7X_VMEM_LIMIT_BYTES)


def _resident(shape, index_map):
    return pl.BlockSpec(shape, index_map, pipeline_mode=pl.Buffered(1))


def _rms(x, g):
    return x * lax.rsqrt(jnp.mean(x * x, axis=-1, keepdims=True) + EPS) * g


def _nt_dot(a, b):
    return lax.dot_general(a, b, (((1,), (1,)), ((), ())), preferred_element_type=F32)


def _dot(a, b):
    return jnp.dot(a, b, preferred_element_type=F32)


def _head_selectors():
    lane = lax.broadcasted_iota(jnp.int32, (1, LANES), 1)
    low = (lane < HEAD_DIM).astype(F32)
    return low.astype(BF16), (1.0 - low).astype(BF16)


def _ev_proj_kernel(x_ref, g_ref, w_ref, rc_ref, rs1_ref, rs2_ref, o_ref, o4_ref, o16_ref, slab):
    h = _rms(x_ref[...], g_ref[...]).astype(BF16)
    scale = HEAD_DIM ** -0.5 * LOG2E
    tm = x_ref.shape[0]
    n_slabs = GROUP_W // LANES
    for c in range(6):
        acc = _dot(h, w_ref[:, c * GROUP_W:(c + 1) * GROUP_W])
        if c in (0, 1):
            rc, rs1, rs2 = rc_ref[...], rs1_ref[...], rs2_ref[...]
            parts = []
            for j in range(n_slabs):
                t = acc[:, j * LANES:(j + 1) * LANES]
                t = t * rc + pltpu.roll(t, LANES - ROT_DIM // 2, 1) * rs1 + pltpu.roll(t, ROT_DIM // 2, 1) * rs2
                parts.append(t)
            acc = jnp.concatenate(parts, axis=-1)
        if c in (0, 3):
            acc = acc * scale
        o_ref[:, c * GROUP_W:(c + 1) * GROUP_W] = acc.astype(BF16)
        if c < 3:
            for j in range(n_slabs):
                slab[j] = acc[:, j * LANES:(j + 1) * LANES]
            for dil, dst in ((DILATIONS[1], o4_ref), (DILATIONS[2], o16_ref)):
                for r in range(dil):
                    for j in range(n_slabs):
                        col = r * 3 * GROUP_W + c * GROUP_W + j * LANES
                        dst[:, col:col + LANES] = slab[j, pl.ds(r, tm // dil, stride=dil), :].astype(BF16)


def _rope_tables(s_len):
    half = ROT_DIM // 2
    inv = jnp.asarray(ROPE_THETA ** (-np.arange(half) * 2.0 / ROT_DIM), F32)
    ang = jnp.arange(s_len, dtype=F32)[:, None] * inv[None, :]
    cos, sin = jnp.cos(ang), jnp.sin(ang)
    ones = jnp.ones((s_len, HEAD_DIM - ROT_DIM), F32)
    zeros = jnp.zeros((s_len, HEAD_DIM - ROT_DIM), F32)
    zh = jnp.zeros((s_len, half), F32)
    rc = jnp.concatenate([cos, cos, ones], axis=-1)
    rs1 = jnp.concatenate([-sin, zh, zeros], axis=-1)
    rs2 = jnp.concatenate([zh, sin, zeros], axis=-1)
    rep = LANES // HEAD_DIM
    return jnp.tile(rc, (1, rep)), jnp.tile(rs1, (1, rep)), jnp.tile(rs2, (1, rep))


def _ev_proj(x, g, w_bf16):
    bn, s_len, d = x.shape
    tm = min(ROW_TILE_WIDE, s_len)
    n_out = w_bf16.shape[1]
    rc, rs1, rs2 = _rope_tables(s_len)
    tbl_spec = pl.BlockSpec((tm, LANES), lambda b, i: (i, 0))
    wa = 3 * GROUP_W
    d4, d16 = DILATIONS[1], DILATIONS[2]
    return pl.pallas_call(
        _ev_proj_kernel,
        out_shape=(jax.ShapeDtypeStruct((bn, s_len, n_out), BF16),
                   jax.ShapeDtypeStruct((bn, s_len // d4, d4 * wa), BF16),
                   jax.ShapeDtypeStruct((bn, s_len // d16, d16 * wa), BF16)),
        grid=(bn, s_len // tm),
        in_specs=[
            pl.BlockSpec((None, tm, d), lambda b, i: (b, i, 0)),
            pl.BlockSpec((1, d), lambda b, i: (0, 0)),
            _resident((d, n_out), lambda b, i: (0, 0)),
            tbl_spec, tbl_spec, tbl_spec,
        ],
        out_specs=(pl.BlockSpec((None, tm, n_out), lambda b, i: (b, i, 0)),
                   pl.BlockSpec((None, tm // d4, d4 * wa), lambda b, i: (b, i, 0)),
                   pl.BlockSpec((None, tm // d16, d16 * wa), lambda b, i: (b, i, 0))),
        scratch_shapes=[pltpu.VMEM((GROUP_W // LANES, tm, LANES), F32)],
        compiler_params=_cparams(("parallel", "parallel")),
        name="ev_proj",
    )(x, g.reshape(1, d), w_bf16, rc, rs1, rs2)


def _dilated_kernel(q_ref, kl_ref, kc_ref, kr_ref, vl_ref, vc_ref, vr_ref, o_ref, max_ref, den_ref,
                    kext, vext, s_scr, p_scr, *, lq, l_total):
    i = pl.program_id(2)
    kext[0:WIN_HALF, :] = kl_ref[...]
    kext[WIN_HALF:WIN_HALF + lq, :] = kc_ref[...]
    kext[WIN_HALF + lq:, :] = kr_ref[...]
    vext[0:WIN_HALF, :] = vl_ref[...]
    vext[WIN_HALF:WIN_HALF + lq, :] = vc_ref[...]
    vext[WIN_HALF + lq:, :] = vr_ref[...]

    wk = ATT_SUB + 2 * WIN_HALF
    qi = lax.broadcasted_iota(jnp.int32, (ATT_SUB, wk), 0)
    ci = lax.broadcasted_iota(jnp.int32, (ATT_SUB, wk), 1)
    band_bias = jnp.where((ci - qi >= 0) & (ci - qi <= 2 * WIN_HALF), 0.0, -jnp.inf).astype(F32)
    crow = lax.broadcasted_iota(jnp.int32, (1, wk), 1)
    lane = lax.broadcasted_iota(jnp.int32, (ATT_SUB, LANES), 1)
    low_half = lane < HEAD_DIM
    sel_lo, sel_hi = _head_selectors()
    n_sub = lq // ATT_SUB
    n_hp = GROUP_W // LANES

    for j in range(n_sub):
        base = i * lq + j * ATT_SUB - WIN_HALF
        in_seq = (crow + base >= 0) & (crow + base < l_total)
        bias = band_bias + jnp.where(in_seq, 0.0, -jnp.inf).astype(F32)
        bias2 = jnp.concatenate([bias, bias], axis=0)
        rows = slice(j * ATT_SUB, (j + 1) * ATT_SUB)
        wrows = slice(j * ATT_SUB, j * ATT_SUB + wk)
        for hp in range(n_hp):
            cols = slice(hp * LANES, (hp + 1) * LANES)
            qp = q_ref[rows, cols]
            q2 = jnp.concatenate([qp * sel_lo, qp * sel_hi], axis=0)
            s_scr[j * n_hp + hp] = _nt_dot(q2, kext[wrows, cols]) + bias2
    for j in range(n_sub):
        m_tile = jnp.zeros((ATT_SUB, LANES), F32)
        den_tile = jnp.ones((ATT_SUB, LANES), F32)
        for hp in range(n_hp):
            t = j * n_hp + hp
            s = s_scr[t]
            m = jnp.max(s, axis=-1, keepdims=True)
            p = jnp.exp2(s - m)
            den = jnp.sum(p, axis=-1, keepdims=True)
            p_scr[t] = p.astype(BF16)
            is_lo, is_hi = lane == 2 * hp, lane == 2 * hp + 1
            m_tile = jnp.where(is_lo, m[0:ATT_SUB], jnp.where(is_hi, m[ATT_SUB:], m_tile))
            den_tile = jnp.where(is_lo, den[0:ATT_SUB], jnp.where(is_hi, den[ATT_SUB:], den_tile))
        max_ref[j * ATT_SUB:(j + 1) * ATT_SUB, :] = m_tile
        den_ref[j * ATT_SUB:(j + 1) * ATT_SUB, :] = den_tile
    for j in range(n_sub):
        rows = slice(j * ATT_SUB, (j + 1) * ATT_SUB)
        wrows = slice(j * ATT_SUB, j * ATT_SUB + wk)
        for hp in range(n_hp):
            cols = slice(hp * LANES, (hp + 1) * LANES)
            o = _dot(p_scr[j * n_hp + hp], vext[wrows, cols])
            o_ref[rows, cols] = jnp.where(low_half, o[0:ATT_SUB], o[ATT_SUB:]).astype(BF16)


def _dilated_branch(view, dil):
    bn, l_total, width = view.shape
    n_groups = width // (dil * GROUP_W)
    lq = min(ATT_Q_BLOCK, l_total)
    nblk = l_total // lq
    per = lq // WIN_HALF
    n_halo = l_total // WIN_HALF
    n_units = (lq // ATT_SUB) * (GROUP_W // LANES)

    def cur(g):
        return pl.BlockSpec((None, lq, GROUP_W), lambda b, r, i: (b, i, r * n_groups + g))

    def left(g):
        return pl.BlockSpec((None, WIN_HALF, GROUP_W),
                            lambda b, r, i: (b, jnp.maximum(i * per - 1, 0), r * n_groups + g))

    def right(g):
        return pl.BlockSpec((None, WIN_HALF, GROUP_W),
                            lambda b, r, i: (b, jnp.minimum((i + 1) * per, n_halo - 1), r * n_groups + g))

    stat_shape = jax.ShapeDtypeStruct((bn, l_total, dil * LANES), F32)
    stat_spec = pl.BlockSpec((None, lq, LANES), lambda b, r, i: (b, i, r))
    return pl.pallas_call(
        functools.partial(_dilated_kernel, lq=lq, l_total=l_total),
        out_shape=(jax.ShapeDtypeStruct((bn, l_total, dil * GROUP_W), BF16), stat_shape, stat_shape),
        grid=(bn, dil, nblk),
        in_specs=[cur(0), left(1), cur(1), right(1), left(2), cur(2), right(2)],
        out_specs=(pl.BlockSpec((None, lq, GROUP_W), lambda b, r, i: (b, i, r)), stat_spec, stat_spec),
        scratch_shapes=[pltpu.VMEM((lq + 2 * WIN_HALF, GROUP_W), BF16),
                        pltpu.VMEM((lq + 2 * WIN_HALF, GROUP_W), BF16),
                        pltpu.VMEM((n_units, 2 * ATT_SUB, ATT_SUB + 2 * WIN_HALF), F32),
                        pltpu.VMEM((n_units, 2 * ATT_SUB, ATT_SUB + 2 * WIN_HALF), BF16)],
        compiler_params=_cparams(("parallel", "parallel", "parallel")),
        name=f"dilated_attn_d{dil}",
    )(view, view, view, view, view, view, view)


NA_QT = NA_ROW_GROUP * GRID_W
NA_KT = 2 * NA_QT
NA_KW = NA_ROWS * GRID_W


def _na_tables(rpb):
    rpb = rpb.astype(F32)
    nh = rpb.shape[0]
    ext = GRID_W - NA_COLS
    padded = jnp.concatenate([jnp.repeat(rpb[..., :1], ext, axis=-1), rpb,
                              jnp.repeat(rpb[..., -1:], ext, axis=-1)], axis=-1)
    t1 = jnp.stack([padded[..., GRID_W - 1 - qc:2 * GRID_W - 1 - qc] for qc in range(GRID_W)], axis=2)
    qc = np.arange(GRID_W)[:, None]
    kc = np.arange(GRID_W)[None, :]
    c0 = np.clip(qc - NA_COLS // 2, 0, GRID_W - NA_COLS)
    t1 = jnp.where((kc >= c0) & (kc < c0 + NA_COLS), t1 * LOG2E, -jnp.inf)
    per_delta = [jnp.concatenate([t1[:, kr - delta + NA_ROWS - 1] for kr in range(NA_ROWS)], axis=-1)
                 for delta in range(NA_ROWS)]
    return jnp.stack(per_delta).reshape(NA_ROWS, nh // 2, 2 * GRID_W, NA_KW)


def _na_kernel(q_ref, k_ref, v_ref, tb_ref, o_ref, s_scr, p_scr, inv_scr, *, rows):
    i0 = pl.program_id(1) * NA_ROW_GROUP
    w0 = jnp.clip(i0 - NA_ROWS // 2, 0, rows - 2 * NA_ROW_GROUP)
    lane = lax.broadcasted_iota(jnp.int32, (GRID_W, LANES), 1)
    low_half = lane < HEAD_DIM
    sel_lo, sel_hi = _head_selectors()
    n_hp = GROUP_W // LANES

    def key_rows(a):
        r0 = jnp.clip(i0 + a - NA_ROWS // 2, 0, rows - NA_ROWS)
        return pl.ds(pl.multiple_of((r0 - w0) * GRID_W, GRID_W), NA_KW), i0 + a - r0

    for a in range(NA_ROW_GROUP):
        krows, delta = key_rows(a)
        qrows = slice(a * GRID_W, (a + 1) * GRID_W)
        for hp in range(n_hp):
            cols = slice(hp * LANES, (hp + 1) * LANES)
            qp = q_ref[qrows, cols]
            q2 = jnp.concatenate([qp * sel_lo, qp * sel_hi], axis=0)
            s_scr[a * n_hp + hp] = _nt_dot(q2, k_ref[0, krows, cols]) + tb_ref[delta, hp]
    for t in range(NA_ROW_GROUP * n_hp):
        s = s_scr[t]
        p = jnp.exp2(s - jnp.max(s, axis=-1, keepdims=True))
        inv_scr[t] = jnp.broadcast_to(1.0 / jnp.sum(p, axis=-1, keepdims=True), (2 * GRID_W, LANES))
        p_scr[t] = p.astype(BF16)
    for a in range(NA_ROW_GROUP):
        krows, _ = key_rows(a)
        qrows = slice(a * GRID_W, (a + 1) * GRID_W)
        for hp in range(n_hp):
            cols = slice(hp * LANES, (hp + 1) * LANES)
            t = a * n_hp + hp
            o = _dot(p_scr[t], v_ref[0, krows, cols]) * inv_scr[t]
            o_ref[qrows, cols] = jnp.where(low_half, o[0:GRID_W], o[GRID_W:]).astype(BF16)


def _neighborhood(qkv, rpb):
    bn, s_len, _ = qkv.shape
    rows = s_len // GRID_W
    n_groups = rows // NA_ROW_GROUP
    tables = _na_tables(rpb)

    def window(g):
        return pl.BlockSpec(
            (pl.Element(1), pl.Element(NA_KT), pl.Element(GROUP_W)),
            lambda b, i: (b, jnp.clip(i * NA_ROW_GROUP - NA_ROWS // 2, 0, rows - 2 * NA_ROW_GROUP) * GRID_W,
                          g * GROUP_W))

    return pl.pallas_call(
        functools.partial(_na_kernel, rows=rows),
        out_shape=jax.ShapeDtypeStruct((bn, s_len, GROUP_W), BF16),
        grid=(bn, n_groups),
        in_specs=[pl.BlockSpec((None, NA_QT, GROUP_W), lambda b, i: (b, i, 3)), window(4), window(5),
                  pl.BlockSpec(tables.shape, lambda b, i: (0, 0, 0, 0))],
        out_specs=pl.BlockSpec((None, NA_QT, GROUP_W), lambda b, i: (b, i, 0)),
        scratch_shapes=[pltpu.VMEM((NA_ROW_GROUP * GROUP_W // LANES, 2 * GRID_W, NA_KW), F32),
                        pltpu.VMEM((NA_ROW_GROUP * GROUP_W // LANES, 2 * GRID_W, NA_KW), BF16),
                        pltpu.VMEM((NA_ROW_GROUP * GROUP_W // LANES, 2 * GRID_W, LANES), F32)],
        compiler_params=_cparams(("parallel", "parallel")),
        name="neighborhood_attn",
    )(qkv, qkv, qkv, tables)


def _ev_out_kernel(o1, o2v, o3v, m1, m2v, m3v, d1, d2v, d3v, yb_ref, ex_ref, w_ref, x_ref, out_ref,
                   so2, so3, st2, st3):
    tm = x_ref.shape[0]
    n_slabs = GROUP_W // LANES
    for dil, ov, mv, dv, so, st in ((DILATIONS[1], o2v, m2v, d2v, so2, st2), (DILATIONS[2], o3v, m3v, d3v, so3, st3)):
        for r in range(dil):
            dst = pl.ds(r, tm // dil, stride=dil)
            for j in range(n_slabs):
                col = r * GROUP_W + j * LANES
                so[j, dst, :] = ov[:, col:col + LANES].astype(F32)
            st[0, dst, :] = mv[:, r * LANES:(r + 1) * LANES]
            st[1, dst, :] = dv[:, r * LANES:(r + 1) * LANES]
    o2 = jnp.concatenate([so2[j] for j in range(n_slabs)], axis=-1)
    o3 = jnp.concatenate([so3[j] for j in range(n_slabs)], axis=-1)
    ma, mb, mc = m1[...], st2[0], st3[0]
    m = jnp.maximum(jnp.maximum(ma, mb), mc)
    ea, eb, ec = jnp.exp2(ma - m), jnp.exp2(mb - m), jnp.exp2(mc - m)
    inv = 1.0 / (ea * d1[...] + eb * st2[1] + ec * st3[1])
    ex = ex_ref[...]

    def expand(w):
        hi = w.astype(BF16)
        lo = (w - hi.astype(F32)).astype(BF16)
        return _dot(hi, ex) + _dot(lo, ex)

    ya = expand(ea * inv) * o1[...].astype(F32) + expand(eb * inv) * o2 + expand(ec * inv) * o3
    y = _dot(ya.astype(BF16), w_ref[0:GROUP_W, :]) + _dot(yb_ref[...], w_ref[GROUP_W:, :])
    out_ref[...] = x_ref[...] + y


def _ev_out(x, outs, maxes, dens, yb, w_bf16):
    bn, s_len, d = x.shape
    tm = min(ROW_TILE, s_len)
    ex = np.zeros((LANES, GROUP_W), np.float32)
    for h in range(N_HEADS_A):
        ex[h, h * HEAD_DIM:(h + 1) * HEAD_DIM] = 1.0
    def o_spec(dil):
        return pl.BlockSpec((None, tm // dil, dil * GROUP_W), lambda b, i: (b, i, 0))

    def l_spec(dil):
        return pl.BlockSpec((None, tm // dil, dil * LANES), lambda b, i: (b, i, 0))

    x_spec = pl.BlockSpec((None, tm, d), lambda b, i: (b, i, 0))
    n_slabs = GROUP_W // LANES
    return pl.pallas_call(
        _ev_out_kernel,
        out_shape=jax.ShapeDtypeStruct(x.shape, F32),
        grid=(bn, s_len // tm),
        in_specs=[o_spec(d_) for d_ in DILATIONS] + 2 * [l_spec(d_) for d_ in DILATIONS]
        + [o_spec(1),
           pl.BlockSpec((LANES, GROUP_W), lambda b, i: (0, 0)),
           pl.BlockSpec((2 * GROUP_W, d), lambda b, i: (0, 0)),
           x_spec],
        out_specs=x_spec,
        scratch_shapes=[pltpu.VMEM((n_slabs, tm, LANES), F32), pltpu.VMEM((n_slabs, tm, LANES), F32),
                        pltpu.VMEM((2, tm, LANES), F32), pltpu.VMEM((2, tm, LANES), F32)],
        compiler_params=_cparams(("parallel", "parallel")),
        name="ev_out",
    )(*outs, *maxes, *dens, yb, jnp.asarray(ex, BF16), w_bf16, x)


def _norm_matmul_kernel(x_ref, g_ref, w_ref, o_ref):
    o_ref[...] = _dot(_rms(x_ref[...], g_ref[...]).astype(BF16), w_ref[...]).astype(o_ref.dtype)


def _mem_kv(mem, g, w_bf16):
    bn, m_len, d = mem.shape
    n_out = w_bf16.shape[1]
    return pl.pallas_call(
        _norm_matmul_kernel,
        out_shape=jax.ShapeDtypeStruct((bn, m_len, n_out), BF16),
        grid=(bn,),
        in_specs=[pl.BlockSpec((None, m_len, d), lambda b: (b, 0, 0)),
                  pl.BlockSpec((1, d), lambda b: (0, 0)),
                  pl.BlockSpec((d, n_out), lambda b: (0, 0))],
        out_specs=pl.BlockSpec((None, m_len, n_out), lambda b: (b, 0, 0)),
        compiler_params=_cparams(("parallel",)),
        name="mem_kv",
    )(mem, g.reshape(1, d), w_bf16)


def _xattn_kernel(x_ref, g_ref, wq_ref, kv_ref, wo_ref, out_ref):
    x = x_ref[...]
    q = (_dot(_rms(x, g_ref[...]).astype(BF16), wq_ref[...]) * (XA_DH ** -0.5 * LOG2E)).astype(BF16)
    heads = []
    for hd in range(XA_HEADS):
        cols = slice(hd * XA_DH, (hd + 1) * XA_DH)
        s = _nt_dot(q[:, cols], kv_ref[:, cols])
        m = jnp.max(s, axis=-1, keepdims=True)
        p = jnp.exp2(s - m)
        den = jnp.sum(p, axis=-1, keepdims=True)
        vcols = slice(D_MODEL + hd * XA_DH, D_MODEL + (hd + 1) * XA_DH)
        heads.append((_dot(p.astype(BF16), kv_ref[:, vcols]) * (1.0 / den)).astype(BF16))
    o = jnp.concatenate(heads, axis=-1)
    out_ref[...] = x + _dot(o, wo_ref[...])


def _xattn(x, g, wq, kv, wo):
    bn, s_len, d = x.shape
    tm = min(ROW_TILE_WIDE, s_len)
    m_len = kv.shape[1]
    x_spec = pl.BlockSpec((None, tm, d), lambda b, i: (b, i, 0))
    return pl.pallas_call(
        _xattn_kernel,
        out_shape=jax.ShapeDtypeStruct(x.shape, F32),
        grid=(bn, s_len // tm),
        in_specs=[x_spec,
                  pl.BlockSpec((1, d), lambda b, i: (0, 0)),
                  pl.BlockSpec((d, d), lambda b, i: (0, 0)),
                  pl.BlockSpec((None, m_len, 2 * d), lambda b, i: (b, 0, 0)),
                  pl.BlockSpec((d, d), lambda b, i: (0, 0))],
        out_specs=x_spec,
        compiler_params=_cparams(("parallel", "parallel")),
        name="mem_xattn",
    )(x, g.reshape(1, d), wq, kv, wo)


def _mlp_kernel(x_ref, g_ref, w1_ref, w2_ref, gf_ref, out_ref, *, final_norm):
    x = x_ref[...]
    h = _rms(x, g_ref[...]).astype(BF16)
    acc = x
    for c in range(D_FF // D_MODEL):
        cols = slice(c * D_MODEL, (c + 1) * D_MODEL)
        a = jnp.maximum(_dot(h, w1_ref[:, cols]), 0.0)
        acc = acc + _dot((a * a).astype(BF16), w2_ref[cols, :])
    out_ref[...] = _rms(acc, gf_ref[...]) if final_norm else acc


def _mlp(x, g, w1, w2, g_final=None):
    bn, s_len, d = x.shape
    tm = min(ROW_TILE_WIDE, s_len)
    final_norm = g_final is not None
    gf = (g_final if final_norm else g).reshape(1, d)
    x_spec = pl.BlockSpec((None, tm, d), lambda b, i: (b, i, 0))
    vec = pl.BlockSpec((1, d), lambda b, i: (0, 0))
    return pl.pallas_call(
        functools.partial(_mlp_kernel, final_norm=final_norm),
        out_shape=jax.ShapeDtypeStruct(x.shape, F32),
        grid=(bn, s_len // tm),
        in_specs=[x_spec, vec,
                  _resident((d, D_FF), lambda b, i: (0, 0)),
                  _resident((D_FF, d), lambda b, i: (0, 0)),
                  vec],
        out_specs=x_spec,
        compiler_params=_cparams(("parallel", "parallel")),
        name="mlp_final" if final_norm else "mlp",
    )(x, g.reshape(1, d), w1, w2, gf)


def _od_proj_kernel(x_ref, g_ref, w_ref, ug_ref, *rest):
    hg_refs, slab = rest[:-1], rest[-1]
    bn, tq, d = x_ref.shape
    h = _rms(x_ref[...].reshape(bn * tq, d), g_ref[...]).astype(BF16)
    n_ug = 2 * LRU_W
    n_slabs = GROUP_W // LANES
    for c in range(n_ug // GROUP_W):
        acc = _dot(h, w_ref[:, c * GROUP_W:(c + 1) * GROUP_W])
        for b in range(bn):
            for j in range(n_slabs):
                slab[j, pl.ds(b, tq, stride=bn), :] = acc[b * tq:(b + 1) * tq, j * LANES:(j + 1) * LANES]
        for j in range(n_slabs):
            ug_ref[:, c * GROUP_W + j * LANES:c * GROUP_W + (j + 1) * LANES] = slab[j]
    for c, hg_ref in enumerate(hg_refs):
        acc = _dot(h, w_ref[:, n_ug + c * GROUP_W:n_ug + (c + 1) * GROUP_W])
        hg_ref[...] = acc.reshape(bn, tq, GROUP_W).astype(hg_ref.dtype)


HG_INPUT_DTYPES = (("q", BF16), ("f_fw", F32), ("f_bw", F32), ("i_in", BF16), ("g_out", BF16))


def _od_proj(x, g, w_bf16):
    bn, s_len, d = x.shape
    tq = min(ROW_TILE // bn, s_len)
    n_ug = 2 * LRU_W
    n_hg = w_bf16.shape[1] - n_ug
    assert n_hg == len(HG_INPUT_DTYPES) * GROUP_W
    hg_spec = pl.BlockSpec((bn, tq, GROUP_W), lambda i: (0, i, 0))
    ug, *hg = pl.pallas_call(
        _od_proj_kernel,
        out_shape=[jax.ShapeDtypeStruct((s_len * bn, n_ug), F32)]
        + [jax.ShapeDtypeStruct((bn, s_len, GROUP_W), dt) for _, dt in HG_INPUT_DTYPES],
        grid=(s_len // tq,),
        in_specs=[pl.BlockSpec((bn, tq, d), lambda i: (0, i, 0)),
                  pl.BlockSpec((1, d), lambda i: (0, 0)),
                  pl.BlockSpec((d, n_ug + n_hg), lambda i: (0, 0))],
        out_specs=[pl.BlockSpec((tq * bn, n_ug), lambda i: (i, 0))] + [hg_spec] * len(HG_INPUT_DTYPES),
        scratch_shapes=[pltpu.VMEM((GROUP_W // LANES, tq * bn, LANES), F32)],
        compiler_params=_cparams(("parallel",)),
        name="od_proj",
    )(x, g.reshape(1, d), w_bf16)
    return ug.reshape(s_len, bn, n_ug), dict(zip([n for n, _ in HG_INPUT_DTYPES], hg))


def _gelu_tanh(x):
    return 0.5 * x * (1.0 + jnp.tanh(math.sqrt(2.0 / math.pi) * (x + 0.044715 * (x * x * x))))


def _rglru_kernel(*refs, reverse, n_chunks):
    if reverse:
        (ul_ref, u_ref, ur_ref, gate_ref, hf_ref, cw_ref, cb_ref, wa_ref, ba_ref, wx_ref, bx_ref, sp_ref,
         out_ref, ext, a_s, b_s, h_s, carry) = refs
    else:
        (ul_ref, u_ref, ur_ref, cw_ref, cb_ref, wa_ref, ba_ref, wx_ref, bx_ref, sp_ref,
         out_ref, ext, a_s, b_s, carry) = refs
        h_s = out_ref
    step = pl.program_id(0)
    chunk = (n_chunks - 1 - step) if reverse else step
    t_len, bn, c = u_ref.shape

    @pl.when(step == 0)
    def _():
        carry[...] = jnp.zeros_like(carry)

    ext[0:2] = jnp.where(chunk == 0, 0.0, ul_ref[...])
    ext[2:t_len + 2] = u_ref[...]
    ext[t_len + 2:t_len + 3] = jnp.where(chunk == n_chunks - 1, 0.0, ur_ref[...])
    uc = cb_ref[...].reshape(1, 1, c)
    for j in range(4):
        uc = uc + cw_ref[j:j + 1, :].reshape(1, 1, c) * ext[j:j + t_len]

    u2 = uc.reshape(t_len * bn, c)
    ub = u2.astype(BF16)
    half = c // 2

    def gate(w_ref, b_ref):
        z = jnp.concatenate([_dot(ub[:, :half], w_ref[0]), _dot(ub[:, half:], w_ref[1])], axis=-1)
        return 0.5 * jnp.tanh(0.5 * (z + b_ref[...])) + 0.5

    r = gate(wa_ref, ba_ref)
    ig = gate(wx_ref, bx_ref)
    a = jnp.exp((-LRU_C) * r * sp_ref[...])
    b = jnp.sqrt(1.0 - a * a) * (ig * u2)
    a_s[...] = a.reshape(t_len, bn, c)
    b_s[...] = b.reshape(t_len, bn, c)

    def body(k, h):
        t = (t_len - 1 - k) if reverse else k
        h = a_s[t] * h + b_s[t]
        h_s[t] = h
        return h

    carry[...] = lax.fori_loop(0, t_len, body, carry[...], unroll=8)

    if reverse:
        out_ref[...] = ((hf_ref[...] + h_s[...]) * _gelu_tanh(gate_ref[...])).astype(out_ref.dtype)


def _rglru(ug, hf, conv_w, conv_b, wa, ba, wx, bx, lam, bn, *, reverse):
    s_len = ug.shape[0]
    c = LRU_W
    ug3 = ug
    t_len = min(LRU_T, s_len)
    n_chunks = s_len // t_len

    def ck(i):
        return (n_chunks - 1 - i) if reverse else i

    half = c // 2
    nb = LRU_BLOCKS // 2

    def dense_halves(w):
        w = w.astype(F32).reshape(2, nb, c // LRU_BLOCKS, c // LRU_BLOCKS)
        eye = jnp.eye(nb, dtype=F32)
        return jnp.einsum('gnij,nm->gnimj', w, eye).reshape(2, half, half).astype(BF16)

    softplus_neg_lam = jax.nn.softplus(-lam.astype(F32)).reshape(1, c)
    vec = pl.BlockSpec((1, c), lambda i: (0, 0))
    wspec = pl.BlockSpec((2, half, half), lambda i: (0, 0, 0))
    blk = pl.BlockSpec((t_len, bn, c), lambda i: (ck(i), 0, 0))
    in_specs = [
        pl.BlockSpec((2, bn, c), lambda i: (jnp.maximum(ck(i) * (t_len // 2) - 1, 0), 0, 0)),
        blk,
        pl.BlockSpec((1, bn, c), lambda i: (jnp.minimum((ck(i) + 1) * t_len, s_len - 1), 0, 0)),
    ]
    args = [ug3, ug3, ug3]
    scratch = [pltpu.VMEM((t_len + 3, bn, c), F32), pltpu.VMEM((t_len, bn, c), F32),
               pltpu.VMEM((t_len, bn, c), F32)]
    if reverse:
        in_specs += [pl.BlockSpec((t_len, bn, c), lambda i: (ck(i), 0, 1)), blk]
        args += [ug3, hf]
        scratch.append(pltpu.VMEM((t_len, bn, c), F32))
    scratch.append(pltpu.VMEM((bn, c), F32))
    in_specs += [pl.BlockSpec((4, c), lambda i: (0, 0)), vec, wspec, vec, wspec, vec, vec]
    args += [conv_w.astype(F32), conv_b.astype(F32).reshape(1, c), dense_halves(wa),
             ba.astype(F32).reshape(1, c), dense_halves(wx), bx.astype(F32).reshape(1, c), softplus_neg_lam]
    return pl.pallas_call(
        functools.partial(_rglru_kernel, reverse=reverse, n_chunks=n_chunks),
        out_shape=jax.ShapeDtypeStruct((s_len, bn, c), BF16 if reverse else F32),
        grid=(n_chunks,),
        in_specs=in_specs,
        out_specs=blk,
        scratch_shapes=scratch,
        compiler_params=_cparams(("arbitrary",)),
        name="rglru_bw" if reverse else "rglru_fw",
    )(*args)


def _hg_matrices(c, reverse):
    n_levels = int(math.log2(c))
    t = np.arange(c)[:, None]
    r = np.arange(c)[None, :]
    mats, masks, upper = [], [], []
    if not reverse:
        mats.append(r <= t)
        mats.append(r > t)
    else:
        mats.append(r >= t)
        mats.append(r < t)
    for lev in range(n_levels):
        half = c >> (lev + 1)
        parent = 2 * half
        start = (np.arange(c) // parent) * parent
        mid = (start + half)[:, None]
        later = (np.arange(c) % parent >= half)[:, None]
        if not reverse:
            m = np.where(later, (r >= mid) & (r <= t), (r > t) & (r < mid))
            is_q = later
        else:
            m = np.where(later, (r >= mid) & (r < t), (r >= t) & (r < mid))
            is_q = ~later
        mats.append(m)
        same_parent = (start[:, None] == start[None, :])
        masks.append(same_parent & is_q & (~is_q).T)
        upper.append(np.broadcast_to(is_q, (c, HG_DK)))
    mat = np.concatenate(mats, axis=0).astype(np.float32)
    return (np.concatenate([mat, mat], axis=1), np.stack(masks).astype(np.float32),
            np.stack(upper).astype(np.float32))


def _hgrn_kernel(*refs, reverse, n_steps, n_levels):
    if reverse:
        (q_ref, f_ref, v_ref, go_ref, of_ref, yc_ref, xres_ref, wout_ref, lb_ref, gn_ref, mat_ref, mask_ref,
         isq_ref, out_ref, state, x_scr, a_scr, aux_scr, dec_scr, kk_scr) = refs
    else:
        (q_ref, f_ref, v_ref, lb_ref, mat_ref, mask_ref, isq_ref, out_ref,
         state, x_scr, a_scr, aux_scr, dec_scr, kk_scr) = refs
    step = pl.program_id(1)
    c = HG_CHUNK
    t_len = q_ref.shape[0]
    n_chunks = t_len // c

    @pl.when(step == 0)
    def _():
        state[...] = jnp.zeros_like(state)

    if reverse:
        out_ref[...] = xres_ref[...] + _dot(yc_ref[...], wout_ref[0:LRU_W, :])

    def chunk_body(ci, carry):
        ch = (n_chunks - 1 - ci) if reverse else ci
        rows = slice(ch * c, (ch + 1) * c)
        edge = (c - 1) if not reverse else 0
        for hp in range(HG_HEADS // 2):
            cols2 = slice(2 * hp * HG_DK, 2 * (hp + 1) * HG_DK)
            fl = f_ref[rows, cols2]
            e = jnp.exp(-jnp.abs(fl))
            r = 1.0 / (1.0 + e)
            er = e * r
            pos = fl > 0.0
            lb = lb_ref[0:1, cols2]
            oml = lb_ref[1:2, cols2]
            g2 = jnp.log2(lb + oml * jnp.where(pos, r, er))
            kk_scr[hp] = oml * jnp.where(pos, er, r)
            g_hi = g2.astype(BF16)
            g_lo = (g2 - g_hi.astype(F32)).astype(BF16)
            dec_scr[hp] = jnp.exp2(_dot(mat_ref[...], jnp.concatenate([g_hi, g_lo], axis=0)))

        for hd in range(HG_HEADS):
            cols = slice(hd * HG_DK, (hd + 1) * HG_DK)
            pcols = slice((hd % 2) * HG_DK, (hd % 2 + 1) * HG_DK)
            kk = kk_scr[hd // 2, :, pcols]
            dec = dec_scr.at[hd // 2]
            qv = q_ref[rows, cols].astype(F32)
            qq = qv * jax.nn.sigmoid(qv)
            qb, kb = qq.astype(BF16), kk.astype(BF16)
            d_in = dec[0:c, pcols]
            x_scr[hd, n_levels] = (qq * d_in).astype(BF16)
            x_scr[hd, n_levels + 1] = (kk * dec[c:2 * c, pcols]).astype(BF16)
            aux_scr[hd, 0] = jnp.broadcast_to(d_in[edge:edge + 1, :], (c, HG_DK))
            aux_scr[hd, 1] = jnp.broadcast_to(jnp.sum(qq * kk, axis=-1, keepdims=True), (c, HG_DK))
            for lev in range(n_levels):
                half = c >> (lev + 1)
                if half >= SUBLANES_BF16:
                    first_is_q = bool(reverse)
                    base = jnp.concatenate(
                        [(qb if (blk % 2 == 1) != first_is_q else kb)[blk * half:(blk + 1) * half]
                         for blk in range(c // half)], axis=0)
                else:
                    base = jnp.where(isq_ref[lev] > 0.5, qb, kb)
                x_scr[hd, lev] = base * dec[(2 + lev) * c:(3 + lev) * c, pcols].astype(BF16)

        for hd in range(HG_HEADS):
            cols = slice(hd * HG_DK, (hd + 1) * HG_DK)
            for lev in range(n_levels):
                xl = x_scr[hd, lev]
                a_scr[hd, lev] = _nt_dot(xl, xl)
            st = state[hd]
            a_scr[hd, n_levels] = _nt_dot(x_scr[hd, n_levels], st.astype(BF16))
            vt = v_ref[rows, cols].astype(F32).T.astype(BF16)
            state[hd] = st * aux_scr[hd, 0] + _dot(vt, x_scr[hd, n_levels + 1])

        for hd in range(HG_HEADS):
            att = mask_ref[0] * a_scr[hd, 0]
            for lev in range(1, n_levels):
                att = att + mask_ref[lev] * a_scr[hd, lev]
            x_scr[hd, 0] = att.astype(BF16)

        yd = []
        for hd in range(HG_HEADS):
            cols = slice(hd * HG_DK, (hd + 1) * HG_DK)
            vb = v_ref[rows, cols]
            o = a_scr[hd, n_levels] + _dot(x_scr[hd, 0], vb) + aux_scr[hd, 1] * vb.astype(F32)

            if reverse:
                o = o + of_ref[rows, cols]
                o = o * lax.rsqrt(jnp.mean(o * o, axis=-1, keepdims=True) + EPS) * gn_ref[...]
                gv = go_ref[rows, cols].astype(F32)
                yd.append((o * (gv * jax.nn.sigmoid(gv))).astype(BF16))
            else:
                out_ref[rows, cols] = o
        if reverse:
            out_ref[rows, :] = out_ref[rows, :] + _dot(jnp.concatenate(yd, axis=-1), wout_ref[LRU_W:, :])
        return carry

    for ci in range(n_chunks):
        chunk_body(ci, 0)


def _hgrn(hg, o_fw, lb, gnorm_g, *, reverse, x=None, yc_tm=None, w_out=None):
    bn, s_len, _ = hg["q"].shape
    w = HG_HEADS * HG_DK
    t_len = min(HG_T, s_len)
    n_steps = s_len // t_len
    mats, masks, isq = _hg_matrices(HG_CHUNK, reverse)
    n_levels = masks.shape[0]
    lbf = lb.astype(F32)
    lb_tab = jnp.stack([lbf, 1.0 - lbf])

    blk = pl.BlockSpec((None, t_len, w), lambda b, i: (b, (n_steps - 1 - i) if reverse else i, 0))

    def const(shape):
        return pl.BlockSpec(shape, lambda b, i: (0,) * len(shape))

    in_specs = [blk, blk, blk]
    args = [hg["q"], hg["f_bw" if reverse else "f_fw"], hg["i_in"]]
    out_spec, out_shape = blk, jax.ShapeDtypeStruct((bn, s_len, w), F32)
    if reverse:
        d = x.shape[-1]
        out_spec = pl.BlockSpec((None, t_len, d), lambda b, i: (b, n_steps - 1 - i, 0))
        out_shape = jax.ShapeDtypeStruct(x.shape, F32)
        in_specs += [blk, blk, pl.BlockSpec((t_len, LRU_W), lambda b, i: (n_steps - 1 - i, b)), out_spec,
                     _resident(w_out.shape, lambda b, i: (0, 0))]
        args += [hg["g_out"], o_fw, yc_tm, x, w_out]
    in_specs.append(const((2, w)))
    args.append(lb_tab)
    if reverse:
        in_specs.append(const((1, HG_DK)))
        args.append(gnorm_g.astype(F32).reshape(1, HG_DK))
    in_specs += [const(mats.shape), const(masks.shape), const(isq.shape)]
    args += [jnp.asarray(mats, BF16), jnp.asarray(masks, F32), jnp.asarray(isq, BF16)]
    return pl.pallas_call(
        functools.partial(_hgrn_kernel, reverse=reverse, n_steps=n_steps, n_levels=n_levels),
        out_shape=out_shape,
        grid=(bn, n_steps),
        in_specs=in_specs,
        out_specs=out_spec,
        scratch_shapes=[pltpu.VMEM((HG_HEADS, HG_DK, HG_DK), F32),
                        pltpu.VMEM((HG_HEADS, n_levels + 2, HG_CHUNK, HG_DK), BF16),
                        pltpu.VMEM((HG_HEADS, n_levels + 1, HG_CHUNK, HG_CHUNK), F32),
                        pltpu.VMEM((HG_HEADS, 2, HG_CHUNK, HG_DK), F32),
                        pltpu.VMEM((HG_HEADS // 2, (n_levels + 2) * HG_CHUNK, 2 * HG_DK), F32),
                        pltpu.VMEM((HG_HEADS // 2, HG_CHUNK, 2 * HG_DK), F32)],
        compiler_params=_cparams(("parallel", "arbitrary")),
        name="hgrn_bw" if reverse else "hgrn_fw",
    )(*args)


def _even_layer(x, g, w_in, w_out, rpb):
    qkv, view4, view16 = _ev_proj(x, g, w_in.astype(BF16))
    outs, maxes, dens = zip(*[_dilated_branch(view, dil) for dil, view in zip(DILATIONS, (qkv, view4, view16))])
    yb = _neighborhood(qkv, rpb)
    return _ev_out(x, outs, maxes, dens, yb, w_out.astype(BF16))


def _odd_layer(x, g, w_in, w_out, conv_w, conv_b, wa, ba, wx, bx, lam, lb_f, lb_b, gnorm_g):
    bn, s_len, _ = x.shape
    ug, hg = _od_proj(x, g, w_in.astype(BF16))
    h_fw = _rglru(ug, None, conv_w, conv_b, wa[0], ba[0], wx[0], bx[0], lam[0], bn, reverse=False)
    yc = _rglru(ug, h_fw, conv_w, conv_b, wa[1], ba[1], wx[1], bx[1], lam[1], bn, reverse=True)
    o_fw = _hgrn(hg, None, lb_f, gnorm_g, reverse=False)
    return _hgrn(hg, o_fw, lb_b, gnorm_g, reverse=True, x=x, yc_tm=yc.reshape(s_len, bn * LRU_W),
                 w_out=w_out.astype(BF16))


def kernel(x, mem, norm_mix_g, norm_xa_g, norm_mem_g, norm_mlp_g, final_norm_g, ev_w_in, ev_w_out, na_rpb,
           od_w_in, od_w_out, conv_w, conv_b, lru_wa, lru_ba, lru_wx, lru_bx, lru_lambda, hgrn_lb_logits,
           hgrn_norm_g, xa_wq, xa_wkv, xa_wo, mlp_w1, mlp_w2):
    depth = norm_mix_g.shape[0]
    p_lb = jax.nn.softmax(hgrn_lb_logits.astype(F32), axis=0)
    lower_bounds = jnp.cumsum(p_lb, axis=0) - p_lb[0:1]
    for layer in range(depth):
        if layer % 2 == 0:
            e = layer // 2
            x = _even_layer(x, norm_mix_g[layer], ev_w_in[e], ev_w_out[e], na_rpb[e])
        else:
            o = layer // 2
            x = _odd_layer(x, norm_mix_g[layer], od_w_in[o], od_w_out[o], conv_w[o], conv_b[o],
                           lru_wa[o], lru_ba[o], lru_wx[o], lru_bx[o], lru_lambda[o],
                           lower_bounds[layer, 0], lower_bounds[layer, 1], hgrn_norm_g[o])
        kv = _mem_kv(mem, norm_mem_g[layer], xa_wkv[layer].astype(BF16))
        x = _xattn(x, norm_xa_g[layer], xa_wq[layer].astype(BF16), kv, xa_wo[layer].astype(BF16))
        x = _mlp(x, norm_mlp_g[layer], mlp_w1[layer].astype(BF16), mlp_w2[layer].astype(BF16),
                 g_final=final_norm_g if layer == depth - 1 else None)
    return x
```

```python
import functools
import math

import jax
import jax.numpy as jnp
import numpy as np
from jax import lax
from jax.experimental import pallas as pl
from jax.experimental.pallas import tpu as pltpu

F32 = jnp.float32
BF16 = jnp.bfloat16

D_MODEL = 1024
HEAD_DIM = 64
ROT_DIM = 16
ROPE_THETA = 500000.0
N_HEADS_A = 8
N_HEADS_B = 8
GROUP_W = N_HEADS_A * HEAD_DIM
DILATIONS = (1, 4, 16)
WIN_HALF = 64
GRID_W = 64
NA_ROWS = 8
NA_COLS = 16
LRU_W = 512
LRU_BLOCKS = 8
LRU_C = 8.0
HG_HEADS = 4
HG_DK = 128
XA_HEADS = 4
XA_DH = 256
D_FF = 4096
EPS = 1e-6
LOG2E = math.log2(math.e)

LANES = 128
SUBLANES_BF16 = 16
V7X_VMEM_LIMIT_BYTES = 56 * 1024 * 1024

ROW_TILE = 512
ROW_TILE_WIDE = 1024
ATT_Q_BLOCK = 512
ATT_SUB = 128
NA_ROW_GROUP = 8
LRU_T = 128
HG_CHUNK = 128
HG_T = 512


def _cparams(sem):
    return pltpu.CompilerParams(dimension_semantics=sem, vmem_limit_bytes=V7X_VMEM_LIMIT_BYTES)


def _resident(shape, index_map):
    return pl.BlockSpec(shape, index_map, pipeline_mode=pl.Buffered(1))


def _rms(x, g):
    return x * lax.rsqrt(jnp.mean(x * x, axis=-1, keepdims=True) + EPS) * g


def _nt_dot(a, b):
    return lax.dot_general(a, b, (((1,), (1,)), ((), ())), preferred_element_type=F32)


def _dot(a, b):
    return jnp.dot(a, b, preferred_element_type=F32)


def _head_selectors():
    lane = lax.broadcasted_iota(jnp.int32, (1, LANES), 1)
    low = (lane < HEAD_DIM).astype(F32)
    return low.astype(BF16), (1.0 - low).astype(BF16)


def _ev_proj_kernel(x_ref, g_ref, w_ref, rc_ref, rs1_ref, rs2_ref, o_ref, o4_ref, o16_ref, slab):
    h = _rms(x_ref[...], g_ref[...]).astype(BF16)
    scale = HEAD_DIM ** -0.5 * LOG2E
    tm = x_ref.shape[0]
    n_slabs = GROUP_W // LANES
    for c in range(6):
        acc = _dot(h, w_ref[:, c * GROUP_W:(c + 1) * GROUP_W])
        if c in (0, 1):
            rc, rs1, rs2 = rc_ref[...], rs1_ref[...], rs2_ref[...]
            parts = []
            for j in range(n_slabs):
                t = acc[:, j * LANES:(j + 1) * LANES]
                t = t * rc + pltpu.roll(t, LANES - ROT_DIM // 2, 1) * rs1 + pltpu.roll(t, ROT_DIM // 2, 1) * rs2
                parts.append(t)
            acc = jnp.concatenate(parts, axis=-1)
        if c in (0, 3):
            acc = acc * scale
        o_ref[:, c * GROUP_W:(c + 1) * GROUP_W] = acc.astype(BF16)
        if c < 3:
            for j in range(n_slabs):
                slab[j] = acc[:, j * LANES:(j + 1) * LANES]
            for dil, dst in ((DILATIONS[1], o4_ref), (DILATIONS[2], o16_ref)):
                for r in range(dil):
                    for j in range(n_slabs):
                        col = r * 3 * GROUP_W + c * GROUP_W + j * LANES
                        dst[:, col:col + LANES] = slab[j, pl.ds(r, tm // dil, stride=dil), :].astype(BF16)


def _rope_tables(s_len):
    half = ROT_DIM // 2
    inv = jnp.asarray(ROPE_THETA ** (-np.arange(half) * 2.0 / ROT_DIM), F32)
    ang = jnp.arange(s_len, dtype=F32)[:, None] * inv[None, :]
    cos, sin = jnp.cos(ang), jnp.sin(ang)
    ones = jnp.ones((s_len, HEAD_DIM - ROT_DIM), F32)
    zeros = jnp.zeros((s_len, HEAD_DIM - ROT_DIM), F32)
    zh = jnp.zeros((s_len, half), F32)
    rc = jnp.concatenate([cos, cos, ones], axis=-1)
    rs1 = jnp.concatenate([-sin, zh, zeros], axis=-1)
    rs2 = jnp.concatenate([zh, sin, zeros], axis=-1)
    rep = LANES // HEAD_DIM
    return jnp.tile(rc, (1, rep)), jnp.tile(rs1, (1, rep)), jnp.tile(rs2, (1, rep))


def _ev_proj(x, g, w_bf16):
    bn, s_len, d = x.shape
    tm = min(ROW_TILE_WIDE, s_len)
    n_out = w_bf16.shape[1]
    rc, rs1, rs2 = _rope_tables(s_len)
    tbl_spec = pl.BlockSpec((tm, LANES), lambda b, i: (i, 0))
    wa = 3 * GROUP_W
    d4, d16 = DILATIONS[1], DILATIONS[2]
    return pl.pallas_call(
        _ev_proj_kernel,
        out_shape=(jax.ShapeDtypeStruct((bn, s_len, n_out), BF16),
                   jax.ShapeDtypeStruct((bn, s_len // d4, d4 * wa), BF16),
                   jax.ShapeDtypeStruct((bn, s_len // d16, d16 * wa), BF16)),
        grid=(bn, s_len // tm),
        in_specs=[
            pl.BlockSpec((None, tm, d), lambda b, i: (b, i, 0)),
            pl.BlockSpec((1, d), lambda b, i: (0, 0)),
            _resident((d, n_out), lambda b, i: (0, 0)),
            tbl_spec, tbl_spec, tbl_spec,
        ],
        out_specs=(pl.BlockSpec((None, tm, n_out), lambda b, i: (b, i, 0)),
                   pl.BlockSpec((None, tm // d4, d4 * wa), lambda b, i: (b, i, 0)),
                   pl.BlockSpec((None, tm // d16, d16 * wa), lambda b, i: (b, i, 0))),
        scratch_shapes=[pltpu.VMEM((GROUP_W // LANES, tm, LANES), F32)],
        compiler_params=_cparams(("parallel", "parallel")),
        name="ev_proj",
    )(x, g.reshape(1, d), w_bf16, rc, rs1, rs2)


def _dilated_kernel(q_ref, kl_ref, kc_ref, kr_ref, vl_ref, vc_ref, vr_ref, o_ref, max_ref, den_ref,
                    kext, vext, s_scr, p_scr, *, lq, l_total):
    i = pl.program_id(2)
    kext[0:WIN_HALF, :] = kl_ref[...]
    kext[WIN_HALF:WIN_HALF + lq, :] = kc_ref[...]
    kext[WIN_HALF + lq:, :] = kr_ref[...]
    vext[0:WIN_HALF, :] = vl_ref[...]
    vext[WIN_HALF:WIN_HALF + lq, :] = vc_ref[...]
    vext[WIN_HALF + lq:, :] = vr_ref[...]

    wk = ATT_SUB + 2 * WIN_HALF
    qi = lax.broadcasted_iota(jnp.int32, (ATT_SUB, wk), 0)
    ci = lax.broadcasted_iota(jnp.int32, (ATT_SUB, wk), 1)
    band_bias = jnp.where((ci - qi >= 0) & (ci - qi <= 2 * WIN_HALF), 0.0, -jnp.inf).astype(F32)
    crow = lax.broadcasted_iota(jnp.int32, (1, wk), 1)
    lane = lax.broadcasted_iota(jnp.int32, (ATT_SUB, LANES), 1)
    low_half = lane < HEAD_DIM
    sel_lo, sel_hi = _head_selectors()
    n_sub = lq // ATT_SUB
    n_hp = GROUP_W // LANES

    for j in range(n_sub):
        base = i * lq + j * ATT_SUB - WIN_HALF
        in_seq = (crow + base >= 0) & (crow + base < l_total)
        bias = band_bias + jnp.where(in_seq, 0.0, -jnp.inf).astype(F32)
        bias2 = jnp.concatenate([bias, bias], axis=0)
        rows = slice(j * ATT_SUB, (j + 1) * ATT_SUB)
        wrows = slice(j * ATT_SUB, j * ATT_SUB + wk)
        for hp in range(n_hp):
            cols = slice(hp * LANES, (hp + 1) * LANES)
            qp = q_ref[rows, cols]
            q2 = jnp.concatenate([qp * sel_lo, qp * sel_hi], axis=0)
            s_scr[j * n_hp + hp] = _nt_dot(q2, kext[wrows, cols]) + bias2
    for j in range(n_sub):
        m_tile = jnp.zeros((ATT_SUB, LANES), F32)
        den_tile = jnp.ones((ATT_SUB, LANES), F32)
        for hp in range(n_hp):
            t = j * n_hp + hp
            s = s_scr[t]
            m = jnp.max(s, axis=-1, keepdims=True)
            p = jnp.exp2(s - m)
            den = jnp.sum(p, axis=-1, keepdims=True)
            p_scr[t] = p.astype(BF16)
            is_lo, is_hi = lane == 2 * hp, lane == 2 * hp + 1
            m_tile = jnp.where(is_lo, m[0:ATT_SUB], jnp.where(is_hi, m[ATT_SUB:], m_tile))
            den_tile = jnp.where(is_lo, den[0:ATT_SUB], jnp.where(is_hi, den[ATT_SUB:], den_tile))
        max_ref[j * ATT_SUB:(j + 1) * ATT_SUB, :] = m_tile
        den_ref[j * ATT_SUB:(j + 1) * ATT_SUB, :] = den_tile
    for j in range(n_sub):
        rows = slice(j * ATT_SUB, (j + 1) * ATT_SUB)
        wrows = slice(j * ATT_SUB, j * ATT_SUB + wk)
        for hp in range(n_hp):
            cols = slice(hp * LANES, (hp + 1) * LANES)
            o = _dot(p_scr[j * n_hp + hp], vext[wrows, cols])
            o_ref[rows, cols] = jnp.where(low_half, o[0:ATT_SUB], o[ATT_SUB:]).astype(BF16)


def _dilated_branch(view, dil):
    bn, l_total, width = view.shape
    n_groups = width // (dil * GROUP_W)
    lq = min(ATT_Q_BLOCK, l_total)
    nblk = l_total // lq
    per = lq // WIN_HALF
    n_halo = l_total // WIN_HALF
    n_units = (lq // ATT_SUB) * (GROUP_W // LANES)

    def cur(g):
        return pl.BlockSpec((None, lq, GROUP_W), lambda b, r, i: (b, i, r * n_groups + g))

    def left(g):
        return pl.BlockSpec((None, WIN_HALF, GROUP_W),
                            lambda b, r, i: (b, jnp.maximum(i * per - 1, 0), r * n_groups + g))

    def right(g):
        return pl.BlockSpec((None, WIN_HALF, GROUP_W),
                            lambda b, r, i: (b, jnp.minimum((i + 1) * per, n_halo - 1), r * n_groups + g))

    stat_shape = jax.ShapeDtypeStruct((bn, l_total, dil * LANES), F32)
    stat_spec = pl.BlockSpec((None, lq, LANES), lambda b, r, i: (b, i, r))
    return pl.pallas_call(
        functools.partial(_dilated_kernel, lq=lq, l_total=l_total),
        out_shape=(jax.ShapeDtypeStruct((bn, l_total, dil * GROUP_W), BF16), stat_shape, stat_shape),
        grid=(bn, dil, nblk),
        in_specs=[cur(0), left(1), cur(1), right(1), left(2), cur(2), right(2)],
        out_specs=(pl.BlockSpec((None, lq, GROUP_W), lambda b, r, i: (b, i, r)), stat_spec, stat_spec),
        scratch_shapes=[pltpu.VMEM((lq + 2 * WIN_HALF, GROUP_W), BF16),
                        pltpu.VMEM((lq + 2 * WIN_HALF, GROUP_W), BF16),
                        pltpu.VMEM((n_units, 2 * ATT_SUB, ATT_SUB + 2 * WIN_HALF), F32),
                        pltpu.VMEM((n_units, 2 * ATT_SUB, ATT_SUB + 2 * WIN_HALF), BF16)],
        compiler_params=_cparams(("parallel", "parallel", "parallel")),
        name=f"dilated_attn_d{dil}",
    )(view, view, view, view, view, view, view)


NA_QT = NA_ROW_GROUP * GRID_W
NA_KT = 2 * NA_QT
NA_KW = NA_ROWS * GRID_W


def _na_tables(rpb):
    rpb = rpb.astype(F32)
    nh = rpb.shape[0]
    ext = GRID_W - NA_COLS
    padded = jnp.concatenate([jnp.repeat(rpb[..., :1], ext, axis=-1), rpb,
                              jnp.repeat(rpb[..., -1:], ext, axis=-1)], axis=-1)
    t1 = jnp.stack([padded[..., GRID_W - 1 - qc:2 * GRID_W - 1 - qc] for qc in range(GRID_W)], axis=2)
    qc = np.arange(GRID_W)[:, None]
    kc = np.arange(GRID_W)[None, :]
    c0 = np.clip(qc - NA_COLS // 2, 0, GRID_W - NA_COLS)
    t1 = jnp.where((kc >= c0) & (kc < c0 + NA_COLS), t1 * LOG2E, -jnp.inf)
    per_delta = [jnp.concatenate([t1[:, kr - delta + NA_ROWS - 1] for kr in range(NA_ROWS)], axis=-1)
                 for delta in range(NA_ROWS)]
    return jnp.stack(per_delta).reshape(NA_ROWS, nh // 2, 2 * GRID_W, NA_KW)


def _na_kernel(q_ref, k_ref, v_ref, tb_ref, o_ref, s_scr, p_scr, inv_scr, *, rows):
    i0 = pl.program_id(1) * NA_ROW_GROUP
    w0 = jnp.clip(i0 - NA_ROWS // 2, 0, rows - 2 * NA_ROW_GROUP)
    lane = lax.broadcasted_iota(jnp.int32, (GRID_W, LANES), 1)
    low_half = lane < HEAD_DIM
    sel_lo, sel_hi = _head_selectors()
    n_hp = GROUP_W // LANES

    def key_rows(a):
        r0 = jnp.clip(i0 + a - NA_ROWS // 2, 0, rows - NA_ROWS)
        return pl.ds(pl.multiple_of((r0 - w0) * GRID_W, GRID_W), NA_KW), i0 + a - r0

    for a in range(NA_ROW_GROUP):
        krows, delta = key_rows(a)
        qrows = slice(a * GRID_W, (a + 1) * GRID_W)
        for hp in range(n_hp):
            cols = slice(hp * LANES, (hp + 1) * LANES)
            qp = q_ref[qrows, cols]
            q2 = jnp.concatenate([qp * sel_lo, qp * sel_hi], axis=0)
            s_scr[a * n_hp + hp] = _nt_dot(q2, k_ref[0, krows, cols]) + tb_ref[delta, hp]
    for t in range(NA_ROW_GROUP * n_hp):
        s = s_scr[t]
        p = jnp.exp2(s - jnp.max(s, axis=-1, keepdims=True))
        inv_scr[t] = jnp.broadcast_to(1.0 / jnp.sum(p, axis=-1, keepdims=True), (2 * GRID_W, LANES))
        p_scr[t] = p.astype(BF16)
    for a in range(NA_ROW_GROUP):
        krows, _ = key_rows(a)
        qrows = slice(a * GRID_W, (a + 1) * GRID_W)
        for hp in range(n_hp):
            cols = slice(hp * LANES, (hp + 1) * LANES)
            t = a * n_hp + hp
            o = _dot(p_scr[t], v_ref[0, krows, cols]) * inv_scr[t]
            o_ref[qrows, cols] = jnp.where(low_half, o[0:GRID_W], o[GRID_W:]).astype(BF16)


def _neighborhood(qkv, rpb):
    bn, s_len, _ = qkv.shape
    rows = s_len // GRID_W
    n_groups = rows // NA_ROW_GROUP
    tables = _na_tables(rpb)

    def window(g):
        return pl.BlockSpec(
            (pl.Element(1), pl.Element(NA_KT), pl.Element(GROUP_W)),
            lambda b, i: (b, jnp.clip(i * NA_ROW_GROUP - NA_ROWS // 2, 0, rows - 2 * NA_ROW_GROUP) * GRID_W,
                          g * GROUP_W))

    return pl.pallas_call(
        functools.partial(_na_kernel, rows=rows),
        out_shape=jax.ShapeDtypeStruct((bn, s_len, GROUP_W), BF16),
        grid=(bn, n_groups),
        in_specs=[pl.BlockSpec((None, NA_QT, GROUP_W), lambda b, i: (b, i, 3)), window(4), window(5),
                  pl.BlockSpec(tables.shape, lambda b, i: (0, 0, 0, 0))],
        out_specs=pl.BlockSpec((None, NA_QT, GROUP_W), lambda b, i: (b, i, 0)),
        scratch_shapes=[pltpu.VMEM((NA_ROW_GROUP * GROUP_W // LANES, 2 * GRID_W, NA_KW), F32),
                        pltpu.VMEM((NA_ROW_GROUP * GROUP_W // LANES, 2 * GRID_W, NA_KW), BF16),
                        pltpu.VMEM((NA_ROW_GROUP * GROUP_W // LANES, 2 * GRID_W, LANES), F32)],
        compiler_params=_cparams(("parallel", "parallel")),
        name="neighborhood_attn",
    )(qkv, qkv, qkv, tables)


def _ev_out_kernel(o1, o2v, o3v, m1, m2v, m3v, d1, d2v, d3v, yb_ref, ex_ref, w_ref, x_ref, out_ref,
                   so2, so3, st2, st3):
    tm = x_ref.shape[0]
    n_slabs = GROUP_W // LANES
    for dil, ov, mv, dv, so, st in ((DILATIONS[1], o2v, m2v, d2v, so2, st2), (DILATIONS[2], o3v, m3v, d3v, so3, st3)):
        for r in range(dil):
            dst = pl.ds(r, tm // dil, stride=dil)
            for j in range(n_slabs):
                col = r * GROUP_W + j * LANES
                so[j, dst, :] = ov[:, col:col + LANES].astype(F32)
            st[0, dst, :] = mv[:, r * LANES:(r + 1) * LANES]
            st[1, dst, :] = dv[:, r * LANES:(r + 1) * LANES]
    o2 = jnp.concatenate([so2[j] for j in range(n_slabs)], axis=-1)
    o3 = jnp.concatenate([so3[j] for j in range(n_slabs)], axis=-1)
    ma, mb, mc = m1[...], st2[0], st3[0]
    m = jnp.maximum(jnp.maximum(ma, mb), mc)
    ea, eb, ec = jnp.exp2(ma - m), jnp.exp2(mb - m), jnp.exp2(mc - m)
    inv = 1.0 / (ea * d1[...] + eb * st2[1] + ec * st3[1])
    lane = lax.broadcasted_iota(jnp.int32, (tm, LANES), 1)
    wcat = jnp.where(lane < N_HEADS_A, ea * inv,
                     jnp.where(lane < 2 * N_HEADS_A, pltpu.roll(eb * inv, N_HEADS_A, 1),
                               pltpu.roll(ec * inv, 2 * N_HEADS_A, 1)))
    hi = wcat.astype(BF16)
    lo = (wcat - hi.astype(F32)).astype(BF16)
    wide = _dot(jnp.concatenate([hi, lo], axis=1), ex_ref[...])
    ya = (wide[:, 0:GROUP_W] * o1[...].astype(F32) + wide[:, GROUP_W:2 * GROUP_W] * o2
          + wide[:, 2 * GROUP_W:] * o3)
    y = _dot(ya.astype(BF16), w_ref[0:GROUP_W, :]) + _dot(yb_ref[...], w_ref[GROUP_W:, :])
    out_ref[...] = x_ref[...] + y


def _ev_out(x, outs, maxes, dens, yb, w_bf16):
    bn, s_len, d = x.shape
    tm = min(ROW_TILE, s_len)
    n_br = len(DILATIONS)
    ex = np.zeros((LANES, n_br * GROUP_W), np.float32)
    for i in range(n_br):
        for h in range(N_HEADS_A):
            ex[i * N_HEADS_A + h, i * GROUP_W + h * HEAD_DIM:i * GROUP_W + (h + 1) * HEAD_DIM] = 1.0
    ex = np.concatenate([ex, ex], axis=0)

    def o_spec(dil):
        return pl.BlockSpec((None, tm // dil, dil * GROUP_W), lambda b, i: (b, i, 0))

    def l_spec(dil):
        return pl.BlockSpec((None, tm // dil, dil * LANES), lambda b, i: (b, i, 0))

    x_spec = pl.BlockSpec((None, tm, d), lambda b, i: (b, i, 0))
    n_slabs = GROUP_W // LANES
    return pl.pallas_call(
        _ev_out_kernel,
        out_shape=jax.ShapeDtypeStruct(x.shape, F32),
        grid=(bn, s_len // tm),
        in_specs=[o_spec(d_) for d_ in DILATIONS] + 2 * [l_spec(d_) for d_ in DILATIONS]
        + [o_spec(1),
           pl.BlockSpec(ex.shape, lambda b, i: (0, 0)),
           pl.BlockSpec((2 * GROUP_W, d), lambda b, i: (0, 0)),
           x_spec],
        out_specs=x_spec,
        scratch_shapes=[pltpu.VMEM((n_slabs, tm, LANES), F32), pltpu.VMEM((n_slabs, tm, LANES), F32),
                        pltpu.VMEM((2, tm, LANES), F32), pltpu.VMEM((2, tm, LANES), F32)],
        compiler_params=_cparams(("parallel", "parallel")),
        name="ev_out",
    )(*outs, *maxes, *dens, yb, jnp.asarray(ex, BF16), w_bf16, x)


def _norm_matmul_kernel(x_ref, g_ref, w_ref, o_ref):
    o_ref[...] = _dot(_rms(x_ref[...], g_ref[...]).astype(BF16), w_ref[...]).astype(o_ref.dtype)


def _mem_kv(mem, g, w_bf16):
    bn, m_len, d = mem.shape
    n_out = w_bf16.shape[1]
    return pl.pallas_call(
        _norm_matmul_kernel,
        out_shape=jax.ShapeDtypeStruct((bn, m_len, n_out), BF16),
        grid=(bn,),
        in_specs=[pl.BlockSpec((None, m_len, d), lambda b: (b, 0, 0)),
                  pl.BlockSpec((1, d), lambda b: (0, 0)),
                  pl.BlockSpec((d, n_out), lambda b: (0, 0))],
        out_specs=pl.BlockSpec((None, m_len, n_out), lambda b: (b, 0, 0)),
        compiler_params=_cparams(("parallel",)),
        name="mem_kv",
    )(mem, g.reshape(1, d), w_bf16)


def _xattn_kernel(x_ref, g_ref, wq_ref, kv_ref, wo_ref, out_ref):
    x = x_ref[...]
    q = (_dot(_rms(x, g_ref[...]).astype(BF16), wq_ref[...]) * (XA_DH ** -0.5 * LOG2E)).astype(BF16)
    heads = []
    for hd in range(XA_HEADS):
        cols = slice(hd * XA_DH, (hd + 1) * XA_DH)
        s = _nt_dot(q[:, cols], kv_ref[:, cols])
        m = jnp.max(s, axis=-1, keepdims=True)
        p = jnp.exp2(s - m)
        den = jnp.sum(p, axis=-1, keepdims=True)
        vcols = slice(D_MODEL + hd * XA_DH, D_MODEL + (hd + 1) * XA_DH)
        heads.append((_dot(p.astype(BF16), kv_ref[:, vcols]) * (1.0 / den)).astype(BF16))
    o = jnp.concatenate(heads, axis=-1)
    out_ref[...] = x + _dot(o, wo_ref[...])


def _xattn(x, g, wq, kv, wo):
    bn, s_len, d = x.shape
    tm = min(ROW_TILE_WIDE, s_len)
    m_len = kv.shape[1]
    x_spec = pl.BlockSpec((None, tm, d), lambda b, i: (b, i, 0))
    return pl.pallas_call(
        _xattn_kernel,
        out_shape=jax.ShapeDtypeStruct(x.shape, F32),
        grid=(bn, s_len // tm),
        in_specs=[x_spec,
                  pl.BlockSpec((1, d), lambda b, i: (0, 0)),
                  pl.BlockSpec((d, d), lambda b, i: (0, 0)),
                  pl.BlockSpec((None, m_len, 2 * d), lambda b, i: (b, 0, 0)),
                  pl.BlockSpec((d, d), lambda b, i: (0, 0))],
        out_specs=x_spec,
        compiler_params=_cparams(("parallel", "parallel")),
        name="mem_xattn",
    )(x, g.reshape(1, d), wq, kv, wo)


def _mlp_kernel(x_ref, g_ref, w1_ref, w2_ref, gf_ref, out_ref, *, final_norm):
    x = x_ref[...]
    h = _rms(x, g_ref[...]).astype(BF16)
    acc = x
    for c in range(D_FF // D_MODEL):
        cols = slice(c * D_MODEL, (c + 1) * D_MODEL)
        a = jnp.maximum(_dot(h, w1_ref[:, cols]), 0.0)
        acc = acc + _dot((a * a).astype(BF16), w2_ref[cols, :])
    out_ref[...] = _rms(acc, gf_ref[...]) if final_norm else acc


def _mlp(x, g, w1, w2, g_final=None):
    bn, s_len, d = x.shape
    tm = min(ROW_TILE_WIDE, s_len)
    final_norm = g_final is not None
    gf = (g_final if final_norm else g).reshape(1, d)
    x_spec = pl.BlockSpec((None, tm, d), lambda b, i: (b, i, 0))
    vec = pl.BlockSpec((1, d), lambda b, i: (0, 0))
    return pl.pallas_call(
        functools.partial(_mlp_kernel, final_norm=final_norm),
        out_shape=jax.ShapeDtypeStruct(x.shape, F32),
        grid=(bn, s_len // tm),
        in_specs=[x_spec, vec,
                  _resident((d, D_FF), lambda b, i: (0, 0)),
                  _resident((D_FF, d), lambda b, i: (0, 0)),
                  vec],
        out_specs=x_spec,
        compiler_params=_cparams(("parallel", "parallel")),
        name="mlp_final" if final_norm else "mlp",
    )(x, g.reshape(1, d), w1, w2, gf)


def _od_proj_kernel(x_ref, g_ref, w_ref, ug_ref, *rest):
    hg_refs, slab = rest[:-1], rest[-1]
    bn, tq, d = x_ref.shape
    h = _rms(x_ref[...].reshape(bn * tq, d), g_ref[...]).astype(BF16)
    n_ug = 2 * LRU_W
    n_slabs = GROUP_W // LANES
    for c in range(n_ug // GROUP_W):
        acc = _dot(h, w_ref[:, c * GROUP_W:(c + 1) * GROUP_W])
        for b in range(bn):
            for j in range(n_slabs):
                slab[j, pl.ds(b, tq, stride=bn), :] = acc[b * tq:(b + 1) * tq, j * LANES:(j + 1) * LANES]
        for j in range(n_slabs):
            ug_ref[:, c * GROUP_W + j * LANES:c * GROUP_W + (j + 1) * LANES] = slab[j]
    for c, hg_ref in enumerate(hg_refs):
        acc = _dot(h, w_ref[:, n_ug + c * GROUP_W:n_ug + (c + 1) * GROUP_W])
        hg_ref[...] = acc.reshape(bn, tq, GROUP_W).astype(hg_ref.dtype)


HG_INPUT_DTYPES = (("q", BF16), ("f_fw", F32), ("f_bw", F32), ("i_in", BF16), ("g_out", BF16))


def _od_proj(x, g, w_bf16):
    bn, s_len, d = x.shape
    tq = min(ROW_TILE // bn, s_len)
    n_ug = 2 * LRU_W
    n_hg = w_bf16.shape[1] - n_ug
    assert n_hg == len(HG_INPUT_DTYPES) * GROUP_W
    hg_spec = pl.BlockSpec((bn, tq, GROUP_W), lambda i: (0, i, 0))
    ug, *hg = pl.pallas_call(
        _od_proj_kernel,
        out_shape=[jax.ShapeDtypeStruct((s_len * bn, n_ug), F32)]
        + [jax.ShapeDtypeStruct((bn, s_len, GROUP_W), dt) for _, dt in HG_INPUT_DTYPES],
        grid=(s_len // tq,),
        in_specs=[pl.BlockSpec((bn, tq, d), lambda i: (0, i, 0)),
                  pl.BlockSpec((1, d), lambda i: (0, 0)),
                  pl.BlockSpec((d, n_ug + n_hg), lambda i: (0, 0))],
        out_specs=[pl.BlockSpec((tq * bn, n_ug), lambda i: (i, 0))] + [hg_spec] * len(HG_INPUT_DTYPES),
        scratch_shapes=[pltpu.VMEM((GROUP_W // LANES, tq * bn, LANES), F32)],
        compiler_params=_cparams(("parallel",)),
        name="od_proj",
    )(x, g.reshape(1, d), w_bf16)
    return ug.reshape(s_len, bn, n_ug), dict(zip([n for n, _ in HG_INPUT_DTYPES], hg))


def _gelu_tanh(x):
    return 0.5 * x * (1.0 + jnp.tanh(math.sqrt(2.0 / math.pi) * (x + 0.044715 * (x * x * x))))


def _rglru_kernel(*refs, reverse, n_chunks):
    if reverse:
        (ul_ref, u_ref, ur_ref, gate_ref, hf_ref, cw_ref, cb_ref, wa_ref, ba_ref, wx_ref, bx_ref, sp_ref,
         out_ref, ext, a_s, b_s, h_s, carry) = refs
    else:
        (ul_ref, u_ref, ur_ref, cw_ref, cb_ref, wa_ref, ba_ref, wx_ref, bx_ref, sp_ref,
         out_ref, ext, a_s, b_s, carry) = refs
        h_s = out_ref
    step = pl.program_id(0)
    chunk = (n_chunks - 1 - step) if reverse else step
    t_len, bn, c = u_ref.shape

    @pl.when(step == 0)
    def _():
        carry[...] = jnp.zeros_like(carry)

    ext[0:2] = jnp.where(chunk == 0, 0.0, ul_ref[...])
    ext[2:t_len + 2] = u_ref[...]
    ext[t_len + 2:t_len + 3] = jnp.where(chunk == n_chunks - 1, 0.0, ur_ref[...])
    uc = cb_ref[...].reshape(1, 1, c)
    for j in range(4):
        uc = uc + cw_ref[j:j + 1, :].reshape(1, 1, c) * ext[j:j + t_len]

    u2 = uc.reshape(t_len * bn, c)
    ub = u2.astype(BF16)
    half = c // 2

    def gate(w_ref, b_ref):
        z = jnp.concatenate([_dot(ub[:, :half], w_ref[0]), _dot(ub[:, half:], w_ref[1])], axis=-1)
        return 0.5 * jnp.tanh(0.5 * (z + b_ref[...])) + 0.5

    r = gate(wa_ref, ba_ref)
    ig = gate(wx_ref, bx_ref)
    a = jnp.exp((-LRU_C) * r * sp_ref[...])
    b = jnp.sqrt(1.0 - a * a) * (ig * u2)
    a_s[...] = a.reshape(t_len, bn, c)
    b_s[...] = b.reshape(t_len, bn, c)

    def body(k, h):
        t = (t_len - 1 - k) if reverse else k
        h = a_s[t] * h + b_s[t]
        h_s[t] = h
        return h

    carry[...] = lax.fori_loop(0, t_len, body, carry[...], unroll=8)

    if reverse:
        out_ref[...] = ((hf_ref[...] + h_s[...]) * _gelu_tanh(gate_ref[...])).astype(out_ref.dtype)


def _rglru(ug, hf, conv_w, conv_b, wa, ba, wx, bx, lam, bn, *, reverse):
    s_len = ug.shape[0]
    c = LRU_W
    ug3 = ug
    t_len = min(LRU_T, s_len)
    n_chunks = s_len // t_len

    def ck(i):
        return (n_chunks - 1 - i) if reverse else i

    half = c // 2
    nb = LRU_BLOCKS // 2

    def dense_halves(w):
        w = w.astype(F32).reshape(2, nb, c // LRU_BLOCKS, c // LRU_BLOCKS)
        eye = jnp.eye(nb, dtype=F32)
        return jnp.einsum('gnij,nm->gnimj', w, eye).reshape(2, half, half).astype(BF16)

    softplus_neg_lam = jax.nn.softplus(-lam.astype(F32)).reshape(1, c)
    vec = pl.BlockSpec((1, c), lambda i: (0, 0))
    wspec = pl.BlockSpec((2, half, half), lambda i: (0, 0, 0))
    blk = pl.BlockSpec((t_len, bn, c), lambda i: (ck(i), 0, 0))
    in_specs = [
        pl.BlockSpec((2, bn, c), lambda i: (jnp.maximum(ck(i) * (t_len // 2) - 1, 0), 0, 0)),
        blk,
        pl.BlockSpec((1, bn, c), lambda i: (jnp.minimum((ck(i) + 1) * t_len, s_len - 1), 0, 0)),
    ]
    args = [ug3, ug3, ug3]
    scratch = [pltpu.VMEM((t_len + 3, bn, c), F32), pltpu.VMEM((t_len, bn, c), F32),
               pltpu.VMEM((t_len, bn, c), F32)]
    if reverse:
        in_specs += [pl.BlockSpec((t_len, bn, c), lambda i: (ck(i), 0, 1)), blk]
        args += [ug3, hf]
        scratch.append(pltpu.VMEM((t_len, bn, c), F32))
    scratch.append(pltpu.VMEM((bn, c), F32))
    in_specs += [pl.BlockSpec((4, c), lambda i: (0, 0)), vec, wspec, vec, wspec, vec, vec]
    args += [conv_w.astype(F32), conv_b.astype(F32).reshape(1, c), dense_halves(wa),
             ba.astype(F32).reshape(1, c), dense_halves(wx), bx.astype(F32).reshape(1, c), softplus_neg_lam]
    return pl.pallas_call(
        functools.partial(_rglru_kernel, reverse=reverse, n_chunks=n_chunks),
        out_shape=jax.ShapeDtypeStruct((s_len, bn, c), BF16 if reverse else F32),
        grid=(n_chunks,),
        in_specs=in_specs,
        out_specs=blk,
        scratch_shapes=scratch,
        compiler_params=_cparams(("arbitrary",)),
        name="rglru_bw" if reverse else "rglru_fw",
    )(*args)


def _hg_matrices(c, reverse):
    n_levels = int(math.log2(c))
    t = np.arange(c)[:, None]
    r = np.arange(c)[None, :]
    mats, masks, upper = [], [], []
    if not reverse:
        mats.append(r <= t)
        mats.append(r > t)
    else:
        mats.append(r >= t)
        mats.append(r < t)
    for lev in range(n_levels):
        half = c >> (lev + 1)
        parent = 2 * half
        start = (np.arange(c) // parent) * parent
        mid = (start + half)[:, None]
        later = (np.arange(c) % parent >= half)[:, None]
        if not reverse:
            m = np.where(later, (r >= mid) & (r <= t), (r > t) & (r < mid))
            is_q = later
        else:
            m = np.where(later, (r >= mid) & (r < t), (r >= t) & (r < mid))
            is_q = ~later
        mats.append(m)
        same_parent = (start[:, None] == start[None, :])
        masks.append(same_parent & is_q & (~is_q).T)
        upper.append(np.broadcast_to(is_q, (c, HG_DK)))
    mat = np.concatenate(mats, axis=0).astype(np.float32)
    return (np.concatenate([mat, mat], axis=1), np.stack(masks).astype(np.float32),
            np.stack(upper).astype(np.float32))


def _hgrn_kernel(*refs, reverse, n_steps, n_levels):
    if reverse:
        (q_ref, f_ref, v_ref, go_ref, of_ref, yc_ref, xres_ref, wout_ref, lb_ref, gn_ref, mat_ref, mask_ref,
         isq_ref, out_ref, state, x_scr, oi_scr, aux_scr, dec_scr, kk_scr) = refs
    else:
        (q_ref, f_ref, v_ref, lb_ref, mat_ref, mask_ref, isq_ref, out_ref,
         state, x_scr, oi_scr, aux_scr, dec_scr, kk_scr) = refs
    step = pl.program_id(1)
    c = HG_CHUNK
    t_len = q_ref.shape[0]
    n_chunks = t_len // c

    @pl.when(step == 0)
    def _():
        state[...] = jnp.zeros_like(state)

    if reverse:
        out_ref[...] = xres_ref[...] + _dot(yc_ref[...], wout_ref[0:LRU_W, :])

    def chunk_body(ci, carry):
        ch = (n_chunks - 1 - ci) if reverse else ci
        rows = slice(ch * c, (ch + 1) * c)
        edge = (c - 1) if not reverse else 0
        for hp in range(HG_HEADS // 2):
            cols2 = slice(2 * hp * HG_DK, 2 * (hp + 1) * HG_DK)
            fl = f_ref[rows, cols2]
            e = jnp.exp(-jnp.abs(fl))
            r = 1.0 / (1.0 + e)
            er = e * r
            pos = fl > 0.0
            lb = lb_ref[0:1, cols2]
            oml = lb_ref[1:2, cols2]
            g2 = jnp.log2(lb + oml * jnp.where(pos, r, er))
            kk_scr[hp] = oml * jnp.where(pos, er, r)
            g_hi = g2.astype(BF16)
            g_lo = (g2 - g_hi.astype(F32)).astype(BF16)
            dec_scr[hp] = jnp.exp2(_dot(mat_ref[...], jnp.concatenate([g_hi, g_lo], axis=0)))

        for hd in range(HG_HEADS):
            cols = slice(hd * HG_DK, (hd + 1) * HG_DK)
            pcols = slice((hd % 2) * HG_DK, (hd % 2 + 1) * HG_DK)
            kk = kk_scr[hd // 2, :, pcols]
            dec = dec_scr.at[hd // 2]
            qv = q_ref[rows, cols].astype(F32)
            qq = qv * jax.nn.sigmoid(qv)
            qb, kb = qq.astype(BF16), kk.astype(BF16)
            d_in = dec[0:c, pcols]
            x_scr[hd, n_levels] = (qq * d_in).astype(BF16)
            x_scr[hd, n_levels + 1] = (kk * dec[c:2 * c, pcols]).astype(BF16)
            aux_scr[hd, 0] = jnp.broadcast_to(d_in[edge:edge + 1, :], (c, HG_DK))
            aux_scr[hd, 1] = jnp.broadcast_to(jnp.sum(qq * kk, axis=-1, keepdims=True), (c, HG_DK))
            for lev in range(n_levels):
                half = c >> (lev + 1)
                if half >= SUBLANES_BF16:
                    first_is_q = bool(reverse)
                    base = jnp.concatenate(
                        [(qb if (blk % 2 == 1) != first_is_q else kb)[blk * half:(blk + 1) * half]
                         for blk in range(c // half)], axis=0)
                else:
                    base = jnp.where(isq_ref[lev] > 0.5, qb, kb)
                x_scr[hd, lev] = base * dec[(2 + lev) * c:(3 + lev) * c, pcols].astype(BF16)

        for hd in range(HG_HEADS):
            cols = slice(hd * HG_DK, (hd + 1) * HG_DK)
            att = None
            for lev in range(n_levels):
                xl = x_scr[hd, lev]
                term = mask_ref[lev] * _nt_dot(xl, xl)
                att = term if att is None else att + term
            st = state[hd]
            oi_scr[hd] = _nt_dot(x_scr[hd, n_levels], st.astype(BF16))
            vt = v_ref[rows, cols].astype(F32).T.astype(BF16)
            state[hd] = st * aux_scr[hd, 0] + _dot(vt, x_scr[hd, n_levels + 1])
            x_scr[hd, 0] = att.astype(BF16)

        yd = []
        for hd in range(HG_HEADS):
            cols = slice(hd * HG_DK, (hd + 1) * HG_DK)
            vb = v_ref[rows, cols]
            o = oi_scr[hd] + _dot(x_scr[hd, 0], vb) + aux_scr[hd, 1] * vb.astype(F32)

            if reverse:
                o = o + of_ref[rows, cols]
                o = o * lax.rsqrt(jnp.mean(o * o, axis=-1, keepdims=True) + EPS) * gn_ref[...]
                gv = go_ref[rows, cols].astype(F32)
                yd.append((o * (gv * jax.nn.sigmoid(gv))).astype(BF16))
            else:
                out_ref[rows, cols] = o
        if reverse:
            out_ref[rows, :] = out_ref[rows, :] + _dot(jnp.concatenate(yd, axis=-1), wout_ref[LRU_W:, :])
        return carry

    for ci in range(n_chunks):
        chunk_body(ci, 0)


def _hgrn(hg, o_fw, lb, gnorm_g, *, reverse, x=None, yc_tm=None, w_out=None):
    bn, s_len, _ = hg["q"].shape
    w = HG_HEADS * HG_DK
    t_len = min(HG_T, s_len)
    n_steps = s_len // t_len
    mats, masks, isq = _hg_matrices(HG_CHUNK, reverse)
    n_levels = masks.shape[0]
    lbf = lb.astype(F32)
    lb_tab = jnp.stack([lbf, 1.0 - lbf])

    blk = pl.BlockSpec((None, t_len, w), lambda b, i: (b, (n_steps - 1 - i) if reverse else i, 0))

    def const(shape):
        return pl.BlockSpec(shape, lambda b, i: (0,) * len(shape))

    in_specs = [blk, blk, blk]
    args = [hg["q"], hg["f_bw" if reverse else "f_fw"], hg["i_in"]]
    out_spec, out_shape = blk, jax.ShapeDtypeStruct((bn, s_len, w), F32)
    if reverse:
        d = x.shape[-1]
        out_spec = pl.BlockSpec((None, t_len, d), lambda b, i: (b, n_steps - 1 - i, 0))
        out_shape = jax.ShapeDtypeStruct(x.shape, F32)
        in_specs += [blk, blk, pl.BlockSpec((t_len, LRU_W), lambda b, i: (n_steps - 1 - i, b)), out_spec,
                     _resident(w_out.shape, lambda b, i: (0, 0))]
        args += [hg["g_out"], o_fw, yc_tm, x, w_out]
    in_specs.append(const((2, w)))
    args.append(lb_tab)
    if reverse:
        in_specs.append(const((1, HG_DK)))
        args.append(gnorm_g.astype(F32).reshape(1, HG_DK))
    in_specs += [const(mats.shape), const(masks.shape), const(isq.shape)]
    args += [jnp.asarray(mats, BF16), jnp.asarray(masks, F32), jnp.asarray(isq, BF16)]
    return pl.pallas_call(
        functools.partial(_hgrn_kernel, reverse=reverse, n_steps=n_steps, n_levels=n_levels),
        out_shape=out_shape,
        grid=(bn, n_steps),
        in_specs=in_specs,
        out_specs=out_spec,
        scratch_shapes=[pltpu.VMEM((HG_HEADS, HG_DK, HG_DK), F32),
                        pltpu.VMEM((HG_HEADS, n_levels + 2, HG_CHUNK, HG_DK), BF16),
                        pltpu.VMEM((HG_HEADS, HG_CHUNK, HG_DK), F32),
                        pltpu.VMEM((HG_HEADS, 2, HG_CHUNK, HG_DK), F32),
                        pltpu.VMEM((HG_HEADS // 2, (n_levels + 2) * HG_CHUNK, 2 * HG_DK), F32),
                        pltpu.VMEM((HG_HEADS // 2, HG_CHUNK, 2 * HG_DK), F32)],
        compiler_params=_cparams(("parallel", "arbitrary")),
        name="hgrn_bw" if reverse else "hgrn_fw",
    )(*args)


def _even_layer(x, g, w_in, w_out, rpb):
    qkv, view4, view16 = _ev_proj(x, g, w_in.astype(BF16))
    outs, maxes, dens = zip(*[_dilated_branch(view, dil) for dil, view in zip(DILATIONS, (qkv, view4, view16))])
    yb = _neighborhood(qkv, rpb)
    return _ev_out(x, outs, maxes, dens, yb, w_out.astype(BF16))


def _odd_layer(x, g, w_in, w_out, conv_w, conv_b, wa, ba, wx, bx, lam, lb_f, lb_b, gnorm_g):
    bn, s_len, _ = x.shape
    ug, hg = _od_proj(x, g, w_in.astype(BF16))
    h_fw = _rglru(ug, None, conv_w, conv_b, wa[0], ba[0], wx[0], bx[0], lam[0], bn, reverse=False)
    yc = _rglru(ug, h_fw, conv_w, conv_b, wa[1], ba[1], wx[1], bx[1], lam[1], bn, reverse=True)
    o_fw = _hgrn(hg, None, lb_f, gnorm_g, reverse=False)
    return _hgrn(hg, o_fw, lb_b, gnorm_g, reverse=True, x=x, yc_tm=yc.reshape(s_len, bn * LRU_W),
                 w_out=w_out.astype(BF16))


def kernel(x, mem, norm_mix_g, norm_xa_g, norm_mem_g, norm_mlp_g, final_norm_g, ev_w_in, ev_w_out, na_rpb,
           od_w_in, od_w_out, conv_w, conv_b, lru_wa, lru_ba, lru_wx, lru_bx, lru_lambda, hgrn_lb_logits,
           hgrn_norm_g, xa_wq, xa_wkv, xa_wo, mlp_w1, mlp_w2):
    depth = norm_mix_g.shape[0]
    p_lb = jax.nn.softmax(hgrn_lb_logits.astype(F32), axis=0)
    lower_bounds = jnp.cumsum(p_lb, axis=0) - p_lb[0:1]
    for layer in range(depth):
        if layer % 2 == 0:
            e = layer // 2
            x = _even_layer(x, norm_mix_g[layer], ev_w_in[e], ev_w_out[e], na_rpb[e])
        else:
            o = layer // 2
            x = _odd_layer(x, norm_mix_g[layer], od_w_in[o], od_w_out[o], conv_w[o], conv_b[o],
                           lru_wa[o], lru_ba[o], lru_wx[o], lru_bx[o], lru_lambda[o],
                           lower_bounds[layer, 0], lower_bounds[layer, 1], hgrn_norm_g[o])
        kv = _mem_kv(mem, norm_mem_g[layer], xa_wkv[layer].astype(BF16))
        x = _xattn(x, norm_xa_g[layer], xa_wq[layer].astype(BF16), kv, xa_wo[layer].astype(BF16))
        x = _mlp(x, norm_mlp_g[layer], mlp_w1[layer].astype(BF16), mlp_w2[layer].astype(BF16),
                 g_final=final_norm_g if layer == depth - 1 else None)
    return x
```

```python
import functools
import math

import jax
import jax.numpy as jnp
import numpy as np
from jax import lax
from jax.experimental import pallas as pl
from jax.experimental.pallas import tpu as pltpu

F32 = jnp.float32
BF16 = jnp.bfloat16

D_MODEL = 1024
HEAD_DIM = 64
ROT_DIM = 16
ROPE_THETA = 500000.0
N_HEADS_A = 8
N_HEADS_B = 8
GROUP_W = N_HEADS_A * HEAD_DIM
DILATIONS = (1, 4, 16)
WIN_HALF = 64
GRID_W = 64
NA_ROWS = 8
NA_COLS = 16
LRU_W = 512
LRU_BLOCKS = 8
LRU_C = 8.0
HG_HEADS = 4
HG_DK = 128
XA_HEADS = 4
XA_DH = 256
D_FF = 4096
EPS = 1e-6
LOG2E = math.log2(math.e)

LANES = 128
SUBLANES_BF16 = 16
V7X_VMEM_LIMIT_BYTES = 56 * 1024 * 1024

ROW_TILE = 512
ROW_TILE_WIDE = 1024
ATT_Q_BLOCK = 512
ATT_SUB = 128
NA_ROW_GROUP = 8
LRU_T = 128
HG_CHUNK = 128
HG_T = 512


def _cparams(sem):
    return pltpu.CompilerParams(dimension_semantics=sem, vmem_limit_bytes=V7X_VMEM_LIMIT_BYTES)


def _resident(shape, index_map):
    return pl.BlockSpec(shape, index_map, pipeline_mode=pl.Buffered(1))


def _rms(x, g):
    return x * lax.rsqrt(jnp.mean(x * x, axis=-1, keepdims=True) + EPS) * g


def _nt_dot(a, b):
    return lax.dot_general(a, b, (((1,), (1,)), ((), ())), preferred_element_type=F32)


def _dot(a, b):
    return jnp.dot(a, b, preferred_element_type=F32)


def _head_selectors():
    lane = lax.broadcasted_iota(jnp.int32, (1, LANES), 1)
    low = (lane < HEAD_DIM).astype(F32)
    return low.astype(BF16), (1.0 - low).astype(BF16)


def _ev_proj_kernel(x_ref, g_ref, w_ref, rc_ref, rs1_ref, rs2_ref, o_ref, o4_ref, o16_ref, slab):
    h = _rms(x_ref[...], g_ref[...]).astype(BF16)
    scale = HEAD_DIM ** -0.5 * LOG2E
    tm = x_ref.shape[0]
    n_slabs = GROUP_W // LANES
    for c in range(6):
        acc = _dot(h, w_ref[:, c * GROUP_W:(c + 1) * GROUP_W])
        if c in (0, 1):
            rc, rs1, rs2 = rc_ref[...], rs1_ref[...], rs2_ref[...]
            parts = []
            for j in range(n_slabs):
                t = acc[:, j * LANES:(j + 1) * LANES]
                t = t * rc + pltpu.roll(t, LANES - ROT_DIM // 2, 1) * rs1 + pltpu.roll(t, ROT_DIM // 2, 1) * rs2
                parts.append(t)
            acc = jnp.concatenate(parts, axis=-1)
        if c in (0, 3):
            acc = acc * scale
        o_ref[:, c * GROUP_W:(c + 1) * GROUP_W] = acc.astype(BF16)
        if c < 3:
            for j in range(n_slabs):
                slab[j] = acc[:, j * LANES:(j + 1) * LANES]
            for dil, dst in ((DILATIONS[1], o4_ref), (DILATIONS[2], o16_ref)):
                for r in range(dil):
                    for j in range(n_slabs):
                        col = r * 3 * GROUP_W + c * GROUP_W + j * LANES
                        dst[:, col:col + LANES] = slab[j, pl.ds(r, tm // dil, stride=dil), :].astype(BF16)


def _rope_tables(s_len):
    half = ROT_DIM // 2
    inv = jnp.asarray(ROPE_THETA ** (-np.arange(half) * 2.0 / ROT_DIM), F32)
    ang = jnp.arange(s_len, dtype=F32)[:, None] * inv[None, :]
    cos, sin = jnp.cos(ang), jnp.sin(ang)
    ones = jnp.ones((s_len, HEAD_DIM - ROT_DIM), F32)
    zeros = jnp.zeros((s_len, HEAD_DIM - ROT_DIM), F32)
    zh = jnp.zeros((s_len, half), F32)
    rc = jnp.concatenate([cos, cos, ones], axis=-1)
    rs1 = jnp.concatenate([-sin, zh, zeros], axis=-1)
    rs2 = jnp.concatenate([zh, sin, zeros], axis=-1)
    rep = LANES // HEAD_DIM
    return jnp.tile(rc, (1, rep)), jnp.tile(rs1, (1, rep)), jnp.tile(rs2, (1, rep))


def _ev_proj(x, g, w_bf16):
    bn, s_len, d = x.shape
    tm = min(ROW_TILE_WIDE, s_len)
    n_out = w_bf16.shape[1]
    rc, rs1, rs2 = _rope_tables(s_len)
    tbl_spec = pl.BlockSpec((tm, LANES), lambda b, i: (i, 0))
    wa = 3 * GROUP_W
    d4, d16 = DILATIONS[1], DILATIONS[2]
    return pl.pallas_call(
        _ev_proj_kernel,
        out_shape=(jax.ShapeDtypeStruct((bn, s_len, n_out), BF16),
                   jax.ShapeDtypeStruct((bn, s_len // d4, d4 * wa), BF16),
                   jax.ShapeDtypeStruct((bn, s_len // d16, d16 * wa), BF16)),
        grid=(bn, s_len // tm),
        in_specs=[
            pl.BlockSpec((None, tm, d), lambda b, i: (b, i, 0)),
            pl.BlockSpec((1, d), lambda b, i: (0, 0)),
            _resident((d, n_out), lambda b, i: (0, 0)),
            tbl_spec, tbl_spec, tbl_spec,
        ],
        out_specs=(pl.BlockSpec((None, tm, n_out), lambda b, i: (b, i, 0)),
                   pl.BlockSpec((None, tm // d4, d4 * wa), lambda b, i: (b, i, 0)),
                   pl.BlockSpec((None, tm // d16, d16 * wa), lambda b, i: (b, i, 0))),
        scratch_shapes=[pltpu.VMEM((GROUP_W // LANES, tm, LANES), F32)],
        compiler_params=_cparams(("parallel", "parallel")),
        name="ev_proj",
    )(x, g.reshape(1, d), w_bf16, rc, rs1, rs2)


def _dilated_kernel(q_ref, kl_ref, kc_ref, kr_ref, vl_ref, vc_ref, vr_ref, o_ref, max_ref, den_ref,
                    kext, vext, s_scr, p_scr, *, lq, l_total):
    i = pl.program_id(2)
    kext[0:WIN_HALF, :] = kl_ref[...]
    kext[WIN_HALF:WIN_HALF + lq, :] = kc_ref[...]
    kext[WIN_HALF + lq:, :] = kr_ref[...]
    vext[0:WIN_HALF, :] = vl_ref[...]
    vext[WIN_HALF:WIN_HALF + lq, :] = vc_ref[...]
    vext[WIN_HALF + lq:, :] = vr_ref[...]

    wk = ATT_SUB + 2 * WIN_HALF
    qi = lax.broadcasted_iota(jnp.int32, (ATT_SUB, wk), 0)
    ci = lax.broadcasted_iota(jnp.int32, (ATT_SUB, wk), 1)
    band_bias = jnp.where((ci - qi >= 0) & (ci - qi <= 2 * WIN_HALF), 0.0, -jnp.inf).astype(F32)
    crow = lax.broadcasted_iota(jnp.int32, (1, wk), 1)
    lane = lax.broadcasted_iota(jnp.int32, (ATT_SUB, LANES), 1)
    low_half = lane < HEAD_DIM
    sel_lo, sel_hi = _head_selectors()
    n_sub = lq // ATT_SUB
    n_hp = GROUP_W // LANES

    for j in range(n_sub):
        base = i * lq + j * ATT_SUB - WIN_HALF
        in_seq = (crow + base >= 0) & (crow + base < l_total)
        bias = band_bias + jnp.where(in_seq, 0.0, -jnp.inf).astype(F32)
        bias2 = jnp.concatenate([bias, bias], axis=0)
        rows = slice(j * ATT_SUB, (j + 1) * ATT_SUB)
        wrows = slice(j * ATT_SUB, j * ATT_SUB + wk)
        for hp in range(n_hp):
            cols = slice(hp * LANES, (hp + 1) * LANES)
            qp = q_ref[rows, cols]
            q2 = jnp.concatenate([qp * sel_lo, qp * sel_hi], axis=0)
            s_scr[j * n_hp + hp] = _nt_dot(q2, kext[wrows, cols]) + bias2
    for j in range(n_sub):
        m_tile = jnp.zeros((ATT_SUB, LANES), F32)
        den_tile = jnp.ones((ATT_SUB, LANES), F32)
        for hp in range(n_hp):
            t = j * n_hp + hp
            s = s_scr[t]
            m = jnp.max(s, axis=-1, keepdims=True)
            p = jnp.exp2(s - m)
            den = jnp.sum(p, axis=-1, keepdims=True)
            p_scr[t] = p.astype(BF16)
            is_lo, is_hi = lane == 2 * hp, lane == 2 * hp + 1
            m_tile = jnp.where(is_lo, m[0:ATT_SUB], jnp.where(is_hi, m[ATT_SUB:], m_tile))
            den_tile = jnp.where(is_lo, den[0:ATT_SUB], jnp.where(is_hi, den[ATT_SUB:], den_tile))
        max_ref[j * ATT_SUB:(j + 1) * ATT_SUB, :] = m_tile
        den_ref[j * ATT_SUB:(j + 1) * ATT_SUB, :] = den_tile
    for j in range(n_sub):
        rows = slice(j * ATT_SUB, (j + 1) * ATT_SUB)
        wrows = slice(j * ATT_SUB, j * ATT_SUB + wk)
        for hp in range(n_hp):
            cols = slice(hp * LANES, (hp + 1) * LANES)
            o = _dot(p_scr[j * n_hp + hp], vext[wrows, cols])
            o_ref[rows, cols] = jnp.where(low_half, o[0:ATT_SUB], o[ATT_SUB:]).astype(BF16)


def _dilated_branch(view, dil):
    bn, l_total, width = view.shape
    n_groups = width // (dil * GROUP_W)
    lq = min(ATT_Q_BLOCK, l_total)
    nblk = l_total // lq
    per = lq // WIN_HALF
    n_halo = l_total // WIN_HALF
    n_units = (lq // ATT_SUB) * (GROUP_W // LANES)

    def cur(g):
        return pl.BlockSpec((None, lq, GROUP_W), lambda b, r, i: (b, i, r * n_groups + g))

    def left(g):
        return pl.BlockSpec((None, WIN_HALF, GROUP_W),
                            lambda b, r, i: (b, jnp.maximum(i * per - 1, 0), r * n_groups + g))

    def right(g):
        return pl.BlockSpec((None, WIN_HALF, GROUP_W),
                            lambda b, r, i: (b, jnp.minimum((i + 1) * per, n_halo - 1), r * n_groups + g))

    stat_shape = jax.ShapeDtypeStruct((bn, l_total, dil * LANES), F32)
    stat_spec = pl.BlockSpec((None, lq, LANES), lambda b, r, i: (b, i, r))
    return pl.pallas_call(
        functools.partial(_dilated_kernel, lq=lq, l_total=l_total),
        out_shape=(jax.ShapeDtypeStruct((bn, l_total, dil * GROUP_W), BF16), stat_shape, stat_shape),
        grid=(bn, dil, nblk),
        in_specs=[cur(0), left(1), cur(1), right(1), left(2), cur(2), right(2)],
        out_specs=(pl.BlockSpec((None, lq, GROUP_W), lambda b, r, i: (b, i, r)), stat_spec, stat_spec),
        scratch_shapes=[pltpu.VMEM((lq + 2 * WIN_HALF, GROUP_W), BF16),
                        pltpu.VMEM((lq + 2 * WIN_HALF, GROUP_W), BF16),
                        pltpu.VMEM((n_units, 2 * ATT_SUB, ATT_SUB + 2 * WIN_HALF), F32),
                        pltpu.VMEM((n_units, 2 * ATT_SUB, ATT_SUB + 2 * WIN_HALF), BF16)],
        compiler_params=_cparams(("parallel", "parallel", "parallel")),
        name=f"dilated_attn_d{dil}",
    )(view, view, view, view, view, view, view)


NA_QT = NA_ROW_GROUP * GRID_W
NA_KT = 2 * NA_QT
NA_KW = NA_ROWS * GRID_W


def _na_tables(rpb):
    rpb = rpb.astype(F32)
    nh = rpb.shape[0]
    ext = GRID_W - NA_COLS
    padded = jnp.concatenate([jnp.repeat(rpb[..., :1], ext, axis=-1), rpb,
                              jnp.repeat(rpb[..., -1:], ext, axis=-1)], axis=-1)
    t1 = jnp.stack([padded[..., GRID_W - 1 - qc:2 * GRID_W - 1 - qc] for qc in range(GRID_W)], axis=2)
    qc = np.arange(GRID_W)[:, None]
    kc = np.arange(GRID_W)[None, :]
    c0 = np.clip(qc - NA_COLS // 2, 0, GRID_W - NA_COLS)
    t1 = jnp.where((kc >= c0) & (kc < c0 + NA_COLS), t1 * LOG2E, -jnp.inf)
    per_delta = [jnp.concatenate([t1[:, kr - delta + NA_ROWS - 1] for kr in range(NA_ROWS)], axis=-1)
                 for delta in range(NA_ROWS)]
    return jnp.stack(per_delta).reshape(NA_ROWS, nh // 2, 2 * GRID_W, NA_KW)


def _na_kernel(q_ref, k_ref, v_ref, tb_ref, o_ref, s_scr, p_scr, inv_scr, *, rows):
    i0 = pl.program_id(1) * NA_ROW_GROUP
    w0 = jnp.clip(i0 - NA_ROWS // 2, 0, rows - 2 * NA_ROW_GROUP)
    lane = lax.broadcasted_iota(jnp.int32, (GRID_W, LANES), 1)
    low_half = lane < HEAD_DIM
    sel_lo, sel_hi = _head_selectors()
    n_hp = GROUP_W // LANES

    def key_rows(a):
        r0 = jnp.clip(i0 + a - NA_ROWS // 2, 0, rows - NA_ROWS)
        return pl.ds(pl.multiple_of((r0 - w0) * GRID_W, GRID_W), NA_KW), i0 + a - r0

    for a in range(NA_ROW_GROUP):
        krows, delta = key_rows(a)
        qrows = slice(a * GRID_W, (a + 1) * GRID_W)
        for hp in range(n_hp):
            cols = slice(hp * LANES, (hp + 1) * LANES)
            qp = q_ref[qrows, cols]
            q2 = jnp.concatenate([qp * sel_lo, qp * sel_hi], axis=0)
            s_scr[a * n_hp + hp] = _nt_dot(q2, k_ref[0, krows, cols]) + tb_ref[delta, hp]
    for t in range(NA_ROW_GROUP * n_hp):
        s = s_scr[t]
        p = jnp.exp2(s - jnp.max(s, axis=-1, keepdims=True))
        inv_scr[t] = jnp.broadcast_to(1.0 / jnp.sum(p, axis=-1, keepdims=True), (2 * GRID_W, LANES))
        p_scr[t] = p.astype(BF16)
    for a in range(NA_ROW_GROUP):
        krows, _ = key_rows(a)
        qrows = slice(a * GRID_W, (a + 1) * GRID_W)
        for hp in range(n_hp):
            cols = slice(hp * LANES, (hp + 1) * LANES)
            t = a * n_hp + hp
            o = _dot(p_scr[t], v_ref[0, krows, cols]) * inv_scr[t]
            o_ref[qrows, cols] = jnp.where(low_half, o[0:GRID_W], o[GRID_W:]).astype(BF16)


def _neighborhood(qkv, rpb):
    bn, s_len, _ = qkv.shape
    rows = s_len // GRID_W
    n_groups = rows // NA_ROW_GROUP
    tables = _na_tables(rpb)

    def window(g):
        return pl.BlockSpec(
            (pl.Element(1), pl.Element(NA_KT), pl.Element(GROUP_W)),
            lambda b, i: (b, jnp.clip(i * NA_ROW_GROUP - NA_ROWS // 2, 0, rows - 2 * NA_ROW_GROUP) * GRID_W,
                          g * GROUP_W))

    return pl.pallas_call(
        functools.partial(_na_kernel, rows=rows),
        out_shape=jax.ShapeDtypeStruct((bn, s_len, GROUP_W), BF16),
        grid=(bn, n_groups),
        in_specs=[pl.BlockSpec((None, NA_QT, GROUP_W), lambda b, i: (b, i, 3)), window(4), window(5),
                  pl.BlockSpec(tables.shape, lambda b, i: (0, 0, 0, 0))],
        out_specs=pl.BlockSpec((None, NA_QT, GROUP_W), lambda b, i: (b, i, 0)),
        scratch_shapes=[pltpu.VMEM((NA_ROW_GROUP * GROUP_W // LANES, 2 * GRID_W, NA_KW), F32),
                        pltpu.VMEM((NA_ROW_GROUP * GROUP_W // LANES, 2 * GRID_W, NA_KW), BF16),
                        pltpu.VMEM((NA_ROW_GROUP * GROUP_W // LANES, 2 * GRID_W, LANES), F32)],
        compiler_params=_cparams(("parallel", "parallel")),
        name="neighborhood_attn",
    )(qkv, qkv, qkv, tables)


def _xattn_block(x, g_ref, wq_ref, kv_ref, wo_ref):
    q = (_dot(_rms(x, g_ref[...]).astype(BF16), wq_ref[...]) * (XA_DH ** -0.5 * LOG2E)).astype(BF16)
    heads = []
    for hd in range(XA_HEADS):
        cols = slice(hd * XA_DH, (hd + 1) * XA_DH)
        s = _nt_dot(q[:, cols], kv_ref[:, cols])
        m = jnp.max(s, axis=-1, keepdims=True)
        p = jnp.exp2(s - m)
        den = jnp.sum(p, axis=-1, keepdims=True)
        vcols = slice(D_MODEL + hd * XA_DH, D_MODEL + (hd + 1) * XA_DH)
        heads.append((_dot(p.astype(BF16), kv_ref[:, vcols]) * (1.0 / den)).astype(BF16))
    return x + _dot(jnp.concatenate(heads, axis=-1), wo_ref[...])


def _mlp_block(x, g_ref, w1_ref, w2_ref, gf_ref, final_norm):
    h = _rms(x, g_ref[...]).astype(BF16)
    acc = x
    for c in range(D_FF // D_MODEL):
        cols = slice(c * D_MODEL, (c + 1) * D_MODEL)
        a = jnp.maximum(_dot(h, w1_ref[:, cols]), 0.0)
        acc = acc + _dot((a * a).astype(BF16), w2_ref[cols, :])
    return _rms(acc, gf_ref[...]) if final_norm else acc


def _tail_operands(tail, bn, d):
    vec = pl.BlockSpec((1, d), lambda b, i: (0, 0))
    m_len = tail["kv"].shape[1]
    specs = [vec, _resident((d, d), lambda b, i: (0, 0)),
             pl.BlockSpec((None, m_len, 2 * d), lambda b, i: (b, 0, 0)),
             _resident((d, d), lambda b, i: (0, 0)),
             vec, _resident((d, D_FF), lambda b, i: (0, 0)), _resident((D_FF, d), lambda b, i: (0, 0)), vec]
    final_norm = tail["g_final"] is not None
    gf = tail["g_final"] if final_norm else tail["g_mlp"]
    args = [tail["g_xa"].reshape(1, d), tail["wq"], tail["kv"], tail["wo"],
            tail["g_mlp"].reshape(1, d), tail["w1"], tail["w2"], gf.reshape(1, d)]
    return specs, args, final_norm


def _ev_tail_kernel(o1, o2v, o3v, m1, m2v, m3v, d1, d2v, d3v, yb_ref, ex_ref, w_ref, x_ref,
                    gxa_ref, wq_ref, kv_ref, wo_ref, gmlp_ref, w1_ref, w2_ref, gf_ref, out_ref,
                    so2, so3, st2, st3, *, final_norm):
    tm = x_ref.shape[0]
    n_slabs = GROUP_W // LANES
    for dil, ov, mv, dv, so, st in ((DILATIONS[1], o2v, m2v, d2v, so2, st2), (DILATIONS[2], o3v, m3v, d3v, so3, st3)):
        for r in range(dil):
            dst = pl.ds(r, tm // dil, stride=dil)
            for j in range(n_slabs):
                col = r * GROUP_W + j * LANES
                so[j, dst, :] = ov[:, col:col + LANES].astype(F32)
            st[0, dst, :] = mv[:, r * LANES:(r + 1) * LANES]
            st[1, dst, :] = dv[:, r * LANES:(r + 1) * LANES]
    o2 = jnp.concatenate([so2[j] for j in range(n_slabs)], axis=-1)
    o3 = jnp.concatenate([so3[j] for j in range(n_slabs)], axis=-1)
    ma, mb, mc = m1[...], st2[0], st3[0]
    m = jnp.maximum(jnp.maximum(ma, mb), mc)
    ea, eb, ec = jnp.exp2(ma - m), jnp.exp2(mb - m), jnp.exp2(mc - m)
    inv = 1.0 / (ea * d1[...] + eb * st2[1] + ec * st3[1])
    lane = lax.broadcasted_iota(jnp.int32, (tm, LANES), 1)
    wcat = jnp.where(lane < N_HEADS_A, ea * inv,
                     jnp.where(lane < 2 * N_HEADS_A, pltpu.roll(eb * inv, N_HEADS_A, 1),
                               pltpu.roll(ec * inv, 2 * N_HEADS_A, 1)))
    hi = wcat.astype(BF16)
    lo = (wcat - hi.astype(F32)).astype(BF16)
    wide = _dot(jnp.concatenate([hi, lo], axis=1), ex_ref[...])
    ya = (wide[:, 0:GROUP_W] * o1[...].astype(F32) + wide[:, GROUP_W:2 * GROUP_W] * o2
          + wide[:, 2 * GROUP_W:] * o3)
    y = _dot(ya.astype(BF16), w_ref[0:GROUP_W, :]) + _dot(yb_ref[...], w_ref[GROUP_W:, :])
    x1 = _xattn_block(x_ref[...] + y, gxa_ref, wq_ref, kv_ref, wo_ref)
    out_ref[...] = _mlp_block(x1, gmlp_ref, w1_ref, w2_ref, gf_ref, final_norm)


def _ev_tail(x, outs, maxes, dens, yb, w_bf16, tail):
    bn, s_len, d = x.shape
    tm = min(ROW_TILE, s_len)
    n_br = len(DILATIONS)
    ex = np.zeros((LANES, n_br * GROUP_W), np.float32)
    for i in range(n_br):
        for h in range(N_HEADS_A):
            ex[i * N_HEADS_A + h, i * GROUP_W + h * HEAD_DIM:i * GROUP_W + (h + 1) * HEAD_DIM] = 1.0
    ex = np.concatenate([ex, ex], axis=0)

    def o_spec(dil):
        return pl.BlockSpec((None, tm // dil, dil * GROUP_W), lambda b, i: (b, i, 0))

    def l_spec(dil):
        return pl.BlockSpec((None, tm // dil, dil * LANES), lambda b, i: (b, i, 0))

    x_spec = pl.BlockSpec((None, tm, d), lambda b, i: (b, i, 0))
    n_slabs = GROUP_W // LANES
    tail_specs, tail_args, final_norm = _tail_operands(tail, bn, d)
    return pl.pallas_call(
        functools.partial(_ev_tail_kernel, final_norm=final_norm),
        out_shape=jax.ShapeDtypeStruct(x.shape, F32),
        grid=(bn, s_len // tm),
        in_specs=[o_spec(d_) for d_ in DILATIONS] + 2 * [l_spec(d_) for d_ in DILATIONS]
        + [o_spec(1),
           _resident(ex.shape, lambda b, i: (0, 0)),
           _resident((2 * GROUP_W, d), lambda b, i: (0, 0)),
           x_spec] + tail_specs,
        out_specs=x_spec,
        scratch_shapes=[pltpu.VMEM((n_slabs, tm, LANES), F32), pltpu.VMEM((n_slabs, tm, LANES), F32),
                        pltpu.VMEM((2, tm, LANES), F32), pltpu.VMEM((2, tm, LANES), F32)],
        compiler_params=_cparams(("parallel", "parallel")),
        name="ev_tail",
    )(*outs, *maxes, *dens, yb, jnp.asarray(ex, BF16), w_bf16, x, *tail_args)


def _od_tail_kernel(yc_ref, yd_ref, w_ref, x_ref, gxa_ref, wq_ref, kv_ref, wo_ref, gmlp_ref, w1_ref, w2_ref,
                    gf_ref, out_ref, *, final_norm):
    half = yc_ref.shape[-1]
    x0 = x_ref[...] + _dot(yc_ref[...], w_ref[0:half, :]) + _dot(yd_ref[...], w_ref[half:, :])
    x1 = _xattn_block(x0, gxa_ref, wq_ref, kv_ref, wo_ref)
    out_ref[...] = _mlp_block(x1, gmlp_ref, w1_ref, w2_ref, gf_ref, final_norm)


def _od_tail(x, yc_tm, yd, w_bf16, tail):
    bn, s_len, d = x.shape
    tm = min(ROW_TILE, s_len)
    c = yd.shape[-1]
    x_spec = pl.BlockSpec((None, tm, d), lambda b, i: (b, i, 0))
    tail_specs, tail_args, final_norm = _tail_operands(tail, bn, d)
    return pl.pallas_call(
        functools.partial(_od_tail_kernel, final_norm=final_norm),
        out_shape=jax.ShapeDtypeStruct(x.shape, F32),
        grid=(bn, s_len // tm),
        in_specs=[pl.BlockSpec((tm, c), lambda b, i: (i, b)),
                  pl.BlockSpec((None, tm, c), lambda b, i: (b, i, 0)),
                  _resident((2 * c, d), lambda b, i: (0, 0)),
                  x_spec] + tail_specs,
        out_specs=x_spec,
        compiler_params=_cparams(("parallel", "parallel")),
        name="od_tail",
    )(yc_tm, yd, w_bf16, x, *tail_args)


def _norm_matmul_kernel(x_ref, g_ref, w_ref, o_ref):
    o_ref[...] = _dot(_rms(x_ref[...], g_ref[...]).astype(BF16), w_ref[...]).astype(o_ref.dtype)


def _mem_kv(mem, g, w_bf16):
    bn, m_len, d = mem.shape
    n_out = w_bf16.shape[1]
    return pl.pallas_call(
        _norm_matmul_kernel,
        out_shape=jax.ShapeDtypeStruct((bn, m_len, n_out), BF16),
        grid=(bn,),
        in_specs=[pl.BlockSpec((None, m_len, d), lambda b: (b, 0, 0)),
                  pl.BlockSpec((1, d), lambda b: (0, 0)),
                  pl.BlockSpec((d, n_out), lambda b: (0, 0))],
        out_specs=pl.BlockSpec((None, m_len, n_out), lambda b: (b, 0, 0)),
        compiler_params=_cparams(("parallel",)),
        name="mem_kv",
    )(mem, g.reshape(1, d), w_bf16)


def _od_proj_kernel(x_ref, g_ref, w_ref, ug_ref, *rest):
    hg_refs, slab = rest[:-1], rest[-1]
    bn, tq, d = x_ref.shape
    h = _rms(x_ref[...].reshape(bn * tq, d), g_ref[...]).astype(BF16)
    n_ug = 2 * LRU_W
    n_slabs = GROUP_W // LANES
    for c in range(n_ug // GROUP_W):
        acc = _dot(h, w_ref[:, c * GROUP_W:(c + 1) * GROUP_W])
        for b in range(bn):
            for j in range(n_slabs):
                slab[j, pl.ds(b, tq, stride=bn), :] = acc[b * tq:(b + 1) * tq, j * LANES:(j + 1) * LANES]
        for j in range(n_slabs):
            ug_ref[:, c * GROUP_W + j * LANES:c * GROUP_W + (j + 1) * LANES] = slab[j]
    for c, hg_ref in enumerate(hg_refs):
        acc = _dot(h, w_ref[:, n_ug + c * GROUP_W:n_ug + (c + 1) * GROUP_W])
        hg_ref[...] = acc.reshape(bn, tq, GROUP_W).astype(hg_ref.dtype)


HG_INPUT_DTYPES = (("q", BF16), ("f_fw", F32), ("f_bw", F32), ("i_in", BF16), ("g_out", BF16))


def _od_proj(x, g, w_bf16):
    bn, s_len, d = x.shape
    tq = min(ROW_TILE // bn, s_len)
    n_ug = 2 * LRU_W
    n_hg = w_bf16.shape[1] - n_ug
    assert n_hg == len(HG_INPUT_DTYPES) * GROUP_W
    hg_spec = pl.BlockSpec((bn, tq, GROUP_W), lambda i: (0, i, 0))
    ug, *hg = pl.pallas_call(
        _od_proj_kernel,
        out_shape=[jax.ShapeDtypeStruct((s_len * bn, n_ug), F32)]
        + [jax.ShapeDtypeStruct((bn, s_len, GROUP_W), dt) for _, dt in HG_INPUT_DTYPES],
        grid=(s_len // tq,),
        in_specs=[pl.BlockSpec((bn, tq, d), lambda i: (0, i, 0)),
                  pl.BlockSpec((1, d), lambda i: (0, 0)),
                  pl.BlockSpec((d, n_ug + n_hg), lambda i: (0, 0))],
        out_specs=[pl.BlockSpec((tq * bn, n_ug), lambda i: (i, 0))] + [hg_spec] * len(HG_INPUT_DTYPES),
        scratch_shapes=[pltpu.VMEM((GROUP_W // LANES, tq * bn, LANES), F32)],
        compiler_params=_cparams(("parallel",)),
        name="od_proj",
    )(x, g.reshape(1, d), w_bf16)
    return ug.reshape(s_len, bn, n_ug), dict(zip([n for n, _ in HG_INPUT_DTYPES], hg))


def _gelu_tanh(x):
    return 0.5 * x * (1.0 + jnp.tanh(math.sqrt(2.0 / math.pi) * (x + 0.044715 * (x * x * x))))


def _rglru_kernel(*refs, reverse, n_chunks):
    if reverse:
        (ul_ref, u_ref, ur_ref, gate_ref, hf_ref, cw_ref, cb_ref, wa_ref, ba_ref, wx_ref, bx_ref, sp_ref,
         out_ref, ext, a_s, b_s, h_s, carry) = refs
    else:
        (ul_ref, u_ref, ur_ref, cw_ref, cb_ref, wa_ref, ba_ref, wx_ref, bx_ref, sp_ref,
         out_ref, ext, a_s, b_s, carry) = refs
        h_s = out_ref
    step = pl.program_id(0)
    chunk = (n_chunks - 1 - step) if reverse else step
    t_len, bn, c = u_ref.shape

    @pl.when(step == 0)
    def _():
        carry[...] = jnp.zeros_like(carry)

    ext[0:2] = jnp.where(chunk == 0, 0.0, ul_ref[...])
    ext[2:t_len + 2] = u_ref[...]
    ext[t_len + 2:t_len + 3] = jnp.where(chunk == n_chunks - 1, 0.0, ur_ref[...])
    uc = cb_ref[...].reshape(1, 1, c)
    for j in range(4):
        uc = uc + cw_ref[j:j + 1, :].reshape(1, 1, c) * ext[j:j + t_len]

    u2 = uc.reshape(t_len * bn, c)
    ub = u2.astype(BF16)
    half = c // 2

    def gate(w_ref, b_ref):
        z = jnp.concatenate([_dot(ub[:, :half], w_ref[0]), _dot(ub[:, half:], w_ref[1])], axis=-1)
        return 0.5 * jnp.tanh(0.5 * (z + b_ref[...])) + 0.5

    r = gate(wa_ref, ba_ref)
    ig = gate(wx_ref, bx_ref)
    a = jnp.exp((-LRU_C) * r * sp_ref[...])
    b = jnp.sqrt(1.0 - a * a) * (ig * u2)
    a_s[...] = a.reshape(t_len, bn, c)
    b_s[...] = b.reshape(t_len, bn, c)

    def body(k, h):
        t = (t_len - 1 - k) if reverse else k
        h = a_s[t] * h + b_s[t]
        h_s[t] = h
        return h

    carry[...] = lax.fori_loop(0, t_len, body, carry[...], unroll=8)

    if reverse:
        out_ref[...] = ((hf_ref[...] + h_s[...]) * _gelu_tanh(gate_ref[...])).astype(out_ref.dtype)


def _rglru(ug, hf, conv_w, conv_b, wa, ba, wx, bx, lam, bn, *, reverse):
    s_len = ug.shape[0]
    c = LRU_W
    ug3 = ug
    t_len = min(LRU_T, s_len)
    n_chunks = s_len // t_len

    def ck(i):
        return (n_chunks - 1 - i) if reverse else i

    half = c // 2
    nb = LRU_BLOCKS // 2

    def dense_halves(w):
        w = w.astype(F32).reshape(2, nb, c // LRU_BLOCKS, c // LRU_BLOCKS)
        eye = jnp.eye(nb, dtype=F32)
        return jnp.einsum('gnij,nm->gnimj', w, eye).reshape(2, half, half).astype(BF16)

    softplus_neg_lam = jax.nn.softplus(-lam.astype(F32)).reshape(1, c)
    vec = pl.BlockSpec((1, c), lambda i: (0, 0))
    wspec = pl.BlockSpec((2, half, half), lambda i: (0, 0, 0))
    blk = pl.BlockSpec((t_len, bn, c), lambda i: (ck(i), 0, 0))
    in_specs = [
        pl.BlockSpec((2, bn, c), lambda i: (jnp.maximum(ck(i) * (t_len // 2) - 1, 0), 0, 0)),
        blk,
        pl.BlockSpec((1, bn, c), lambda i: (jnp.minimum((ck(i) + 1) * t_len, s_len - 1), 0, 0)),
    ]
    args = [ug3, ug3, ug3]
    scratch = [pltpu.VMEM((t_len + 3, bn, c), F32), pltpu.VMEM((t_len, bn, c), F32),
               pltpu.VMEM((t_len, bn, c), F32)]
    if reverse:
        in_specs += [pl.BlockSpec((t_len, bn, c), lambda i: (ck(i), 0, 1)), blk]
        args += [ug3, hf]
        scratch.append(pltpu.VMEM((t_len, bn, c), F32))
    scratch.append(pltpu.VMEM((bn, c), F32))
    in_specs += [pl.BlockSpec((4, c), lambda i: (0, 0)), vec, wspec, vec, wspec, vec, vec]
    args += [conv_w.astype(F32), conv_b.astype(F32).reshape(1, c), dense_halves(wa),
             ba.astype(F32).reshape(1, c), dense_halves(wx), bx.astype(F32).reshape(1, c), softplus_neg_lam]
    return pl.pallas_call(
        functools.partial(_rglru_kernel, reverse=reverse, n_chunks=n_chunks),
        out_shape=jax.ShapeDtypeStruct((s_len, bn, c), BF16 if reverse else F32),
        grid=(n_chunks,),
        in_specs=in_specs,
        out_specs=blk,
        scratch_shapes=scratch,
        compiler_params=_cparams(("arbitrary",)),
        name="rglru_bw" if reverse else "rglru_fw",
    )(*args)


def _hg_matrices(c, reverse):
    n_levels = int(math.log2(c))
    t = np.arange(c)[:, None]
    r = np.arange(c)[None, :]
    mats, masks, upper = [], [], []
    if not reverse:
        mats.append(r <= t)
        mats.append(r > t)
    else:
        mats.append(r >= t)
        mats.append(r < t)
    for lev in range(n_levels):
        half = c >> (lev + 1)
        parent = 2 * half
        start = (np.arange(c) // parent) * parent
        mid = (start + half)[:, None]
        later = (np.arange(c) % parent >= half)[:, None]
        if not reverse:
            m = np.where(later, (r >= mid) & (r <= t), (r > t) & (r < mid))
            is_q = later
        else:
            m = np.where(later, (r >= mid) & (r < t), (r >= t) & (r < mid))
            is_q = ~later
        mats.append(m)
        same_parent = (start[:, None] == start[None, :])
        masks.append(same_parent & is_q & (~is_q).T)
        upper.append(np.broadcast_to(is_q, (c, HG_DK)))
    mat = np.concatenate(mats, axis=0).astype(np.float32)
    return (np.concatenate([mat, mat], axis=1), np.stack(masks).astype(np.float32),
            np.stack(upper).astype(np.float32))


def _hgrn_kernel(*refs, reverse, n_steps, n_levels):
    if reverse:
        (q_ref, f_ref, v_ref, go_ref, of_ref, lb_ref, gn_ref, mat_ref, mask_ref, isq_ref, out_ref,
         state, x_scr, oi_scr, aux_scr, dec_scr, kk_scr) = refs
    else:
        (q_ref, f_ref, v_ref, lb_ref, mat_ref, mask_ref, isq_ref, out_ref,
         state, x_scr, oi_scr, aux_scr, dec_scr, kk_scr) = refs
    step = pl.program_id(1)
    c = HG_CHUNK
    t_len = q_ref.shape[0]
    n_chunks = t_len // c

    @pl.when(step == 0)
    def _():
        state[...] = jnp.zeros_like(state)

    def chunk_body(ci, carry):
        ch = (n_chunks - 1 - ci) if reverse else ci
        rows = slice(ch * c, (ch + 1) * c)
        edge = (c - 1) if not reverse else 0
        for hp in range(HG_HEADS // 2):
            cols2 = slice(2 * hp * HG_DK, 2 * (hp + 1) * HG_DK)
            fl = f_ref[rows, cols2]
            e = jnp.exp(-jnp.abs(fl))
            r = 1.0 / (1.0 + e)
            er = e * r
            pos = fl > 0.0
            lb = lb_ref[0:1, cols2]
            oml = lb_ref[1:2, cols2]
            g2 = jnp.log2(lb + oml * jnp.where(pos, r, er))
            kk_scr[hp] = oml * jnp.where(pos, er, r)
            g_hi = g2.astype(BF16)
            g_lo = (g2 - g_hi.astype(F32)).astype(BF16)
            dec_scr[hp] = jnp.exp2(_dot(mat_ref[...], jnp.concatenate([g_hi, g_lo], axis=0)))

        for hd in range(HG_HEADS):
            cols = slice(hd * HG_DK, (hd + 1) * HG_DK)
            pcols = slice((hd % 2) * HG_DK, (hd % 2 + 1) * HG_DK)
            kk = kk_scr[hd // 2, :, pcols]
            dec = dec_scr.at[hd // 2]
            qv = q_ref[rows, cols].astype(F32)
            qq = qv * jax.nn.sigmoid(qv)
            qb, kb = qq.astype(BF16), kk.astype(BF16)
            d_in = dec[0:c, pcols]
            x_scr[hd, n_levels] = (qq * d_in).astype(BF16)
            x_scr[hd, n_levels + 1] = (kk * dec[c:2 * c, pcols]).astype(BF16)
            aux_scr[hd, 0] = jnp.broadcast_to(d_in[edge:edge + 1, :], (c, HG_DK))
            aux_scr[hd, 1] = jnp.broadcast_to(jnp.sum(qq * kk, axis=-1, keepdims=True), (c, HG_DK))
            for lev in range(n_levels):
                half = c >> (lev + 1)
                if half >= SUBLANES_BF16:
                    first_is_q = bool(reverse)
                    base = jnp.concatenate(
                        [(qb if (blk % 2 == 1) != first_is_q else kb)[blk * half:(blk + 1) * half]
                         for blk in range(c // half)], axis=0)
                else:
                    base = jnp.where(isq_ref[lev] > 0.5, qb, kb)
                x_scr[hd, lev] = base * dec[(2 + lev) * c:(3 + lev) * c, pcols].astype(BF16)

        for hd in range(HG_HEADS):
            cols = slice(hd * HG_DK, (hd + 1) * HG_DK)
            att = None
            for lev in range(n_levels):
                xl = x_scr[hd, lev]
                term = mask_ref[lev] * _nt_dot(xl, xl)
                att = term if att is None else att + term
            st = state[hd]
            oi_scr[hd] = _nt_dot(x_scr[hd, n_levels], st.astype(BF16))
            vt = v_ref[rows, cols].astype(F32).T.astype(BF16)
            state[hd] = st * aux_scr[hd, 0] + _dot(vt, x_scr[hd, n_levels + 1])
            x_scr[hd, 0] = att.astype(BF16)

        for hd in range(HG_HEADS):
            cols = slice(hd * HG_DK, (hd + 1) * HG_DK)
            vb = v_ref[rows, cols]
            o = oi_scr[hd] + _dot(x_scr[hd, 0], vb) + aux_scr[hd, 1] * vb.astype(F32)

            if reverse:
                o = o + of_ref[rows, cols]
                o = o * lax.rsqrt(jnp.mean(o * o, axis=-1, keepdims=True) + EPS) * gn_ref[...]
                gv = go_ref[rows, cols].astype(F32)
                out_ref[rows, cols] = (o * (gv * jax.nn.sigmoid(gv))).astype(out_ref.dtype)
            else:
                out_ref[rows, cols] = o
        return carry

    for ci in range(n_chunks):
        chunk_body(ci, 0)


def _hgrn(hg, o_fw, lb, gnorm_g, *, reverse):
    bn, s_len, _ = hg["q"].shape
    w = HG_HEADS * HG_DK
    t_len = min(HG_T, s_len)
    n_steps = s_len // t_len
    mats, masks, isq = _hg_matrices(HG_CHUNK, reverse)
    n_levels = masks.shape[0]
    lbf = lb.astype(F32)
    lb_tab = jnp.stack([lbf, 1.0 - lbf])

    blk = pl.BlockSpec((None, t_len, w), lambda b, i: (b, (n_steps - 1 - i) if reverse else i, 0))

    def const(shape):
        return pl.BlockSpec(shape, lambda b, i: (0,) * len(shape))

    in_specs = [blk, blk, blk]
    args = [hg["q"], hg["f_bw" if reverse else "f_fw"], hg["i_in"]]
    if reverse:
        in_specs += [blk, blk]
        args += [hg["g_out"], o_fw]
    in_specs.append(const((2, w)))
    args.append(lb_tab)
    if reverse:
        in_specs.append(const((1, HG_DK)))
        args.append(gnorm_g.astype(F32).reshape(1, HG_DK))
    in_specs += [const(mats.shape), const(masks.shape), const(isq.shape)]
    args += [jnp.asarray(mats, BF16), jnp.asarray(masks, F32), jnp.asarray(isq, BF16)]
    return pl.pallas_call(
        functools.partial(_hgrn_kernel, reverse=reverse, n_steps=n_steps, n_levels=n_levels),
        out_shape=jax.ShapeDtypeStruct((bn, s_len, w), BF16 if reverse else F32),
        grid=(bn, n_steps),
        in_specs=in_specs,
        out_specs=blk,
        scratch_shapes=[pltpu.VMEM((HG_HEADS, HG_DK, HG_DK), F32),
                        pltpu.VMEM((HG_HEADS, n_levels + 2, HG_CHUNK, HG_DK), BF16),
                        pltpu.VMEM((HG_HEADS, HG_CHUNK, HG_DK), F32),
                        pltpu.VMEM((HG_HEADS, 2, HG_CHUNK, HG_DK), F32),
                        pltpu.VMEM((HG_HEADS // 2, (n_levels + 2) * HG_CHUNK, 2 * HG_DK), F32),
                        pltpu.VMEM((HG_HEADS // 2, HG_CHUNK, 2 * HG_DK), F32)],
        compiler_params=_cparams(("parallel", "arbitrary")),
        name="hgrn_bw" if reverse else "hgrn_fw",
    )(*args)


def _even_layer(x, g, w_in, w_out, rpb, tail):
    qkv, view4, view16 = _ev_proj(x, g, w_in.astype(BF16))
    outs, maxes, dens = zip(*[_dilated_branch(view, dil) for dil, view in zip(DILATIONS, (qkv, view4, view16))])
    yb = _neighborhood(qkv, rpb)
    return _ev_tail(x, outs, maxes, dens, yb, w_out.astype(BF16), tail)


def _odd_layer(x, g, w_in, w_out, conv_w, conv_b, wa, ba, wx, bx, lam, lb_f, lb_b, gnorm_g, tail):
    bn, s_len, _ = x.shape
    ug, hg = _od_proj(x, g, w_in.astype(BF16))
    h_fw = _rglru(ug, None, conv_w, conv_b, wa[0], ba[0], wx[0], bx[0], lam[0], bn, reverse=False)
    yc = _rglru(ug, h_fw, conv_w, conv_b, wa[1], ba[1], wx[1], bx[1], lam[1], bn, reverse=True)
    o_fw = _hgrn(hg, None, lb_f, gnorm_g, reverse=False)
    yd = _hgrn(hg, o_fw, lb_b, gnorm_g, reverse=True)
    return _od_tail(x, yc.reshape(s_len, bn * LRU_W), yd, w_out.astype(BF16), tail)


def kernel(x, mem, norm_mix_g, norm_xa_g, norm_mem_g, norm_mlp_g, final_norm_g, ev_w_in, ev_w_out, na_rpb,
           od_w_in, od_w_out, conv_w, conv_b, lru_wa, lru_ba, lru_wx, lru_bx, lru_lambda, hgrn_lb_logits,
           hgrn_norm_g, xa_wq, xa_wkv, xa_wo, mlp_w1, mlp_w2):
    depth = norm_mix_g.shape[0]
    p_lb = jax.nn.softmax(hgrn_lb_logits.astype(F32), axis=0)
    lower_bounds = jnp.cumsum(p_lb, axis=0) - p_lb[0:1]
    for layer in range(depth):
        tail = dict(g_xa=norm_xa_g[layer], wq=xa_wq[layer].astype(BF16),
                    kv=_mem_kv(mem, norm_mem_g[layer], xa_wkv[layer].astype(BF16)),
                    wo=xa_wo[layer].astype(BF16), g_mlp=norm_mlp_g[layer],
                    w1=mlp_w1[layer].astype(BF16), w2=mlp_w2[layer].astype(BF16),
                    g_final=final_norm_g if layer == depth - 1 else None)
        if layer % 2 == 0:
            e = layer // 2
            x = _even_layer(x, norm_mix_g[layer], ev_w_in[e], ev_w_out[e], na_rpb[e], tail)
        else:
            o = layer // 2
            x = _odd_layer(x, norm_mix_g[layer], od_w_in[o], od_w_out[o], conv_w[o], conv_b[o],
                           lru_wa[o], lru_ba[o], lru_wx[o], lru_bx[o], lru_lambda[o],
                           lower_bounds[layer, 0], lower_bounds[layer, 1], hgrn_norm_g[o], tail)
    return x
```

```python
import functools
import math

import jax
import jax.numpy as jnp
import numpy as np
from jax import lax
from jax.experimental import pallas as pl
from jax.experimental.pallas import tpu as pltpu

F32 = jnp.float32
BF16 = jnp.bfloat16

D_MODEL = 1024
HEAD_DIM = 64
ROT_DIM = 16
ROPE_THETA = 500000.0
N_HEADS_A = 8
N_HEADS_B = 8
GROUP_W = N_HEADS_A * HEAD_DIM
DILATIONS = (1, 4, 16)
WIN_HALF = 64
GRID_W = 64
NA_ROWS = 8
NA_COLS = 16
LRU_W = 512
LRU_BLOCKS = 8
LRU_C = 8.0
HG_HEADS = 4
HG_DK = 128
XA_HEADS = 4
XA_DH = 256
D_FF = 4096
EPS = 1e-6
LOG2E = math.log2(math.e)

LANES = 128
SUBLANES_BF16 = 16
V7X_VMEM_LIMIT_BYTES = 56 * 1024 * 1024

ROW_TILE = 512
ROW_TILE_WIDE = 1024
ATT_Q_BLOCK = 512
ATT_SUB = 128
NA_ROW_GROUP = 8
LRU_T = 128
HG_CHUNK = 128
HG_T = 512


def _cparams(sem):
    return pltpu.CompilerParams(dimension_semantics=sem, vmem_limit_bytes=V7X_VMEM_LIMIT_BYTES)


def _resident(shape, index_map):
    return pl.BlockSpec(shape, index_map, pipeline_mode=pl.Buffered(1))


def _rms(x, g):
    return x * lax.rsqrt(jnp.mean(x * x, axis=-1, keepdims=True) + EPS) * g


def _nt_dot(a, b):
    return lax.dot_general(a, b, (((1,), (1,)), ((), ())), preferred_element_type=F32)


def _dot(a, b):
    return jnp.dot(a, b, preferred_element_type=F32)


def _head_selectors():
    lane = lax.broadcasted_iota(jnp.int32, (1, LANES), 1)
    low = (lane < HEAD_DIM).astype(F32)
    return low.astype(BF16), (1.0 - low).astype(BF16)


def _ev_proj_kernel(x_ref, g_ref, w_ref, rc_ref, rs1_ref, rs2_ref, o_ref, o4_ref, o16_ref, slab):
    h = _rms(x_ref[...], g_ref[...]).astype(BF16)
    scale = HEAD_DIM ** -0.5 * LOG2E
    tm = x_ref.shape[0]
    n_slabs = GROUP_W // LANES
    for c in range(6):
        acc = _dot(h, w_ref[:, c * GROUP_W:(c + 1) * GROUP_W])
        if c in (0, 1):
            rc, rs1, rs2 = rc_ref[...], rs1_ref[...], rs2_ref[...]
            parts = []
            for j in range(n_slabs):
                t = acc[:, j * LANES:(j + 1) * LANES]
                t = t * rc + pltpu.roll(t, LANES - ROT_DIM // 2, 1) * rs1 + pltpu.roll(t, ROT_DIM // 2, 1) * rs2
                parts.append(t)
            acc = jnp.concatenate(parts, axis=-1)
        if c in (0, 3):
            acc = acc * scale
        o_ref[:, c * GROUP_W:(c + 1) * GROUP_W] = acc.astype(BF16)
        if c < 3:
            for j in range(n_slabs):
                slab[j] = acc[:, j * LANES:(j + 1) * LANES]
            for dil, dst in ((DILATIONS[1], o4_ref), (DILATIONS[2], o16_ref)):
                for r in range(dil):
                    for j in range(n_slabs):
                        col = r * 3 * GROUP_W + c * GROUP_W + j * LANES
                        dst[:, col:col + LANES] = slab[j, pl.ds(r, tm // dil, stride=dil), :].astype(BF16)


def _rope_tables(s_len):
    half = ROT_DIM // 2
    inv = jnp.asarray(ROPE_THETA ** (-np.arange(half) * 2.0 / ROT_DIM), F32)
    ang = jnp.arange(s_len, dtype=F32)[:, None] * inv[None, :]
    cos, sin = jnp.cos(ang), jnp.sin(ang)
    ones = jnp.ones((s_len, HEAD_DIM - ROT_DIM), F32)
    zeros = jnp.zeros((s_len, HEAD_DIM - ROT_DIM), F32)
    zh = jnp.zeros((s_len, half), F32)
    rc = jnp.concatenate([cos, cos, ones], axis=-1)
    rs1 = jnp.concatenate([-sin, zh, zeros], axis=-1)
    rs2 = jnp.concatenate([zh, sin, zeros], axis=-1)
    rep = LANES // HEAD_DIM
    return jnp.tile(rc, (1, rep)), jnp.tile(rs1, (1, rep)), jnp.tile(rs2, (1, rep))


def _ev_proj(x, g, w_bf16):
    bn, s_len, d = x.shape
    tm = min(ROW_TILE_WIDE, s_len)
    n_out = w_bf16.shape[1]
    rc, rs1, rs2 = _rope_tables(s_len)
    tbl_spec = pl.BlockSpec((tm, LANES), lambda b, i: (i, 0))
    wa = 3 * GROUP_W
    d4, d16 = DILATIONS[1], DILATIONS[2]
    return pl.pallas_call(
        _ev_proj_kernel,
        out_shape=(jax.ShapeDtypeStruct((bn, s_len, n_out), BF16),
                   jax.ShapeDtypeStruct((bn, s_len // d4, d4 * wa), BF16),
                   jax.ShapeDtypeStruct((bn, s_len // d16, d16 * wa), BF16)),
        grid=(bn, s_len // tm),
        in_specs=[
            pl.BlockSpec((None, tm, d), lambda b, i: (b, i, 0)),
            pl.BlockSpec((1, d), lambda b, i: (0, 0)),
            _resident((d, n_out), lambda b, i: (0, 0)),
            tbl_spec, tbl_spec, tbl_spec,
        ],
        out_specs=(pl.BlockSpec((None, tm, n_out), lambda b, i: (b, i, 0)),
                   pl.BlockSpec((None, tm // d4, d4 * wa), lambda b, i: (b, i, 0)),
                   pl.BlockSpec((None, tm // d16, d16 * wa), lambda b, i: (b, i, 0))),
        scratch_shapes=[pltpu.VMEM((GROUP_W // LANES, tm, LANES), F32)],
        compiler_params=_cparams(("parallel", "parallel")),
        name="ev_proj",
    )(x, g.reshape(1, d), w_bf16, rc, rs1, rs2)


def _dilated_kernel(q_ref, kl_ref, kc_ref, kr_ref, vl_ref, vc_ref, vr_ref, o_ref, max_ref, den_ref,
                    kext, vext, s_scr, p_scr, *, lq, l_total):
    i = pl.program_id(2)
    kext[0:WIN_HALF, :] = kl_ref[...]
    kext[WIN_HALF:WIN_HALF + lq, :] = kc_ref[...]
    kext[WIN_HALF + lq:, :] = kr_ref[...]
    vext[0:WIN_HALF, :] = vl_ref[...]
    vext[WIN_HALF:WIN_HALF + lq, :] = vc_ref[...]
    vext[WIN_HALF + lq:, :] = vr_ref[...]

    wk = ATT_SUB + 2 * WIN_HALF
    qi = lax.broadcasted_iota(jnp.int32, (ATT_SUB, wk), 0)
    ci = lax.broadcasted_iota(jnp.int32, (ATT_SUB, wk), 1)
    band_bias = jnp.where((ci - qi >= 0) & (ci - qi <= 2 * WIN_HALF), 0.0, -jnp.inf).astype(F32)
    crow = lax.broadcasted_iota(jnp.int32, (1, wk), 1)
    lane = lax.broadcasted_iota(jnp.int32, (ATT_SUB, LANES), 1)
    low_half = lane < HEAD_DIM
    sel_lo, sel_hi = _head_selectors()
    n_sub = lq // ATT_SUB
    n_hp = GROUP_W // LANES

    for j in range(n_sub):
        base = i * lq + j * ATT_SUB - WIN_HALF
        in_seq = (crow + base >= 0) & (crow + base < l_total)
        bias = band_bias + jnp.where(in_seq, 0.0, -jnp.inf).astype(F32)
        bias2 = jnp.concatenate([bias, bias], axis=0)
        rows = slice(j * ATT_SUB, (j + 1) * ATT_SUB)
        wrows = slice(j * ATT_SUB, j * ATT_SUB + wk)
        for hp in range(n_hp):
            cols = slice(hp * LANES, (hp + 1) * LANES)
            qp = q_ref[rows, cols]
            q2 = jnp.concatenate([qp * sel_lo, qp * sel_hi], axis=0)
            s_scr[j * n_hp + hp] = _nt_dot(q2, kext[wrows, cols]) + bias2
    for j in range(n_sub):
        m_tile = jnp.zeros((ATT_SUB, LANES), F32)
        den_tile = jnp.ones((ATT_SUB, LANES), F32)
        for hp in range(n_hp):
            t = j * n_hp + hp
            s = s_scr[t]
            m = jnp.max(s, axis=-1, keepdims=True)
            p = jnp.exp2(s - m)
            den = jnp.sum(p, axis=-1, keepdims=True)
            p_scr[t] = p.astype(BF16)
            is_lo, is_hi = lane == 2 * hp, lane == 2 * hp + 1
            m_tile = jnp.where(is_lo, m[0:ATT_SUB], jnp.where(is_hi, m[ATT_SUB:], m_tile))
            den_tile = jnp.where(is_lo, den[0:ATT_SUB], jnp.where(is_hi, den[ATT_SUB:], den_tile))
        max_ref[j * ATT_SUB:(j + 1) * ATT_SUB, :] = m_tile
        den_ref[j * ATT_SUB:(j + 1) * ATT_SUB, :] = den_tile
    for j in range(n_sub):
        rows = slice(j * ATT_SUB, (j + 1) * ATT_SUB)
        wrows = slice(j * ATT_SUB, j * ATT_SUB + wk)
        for hp in range(n_hp):
            cols = slice(hp * LANES, (hp + 1) * LANES)
            o = _dot(p_scr[j * n_hp + hp], vext[wrows, cols])
            o_ref[rows, cols] = jnp.where(low_half, o[0:ATT_SUB], o[ATT_SUB:]).astype(BF16)


def _dilated_branch(view, dil):
    bn, l_total, width = view.shape
    n_groups = width // (dil * GROUP_W)
    lq = min(ATT_Q_BLOCK, l_total)
    nblk = l_total // lq
    per = lq // WIN_HALF
    n_halo = l_total // WIN_HALF
    n_units = (lq // ATT_SUB) * (GROUP_W // LANES)

    def cur(g):
        return pl.BlockSpec((None, lq, GROUP_W), lambda b, r, i: (b, i, r * n_groups + g))

    def left(g):
        return pl.BlockSpec((None, WIN_HALF, GROUP_W),
                            lambda b, r, i: (b, jnp.maximum(i * per - 1, 0), r * n_groups + g))

    def right(g):
        return pl.BlockSpec((None, WIN_HALF, GROUP_W),
                            lambda b, r, i: (b, jnp.minimum((i + 1) * per, n_halo - 1), r * n_groups + g))

    stat_shape = jax.ShapeDtypeStruct((bn, l_total, dil * LANES), F32)
    stat_spec = pl.BlockSpec((None, lq, LANES), lambda b, r, i: (b, i, r))
    return pl.pallas_call(
        functools.partial(_dilated_kernel, lq=lq, l_total=l_total),
        out_shape=(jax.ShapeDtypeStruct((bn, l_total, dil * GROUP_W), BF16), stat_shape, stat_shape),
        grid=(bn, dil, nblk),
        in_specs=[cur(0), left(1), cur(1), right(1), left(2), cur(2), right(2)],
        out_specs=(pl.BlockSpec((None, lq, GROUP_W), lambda b, r, i: (b, i, r)), stat_spec, stat_spec),
        scratch_shapes=[pltpu.VMEM((lq + 2 * WIN_HALF, GROUP_W), BF16),
                        pltpu.VMEM((lq + 2 * WIN_HALF, GROUP_W), BF16),
                        pltpu.VMEM((n_units, 2 * ATT_SUB, ATT_SUB + 2 * WIN_HALF), F32),
                        pltpu.VMEM((n_units, 2 * ATT_SUB, ATT_SUB + 2 * WIN_HALF), BF16)],
        compiler_params=_cparams(("parallel", "parallel", "parallel")),
        name=f"dilated_attn_d{dil}",
    )(view, view, view, view, view, view, view)


NA_QT = NA_ROW_GROUP * GRID_W
NA_KT = 2 * NA_QT
NA_KW = NA_ROWS * GRID_W


def _na_tables(rpb):
    rpb = rpb.astype(F32)
    nh = rpb.shape[0]
    ext = GRID_W - NA_COLS
    padded = jnp.concatenate([jnp.repeat(rpb[..., :1], ext, axis=-1), rpb,
                              jnp.repeat(rpb[..., -1:], ext, axis=-1)], axis=-1)
    skew = jnp.tile(padded, (1, 1, GRID_W + 1))[..., :2 * GRID_W * GRID_W]
    t1 = skew.reshape(nh, -1, GRID_W, 2 * GRID_W)[..., ::-1, :GRID_W]
    qc = np.arange(GRID_W)[:, None]
    kc = np.arange(GRID_W)[None, :]
    c0 = np.clip(qc - NA_COLS // 2, 0, GRID_W - NA_COLS)
    t1 = jnp.where((kc >= c0) & (kc < c0 + NA_COLS), t1 * LOG2E, -jnp.inf)
    per_delta = [jnp.transpose(t1[:, NA_ROWS - 1 - delta:2 * NA_ROWS - 1 - delta], (0, 2, 1, 3))
                 .reshape(nh, GRID_W, NA_KW) for delta in range(NA_ROWS)]
    return jnp.stack(per_delta).reshape(NA_ROWS, nh // 2, 2 * GRID_W, NA_KW)


def _na_kernel(q_ref, k_ref, v_ref, tb_ref, o_ref, s_scr, p_scr, inv_scr, *, rows):
    i0 = pl.program_id(1) * NA_ROW_GROUP
    w0 = jnp.clip(i0 - NA_ROWS // 2, 0, rows - 2 * NA_ROW_GROUP)
    lane = lax.broadcasted_iota(jnp.int32, (GRID_W, LANES), 1)
    low_half = lane < HEAD_DIM
    sel_lo, sel_hi = _head_selectors()
    n_hp = GROUP_W // LANES

    def key_rows(a):
        r0 = jnp.clip(i0 + a - NA_ROWS // 2, 0, rows - NA_ROWS)
        return pl.ds(pl.multiple_of((r0 - w0) * GRID_W, GRID_W), NA_KW), i0 + a - r0

    for a in range(NA_ROW_GROUP):
        krows, delta = key_rows(a)
        qrows = slice(a * GRID_W, (a + 1) * GRID_W)
        for hp in range(n_hp):
            cols = slice(hp * LANES, (hp + 1) * LANES)
            qp = q_ref[qrows, cols]
            q2 = jnp.concatenate([qp * sel_lo, qp * sel_hi], axis=0)
            s_scr[a * n_hp + hp] = _nt_dot(q2, k_ref[0, krows, cols]) + tb_ref[delta, hp]
    for t in range(NA_ROW_GROUP * n_hp):
        s = s_scr[t]
        p = jnp.exp2(s - jnp.max(s, axis=-1, keepdims=True))
        inv_scr[t] = jnp.broadcast_to(1.0 / jnp.sum(p, axis=-1, keepdims=True), (2 * GRID_W, LANES))
        p_scr[t] = p.astype(BF16)
    for a in range(NA_ROW_GROUP):
        krows, _ = key_rows(a)
        qrows = slice(a * GRID_W, (a + 1) * GRID_W)
        for hp in range(n_hp):
            cols = slice(hp * LANES, (hp + 1) * LANES)
            t = a * n_hp + hp
            o = _dot(p_scr[t], v_ref[0, krows, cols]) * inv_scr[t]
            o_ref[qrows, cols] = jnp.where(low_half, o[0:GRID_W], o[GRID_W:]).astype(BF16)


def _neighborhood(qkv, rpb):
    bn, s_len, _ = qkv.shape
    rows = s_len // GRID_W
    n_groups = rows // NA_ROW_GROUP
    tables = _na_tables(rpb)

    def window(g):
        return pl.BlockSpec(
            (pl.Element(1), pl.Element(NA_KT), pl.Element(GROUP_W)),
            lambda b, i: (b, jnp.clip(i * NA_ROW_GROUP - NA_ROWS // 2, 0, rows - 2 * NA_ROW_GROUP) * GRID_W,
                          g * GROUP_W))

    return pl.pallas_call(
        functools.partial(_na_kernel, rows=rows),
        out_shape=jax.ShapeDtypeStruct((bn, s_len, GROUP_W), BF16),
        grid=(bn, n_groups),
        in_specs=[pl.BlockSpec((None, NA_QT, GROUP_W), lambda b, i: (b, i, 3)), window(4), window(5),
                  pl.BlockSpec(tables.shape, lambda b, i: (0, 0, 0, 0))],
        out_specs=pl.BlockSpec((None, NA_QT, GROUP_W), lambda b, i: (b, i, 0)),
        scratch_shapes=[pltpu.VMEM((NA_ROW_GROUP * GROUP_W // LANES, 2 * GRID_W, NA_KW), F32),
                        pltpu.VMEM((NA_ROW_GROUP * GROUP_W // LANES, 2 * GRID_W, NA_KW), BF16),
                        pltpu.VMEM((NA_ROW_GROUP * GROUP_W // LANES, 2 * GRID_W, LANES), F32)],
        compiler_params=_cparams(("parallel", "parallel")),
        name="neighborhood_attn",
    )(qkv, qkv, qkv, tables)


def _xattn_block(x, g_ref, wq_ref, kv_ref, wo_ref):
    q = (_dot(_rms(x, g_ref[...]).astype(BF16), wq_ref[...]) * (XA_DH ** -0.5 * LOG2E)).astype(BF16)
    heads = []
    for hd in range(XA_HEADS):
        cols = slice(hd * XA_DH, (hd + 1) * XA_DH)
        s = _nt_dot(q[:, cols], kv_ref[:, cols])
        m = jnp.max(s, axis=-1, keepdims=True)
        p = jnp.exp2(s - m)
        den = jnp.sum(p, axis=-1, keepdims=True)
        vcols = slice(D_MODEL + hd * XA_DH, D_MODEL + (hd + 1) * XA_DH)
        heads.append((_dot(p.astype(BF16), kv_ref[:, vcols]) * (1.0 / den)).astype(BF16))
    return x + _dot(jnp.concatenate(heads, axis=-1), wo_ref[...])


def _mlp_block(x, g_ref, w1_ref, w2_ref, gf_ref, final_norm):
    h = _rms(x, g_ref[...]).astype(BF16)
    acc = x
    for c in range(D_FF // D_MODEL):
        cols = slice(c * D_MODEL, (c + 1) * D_MODEL)
        a = jnp.maximum(_dot(h, w1_ref[:, cols]), 0.0)
        acc = acc + _dot((a * a).astype(BF16), w2_ref[cols, :])
    return _rms(acc, gf_ref[...]) if final_norm else acc


def _tail_operands(tail, bn, d):
    vec = pl.BlockSpec((1, d), lambda b, i: (0, 0))
    m_len = tail["kv"].shape[1]
    specs = [vec, _resident((d, d), lambda b, i: (0, 0)),
             pl.BlockSpec((None, m_len, 2 * d), lambda b, i: (b, 0, 0)),
             _resident((d, d), lambda b, i: (0, 0)),
             vec, _resident((d, D_FF), lambda b, i: (0, 0)), _resident((D_FF, d), lambda b, i: (0, 0)), vec]
    final_norm = tail["g_final"] is not None
    gf = tail["g_final"] if final_norm else tail["g_mlp"]
    args = [tail["g_xa"].reshape(1, d), tail["wq"], tail["kv"], tail["wo"],
            tail["g_mlp"].reshape(1, d), tail["w1"], tail["w2"], gf.reshape(1, d)]
    return specs, args, final_norm


def _ev_tail_kernel(o1, o2v, o3v, m1, m2v, m3v, d1, d2v, d3v, yb_ref, ex_ref, w_ref, x_ref,
                    gxa_ref, wq_ref, kv_ref, wo_ref, gmlp_ref, w1_ref, w2_ref, gf_ref, out_ref,
                    so2, so3, st2, st3, *, final_norm):
    tm = x_ref.shape[0]
    n_slabs = GROUP_W // LANES
    for dil, ov, mv, dv, so, st in ((DILATIONS[1], o2v, m2v, d2v, so2, st2), (DILATIONS[2], o3v, m3v, d3v, so3, st3)):
        for r in range(dil):
            dst = pl.ds(r, tm // dil, stride=dil)
            for j in range(n_slabs):
                col = r * GROUP_W + j * LANES
                so[j, dst, :] = ov[:, col:col + LANES].astype(F32)
            st[0, dst, :] = mv[:, r * LANES:(r + 1) * LANES]
            st[1, dst, :] = dv[:, r * LANES:(r + 1) * LANES]
    o2 = jnp.concatenate([so2[j] for j in range(n_slabs)], axis=-1)
    o3 = jnp.concatenate([so3[j] for j in range(n_slabs)], axis=-1)
    ma, mb, mc = m1[...], st2[0], st3[0]
    m = jnp.maximum(jnp.maximum(ma, mb), mc)
    ea, eb, ec = jnp.exp2(ma - m), jnp.exp2(mb - m), jnp.exp2(mc - m)
    inv = 1.0 / (ea * d1[...] + eb * st2[1] + ec * st3[1])
    lane = lax.broadcasted_iota(jnp.int32, (tm, LANES), 1)
    wcat = jnp.where(lane < N_HEADS_A, ea * inv,
                     jnp.where(lane < 2 * N_HEADS_A, pltpu.roll(eb * inv, N_HEADS_A, 1),
                               pltpu.roll(ec * inv, 2 * N_HEADS_A, 1)))
    hi = wcat.astype(BF16)
    lo = (wcat - hi.astype(F32)).astype(BF16)
    wide = _dot(jnp.concatenate([hi, lo], axis=1), ex_ref[...])
    ya = (wide[:, 0:GROUP_W] * o1[...].astype(F32) + wide[:, GROUP_W:2 * GROUP_W] * o2
          + wide[:, 2 * GROUP_W:] * o3)
    y = _dot(ya.astype(BF16), w_ref[0:GROUP_W, :]) + _dot(yb_ref[...], w_ref[GROUP_W:, :])
    x1 = _xattn_block(x_ref[...] + y, gxa_ref, wq_ref, kv_ref, wo_ref)
    out_ref[...] = _mlp_block(x1, gmlp_ref, w1_ref, w2_ref, gf_ref, final_norm)


def _ev_tail(x, outs, maxes, dens, yb, w_bf16, tail):
    bn, s_len, d = x.shape
    tm = min(ROW_TILE, s_len)
    n_br = len(DILATIONS)
    ex = np.zeros((LANES, n_br * GROUP_W), np.float32)
    for i in range(n_br):
        for h in range(N_HEADS_A):
            ex[i * N_HEADS_A + h, i * GROUP_W + h * HEAD_DIM:i * GROUP_W + (h + 1) * HEAD_DIM] = 1.0
    ex = np.concatenate([ex, ex], axis=0)

    def o_spec(dil):
        return pl.BlockSpec((None, tm // dil, dil * GROUP_W), lambda b, i: (b, i, 0))

    def l_spec(dil):
        return pl.BlockSpec((None, tm // dil, dil * LANES), lambda b, i: (b, i, 0))

    x_spec = pl.BlockSpec((None, tm, d), lambda b, i: (b, i, 0))
    n_slabs = GROUP_W // LANES
    tail_specs, tail_args, final_norm = _tail_operands(tail, bn, d)
    return pl.pallas_call(
        functools.partial(_ev_tail_kernel, final_norm=final_norm),
        out_shape=jax.ShapeDtypeStruct(x.shape, F32),
        grid=(bn, s_len // tm),
        in_specs=[o_spec(d_) for d_ in DILATIONS] + 2 * [l_spec(d_) for d_ in DILATIONS]
        + [o_spec(1),
           _resident(ex.shape, lambda b, i: (0, 0)),
           _resident((2 * GROUP_W, d), lambda b, i: (0, 0)),
           x_spec] + tail_specs,
        out_specs=x_spec,
        scratch_shapes=[pltpu.VMEM((n_slabs, tm, LANES), F32), pltpu.VMEM((n_slabs, tm, LANES), F32),
                        pltpu.VMEM((2, tm, LANES), F32), pltpu.VMEM((2, tm, LANES), F32)],
        compiler_params=_cparams(("parallel", "parallel")),
        name="ev_tail",
    )(*outs, *maxes, *dens, yb, jnp.asarray(ex, BF16), w_bf16, x, *tail_args)


def _od_tail_kernel(yc_ref, yd_ref, w_ref, x_ref, gxa_ref, wq_ref, kv_ref, wo_ref, gmlp_ref, w1_ref, w2_ref,
                    gf_ref, out_ref, *, final_norm):
    half = yc_ref.shape[-1]
    x0 = x_ref[...] + _dot(yc_ref[...], w_ref[0:half, :]) + _dot(yd_ref[...], w_ref[half:, :])
    x1 = _xattn_block(x0, gxa_ref, wq_ref, kv_ref, wo_ref)
    out_ref[...] = _mlp_block(x1, gmlp_ref, w1_ref, w2_ref, gf_ref, final_norm)


def _od_tail(x, yc_tm, yd, w_bf16, tail):
    bn, s_len, d = x.shape
    tm = min(ROW_TILE, s_len)
    c = yd.shape[-1]
    x_spec = pl.BlockSpec((None, tm, d), lambda b, i: (b, i, 0))
    tail_specs, tail_args, final_norm = _tail_operands(tail, bn, d)
    return pl.pallas_call(
        functools.partial(_od_tail_kernel, final_norm=final_norm),
        out_shape=jax.ShapeDtypeStruct(x.shape, F32),
        grid=(bn, s_len // tm),
        in_specs=[pl.BlockSpec((tm, c), lambda b, i: (i, b)),
                  pl.BlockSpec((None, tm, c), lambda b, i: (b, i, 0)),
                  _resident((2 * c, d), lambda b, i: (0, 0)),
                  x_spec] + tail_specs,
        out_specs=x_spec,
        compiler_params=_cparams(("parallel", "parallel")),
        name="od_tail",
    )(yc_tm, yd, w_bf16, x, *tail_args)


def _norm_matmul_kernel(x_ref, g_ref, w_ref, o_ref):
    o_ref[...] = _dot(_rms(x_ref[...], g_ref[...]).astype(BF16), w_ref[...]).astype(o_ref.dtype)


def _mem_kv(mem, g, w_bf16):
    bn, m_len, d = mem.shape
    n_out = w_bf16.shape[1]
    return pl.pallas_call(
        _norm_matmul_kernel,
        out_shape=jax.ShapeDtypeStruct((bn, m_len, n_out), BF16),
        grid=(bn,),
        in_specs=[pl.BlockSpec((None, m_len, d), lambda b: (b, 0, 0)),
                  pl.BlockSpec((1, d), lambda b: (0, 0)),
                  pl.BlockSpec((d, n_out), lambda b: (0, 0))],
        out_specs=pl.BlockSpec((None, m_len, n_out), lambda b: (b, 0, 0)),
        compiler_params=_cparams(("parallel",)),
        name="mem_kv",
    )(mem, g.reshape(1, d), w_bf16)


def _od_proj_kernel(x_ref, g_ref, w_ref, ug_ref, *rest):
    hg_refs, slab = rest[:-1], rest[-1]
    bn, tq, d = x_ref.shape
    h = _rms(x_ref[...].reshape(bn * tq, d), g_ref[...]).astype(BF16)
    n_ug = 2 * LRU_W
    n_slabs = GROUP_W // LANES
    for c in range(n_ug // GROUP_W):
        acc = _dot(h, w_ref[:, c * GROUP_W:(c + 1) * GROUP_W])
        for b in range(bn):
            for j in range(n_slabs):
                slab[j, pl.ds(b, tq, stride=bn), :] = acc[b * tq:(b + 1) * tq, j * LANES:(j + 1) * LANES]
        for j in range(n_slabs):
            ug_ref[:, c * GROUP_W + j * LANES:c * GROUP_W + (j + 1) * LANES] = slab[j]
    for c, hg_ref in enumerate(hg_refs):
        acc = _dot(h, w_ref[:, n_ug + c * GROUP_W:n_ug + (c + 1) * GROUP_W])
        hg_ref[...] = acc.reshape(bn, tq, GROUP_W).astype(hg_ref.dtype)


HG_INPUT_DTYPES = (("q", BF16), ("f_fw", F32), ("f_bw", F32), ("i_in", BF16), ("g_out", BF16))


def _od_proj(x, g, w_bf16):
    bn, s_len, d = x.shape
    tq = min(ROW_TILE // bn, s_len)
    n_ug = 2 * LRU_W
    n_hg = w_bf16.shape[1] - n_ug
    assert n_hg == len(HG_INPUT_DTYPES) * GROUP_W
    hg_spec = pl.BlockSpec((bn, tq, GROUP_W), lambda i: (0, i, 0))
    ug, *hg = pl.pallas_call(
        _od_proj_kernel,
        out_shape=[jax.ShapeDtypeStruct((s_len * bn, n_ug), F32)]
        + [jax.ShapeDtypeStruct((bn, s_len, GROUP_W), dt) for _, dt in HG_INPUT_DTYPES],
        grid=(s_len // tq,),
        in_specs=[pl.BlockSpec((bn, tq, d), lambda i: (0, i, 0)),
                  pl.BlockSpec((1, d), lambda i: (0, 0)),
                  pl.BlockSpec((d, n_ug + n_hg), lambda i: (0, 0))],
        out_specs=[pl.BlockSpec((tq * bn, n_ug), lambda i: (i, 0))] + [hg_spec] * len(HG_INPUT_DTYPES),
        scratch_shapes=[pltpu.VMEM((GROUP_W // LANES, tq * bn, LANES), F32)],
        compiler_params=_cparams(("parallel",)),
        name="od_proj",
    )(x, g.reshape(1, d), w_bf16)
    return ug.reshape(s_len, bn, n_ug), dict(zip([n for n, _ in HG_INPUT_DTYPES], hg))


def _gelu_tanh(x):
    return 0.5 * x * (1.0 + jnp.tanh(math.sqrt(2.0 / math.pi) * (x + 0.044715 * (x * x * x))))


def _rglru_kernel(*refs, reverse, n_chunks):
    if reverse:
        (ul_ref, u_ref, ur_ref, gate_ref, hf_ref, cw_ref, cb_ref, wa_ref, ba_ref, wx_ref, bx_ref, sp_ref,
         out_ref, ext, a_s, b_s, h_s, carry) = refs
    else:
        (ul_ref, u_ref, ur_ref, cw_ref, cb_ref, wa_ref, ba_ref, wx_ref, bx_ref, sp_ref,
         out_ref, ext, a_s, b_s, carry) = refs
        h_s = out_ref
    step = pl.program_id(0)
    chunk = (n_chunks - 1 - step) if reverse else step
    t_len, bn, c = u_ref.shape

    @pl.when(step == 0)
    def _():
        carry[...] = jnp.zeros_like(carry)

    ext[0:2] = jnp.where(chunk == 0, 0.0, ul_ref[...])
    ext[2:t_len + 2] = u_ref[...]
    ext[t_len + 2:t_len + 3] = jnp.where(chunk == n_chunks - 1, 0.0, ur_ref[...])
    uc = cb_ref[...].reshape(1, 1, c)
    for j in range(4):
        uc = uc + cw_ref[j:j + 1, :].reshape(1, 1, c) * ext[j:j + t_len]

    u2 = uc.reshape(t_len * bn, c)
    ub = u2.astype(BF16)
    half = c // 2

    def gate(w_ref, b_ref):
        z = jnp.concatenate([_dot(ub[:, :half], w_ref[0]), _dot(ub[:, half:], w_ref[1])], axis=-1)
        return 0.5 * jnp.tanh(0.5 * (z + b_ref[...])) + 0.5

    r = gate(wa_ref, ba_ref)
    ig = gate(wx_ref, bx_ref)
    a = jnp.exp((-LRU_C) * r * sp_ref[...])
    b = jnp.sqrt(1.0 - a * a) * (ig * u2)
    a_s[...] = a.reshape(t_len, bn, c)
    b_s[...] = b.reshape(t_len, bn, c)

    def body(k, h):
        t = (t_len - 1 - k) if reverse else k
        h = a_s[t] * h + b_s[t]
        h_s[t] = h
        return h

    carry[...] = lax.fori_loop(0, t_len, body, carry[...], unroll=8)

    if reverse:
        out_ref[...] = ((hf_ref[...] + h_s[...]) * _gelu_tanh(gate_ref[...])).astype(out_ref.dtype)


def _rglru(ug, hf, conv_w, conv_b, wa, ba, wx, bx, lam, bn, *, reverse):
    s_len = ug.shape[0]
    c = LRU_W
    ug3 = ug
    t_len = min(LRU_T, s_len)
    n_chunks = s_len // t_len

    def ck(i):
        return (n_chunks - 1 - i) if reverse else i

    half = c // 2
    nb = LRU_BLOCKS // 2

    def dense_halves(w):
        w = w.astype(F32).reshape(2, nb, c // LRU_BLOCKS, c // LRU_BLOCKS)
        eye = jnp.eye(nb, dtype=F32)
        return jnp.einsum('gnij,nm->gnimj', w, eye).reshape(2, half, half).astype(BF16)

    softplus_neg_lam = jax.nn.softplus(-lam.astype(F32)).reshape(1, c)
    vec = pl.BlockSpec((1, c), lambda i: (0, 0))
    wspec = pl.BlockSpec((2, half, half), lambda i: (0, 0, 0))
    blk = pl.BlockSpec((t_len, bn, c), lambda i: (ck(i), 0, 0))
    in_specs = [
        pl.BlockSpec((2, bn, c), lambda i: (jnp.maximum(ck(i) * (t_len // 2) - 1, 0), 0, 0)),
        blk,
        pl.BlockSpec((1, bn, c), lambda i: (jnp.minimum((ck(i) + 1) * t_len, s_len - 1), 0, 0)),
    ]
    args = [ug3, ug3, ug3]
    scratch = [pltpu.VMEM((t_len + 3, bn, c), F32), pltpu.VMEM((t_len, bn, c), F32),
               pltpu.VMEM((t_len, bn, c), F32)]
    if reverse:
        in_specs += [pl.BlockSpec((t_len, bn, c), lambda i: (ck(i), 0, 1)), blk]
        args += [ug3, hf]
        scratch.append(pltpu.VMEM((t_len, bn, c), F32))
    scratch.append(pltpu.VMEM((bn, c), F32))
    in_specs += [pl.BlockSpec((4, c), lambda i: (0, 0)), vec, wspec, vec, wspec, vec, vec]
    args += [conv_w.astype(F32), conv_b.astype(F32).reshape(1, c), dense_halves(wa),
             ba.astype(F32).reshape(1, c), dense_halves(wx), bx.astype(F32).reshape(1, c), softplus_neg_lam]
    return pl.pallas_call(
        functools.partial(_rglru_kernel, reverse=reverse, n_chunks=n_chunks),
        out_shape=jax.ShapeDtypeStruct((s_len, bn, c), BF16 if reverse else F32),
        grid=(n_chunks,),
        in_specs=in_specs,
        out_specs=blk,
        scratch_shapes=scratch,
        compiler_params=_cparams(("arbitrary",)),
        name="rglru_bw" if reverse else "rglru_fw",
    )(*args)


def _hg_matrices(c, reverse):
    n_levels = int(math.log2(c))
    t = np.arange(c)[:, None]
    r = np.arange(c)[None, :]
    mats, masks, upper = [], [], []
    if not reverse:
        mats.append(r <= t)
        mats.append(r > t)
    else:
        mats.append(r >= t)
        mats.append(r < t)
    for lev in range(n_levels):
        half = c >> (lev + 1)
        parent = 2 * half
        start = (np.arange(c) // parent) * parent
        mid = (start + half)[:, None]
        later = (np.arange(c) % parent >= half)[:, None]
        if not reverse:
            m = np.where(later, (r >= mid) & (r <= t), (r > t) & (r < mid))
            is_q = later
        else:
            m = np.where(later, (r >= mid) & (r < t), (r >= t) & (r < mid))
            is_q = ~later
        mats.append(m)
        same_parent = (start[:, None] == start[None, :])
        masks.append(same_parent & is_q & (~is_q).T)
        upper.append(np.broadcast_to(is_q, (c, HG_DK)))
    mat = np.concatenate(mats, axis=0).astype(np.float32)
    return (np.concatenate([mat, mat], axis=1), np.stack(masks).astype(np.float32),
            np.stack(upper).astype(np.float32))


def _hgrn_kernel(*refs, reverse, n_steps, n_levels):
    if reverse:
        (q_ref, f_ref, v_ref, go_ref, of_ref, lb_ref, gn_ref, mat_ref, mask_ref, isq_ref, out_ref,
         state, x_scr, oi_scr, aux_scr, dec_scr, kk_scr) = refs
    else:
        (q_ref, f_ref, v_ref, lb_ref, mat_ref, mask_ref, isq_ref, out_ref,
         state, x_scr, oi_scr, aux_scr, dec_scr, kk_scr) = refs
    step = pl.program_id(1)
    c = HG_CHUNK
    t_len = q_ref.shape[0]
    n_chunks = t_len // c

    @pl.when(step == 0)
    def _():
        state[...] = jnp.zeros_like(state)

    def chunk_body(ci, carry):
        ch = (n_chunks - 1 - ci) if reverse else ci
        rows = slice(ch * c, (ch + 1) * c)
        edge = (c - 1) if not reverse else 0
        for hp in range(HG_HEADS // 2):
            cols2 = slice(2 * hp * HG_DK, 2 * (hp + 1) * HG_DK)
            fl = f_ref[rows, cols2]
            e = jnp.exp(-jnp.abs(fl))
            r = 1.0 / (1.0 + e)
            er = e * r
            pos = fl > 0.0
            lb = lb_ref[0:1, cols2]
            oml = lb_ref[1:2, cols2]
            g2 = jnp.log2(lb + oml * jnp.where(pos, r, er))
            kk_scr[hp] = oml * jnp.where(pos, er, r)
            g_hi = g2.astype(BF16)
            g_lo = (g2 - g_hi.astype(F32)).astype(BF16)
            dec_scr[hp] = jnp.exp2(_dot(mat_ref[...], jnp.concatenate([g_hi, g_lo], axis=0)))

        for hd in range(HG_HEADS):
            cols = slice(hd * HG_DK, (hd + 1) * HG_DK)
            pcols = slice((hd % 2) * HG_DK, (hd % 2 + 1) * HG_DK)
            kk = kk_scr[hd // 2, :, pcols]
            dec = dec_scr.at[hd // 2]
            qv = q_ref[rows, cols].astype(F32)
            qq = qv * jax.nn.sigmoid(qv)
            qb, kb = qq.astype(BF16), kk.astype(BF16)
            d_in = dec[0:c, pcols]
            x_scr[hd, n_levels] = (qq * d_in).astype(BF16)
            x_scr[hd, n_levels + 1] = (kk * dec[c:2 * c, pcols]).astype(BF16)
            aux_scr[hd, 0] = jnp.broadcast_to(d_in[edge:edge + 1, :], (c, HG_DK))
            aux_scr[hd, 1] = jnp.broadcast_to(jnp.sum(qq * kk, axis=-1, keepdims=True), (c, HG_DK))
            for lev in range(n_levels):
                half = c >> (lev + 1)
                if half >= SUBLANES_BF16:
                    first_is_q = bool(reverse)
                    base = jnp.concatenate(
                        [(qb if (blk % 2 == 1) != first_is_q else kb)[blk * half:(blk + 1) * half]
                         for blk in range(c // half)], axis=0)
                else:
                    base = jnp.where(isq_ref[lev] > 0.5, qb, kb)
                x_scr[hd, lev] = base * dec[(2 + lev) * c:(3 + lev) * c, pcols].astype(BF16)

        for hd in range(HG_HEADS):
            cols = slice(hd * HG_DK, (hd + 1) * HG_DK)
            att = None
            for lev in range(n_levels):
                xl = x_scr[hd, lev]
                term = mask_ref[lev] * _nt_dot(xl, xl)
                att = term if att is None else att + term
            st = state[hd]
            oi_scr[hd] = _nt_dot(x_scr[hd, n_levels], st.astype(BF16))
            vt = v_ref[rows, cols].astype(F32).T.astype(BF16)
            state[hd] = st * aux_scr[hd, 0] + _dot(vt, x_scr[hd, n_levels + 1])
            x_scr[hd, 0] = att.astype(BF16)

        for hd in range(HG_HEADS):
            cols = slice(hd * HG_DK, (hd + 1) * HG_DK)
            vb = v_ref[rows, cols]
            o = oi_scr[hd] + _dot(x_scr[hd, 0], vb) + aux_scr[hd, 1] * vb.astype(F32)

            if reverse:
                o = o + of_ref[rows, cols]
                o = o * lax.rsqrt(jnp.mean(o * o, axis=-1, keepdims=True) + EPS) * gn_ref[...]
                gv = go_ref[rows, cols].astype(F32)
                out_ref[rows, cols] = (o * (gv * jax.nn.sigmoid(gv))).astype(out_ref.dtype)
            else:
                out_ref[rows, cols] = o
        return carry

    for ci in range(n_chunks):
        chunk_body(ci, 0)


def _hgrn(hg, o_fw, lb, gnorm_g, *, reverse):
    bn, s_len, _ = hg["q"].shape
    w = HG_HEADS * HG_DK
    t_len = min(HG_T, s_len)
    n_steps = s_len // t_len
    mats, masks, isq = _hg_matrices(HG_CHUNK, reverse)
    n_levels = masks.shape[0]
    lbf = lb.astype(F32)
    lb_tab = jnp.stack([lbf, 1.0 - lbf])

    blk = pl.BlockSpec((None, t_len, w), lambda b, i: (b, (n_steps - 1 - i) if reverse else i, 0))

    def const(shape):
        return pl.BlockSpec(shape, lambda b, i: (0,) * len(shape))

    in_specs = [blk, blk, blk]
    args = [hg["q"], hg["f_bw" if reverse else "f_fw"], hg["i_in"]]
    if reverse:
        in_specs += [blk, blk]
        args += [hg["g_out"], o_fw]
    in_specs.append(const((2, w)))
    args.append(lb_tab)
    if reverse:
        in_specs.append(const((1, HG_DK)))
        args.append(gnorm_g.astype(F32).reshape(1, HG_DK))
    in_specs += [const(mats.shape), const(masks.shape), const(isq.shape)]
    args += [jnp.asarray(mats, BF16), jnp.asarray(masks, F32), jnp.asarray(isq, BF16)]
    return pl.pallas_call(
        functools.partial(_hgrn_kernel, reverse=reverse, n_steps=n_steps, n_levels=n_levels),
        out_shape=jax.ShapeDtypeStruct((bn, s_len, w), BF16 if reverse else F32),
        grid=(bn, n_steps),
        in_specs=in_specs,
        out_specs=blk,
        scratch_shapes=[pltpu.VMEM((HG_HEADS, HG_DK, HG_DK), F32),
                        pltpu.VMEM((HG_HEADS, n_levels + 2, HG_CHUNK, HG_DK), BF16),
                        pltpu.VMEM((HG_HEADS, HG_CHUNK, HG_DK), F32),
                        pltpu.VMEM((HG_HEADS, 2, HG_CHUNK, HG_DK), F32),
                        pltpu.VMEM((HG_HEADS // 2, (n_levels + 2) * HG_CHUNK, 2 * HG_DK), F32),
                        pltpu.VMEM((HG_HEADS // 2, HG_CHUNK, 2 * HG_DK), F32)],
        compiler_params=_cparams(("parallel", "arbitrary")),
        name="hgrn_bw" if reverse else "hgrn_fw",
    )(*args)


def _even_layer(x, g, w_in, w_out, rpb, tail):
    qkv, view4, view16 = _ev_proj(x, g, w_in.astype(BF16))
    outs, maxes, dens = zip(*[_dilated_branch(view, dil) for dil, view in zip(DILATIONS, (qkv, view4, view16))])
    yb = _neighborhood(qkv, rpb)
    return _ev_tail(x, outs, maxes, dens, yb, w_out.astype(BF16), tail)


def _odd_layer(x, g, w_in, w_out, conv_w, conv_b, wa, ba, wx, bx, lam, lb_f, lb_b, gnorm_g, tail):
    bn, s_len, _ = x.shape
    ug, hg = _od_proj(x, g, w_in.astype(BF16))
    h_fw = _rglru(ug, None, conv_w, conv_b, wa[0], ba[0], wx[0], bx[0], lam[0], bn, reverse=False)
    yc = _rglru(ug, h_fw, conv_w, conv_b, wa[1], ba[1], wx[1], bx[1], lam[1], bn, reverse=True)
    o_fw = _hgrn(hg, None, lb_f, gnorm_g, reverse=False)
    yd = _hgrn(hg, o_fw, lb_b, gnorm_g, reverse=True)
    return _od_tail(x, yc.reshape(s_len, bn * LRU_W), yd, w_out.astype(BF16), tail)


def kernel(x, mem, norm_mix_g, norm_xa_g, norm_mem_g, norm_mlp_g, final_norm_g, ev_w_in, ev_w_out, na_rpb,
           od_w_in, od_w_out, conv_w, conv_b, lru_wa, lru_ba, lru_wx, lru_bx, lru_lambda, hgrn_lb_logits,
           hgrn_norm_g, xa_wq, xa_wkv, xa_wo, mlp_w1, mlp_w2):
    depth = norm_mix_g.shape[0]
    p_lb = jax.nn.softmax(hgrn_lb_logits.astype(F32), axis=0)
    lower_bounds = jnp.cumsum(p_lb, axis=0) - p_lb[0:1]
    for layer in range(depth):
        tail = dict(g_xa=norm_xa_g[layer], wq=xa_wq[layer].astype(BF16),
                    kv=_mem_kv(mem, norm_mem_g[layer], xa_wkv[layer].astype(BF16)),
                    wo=xa_wo[layer].astype(BF16), g_mlp=norm_mlp_g[layer],
                    w1=mlp_w1[layer].astype(BF16), w2=mlp_w2[layer].astype(BF16),
                    g_final=final_norm_g if layer == depth - 1 else None)
        if layer % 2 == 0:
            e = layer // 2
            x = _even_layer(x, norm_mix_g[layer], ev_w_in[e], ev_w_out[e], na_rpb[e], tail)
        else:
            o = layer // 2
            x = _odd_layer(x, norm_mix_g[layer], od_w_in[o], od_w_out[o], conv_w[o], conv_b[o],
                           lru_wa[o], lru_ba[o], lru_wx[o], lru_bx[o], lru_lambda[o],
                           lower_bounds[layer, 0], lower_bounds[layer, 1], hgrn_norm_g[o], tail)
    return x
```

```python
import functools
import math

import jax
import jax.numpy as jnp
import numpy as np
from jax import lax
from jax.experimental import pallas as pl
from jax.experimental.pallas import tpu as pltpu

F32 = jnp.float32
BF16 = jnp.bfloat16

D_MODEL = 1024
HEAD_DIM = 64
ROT_DIM = 16
ROPE_THETA = 500000.0
N_HEADS_A = 8
N_HEADS_B = 8
GROUP_W = N_HEADS_A * HEAD_DIM
DILATIONS = (1, 4, 16)
WIN_HALF = 64
GRID_W = 64
NA_ROWS = 8
NA_COLS = 16
LRU_W = 512
LRU_BLOCKS = 8
LRU_C = 8.0
HG_HEADS = 4
HG_DK = 128
XA_HEADS = 4
XA_DH = 256
D_FF = 4096
EPS = 1e-6
LOG2E = math.log2(math.e)

LANES = 128
SUBLANES_BF16 = 16
V7X_VMEM_LIMIT_BYTES = 56 * 1024 * 1024

ROW_TILE = 512
ROW_TILE_WIDE = 1024
ATT_Q_BLOCK = 512
ATT_SUB = 128
NA_ROW_GROUP = 8
LRU_T = 128
HG_CHUNK = 128
HG_T = 1024


def _cparams(sem):
    return pltpu.CompilerParams(dimension_semantics=sem, vmem_limit_bytes=V7X_VMEM_LIMIT_BYTES)


def _resident(shape, index_map):
    return pl.BlockSpec(shape, index_map, pipeline_mode=pl.Buffered(1))


def _rms(x, g):
    return x * lax.rsqrt(jnp.mean(x * x, axis=-1, keepdims=True) + EPS) * g


def _nt_dot(a, b):
    return lax.dot_general(a, b, (((1,), (1,)), ((), ())), preferred_element_type=F32)


def _dot(a, b):
    return jnp.dot(a, b, preferred_element_type=F32)


def _head_selectors():
    lane = lax.broadcasted_iota(jnp.int32, (1, LANES), 1)
    low = (lane < HEAD_DIM).astype(F32)
    return low.astype(BF16), (1.0 - low).astype(BF16)


def _ev_proj_kernel(x_ref, g_ref, w_ref, rc_ref, rs1_ref, rs2_ref, o_ref, o4_ref, o16_ref, slab):
    h = _rms(x_ref[...], g_ref[...]).astype(BF16)
    scale = HEAD_DIM ** -0.5 * LOG2E
    tm = x_ref.shape[0]
    n_slabs = GROUP_W // LANES
    for c in range(6):
        acc = _dot(h, w_ref[:, c * GROUP_W:(c + 1) * GROUP_W])
        if c in (0, 1):
            rc, rs1, rs2 = rc_ref[...], rs1_ref[...], rs2_ref[...]
            parts = []
            for j in range(n_slabs):
                t = acc[:, j * LANES:(j + 1) * LANES]
                t = t * rc + pltpu.roll(t, LANES - ROT_DIM // 2, 1) * rs1 + pltpu.roll(t, ROT_DIM // 2, 1) * rs2
                parts.append(t)
            acc = jnp.concatenate(parts, axis=-1)
        if c in (0, 3):
            acc = acc * scale
        o_ref[:, c * GROUP_W:(c + 1) * GROUP_W] = acc.astype(BF16)
        if c < 3:
            for j in range(n_slabs):
                slab[j] = acc[:, j * LANES:(j + 1) * LANES]
            for dil, dst in ((DILATIONS[1], o4_ref), (DILATIONS[2], o16_ref)):
                for r in range(dil):
                    for j in range(n_slabs):
                        col = r * 3 * GROUP_W + c * GROUP_W + j * LANES
                        dst[:, col:col + LANES] = slab[j, pl.ds(r, tm // dil, stride=dil), :].astype(BF16)


def _rope_tables(s_len):
    half = ROT_DIM // 2
    inv = jnp.asarray(ROPE_THETA ** (-np.arange(half) * 2.0 / ROT_DIM), F32)
    ang = jnp.arange(s_len, dtype=F32)[:, None] * inv[None, :]
    cos, sin = jnp.cos(ang), jnp.sin(ang)
    ones = jnp.ones((s_len, HEAD_DIM - ROT_DIM), F32)
    zeros = jnp.zeros((s_len, HEAD_DIM - ROT_DIM), F32)
    zh = jnp.zeros((s_len, half), F32)
    rc = jnp.concatenate([cos, cos, ones], axis=-1)
    rs1 = jnp.concatenate([-sin, zh, zeros], axis=-1)
    rs2 = jnp.concatenate([zh, sin, zeros], axis=-1)
    rep = LANES // HEAD_DIM
    return jnp.tile(rc, (1, rep)), jnp.tile(rs1, (1, rep)), jnp.tile(rs2, (1, rep))


def _ev_proj(x, g, w_bf16):
    bn, s_len, d = x.shape
    tm = min(ROW_TILE_WIDE, s_len)
    n_out = w_bf16.shape[1]
    rc, rs1, rs2 = _rope_tables(s_len)
    tbl_spec = pl.BlockSpec((tm, LANES), lambda b, i: (i, 0))
    wa = 3 * GROUP_W
    d4, d16 = DILATIONS[1], DILATIONS[2]
    return pl.pallas_call(
        _ev_proj_kernel,
        out_shape=(jax.ShapeDtypeStruct((bn, s_len, n_out), BF16),
                   jax.ShapeDtypeStruct((bn, s_len // d4, d4 * wa), BF16),
                   jax.ShapeDtypeStruct((bn, s_len // d16, d16 * wa), BF16)),
        grid=(bn, s_len // tm),
        in_specs=[
            pl.BlockSpec((None, tm, d), lambda b, i: (b, i, 0)),
            pl.BlockSpec((1, d), lambda b, i: (0, 0)),
            _resident((d, n_out), lambda b, i: (0, 0)),
            tbl_spec, tbl_spec, tbl_spec,
        ],
        out_specs=(pl.BlockSpec((None, tm, n_out), lambda b, i: (b, i, 0)),
                   pl.BlockSpec((None, tm // d4, d4 * wa), lambda b, i: (b, i, 0)),
                   pl.BlockSpec((None, tm // d16, d16 * wa), lambda b, i: (b, i, 0))),
        scratch_shapes=[pltpu.VMEM((GROUP_W // LANES, tm, LANES), F32)],
        compiler_params=_cparams(("parallel", "parallel")),
        name="ev_proj",
    )(x, g.reshape(1, d), w_bf16, rc, rs1, rs2)


def _dilated_kernel(q_ref, kl_ref, kc_ref, kr_ref, vl_ref, vc_ref, vr_ref, o_ref, max_ref, den_ref,
                    kext, vext, s_scr, p_scr, *, lq, l_total):
    i = pl.program_id(2)
    kext[0:WIN_HALF, :] = kl_ref[...]
    kext[WIN_HALF:WIN_HALF + lq, :] = kc_ref[...]
    kext[WIN_HALF + lq:, :] = kr_ref[...]
    vext[0:WIN_HALF, :] = vl_ref[...]
    vext[WIN_HALF:WIN_HALF + lq, :] = vc_ref[...]
    vext[WIN_HALF + lq:, :] = vr_ref[...]

    wk = ATT_SUB + 2 * WIN_HALF
    qi = lax.broadcasted_iota(jnp.int32, (ATT_SUB, wk), 0)
    ci = lax.broadcasted_iota(jnp.int32, (ATT_SUB, wk), 1)
    band_bias = jnp.where((ci - qi >= 0) & (ci - qi <= 2 * WIN_HALF), 0.0, -jnp.inf).astype(F32)
    crow = lax.broadcasted_iota(jnp.int32, (1, wk), 1)
    lane = lax.broadcasted_iota(jnp.int32, (ATT_SUB, LANES), 1)
    low_half = lane < HEAD_DIM
    sel_lo, sel_hi = _head_selectors()
    n_sub = lq // ATT_SUB
    n_hp = GROUP_W // LANES

    for j in range(n_sub):
        base = i * lq + j * ATT_SUB - WIN_HALF
        in_seq = (crow + base >= 0) & (crow + base < l_total)
        bias = band_bias + jnp.where(in_seq, 0.0, -jnp.inf).astype(F32)
        bias2 = jnp.concatenate([bias, bias], axis=0)
        rows = slice(j * ATT_SUB, (j + 1) * ATT_SUB)
        wrows = slice(j * ATT_SUB, j * ATT_SUB + wk)
        for hp in range(n_hp):
            cols = slice(hp * LANES, (hp + 1) * LANES)
            qp = q_ref[rows, cols]
            q2 = jnp.concatenate([qp * sel_lo, qp * sel_hi], axis=0)
            s_scr[j * n_hp + hp] = _nt_dot(q2, kext[wrows, cols]) + bias2
    for j in range(n_sub):
        m_tile = jnp.zeros((ATT_SUB, LANES), F32)
        den_tile = jnp.ones((ATT_SUB, LANES), F32)
        for hp in range(n_hp):
            t = j * n_hp + hp
            s = s_scr[t]
            m = jnp.max(s, axis=-1, keepdims=True)
            p = jnp.exp2(s - m)
            den = jnp.sum(p, axis=-1, keepdims=True)
            p_scr[t] = p.astype(BF16)
            is_lo, is_hi = lane == 2 * hp, lane == 2 * hp + 1
            m_tile = jnp.where(is_lo, m[0:ATT_SUB], jnp.where(is_hi, m[ATT_SUB:], m_tile))
            den_tile = jnp.where(is_lo, den[0:ATT_SUB], jnp.where(is_hi, den[ATT_SUB:], den_tile))
        max_ref[j * ATT_SUB:(j + 1) * ATT_SUB, :] = m_tile
        den_ref[j * ATT_SUB:(j + 1) * ATT_SUB, :] = den_tile
    for j in range(n_sub):
        rows = slice(j * ATT_SUB, (j + 1) * ATT_SUB)
        wrows = slice(j * ATT_SUB, j * ATT_SUB + wk)
        for hp in range(n_hp):
            cols = slice(hp * LANES, (hp + 1) * LANES)
            o = _dot(p_scr[j * n_hp + hp], vext[wrows, cols])
            o_ref[rows, cols] = jnp.where(low_half, o[0:ATT_SUB], o[ATT_SUB:]).astype(BF16)


def _dilated_branch(view, dil):
    bn, l_total, width = view.shape
    n_groups = width // (dil * GROUP_W)
    lq = min(ATT_Q_BLOCK, l_total)
    nblk = l_total // lq
    per = lq // WIN_HALF
    n_halo = l_total // WIN_HALF
    n_units = (lq // ATT_SUB) * (GROUP_W // LANES)

    def cur(g):
        return pl.BlockSpec((None, lq, GROUP_W), lambda b, r, i: (b, i, r * n_groups + g))

    def left(g):
        return pl.BlockSpec((None, WIN_HALF, GROUP_W),
                            lambda b, r, i: (b, jnp.maximum(i * per - 1, 0), r * n_groups + g))

    def right(g):
        return pl.BlockSpec((None, WIN_HALF, GROUP_W),
                            lambda b, r, i: (b, jnp.minimum((i + 1) * per, n_halo - 1), r * n_groups + g))

    stat_shape = jax.ShapeDtypeStruct((bn, l_total, dil * LANES), F32)
    stat_spec = pl.BlockSpec((None, lq, LANES), lambda b, r, i: (b, i, r))
    return pl.pallas_call(
        functools.partial(_dilated_kernel, lq=lq, l_total=l_total),
        out_shape=(jax.ShapeDtypeStruct((bn, l_total, dil * GROUP_W), BF16), stat_shape, stat_shape),
        grid=(bn, dil, nblk),
        in_specs=[cur(0), left(1), cur(1), right(1), left(2), cur(2), right(2)],
        out_specs=(pl.BlockSpec((None, lq, GROUP_W), lambda b, r, i: (b, i, r)), stat_spec, stat_spec),
        scratch_shapes=[pltpu.VMEM((lq + 2 * WIN_HALF, GROUP_W), BF16),
                        pltpu.VMEM((lq + 2 * WIN_HALF, GROUP_W), BF16),
                        pltpu.VMEM((n_units, 2 * ATT_SUB, ATT_SUB + 2 * WIN_HALF), F32),
                        pltpu.VMEM((n_units, 2 * ATT_SUB, ATT_SUB + 2 * WIN_HALF), BF16)],
        compiler_params=_cparams(("parallel", "parallel", "parallel")),
        name=f"dilated_attn_d{dil}",
    )(view, view, view, view, view, view, view)


NA_QT = NA_ROW_GROUP * GRID_W
NA_KT = 2 * NA_QT
NA_KW = NA_ROWS * GRID_W


def _na_tables(rpb):
    rpb = rpb.astype(F32)
    nh = rpb.shape[0]
    ext = GRID_W - NA_COLS
    padded = jnp.concatenate([jnp.repeat(rpb[..., :1], ext, axis=-1), rpb,
                              jnp.repeat(rpb[..., -1:], ext, axis=-1)], axis=-1)
    skew = jnp.tile(padded, (1, 1, GRID_W + 1))[..., :2 * GRID_W * GRID_W]
    t1 = skew.reshape(nh, -1, GRID_W, 2 * GRID_W)[..., ::-1, :GRID_W]
    qc = np.arange(GRID_W)[:, None]
    kc = np.arange(GRID_W)[None, :]
    c0 = np.clip(qc - NA_COLS // 2, 0, GRID_W - NA_COLS)
    t1 = jnp.where((kc >= c0) & (kc < c0 + NA_COLS), t1 * LOG2E, -jnp.inf)
    per_delta = [jnp.transpose(t1[:, NA_ROWS - 1 - delta:2 * NA_ROWS - 1 - delta], (0, 2, 1, 3))
                 .reshape(nh, GRID_W, NA_KW) for delta in range(NA_ROWS)]
    return jnp.stack(per_delta).reshape(NA_ROWS, nh // 2, 2 * GRID_W, NA_KW)


def _na_kernel(q_ref, k_ref, v_ref, tb_ref, o_ref, s_scr, p_scr, inv_scr, *, rows):
    i0 = pl.program_id(1) * NA_ROW_GROUP
    w0 = jnp.clip(i0 - NA_ROWS // 2, 0, rows - 2 * NA_ROW_GROUP)
    lane = lax.broadcasted_iota(jnp.int32, (GRID_W, LANES), 1)
    low_half = lane < HEAD_DIM
    sel_lo, sel_hi = _head_selectors()
    n_hp = GROUP_W // LANES

    def key_rows(a):
        r0 = jnp.clip(i0 + a - NA_ROWS // 2, 0, rows - NA_ROWS)
        return pl.ds(pl.multiple_of((r0 - w0) * GRID_W, GRID_W), NA_KW), i0 + a - r0

    for a in range(NA_ROW_GROUP):
        krows, delta = key_rows(a)
        qrows = slice(a * GRID_W, (a + 1) * GRID_W)
        for hp in range(n_hp):
            cols = slice(hp * LANES, (hp + 1) * LANES)
            qp = q_ref[qrows, cols]
            q2 = jnp.concatenate([qp * sel_lo, qp * sel_hi], axis=0)
            s_scr[a * n_hp + hp] = _nt_dot(q2, k_ref[0, krows, cols]) + tb_ref[delta, hp]
    for t in range(NA_ROW_GROUP * n_hp):
        s = s_scr[t]
        p = jnp.exp2(s - jnp.max(s, axis=-1, keepdims=True))
        inv_scr[t] = jnp.broadcast_to(1.0 / jnp.sum(p, axis=-1, keepdims=True), (2 * GRID_W, LANES))
        p_scr[t] = p.astype(BF16)
    for a in range(NA_ROW_GROUP):
        krows, _ = key_rows(a)
        qrows = slice(a * GRID_W, (a + 1) * GRID_W)
        for hp in range(n_hp):
            cols = slice(hp * LANES, (hp + 1) * LANES)
            t = a * n_hp + hp
            o = _dot(p_scr[t], v_ref[0, krows, cols]) * inv_scr[t]
            o_ref[qrows, cols] = jnp.where(low_half, o[0:GRID_W], o[GRID_W:]).astype(BF16)


def _neighborhood(qkv, rpb):
    bn, s_len, _ = qkv.shape
    rows = s_len // GRID_W
    n_groups = rows // NA_ROW_GROUP
    tables = _na_tables(rpb)

    def window(g):
        return pl.BlockSpec(
            (pl.Element(1), pl.Element(NA_KT), pl.Element(GROUP_W)),
            lambda b, i: (b, jnp.clip(i * NA_ROW_GROUP - NA_ROWS // 2, 0, rows - 2 * NA_ROW_GROUP) * GRID_W,
                          g * GROUP_W))

    return pl.pallas_call(
        functools.partial(_na_kernel, rows=rows),
        out_shape=jax.ShapeDtypeStruct((bn, s_len, GROUP_W), BF16),
        grid=(bn, n_groups),
        in_specs=[pl.BlockSpec((None, NA_QT, GROUP_W), lambda b, i: (b, i, 3)), window(4), window(5),
                  pl.BlockSpec(tables.shape, lambda b, i: (0, 0, 0, 0))],
        out_specs=pl.BlockSpec((None, NA_QT, GROUP_W), lambda b, i: (b, i, 0)),
        scratch_shapes=[pltpu.VMEM((NA_ROW_GROUP * GROUP_W // LANES, 2 * GRID_W, NA_KW), F32),
                        pltpu.VMEM((NA_ROW_GROUP * GROUP_W // LANES, 2 * GRID_W, NA_KW), BF16),
                        pltpu.VMEM((NA_ROW_GROUP * GROUP_W // LANES, 2 * GRID_W, LANES), F32)],
        compiler_params=_cparams(("parallel", "parallel")),
        name="neighborhood_attn",
    )(qkv, qkv, qkv, tables)


def _xattn_block(x, g_ref, wq_ref, kv_ref, wo_ref):
    q = (_dot(_rms(x, g_ref[...]).astype(BF16), wq_ref[...]) * (XA_DH ** -0.5 * LOG2E)).astype(BF16)
    heads = []
    for hd in range(XA_HEADS):
        cols = slice(hd * XA_DH, (hd + 1) * XA_DH)
        s = _nt_dot(q[:, cols], kv_ref[:, cols])
        m = jnp.max(s, axis=-1, keepdims=True)
        p = jnp.exp2(s - m)
        den = jnp.sum(p, axis=-1, keepdims=True)
        vcols = slice(D_MODEL + hd * XA_DH, D_MODEL + (hd + 1) * XA_DH)
        heads.append((_dot(p.astype(BF16), kv_ref[:, vcols]) * (1.0 / den)).astype(BF16))
    return x + _dot(jnp.concatenate(heads, axis=-1), wo_ref[...])


def _mlp_block(x, g_ref, w1_ref, w2_ref, gf_ref, final_norm):
    h = _rms(x, g_ref[...]).astype(BF16)
    acc = x
    for c in range(D_FF // D_MODEL):
        cols = slice(c * D_MODEL, (c + 1) * D_MODEL)
        a = jnp.maximum(_dot(h, w1_ref[:, cols]), 0.0)
        acc = acc + _dot((a * a).astype(BF16), w2_ref[cols, :])
    return _rms(acc, gf_ref[...]) if final_norm else acc


def _tail_operands(tail, bn, d):
    vec = pl.BlockSpec((1, d), lambda b, i: (0, 0))
    m_len = tail["kv"].shape[1]
    specs = [vec, _resident((d, d), lambda b, i: (0, 0)),
             pl.BlockSpec((None, m_len, 2 * d), lambda b, i: (b, 0, 0)),
             _resident((d, d), lambda b, i: (0, 0)),
             vec, _resident((d, D_FF), lambda b, i: (0, 0)), _resident((D_FF, d), lambda b, i: (0, 0)), vec]
    final_norm = tail["g_final"] is not None
    gf = tail["g_final"] if final_norm else tail["g_mlp"]
    args = [tail["g_xa"].reshape(1, d), tail["wq"], tail["kv"], tail["wo"],
            tail["g_mlp"].reshape(1, d), tail["w1"], tail["w2"], gf.reshape(1, d)]
    return specs, args, final_norm


def _ev_tail_kernel(o1, o2v, o3v, m1, m2v, m3v, d1, d2v, d3v, yb_ref, ex_ref, w_ref, x_ref,
                    gxa_ref, wq_ref, kv_ref, wo_ref, gmlp_ref, w1_ref, w2_ref, gf_ref, out_ref,
                    so2, so3, st2, st3, *, final_norm):
    tm = x_ref.shape[0]
    n_slabs = GROUP_W // LANES
    for dil, ov, mv, dv, so, st in ((DILATIONS[1], o2v, m2v, d2v, so2, st2), (DILATIONS[2], o3v, m3v, d3v, so3, st3)):
        for r in range(dil):
            dst = pl.ds(r, tm // dil, stride=dil)
            for j in range(n_slabs):
                col = r * GROUP_W + j * LANES
                so[j, dst, :] = ov[:, col:col + LANES].astype(F32)
            st[0, dst, :] = mv[:, r * LANES:(r + 1) * LANES]
            st[1, dst, :] = dv[:, r * LANES:(r + 1) * LANES]
    o2 = jnp.concatenate([so2[j] for j in range(n_slabs)], axis=-1)
    o3 = jnp.concatenate([so3[j] for j in range(n_slabs)], axis=-1)
    ma, mb, mc = m1[...], st2[0], st3[0]
    m = jnp.maximum(jnp.maximum(ma, mb), mc)
    ea, eb, ec = jnp.exp2(ma - m), jnp.exp2(mb - m), jnp.exp2(mc - m)
    inv = 1.0 / (ea * d1[...] + eb * st2[1] + ec * st3[1])
    lane = lax.broadcasted_iota(jnp.int32, (tm, LANES), 1)
    wcat = jnp.where(lane < N_HEADS_A, ea * inv,
                     jnp.where(lane < 2 * N_HEADS_A, pltpu.roll(eb * inv, N_HEADS_A, 1),
                               pltpu.roll(ec * inv, 2 * N_HEADS_A, 1)))
    hi = wcat.astype(BF16)
    lo = (wcat - hi.astype(F32)).astype(BF16)
    wide = _dot(jnp.concatenate([hi, lo], axis=1), ex_ref[...])
    ya = (wide[:, 0:GROUP_W] * o1[...].astype(F32) + wide[:, GROUP_W:2 * GROUP_W] * o2
          + wide[:, 2 * GROUP_W:] * o3)
    y = _dot(ya.astype(BF16), w_ref[0:GROUP_W, :]) + _dot(yb_ref[...], w_ref[GROUP_W:, :])
    x1 = _xattn_block(x_ref[...] + y, gxa_ref, wq_ref, kv_ref, wo_ref)
    out_ref[...] = _mlp_block(x1, gmlp_ref, w1_ref, w2_ref, gf_ref, final_norm)


def _ev_tail(x, outs, maxes, dens, yb, w_bf16, tail):
    bn, s_len, d = x.shape
    tm = min(ROW_TILE, s_len)
    n_br = len(DILATIONS)
    ex = np.zeros((LANES, n_br * GROUP_W), np.float32)
    for i in range(n_br):
        for h in range(N_HEADS_A):
            ex[i * N_HEADS_A + h, i * GROUP_W + h * HEAD_DIM:i * GROUP_W + (h + 1) * HEAD_DIM] = 1.0
    ex = np.concatenate([ex, ex], axis=0)

    def o_spec(dil):
        return pl.BlockSpec((None, tm // dil, dil * GROUP_W), lambda b, i: (b, i, 0))

    def l_spec(dil):
        return pl.BlockSpec((None, tm // dil, dil * LANES), lambda b, i: (b, i, 0))

    x_spec = pl.BlockSpec((None, tm, d), lambda b, i: (b, i, 0))
    n_slabs = GROUP_W // LANES
    tail_specs, tail_args, final_norm = _tail_operands(tail, bn, d)
    return pl.pallas_call(
        functools.partial(_ev_tail_kernel, final_norm=final_norm),
        out_shape=jax.ShapeDtypeStruct(x.shape, F32),
        grid=(bn, s_len // tm),
        in_specs=[o_spec(d_) for d_ in DILATIONS] + 2 * [l_spec(d_) for d_ in DILATIONS]
        + [o_spec(1),
           _resident(ex.shape, lambda b, i: (0, 0)),
           _resident((2 * GROUP_W, d), lambda b, i: (0, 0)),
           x_spec] + tail_specs,
        out_specs=x_spec,
        scratch_shapes=[pltpu.VMEM((n_slabs, tm, LANES), F32), pltpu.VMEM((n_slabs, tm, LANES), F32),
                        pltpu.VMEM((2, tm, LANES), F32), pltpu.VMEM((2, tm, LANES), F32)],
        compiler_params=_cparams(("parallel", "parallel")),
        name="ev_tail",
    )(*outs, *maxes, *dens, yb, jnp.asarray(ex, BF16), w_bf16, x, *tail_args)


def _od_tail_kernel(yc_ref, yd_ref, w_ref, x_ref, gxa_ref, wq_ref, kv_ref, wo_ref, gmlp_ref, w1_ref, w2_ref,
                    gf_ref, out_ref, *, final_norm):
    half = yc_ref.shape[-1]
    x0 = x_ref[...] + _dot(yc_ref[...], w_ref[0:half, :]) + _dot(yd_ref[...], w_ref[half:, :])
    x1 = _xattn_block(x0, gxa_ref, wq_ref, kv_ref, wo_ref)
    out_ref[...] = _mlp_block(x1, gmlp_ref, w1_ref, w2_ref, gf_ref, final_norm)


def _od_tail(x, yc_tm, yd, w_bf16, tail):
    bn, s_len, d = x.shape
    tm = min(ROW_TILE, s_len)
    c = yd.shape[-1]
    x_spec = pl.BlockSpec((None, tm, d), lambda b, i: (b, i, 0))
    tail_specs, tail_args, final_norm = _tail_operands(tail, bn, d)
    return pl.pallas_call(
        functools.partial(_od_tail_kernel, final_norm=final_norm),
        out_shape=jax.ShapeDtypeStruct(x.shape, F32),
        grid=(bn, s_len // tm),
        in_specs=[pl.BlockSpec((tm, c), lambda b, i: (i, b)),
                  pl.BlockSpec((None, tm, c), lambda b, i: (b, i, 0)),
                  _resident((2 * c, d), lambda b, i: (0, 0)),
                  x_spec] + tail_specs,
        out_specs=x_spec,
        compiler_params=_cparams(("parallel", "parallel")),
        name="od_tail",
    )(yc_tm, yd, w_bf16, x, *tail_args)


def _norm_matmul_kernel(x_ref, g_ref, w_ref, o_ref):
    o_ref[...] = _dot(_rms(x_ref[...], g_ref[...]).astype(BF16), w_ref[...]).astype(o_ref.dtype)


def _mem_kv(mem, g, w_bf16):
    bn, m_len, d = mem.shape
    n_out = w_bf16.shape[1]
    return pl.pallas_call(
        _norm_matmul_kernel,
        out_shape=jax.ShapeDtypeStruct((bn, m_len, n_out), BF16),
        grid=(bn,),
        in_specs=[pl.BlockSpec((None, m_len, d), lambda b: (b, 0, 0)),
                  pl.BlockSpec((1, d), lambda b: (0, 0)),
                  pl.BlockSpec((d, n_out), lambda b: (0, 0))],
        out_specs=pl.BlockSpec((None, m_len, n_out), lambda b: (b, 0, 0)),
        compiler_params=_cparams(("parallel",)),
        name="mem_kv",
    )(mem, g.reshape(1, d), w_bf16)


def _od_proj_kernel(x_ref, g_ref, w_ref, ug_ref, *rest):
    hg_refs, slab = rest[:-1], rest[-1]
    bn, tq, d = x_ref.shape
    h = _rms(x_ref[...].reshape(bn * tq, d), g_ref[...]).astype(BF16)
    n_ug = 2 * LRU_W
    n_slabs = GROUP_W // LANES
    for c in range(n_ug // GROUP_W):
        acc = _dot(h, w_ref[:, c * GROUP_W:(c + 1) * GROUP_W])
        for b in range(bn):
            for j in range(n_slabs):
                slab[j, pl.ds(b, tq, stride=bn), :] = acc[b * tq:(b + 1) * tq, j * LANES:(j + 1) * LANES]
        for j in range(n_slabs):
            ug_ref[:, c * GROUP_W + j * LANES:c * GROUP_W + (j + 1) * LANES] = slab[j]
    for c, hg_ref in enumerate(hg_refs):
        acc = _dot(h, w_ref[:, n_ug + c * GROUP_W:n_ug + (c + 1) * GROUP_W])
        hg_ref[...] = acc.reshape(bn, tq, GROUP_W).astype(hg_ref.dtype)


HG_INPUT_DTYPES = (("q", BF16), ("f_fw", F32), ("f_bw", F32), ("i_in", BF16), ("g_out", BF16))


def _od_proj(x, g, w_bf16):
    bn, s_len, d = x.shape
    tq = min(ROW_TILE // bn, s_len)
    n_ug = 2 * LRU_W
    n_hg = w_bf16.shape[1] - n_ug
    assert n_hg == len(HG_INPUT_DTYPES) * GROUP_W
    hg_spec = pl.BlockSpec((bn, tq, GROUP_W), lambda i: (0, i, 0))
    ug, *hg = pl.pallas_call(
        _od_proj_kernel,
        out_shape=[jax.ShapeDtypeStruct((s_len * bn, n_ug), F32)]
        + [jax.ShapeDtypeStruct((bn, s_len, GROUP_W), dt) for _, dt in HG_INPUT_DTYPES],
        grid=(s_len // tq,),
        in_specs=[pl.BlockSpec((bn, tq, d), lambda i: (0, i, 0)),
                  pl.BlockSpec((1, d), lambda i: (0, 0)),
                  pl.BlockSpec((d, n_ug + n_hg), lambda i: (0, 0))],
        out_specs=[pl.BlockSpec((tq * bn, n_ug), lambda i: (i, 0))] + [hg_spec] * len(HG_INPUT_DTYPES),
        scratch_shapes=[pltpu.VMEM((GROUP_W // LANES, tq * bn, LANES), F32)],
        compiler_params=_cparams(("parallel",)),
        name="od_proj",
    )(x, g.reshape(1, d), w_bf16)
    return ug.reshape(s_len, bn, n_ug), dict(zip([n for n, _ in HG_INPUT_DTYPES], hg))


def _gelu_tanh(x):
    return 0.5 * x * (1.0 + jnp.tanh(math.sqrt(2.0 / math.pi) * (x + 0.044715 * (x * x * x))))


def _rglru_kernel(*refs, reverse, n_chunks):
    if reverse:
        (ul_ref, u_ref, ur_ref, gate_ref, hf_ref, cw_ref, cb_ref, wa_ref, ba_ref, wx_ref, bx_ref, sp_ref,
         out_ref, ext, a_s, b_s, h_s, carry) = refs
    else:
        (ul_ref, u_ref, ur_ref, cw_ref, cb_ref, wa_ref, ba_ref, wx_ref, bx_ref, sp_ref,
         out_ref, ext, a_s, b_s, carry) = refs
        h_s = out_ref
    step = pl.program_id(0)
    chunk = (n_chunks - 1 - step) if reverse else step
    t_len, bn, c = u_ref.shape

    @pl.when(step == 0)
    def _():
        carry[...] = jnp.zeros_like(carry)

    ext[0:2] = jnp.where(chunk == 0, 0.0, ul_ref[...])
    ext[2:t_len + 2] = u_ref[...]
    ext[t_len + 2:t_len + 3] = jnp.where(chunk == n_chunks - 1, 0.0, ur_ref[...])
    uc = cb_ref[...].reshape(1, 1, c)
    for j in range(4):
        uc = uc + cw_ref[j:j + 1, :].reshape(1, 1, c) * ext[j:j + t_len]

    u2 = uc.reshape(t_len * bn, c)
    ub = u2.astype(BF16)
    half = c // 2

    def gate(w_ref, b_ref):
        z = jnp.concatenate([_dot(ub[:, :half], w_ref[0]), _dot(ub[:, half:], w_ref[1])], axis=-1)
        return 0.5 * jnp.tanh(0.5 * (z + b_ref[...])) + 0.5

    r = gate(wa_ref, ba_ref)
    ig = gate(wx_ref, bx_ref)
    a = jnp.exp((-LRU_C) * r * sp_ref[...])
    b = jnp.sqrt(1.0 - a * a) * (ig * u2)
    a_s[...] = a.reshape(t_len, bn, c)
    b_s[...] = b.reshape(t_len, bn, c)

    def body(k, h):
        t = (t_len - 1 - k) if reverse else k
        h = a_s[t] * h + b_s[t]
        h_s[t] = h
        return h

    carry[...] = lax.fori_loop(0, t_len, body, carry[...], unroll=8)

    if reverse:
        out_ref[...] = ((hf_ref[...] + h_s[...]) * _gelu_tanh(gate_ref[...])).astype(out_ref.dtype)


def _rglru(ug, hf, conv_w, conv_b, wa, ba, wx, bx, lam, bn, *, reverse):
    s_len = ug.shape[0]
    c = LRU_W
    ug3 = ug
    t_len = min(LRU_T, s_len)
    n_chunks = s_len // t_len

    def ck(i):
        return (n_chunks - 1 - i) if reverse else i

    half = c // 2
    nb = LRU_BLOCKS // 2

    def dense_halves(w):
        w = w.astype(F32).reshape(2, nb, c // LRU_BLOCKS, c // LRU_BLOCKS)
        eye = jnp.eye(nb, dtype=F32)
        return jnp.einsum('gnij,nm->gnimj', w, eye).reshape(2, half, half).astype(BF16)

    softplus_neg_lam = jax.nn.softplus(-lam.astype(F32)).reshape(1, c)
    vec = pl.BlockSpec((1, c), lambda i: (0, 0))
    wspec = pl.BlockSpec((2, half, half), lambda i: (0, 0, 0))
    blk = pl.BlockSpec((t_len, bn, c), lambda i: (ck(i), 0, 0))
    in_specs = [
        pl.BlockSpec((2, bn, c), lambda i: (jnp.maximum(ck(i) * (t_len // 2) - 1, 0), 0, 0)),
        blk,
        pl.BlockSpec((1, bn, c), lambda i: (jnp.minimum((ck(i) + 1) * t_len, s_len - 1), 0, 0)),
    ]
    args = [ug3, ug3, ug3]
    scratch = [pltpu.VMEM((t_len + 3, bn, c), F32), pltpu.VMEM((t_len, bn, c), F32),
               pltpu.VMEM((t_len, bn, c), F32)]
    if reverse:
        in_specs += [pl.BlockSpec((t_len, bn, c), lambda i: (ck(i), 0, 1)), blk]
        args += [ug3, hf]
        scratch.append(pltpu.VMEM((t_len, bn, c), F32))
    scratch.append(pltpu.VMEM((bn, c), F32))
    in_specs += [pl.BlockSpec((4, c), lambda i: (0, 0)), vec, wspec, vec, wspec, vec, vec]
    args += [conv_w.astype(F32), conv_b.astype(F32).reshape(1, c), dense_halves(wa),
             ba.astype(F32).reshape(1, c), dense_halves(wx), bx.astype(F32).reshape(1, c), softplus_neg_lam]
    return pl.pallas_call(
        functools.partial(_rglru_kernel, reverse=reverse, n_chunks=n_chunks),
        out_shape=jax.ShapeDtypeStruct((s_len, bn, c), BF16 if reverse else F32),
        grid=(n_chunks,),
        in_specs=in_specs,
        out_specs=blk,
        scratch_shapes=scratch,
        compiler_params=_cparams(("arbitrary",)),
        name="rglru_bw" if reverse else "rglru_fw",
    )(*args)


def _hg_matrices(c, reverse):
    n_levels = int(math.log2(c))
    t = np.arange(c)[:, None]
    r = np.arange(c)[None, :]
    mats, masks, upper = [], [], []
    if not reverse:
        mats.append(r <= t)
        mats.append(r > t)
    else:
        mats.append(r >= t)
        mats.append(r < t)
    for lev in range(n_levels):
        half = c >> (lev + 1)
        parent = 2 * half
        start = (np.arange(c) // parent) * parent
        mid = (start + half)[:, None]
        later = (np.arange(c) % parent >= half)[:, None]
        if not reverse:
            m = np.where(later, (r >= mid) & (r <= t), (r > t) & (r < mid))
            is_q = later
        else:
            m = np.where(later, (r >= mid) & (r < t), (r >= t) & (r < mid))
            is_q = ~later
        mats.append(m)
        same_parent = (start[:, None] == start[None, :])
        masks.append(same_parent & is_q & (~is_q).T)
        upper.append(np.broadcast_to(is_q, (c, HG_DK)))
    mat = np.concatenate(mats, axis=0).astype(np.float32)
    return (np.concatenate([mat, mat], axis=1), np.stack(masks).astype(np.float32),
            np.stack(upper).astype(np.float32))


def _hgrn_kernel(*refs, reverse, n_steps, n_levels):
    if reverse:
        (q_ref, f_ref, v_ref, go_ref, of_ref, lb_ref, gn_ref, mat_ref, mask_ref, isq_ref, out_ref,
         state, x_scr, oi_scr, aux_scr, dec_scr, kk_scr) = refs
    else:
        (q_ref, f_ref, v_ref, lb_ref, mat_ref, mask_ref, isq_ref, out_ref,
         state, x_scr, oi_scr, aux_scr, dec_scr, kk_scr) = refs
    step = pl.program_id(1)
    c = HG_CHUNK
    t_len = q_ref.shape[0]
    n_chunks = t_len // c

    @pl.when(step == 0)
    def _():
        state[...] = jnp.zeros_like(state)

    def chunk_body(ci, carry):
        ch = (n_chunks - 1 - ci) if reverse else ci
        rows = slice(ch * c, (ch + 1) * c)
        edge = (c - 1) if not reverse else 0
        for hp in range(HG_HEADS // 2):
            cols2 = slice(2 * hp * HG_DK, 2 * (hp + 1) * HG_DK)
            fl = f_ref[rows, cols2]
            e = jnp.exp(-jnp.abs(fl))
            r = 1.0 / (1.0 + e)
            er = e * r
            pos = fl > 0.0
            lb = lb_ref[0:1, cols2]
            oml = lb_ref[1:2, cols2]
            g2 = jnp.log2(lb + oml * jnp.where(pos, r, er))
            kk_scr[hp] = oml * jnp.where(pos, er, r)
            g_hi = g2.astype(BF16)
            g_lo = (g2 - g_hi.astype(F32)).astype(BF16)
            dec_scr[hp] = jnp.exp2(_dot(mat_ref[...], jnp.concatenate([g_hi, g_lo], axis=0)))

        for hd in range(HG_HEADS):
            cols = slice(hd * HG_DK, (hd + 1) * HG_DK)
            pcols = slice((hd % 2) * HG_DK, (hd % 2 + 1) * HG_DK)
            kk = kk_scr[hd // 2, :, pcols]
            dec = dec_scr.at[hd // 2]
            qv = q_ref[rows, cols].astype(F32)
            qq = qv * jax.nn.sigmoid(qv)
            qb, kb = qq.astype(BF16), kk.astype(BF16)
            d_in = dec[0:c, pcols]
            x_scr[hd, n_levels] = (qq * d_in).astype(BF16)
            x_scr[hd, n_levels + 1] = (kk * dec[c:2 * c, pcols]).astype(BF16)
            aux_scr[hd, 0] = jnp.broadcast_to(d_in[edge:edge + 1, :], (c, HG_DK))
            aux_scr[hd, 1] = jnp.broadcast_to(jnp.sum(qq * kk, axis=-1, keepdims=True), (c, HG_DK))
            for lev in range(n_levels):
                half = c >> (lev + 1)
                if half >= SUBLANES_BF16:
                    first_is_q = bool(reverse)
                    base = jnp.concatenate(
                        [(qb if (blk % 2 == 1) != first_is_q else kb)[blk * half:(blk + 1) * half]
                         for blk in range(c // half)], axis=0)
                else:
                    base = jnp.where(isq_ref[lev] > 0.5, qb, kb)
                x_scr[hd, lev] = base * dec[(2 + lev) * c:(3 + lev) * c, pcols].astype(BF16)

        for hd in range(HG_HEADS):
            cols = slice(hd * HG_DK, (hd + 1) * HG_DK)
            att = None
            for lev in range(n_levels):
                xl = x_scr[hd, lev]
                term = mask_ref[lev] * _nt_dot(xl, xl)
                att = term if att is None else att + term
            st = state[hd]
            oi_scr[hd] = _nt_dot(x_scr[hd, n_levels], st.astype(BF16))
            vt = v_ref[rows, cols].astype(F32).T.astype(BF16)
            state[hd] = st * aux_scr[hd, 0] + _dot(vt, x_scr[hd, n_levels + 1])
            x_scr[hd, 0] = att.astype(BF16)

        for hd in range(HG_HEADS):
            cols = slice(hd * HG_DK, (hd + 1) * HG_DK)
            vb = v_ref[rows, cols]
            o = oi_scr[hd] + _dot(x_scr[hd, 0], vb) + aux_scr[hd, 1] * vb.astype(F32)

            if reverse:
                o = o + of_ref[rows, cols]
                o = o * lax.rsqrt(jnp.mean(o * o, axis=-1, keepdims=True) + EPS) * gn_ref[...]
                gv = go_ref[rows, cols].astype(F32)
                out_ref[rows, cols] = (o * (gv * jax.nn.sigmoid(gv))).astype(out_ref.dtype)
            else:
                out_ref[rows, cols] = o
        return carry

    for ci in range(n_chunks):
        chunk_body(ci, 0)


def _hgrn(hg, o_fw, lb, gnorm_g, *, reverse):
    bn, s_len, _ = hg["q"].shape
    w = HG_HEADS * HG_DK
    t_len = min(HG_T, s_len)
    n_steps = s_len // t_len
    mats, masks, isq = _hg_matrices(HG_CHUNK, reverse)
    n_levels = masks.shape[0]
    lbf = lb.astype(F32)
    lb_tab = jnp.stack([lbf, 1.0 - lbf])

    blk = pl.BlockSpec((None, t_len, w), lambda b, i: (b, (n_steps - 1 - i) if reverse else i, 0))

    def const(shape):
        return pl.BlockSpec(shape, lambda b, i: (0,) * len(shape))

    in_specs = [blk, blk, blk]
    args = [hg["q"], hg["f_bw" if reverse else "f_fw"], hg["i_in"]]
    if reverse:
        in_specs += [blk, blk]
        args += [hg["g_out"], o_fw]
    in_specs.append(const((2, w)))
    args.append(lb_tab)
    if reverse:
        in_specs.append(const((1, HG_DK)))
        args.append(gnorm_g.astype(F32).reshape(1, HG_DK))
    in_specs += [const(mats.shape), const(masks.shape), const(isq.shape)]
    args += [jnp.asarray(mats, BF16), jnp.asarray(masks, F32), jnp.asarray(isq, BF16)]
    return pl.pallas_call(
        functools.partial(_hgrn_kernel, reverse=reverse, n_steps=n_steps, n_levels=n_levels),
        out_shape=jax.ShapeDtypeStruct((bn, s_len, w), BF16 if reverse else F32),
        grid=(bn, n_steps),
        in_specs=in_specs,
        out_specs=blk,
        scratch_shapes=[pltpu.VMEM((HG_HEADS, HG_DK, HG_DK), F32),
                        pltpu.VMEM((HG_HEADS, n_levels + 2, HG_CHUNK, HG_DK), BF16),
                        pltpu.VMEM((HG_HEADS, HG_CHUNK, HG_DK), F32),
                        pltpu.VMEM((HG_HEADS, 2, HG_CHUNK, HG_DK), F32),
                        pltpu.VMEM((HG_HEADS // 2, (n_levels + 2) * HG_CHUNK, 2 * HG_DK), F32),
                        pltpu.VMEM((HG_HEADS // 2, HG_CHUNK, 2 * HG_DK), F32)],
        compiler_params=_cparams(("parallel", "arbitrary")),
        name="hgrn_bw" if reverse else "hgrn_fw",
    )(*args)


def _even_layer(x, g, w_in, w_out, rpb, tail):
    qkv, view4, view16 = _ev_proj(x, g, w_in.astype(BF16))
    outs, maxes, dens = zip(*[_dilated_branch(view, dil) for dil, view in zip(DILATIONS, (qkv, view4, view16))])
    yb = _neighborhood(qkv, rpb)
    return _ev_tail(x, outs, maxes, dens, yb, w_out.astype(BF16), tail)


def _odd_layer(x, g, w_in, w_out, conv_w, conv_b, wa, ba, wx, bx, lam, lb_f, lb_b, gnorm_g, tail):
    bn, s_len, _ = x.shape
    ug, hg = _od_proj(x, g, w_in.astype(BF16))
    h_fw = _rglru(ug, None, conv_w, conv_b, wa[0], ba[0], wx[0], bx[0], lam[0], bn, reverse=False)
    yc = _rglru(ug, h_fw, conv_w, conv_b, wa[1], ba[1], wx[1], bx[1], lam[1], bn, reverse=True)
    o_fw = _hgrn(hg, None, lb_f, gnorm_g, reverse=False)
    yd = _hgrn(hg, o_fw, lb_b, gnorm_g, reverse=True)
    return _od_tail(x, yc.reshape(s_len, bn * LRU_W), yd, w_out.astype(BF16), tail)


def kernel(x, mem, norm_mix_g, norm_xa_g, norm_mem_g, norm_mlp_g, final_norm_g, ev_w_in, ev_w_out, na_rpb,
           od_w_in, od_w_out, conv_w, conv_b, lru_wa, lru_ba, lru_wx, lru_bx, lru_lambda, hgrn_lb_logits,
           hgrn_norm_g, xa_wq, xa_wkv, xa_wo, mlp_w1, mlp_w2):
    depth = norm_mix_g.shape[0]
    p_lb = jax.nn.softmax(hgrn_lb_logits.astype(F32), axis=0)
    lower_bounds = jnp.cumsum(p_lb, axis=0) - p_lb[0:1]
    for layer in range(depth):
        tail = dict(g_xa=norm_xa_g[layer], wq=xa_wq[layer].astype(BF16),
                    kv=_mem_kv(mem, norm_mem_g[layer], xa_wkv[layer].astype(BF16)),
                    wo=xa_wo[layer].astype(BF16), g_mlp=norm_mlp_g[layer],
                    w1=mlp_w1[layer].astype(BF16), w2=mlp_w2[layer].astype(BF16),
                    g_final=final_norm_g if layer == depth - 1 else None)
        if layer % 2 == 0:
            e = layer // 2
            x = _even_layer(x, norm_mix_g[layer], ev_w_in[e], ev_w_out[e], na_rpb[e], tail)
        else:
            o = layer // 2
            x = _odd_layer(x, norm_mix_g[layer], od_w_in[o], od_w_out[o], conv_w[o], conv_b[o],
                           lru_wa[o], lru_ba[o], lru_wx[o], lru_bx[o], lru_lambda[o],
                           lower_bounds[layer, 0], lower_bounds[layer, 1], hgrn_norm_g[o], tail)
    return x
```

```python
import functools
import math

import jax
import jax.numpy as jnp
import numpy as np
from jax import lax
from jax.experimental import pallas as pl
from jax.experimental.pallas import tpu as pltpu

F32 = jnp.float32
BF16 = jnp.bfloat16

D_MODEL = 1024
HEAD_DIM = 64
ROT_DIM = 16
ROPE_THETA = 500000.0
N_HEADS_A = 8
N_HEADS_B = 8
GROUP_W = N_HEADS_A * HEAD_DIM
DILATIONS = (1, 4, 16)
WIN_HALF = 64
GRID_W = 64
NA_ROWS = 8
NA_COLS = 16
LRU_W = 512
LRU_BLOCKS = 8
LRU_C = 8.0
HG_HEADS = 4
HG_DK = 128
XA_HEADS = 4
XA_DH = 256
D_FF = 4096
EPS = 1e-6
LOG2E = math.log2(math.e)

LANES = 128
SUBLANES_BF16 = 16
V7X_VMEM_LIMIT_BYTES = 56 * 1024 * 1024

ROW_TILE = 512
ROW_TILE_WIDE = 1024
ATT_Q_BLOCK = 512
ATT_SUB = 128
NA_ROW_GROUP = 8
LRU_T = 128
HG_CHUNK = 128
HG_T = 1024


def _cparams(sem):
    return pltpu.CompilerParams(dimension_semantics=sem, vmem_limit_bytes=V7X_VMEM_LIMIT_BYTES)


def _resident(shape, index_map):
    return pl.BlockSpec(shape, index_map, pipeline_mode=pl.Buffered(1))


def _rms(x, g):
    return x * lax.rsqrt(jnp.mean(x * x, axis=-1, keepdims=True) + EPS) * g


def _nt_dot(a, b):
    return lax.dot_general(a, b, (((1,), (1,)), ((), ())), preferred_element_type=F32)


def _dot(a, b):
    return jnp.dot(a, b, preferred_element_type=F32)


def _head_selectors():
    lane = lax.broadcasted_iota(jnp.int32, (1, LANES), 1)
    low = (lane < HEAD_DIM).astype(F32)
    return low.astype(BF16), (1.0 - low).astype(BF16)


def _ev_proj_kernel(x_ref, g_ref, w_ref, rc_ref, rs1_ref, rs2_ref, o_ref, o4_ref, o16_ref, slab):
    h = _rms(x_ref[...], g_ref[...]).astype(BF16)
    scale = HEAD_DIM ** -0.5 * LOG2E
    tm = x_ref.shape[0]
    n_slabs = GROUP_W // LANES
    for c in range(6):
        acc = _dot(h, w_ref[:, c * GROUP_W:(c + 1) * GROUP_W])
        if c in (0, 1):
            rc, rs1, rs2 = rc_ref[...], rs1_ref[...], rs2_ref[...]
            parts = []
            for j in range(n_slabs):
                t = acc[:, j * LANES:(j + 1) * LANES]
                t = t * rc + pltpu.roll(t, LANES - ROT_DIM // 2, 1) * rs1 + pltpu.roll(t, ROT_DIM // 2, 1) * rs2
                parts.append(t)
            acc = jnp.concatenate(parts, axis=-1)
        if c in (0, 3):
            acc = acc * scale
        o_ref[:, c * GROUP_W:(c + 1) * GROUP_W] = acc.astype(BF16)
        if c < 3:
            for j in range(n_slabs):
                slab[j] = acc[:, j * LANES:(j + 1) * LANES]
            for dil, dst in ((DILATIONS[1], o4_ref), (DILATIONS[2], o16_ref)):
                for r in range(dil):
                    for j in range(n_slabs):
                        col = r * 3 * GROUP_W + c * GROUP_W + j * LANES
                        dst[:, col:col + LANES] = slab[j, pl.ds(r, tm // dil, stride=dil), :].astype(BF16)


def _rope_tables(s_len):
    half = ROT_DIM // 2
    inv = jnp.asarray(ROPE_THETA ** (-np.arange(half) * 2.0 / ROT_DIM), F32)
    ang = jnp.arange(s_len, dtype=F32)[:, None] * inv[None, :]
    cos, sin = jnp.cos(ang), jnp.sin(ang)
    ones = jnp.ones((s_len, HEAD_DIM - ROT_DIM), F32)
    zeros = jnp.zeros((s_len, HEAD_DIM - ROT_DIM), F32)
    zh = jnp.zeros((s_len, half), F32)
    rc = jnp.concatenate([cos, cos, ones], axis=-1)
    rs1 = jnp.concatenate([-sin, zh, zeros], axis=-1)
    rs2 = jnp.concatenate([zh, sin, zeros], axis=-1)
    rep = LANES // HEAD_DIM
    return jnp.tile(rc, (1, rep)), jnp.tile(rs1, (1, rep)), jnp.tile(rs2, (1, rep))


def _ev_proj(x, g, w_bf16):
    bn, s_len, d = x.shape
    tm = min(ROW_TILE_WIDE, s_len)
    n_out = w_bf16.shape[1]
    rc, rs1, rs2 = _rope_tables(s_len)
    tbl_spec = pl.BlockSpec((tm, LANES), lambda b, i: (i, 0))
    wa = 3 * GROUP_W
    d4, d16 = DILATIONS[1], DILATIONS[2]
    return pl.pallas_call(
        _ev_proj_kernel,
        out_shape=(jax.ShapeDtypeStruct((bn, s_len, n_out), BF16),
                   jax.ShapeDtypeStruct((bn, s_len // d4, d4 * wa), BF16),
                   jax.ShapeDtypeStruct((bn, s_len // d16, d16 * wa), BF16)),
        grid=(bn, s_len // tm),
        in_specs=[
            pl.BlockSpec((None, tm, d), lambda b, i: (b, i, 0)),
            pl.BlockSpec((1, d), lambda b, i: (0, 0)),
            _resident((d, n_out), lambda b, i: (0, 0)),
            tbl_spec, tbl_spec, tbl_spec,
        ],
        out_specs=(pl.BlockSpec((None, tm, n_out), lambda b, i: (b, i, 0)),
                   pl.BlockSpec((None, tm // d4, d4 * wa), lambda b, i: (b, i, 0)),
                   pl.BlockSpec((None, tm // d16, d16 * wa), lambda b, i: (b, i, 0))),
        scratch_shapes=[pltpu.VMEM((GROUP_W // LANES, tm, LANES), F32)],
        compiler_params=_cparams(("parallel", "parallel")),
        name="ev_proj",
    )(x, g.reshape(1, d), w_bf16, rc, rs1, rs2)


def _dilated_kernel(q_ref, kl_ref, kc_ref, kr_ref, vl_ref, vc_ref, vr_ref, o_ref, max_ref, den_ref,
                    kext, vext, s_scr, p_scr, *, lq, l_total):
    i = pl.program_id(2)
    kext[0:WIN_HALF, :] = kl_ref[...]
    kext[WIN_HALF:WIN_HALF + lq, :] = kc_ref[...]
    kext[WIN_HALF + lq:, :] = kr_ref[...]
    vext[0:WIN_HALF, :] = vl_ref[...]
    vext[WIN_HALF:WIN_HALF + lq, :] = vc_ref[...]
    vext[WIN_HALF + lq:, :] = vr_ref[...]

    wk = ATT_SUB + 2 * WIN_HALF
    qi = lax.broadcasted_iota(jnp.int32, (ATT_SUB, wk), 0)
    ci = lax.broadcasted_iota(jnp.int32, (ATT_SUB, wk), 1)
    band_bias = jnp.where((ci - qi >= 0) & (ci - qi <= 2 * WIN_HALF), 0.0, -jnp.inf).astype(F32)
    crow = lax.broadcasted_iota(jnp.int32, (1, wk), 1)
    lane = lax.broadcasted_iota(jnp.int32, (ATT_SUB, LANES), 1)
    low_half = lane < HEAD_DIM
    sel_lo, sel_hi = _head_selectors()
    n_sub = lq // ATT_SUB
    n_hp = GROUP_W // LANES

    for j in range(n_sub):
        base = i * lq + j * ATT_SUB - WIN_HALF
        in_seq = (crow + base >= 0) & (crow + base < l_total)
        bias = band_bias + jnp.where(in_seq, 0.0, -jnp.inf).astype(F32)
        bias2 = jnp.concatenate([bias, bias], axis=0)
        rows = slice(j * ATT_SUB, (j + 1) * ATT_SUB)
        wrows = slice(j * ATT_SUB, j * ATT_SUB + wk)
        for hp in range(n_hp):
            cols = slice(hp * LANES, (hp + 1) * LANES)
            qp = q_ref[rows, cols]
            q2 = jnp.concatenate([qp * sel_lo, qp * sel_hi], axis=0)
            s_scr[j * n_hp + hp] = _nt_dot(q2, kext[wrows, cols]) + bias2
    for j in range(n_sub):
        m_tile = jnp.zeros((ATT_SUB, LANES), F32)
        den_tile = jnp.ones((ATT_SUB, LANES), F32)
        for hp in range(n_hp):
            t = j * n_hp + hp
            s = s_scr[t]
            m = jnp.max(s, axis=-1, keepdims=True)
            p = jnp.exp2(s - m)
            den = jnp.sum(p, axis=-1, keepdims=True)
            p_scr[t] = p.astype(BF16)
            is_lo, is_hi = lane == 2 * hp, lane == 2 * hp + 1
            m_tile = jnp.where(is_lo, m[0:ATT_SUB], jnp.where(is_hi, m[ATT_SUB:], m_tile))
            den_tile = jnp.where(is_lo, den[0:ATT_SUB], jnp.where(is_hi, den[ATT_SUB:], den_tile))
        max_ref[j * ATT_SUB:(j + 1) * ATT_SUB, :] = m_tile
        den_ref[j * ATT_SUB:(j + 1) * ATT_SUB, :] = den_tile
    for j in range(n_sub):
        rows = slice(j * ATT_SUB, (j + 1) * ATT_SUB)
        wrows = slice(j * ATT_SUB, j * ATT_SUB + wk)
        for hp in range(n_hp):
            cols = slice(hp * LANES, (hp + 1) * LANES)
            o = _dot(p_scr[j * n_hp + hp], vext[wrows, cols])
            o_ref[rows, cols] = jnp.where(low_half, o[0:ATT_SUB], o[ATT_SUB:]).astype(BF16)


def _dilated_branch(view, dil):
    bn, l_total, width = view.shape
    n_groups = width // (dil * GROUP_W)
    lq = min(ATT_Q_BLOCK, l_total)
    nblk = l_total // lq
    per = lq // WIN_HALF
    n_halo = l_total // WIN_HALF
    n_units = (lq // ATT_SUB) * (GROUP_W // LANES)

    def cur(g):
        return pl.BlockSpec((None, lq, GROUP_W), lambda b, r, i: (b, i, r * n_groups + g))

    def left(g):
        return pl.BlockSpec((None, WIN_HALF, GROUP_W),
                            lambda b, r, i: (b, jnp.maximum(i * per - 1, 0), r * n_groups + g))

    def right(g):
        return pl.BlockSpec((None, WIN_HALF, GROUP_W),
                            lambda b, r, i: (b, jnp.minimum((i + 1) * per, n_halo - 1), r * n_groups + g))

    stat_shape = jax.ShapeDtypeStruct((bn, l_total, dil * LANES), F32)
    stat_spec = pl.BlockSpec((None, lq, LANES), lambda b, r, i: (b, i, r))
    return pl.pallas_call(
        functools.partial(_dilated_kernel, lq=lq, l_total=l_total),
        out_shape=(jax.ShapeDtypeStruct((bn, l_total, dil * GROUP_W), BF16), stat_shape, stat_shape),
        grid=(bn, dil, nblk),
        in_specs=[cur(0), left(1), cur(1), right(1), left(2), cur(2), right(2)],
        out_specs=(pl.BlockSpec((None, lq, GROUP_W), lambda b, r, i: (b, i, r)), stat_spec, stat_spec),
        scratch_shapes=[pltpu.VMEM((lq + 2 * WIN_HALF, GROUP_W), BF16),
                        pltpu.VMEM((lq + 2 * WIN_HALF, GROUP_W), BF16),
                        pltpu.VMEM((n_units, 2 * ATT_SUB, ATT_SUB + 2 * WIN_HALF), F32),
                        pltpu.VMEM((n_units, 2 * ATT_SUB, ATT_SUB + 2 * WIN_HALF), BF16)],
        compiler_params=_cparams(("parallel", "parallel", "parallel")),
        name=f"dilated_attn_d{dil}",
    )(view, view, view, view, view, view, view)


NA_QT = NA_ROW_GROUP * GRID_W
NA_KT = 2 * NA_QT
NA_KW = NA_ROWS * GRID_W


def _na_tables(rpb):
    rpb = rpb.astype(F32)
    nh = rpb.shape[0]
    ext = GRID_W - NA_COLS
    padded = jnp.concatenate([jnp.repeat(rpb[..., :1], ext, axis=-1), rpb,
                              jnp.repeat(rpb[..., -1:], ext, axis=-1)], axis=-1)
    skew = jnp.tile(padded, (1, 1, GRID_W + 1))[..., :2 * GRID_W * GRID_W]
    t1 = skew.reshape(nh, -1, GRID_W, 2 * GRID_W)[..., ::-1, :GRID_W]
    qc = np.arange(GRID_W)[:, None]
    kc = np.arange(GRID_W)[None, :]
    c0 = np.clip(qc - NA_COLS // 2, 0, GRID_W - NA_COLS)
    t1 = jnp.where((kc >= c0) & (kc < c0 + NA_COLS), t1 * LOG2E, -jnp.inf)
    per_delta = [jnp.transpose(t1[:, NA_ROWS - 1 - delta:2 * NA_ROWS - 1 - delta], (0, 2, 1, 3))
                 .reshape(nh, GRID_W, NA_KW) for delta in range(NA_ROWS)]
    return jnp.stack(per_delta).reshape(NA_ROWS, nh // 2, 2 * GRID_W, NA_KW)


def _na_kernel(q_ref, k_ref, v_ref, tb_ref, o_ref, s_scr, p_scr, inv_scr, *, rows):
    i0 = pl.program_id(1) * NA_ROW_GROUP
    w0 = jnp.clip(i0 - NA_ROWS // 2, 0, rows - 2 * NA_ROW_GROUP)
    lane = lax.broadcasted_iota(jnp.int32, (GRID_W, LANES), 1)
    low_half = lane < HEAD_DIM
    sel_lo, sel_hi = _head_selectors()
    n_hp = GROUP_W // LANES

    def key_rows(a):
        r0 = jnp.clip(i0 + a - NA_ROWS // 2, 0, rows - NA_ROWS)
        return pl.ds(pl.multiple_of((r0 - w0) * GRID_W, GRID_W), NA_KW), i0 + a - r0

    for a in range(NA_ROW_GROUP):
        krows, delta = key_rows(a)
        qrows = slice(a * GRID_W, (a + 1) * GRID_W)
        for hp in range(n_hp):
            cols = slice(hp * LANES, (hp + 1) * LANES)
            qp = q_ref[qrows, cols]
            q2 = jnp.concatenate([qp * sel_lo, qp * sel_hi], axis=0)
            s_scr[a * n_hp + hp] = _nt_dot(q2, k_ref[0, krows, cols]) + tb_ref[delta, hp]
    for t in range(NA_ROW_GROUP * n_hp):
        s = s_scr[t]
        p = jnp.exp2(s - jnp.max(s, axis=-1, keepdims=True))
        inv_scr[t] = jnp.broadcast_to(1.0 / jnp.sum(p, axis=-1, keepdims=True), (2 * GRID_W, LANES))
        p_scr[t] = p.astype(BF16)
    for a in range(NA_ROW_GROUP):
        krows, _ = key_rows(a)
        qrows = slice(a * GRID_W, (a + 1) * GRID_W)
        for hp in range(n_hp):
            cols = slice(hp * LANES, (hp + 1) * LANES)
            t = a * n_hp + hp
            o = _dot(p_scr[t], v_ref[0, krows, cols]) * inv_scr[t]
            o_ref[qrows, cols] = jnp.where(low_half, o[0:GRID_W], o[GRID_W:]).astype(BF16)


def _neighborhood(qkv, rpb):
    bn, s_len, _ = qkv.shape
    rows = s_len // GRID_W
    n_groups = rows // NA_ROW_GROUP
    tables = _na_tables(rpb)

    def window(g):
        return pl.BlockSpec(
            (pl.Element(1), pl.Element(NA_KT), pl.Element(GROUP_W)),
            lambda b, i: (b, jnp.clip(i * NA_ROW_GROUP - NA_ROWS // 2, 0, rows - 2 * NA_ROW_GROUP) * GRID_W,
                          g * GROUP_W))

    return pl.pallas_call(
        functools.partial(_na_kernel, rows=rows),
        out_shape=jax.ShapeDtypeStruct((bn, s_len, GROUP_W), BF16),
        grid=(bn, n_groups),
        in_specs=[pl.BlockSpec((None, NA_QT, GROUP_W), lambda b, i: (b, i, 3)), window(4), window(5),
                  pl.BlockSpec(tables.shape, lambda b, i: (0, 0, 0, 0))],
        out_specs=pl.BlockSpec((None, NA_QT, GROUP_W), lambda b, i: (b, i, 0)),
        scratch_shapes=[pltpu.VMEM((NA_ROW_GROUP * GROUP_W // LANES, 2 * GRID_W, NA_KW), F32),
                        pltpu.VMEM((NA_ROW_GROUP * GROUP_W // LANES, 2 * GRID_W, NA_KW), BF16),
                        pltpu.VMEM((NA_ROW_GROUP * GROUP_W // LANES, 2 * GRID_W, LANES), F32)],
        compiler_params=_cparams(("parallel", "parallel")),
        name="neighborhood_attn",
    )(qkv, qkv, qkv, tables)


def _xattn_block(x, g_ref, wq_ref, kv_ref, wo_ref):
    q = (_dot(_rms(x, g_ref[...]).astype(BF16), wq_ref[...]) * (XA_DH ** -0.5 * LOG2E)).astype(BF16)
    heads = []
    for hd in range(XA_HEADS):
        cols = slice(hd * XA_DH, (hd + 1) * XA_DH)
        s = _nt_dot(q[:, cols], kv_ref[:, cols])
        m = jnp.max(s, axis=-1, keepdims=True)
        p = jnp.exp2(s - m)
        den = jnp.sum(p, axis=-1, keepdims=True)
        vcols = slice(D_MODEL + hd * XA_DH, D_MODEL + (hd + 1) * XA_DH)
        heads.append((_dot(p.astype(BF16), kv_ref[:, vcols]) * (1.0 / den)).astype(BF16))
    return x + _dot(jnp.concatenate(heads, axis=-1), wo_ref[...])


def _mlp_block(x, g_ref, w1_ref, w2_ref, gf_ref, final_norm):
    h = _rms(x, g_ref[...]).astype(BF16)
    acc = x
    for c in range(D_FF // D_MODEL):
        cols = slice(c * D_MODEL, (c + 1) * D_MODEL)
        a = jnp.maximum(_dot(h, w1_ref[:, cols]), 0.0)
        acc = acc + _dot((a * a).astype(BF16), w2_ref[cols, :])
    return _rms(acc, gf_ref[...]) if final_norm else acc


def _xattn_operands(tail, d):
    m_len = tail["kv"].shape[1]
    specs = [pl.BlockSpec((1, d), lambda b, i: (0, 0)), _resident((d, d), lambda b, i: (0, 0)),
             pl.BlockSpec((None, m_len, 2 * d), lambda b, i: (b, 0, 0)),
             _resident((d, d), lambda b, i: (0, 0))]
    return specs, [tail["g_xa"].reshape(1, d), tail["wq"], tail["kv"], tail["wo"]]


def _mlp_kernel(x_ref, g_ref, w1_ref, w2_ref, gf_ref, out_ref, *, final_norm):
    out_ref[...] = _mlp_block(x_ref[...], g_ref, w1_ref, w2_ref, gf_ref, final_norm)


def _mlp(x, tail):
    bn, s_len, d = x.shape
    tm = min(ROW_TILE_WIDE, s_len)
    final_norm = tail["g_final"] is not None
    gf = tail["g_final"] if final_norm else tail["g_mlp"]
    x_spec = pl.BlockSpec((None, tm, d), lambda b, i: (b, i, 0))
    vec = pl.BlockSpec((1, d), lambda b, i: (0, 0))
    return pl.pallas_call(
        functools.partial(_mlp_kernel, final_norm=final_norm),
        out_shape=jax.ShapeDtypeStruct(x.shape, F32),
        grid=(bn, s_len // tm),
        in_specs=[x_spec, vec, _resident((d, D_FF), lambda b, i: (0, 0)),
                  _resident((D_FF, d), lambda b, i: (0, 0)), vec],
        out_specs=x_spec,
        compiler_params=_cparams(("parallel", "parallel")),
        name="mlp_final" if final_norm else "mlp",
    )(x, tail["g_mlp"].reshape(1, d), tail["w1"], tail["w2"], gf.reshape(1, d))


def _ev_tail_kernel(o1, o2v, o3v, m1, m2v, m3v, d1, d2v, d3v, yb_ref, ex_ref, w_ref, x_ref,
                    gxa_ref, wq_ref, kv_ref, wo_ref, out_ref, so2, so3, st2, st3):
    tm = x_ref.shape[0]
    n_slabs = GROUP_W // LANES
    for dil, ov, mv, dv, so, st in ((DILATIONS[1], o2v, m2v, d2v, so2, st2), (DILATIONS[2], o3v, m3v, d3v, so3, st3)):
        for r in range(dil):
            dst = pl.ds(r, tm // dil, stride=dil)
            for j in range(n_slabs):
                col = r * GROUP_W + j * LANES
                so[j, dst, :] = ov[:, col:col + LANES].astype(F32)
            st[0, dst, :] = mv[:, r * LANES:(r + 1) * LANES]
            st[1, dst, :] = dv[:, r * LANES:(r + 1) * LANES]
    o2 = jnp.concatenate([so2[j] for j in range(n_slabs)], axis=-1)
    o3 = jnp.concatenate([so3[j] for j in range(n_slabs)], axis=-1)
    ma, mb, mc = m1[...], st2[0], st3[0]
    m = jnp.maximum(jnp.maximum(ma, mb), mc)
    ea, eb, ec = jnp.exp2(ma - m), jnp.exp2(mb - m), jnp.exp2(mc - m)
    inv = 1.0 / (ea * d1[...] + eb * st2[1] + ec * st3[1])
    lane = lax.broadcasted_iota(jnp.int32, (tm, LANES), 1)
    wcat = jnp.where(lane < N_HEADS_A, ea * inv,
                     jnp.where(lane < 2 * N_HEADS_A, pltpu.roll(eb * inv, N_HEADS_A, 1),
                               pltpu.roll(ec * inv, 2 * N_HEADS_A, 1)))
    hi = wcat.astype(BF16)
    lo = (wcat - hi.astype(F32)).astype(BF16)
    wide = _dot(jnp.concatenate([hi, lo], axis=1), ex_ref[...])
    ya = (wide[:, 0:GROUP_W] * o1[...].astype(F32) + wide[:, GROUP_W:2 * GROUP_W] * o2
          + wide[:, 2 * GROUP_W:] * o3)
    y = _dot(ya.astype(BF16), w_ref[0:GROUP_W, :]) + _dot(yb_ref[...], w_ref[GROUP_W:, :])
    out_ref[...] = _xattn_block(x_ref[...] + y, gxa_ref, wq_ref, kv_ref, wo_ref)


def _ev_tail(x, outs, maxes, dens, yb, w_bf16, tail):
    bn, s_len, d = x.shape
    tm = min(ROW_TILE, s_len)
    n_br = len(DILATIONS)
    ex = np.zeros((LANES, n_br * GROUP_W), np.float32)
    for i in range(n_br):
        for h in range(N_HEADS_A):
            ex[i * N_HEADS_A + h, i * GROUP_W + h * HEAD_DIM:i * GROUP_W + (h + 1) * HEAD_DIM] = 1.0
    ex = np.concatenate([ex, ex], axis=0)

    def o_spec(dil):
        return pl.BlockSpec((None, tm // dil, dil * GROUP_W), lambda b, i: (b, i, 0))

    def l_spec(dil):
        return pl.BlockSpec((None, tm // dil, dil * LANES), lambda b, i: (b, i, 0))

    x_spec = pl.BlockSpec((None, tm, d), lambda b, i: (b, i, 0))
    n_slabs = GROUP_W // LANES
    tail_specs, tail_args = _xattn_operands(tail, d)
    return pl.pallas_call(
        _ev_tail_kernel,
        out_shape=jax.ShapeDtypeStruct(x.shape, F32),
        grid=(bn, s_len // tm),
        in_specs=[o_spec(d_) for d_ in DILATIONS] + 2 * [l_spec(d_) for d_ in DILATIONS]
        + [o_spec(1),
           _resident(ex.shape, lambda b, i: (0, 0)),
           _resident((2 * GROUP_W, d), lambda b, i: (0, 0)),
           x_spec] + tail_specs,
        out_specs=x_spec,
        scratch_shapes=[pltpu.VMEM((n_slabs, tm, LANES), F32), pltpu.VMEM((n_slabs, tm, LANES), F32),
                        pltpu.VMEM((2, tm, LANES), F32), pltpu.VMEM((2, tm, LANES), F32)],
        compiler_params=_cparams(("parallel", "parallel")),
        name="ev_tail",
    )(*outs, *maxes, *dens, yb, jnp.asarray(ex, BF16), w_bf16, x, *tail_args)


def _od_tail_kernel(yc_ref, yd_ref, w_ref, x_ref, gxa_ref, wq_ref, kv_ref, wo_ref, out_ref):
    half = yc_ref.shape[-1]
    x0 = x_ref[...] + _dot(yc_ref[...], w_ref[0:half, :]) + _dot(yd_ref[...], w_ref[half:, :])
    out_ref[...] = _xattn_block(x0, gxa_ref, wq_ref, kv_ref, wo_ref)


def _od_tail(x, yc_tm, yd, w_bf16, tail):
    bn, s_len, d = x.shape
    tm = min(ROW_TILE_WIDE, s_len)
    c = yd.shape[-1]
    x_spec = pl.BlockSpec((None, tm, d), lambda b, i: (b, i, 0))
    tail_specs, tail_args = _xattn_operands(tail, d)
    return pl.pallas_call(
        _od_tail_kernel,
        out_shape=jax.ShapeDtypeStruct(x.shape, F32),
        grid=(bn, s_len // tm),
        in_specs=[pl.BlockSpec((tm, c), lambda b, i: (i, b)),
                  pl.BlockSpec((None, tm, c), lambda b, i: (b, i, 0)),
                  _resident((2 * c, d), lambda b, i: (0, 0)),
                  x_spec] + tail_specs,
        out_specs=x_spec,
        compiler_params=_cparams(("parallel", "parallel")),
        name="od_tail",
    )(yc_tm, yd, w_bf16, x, *tail_args)


def _norm_matmul_kernel(x_ref, g_ref, w_ref, o_ref):
    o_ref[...] = _dot(_rms(x_ref[...], g_ref[...]).astype(BF16), w_ref[...]).astype(o_ref.dtype)


def _mem_kv(mem, g, w_bf16):
    bn, m_len, d = mem.shape
    n_out = w_bf16.shape[1]
    return pl.pallas_call(
        _norm_matmul_kernel,
        out_shape=jax.ShapeDtypeStruct((bn, m_len, n_out), BF16),
        grid=(bn,),
        in_specs=[pl.BlockSpec((None, m_len, d), lambda b: (b, 0, 0)),
                  pl.BlockSpec((1, d), lambda b: (0, 0)),
                  pl.BlockSpec((d, n_out), lambda b: (0, 0))],
        out_specs=pl.BlockSpec((None, m_len, n_out), lambda b: (b, 0, 0)),
        compiler_params=_cparams(("parallel",)),
        name="mem_kv",
    )(mem, g.reshape(1, d), w_bf16)


def _od_proj_kernel(x_ref, g_ref, w_ref, ug_ref, *rest):
    hg_refs, slab = rest[:-1], rest[-1]
    bn, tq, d = x_ref.shape
    h = _rms(x_ref[...].reshape(bn * tq, d), g_ref[...]).astype(BF16)
    n_ug = 2 * LRU_W
    n_slabs = GROUP_W // LANES
    for c in range(n_ug // GROUP_W):
        acc = _dot(h, w_ref[:, c * GROUP_W:(c + 1) * GROUP_W])
        for b in range(bn):
            for j in range(n_slabs):
                slab[j, pl.ds(b, tq, stride=bn), :] = acc[b * tq:(b + 1) * tq, j * LANES:(j + 1) * LANES]
        for j in range(n_slabs):
            ug_ref[:, c * GROUP_W + j * LANES:c * GROUP_W + (j + 1) * LANES] = slab[j]
    for c, hg_ref in enumerate(hg_refs):
        acc = _dot(h, w_ref[:, n_ug + c * GROUP_W:n_ug + (c + 1) * GROUP_W])
        hg_ref[...] = acc.reshape(bn, tq, GROUP_W).astype(hg_ref.dtype)


HG_INPUT_DTYPES = (("q", BF16), ("f_fw", F32), ("f_bw", F32), ("i_in", BF16), ("g_out", BF16))


def _od_proj(x, g, w_bf16):
    bn, s_len, d = x.shape
    tq = min(ROW_TILE // bn, s_len)
    n_ug = 2 * LRU_W
    n_hg = w_bf16.shape[1] - n_ug
    assert n_hg == len(HG_INPUT_DTYPES) * GROUP_W
    hg_spec = pl.BlockSpec((bn, tq, GROUP_W), lambda i: (0, i, 0))
    ug, *hg = pl.pallas_call(
        _od_proj_kernel,
        out_shape=[jax.ShapeDtypeStruct((s_len * bn, n_ug), F32)]
        + [jax.ShapeDtypeStruct((bn, s_len, GROUP_W), dt) for _, dt in HG_INPUT_DTYPES],
        grid=(s_len // tq,),
        in_specs=[pl.BlockSpec((bn, tq, d), lambda i: (0, i, 0)),
                  pl.BlockSpec((1, d), lambda i: (0, 0)),
                  pl.BlockSpec((d, n_ug + n_hg), lambda i: (0, 0))],
        out_specs=[pl.BlockSpec((tq * bn, n_ug), lambda i: (i, 0))] + [hg_spec] * len(HG_INPUT_DTYPES),
        scratch_shapes=[pltpu.VMEM((GROUP_W // LANES, tq * bn, LANES), F32)],
        compiler_params=_cparams(("parallel",)),
        name="od_proj",
    )(x, g.reshape(1, d), w_bf16)
    return ug.reshape(s_len, bn, n_ug), dict(zip([n for n, _ in HG_INPUT_DTYPES], hg))


def _gelu_tanh(x):
    return 0.5 * x * (1.0 + jnp.tanh(math.sqrt(2.0 / math.pi) * (x + 0.044715 * (x * x * x))))


def _rglru_kernel(*refs, reverse, n_chunks):
    if reverse:
        (ul_ref, u_ref, ur_ref, gate_ref, hf_ref, cw_ref, cb_ref, wa_ref, ba_ref, wx_ref, bx_ref, sp_ref,
         out_ref, ext, a_s, b_s, h_s, carry) = refs
    else:
        (ul_ref, u_ref, ur_ref, cw_ref, cb_ref, wa_ref, ba_ref, wx_ref, bx_ref, sp_ref,
         out_ref, ext, a_s, b_s, carry) = refs
        h_s = out_ref
    step = pl.program_id(0)
    chunk = (n_chunks - 1 - step) if reverse else step
    t_len, bn, c = u_ref.shape

    @pl.when(step == 0)
    def _():
        carry[...] = jnp.zeros_like(carry)

    ext[0:2] = jnp.where(chunk == 0, 0.0, ul_ref[...])
    ext[2:t_len + 2] = u_ref[...]
    ext[t_len + 2:t_len + 3] = jnp.where(chunk == n_chunks - 1, 0.0, ur_ref[...])
    uc = cb_ref[...].reshape(1, 1, c)
    for j in range(4):
        uc = uc + cw_ref[j:j + 1, :].reshape(1, 1, c) * ext[j:j + t_len]

    u2 = uc.reshape(t_len * bn, c)
    ub = u2.astype(BF16)
    half = c // 2

    def gate(w_ref, b_ref):
        z = jnp.concatenate([_dot(ub[:, :half], w_ref[0]), _dot(ub[:, half:], w_ref[1])], axis=-1)
        return 0.5 * jnp.tanh(0.5 * (z + b_ref[...])) + 0.5

    r = gate(wa_ref, ba_ref)
    ig = gate(wx_ref, bx_ref)
    a = jnp.exp((-LRU_C) * r * sp_ref[...])
    b = jnp.sqrt(1.0 - a * a) * (ig * u2)
    a_s[...] = a.reshape(t_len, bn, c)
    b_s[...] = b.reshape(t_len, bn, c)

    def body(k, h):
        t = (t_len - 1 - k) if reverse else k
        h = a_s[t] * h + b_s[t]
        h_s[t] = h
        return h

    carry[...] = lax.fori_loop(0, t_len, body, carry[...], unroll=8)

    if reverse:
        out_ref[...] = ((hf_ref[...] + h_s[...]) * _gelu_tanh(gate_ref[...])).astype(out_ref.dtype)


def _rglru(ug, hf, conv_w, conv_b, wa, ba, wx, bx, lam, bn, *, reverse):
    s_len = ug.shape[0]
    c = LRU_W
    ug3 = ug
    t_len = min(LRU_T, s_len)
    n_chunks = s_len // t_len

    def ck(i):
        return (n_chunks - 1 - i) if reverse else i

    half = c // 2
    nb = LRU_BLOCKS // 2

    def dense_halves(w):
        w = w.astype(F32).reshape(2, nb, c // LRU_BLOCKS, c // LRU_BLOCKS)
        eye = jnp.eye(nb, dtype=F32)
        return jnp.einsum('gnij,nm->gnimj', w, eye).reshape(2, half, half).astype(BF16)

    softplus_neg_lam = jax.nn.softplus(-lam.astype(F32)).reshape(1, c)
    vec = pl.BlockSpec((1, c), lambda i: (0, 0))
    wspec = pl.BlockSpec((2, half, half), lambda i: (0, 0, 0))
    blk = pl.BlockSpec((t_len, bn, c), lambda i: (ck(i), 0, 0))
    in_specs = [
        pl.BlockSpec((2, bn, c), lambda i: (jnp.maximum(ck(i) * (t_len // 2) - 1, 0), 0, 0)),
        blk,
        pl.BlockSpec((1, bn, c), lambda i: (jnp.minimum((ck(i) + 1) * t_len, s_len - 1), 0, 0)),
    ]
    args = [ug3, ug3, ug3]
    scratch = [pltpu.VMEM((t_len + 3, bn, c), F32), pltpu.VMEM((t_len, bn, c), F32),
               pltpu.VMEM((t_len, bn, c), F32)]
    if reverse:
        in_specs += [pl.BlockSpec((t_len, bn, c), lambda i: (ck(i), 0, 1)), blk]
        args += [ug3, hf]
        scratch.append(pltpu.VMEM((t_len, bn, c), F32))
    scratch.append(pltpu.VMEM((bn, c), F32))
    in_specs += [pl.BlockSpec((4, c), lambda i: (0, 0)), vec, wspec, vec, wspec, vec, vec]
    args += [conv_w.astype(F32), conv_b.astype(F32).reshape(1, c), dense_halves(wa),
             ba.astype(F32).reshape(1, c), dense_halves(wx), bx.astype(F32).reshape(1, c), softplus_neg_lam]
    return pl.pallas_call(
        functools.partial(_rglru_kernel, reverse=reverse, n_chunks=n_chunks),
        out_shape=jax.ShapeDtypeStruct((s_len, bn, c), BF16 if reverse else F32),
        grid=(n_chunks,),
        in_specs=in_specs,
        out_specs=blk,
        scratch_shapes=scratch,
        compiler_params=_cparams(("arbitrary",)),
        name="rglru_bw" if reverse else "rglru_fw",
    )(*args)


def _hg_matrices(c, reverse):
    n_levels = int(math.log2(c))
    t = np.arange(c)[:, None]
    r = np.arange(c)[None, :]
    mats, masks, upper = [], [], []
    if not reverse:
        mats.append(r <= t)
        mats.append(r > t)
    else:
        mats.append(r >= t)
        mats.append(r < t)
    for lev in range(n_levels):
        half = c >> (lev + 1)
        parent = 2 * half
        start = (np.arange(c) // parent) * parent
        mid = (start + half)[:, None]
        later = (np.arange(c) % parent >= half)[:, None]
        if not reverse:
            m = np.where(later, (r >= mid) & (r <= t), (r > t) & (r < mid))
            is_q = later
        else:
            m = np.where(later, (r >= mid) & (r < t), (r >= t) & (r < mid))
            is_q = ~later
        mats.append(m)
        same_parent = (start[:, None] == start[None, :])
        masks.append(same_parent & is_q & (~is_q).T)
        upper.append(np.broadcast_to(is_q, (c, HG_DK)))
    mat = np.concatenate(mats, axis=0).astype(np.float32)
    return (np.concatenate([mat, mat], axis=1), np.stack(masks).astype(np.float32),
            np.stack(upper).astype(np.float32))


def _hgrn_kernel(*refs, reverse, n_steps, n_levels):
    if reverse:
        (q_ref, f_ref, v_ref, go_ref, of_ref, lb_ref, gn_ref, mat_ref, mask_ref, isq_ref, out_ref,
         state, x_scr, oi_scr, aux_scr, dec_scr, kk_scr) = refs
    else:
        (q_ref, f_ref, v_ref, lb_ref, mat_ref, mask_ref, isq_ref, out_ref,
         state, x_scr, oi_scr, aux_scr, dec_scr, kk_scr) = refs
    step = pl.program_id(1)
    c = HG_CHUNK
    t_len = q_ref.shape[0]
    n_chunks = t_len // c

    @pl.when(step == 0)
    def _():
        state[...] = jnp.zeros_like(state)

    def chunk_body(ci, carry):
        ch = (n_chunks - 1 - ci) if reverse else ci
        rows = slice(ch * c, (ch + 1) * c)
        edge = (c - 1) if not reverse else 0
        for hp in range(HG_HEADS // 2):
            cols2 = slice(2 * hp * HG_DK, 2 * (hp + 1) * HG_DK)
            fl = f_ref[rows, cols2]
            e = jnp.exp(-jnp.abs(fl))
            r = 1.0 / (1.0 + e)
            er = e * r
            pos = fl > 0.0
            lb = lb_ref[0:1, cols2]
            oml = lb_ref[1:2, cols2]
            g2 = jnp.log2(lb + oml * jnp.where(pos, r, er))
            kk_scr[hp] = oml * jnp.where(pos, er, r)
            g_hi = g2.astype(BF16)
            g_lo = (g2 - g_hi.astype(F32)).astype(BF16)
            dec_scr[hp] = jnp.exp2(_dot(mat_ref[...], jnp.concatenate([g_hi, g_lo], axis=0)))

        for hd in range(HG_HEADS):
            cols = slice(hd * HG_DK, (hd + 1) * HG_DK)
            pcols = slice((hd % 2) * HG_DK, (hd % 2 + 1) * HG_DK)
            kk = kk_scr[hd // 2, :, pcols]
            dec = dec_scr.at[hd // 2]
            qv = q_ref[rows, cols].astype(F32)
            qq = qv * jax.nn.sigmoid(qv)
            qb, kb = qq.astype(BF16), kk.astype(BF16)
            d_in = dec[0:c, pcols]
            x_scr[hd, n_levels] = (qq * d_in).astype(BF16)
            x_scr[hd, n_levels + 1] = (kk * dec[c:2 * c, pcols]).astype(BF16)
            aux_scr[hd, 0] = jnp.broadcast_to(d_in[edge:edge + 1, :], (c, HG_DK))
            aux_scr[hd, 1] = jnp.broadcast_to(jnp.sum(qq * kk, axis=-1, keepdims=True), (c, HG_DK))
            for lev in range(n_levels):
                half = c >> (lev + 1)
                if half >= SUBLANES_BF16:
                    first_is_q = bool(reverse)
                    base = jnp.concatenate(
                        [(qb if (blk % 2 == 1) != first_is_q else kb)[blk * half:(blk + 1) * half]
                         for blk in range(c // half)], axis=0)
                else:
                    base = jnp.where(isq_ref[lev] > 0.5, qb, kb)
                x_scr[hd, lev] = base * dec[(2 + lev) * c:(3 + lev) * c, pcols].astype(BF16)

        for hd in range(HG_HEADS):
            cols = slice(hd * HG_DK, (hd + 1) * HG_DK)
            att = None
            for lev in range(n_levels):
                xl = x_scr[hd, lev]
                term = mask_ref[lev] * _nt_dot(xl, xl)
                att = term if att is None else att + term
            st = state[hd]
            oi_scr[hd] = _nt_dot(x_scr[hd, n_levels], st.astype(BF16))
            vt = v_ref[rows, cols].astype(F32).T.astype(BF16)
            state[hd] = st * aux_scr[hd, 0] + _dot(vt, x_scr[hd, n_levels + 1])
            x_scr[hd, 0] = att.astype(BF16)

        for hd in range(HG_HEADS):
            cols = slice(hd * HG_DK, (hd + 1) * HG_DK)
            vb = v_ref[rows, cols]
            o = oi_scr[hd] + _dot(x_scr[hd, 0], vb) + aux_scr[hd, 1] * vb.astype(F32)

            if reverse:
                o = o + of_ref[rows, cols]
                o = o * lax.rsqrt(jnp.mean(o * o, axis=-1, keepdims=True) + EPS) * gn_ref[...]
                gv = go_ref[rows, cols].astype(F32)
                out_ref[rows, cols] = (o * (gv * jax.nn.sigmoid(gv))).astype(out_ref.dtype)
            else:
                out_ref[rows, cols] = o
        return carry

    for ci in range(n_chunks):
        chunk_body(ci, 0)


def _hgrn(hg, o_fw, lb, gnorm_g, *, reverse):
    bn, s_len, _ = hg["q"].shape
    w = HG_HEADS * HG_DK
    t_len = min(HG_T, s_len)
    n_steps = s_len // t_len
    mats, masks, isq = _hg_matrices(HG_CHUNK, reverse)
    n_levels = masks.shape[0]
    lbf = lb.astype(F32)
    lb_tab = jnp.stack([lbf, 1.0 - lbf])

    blk = pl.BlockSpec((None, t_len, w), lambda b, i: (b, (n_steps - 1 - i) if reverse else i, 0))

    def const(shape):
        return pl.BlockSpec(shape, lambda b, i: (0,) * len(shape))

    in_specs = [blk, blk, blk]
    args = [hg["q"], hg["f_bw" if reverse else "f_fw"], hg["i_in"]]
    if reverse:
        in_specs += [blk, blk]
        args += [hg["g_out"], o_fw]
    in_specs.append(const((2, w)))
    args.append(lb_tab)
    if reverse:
        in_specs.append(const((1, HG_DK)))
        args.append(gnorm_g.astype(F32).reshape(1, HG_DK))
    in_specs += [const(mats.shape), const(masks.shape), const(isq.shape)]
    args += [jnp.asarray(mats, BF16), jnp.asarray(masks, F32), jnp.asarray(isq, BF16)]
    return pl.pallas_call(
        functools.partial(_hgrn_kernel, reverse=reverse, n_steps=n_steps, n_levels=n_levels),
        out_shape=jax.ShapeDtypeStruct((bn, s_len, w), BF16 if reverse else F32),
        grid=(bn, n_steps),
        in_specs=in_specs,
        out_specs=blk,
        scratch_shapes=[pltpu.VMEM((HG_HEADS, HG_DK, HG_DK), F32),
                        pltpu.VMEM((HG_HEADS, n_levels + 2, HG_CHUNK, HG_DK), BF16),
                        pltpu.VMEM((HG_HEADS, HG_CHUNK, HG_DK), F32),
                        pltpu.VMEM((HG_HEADS, 2, HG_CHUNK, HG_DK), F32),
                        pltpu.VMEM((HG_HEADS // 2, (n_levels + 2) * HG_CHUNK, 2 * HG_DK), F32),
                        pltpu.VMEM((HG_HEADS // 2, HG_CHUNK, 2 * HG_DK), F32)],
        compiler_params=_cparams(("parallel", "arbitrary")),
        name="hgrn_bw" if reverse else "hgrn_fw",
    )(*args)


def _even_layer(x, g, w_in, w_out, rpb, tail):
    qkv, view4, view16 = _ev_proj(x, g, w_in.astype(BF16))
    outs, maxes, dens = zip(*[_dilated_branch(view, dil) for dil, view in zip(DILATIONS, (qkv, view4, view16))])
    yb = _neighborhood(qkv, rpb)
    return _ev_tail(x, outs, maxes, dens, yb, w_out.astype(BF16), tail)


def _odd_layer(x, g, w_in, w_out, conv_w, conv_b, wa, ba, wx, bx, lam, lb_f, lb_b, gnorm_g, tail):
    bn, s_len, _ = x.shape
    ug, hg = _od_proj(x, g, w_in.astype(BF16))
    h_fw = _rglru(ug, None, conv_w, conv_b, wa[0], ba[0], wx[0], bx[0], lam[0], bn, reverse=False)
    yc = _rglru(ug, h_fw, conv_w, conv_b, wa[1], ba[1], wx[1], bx[1], lam[1], bn, reverse=True)
    o_fw = _hgrn(hg, None, lb_f, gnorm_g, reverse=False)
    yd = _hgrn(hg, o_fw, lb_b, gnorm_g, reverse=True)
    return _od_tail(x, yc.reshape(s_len, bn * LRU_W), yd, w_out.astype(BF16), tail)


def kernel(x, mem, norm_mix_g, norm_xa_g, norm_mem_g, norm_mlp_g, final_norm_g, ev_w_in, ev_w_out, na_rpb,
           od_w_in, od_w_out, conv_w, conv_b, lru_wa, lru_ba, lru_wx, lru_bx, lru_lambda, hgrn_lb_logits,
           hgrn_norm_g, xa_wq, xa_wkv, xa_wo, mlp_w1, mlp_w2):
    depth = norm_mix_g.shape[0]
    p_lb = jax.nn.softmax(hgrn_lb_logits.astype(F32), axis=0)
    lower_bounds = jnp.cumsum(p_lb, axis=0) - p_lb[0:1]
    for layer in range(depth):
        tail = dict(g_xa=norm_xa_g[layer], wq=xa_wq[layer].astype(BF16),
                    kv=_mem_kv(mem, norm_mem_g[layer], xa_wkv[layer].astype(BF16)),
                    wo=xa_wo[layer].astype(BF16), g_mlp=norm_mlp_g[layer],
                    w1=mlp_w1[layer].astype(BF16), w2=mlp_w2[layer].astype(BF16),
                    g_final=final_norm_g if layer == depth - 1 else None)
        if layer % 2 == 0:
            e = layer // 2
            x = _even_layer(x, norm_mix_g[layer], ev_w_in[e], ev_w_out[e], na_rpb[e], tail)
        else:
            o = layer // 2
            x = _odd_layer(x, norm_mix_g[layer], od_w_in[o], od_w_out[o], conv_w[o], conv_b[o],
                           lru_wa[o], lru_ba[o], lru_wx[o], lru_bx[o], lru_lambda[o],
                           lower_bounds[layer, 0], lower_bounds[layer, 1], hgrn_norm_g[o], tail)
        x = _mlp(x, tail)
    return x
```

```python
import functools
import math

import jax
import jax.numpy as jnp
import numpy as np
from jax import lax
from jax.experimental import pallas as pl
from jax.experimental.pallas import tpu as pltpu

F32 = jnp.float32
BF16 = jnp.bfloat16

D_MODEL = 1024
HEAD_DIM = 64
ROT_DIM = 16
ROPE_THETA = 500000.0
N_HEADS_A = 8
N_HEADS_B = 8
GROUP_W = N_HEADS_A * HEAD_DIM
DILATIONS = (1, 4, 16)
WIN_HALF = 64
GRID_W = 64
NA_ROWS = 8
NA_COLS = 16
LRU_W = 512
LRU_BLOCKS = 8
LRU_C = 8.0
HG_HEADS = 4
HG_DK = 128
XA_HEADS = 4
XA_DH = 256
D_FF = 4096
EPS = 1e-6
LOG2E = math.log2(math.e)

LANES = 128
SUBLANES_BF16 = 16
V7X_VMEM_LIMIT_BYTES = 56 * 1024 * 1024

ROW_TILE = 512
ROW_TILE_WIDE = 1024
ATT_Q_BLOCK = 512
ATT_SUB = 128
NA_ROW_GROUP = 8
LRU_T = 128
HG_CHUNK = 128
HG_T = 1024


def _cparams(sem):
    return pltpu.CompilerParams(dimension_semantics=sem, vmem_limit_bytes=V7X_VMEM_LIMIT_BYTES)


def _resident(shape, index_map):
    return pl.BlockSpec(shape, index_map, pipeline_mode=pl.Buffered(1))


def _rms(x, g):
    return x * lax.rsqrt(jnp.mean(x * x, axis=-1, keepdims=True) + EPS) * g


def _nt_dot(a, b):
    return lax.dot_general(a, b, (((1,), (1,)), ((), ())), preferred_element_type=F32)


def _dot(a, b):
    return jnp.dot(a, b, preferred_element_type=F32)


def _head_selectors():
    lane = lax.broadcasted_iota(jnp.int32, (1, LANES), 1)
    low = (lane < HEAD_DIM).astype(F32)
    return low.astype(BF16), (1.0 - low).astype(BF16)


def _ev_proj_kernel(x_ref, g_ref, w_ref, rc_ref, rs1_ref, rs2_ref, o_ref, o4_ref, o16_ref, slab):
    h = _rms(x_ref[...], g_ref[...]).astype(BF16)
    scale = HEAD_DIM ** -0.5 * LOG2E
    tm = x_ref.shape[0]
    n_slabs = GROUP_W // LANES
    for c in range(6):
        acc = _dot(h, w_ref[:, c * GROUP_W:(c + 1) * GROUP_W])
        if c in (0, 1):
            rc, rs1, rs2 = rc_ref[...], rs1_ref[...], rs2_ref[...]
            parts = []
            for j in range(n_slabs):
                t = acc[:, j * LANES:(j + 1) * LANES]
                t = t * rc + pltpu.roll(t, LANES - ROT_DIM // 2, 1) * rs1 + pltpu.roll(t, ROT_DIM // 2, 1) * rs2
                parts.append(t)
            acc = jnp.concatenate(parts, axis=-1)
        if c in (0, 3):
            acc = acc * scale
        o_ref[:, c * GROUP_W:(c + 1) * GROUP_W] = acc.astype(BF16)
        if c < 3:
            for j in range(n_slabs):
                slab[j] = acc[:, j * LANES:(j + 1) * LANES]
            for dil, dst in ((DILATIONS[1], o4_ref), (DILATIONS[2], o16_ref)):
                for r in range(dil):
                    for j in range(n_slabs):
                        col = r * 3 * GROUP_W + c * GROUP_W + j * LANES
                        dst[:, col:col + LANES] = slab[j, pl.ds(r, tm // dil, stride=dil), :].astype(BF16)


def _rope_tables(s_len):
    half = ROT_DIM // 2
    inv = jnp.asarray(ROPE_THETA ** (-np.arange(half) * 2.0 / ROT_DIM), F32)
    ang = jnp.arange(s_len, dtype=F32)[:, None] * inv[None, :]
    cos, sin = jnp.cos(ang), jnp.sin(ang)
    ones = jnp.ones((s_len, HEAD_DIM - ROT_DIM), F32)
    zeros = jnp.zeros((s_len, HEAD_DIM - ROT_DIM), F32)
    zh = jnp.zeros((s_len, half), F32)
    rc = jnp.concatenate([cos, cos, ones], axis=-1)
    rs1 = jnp.concatenate([-sin, zh, zeros], axis=-1)
    rs2 = jnp.concatenate([zh, sin, zeros], axis=-1)
    rep = LANES // HEAD_DIM
    return jnp.tile(rc, (1, rep)), jnp.tile(rs1, (1, rep)), jnp.tile(rs2, (1, rep))


def _ev_proj(x, g, w_bf16):
    bn, s_len, d = x.shape
    tm = min(ROW_TILE_WIDE, s_len)
    n_out = w_bf16.shape[1]
    rc, rs1, rs2 = _rope_tables(s_len)
    tbl_spec = pl.BlockSpec((tm, LANES), lambda b, i: (i, 0))
    wa = 3 * GROUP_W
    d4, d16 = DILATIONS[1], DILATIONS[2]
    return pl.pallas_call(
        _ev_proj_kernel,
        out_shape=(jax.ShapeDtypeStruct((bn, s_len, n_out), BF16),
                   jax.ShapeDtypeStruct((bn, s_len // d4, d4 * wa), BF16),
                   jax.ShapeDtypeStruct((bn, s_len // d16, d16 * wa), BF16)),
        grid=(bn, s_len // tm),
        in_specs=[
            pl.BlockSpec((None, tm, d), lambda b, i: (b, i, 0)),
            pl.BlockSpec((1, d), lambda b, i: (0, 0)),
            _resident((d, n_out), lambda b, i: (0, 0)),
            tbl_spec, tbl_spec, tbl_spec,
        ],
        out_specs=(pl.BlockSpec((None, tm, n_out), lambda b, i: (b, i, 0)),
                   pl.BlockSpec((None, tm // d4, d4 * wa), lambda b, i: (b, i, 0)),
                   pl.BlockSpec((None, tm // d16, d16 * wa), lambda b, i: (b, i, 0))),
        scratch_shapes=[pltpu.VMEM((GROUP_W // LANES, tm, LANES), F32)],
        compiler_params=_cparams(("parallel", "parallel")),
        name="ev_proj",
    )(x, g.reshape(1, d), w_bf16, rc, rs1, rs2)


def _dilated_kernel(q_ref, kl_ref, kc_ref, kr_ref, vl_ref, vc_ref, vr_ref, o_ref, stat_ref,
                    kext, vext, s_scr, p_scr, *, lq, l_total):
    i = pl.program_id(2)
    kext[0:WIN_HALF, :] = kl_ref[...]
    kext[WIN_HALF:WIN_HALF + lq, :] = kc_ref[...]
    kext[WIN_HALF + lq:, :] = kr_ref[...]
    vext[0:WIN_HALF, :] = vl_ref[...]
    vext[WIN_HALF:WIN_HALF + lq, :] = vc_ref[...]
    vext[WIN_HALF + lq:, :] = vr_ref[...]

    wk = ATT_SUB + 2 * WIN_HALF
    qi = lax.broadcasted_iota(jnp.int32, (ATT_SUB, wk), 0)
    ci = lax.broadcasted_iota(jnp.int32, (ATT_SUB, wk), 1)
    band_bias = jnp.where((ci - qi >= 0) & (ci - qi <= 2 * WIN_HALF), 0.0, -jnp.inf).astype(F32)
    crow = lax.broadcasted_iota(jnp.int32, (1, wk), 1)
    lane = lax.broadcasted_iota(jnp.int32, (ATT_SUB, LANES), 1)
    low_half = lane < HEAD_DIM
    sel_lo, sel_hi = _head_selectors()
    n_sub = lq // ATT_SUB
    n_hp = GROUP_W // LANES

    for j in range(n_sub):
        base = i * lq + j * ATT_SUB - WIN_HALF
        in_seq = (crow + base >= 0) & (crow + base < l_total)
        bias = band_bias + jnp.where(in_seq, 0.0, -jnp.inf).astype(F32)
        bias2 = jnp.concatenate([bias, bias], axis=0)
        rows = slice(j * ATT_SUB, (j + 1) * ATT_SUB)
        wrows = slice(j * ATT_SUB, j * ATT_SUB + wk)
        for hp in range(n_hp):
            cols = slice(hp * LANES, (hp + 1) * LANES)
            qp = q_ref[rows, cols]
            q2 = jnp.concatenate([qp * sel_lo, qp * sel_hi], axis=0)
            s_scr[j * n_hp + hp] = _nt_dot(q2, kext[wrows, cols]) + bias2
    for j in range(n_sub):
        stat = jnp.ones((ATT_SUB, LANES), F32)
        for hp in range(n_hp):
            t = j * n_hp + hp
            s = s_scr[t]
            m = jnp.max(s, axis=-1, keepdims=True)
            p = jnp.exp2(s - m)
            den = jnp.sum(p, axis=-1, keepdims=True)
            p_scr[t] = p.astype(BF16)
            stat = jnp.where(lane == 2 * hp, m[0:ATT_SUB], jnp.where(lane == 2 * hp + 1, m[ATT_SUB:], stat))
            stat = jnp.where(lane == N_HEADS_A + 2 * hp, den[0:ATT_SUB],
                             jnp.where(lane == N_HEADS_A + 2 * hp + 1, den[ATT_SUB:], stat))
        stat_ref[j * ATT_SUB:(j + 1) * ATT_SUB, :] = stat
    for j in range(n_sub):
        rows = slice(j * ATT_SUB, (j + 1) * ATT_SUB)
        wrows = slice(j * ATT_SUB, j * ATT_SUB + wk)
        for hp in range(n_hp):
            cols = slice(hp * LANES, (hp + 1) * LANES)
            o = _dot(p_scr[j * n_hp + hp], vext[wrows, cols])
            o_ref[rows, cols] = jnp.where(low_half, o[0:ATT_SUB], o[ATT_SUB:]).astype(BF16)


def _dilated_branch(view, dil):
    bn, l_total, width = view.shape
    n_groups = width // (dil * GROUP_W)
    lq = min(ATT_Q_BLOCK, l_total)
    nblk = l_total // lq
    per = lq // WIN_HALF
    n_halo = l_total // WIN_HALF
    n_units = (lq // ATT_SUB) * (GROUP_W // LANES)

    def cur(g):
        return pl.BlockSpec((None, lq, GROUP_W), lambda b, r, i: (b, i, r * n_groups + g))

    def left(g):
        return pl.BlockSpec((None, WIN_HALF, GROUP_W),
                            lambda b, r, i: (b, jnp.maximum(i * per - 1, 0), r * n_groups + g))

    def right(g):
        return pl.BlockSpec((None, WIN_HALF, GROUP_W),
                            lambda b, r, i: (b, jnp.minimum((i + 1) * per, n_halo - 1), r * n_groups + g))

    stat_shape = jax.ShapeDtypeStruct((bn, l_total, dil * LANES), F32)
    stat_spec = pl.BlockSpec((None, lq, LANES), lambda b, r, i: (b, i, r))
    return pl.pallas_call(
        functools.partial(_dilated_kernel, lq=lq, l_total=l_total),
        out_shape=(jax.ShapeDtypeStruct((bn, l_total, dil * GROUP_W), BF16), stat_shape),
        grid=(bn, dil, nblk),
        in_specs=[cur(0), left(1), cur(1), right(1), left(2), cur(2), right(2)],
        out_specs=(pl.BlockSpec((None, lq, GROUP_W), lambda b, r, i: (b, i, r)), stat_spec),
        scratch_shapes=[pltpu.VMEM((lq + 2 * WIN_HALF, GROUP_W), BF16),
                        pltpu.VMEM((lq + 2 * WIN_HALF, GROUP_W), BF16),
                        pltpu.VMEM((n_units, 2 * ATT_SUB, ATT_SUB + 2 * WIN_HALF), F32),
                        pltpu.VMEM((n_units, 2 * ATT_SUB, ATT_SUB + 2 * WIN_HALF), BF16)],
        compiler_params=_cparams(("parallel", "parallel", "parallel")),
        name=f"dilated_attn_d{dil}",
    )(view, view, view, view, view, view, view)


NA_QT = NA_ROW_GROUP * GRID_W
NA_KT = 2 * NA_QT
NA_KW = NA_ROWS * GRID_W


def _na_tables(rpb):
    rpb = rpb.astype(F32)
    nh = rpb.shape[0]
    ext = GRID_W - NA_COLS
    padded = jnp.concatenate([jnp.repeat(rpb[..., :1], ext, axis=-1), rpb,
                              jnp.repeat(rpb[..., -1:], ext, axis=-1)], axis=-1)
    skew = jnp.tile(padded, (1, 1, GRID_W + 1))[..., :2 * GRID_W * GRID_W]
    t1 = skew.reshape(nh, -1, GRID_W, 2 * GRID_W)[..., ::-1, :GRID_W]
    qc = np.arange(GRID_W)[:, None]
    kc = np.arange(GRID_W)[None, :]
    c0 = np.clip(qc - NA_COLS // 2, 0, GRID_W - NA_COLS)
    t1 = jnp.where((kc >= c0) & (kc < c0 + NA_COLS), t1 * LOG2E, -jnp.inf)
    per_delta = [jnp.transpose(t1[:, NA_ROWS - 1 - delta:2 * NA_ROWS - 1 - delta], (0, 2, 1, 3))
                 .reshape(nh, GRID_W, NA_KW) for delta in range(NA_ROWS)]
    return jnp.stack(per_delta).reshape(NA_ROWS, nh // 2, 2 * GRID_W, NA_KW)


def _na_kernel(q_ref, k_ref, v_ref, tb_ref, o_ref, s_scr, p_scr, inv_scr, *, rows):
    i0 = pl.program_id(1) * NA_ROW_GROUP
    w0 = jnp.clip(i0 - NA_ROWS // 2, 0, rows - 2 * NA_ROW_GROUP)
    lane = lax.broadcasted_iota(jnp.int32, (GRID_W, LANES), 1)
    low_half = lane < HEAD_DIM
    sel_lo, sel_hi = _head_selectors()
    n_hp = GROUP_W // LANES

    def key_rows(a):
        r0 = jnp.clip(i0 + a - NA_ROWS // 2, 0, rows - NA_ROWS)
        return pl.ds(pl.multiple_of((r0 - w0) * GRID_W, GRID_W), NA_KW), i0 + a - r0

    for a in range(NA_ROW_GROUP):
        krows, delta = key_rows(a)
        qrows = slice(a * GRID_W, (a + 1) * GRID_W)
        for hp in range(n_hp):
            cols = slice(hp * LANES, (hp + 1) * LANES)
            qp = q_ref[qrows, cols]
            q2 = jnp.concatenate([qp * sel_lo, qp * sel_hi], axis=0)
            s_scr[a * n_hp + hp] = _nt_dot(q2, k_ref[0, krows, cols]) + tb_ref[delta, hp]
    for t in range(NA_ROW_GROUP * n_hp):
        s = s_scr[t]
        p = jnp.exp2(s - jnp.max(s, axis=-1, keepdims=True))
        inv_scr[t] = jnp.broadcast_to(1.0 / jnp.sum(p, axis=-1, keepdims=True), (2 * GRID_W, LANES))
        p_scr[t] = p.astype(BF16)
    for a in range(NA_ROW_GROUP):
        krows, _ = key_rows(a)
        qrows = slice(a * GRID_W, (a + 1) * GRID_W)
        for hp in range(n_hp):
            cols = slice(hp * LANES, (hp + 1) * LANES)
            t = a * n_hp + hp
            o = _dot(p_scr[t], v_ref[0, krows, cols]) * inv_scr[t]
            o_ref[qrows, cols] = jnp.where(low_half, o[0:GRID_W], o[GRID_W:]).astype(BF16)


def _neighborhood(qkv, rpb):
    bn, s_len, _ = qkv.shape
    rows = s_len // GRID_W
    n_groups = rows // NA_ROW_GROUP
    tables = _na_tables(rpb)

    def window(g):
        return pl.BlockSpec(
            (pl.Element(1), pl.Element(NA_KT), pl.Element(GROUP_W)),
            lambda b, i: (b, jnp.clip(i * NA_ROW_GROUP - NA_ROWS // 2, 0, rows - 2 * NA_ROW_GROUP) * GRID_W,
                          g * GROUP_W))

    return pl.pallas_call(
        functools.partial(_na_kernel, rows=rows),
        out_shape=jax.ShapeDtypeStruct((bn, s_len, GROUP_W), BF16),
        grid=(bn, n_groups),
        in_specs=[pl.BlockSpec((None, NA_QT, GROUP_W), lambda b, i: (b, i, 3)), window(4), window(5),
                  pl.BlockSpec(tables.shape, lambda b, i: (0, 0, 0, 0))],
        out_specs=pl.BlockSpec((None, NA_QT, GROUP_W), lambda b, i: (b, i, 0)),
        scratch_shapes=[pltpu.VMEM((NA_ROW_GROUP * GROUP_W // LANES, 2 * GRID_W, NA_KW), F32),
                        pltpu.VMEM((NA_ROW_GROUP * GROUP_W // LANES, 2 * GRID_W, NA_KW), BF16),
                        pltpu.VMEM((NA_ROW_GROUP * GROUP_W // LANES, 2 * GRID_W, LANES), F32)],
        compiler_params=_cparams(("parallel", "parallel")),
        name="neighborhood_attn",
    )(qkv, qkv, qkv, tables)


def _xattn_block(x, g_ref, wq_ref, kv_ref, wo_ref):
    q = (_dot(_rms(x, g_ref[...]).astype(BF16), wq_ref[...]) * (XA_DH ** -0.5 * LOG2E)).astype(BF16)
    heads = []
    for hd in range(XA_HEADS):
        cols = slice(hd * XA_DH, (hd + 1) * XA_DH)
        s = _nt_dot(q[:, cols], kv_ref[:, cols])
        m = jnp.max(s, axis=-1, keepdims=True)
        p = jnp.exp2(s - m)
        den = jnp.sum(p, axis=-1, keepdims=True)
        vcols = slice(D_MODEL + hd * XA_DH, D_MODEL + (hd + 1) * XA_DH)
        heads.append((_dot(p.astype(BF16), kv_ref[:, vcols]) * (1.0 / den)).astype(BF16))
    return x + _dot(jnp.concatenate(heads, axis=-1), wo_ref[...])


def _mlp_block(x, g_ref, w1_ref, w2_ref, gf_ref, final_norm):
    h = _rms(x, g_ref[...]).astype(BF16)
    acc = x
    for c in range(D_FF // D_MODEL):
        cols = slice(c * D_MODEL, (c + 1) * D_MODEL)
        a = jnp.maximum(_dot(h, w1_ref[:, cols]), 0.0)
        acc = acc + _dot((a * a).astype(BF16), w2_ref[cols, :])
    return _rms(acc, gf_ref[...]) if final_norm else acc


def _xattn_operands(tail, d):
    m_len = tail["kv"].shape[1]
    specs = [pl.BlockSpec((1, d), lambda b, i: (0, 0)), _resident((d, d), lambda b, i: (0, 0)),
             pl.BlockSpec((None, m_len, 2 * d), lambda b, i: (b, 0, 0)),
             _resident((d, d), lambda b, i: (0, 0))]
    return specs, [tail["g_xa"].reshape(1, d), tail["wq"], tail["kv"], tail["wo"]]


def _mlp_kernel(x_ref, g_ref, w1_ref, w2_ref, gf_ref, out_ref, *, final_norm):
    out_ref[...] = _mlp_block(x_ref[...], g_ref, w1_ref, w2_ref, gf_ref, final_norm)


def _mlp(x, tail):
    bn, s_len, d = x.shape
    tm = min(ROW_TILE_WIDE, s_len)
    final_norm = tail["g_final"] is not None
    gf = tail["g_final"] if final_norm else tail["g_mlp"]
    x_spec = pl.BlockSpec((None, tm, d), lambda b, i: (b, i, 0))
    vec = pl.BlockSpec((1, d), lambda b, i: (0, 0))
    return pl.pallas_call(
        functools.partial(_mlp_kernel, final_norm=final_norm),
        out_shape=jax.ShapeDtypeStruct(x.shape, F32),
        grid=(bn, s_len // tm),
        in_specs=[x_spec, vec, _resident((d, D_FF), lambda b, i: (0, 0)),
                  _resident((D_FF, d), lambda b, i: (0, 0)), vec],
        out_specs=x_spec,
        compiler_params=_cparams(("parallel", "parallel")),
        name="mlp_final" if final_norm else "mlp",
    )(x, tail["g_mlp"].reshape(1, d), tail["w1"], tail["w2"], gf.reshape(1, d))


def _ev_tail_kernel(o1, o2v, o3v, s1, s2v, s3v, yb_ref, ex_ref, w_ref, x_ref,
                    gxa_ref, wq_ref, kv_ref, wo_ref, out_ref, so2, so3, st2, st3):
    tm = x_ref.shape[0]
    n_slabs = GROUP_W // LANES
    for dil, ov, sv, so, st in ((DILATIONS[1], o2v, s2v, so2, st2), (DILATIONS[2], o3v, s3v, so3, st3)):
        for r in range(dil):
            dst = pl.ds(r, tm // dil, stride=dil)
            for j in range(n_slabs):
                col = r * GROUP_W + j * LANES
                so[j, dst, :] = ov[:, col:col + LANES].astype(F32)
            st[dst, :] = sv[:, r * LANES:(r + 1) * LANES]
    o2 = jnp.concatenate([so2[j] for j in range(n_slabs)], axis=-1)
    o3 = jnp.concatenate([so3[j] for j in range(n_slabs)], axis=-1)
    ma, mb, mc = s1[...], st2[...], st3[...]
    da, db, dc = [pltpu.roll(t, LANES - N_HEADS_A, 1) for t in (ma, mb, mc)]
    m = jnp.maximum(jnp.maximum(ma, mb), mc)
    ea, eb, ec = jnp.exp2(ma - m), jnp.exp2(mb - m), jnp.exp2(mc - m)
    inv = 1.0 / (ea * da + eb * db + ec * dc)
    lane = lax.broadcasted_iota(jnp.int32, (tm, LANES), 1)
    wcat = jnp.where(lane < N_HEADS_A, ea * inv,
                     jnp.where(lane < 2 * N_HEADS_A, pltpu.roll(eb * inv, N_HEADS_A, 1),
                               pltpu.roll(ec * inv, 2 * N_HEADS_A, 1)))
    hi = wcat.astype(BF16)
    lo = (wcat - hi.astype(F32)).astype(BF16)
    wide = _dot(jnp.concatenate([hi, lo], axis=1), ex_ref[...])
    ya = (wide[:, 0:GROUP_W] * o1[...].astype(F32) + wide[:, GROUP_W:2 * GROUP_W] * o2
          + wide[:, 2 * GROUP_W:] * o3)
    y = _dot(ya.astype(BF16), w_ref[0:GROUP_W, :]) + _dot(yb_ref[...], w_ref[GROUP_W:, :])
    out_ref[...] = _xattn_block(x_ref[...] + y, gxa_ref, wq_ref, kv_ref, wo_ref)


def _ev_tail(x, outs, stats, yb, w_bf16, tail):
    bn, s_len, d = x.shape
    tm = min(ROW_TILE, s_len)
    n_br = len(DILATIONS)
    ex = np.zeros((LANES, n_br * GROUP_W), np.float32)
    for i in range(n_br):
        for h in range(N_HEADS_A):
            ex[i * N_HEADS_A + h, i * GROUP_W + h * HEAD_DIM:i * GROUP_W + (h + 1) * HEAD_DIM] = 1.0
    ex = np.concatenate([ex, ex], axis=0)

    def o_spec(dil):
        return pl.BlockSpec((None, tm // dil, dil * GROUP_W), lambda b, i: (b, i, 0))

    def l_spec(dil):
        return pl.BlockSpec((None, tm // dil, dil * LANES), lambda b, i: (b, i, 0))

    x_spec = pl.BlockSpec((None, tm, d), lambda b, i: (b, i, 0))
    n_slabs = GROUP_W // LANES
    tail_specs, tail_args = _xattn_operands(tail, d)
    return pl.pallas_call(
        _ev_tail_kernel,
        out_shape=jax.ShapeDtypeStruct(x.shape, F32),
        grid=(bn, s_len // tm),
        in_specs=[o_spec(d_) for d_ in DILATIONS] + [l_spec(d_) for d_ in DILATIONS]
        + [o_spec(1),
           _resident(ex.shape, lambda b, i: (0, 0)),
           _resident((2 * GROUP_W, d), lambda b, i: (0, 0)),
           x_spec] + tail_specs,
        out_specs=x_spec,
        scratch_shapes=[pltpu.VMEM((n_slabs, tm, LANES), F32), pltpu.VMEM((n_slabs, tm, LANES), F32),
                        pltpu.VMEM((tm, LANES), F32), pltpu.VMEM((tm, LANES), F32)],
        compiler_params=_cparams(("parallel", "parallel")),
        name="ev_tail",
    )(*outs, *stats, yb, jnp.asarray(ex, BF16), w_bf16, x, *tail_args)


def _od_tail_kernel(yc_ref, yd_ref, w_ref, x_ref, gxa_ref, wq_ref, kv_ref, wo_ref, out_ref):
    half = yc_ref.shape[-1]
    x0 = x_ref[...] + _dot(yc_ref[...], w_ref[0:half, :]) + _dot(yd_ref[...], w_ref[half:, :])
    out_ref[...] = _xattn_block(x0, gxa_ref, wq_ref, kv_ref, wo_ref)


def _od_tail(x, yc_tm, yd, w_bf16, tail):
    bn, s_len, d = x.shape
    tm = min(ROW_TILE_WIDE, s_len)
    c = yd.shape[-1]
    x_spec = pl.BlockSpec((None, tm, d), lambda b, i: (b, i, 0))
    tail_specs, tail_args = _xattn_operands(tail, d)
    return pl.pallas_call(
        _od_tail_kernel,
        out_shape=jax.ShapeDtypeStruct(x.shape, F32),
        grid=(bn, s_len // tm),
        in_specs=[pl.BlockSpec((tm, c), lambda b, i: (i, b)),
                  pl.BlockSpec((None, tm, c), lambda b, i: (b, i, 0)),
                  _resident((2 * c, d), lambda b, i: (0, 0)),
                  x_spec] + tail_specs,
        out_specs=x_spec,
        compiler_params=_cparams(("parallel", "parallel")),
        name="od_tail",
    )(yc_tm, yd, w_bf16, x, *tail_args)


def _norm_matmul_kernel(x_ref, g_ref, w_ref, o_ref):
    o_ref[...] = _dot(_rms(x_ref[...], g_ref[...]).astype(BF16), w_ref[...]).astype(o_ref.dtype)


def _mem_kv(mem, g, w_bf16):
    bn, m_len, d = mem.shape
    n_out = w_bf16.shape[1]
    return pl.pallas_call(
        _norm_matmul_kernel,
        out_shape=jax.ShapeDtypeStruct((bn, m_len, n_out), BF16),
        grid=(bn,),
        in_specs=[pl.BlockSpec((None, m_len, d), lambda b: (b, 0, 0)),
                  pl.BlockSpec((1, d), lambda b: (0, 0)),
                  pl.BlockSpec((d, n_out), lambda b: (0, 0))],
        out_specs=pl.BlockSpec((None, m_len, n_out), lambda b: (b, 0, 0)),
        compiler_params=_cparams(("parallel",)),
        name="mem_kv",
    )(mem, g.reshape(1, d), w_bf16)


def _od_proj_kernel(x_ref, g_ref, w_ref, ug_ref, *rest):
    hg_refs, slab = rest[:-1], rest[-1]
    bn, tq, d = x_ref.shape
    h = _rms(x_ref[...].reshape(bn * tq, d), g_ref[...]).astype(BF16)
    n_ug = 2 * LRU_W
    n_slabs = GROUP_W // LANES
    for c in range(n_ug // GROUP_W):
        acc = _dot(h, w_ref[:, c * GROUP_W:(c + 1) * GROUP_W])
        for b in range(bn):
            for j in range(n_slabs):
                slab[j, pl.ds(b, tq, stride=bn), :] = acc[b * tq:(b + 1) * tq, j * LANES:(j + 1) * LANES]
        for j in range(n_slabs):
            ug_ref[:, c * GROUP_W + j * LANES:c * GROUP_W + (j + 1) * LANES] = slab[j]
    for c, hg_ref in enumerate(hg_refs):
        acc = _dot(h, w_ref[:, n_ug + c * GROUP_W:n_ug + (c + 1) * GROUP_W])
        hg_ref[...] = acc.reshape(bn, tq, GROUP_W).astype(hg_ref.dtype)


HG_INPUT_DTYPES = (("q", BF16), ("f_fw", F32), ("f_bw", F32), ("i_in", BF16), ("g_out", BF16))


def _od_proj(x, g, w_bf16):
    bn, s_len, d = x.shape
    tq = min(ROW_TILE_WIDE // bn, s_len)
    n_ug = 2 * LRU_W
    n_hg = w_bf16.shape[1] - n_ug
    assert n_hg == len(HG_INPUT_DTYPES) * GROUP_W
    hg_spec = pl.BlockSpec((bn, tq, GROUP_W), lambda i: (0, i, 0))
    ug, *hg = pl.pallas_call(
        _od_proj_kernel,
        out_shape=[jax.ShapeDtypeStruct((s_len * bn, n_ug), F32)]
        + [jax.ShapeDtypeStruct((bn, s_len, GROUP_W), dt) for _, dt in HG_INPUT_DTYPES],
        grid=(s_len // tq,),
        in_specs=[pl.BlockSpec((bn, tq, d), lambda i: (0, i, 0)),
                  pl.BlockSpec((1, d), lambda i: (0, 0)),
                  _resident((d, n_ug + n_hg), lambda i: (0, 0))],
        out_specs=[pl.BlockSpec((tq * bn, n_ug), lambda i: (i, 0))] + [hg_spec] * len(HG_INPUT_DTYPES),
        scratch_shapes=[pltpu.VMEM((GROUP_W // LANES, tq * bn, LANES), F32)],
        compiler_params=_cparams(("parallel",)),
        name="od_proj",
    )(x, g.reshape(1, d), w_bf16)
    return ug.reshape(s_len, bn, n_ug), dict(zip([n for n, _ in HG_INPUT_DTYPES], hg))


def _gelu_tanh(x):
    return 0.5 * x * (1.0 + jnp.tanh(math.sqrt(2.0 / math.pi) * (x + 0.044715 * (x * x * x))))


def _rglru_kernel(*refs, reverse, n_chunks):
    if reverse:
        (ul_ref, u_ref, ur_ref, gate_ref, hf_ref, cw_ref, cb_ref, wa_ref, ba_ref, wx_ref, bx_ref, sp_ref,
         out_ref, ext, a_s, b_s, h_s, carry) = refs
    else:
        (ul_ref, u_ref, ur_ref, cw_ref, cb_ref, wa_ref, ba_ref, wx_ref, bx_ref, sp_ref,
         out_ref, ext, a_s, b_s, carry) = refs
        h_s = out_ref
    step = pl.program_id(0)
    chunk = (n_chunks - 1 - step) if reverse else step
    t_len, bn, c = u_ref.shape

    @pl.when(step == 0)
    def _():
        carry[...] = jnp.zeros_like(carry)

    ext[0:2] = jnp.where(chunk == 0, 0.0, ul_ref[...])
    ext[2:t_len + 2] = u_ref[...]
    ext[t_len + 2:t_len + 3] = jnp.where(chunk == n_chunks - 1, 0.0, ur_ref[...])
    uc = cb_ref[...].reshape(1, 1, c)
    for j in range(4):
        uc = uc + cw_ref[j:j + 1, :].reshape(1, 1, c) * ext[j:j + t_len]

    u2 = uc.reshape(t_len * bn, c)
    ub = u2.astype(BF16)
    half = c // 2

    def gate(w_ref, b_ref):
        z = jnp.concatenate([_dot(ub[:, :half], w_ref[0]), _dot(ub[:, half:], w_ref[1])], axis=-1)
        return 0.5 * jnp.tanh(0.5 * (z + b_ref[...])) + 0.5

    r = gate(wa_ref, ba_ref)
    ig = gate(wx_ref, bx_ref)
    a = jnp.exp((-LRU_C) * r * sp_ref[...])
    b = jnp.sqrt(1.0 - a * a) * (ig * u2)
    a_s[...] = a.reshape(t_len, bn, c)
    b_s[...] = b.reshape(t_len, bn, c)

    def body(k, h):
        t = (t_len - 1 - k) if reverse else k
        h = a_s[t] * h + b_s[t]
        h_s[t] = h
        return h

    carry[...] = lax.fori_loop(0, t_len, body, carry[...], unroll=8)

    if reverse:
        out_ref[...] = ((hf_ref[...] + h_s[...]) * _gelu_tanh(gate_ref[...])).astype(out_ref.dtype)


def _rglru(ug, hf, conv_w, conv_b, wa, ba, wx, bx, lam, bn, *, reverse):
    s_len = ug.shape[0]
    c = LRU_W
    ug3 = ug
    t_len = min(LRU_T, s_len)
    n_chunks = s_len // t_len

    def ck(i):
        return (n_chunks - 1 - i) if reverse else i

    half = c // 2
    nb = LRU_BLOCKS // 2

    def dense_halves(w):
        w = w.astype(F32).reshape(2, nb, c // LRU_BLOCKS, c // LRU_BLOCKS)
        eye = jnp.eye(nb, dtype=F32)
        return jnp.einsum('gnij,nm->gnimj', w, eye).reshape(2, half, half).astype(BF16)

    softplus_neg_lam = jax.nn.softplus(-lam.astype(F32)).reshape(1, c)
    vec = pl.BlockSpec((1, c), lambda i: (0, 0))
    wspec = pl.BlockSpec((2, half, half), lambda i: (0, 0, 0))
    blk = pl.BlockSpec((t_len, bn, c), lambda i: (ck(i), 0, 0))
    in_specs = [
        pl.BlockSpec((2, bn, c), lambda i: (jnp.maximum(ck(i) * (t_len // 2) - 1, 0), 0, 0)),
        blk,
        pl.BlockSpec((1, bn, c), lambda i: (jnp.minimum((ck(i) + 1) * t_len, s_len - 1), 0, 0)),
    ]
    args = [ug3, ug3, ug3]
    scratch = [pltpu.VMEM((t_len + 3, bn, c), F32), pltpu.VMEM((t_len, bn, c), F32),
               pltpu.VMEM((t_len, bn, c), F32)]
    if reverse:
        in_specs += [pl.BlockSpec((t_len, bn, c), lambda i: (ck(i), 0, 1)), blk]
        args += [ug3, hf]
        scratch.append(pltpu.VMEM((t_len, bn, c), F32))
    scratch.append(pltpu.VMEM((bn, c), F32))
    in_specs += [pl.BlockSpec((4, c), lambda i: (0, 0)), vec, wspec, vec, wspec, vec, vec]
    args += [conv_w.astype(F32), conv_b.astype(F32).reshape(1, c), dense_halves(wa),
             ba.astype(F32).reshape(1, c), dense_halves(wx), bx.astype(F32).reshape(1, c), softplus_neg_lam]
    return pl.pallas_call(
        functools.partial(_rglru_kernel, reverse=reverse, n_chunks=n_chunks),
        out_shape=jax.ShapeDtypeStruct((s_len, bn, c), BF16 if reverse else F32),
        grid=(n_chunks,),
        in_specs=in_specs,
        out_specs=blk,
        scratch_shapes=scratch,
        compiler_params=_cparams(("arbitrary",)),
        name="rglru_bw" if reverse else "rglru_fw",
    )(*args)


def _hg_matrices(c, reverse):
    n_levels = int(math.log2(c))
    t = np.arange(c)[:, None]
    r = np.arange(c)[None, :]
    mats, masks, upper = [], [], []
    if not reverse:
        mats.append(r <= t)
        mats.append(r > t)
    else:
        mats.append(r >= t)
        mats.append(r < t)
    for lev in range(n_levels):
        half = c >> (lev + 1)
        parent = 2 * half
        start = (np.arange(c) // parent) * parent
        mid = (start + half)[:, None]
        later = (np.arange(c) % parent >= half)[:, None]
        if not reverse:
            m = np.where(later, (r >= mid) & (r <= t), (r > t) & (r < mid))
            is_q = later
        else:
            m = np.where(later, (r >= mid) & (r < t), (r >= t) & (r < mid))
            is_q = ~later
        mats.append(m)
        same_parent = (start[:, None] == start[None, :])
        masks.append(same_parent & is_q & (~is_q).T)
        upper.append(np.broadcast_to(is_q, (c, HG_DK)))
    mat = np.concatenate(mats, axis=0).astype(np.float32)
    return (np.concatenate([mat, mat], axis=1), np.stack(masks).astype(np.float32),
            np.stack(upper).astype(np.float32))


def _hgrn_kernel(*refs, reverse, n_steps, n_levels):
    if reverse:
        (q_ref, f_ref, v_ref, go_ref, of_ref, lb_ref, gn_ref, mat_ref, mask_ref, isq_ref, out_ref,
         state, x_scr, oi_scr, aux_scr, dec_scr, kk_scr) = refs
    else:
        (q_ref, f_ref, v_ref, lb_ref, mat_ref, mask_ref, isq_ref, out_ref,
         state, x_scr, oi_scr, aux_scr, dec_scr, kk_scr) = refs
    step = pl.program_id(1)
    c = HG_CHUNK
    t_len = q_ref.shape[0]
    n_chunks = t_len // c

    @pl.when(step == 0)
    def _():
        state[...] = jnp.zeros_like(state)

    def chunk_body(ci, carry):
        ch = (n_chunks - 1 - ci) if reverse else ci
        rows = slice(ch * c, (ch + 1) * c)
        edge = (c - 1) if not reverse else 0
        for hp in range(HG_HEADS // 2):
            cols2 = slice(2 * hp * HG_DK, 2 * (hp + 1) * HG_DK)
            fl = f_ref[rows, cols2]
            e = jnp.exp(-jnp.abs(fl))
            r = 1.0 / (1.0 + e)
            er = e * r
            pos = fl > 0.0
            lb = lb_ref[0:1, cols2]
            oml = lb_ref[1:2, cols2]
            g2 = jnp.log2(lb + oml * jnp.where(pos, r, er))
            kk_scr[hp] = oml * jnp.where(pos, er, r)
            g_hi = g2.astype(BF16)
            g_lo = (g2 - g_hi.astype(F32)).astype(BF16)
            dec_scr[hp] = jnp.exp2(_dot(mat_ref[...], jnp.concatenate([g_hi, g_lo], axis=0)))

        for hd in range(HG_HEADS):
            cols = slice(hd * HG_DK, (hd + 1) * HG_DK)
            pcols = slice((hd % 2) * HG_DK, (hd % 2 + 1) * HG_DK)
            kk = kk_scr[hd // 2, :, pcols]
            dec = dec_scr.at[hd // 2]
            qv = q_ref[rows, cols].astype(F32)
            qq = qv * jax.nn.sigmoid(qv)
            qb, kb = qq.astype(BF16), kk.astype(BF16)
            d_in = dec[0:c, pcols]
            x_scr[hd, n_levels] = (qq * d_in).astype(BF16)
            x_scr[hd, n_levels + 1] = (kk * dec[c:2 * c, pcols]).astype(BF16)
            aux_scr[hd, 0] = jnp.broadcast_to(d_in[edge:edge + 1, :], (c, HG_DK))
            aux_scr[hd, 1] = jnp.broadcast_to(jnp.sum(qq * kk, axis=-1, keepdims=True), (c, HG_DK))
            for lev in range(n_levels):
                half = c >> (lev + 1)
                if half >= SUBLANES_BF16:
                    first_is_q = bool(reverse)
                    base = jnp.concatenate(
                        [(qb if (blk % 2 == 1) != first_is_q else kb)[blk * half:(blk + 1) * half]
                         for blk in range(c // half)], axis=0)
                else:
                    base = jnp.where(isq_ref[lev] > 0.5, qb, kb)
                x_scr[hd, lev] = base * dec[(2 + lev) * c:(3 + lev) * c, pcols].astype(BF16)

        for hd in range(HG_HEADS):
            cols = slice(hd * HG_DK, (hd + 1) * HG_DK)
            att = None
            for lev in range(n_levels):
                xl = x_scr[hd, lev]
                term = mask_ref[lev] * _nt_dot(xl, xl)
                att = term if att is None else att + term
            st = state[hd]
            oi_scr[hd] = _nt_dot(x_scr[hd, n_levels], st.astype(BF16))
            vt = v_ref[rows, cols].astype(F32).T.astype(BF16)
            state[hd] = st * aux_scr[hd, 0] + _dot(vt, x_scr[hd, n_levels + 1])
            x_scr[hd, 0] = att.astype(BF16)

        for hd in range(HG_HEADS):
            cols = slice(hd * HG_DK, (hd + 1) * HG_DK)
            vb = v_ref[rows, cols]
            o = oi_scr[hd] + _dot(x_scr[hd, 0], vb) + aux_scr[hd, 1] * vb.astype(F32)

            if reverse:
                o = o + of_ref[rows, cols]
                o = o * lax.rsqrt(jnp.mean(o * o, axis=-1, keepdims=True) + EPS) * gn_ref[...]
                gv = go_ref[rows, cols].astype(F32)
                out_ref[rows, cols] = (o * (gv * jax.nn.sigmoid(gv))).astype(out_ref.dtype)
            else:
                out_ref[rows, cols] = o
        return carry

    for ci in range(n_chunks):
        chunk_body(ci, 0)


def _hgrn(hg, o_fw, lb, gnorm_g, *, reverse):
    bn, s_len, _ = hg["q"].shape
    w = HG_HEADS * HG_DK
    t_len = min(HG_T, s_len)
    n_steps = s_len // t_len
    mats, masks, isq = _hg_matrices(HG_CHUNK, reverse)
    n_levels = masks.shape[0]
    lbf = lb.astype(F32)
    lb_tab = jnp.stack([lbf, 1.0 - lbf])

    blk = pl.BlockSpec((None, t_len, w), lambda b, i: (b, (n_steps - 1 - i) if reverse else i, 0))

    def const(shape):
        return pl.BlockSpec(shape, lambda b, i: (0,) * len(shape))

    in_specs = [blk, blk, blk]
    args = [hg["q"], hg["f_bw" if reverse else "f_fw"], hg["i_in"]]
    if reverse:
        in_specs += [blk, blk]
        args += [hg["g_out"], o_fw]
    in_specs.append(const((2, w)))
    args.append(lb_tab)
    if reverse:
        in_specs.append(const((1, HG_DK)))
        args.append(gnorm_g.astype(F32).reshape(1, HG_DK))
    in_specs += [const(mats.shape), const(masks.shape), const(isq.shape)]
    args += [jnp.asarray(mats, BF16), jnp.asarray(masks, F32), jnp.asarray(isq, BF16)]
    return pl.pallas_call(
        functools.partial(_hgrn_kernel, reverse=reverse, n_steps=n_steps, n_levels=n_levels),
        out_shape=jax.ShapeDtypeStruct((bn, s_len, w), BF16 if reverse else F32),
        grid=(bn, n_steps),
        in_specs=in_specs,
        out_specs=blk,
        scratch_shapes=[pltpu.VMEM((HG_HEADS, HG_DK, HG_DK), F32),
                        pltpu.VMEM((HG_HEADS, n_levels + 2, HG_CHUNK, HG_DK), BF16),
                        pltpu.VMEM((HG_HEADS, HG_CHUNK, HG_DK), F32),
                        pltpu.VMEM((HG_HEADS, 2, HG_CHUNK, HG_DK), F32),
                        pltpu.VMEM((HG_HEADS // 2, (n_levels + 2) * HG_CHUNK, 2 * HG_DK), F32),
                        pltpu.VMEM((HG_HEADS // 2, HG_CHUNK, 2 * HG_DK), F32)],
        compiler_params=_cparams(("parallel", "arbitrary")),
        name="hgrn_bw" if reverse else "hgrn_fw",
    )(*args)


def _even_layer(x, g, w_in, w_out, rpb, tail):
    qkv, view4, view16 = _ev_proj(x, g, w_in.astype(BF16))
    outs, stats = zip(*[_dilated_branch(view, dil) for dil, view in zip(DILATIONS, (qkv, view4, view16))])
    yb = _neighborhood(qkv, rpb)
    return _ev_tail(x, outs, stats, yb, w_out.astype(BF16), tail)


def _odd_layer(x, g, w_in, w_out, conv_w, conv_b, wa, ba, wx, bx, lam, lb_f, lb_b, gnorm_g, tail):
    bn, s_len, _ = x.shape
    ug, hg = _od_proj(x, g, w_in.astype(BF16))
    h_fw = _rglru(ug, None, conv_w, conv_b, wa[0], ba[0], wx[0], bx[0], lam[0], bn, reverse=False)
    yc = _rglru(ug, h_fw, conv_w, conv_b, wa[1], ba[1], wx[1], bx[1], lam[1], bn, reverse=True)
    o_fw = _hgrn(hg, None, lb_f, gnorm_g, reverse=False)
    yd = _hgrn(hg, o_fw, lb_b, gnorm_g, reverse=True)
    return _od_tail(x, yc.reshape(s_len, bn * LRU_W), yd, w_out.astype(BF16), tail)


def kernel(x, mem, norm_mix_g, norm_xa_g, norm_mem_g, norm_mlp_g, final_norm_g, ev_w_in, ev_w_out, na_rpb,
           od_w_in, od_w_out, conv_w, conv_b, lru_wa, lru_ba, lru_wx, lru_bx, lru_lambda, hgrn_lb_logits,
           hgrn_norm_g, xa_wq, xa_wkv, xa_wo, mlp_w1, mlp_w2):
    depth = norm_mix_g.shape[0]
    p_lb = jax.nn.softmax(hgrn_lb_logits.astype(F32), axis=0)
    lower_bounds = jnp.cumsum(p_lb, axis=0) - p_lb[0:1]
    for layer in range(depth):
        tail = dict(g_xa=norm_xa_g[layer], wq=xa_wq[layer].astype(BF16),
                    kv=_mem_kv(mem, norm_mem_g[layer], xa_wkv[layer].astype(BF16)),
                    wo=xa_wo[layer].astype(BF16), g_mlp=norm_mlp_g[layer],
                    w1=mlp_w1[layer].astype(BF16), w2=mlp_w2[layer].astype(BF16),
                    g_final=final_norm_g if layer == depth - 1 else None)
        if layer % 2 == 0:
            e = layer // 2
            x = _even_layer(x, norm_mix_g[layer], ev_w_in[e], ev_w_out[e], na_rpb[e], tail)
        else:
            o = layer // 2
            x = _odd_layer(x, norm_mix_g[layer], od_w_in[o], od_w_out[o], conv_w[o], conv_b[o],
                           lru_wa[o], lru_ba[o], lru_wx[o], lru_bx[o], lru_lambda[o],
                           lower_bounds[layer, 0], lower_bounds[layer, 1], hgrn_norm_g[o], tail)
        x = _mlp(x, tail)
    return x
```

```python
import functools
import math

import jax
import jax.numpy as jnp
import numpy as np
from jax import lax
from jax.experimental import pallas as pl
from jax.experimental.pallas import tpu as pltpu

F32 = jnp.float32
BF16 = jnp.bfloat16

D_MODEL = 1024
HEAD_DIM = 64
ROT_DIM = 16
ROPE_THETA = 500000.0
N_HEADS_A = 8
N_HEADS_B = 8
GROUP_W = N_HEADS_A * HEAD_DIM
DILATIONS = (1, 4, 16)
WIN_HALF = 64
GRID_W = 64
NA_ROWS = 8
NA_COLS = 16
LRU_W = 512
LRU_BLOCKS = 8
LRU_C = 8.0
HG_HEADS = 4
HG_DK = 128
XA_HEADS = 4
XA_DH = 256
D_FF = 4096
EPS = 1e-6
LOG2E = math.log2(math.e)

LANES = 128
SUBLANES_BF16 = 16
V7X_VMEM_LIMIT_BYTES = 56 * 1024 * 1024

ROW_TILE = 512
ROW_TILE_WIDE = 1024
ATT_Q_BLOCK = 512
ATT_SUB = 128
NA_ROW_GROUP = 8
LRU_T = 128
HG_CHUNK = 128
HG_T = 1024


def _cparams(sem):
    return pltpu.CompilerParams(dimension_semantics=sem, vmem_limit_bytes=V7X_VMEM_LIMIT_BYTES)


def _resident(shape, index_map):
    return pl.BlockSpec(shape, index_map, pipeline_mode=pl.Buffered(1))


def _rms(x, g):
    return x * lax.rsqrt(jnp.mean(x * x, axis=-1, keepdims=True) + EPS) * g


def _nt_dot(a, b):
    return lax.dot_general(a, b, (((1,), (1,)), ((), ())), preferred_element_type=F32)


def _dot(a, b):
    return jnp.dot(a, b, preferred_element_type=F32)


def _head_selectors():
    lane = lax.broadcasted_iota(jnp.int32, (1, LANES), 1)
    low = (lane < HEAD_DIM).astype(F32)
    return low.astype(BF16), (1.0 - low).astype(BF16)


def _ev_proj_kernel(x_ref, g_ref, w_ref, rc_ref, rs1_ref, rs2_ref, o_ref, o4_ref, o16_ref, slab):
    h = _rms(x_ref[...], g_ref[...]).astype(BF16)
    scale = HEAD_DIM ** -0.5 * LOG2E
    tm = x_ref.shape[0]
    n_slabs = GROUP_W // LANES
    for c in range(6):
        acc = _dot(h, w_ref[:, c * GROUP_W:(c + 1) * GROUP_W])
        if c in (0, 1):
            rc, rs1, rs2 = rc_ref[...], rs1_ref[...], rs2_ref[...]
            parts = []
            for j in range(n_slabs):
                t = acc[:, j * LANES:(j + 1) * LANES]
                t = t * rc + pltpu.roll(t, LANES - ROT_DIM // 2, 1) * rs1 + pltpu.roll(t, ROT_DIM // 2, 1) * rs2
                parts.append(t)
            acc = jnp.concatenate(parts, axis=-1)
        if c in (0, 3):
            acc = acc * scale
        o_ref[:, c * GROUP_W:(c + 1) * GROUP_W] = acc.astype(BF16)
        if c < 3:
            for j in range(n_slabs):
                slab[j] = acc[:, j * LANES:(j + 1) * LANES]
            for dil, dst in ((DILATIONS[1], o4_ref), (DILATIONS[2], o16_ref)):
                for r in range(dil):
                    for j in range(n_slabs):
                        col = r * 3 * GROUP_W + c * GROUP_W + j * LANES
                        dst[:, col:col + LANES] = slab[j, pl.ds(r, tm // dil, stride=dil), :].astype(BF16)


def _rope_tables(s_len):
    half = ROT_DIM // 2
    inv = jnp.asarray(ROPE_THETA ** (-np.arange(half) * 2.0 / ROT_DIM), F32)
    ang = jnp.arange(s_len, dtype=F32)[:, None] * inv[None, :]
    cos, sin = jnp.cos(ang), jnp.sin(ang)
    ones = jnp.ones((s_len, HEAD_DIM - ROT_DIM), F32)
    zeros = jnp.zeros((s_len, HEAD_DIM - ROT_DIM), F32)
    zh = jnp.zeros((s_len, half), F32)
    rc = jnp.concatenate([cos, cos, ones], axis=-1)
    rs1 = jnp.concatenate([-sin, zh, zeros], axis=-1)
    rs2 = jnp.concatenate([zh, sin, zeros], axis=-1)
    rep = LANES // HEAD_DIM
    return jnp.tile(rc, (1, rep)), jnp.tile(rs1, (1, rep)), jnp.tile(rs2, (1, rep))


def _ev_proj(x, g, w_bf16):
    bn, s_len, d = x.shape
    tm = min(ROW_TILE_WIDE, s_len)
    n_out = w_bf16.shape[1]
    rc, rs1, rs2 = _rope_tables(s_len)
    tbl_spec = pl.BlockSpec((tm, LANES), lambda b, i: (i, 0))
    wa = 3 * GROUP_W
    d4, d16 = DILATIONS[1], DILATIONS[2]
    return pl.pallas_call(
        _ev_proj_kernel,
        out_shape=(jax.ShapeDtypeStruct((bn, s_len, n_out), BF16),
                   jax.ShapeDtypeStruct((bn, s_len // d4, d4 * wa), BF16),
                   jax.ShapeDtypeStruct((bn, s_len // d16, d16 * wa), BF16)),
        grid=(bn, s_len // tm),
        in_specs=[
            pl.BlockSpec((None, tm, d), lambda b, i: (b, i, 0)),
            pl.BlockSpec((1, d), lambda b, i: (0, 0)),
            _resident((d, n_out), lambda b, i: (0, 0)),
            tbl_spec, tbl_spec, tbl_spec,
        ],
        out_specs=(pl.BlockSpec((None, tm, n_out), lambda b, i: (b, i, 0)),
                   pl.BlockSpec((None, tm // d4, d4 * wa), lambda b, i: (b, i, 0)),
                   pl.BlockSpec((None, tm // d16, d16 * wa), lambda b, i: (b, i, 0))),
        scratch_shapes=[pltpu.VMEM((GROUP_W // LANES, tm, LANES), F32)],
        compiler_params=_cparams(("parallel", "parallel")),
        name="ev_proj",
    )(x, g.reshape(1, d), w_bf16, rc, rs1, rs2)


def _dilated_kernel(q_ref, kl_ref, kc_ref, kr_ref, vl_ref, vc_ref, vr_ref, o_ref, stat_ref,
                    kext, vext, s_scr, p_scr, *, lq, l_total):
    i = pl.program_id(2)
    kext[0:WIN_HALF, :] = kl_ref[...]
    kext[WIN_HALF:WIN_HALF + lq, :] = kc_ref[...]
    kext[WIN_HALF + lq:, :] = kr_ref[...]
    vext[0:WIN_HALF, :] = vl_ref[...]
    vext[WIN_HALF:WIN_HALF + lq, :] = vc_ref[...]
    vext[WIN_HALF + lq:, :] = vr_ref[...]

    wk = ATT_SUB + 2 * WIN_HALF
    qi = lax.broadcasted_iota(jnp.int32, (ATT_SUB, wk), 0)
    ci = lax.broadcasted_iota(jnp.int32, (ATT_SUB, wk), 1)
    band_bias = jnp.where((ci - qi >= 0) & (ci - qi <= 2 * WIN_HALF), 0.0, -jnp.inf).astype(F32)
    crow = lax.broadcasted_iota(jnp.int32, (1, wk), 1)
    lane = lax.broadcasted_iota(jnp.int32, (ATT_SUB, LANES), 1)
    low_half = lane < HEAD_DIM
    sel_lo, sel_hi = _head_selectors()
    n_sub = lq // ATT_SUB
    n_hp = GROUP_W // LANES

    for j in range(n_sub):
        base = i * lq + j * ATT_SUB - WIN_HALF
        in_seq = (crow + base >= 0) & (crow + base < l_total)
        bias = band_bias + jnp.where(in_seq, 0.0, -jnp.inf).astype(F32)
        bias2 = jnp.concatenate([bias, bias], axis=0)
        rows = slice(j * ATT_SUB, (j + 1) * ATT_SUB)
        wrows = slice(j * ATT_SUB, j * ATT_SUB + wk)
        for hp in range(n_hp):
            cols = slice(hp * LANES, (hp + 1) * LANES)
            qp = q_ref[rows, cols]
            q2 = jnp.concatenate([qp * sel_lo, qp * sel_hi], axis=0)
            s_scr[j * n_hp + hp] = _nt_dot(q2, kext[wrows, cols]) + bias2
    for j in range(n_sub):
        stat = jnp.ones((ATT_SUB, LANES), F32)
        for hp in range(n_hp):
            t = j * n_hp + hp
            s = s_scr[t]
            m = jnp.max(s, axis=-1, keepdims=True)
            p = jnp.exp2(s - m)
            den = jnp.sum(p, axis=-1, keepdims=True)
            p_scr[t] = p.astype(BF16)
            stat = jnp.where(lane == 2 * hp, m[0:ATT_SUB], jnp.where(lane == 2 * hp + 1, m[ATT_SUB:], stat))
            stat = jnp.where(lane == N_HEADS_A + 2 * hp, den[0:ATT_SUB],
                             jnp.where(lane == N_HEADS_A + 2 * hp + 1, den[ATT_SUB:], stat))
        stat_ref[j * ATT_SUB:(j + 1) * ATT_SUB, :] = stat
    for j in range(n_sub):
        rows = slice(j * ATT_SUB, (j + 1) * ATT_SUB)
        wrows = slice(j * ATT_SUB, j * ATT_SUB + wk)
        for hp in range(n_hp):
            cols = slice(hp * LANES, (hp + 1) * LANES)
            o = _dot(p_scr[j * n_hp + hp], vext[wrows, cols])
            o_ref[rows, cols] = jnp.where(low_half, o[0:ATT_SUB], o[ATT_SUB:]).astype(BF16)


def _dilated_branch(view, dil):
    bn, l_total, width = view.shape
    n_groups = width // (dil * GROUP_W)
    lq = min(ATT_Q_BLOCK, l_total)
    nblk = l_total // lq
    per = lq // WIN_HALF
    n_halo = l_total // WIN_HALF
    n_units = (lq // ATT_SUB) * (GROUP_W // LANES)

    def cur(g):
        return pl.BlockSpec((None, lq, GROUP_W), lambda b, r, i: (b, i, r * n_groups + g))

    def left(g):
        return pl.BlockSpec((None, WIN_HALF, GROUP_W),
                            lambda b, r, i: (b, jnp.maximum(i * per - 1, 0), r * n_groups + g))

    def right(g):
        return pl.BlockSpec((None, WIN_HALF, GROUP_W),
                            lambda b, r, i: (b, jnp.minimum((i + 1) * per, n_halo - 1), r * n_groups + g))

    stat_shape = jax.ShapeDtypeStruct((bn, l_total, dil * LANES), F32)
    stat_spec = pl.BlockSpec((None, lq, LANES), lambda b, r, i: (b, i, r))
    return pl.pallas_call(
        functools.partial(_dilated_kernel, lq=lq, l_total=l_total),
        out_shape=(jax.ShapeDtypeStruct((bn, l_total, dil * GROUP_W), BF16), stat_shape),
        grid=(bn, dil, nblk),
        in_specs=[cur(0), left(1), cur(1), right(1), left(2), cur(2), right(2)],
        out_specs=(pl.BlockSpec((None, lq, GROUP_W), lambda b, r, i: (b, i, r)), stat_spec),
        scratch_shapes=[pltpu.VMEM((lq + 2 * WIN_HALF, GROUP_W), BF16),
                        pltpu.VMEM((lq + 2 * WIN_HALF, GROUP_W), BF16),
                        pltpu.VMEM((n_units, 2 * ATT_SUB, ATT_SUB + 2 * WIN_HALF), F32),
                        pltpu.VMEM((n_units, 2 * ATT_SUB, ATT_SUB + 2 * WIN_HALF), BF16)],
        compiler_params=_cparams(("parallel", "parallel", "parallel")),
        name=f"dilated_attn_d{dil}",
    )(view, view, view, view, view, view, view)


NA_QT = NA_ROW_GROUP * GRID_W
NA_KT = 2 * NA_QT
NA_KW = NA_ROWS * GRID_W


def _na_tables(rpb):
    rpb = rpb.astype(F32)
    nh = rpb.shape[0]
    ext = GRID_W - NA_COLS
    padded = jnp.concatenate([jnp.repeat(rpb[..., :1], ext, axis=-1), rpb,
                              jnp.repeat(rpb[..., -1:], ext, axis=-1)], axis=-1)
    skew = jnp.tile(padded, (1, 1, GRID_W + 1))[..., :2 * GRID_W * GRID_W]
    t1 = skew.reshape(nh, -1, GRID_W, 2 * GRID_W)[..., ::-1, :GRID_W]
    qc = np.arange(GRID_W)[:, None]
    kc = np.arange(GRID_W)[None, :]
    c0 = np.clip(qc - NA_COLS // 2, 0, GRID_W - NA_COLS)
    t1 = jnp.where((kc >= c0) & (kc < c0 + NA_COLS), t1 * LOG2E, -jnp.inf)
    per_delta = [jnp.transpose(t1[:, NA_ROWS - 1 - delta:2 * NA_ROWS - 1 - delta], (0, 2, 1, 3))
                 .reshape(nh, GRID_W, NA_KW) for delta in range(NA_ROWS)]
    return jnp.stack(per_delta).reshape(NA_ROWS, nh // 2, 2 * GRID_W, NA_KW)


def _na_kernel(q_ref, k_ref, v_ref, tb_ref, o_ref, s_scr, p_scr, inv_scr, *, rows):
    i0 = pl.program_id(1) * NA_ROW_GROUP
    w0 = jnp.clip(i0 - NA_ROWS // 2, 0, rows - 2 * NA_ROW_GROUP)
    lane = lax.broadcasted_iota(jnp.int32, (GRID_W, LANES), 1)
    low_half = lane < HEAD_DIM
    sel_lo, sel_hi = _head_selectors()
    n_hp = GROUP_W // LANES

    def key_rows(a):
        r0 = jnp.clip(i0 + a - NA_ROWS // 2, 0, rows - NA_ROWS)
        return pl.ds(pl.multiple_of((r0 - w0) * GRID_W, GRID_W), NA_KW), i0 + a - r0

    for a in range(NA_ROW_GROUP):
        krows, delta = key_rows(a)
        qrows = slice(a * GRID_W, (a + 1) * GRID_W)
        for hp in range(n_hp):
            cols = slice(hp * LANES, (hp + 1) * LANES)
            qp = q_ref[qrows, cols]
            q2 = jnp.concatenate([qp * sel_lo, qp * sel_hi], axis=0)
            s_scr[a * n_hp + hp] = _nt_dot(q2, k_ref[0, krows, cols]) + tb_ref[delta, hp]
    for t in range(NA_ROW_GROUP * n_hp):
        s = s_scr[t]
        p = jnp.exp2(s - jnp.max(s, axis=-1, keepdims=True))
        inv_scr[t] = jnp.broadcast_to(1.0 / jnp.sum(p, axis=-1, keepdims=True), (2 * GRID_W, LANES))
        p_scr[t] = p.astype(BF16)
    for a in range(NA_ROW_GROUP):
        krows, _ = key_rows(a)
        qrows = slice(a * GRID_W, (a + 1) * GRID_W)
        for hp in range(n_hp):
            cols = slice(hp * LANES, (hp + 1) * LANES)
            t = a * n_hp + hp
            o = _dot(p_scr[t], v_ref[0, krows, cols]) * inv_scr[t]
            o_ref[qrows, cols] = jnp.where(low_half, o[0:GRID_W], o[GRID_W:]).astype(BF16)


def _neighborhood(qkv, rpb):
    bn, s_len, _ = qkv.shape
    rows = s_len // GRID_W
    n_groups = rows // NA_ROW_GROUP
    tables = _na_tables(rpb)

    def window(g):
        return pl.BlockSpec(
            (pl.Element(1), pl.Element(NA_KT), pl.Element(GROUP_W)),
            lambda b, i: (b, jnp.clip(i * NA_ROW_GROUP - NA_ROWS // 2, 0, rows - 2 * NA_ROW_GROUP) * GRID_W,
                          g * GROUP_W))

    return pl.pallas_call(
        functools.partial(_na_kernel, rows=rows),
        out_shape=jax.ShapeDtypeStruct((bn, s_len, GROUP_W), BF16),
        grid=(bn, n_groups),
        in_specs=[pl.BlockSpec((None, NA_QT, GROUP_W), lambda b, i: (b, i, 3)), window(4), window(5),
                  pl.BlockSpec(tables.shape, lambda b, i: (0, 0, 0, 0))],
        out_specs=pl.BlockSpec((None, NA_QT, GROUP_W), lambda b, i: (b, i, 0)),
        scratch_shapes=[pltpu.VMEM((NA_ROW_GROUP * GROUP_W // LANES, 2 * GRID_W, NA_KW), F32),
                        pltpu.VMEM((NA_ROW_GROUP * GROUP_W // LANES, 2 * GRID_W, NA_KW), BF16),
                        pltpu.VMEM((NA_ROW_GROUP * GROUP_W // LANES, 2 * GRID_W, LANES), F32)],
        compiler_params=_cparams(("parallel", "parallel")),
        name="neighborhood_attn",
    )(qkv, qkv, qkv, tables)


def _xattn_block(x, g_ref, wq_ref, kv_ref, wo_ref):
    q = (_dot(_rms(x, g_ref[...]).astype(BF16), wq_ref[...]) * (XA_DH ** -0.5 * LOG2E)).astype(BF16)
    heads = []
    for hd in range(XA_HEADS):
        cols = slice(hd * XA_DH, (hd + 1) * XA_DH)
        s = _nt_dot(q[:, cols], kv_ref[:, cols])
        m = jnp.max(s, axis=-1, keepdims=True)
        p = jnp.exp2(s - m)
        den = jnp.sum(p, axis=-1, keepdims=True)
        vcols = slice(D_MODEL + hd * XA_DH, D_MODEL + (hd + 1) * XA_DH)
        heads.append((_dot(p.astype(BF16), kv_ref[:, vcols]) * (1.0 / den)).astype(BF16))
    return x + _dot(jnp.concatenate(heads, axis=-1), wo_ref[...])


def _mlp_block(x, g_ref, w1_ref, w2_ref, gf_ref, final_norm):
    h = _rms(x, g_ref[...]).astype(BF16)
    acc = x
    for c in range(D_FF // D_MODEL):
        cols = slice(c * D_MODEL, (c + 1) * D_MODEL)
        a = jnp.maximum(_dot(h, w1_ref[:, cols]), 0.0)
        acc = acc + _dot((a * a).astype(BF16), w2_ref[cols, :])
    return _rms(acc, gf_ref[...]) if final_norm else acc


def _xattn_operands(tail, d):
    m_len = tail["kv"].shape[1]
    specs = [pl.BlockSpec((1, d), lambda b, i: (0, 0)), _resident((d, d), lambda b, i: (0, 0)),
             pl.BlockSpec((None, m_len, 2 * d), lambda b, i: (b, 0, 0)),
             _resident((d, d), lambda b, i: (0, 0))]
    return specs, [tail["g_xa"].reshape(1, d), tail["wq"], tail["kv"], tail["wo"]]


def _mlp_kernel(x_ref, g_ref, w1_ref, w2_ref, gf_ref, out_ref, *, final_norm):
    out_ref[...] = _mlp_block(x_ref[...], g_ref, w1_ref, w2_ref, gf_ref, final_norm)


def _mlp(x, tail):
    bn, s_len, d = x.shape
    tm = min(ROW_TILE_WIDE, s_len)
    final_norm = tail["g_final"] is not None
    gf = tail["g_final"] if final_norm else tail["g_mlp"]
    x_spec = pl.BlockSpec((None, tm, d), lambda b, i: (b, i, 0))
    vec = pl.BlockSpec((1, d), lambda b, i: (0, 0))
    return pl.pallas_call(
        functools.partial(_mlp_kernel, final_norm=final_norm),
        out_shape=jax.ShapeDtypeStruct(x.shape, F32),
        grid=(bn, s_len // tm),
        in_specs=[x_spec, vec, _resident((d, D_FF), lambda b, i: (0, 0)),
                  _resident((D_FF, d), lambda b, i: (0, 0)), vec],
        out_specs=x_spec,
        compiler_params=_cparams(("parallel", "parallel")),
        name="mlp_final" if final_norm else "mlp",
    )(x, tail["g_mlp"].reshape(1, d), tail["w1"], tail["w2"], gf.reshape(1, d))


def _ev_tail_kernel(o1, o2v, o3v, s1, s2v, s3v, yb_ref, ex_ref, w_ref, x_ref,
                    gxa_ref, wq_ref, kv_ref, wo_ref, out_ref, so2, so3, st2, st3):
    tm = x_ref.shape[0]
    n_slabs = GROUP_W // LANES
    for dil, ov, sv, so, st in ((DILATIONS[1], o2v, s2v, so2, st2), (DILATIONS[2], o3v, s3v, so3, st3)):
        for r in range(dil):
            dst = pl.ds(r, tm // dil, stride=dil)
            for j in range(n_slabs):
                col = r * GROUP_W + j * LANES
                so[j, dst, :] = ov[:, col:col + LANES].astype(F32)
            st[dst, :] = sv[:, r * LANES:(r + 1) * LANES]
    o2 = jnp.concatenate([so2[j] for j in range(n_slabs)], axis=-1)
    o3 = jnp.concatenate([so3[j] for j in range(n_slabs)], axis=-1)
    ma, mb, mc = s1[...], st2[...], st3[...]
    da, db, dc = [pltpu.roll(t, LANES - N_HEADS_A, 1) for t in (ma, mb, mc)]
    m = jnp.maximum(jnp.maximum(ma, mb), mc)
    ea, eb, ec = jnp.exp2(ma - m), jnp.exp2(mb - m), jnp.exp2(mc - m)
    inv = 1.0 / (ea * da + eb * db + ec * dc)
    lane = lax.broadcasted_iota(jnp.int32, (tm, LANES), 1)
    wcat = jnp.where(lane < N_HEADS_A, ea * inv,
                     jnp.where(lane < 2 * N_HEADS_A, pltpu.roll(eb * inv, N_HEADS_A, 1),
                               pltpu.roll(ec * inv, 2 * N_HEADS_A, 1)))
    hi = wcat.astype(BF16)
    lo = (wcat - hi.astype(F32)).astype(BF16)
    wide = _dot(jnp.concatenate([hi, lo], axis=1), ex_ref[...])
    ya = (wide[:, 0:GROUP_W] * o1[...].astype(F32) + wide[:, GROUP_W:2 * GROUP_W] * o2
          + wide[:, 2 * GROUP_W:] * o3)
    y = _dot(ya.astype(BF16), w_ref[0:GROUP_W, :]) + _dot(yb_ref[...], w_ref[GROUP_W:, :])
    out_ref[...] = _xattn_block(x_ref[...] + y, gxa_ref, wq_ref, kv_ref, wo_ref)


def _ev_tail(x, outs, stats, yb, w_bf16, tail):
    bn, s_len, d = x.shape
    tm = min(ROW_TILE, s_len)
    n_br = len(DILATIONS)
    ex = np.zeros((LANES, n_br * GROUP_W), np.float32)
    for i in range(n_br):
        for h in range(N_HEADS_A):
            ex[i * N_HEADS_A + h, i * GROUP_W + h * HEAD_DIM:i * GROUP_W + (h + 1) * HEAD_DIM] = 1.0
    ex = np.concatenate([ex, ex], axis=0)

    def o_spec(dil):
        return pl.BlockSpec((None, tm // dil, dil * GROUP_W), lambda b, i: (b, i, 0))

    def l_spec(dil):
        return pl.BlockSpec((None, tm // dil, dil * LANES), lambda b, i: (b, i, 0))

    x_spec = pl.BlockSpec((None, tm, d), lambda b, i: (b, i, 0))
    n_slabs = GROUP_W // LANES
    tail_specs, tail_args = _xattn_operands(tail, d)
    return pl.pallas_call(
        _ev_tail_kernel,
        out_shape=jax.ShapeDtypeStruct(x.shape, F32),
        grid=(bn, s_len // tm),
        in_specs=[o_spec(d_) for d_ in DILATIONS] + [l_spec(d_) for d_ in DILATIONS]
        + [o_spec(1),
           _resident(ex.shape, lambda b, i: (0, 0)),
           _resident((2 * GROUP_W, d), lambda b, i: (0, 0)),
           x_spec] + tail_specs,
        out_specs=x_spec,
        scratch_shapes=[pltpu.VMEM((n_slabs, tm, LANES), F32), pltpu.VMEM((n_slabs, tm, LANES), F32),
                        pltpu.VMEM((tm, LANES), F32), pltpu.VMEM((tm, LANES), F32)],
        compiler_params=_cparams(("parallel", "parallel")),
        name="ev_tail",
    )(*outs, *stats, yb, jnp.asarray(ex, BF16), w_bf16, x, *tail_args)


def _od_tail_kernel(yc_ref, yd_ref, w_ref, x_ref, gxa_ref, wq_ref, kv_ref, wo_ref, out_ref):
    half = yc_ref.shape[-1]
    x0 = x_ref[...] + _dot(yc_ref[...], w_ref[0:half, :]) + _dot(yd_ref[...], w_ref[half:, :])
    out_ref[...] = _xattn_block(x0, gxa_ref, wq_ref, kv_ref, wo_ref)


def _od_tail(x, yc_tm, yd, w_bf16, tail):
    bn, s_len, d = x.shape
    tm = min(ROW_TILE_WIDE, s_len)
    c = yd.shape[-1]
    x_spec = pl.BlockSpec((None, tm, d), lambda b, i: (b, i, 0))
    tail_specs, tail_args = _xattn_operands(tail, d)
    return pl.pallas_call(
        _od_tail_kernel,
        out_shape=jax.ShapeDtypeStruct(x.shape, F32),
        grid=(bn, s_len // tm),
        in_specs=[pl.BlockSpec((tm, c), lambda b, i: (i, b)),
                  pl.BlockSpec((None, tm, c), lambda b, i: (b, i, 0)),
                  _resident((2 * c, d), lambda b, i: (0, 0)),
                  x_spec] + tail_specs,
        out_specs=x_spec,
        compiler_params=_cparams(("parallel", "parallel")),
        name="od_tail",
    )(yc_tm, yd, w_bf16, x, *tail_args)


def _norm_matmul_kernel(x_ref, g_ref, w_ref, o_ref):
    o_ref[...] = _dot(_rms(x_ref[...], g_ref[...]).astype(BF16), w_ref[...]).astype(o_ref.dtype)


def _mem_kv(mem, g, w_bf16):
    bn, m_len, d = mem.shape
    n_out = w_bf16.shape[1]
    return pl.pallas_call(
        _norm_matmul_kernel,
        out_shape=jax.ShapeDtypeStruct((bn, m_len, n_out), BF16),
        grid=(bn,),
        in_specs=[pl.BlockSpec((None, m_len, d), lambda b: (b, 0, 0)),
                  pl.BlockSpec((1, d), lambda b: (0, 0)),
                  pl.BlockSpec((d, n_out), lambda b: (0, 0))],
        out_specs=pl.BlockSpec((None, m_len, n_out), lambda b: (b, 0, 0)),
        compiler_params=_cparams(("parallel",)),
        name="mem_kv",
    )(mem, g.reshape(1, d), w_bf16)


def _od_proj_kernel(x_ref, g_ref, w_ref, ug_ref, *rest):
    hg_refs, slab = rest[:-1], rest[-1]
    bn, tq, d = x_ref.shape
    h = _rms(x_ref[...].reshape(bn * tq, d), g_ref[...]).astype(BF16)
    n_ug = 2 * LRU_W
    n_slabs = GROUP_W // LANES
    for c in range(n_ug // GROUP_W):
        acc = _dot(h, w_ref[:, c * GROUP_W:(c + 1) * GROUP_W])
        for b in range(bn):
            for j in range(n_slabs):
                slab[j, pl.ds(b, tq, stride=bn), :] = acc[b * tq:(b + 1) * tq, j * LANES:(j + 1) * LANES]
        for j in range(n_slabs):
            ug_ref[:, c * GROUP_W + j * LANES:c * GROUP_W + (j + 1) * LANES] = slab[j]
    for c, hg_ref in enumerate(hg_refs):
        acc = _dot(h, w_ref[:, n_ug + c * GROUP_W:n_ug + (c + 1) * GROUP_W])
        hg_ref[...] = acc.reshape(bn, tq, GROUP_W).astype(hg_ref.dtype)


HG_INPUT_DTYPES = (("q", BF16), ("f_fw", F32), ("f_bw", F32), ("i_in", BF16), ("g_out", BF16))


def _od_proj(x, g, w_bf16):
    bn, s_len, d = x.shape
    tq = min(ROW_TILE_WIDE // bn, s_len)
    n_ug = 2 * LRU_W
    n_hg = w_bf16.shape[1] - n_ug
    assert n_hg == len(HG_INPUT_DTYPES) * GROUP_W
    hg_spec = pl.BlockSpec((bn, tq, GROUP_W), lambda i: (0, i, 0))
    ug, *hg = pl.pallas_call(
        _od_proj_kernel,
        out_shape=[jax.ShapeDtypeStruct((s_len * bn, n_ug), F32)]
        + [jax.ShapeDtypeStruct((bn, s_len, GROUP_W), dt) for _, dt in HG_INPUT_DTYPES],
        grid=(s_len // tq,),
        in_specs=[pl.BlockSpec((bn, tq, d), lambda i: (0, i, 0)),
                  pl.BlockSpec((1, d), lambda i: (0, 0)),
                  _resident((d, n_ug + n_hg), lambda i: (0, 0))],
        out_specs=[pl.BlockSpec((tq * bn, n_ug), lambda i: (i, 0))] + [hg_spec] * len(HG_INPUT_DTYPES),
        scratch_shapes=[pltpu.VMEM((GROUP_W // LANES, tq * bn, LANES), F32)],
        compiler_params=_cparams(("parallel",)),
        name="od_proj",
    )(x, g.reshape(1, d), w_bf16)
    return ug.reshape(s_len, bn, n_ug), dict(zip([n for n, _ in HG_INPUT_DTYPES], hg))


def _gelu_tanh(x):
    return 0.5 * x * (1.0 + jnp.tanh(math.sqrt(2.0 / math.pi) * (x + 0.044715 * (x * x * x))))


def _rglru_kernel(*refs, reverse, n_chunks):
    if reverse:
        (ul_ref, u_ref, ur_ref, gate_ref, hf_ref, cw_ref, cb_ref, wa_ref, ba_ref, wx_ref, bx_ref, sp_ref,
         out_ref, ext, a_s, b_s, h_s, carry) = refs
    else:
        (ul_ref, u_ref, ur_ref, cw_ref, cb_ref, wa_ref, ba_ref, wx_ref, bx_ref, sp_ref,
         out_ref, ext, a_s, b_s, carry) = refs
        h_s = out_ref
    step = pl.program_id(0)
    chunk = (n_chunks - 1 - step) if reverse else step
    t_len, bn, c = u_ref.shape

    @pl.when(step == 0)
    def _():
        carry[...] = jnp.zeros_like(carry)

    ext[0:2] = jnp.where(chunk == 0, 0.0, ul_ref[...])
    ext[2:t_len + 2] = u_ref[...]
    ext[t_len + 2:t_len + 3] = jnp.where(chunk == n_chunks - 1, 0.0, ur_ref[...])
    uc = cb_ref[...].reshape(1, 1, c)
    for j in range(4):
        uc = uc + cw_ref[j:j + 1, :].reshape(1, 1, c) * ext[j:j + t_len]

    u2 = uc.reshape(t_len * bn, c)
    ub = u2.astype(BF16)
    half = c // 2

    def gate_tanh(w_ref, b_ref):
        z = jnp.concatenate([_dot(ub[:, :half], w_ref[0]), _dot(ub[:, half:], w_ref[1])], axis=-1)
        return jnp.tanh(z + b_ref[...])

    tr = gate_tanh(wa_ref, ba_ref)
    ti = gate_tanh(wx_ref, bx_ref)
    a = jnp.exp2(sp_ref[...] * tr + sp_ref[...])
    b = jnp.sqrt(1.0 - a * a) * ((0.5 * ti + 0.5) * u2)
    a_s[...] = a.reshape(t_len, bn, c)
    b_s[...] = b.reshape(t_len, bn, c)

    def body(k, h):
        t = (t_len - 1 - k) if reverse else k
        h = a_s[t] * h + b_s[t]
        h_s[t] = h
        return h

    carry[...] = lax.fori_loop(0, t_len, body, carry[...], unroll=8)

    if reverse:
        out_ref[...] = ((hf_ref[...] + h_s[...]) * _gelu_tanh(gate_ref[...])).astype(out_ref.dtype)


def _rglru(ug, hf, conv_w, conv_b, wa, ba, wx, bx, lam, bn, *, reverse):
    s_len = ug.shape[0]
    c = LRU_W
    ug3 = ug
    t_len = min(LRU_T, s_len)
    n_chunks = s_len // t_len

    def ck(i):
        return (n_chunks - 1 - i) if reverse else i

    half = c // 2
    nb = LRU_BLOCKS // 2

    def dense_halves(w):
        w = 0.5 * w.astype(F32).reshape(2, nb, c // LRU_BLOCKS, c // LRU_BLOCKS)
        eye = jnp.eye(nb, dtype=F32)
        return jnp.einsum('gnij,nm->gnimj', w, eye).reshape(2, half, half).astype(BF16)

    decay_scale = ((-0.5 * LRU_C * LOG2E) * jax.nn.softplus(-lam.astype(F32))).reshape(1, c)
    vec = pl.BlockSpec((1, c), lambda i: (0, 0))
    wspec = pl.BlockSpec((2, half, half), lambda i: (0, 0, 0))
    blk = pl.BlockSpec((t_len, bn, c), lambda i: (ck(i), 0, 0))
    in_specs = [
        pl.BlockSpec((2, bn, c), lambda i: (jnp.maximum(ck(i) * (t_len // 2) - 1, 0), 0, 0)),
        blk,
        pl.BlockSpec((1, bn, c), lambda i: (jnp.minimum((ck(i) + 1) * t_len, s_len - 1), 0, 0)),
    ]
    args = [ug3, ug3, ug3]
    scratch = [pltpu.VMEM((t_len + 3, bn, c), F32), pltpu.VMEM((t_len, bn, c), F32),
               pltpu.VMEM((t_len, bn, c), F32)]
    if reverse:
        in_specs += [pl.BlockSpec((t_len, bn, c), lambda i: (ck(i), 0, 1)), blk]
        args += [ug3, hf]
        scratch.append(pltpu.VMEM((t_len, bn, c), F32))
    scratch.append(pltpu.VMEM((bn, c), F32))
    in_specs += [pl.BlockSpec((4, c), lambda i: (0, 0)), vec, wspec, vec, wspec, vec, vec]
    args += [conv_w.astype(F32), conv_b.astype(F32).reshape(1, c), dense_halves(wa),
             0.5 * ba.astype(F32).reshape(1, c), dense_halves(wx), 0.5 * bx.astype(F32).reshape(1, c),
             decay_scale]
    return pl.pallas_call(
        functools.partial(_rglru_kernel, reverse=reverse, n_chunks=n_chunks),
        out_shape=jax.ShapeDtypeStruct((s_len, bn, c), BF16 if reverse else F32),
        grid=(n_chunks,),
        in_specs=in_specs,
        out_specs=blk,
        scratch_shapes=scratch,
        compiler_params=_cparams(("arbitrary",)),
        name="rglru_bw" if reverse else "rglru_fw",
    )(*args)


def _hg_matrices(c, reverse):
    n_levels = int(math.log2(c))
    t = np.arange(c)[:, None]
    r = np.arange(c)[None, :]
    mats, masks, upper = [], [], []
    if not reverse:
        mats.append(r <= t)
        mats.append(r > t)
    else:
        mats.append(r >= t)
        mats.append(r < t)
    for lev in range(n_levels):
        half = c >> (lev + 1)
        parent = 2 * half
        start = (np.arange(c) // parent) * parent
        mid = (start + half)[:, None]
        later = (np.arange(c) % parent >= half)[:, None]
        if not reverse:
            m = np.where(later, (r >= mid) & (r <= t), (r > t) & (r < mid))
            is_q = later
        else:
            m = np.where(later, (r >= mid) & (r < t), (r >= t) & (r < mid))
            is_q = ~later
        mats.append(m)
        same_parent = (start[:, None] == start[None, :])
        masks.append(same_parent & is_q & (~is_q).T)
        upper.append(np.broadcast_to(is_q, (c, HG_DK)))
    mat = np.concatenate(mats, axis=0).astype(np.float32)
    return (np.concatenate([mat, mat], axis=1), np.stack(masks).astype(np.float32),
            np.stack(upper).astype(np.float32))


def _hgrn_kernel(*refs, reverse, n_steps, n_levels):
    if reverse:
        (q_ref, f_ref, v_ref, go_ref, of_ref, lb_ref, gn_ref, mat_ref, mask_ref, isq_ref, out_ref,
         state, x_scr, oi_scr, aux_scr, dec_scr, kk_scr) = refs
    else:
        (q_ref, f_ref, v_ref, lb_ref, mat_ref, mask_ref, isq_ref, out_ref,
         state, x_scr, oi_scr, aux_scr, dec_scr, kk_scr) = refs
    step = pl.program_id(1)
    c = HG_CHUNK
    t_len = q_ref.shape[0]
    n_chunks = t_len // c

    @pl.when(step == 0)
    def _():
        state[...] = jnp.zeros_like(state)

    def chunk_body(ci, carry):
        ch = (n_chunks - 1 - ci) if reverse else ci
        rows = slice(ch * c, (ch + 1) * c)
        edge = (c - 1) if not reverse else 0
        for hp in range(HG_HEADS // 2):
            cols2 = slice(2 * hp * HG_DK, 2 * (hp + 1) * HG_DK)
            fl = f_ref[rows, cols2]
            e = jnp.exp(-jnp.abs(fl))
            r = 1.0 / (1.0 + e)
            er = e * r
            pos = fl > 0.0
            lb = lb_ref[0:1, cols2]
            oml = lb_ref[1:2, cols2]
            g2 = jnp.log2(lb + oml * jnp.where(pos, r, er))
            kk_scr[hp] = oml * jnp.where(pos, er, r)
            g_hi = g2.astype(BF16)
            g_lo = (g2 - g_hi.astype(F32)).astype(BF16)
            dec_scr[hp] = jnp.exp2(_dot(mat_ref[...], jnp.concatenate([g_hi, g_lo], axis=0)))

        for hd in range(HG_HEADS):
            cols = slice(hd * HG_DK, (hd + 1) * HG_DK)
            pcols = slice((hd % 2) * HG_DK, (hd % 2 + 1) * HG_DK)
            kk = kk_scr[hd // 2, :, pcols]
            dec = dec_scr.at[hd // 2]
            qv = q_ref[rows, cols].astype(F32)
            qq = qv * jax.nn.sigmoid(qv)
            qb, kb = qq.astype(BF16), kk.astype(BF16)
            d_in = dec[0:c, pcols]
            x_scr[hd, n_levels] = (qq * d_in).astype(BF16)
            x_scr[hd, n_levels + 1] = (kk * dec[c:2 * c, pcols]).astype(BF16)
            aux_scr[hd, 0] = jnp.broadcast_to(d_in[edge:edge + 1, :], (c, HG_DK))
            aux_scr[hd, 1] = jnp.broadcast_to(jnp.sum(qq * kk, axis=-1, keepdims=True), (c, HG_DK))
            for lev in range(n_levels):
                half = c >> (lev + 1)
                if half >= SUBLANES_BF16:
                    first_is_q = bool(reverse)
                    base = jnp.concatenate(
                        [(qb if (blk % 2 == 1) != first_is_q else kb)[blk * half:(blk + 1) * half]
                         for blk in range(c // half)], axis=0)
                else:
                    base = jnp.where(isq_ref[lev] > 0.5, qb, kb)
                x_scr[hd, lev] = base * dec[(2 + lev) * c:(3 + lev) * c, pcols].astype(BF16)

        for hd in range(HG_HEADS):
            cols = slice(hd * HG_DK, (hd + 1) * HG_DK)
            att = None
            for lev in range(n_levels):
                xl = x_scr[hd, lev]
                term = mask_ref[lev] * _nt_dot(xl, xl)
                att = term if att is None else att + term
            st = state[hd]
            oi_scr[hd] = _nt_dot(x_scr[hd, n_levels], st.astype(BF16))
            vt = v_ref[rows, cols].astype(F32).T.astype(BF16)
            state[hd] = st * aux_scr[hd, 0] + _dot(vt, x_scr[hd, n_levels + 1])
            x_scr[hd, 0] = att.astype(BF16)

        for hd in range(HG_HEADS):
            cols = slice(hd * HG_DK, (hd + 1) * HG_DK)
            vb = v_ref[rows, cols]
            o = oi_scr[hd] + _dot(x_scr[hd, 0], vb) + aux_scr[hd, 1] * vb.astype(F32)

            if reverse:
                o = o + of_ref[rows, cols]
                o = o * lax.rsqrt(jnp.mean(o * o, axis=-1, keepdims=True) + EPS) * gn_ref[...]
                gv = go_ref[rows, cols].astype(F32)
                out_ref[rows, cols] = (o * (gv * jax.nn.sigmoid(gv))).astype(out_ref.dtype)
            else:
                out_ref[rows, cols] = o
        return carry

    for ci in range(n_chunks):
        chunk_body(ci, 0)


def _hgrn(hg, o_fw, lb, gnorm_g, *, reverse):
    bn, s_len, _ = hg["q"].shape
    w = HG_HEADS * HG_DK
    t_len = min(HG_T, s_len)
    n_steps = s_len // t_len
    mats, masks, isq = _hg_matrices(HG_CHUNK, reverse)
    n_levels = masks.shape[0]
    lbf = lb.astype(F32)
    lb_tab = jnp.stack([lbf, 1.0 - lbf])

    blk = pl.BlockSpec((None, t_len, w), lambda b, i: (b, (n_steps - 1 - i) if reverse else i, 0))

    def const(shape):
        return pl.BlockSpec(shape, lambda b, i: (0,) * len(shape))

    in_specs = [blk, blk, blk]
    args = [hg["q"], hg["f_bw" if reverse else "f_fw"], hg["i_in"]]
    if reverse:
        in_specs += [blk, blk]
        args += [hg["g_out"], o_fw]
    in_specs.append(const((2, w)))
    args.append(lb_tab)
    if reverse:
        in_specs.append(const((1, HG_DK)))
        args.append(gnorm_g.astype(F32).reshape(1, HG_DK))
    in_specs += [const(mats.shape), const(masks.shape), const(isq.shape)]
    args += [jnp.asarray(mats, BF16), jnp.asarray(masks, F32), jnp.asarray(isq, BF16)]
    return pl.pallas_call(
        functools.partial(_hgrn_kernel, reverse=reverse, n_steps=n_steps, n_levels=n_levels),
        out_shape=jax.ShapeDtypeStruct((bn, s_len, w), BF16 if reverse else F32),
        grid=(bn, n_steps),
        in_specs=in_specs,
        out_specs=blk,
        scratch_shapes=[pltpu.VMEM((HG_HEADS, HG_DK, HG_DK), F32),
                        pltpu.VMEM((HG_HEADS, n_levels + 2, HG_CHUNK, HG_DK), BF16),
                        pltpu.VMEM((HG_HEADS, HG_CHUNK, HG_DK), F32),
                        pltpu.VMEM((HG_HEADS, 2, HG_CHUNK, HG_DK), F32),
                        pltpu.VMEM((HG_HEADS // 2, (n_levels + 2) * HG_CHUNK, 2 * HG_DK), F32),
                        pltpu.VMEM((HG_HEADS // 2, HG_CHUNK, 2 * HG_DK), F32)],
        compiler_params=_cparams(("parallel", "arbitrary")),
        name="hgrn_bw" if reverse else "hgrn_fw",
    )(*args)


def _even_layer(x, g, w_in, w_out, rpb, tail):
    qkv, view4, view16 = _ev_proj(x, g, w_in.astype(BF16))
    outs, stats = zip(*[_dilated_branch(view, dil) for dil, view in zip(DILATIONS, (qkv, view4, view16))])
    yb = _neighborhood(qkv, rpb)
    return _ev_tail(x, outs, stats, yb, w_out.astype(BF16), tail)


def _odd_layer(x, g, w_in, w_out, conv_w, conv_b, wa, ba, wx, bx, lam, lb_f, lb_b, gnorm_g, tail):
    bn, s_len, _ = x.shape
    ug, hg = _od_proj(x, g, w_in.astype(BF16))
    h_fw = _rglru(ug, None, conv_w, conv_b, wa[0], ba[0], wx[0], bx[0], lam[0], bn, reverse=False)
    yc = _rglru(ug, h_fw, conv_w, conv_b, wa[1], ba[1], wx[1], bx[1], lam[1], bn, reverse=True)
    o_fw = _hgrn(hg, None, lb_f, gnorm_g, reverse=False)
    yd = _hgrn(hg, o_fw, lb_b, gnorm_g, reverse=True)
    return _od_tail(x, yc.reshape(s_len, bn * LRU_W), yd, w_out.astype(BF16), tail)


def kernel(x, mem, norm_mix_g, norm_xa_g, norm_mem_g, norm_mlp_g, final_norm_g, ev_w_in, ev_w_out, na_rpb,
           od_w_in, od_w_out, conv_w, conv_b, lru_wa, lru_ba, lru_wx, lru_bx, lru_lambda, hgrn_lb_logits,
           hgrn_norm_g, xa_wq, xa_wkv, xa_wo, mlp_w1, mlp_w2):
    depth = norm_mix_g.shape[0]
    p_lb = jax.nn.softmax(hgrn_lb_logits.astype(F32), axis=0)
    lower_bounds = jnp.cumsum(p_lb, axis=0) - p_lb[0:1]
    for layer in range(depth):
        tail = dict(g_xa=norm_xa_g[layer], wq=xa_wq[layer].astype(BF16),
                    kv=_mem_kv(mem, norm_mem_g[layer], xa_wkv[layer].astype(BF16)),
                    wo=xa_wo[layer].astype(BF16), g_mlp=norm_mlp_g[layer],
                    w1=mlp_w1[layer].astype(BF16), w2=mlp_w2[layer].astype(BF16),
                    g_final=final_norm_g if layer == depth - 1 else None)
        if layer % 2 == 0:
            e = layer // 2
            x = _even_layer(x, norm_mix_g[layer], ev_w_in[e], ev_w_out[e], na_rpb[e], tail)
        else:
            o = layer // 2
            x = _odd_layer(x, norm_mix_g[layer], od_w_in[o], od_w_out[o], conv_w[o], conv_b[o],
                           lru_wa[o], lru_ba[o], lru_wx[o], lru_bx[o], lru_lambda[o],
                           lower_bounds[layer, 0], lower_bounds[layer, 1], hgrn_norm_g[o], tail)
        x = _mlp(x, tail)
    return x
```

```python
import functools
import math

import jax
import jax.numpy as jnp
import numpy as np
from jax import lax
from jax.experimental import pallas as pl
from jax.experimental.pallas import tpu as pltpu

F32 = jnp.float32
BF16 = jnp.bfloat16

D_MODEL = 1024
HEAD_DIM = 64
ROT_DIM = 16
ROPE_THETA = 500000.0
N_HEADS_A = 8
N_HEADS_B = 8
GROUP_W = N_HEADS_A * HEAD_DIM
DILATIONS = (1, 4, 16)
WIN_HALF = 64
GRID_W = 64
NA_ROWS = 8
NA_COLS = 16
LRU_W = 512
LRU_BLOCKS = 8
LRU_C = 8.0
HG_HEADS = 4
HG_DK = 128
XA_HEADS = 4
XA_DH = 256
D_FF = 4096
EPS = 1e-6
LOG2E = math.log2(math.e)
F32_MIN_EXPONENT = -126.0

LANES = 128
SUBLANES_BF16 = 16
V7X_VMEM_LIMIT_BYTES = 56 * 1024 * 1024

ROW_TILE = 512
ROW_TILE_WIDE = 1024
ATT_Q_BLOCK = 512
ATT_SUB = 128
NA_ROW_GROUP = 8
LRU_T = 128
HG_CHUNK = 128
HG_T = 1024


def _cparams(sem):
    return pltpu.CompilerParams(dimension_semantics=sem, vmem_limit_bytes=V7X_VMEM_LIMIT_BYTES)


def _resident(shape, index_map):
    return pl.BlockSpec(shape, index_map, pipeline_mode=pl.Buffered(1))


def _rms(x, g):
    return x * lax.rsqrt(jnp.mean(x * x, axis=-1, keepdims=True) + EPS) * g


def _nt_dot(a, b):
    return lax.dot_general(a, b, (((1,), (1,)), ((), ())), preferred_element_type=F32)


def _dot(a, b):
    return jnp.dot(a, b, preferred_element_type=F32)


def _head_selectors():
    lane = lax.broadcasted_iota(jnp.int32, (1, LANES), 1)
    low = (lane < HEAD_DIM).astype(F32)
    return low.astype(BF16), (1.0 - low).astype(BF16)


def _ev_proj_kernel(x_ref, g_ref, w_ref, rc_ref, rs1_ref, rs2_ref, o_ref, o4_ref, o16_ref, slab):
    h = _rms(x_ref[...], g_ref[...]).astype(BF16)
    scale = HEAD_DIM ** -0.5 * LOG2E
    tm = x_ref.shape[0]
    n_slabs = GROUP_W // LANES
    for c in range(6):
        acc = _dot(h, w_ref[:, c * GROUP_W:(c + 1) * GROUP_W])
        if c in (0, 1):
            rc, rs1, rs2 = rc_ref[...], rs1_ref[...], rs2_ref[...]
            parts = []
            for j in range(n_slabs):
                t = acc[:, j * LANES:(j + 1) * LANES]
                t = t * rc + pltpu.roll(t, LANES - ROT_DIM // 2, 1) * rs1 + pltpu.roll(t, ROT_DIM // 2, 1) * rs2
                parts.append(t)
            acc = jnp.concatenate(parts, axis=-1)
        if c in (0, 3):
            acc = acc * scale
        o_ref[:, c * GROUP_W:(c + 1) * GROUP_W] = acc.astype(BF16)
        if c < 3:
            for j in range(n_slabs):
                slab[j] = acc[:, j * LANES:(j + 1) * LANES]
            for dil, dst in ((DILATIONS[1], o4_ref), (DILATIONS[2], o16_ref)):
                for r in range(dil):
                    for j in range(n_slabs):
                        col = r * 3 * GROUP_W + c * GROUP_W + j * LANES
                        dst[:, col:col + LANES] = slab[j, pl.ds(r, tm // dil, stride=dil), :].astype(BF16)


def _rope_tables(s_len):
    half = ROT_DIM // 2
    inv = jnp.asarray(ROPE_THETA ** (-np.arange(half) * 2.0 / ROT_DIM), F32)
    ang = jnp.arange(s_len, dtype=F32)[:, None] * inv[None, :]
    cos, sin = jnp.cos(ang), jnp.sin(ang)
    ones = jnp.ones((s_len, HEAD_DIM - ROT_DIM), F32)
    zeros = jnp.zeros((s_len, HEAD_DIM - ROT_DIM), F32)
    zh = jnp.zeros((s_len, half), F32)
    rc = jnp.concatenate([cos, cos, ones], axis=-1)
    rs1 = jnp.concatenate([-sin, zh, zeros], axis=-1)
    rs2 = jnp.concatenate([zh, sin, zeros], axis=-1)
    rep = LANES // HEAD_DIM
    return jnp.tile(rc, (1, rep)), jnp.tile(rs1, (1, rep)), jnp.tile(rs2, (1, rep))


def _ev_proj(x, g, w_bf16):
    bn, s_len, d = x.shape
    tm = min(ROW_TILE_WIDE, s_len)
    n_out = w_bf16.shape[1]
    rc, rs1, rs2 = _rope_tables(s_len)
    tbl_spec = pl.BlockSpec((tm, LANES), lambda b, i: (i, 0))
    wa = 3 * GROUP_W
    d4, d16 = DILATIONS[1], DILATIONS[2]
    return pl.pallas_call(
        _ev_proj_kernel,
        out_shape=(jax.ShapeDtypeStruct((bn, s_len, n_out), BF16),
                   jax.ShapeDtypeStruct((bn, s_len // d4, d4 * wa), BF16),
                   jax.ShapeDtypeStruct((bn, s_len // d16, d16 * wa), BF16)),
        grid=(bn, s_len // tm),
        in_specs=[
            pl.BlockSpec((None, tm, d), lambda b, i: (b, i, 0)),
            pl.BlockSpec((1, d), lambda b, i: (0, 0)),
            _resident((d, n_out), lambda b, i: (0, 0)),
            tbl_spec, tbl_spec, tbl_spec,
        ],
        out_specs=(pl.BlockSpec((None, tm, n_out), lambda b, i: (b, i, 0)),
                   pl.BlockSpec((None, tm // d4, d4 * wa), lambda b, i: (b, i, 0)),
                   pl.BlockSpec((None, tm // d16, d16 * wa), lambda b, i: (b, i, 0))),
        scratch_shapes=[pltpu.VMEM((GROUP_W // LANES, tm, LANES), F32)],
        compiler_params=_cparams(("parallel", "parallel")),
        name="ev_proj",
    )(x, g.reshape(1, d), w_bf16, rc, rs1, rs2)


def _dilated_kernel(q_ref, kl_ref, kc_ref, kr_ref, vl_ref, vc_ref, vr_ref, o_ref, stat_ref,
                    kext, vext, s_scr, p_scr, *, lq, l_total):
    i = pl.program_id(2)
    kext[0:WIN_HALF, :] = kl_ref[...]
    kext[WIN_HALF:WIN_HALF + lq, :] = kc_ref[...]
    kext[WIN_HALF + lq:, :] = kr_ref[...]
    vext[0:WIN_HALF, :] = vl_ref[...]
    vext[WIN_HALF:WIN_HALF + lq, :] = vc_ref[...]
    vext[WIN_HALF + lq:, :] = vr_ref[...]

    wk = ATT_SUB + 2 * WIN_HALF
    qi = lax.broadcasted_iota(jnp.int32, (ATT_SUB, wk), 0)
    ci = lax.broadcasted_iota(jnp.int32, (ATT_SUB, wk), 1)
    band_bias = jnp.where((ci - qi >= 0) & (ci - qi <= 2 * WIN_HALF), 0.0, -jnp.inf).astype(F32)
    crow = lax.broadcasted_iota(jnp.int32, (1, wk), 1)
    lane = lax.broadcasted_iota(jnp.int32, (ATT_SUB, LANES), 1)
    low_half = lane < HEAD_DIM
    sel_lo, sel_hi = _head_selectors()
    n_sub = lq // ATT_SUB
    n_hp = GROUP_W // LANES

    for j in range(n_sub):
        base = i * lq + j * ATT_SUB - WIN_HALF
        in_seq = (crow + base >= 0) & (crow + base < l_total)
        bias = band_bias + jnp.where(in_seq, 0.0, -jnp.inf).astype(F32)
        bias2 = jnp.concatenate([bias, bias], axis=0)
        rows = slice(j * ATT_SUB, (j + 1) * ATT_SUB)
        wrows = slice(j * ATT_SUB, j * ATT_SUB + wk)
        for hp in range(n_hp):
            cols = slice(hp * LANES, (hp + 1) * LANES)
            qp = q_ref[rows, cols]
            q2 = jnp.concatenate([qp * sel_lo, qp * sel_hi], axis=0)
            s_scr[j * n_hp + hp] = _nt_dot(q2, kext[wrows, cols]) + bias2
    for j in range(n_sub):
        stat = jnp.ones((ATT_SUB, LANES), F32)
        for hp in range(n_hp):
            t = j * n_hp + hp
            s = s_scr[t]
            m = jnp.max(s, axis=-1, keepdims=True)
            p = jnp.exp2(s - m)
            den = jnp.sum(p, axis=-1, keepdims=True)
            p_scr[t] = p.astype(BF16)
            stat = jnp.where(lane == 2 * hp, m[0:ATT_SUB], jnp.where(lane == 2 * hp + 1, m[ATT_SUB:], stat))
            stat = jnp.where(lane == N_HEADS_A + 2 * hp, den[0:ATT_SUB],
                             jnp.where(lane == N_HEADS_A + 2 * hp + 1, den[ATT_SUB:], stat))
        stat_ref[j * ATT_SUB:(j + 1) * ATT_SUB, :] = stat
    for j in range(n_sub):
        rows = slice(j * ATT_SUB, (j + 1) * ATT_SUB)
        wrows = slice(j * ATT_SUB, j * ATT_SUB + wk)
        for hp in range(n_hp):
            cols = slice(hp * LANES, (hp + 1) * LANES)
            o = _dot(p_scr[j * n_hp + hp], vext[wrows, cols])
            o_ref[rows, cols] = jnp.where(low_half, o[0:ATT_SUB], o[ATT_SUB:]).astype(BF16)


def _dilated_branch(view, dil):
    bn, l_total, width = view.shape
    n_groups = width // (dil * GROUP_W)
    lq = min(ATT_Q_BLOCK, l_total)
    nblk = l_total // lq
    per = lq // WIN_HALF
    n_halo = l_total // WIN_HALF
    n_units = (lq // ATT_SUB) * (GROUP_W // LANES)

    def cur(g):
        return pl.BlockSpec((None, lq, GROUP_W), lambda b, r, i: (b, i, r * n_groups + g))

    def left(g):
        return pl.BlockSpec((None, WIN_HALF, GROUP_W),
                            lambda b, r, i: (b, jnp.maximum(i * per - 1, 0), r * n_groups + g))

    def right(g):
        return pl.BlockSpec((None, WIN_HALF, GROUP_W),
                            lambda b, r, i: (b, jnp.minimum((i + 1) * per, n_halo - 1), r * n_groups + g))

    stat_shape = jax.ShapeDtypeStruct((bn, l_total, dil * LANES), F32)
    stat_spec = pl.BlockSpec((None, lq, LANES), lambda b, r, i: (b, i, r))
    return pl.pallas_call(
        functools.partial(_dilated_kernel, lq=lq, l_total=l_total),
        out_shape=(jax.ShapeDtypeStruct((bn, l_total, dil * GROUP_W), BF16), stat_shape),
        grid=(bn, dil, nblk),
        in_specs=[cur(0), left(1), cur(1), right(1), left(2), cur(2), right(2)],
        out_specs=(pl.BlockSpec((None, lq, GROUP_W), lambda b, r, i: (b, i, r)), stat_spec),
        scratch_shapes=[pltpu.VMEM((lq + 2 * WIN_HALF, GROUP_W), BF16),
                        pltpu.VMEM((lq + 2 * WIN_HALF, GROUP_W), BF16),
                        pltpu.VMEM((n_units, 2 * ATT_SUB, ATT_SUB + 2 * WIN_HALF), F32),
                        pltpu.VMEM((n_units, 2 * ATT_SUB, ATT_SUB + 2 * WIN_HALF), BF16)],
        compiler_params=_cparams(("parallel", "parallel", "parallel")),
        name=f"dilated_attn_d{dil}",
    )(view, view, view, view, view, view, view)


NA_QT = NA_ROW_GROUP * GRID_W
NA_KT = 2 * NA_QT
NA_KW = NA_ROWS * GRID_W


def _na_tables(rpb):
    rpb = rpb.astype(F32)
    nh = rpb.shape[0]
    ext = GRID_W - NA_COLS
    padded = jnp.concatenate([jnp.repeat(rpb[..., :1], ext, axis=-1), rpb,
                              jnp.repeat(rpb[..., -1:], ext, axis=-1)], axis=-1)
    skew = jnp.tile(padded, (1, 1, GRID_W + 1))[..., :2 * GRID_W * GRID_W]
    t1 = skew.reshape(nh, -1, GRID_W, 2 * GRID_W)[..., ::-1, :GRID_W]
    qc = np.arange(GRID_W)[:, None]
    kc = np.arange(GRID_W)[None, :]
    c0 = np.clip(qc - NA_COLS // 2, 0, GRID_W - NA_COLS)
    t1 = jnp.where((kc >= c0) & (kc < c0 + NA_COLS), t1 * LOG2E, -jnp.inf)
    per_delta = [jnp.transpose(t1[:, NA_ROWS - 1 - delta:2 * NA_ROWS - 1 - delta], (0, 2, 1, 3))
                 .reshape(nh, GRID_W, NA_KW) for delta in range(NA_ROWS)]
    return jnp.stack(per_delta).reshape(NA_ROWS, nh // 2, 2 * GRID_W, NA_KW)


def _na_kernel(q_ref, k_ref, v_ref, tb_ref, o_ref, s_scr, p_scr, inv_scr, *, rows):
    i0 = pl.program_id(1) * NA_ROW_GROUP
    w0 = jnp.clip(i0 - NA_ROWS // 2, 0, rows - 2 * NA_ROW_GROUP)
    lane = lax.broadcasted_iota(jnp.int32, (GRID_W, LANES), 1)
    low_half = lane < HEAD_DIM
    sel_lo, sel_hi = _head_selectors()
    n_hp = GROUP_W // LANES

    def key_rows(a):
        r0 = jnp.clip(i0 + a - NA_ROWS // 2, 0, rows - NA_ROWS)
        return pl.ds(pl.multiple_of((r0 - w0) * GRID_W, GRID_W), NA_KW), i0 + a - r0

    for a in range(NA_ROW_GROUP):
        krows, delta = key_rows(a)
        qrows = slice(a * GRID_W, (a + 1) * GRID_W)
        for hp in range(n_hp):
            cols = slice(hp * LANES, (hp + 1) * LANES)
            qp = q_ref[qrows, cols]
            q2 = jnp.concatenate([qp * sel_lo, qp * sel_hi], axis=0)
            s_scr[a * n_hp + hp] = _nt_dot(q2, k_ref[0, krows, cols]) + tb_ref[delta, hp]
    for t in range(NA_ROW_GROUP * n_hp):
        s = s_scr[t]
        p = jnp.exp2(s - jnp.max(s, axis=-1, keepdims=True))
        inv_scr[t] = jnp.broadcast_to(1.0 / jnp.sum(p, axis=-1, keepdims=True), (2 * GRID_W, LANES))
        p_scr[t] = p.astype(BF16)
    for a in range(NA_ROW_GROUP):
        krows, _ = key_rows(a)
        qrows = slice(a * GRID_W, (a + 1) * GRID_W)
        for hp in range(n_hp):
            cols = slice(hp * LANES, (hp + 1) * LANES)
            t = a * n_hp + hp
            o = _dot(p_scr[t], v_ref[0, krows, cols]) * inv_scr[t]
            o_ref[qrows, cols] = jnp.where(low_half, o[0:GRID_W], o[GRID_W:]).astype(BF16)


def _neighborhood(qkv, rpb):
    bn, s_len, _ = qkv.shape
    rows = s_len // GRID_W
    n_groups = rows // NA_ROW_GROUP
    tables = _na_tables(rpb)

    def window(g):
        return pl.BlockSpec(
            (pl.Element(1), pl.Element(NA_KT), pl.Element(GROUP_W)),
            lambda b, i: (b, jnp.clip(i * NA_ROW_GROUP - NA_ROWS // 2, 0, rows - 2 * NA_ROW_GROUP) * GRID_W,
                          g * GROUP_W))

    return pl.pallas_call(
        functools.partial(_na_kernel, rows=rows),
        out_shape=jax.ShapeDtypeStruct((bn, s_len, GROUP_W), BF16),
        grid=(bn, n_groups),
        in_specs=[pl.BlockSpec((None, NA_QT, GROUP_W), lambda b, i: (b, i, 3)), window(4), window(5),
                  pl.BlockSpec(tables.shape, lambda b, i: (0, 0, 0, 0))],
        out_specs=pl.BlockSpec((None, NA_QT, GROUP_W), lambda b, i: (b, i, 0)),
        scratch_shapes=[pltpu.VMEM((NA_ROW_GROUP * GROUP_W // LANES, 2 * GRID_W, NA_KW), F32),
                        pltpu.VMEM((NA_ROW_GROUP * GROUP_W // LANES, 2 * GRID_W, NA_KW), BF16),
                        pltpu.VMEM((NA_ROW_GROUP * GROUP_W // LANES, 2 * GRID_W, LANES), F32)],
        compiler_params=_cparams(("parallel", "parallel")),
        name="neighborhood_attn",
    )(qkv, qkv, qkv, tables)


def _xattn_block(x, g_ref, wq_ref, kv_ref, wo_ref):
    q = (_dot(_rms(x, g_ref[...]).astype(BF16), wq_ref[...]) * (XA_DH ** -0.5 * LOG2E)).astype(BF16)
    heads = []
    for hd in range(XA_HEADS):
        cols = slice(hd * XA_DH, (hd + 1) * XA_DH)
        s = _nt_dot(q[:, cols], kv_ref[:, cols])
        m = jnp.max(s, axis=-1, keepdims=True)
        p = jnp.exp2(s - m)
        den = jnp.sum(p, axis=-1, keepdims=True)
        vcols = slice(D_MODEL + hd * XA_DH, D_MODEL + (hd + 1) * XA_DH)
        heads.append((_dot(p.astype(BF16), kv_ref[:, vcols]) * (1.0 / den)).astype(BF16))
    return x + _dot(jnp.concatenate(heads, axis=-1), wo_ref[...])


def _mlp_block(x, g_ref, w1_ref, w2_ref, gf_ref, final_norm):
    h = _rms(x, g_ref[...]).astype(BF16)
    acc = x
    for c in range(D_FF // D_MODEL):
        cols = slice(c * D_MODEL, (c + 1) * D_MODEL)
        a = jnp.maximum(_dot(h, w1_ref[:, cols]), 0.0)
        acc = acc + _dot((a * a).astype(BF16), w2_ref[cols, :])
    return _rms(acc, gf_ref[...]) if final_norm else acc


def _xattn_operands(tail, d):
    m_len = tail["kv"].shape[1]
    specs = [pl.BlockSpec((1, d), lambda b, i: (0, 0)), _resident((d, d), lambda b, i: (0, 0)),
             pl.BlockSpec((None, m_len, 2 * d), lambda b, i: (b, 0, 0)),
             _resident((d, d), lambda b, i: (0, 0))]
    return specs, [tail["g_xa"].reshape(1, d), tail["wq"], tail["kv"], tail["wo"]]


def _mlp_kernel(x_ref, g_ref, w1_ref, w2_ref, gf_ref, out_ref, *, final_norm):
    out_ref[...] = _mlp_block(x_ref[...], g_ref, w1_ref, w2_ref, gf_ref, final_norm)


def _mlp(x, tail):
    bn, s_len, d = x.shape
    tm = min(ROW_TILE_WIDE, s_len)
    final_norm = tail["g_final"] is not None
    gf = tail["g_final"] if final_norm else tail["g_mlp"]
    x_spec = pl.BlockSpec((None, tm, d), lambda b, i: (b, i, 0))
    vec = pl.BlockSpec((1, d), lambda b, i: (0, 0))
    return pl.pallas_call(
        functools.partial(_mlp_kernel, final_norm=final_norm),
        out_shape=jax.ShapeDtypeStruct(x.shape, F32),
        grid=(bn, s_len // tm),
        in_specs=[x_spec, vec, _resident((d, D_FF), lambda b, i: (0, 0)),
                  _resident((D_FF, d), lambda b, i: (0, 0)), vec],
        out_specs=x_spec,
        compiler_params=_cparams(("parallel", "parallel")),
        name="mlp_final" if final_norm else "mlp",
    )(x, tail["g_mlp"].reshape(1, d), tail["w1"], tail["w2"], gf.reshape(1, d))


def _ev_tail_kernel(o1, o2v, o3v, s1, s2v, s3v, yb_ref, ex_ref, w_ref, x_ref,
                    gxa_ref, wq_ref, kv_ref, wo_ref, out_ref, so2, so3, st2, st3):
    tm = x_ref.shape[0]
    n_slabs = GROUP_W // LANES
    for dil, ov, sv, so, st in ((DILATIONS[1], o2v, s2v, so2, st2), (DILATIONS[2], o3v, s3v, so3, st3)):
        for r in range(dil):
            dst = pl.ds(r, tm // dil, stride=dil)
            for j in range(n_slabs):
                col = r * GROUP_W + j * LANES
                so[j, dst, :] = ov[:, col:col + LANES].astype(F32)
            st[dst, :] = sv[:, r * LANES:(r + 1) * LANES]
    o2 = jnp.concatenate([so2[j] for j in range(n_slabs)], axis=-1)
    o3 = jnp.concatenate([so3[j] for j in range(n_slabs)], axis=-1)
    ma, mb, mc = s1[...], st2[...], st3[...]
    da, db, dc = [pltpu.roll(t, LANES - N_HEADS_A, 1) for t in (ma, mb, mc)]
    m = jnp.maximum(jnp.maximum(ma, mb), mc)
    ea, eb, ec = jnp.exp2(ma - m), jnp.exp2(mb - m), jnp.exp2(mc - m)
    inv = 1.0 / (ea * da + eb * db + ec * dc)
    lane = lax.broadcasted_iota(jnp.int32, (tm, LANES), 1)
    wcat = jnp.where(lane < N_HEADS_A, ea * inv,
                     jnp.where(lane < 2 * N_HEADS_A, pltpu.roll(eb * inv, N_HEADS_A, 1),
                               pltpu.roll(ec * inv, 2 * N_HEADS_A, 1)))
    hi = wcat.astype(BF16)
    lo = (wcat - hi.astype(F32)).astype(BF16)
    wide = _dot(jnp.concatenate([hi, lo], axis=1), ex_ref[...])
    ya = (wide[:, 0:GROUP_W] * o1[...].astype(F32) + wide[:, GROUP_W:2 * GROUP_W] * o2
          + wide[:, 2 * GROUP_W:] * o3)
    y = _dot(ya.astype(BF16), w_ref[0:GROUP_W, :]) + _dot(yb_ref[...], w_ref[GROUP_W:, :])
    out_ref[...] = _xattn_block(x_ref[...] + y, gxa_ref, wq_ref, kv_ref, wo_ref)


def _ev_tail(x, outs, stats, yb, w_bf16, tail):
    bn, s_len, d = x.shape
    tm = min(ROW_TILE, s_len)
    n_br = len(DILATIONS)
    ex = np.zeros((LANES, n_br * GROUP_W), np.float32)
    for i in range(n_br):
        for h in range(N_HEADS_A):
            ex[i * N_HEADS_A + h, i * GROUP_W + h * HEAD_DIM:i * GROUP_W + (h + 1) * HEAD_DIM] = 1.0
    ex = np.concatenate([ex, ex], axis=0)

    def o_spec(dil):
        return pl.BlockSpec((None, tm // dil, dil * GROUP_W), lambda b, i: (b, i, 0))

    def l_spec(dil):
        return pl.BlockSpec((None, tm // dil, dil * LANES), lambda b, i: (b, i, 0))

    x_spec = pl.BlockSpec((None, tm, d), lambda b, i: (b, i, 0))
    n_slabs = GROUP_W // LANES
    tail_specs, tail_args = _xattn_operands(tail, d)
    return pl.pallas_call(
        _ev_tail_kernel,
        out_shape=jax.ShapeDtypeStruct(x.shape, F32),
        grid=(bn, s_len // tm),
        in_specs=[o_spec(d_) for d_ in DILATIONS] + [l_spec(d_) for d_ in DILATIONS]
        + [o_spec(1),
           _resident(ex.shape, lambda b, i: (0, 0)),
           _resident((2 * GROUP_W, d), lambda b, i: (0, 0)),
           x_spec] + tail_specs,
        out_specs=x_spec,
        scratch_shapes=[pltpu.VMEM((n_slabs, tm, LANES), F32), pltpu.VMEM((n_slabs, tm, LANES), F32),
                        pltpu.VMEM((tm, LANES), F32), pltpu.VMEM((tm, LANES), F32)],
        compiler_params=_cparams(("parallel", "parallel")),
        name="ev_tail",
    )(*outs, *stats, yb, jnp.asarray(ex, BF16), w_bf16, x, *tail_args)


def _od_tail_kernel(yc_ref, yd_ref, w_ref, x_ref, gxa_ref, wq_ref, kv_ref, wo_ref, out_ref):
    half = yc_ref.shape[-1]
    x0 = x_ref[...] + _dot(yc_ref[...], w_ref[0:half, :]) + _dot(yd_ref[...], w_ref[half:, :])
    out_ref[...] = _xattn_block(x0, gxa_ref, wq_ref, kv_ref, wo_ref)


def _od_tail(x, yc_tm, yd, w_bf16, tail):
    bn, s_len, d = x.shape
    tm = min(ROW_TILE_WIDE, s_len)
    c = yd.shape[-1]
    x_spec = pl.BlockSpec((None, tm, d), lambda b, i: (b, i, 0))
    tail_specs, tail_args = _xattn_operands(tail, d)
    return pl.pallas_call(
        _od_tail_kernel,
        out_shape=jax.ShapeDtypeStruct(x.shape, F32),
        grid=(bn, s_len // tm),
        in_specs=[pl.BlockSpec((tm, c), lambda b, i: (i, b)),
                  pl.BlockSpec((None, tm, c), lambda b, i: (b, i, 0)),
                  _resident((2 * c, d), lambda b, i: (0, 0)),
                  x_spec] + tail_specs,
        out_specs=x_spec,
        compiler_params=_cparams(("parallel", "parallel")),
        name="od_tail",
    )(yc_tm, yd, w_bf16, x, *tail_args)


def _norm_matmul_kernel(x_ref, g_ref, w_ref, o_ref):
    o_ref[...] = _dot(_rms(x_ref[...], g_ref[...]).astype(BF16), w_ref[...]).astype(o_ref.dtype)


def _mem_kv(mem, g, w_bf16):
    bn, m_len, d = mem.shape
    n_out = w_bf16.shape[1]
    return pl.pallas_call(
        _norm_matmul_kernel,
        out_shape=jax.ShapeDtypeStruct((bn, m_len, n_out), BF16),
        grid=(bn,),
        in_specs=[pl.BlockSpec((None, m_len, d), lambda b: (b, 0, 0)),
                  pl.BlockSpec((1, d), lambda b: (0, 0)),
                  pl.BlockSpec((d, n_out), lambda b: (0, 0))],
        out_specs=pl.BlockSpec((None, m_len, n_out), lambda b: (b, 0, 0)),
        compiler_params=_cparams(("parallel",)),
        name="mem_kv",
    )(mem, g.reshape(1, d), w_bf16)


def _od_proj_kernel(x_ref, g_ref, w_ref, ug_ref, *rest):
    hg_refs, slab = rest[:-1], rest[-1]
    bn, tq, d = x_ref.shape
    h = _rms(x_ref[...].reshape(bn * tq, d), g_ref[...]).astype(BF16)
    n_ug = 2 * LRU_W
    n_slabs = GROUP_W // LANES
    for c in range(n_ug // GROUP_W):
        acc = _dot(h, w_ref[:, c * GROUP_W:(c + 1) * GROUP_W])
        for b in range(bn):
            for j in range(n_slabs):
                slab[j, pl.ds(b, tq, stride=bn), :] = acc[b * tq:(b + 1) * tq, j * LANES:(j + 1) * LANES]
        for j in range(n_slabs):
            ug_ref[:, c * GROUP_W + j * LANES:c * GROUP_W + (j + 1) * LANES] = slab[j]
    for c, hg_ref in enumerate(hg_refs):
        acc = _dot(h, w_ref[:, n_ug + c * GROUP_W:n_ug + (c + 1) * GROUP_W])
        hg_ref[...] = acc.reshape(bn, tq, GROUP_W).astype(hg_ref.dtype)


HG_INPUT_DTYPES = (("q", BF16), ("f_fw", F32), ("f_bw", F32), ("i_in", BF16), ("g_out", BF16))


def _od_proj(x, g, w_bf16):
    bn, s_len, d = x.shape
    tq = min(ROW_TILE_WIDE // bn, s_len)
    n_ug = 2 * LRU_W
    n_hg = w_bf16.shape[1] - n_ug
    assert n_hg == len(HG_INPUT_DTYPES) * GROUP_W
    hg_spec = pl.BlockSpec((bn, tq, GROUP_W), lambda i: (0, i, 0))
    ug, *hg = pl.pallas_call(
        _od_proj_kernel,
        out_shape=[jax.ShapeDtypeStruct((s_len * bn, n_ug), F32)]
        + [jax.ShapeDtypeStruct((bn, s_len, GROUP_W), dt) for _, dt in HG_INPUT_DTYPES],
        grid=(s_len // tq,),
        in_specs=[pl.BlockSpec((bn, tq, d), lambda i: (0, i, 0)),
                  pl.BlockSpec((1, d), lambda i: (0, 0)),
                  _resident((d, n_ug + n_hg), lambda i: (0, 0))],
        out_specs=[pl.BlockSpec((tq * bn, n_ug), lambda i: (i, 0))] + [hg_spec] * len(HG_INPUT_DTYPES),
        scratch_shapes=[pltpu.VMEM((GROUP_W // LANES, tq * bn, LANES), F32)],
        compiler_params=_cparams(("parallel",)),
        name="od_proj",
    )(x, g.reshape(1, d), w_bf16)
    return ug.reshape(s_len, bn, n_ug), dict(zip([n for n, _ in HG_INPUT_DTYPES], hg))


def _gelu_tanh(x):
    return 0.5 * x * (1.0 + jnp.tanh(math.sqrt(2.0 / math.pi) * (x + 0.044715 * (x * x * x))))


def _rglru_kernel(*refs, reverse, n_chunks):
    if reverse:
        (ul_ref, u_ref, ur_ref, gate_ref, hf_ref, cw_ref, cb_ref, wa_ref, ba_ref, wx_ref, bx_ref, sp_ref,
         out_ref, ext, a_s, b_s, h_s, carry) = refs
    else:
        (ul_ref, u_ref, ur_ref, cw_ref, cb_ref, wa_ref, ba_ref, wx_ref, bx_ref, sp_ref,
         out_ref, ext, a_s, b_s, carry) = refs
        h_s = out_ref
    step = pl.program_id(0)
    chunk = (n_chunks - 1 - step) if reverse else step
    t_len, bn, c = u_ref.shape

    @pl.when(step == 0)
    def _():
        carry[...] = jnp.zeros_like(carry)

    ext[0:2] = jnp.where(chunk == 0, 0.0, ul_ref[...])
    ext[2:t_len + 2] = u_ref[...]
    ext[t_len + 2:t_len + 3] = jnp.where(chunk == n_chunks - 1, 0.0, ur_ref[...])
    uc = cb_ref[...].reshape(1, 1, c)
    for j in range(4):
        uc = uc + cw_ref[j:j + 1, :].reshape(1, 1, c) * ext[j:j + t_len]

    u2 = uc.reshape(t_len * bn, c)
    ub = u2.astype(BF16)
    half = c // 2

    def gate_tanh(w_ref, b_ref):
        z = jnp.concatenate([_dot(ub[:, :half], w_ref[0]), _dot(ub[:, half:], w_ref[1])], axis=-1)
        return jnp.tanh(z + b_ref[...])

    tr = gate_tanh(wa_ref, ba_ref)
    ti = gate_tanh(wx_ref, bx_ref)
    a = jnp.exp2(sp_ref[...] * tr + sp_ref[...])
    b = jnp.sqrt(1.0 - a * a) * ((0.5 * ti + 0.5) * u2)
    a_s[...] = a.reshape(t_len, bn, c)
    b_s[...] = b.reshape(t_len, bn, c)

    def body(k, h):
        t = (t_len - 1 - k) if reverse else k
        h = a_s[t] * h + b_s[t]
        h_s[t] = h
        return h

    carry[...] = lax.fori_loop(0, t_len, body, carry[...], unroll=8)

    if reverse:
        out_ref[...] = ((hf_ref[...] + h_s[...]) * _gelu_tanh(gate_ref[...])).astype(out_ref.dtype)


def _rglru(ug, hf, conv_w, conv_b, wa, ba, wx, bx, lam, bn, *, reverse):
    s_len = ug.shape[0]
    c = LRU_W
    ug3 = ug
    t_len = min(LRU_T, s_len)
    n_chunks = s_len // t_len

    def ck(i):
        return (n_chunks - 1 - i) if reverse else i

    half = c // 2
    nb = LRU_BLOCKS // 2

    def dense_halves(w):
        w = 0.5 * w.astype(F32).reshape(2, nb, c // LRU_BLOCKS, c // LRU_BLOCKS)
        eye = jnp.eye(nb, dtype=F32)
        return jnp.einsum('gnij,nm->gnimj', w, eye).reshape(2, half, half).astype(BF16)

    decay_scale = ((-0.5 * LRU_C * LOG2E) * jax.nn.softplus(-lam.astype(F32))).reshape(1, c)
    vec = pl.BlockSpec((1, c), lambda i: (0, 0))
    wspec = pl.BlockSpec((2, half, half), lambda i: (0, 0, 0))
    blk = pl.BlockSpec((t_len, bn, c), lambda i: (ck(i), 0, 0))
    in_specs = [
        pl.BlockSpec((2, bn, c), lambda i: (jnp.maximum(ck(i) * (t_len // 2) - 1, 0), 0, 0)),
        blk,
        pl.BlockSpec((1, bn, c), lambda i: (jnp.minimum((ck(i) + 1) * t_len, s_len - 1), 0, 0)),
    ]
    args = [ug3, ug3, ug3]
    scratch = [pltpu.VMEM((t_len + 3, bn, c), F32), pltpu.VMEM((t_len, bn, c), F32),
               pltpu.VMEM((t_len, bn, c), F32)]
    if reverse:
        in_specs += [pl.BlockSpec((t_len, bn, c), lambda i: (ck(i), 0, 1)), blk]
        args += [ug3, hf]
        scratch.append(pltpu.VMEM((t_len, bn, c), F32))
    scratch.append(pltpu.VMEM((bn, c), F32))
    in_specs += [pl.BlockSpec((4, c), lambda i: (0, 0)), vec, wspec, vec, wspec, vec, vec]
    args += [conv_w.astype(F32), conv_b.astype(F32).reshape(1, c), dense_halves(wa),
             0.5 * ba.astype(F32).reshape(1, c), dense_halves(wx), 0.5 * bx.astype(F32).reshape(1, c),
             decay_scale]
    return pl.pallas_call(
        functools.partial(_rglru_kernel, reverse=reverse, n_chunks=n_chunks),
        out_shape=jax.ShapeDtypeStruct((s_len, bn, c), BF16 if reverse else F32),
        grid=(n_chunks,),
        in_specs=in_specs,
        out_specs=blk,
        scratch_shapes=scratch,
        compiler_params=_cparams(("arbitrary",)),
        name="rglru_bw" if reverse else "rglru_fw",
    )(*args)


def _hg_matrices(c, reverse):
    n_levels = int(math.log2(c))
    t = np.arange(c)[:, None]
    r = np.arange(c)[None, :]
    mats, masks, upper = [], [], []
    if not reverse:
        mats.append(r <= t)
        mats.append(r > t)
    else:
        mats.append(r >= t)
        mats.append(r < t)
    for lev in range(n_levels):
        half = c >> (lev + 1)
        parent = 2 * half
        start = (np.arange(c) // parent) * parent
        mid = (start + half)[:, None]
        later = (np.arange(c) % parent >= half)[:, None]
        if not reverse:
            m = np.where(later, (r >= mid) & (r <= t), (r > t) & (r < mid))
            is_q = later
        else:
            m = np.where(later, (r >= mid) & (r < t), (r >= t) & (r < mid))
            is_q = ~later
        mats.append(m)
        same_parent = (start[:, None] == start[None, :])
        masks.append(same_parent & is_q & (~is_q).T)
        upper.append(np.broadcast_to(is_q, (c, HG_DK)))
    mat = np.concatenate(mats, axis=0).astype(np.float32)
    return (np.concatenate([mat, mat], axis=1), np.stack(masks).astype(np.float32),
            np.stack(upper).astype(np.float32))


def _hgrn_kernel(*refs, reverse, n_steps, n_levels):
    if reverse:
        (q_ref, f_ref, v_ref, go_ref, of_ref, lb_ref, gn_ref, mat_ref, mask_ref, isq_ref, out_ref,
         state, x_scr, oi_scr, aux_scr, dec_scr, kk_scr) = refs
    else:
        (q_ref, f_ref, v_ref, lb_ref, mat_ref, mask_ref, isq_ref, out_ref,
         state, x_scr, oi_scr, aux_scr, dec_scr, kk_scr) = refs
    step = pl.program_id(1)
    c = HG_CHUNK
    t_len = q_ref.shape[0]
    n_chunks = t_len // c

    @pl.when(step == 0)
    def _():
        state[...] = jnp.zeros_like(state)

    def chunk_body(ci, carry):
        ch = (n_chunks - 1 - ci) if reverse else ci
        rows = slice(ch * c, (ch + 1) * c)
        edge = (c - 1) if not reverse else 0
        for hp in range(HG_HEADS // 2):
            cols2 = slice(2 * hp * HG_DK, 2 * (hp + 1) * HG_DK)
            fl = f_ref[rows, cols2]
            e = jnp.exp(-jnp.abs(fl))
            r = 1.0 / (1.0 + e)
            er = e * r
            pos = fl > 0.0
            lb = lb_ref[0:1, cols2]
            oml = lb_ref[1:2, cols2]
            g2 = jnp.maximum(jnp.log2(lb + oml * jnp.where(pos, r, er)), F32_MIN_EXPONENT)
            kk_scr[hp] = oml * jnp.where(pos, er, r)
            g_hi = g2.astype(BF16)
            g_lo = (g2 - g_hi.astype(F32)).astype(BF16)
            dec_scr[hp] = jnp.exp2(_dot(mat_ref[...], jnp.concatenate([g_hi, g_lo], axis=0)))

        for hd in range(HG_HEADS):
            cols = slice(hd * HG_DK, (hd + 1) * HG_DK)
            pcols = slice((hd % 2) * HG_DK, (hd % 2 + 1) * HG_DK)
            kk = kk_scr[hd // 2, :, pcols]
            dec = dec_scr.at[hd // 2]
            qv = q_ref[rows, cols].astype(F32)
            qq = qv * jax.nn.sigmoid(qv)
            qb, kb = qq.astype(BF16), kk.astype(BF16)
            d_in = dec[0:c, pcols]
            x_scr[hd, n_levels] = (qq * d_in).astype(BF16)
            x_scr[hd, n_levels + 1] = (kk * dec[c:2 * c, pcols]).astype(BF16)
            aux_scr[hd, 0] = jnp.broadcast_to(d_in[edge:edge + 1, :], (c, HG_DK))
            aux_scr[hd, 1] = jnp.broadcast_to(jnp.sum(qq * kk, axis=-1, keepdims=True), (c, HG_DK))
            for lev in range(n_levels):
                half = c >> (lev + 1)
                if half >= SUBLANES_BF16:
                    first_is_q = bool(reverse)
                    base = jnp.concatenate(
                        [(qb if (blk % 2 == 1) != first_is_q else kb)[blk * half:(blk + 1) * half]
                         for blk in range(c // half)], axis=0)
                else:
                    base = jnp.where(isq_ref[lev] > 0.5, qb, kb)
                x_scr[hd, lev] = base * dec[(2 + lev) * c:(3 + lev) * c, pcols].astype(BF16)

        for hd in range(HG_HEADS):
            cols = slice(hd * HG_DK, (hd + 1) * HG_DK)
            att = None
            for lev in range(n_levels):
                xl = x_scr[hd, lev]
                term = mask_ref[lev] * _nt_dot(xl, xl)
                att = term if att is None else att + term
            st = state[hd]
            oi_scr[hd] = _nt_dot(x_scr[hd, n_levels], st.astype(BF16))
            vt = v_ref[rows, cols].astype(F32).T.astype(BF16)
            state[hd] = st * aux_scr[hd, 0] + _dot(vt, x_scr[hd, n_levels + 1])
            x_scr[hd, 0] = att.astype(BF16)

        for hd in range(HG_HEADS):
            cols = slice(hd * HG_DK, (hd + 1) * HG_DK)
            vb = v_ref[rows, cols]
            o = oi_scr[hd] + _dot(x_scr[hd, 0], vb) + aux_scr[hd, 1] * vb.astype(F32)

            if reverse:
                o = o + of_ref[rows, cols]
                o = o * lax.rsqrt(jnp.mean(o * o, axis=-1, keepdims=True) + EPS) * gn_ref[...]
                gv = go_ref[rows, cols].astype(F32)
                out_ref[rows, cols] = (o * (gv * jax.nn.sigmoid(gv))).astype(out_ref.dtype)
            else:
                out_ref[rows, cols] = o
        return carry

    for ci in range(n_chunks):
        chunk_body(ci, 0)


def _hgrn(hg, o_fw, lb, gnorm_g, *, reverse):
    bn, s_len, _ = hg["q"].shape
    w = HG_HEADS * HG_DK
    t_len = min(HG_T, s_len)
    n_steps = s_len // t_len
    mats, masks, isq = _hg_matrices(HG_CHUNK, reverse)
    n_levels = masks.shape[0]
    lbf = lb.astype(F32)
    lb_tab = jnp.stack([lbf, 1.0 - lbf])

    blk = pl.BlockSpec((None, t_len, w), lambda b, i: (b, (n_steps - 1 - i) if reverse else i, 0))

    def const(shape):
        return pl.BlockSpec(shape, lambda b, i: (0,) * len(shape))

    in_specs = [blk, blk, blk]
    args = [hg["q"], hg["f_bw" if reverse else "f_fw"], hg["i_in"]]
    if reverse:
        in_specs += [blk, blk]
        args += [hg["g_out"], o_fw]
    in_specs.append(const((2, w)))
    args.append(lb_tab)
    if reverse:
        in_specs.append(const((1, HG_DK)))
        args.append(gnorm_g.astype(F32).reshape(1, HG_DK))
    in_specs += [const(mats.shape), const(masks.shape), const(isq.shape)]
    args += [jnp.asarray(mats, BF16), jnp.asarray(masks, F32), jnp.asarray(isq, BF16)]
    return pl.pallas_call(
        functools.partial(_hgrn_kernel, reverse=reverse, n_steps=n_steps, n_levels=n_levels),
        out_shape=jax.ShapeDtypeStruct((bn, s_len, w), BF16 if reverse else F32),
        grid=(bn, n_steps),
        in_specs=in_specs,
        out_specs=blk,
        scratch_shapes=[pltpu.VMEM((HG_HEADS, HG_DK, HG_DK), F32),
                        pltpu.VMEM((HG_HEADS, n_levels + 2, HG_CHUNK, HG_DK), BF16),
                        pltpu.VMEM((HG_HEADS, HG_CHUNK, HG_DK), F32),
                        pltpu.VMEM((HG_HEADS, 2, HG_CHUNK, HG_DK), F32),
                        pltpu.VMEM((HG_HEADS // 2, (n_levels + 2) * HG_CHUNK, 2 * HG_DK), F32),
                        pltpu.VMEM((HG_HEADS // 2, HG_CHUNK, 2 * HG_DK), F32)],
        compiler_params=_cparams(("parallel", "arbitrary")),
        name="hgrn_bw" if reverse else "hgrn_fw",
    )(*args)


def _even_layer(x, g, w_in, w_out, rpb, tail):
    qkv, view4, view16 = _ev_proj(x, g, w_in.astype(BF16))
    outs, stats = zip(*[_dilated_branch(view, dil) for dil, view in zip(DILATIONS, (qkv, view4, view16))])
    yb = _neighborhood(qkv, rpb)
    return _ev_tail(x, outs, stats, yb, w_out.astype(BF16), tail)


def _odd_layer(x, g, w_in, w_out, conv_w, conv_b, wa, ba, wx, bx, lam, lb_f, lb_b, gnorm_g, tail):
    bn, s_len, _ = x.shape
    ug, hg = _od_proj(x, g, w_in.astype(BF16))
    h_fw = _rglru(ug, None, conv_w, conv_b, wa[0], ba[0], wx[0], bx[0], lam[0], bn, reverse=False)
    yc = _rglru(ug, h_fw, conv_w, conv_b, wa[1], ba[1], wx[1], bx[1], lam[1], bn, reverse=True)
    o_fw = _hgrn(hg, None, lb_f, gnorm_g, reverse=False)
    yd = _hgrn(hg, o_fw, lb_b, gnorm_g, reverse=True)
    return _od_tail(x, yc.reshape(s_len, bn * LRU_W), yd, w_out.astype(BF16), tail)


def kernel(x, mem, norm_mix_g, norm_xa_g, norm_mem_g, norm_mlp_g, final_norm_g, ev_w_in, ev_w_out, na_rpb,
           od_w_in, od_w_out, conv_w, conv_b, lru_wa, lru_ba, lru_wx, lru_bx, lru_lambda, hgrn_lb_logits,
           hgrn_norm_g, xa_wq, xa_wkv, xa_wo, mlp_w1, mlp_w2):
    depth = norm_mix_g.shape[0]
    p_lb = jax.nn.softmax(hgrn_lb_logits.astype(F32), axis=0)
    lower_bounds = jnp.cumsum(p_lb, axis=0) - p_lb[0:1]
    for layer in range(depth):
        tail = dict(g_xa=norm_xa_g[layer], wq=xa_wq[layer].astype(BF16),
                    kv=_mem_kv(mem, norm_mem_g[layer], xa_wkv[layer].astype(BF16)),
                    wo=xa_wo[layer].astype(BF16), g_mlp=norm_mlp_g[layer],
                    w1=mlp_w1[layer].astype(BF16), w2=mlp_w2[layer].astype(BF16),
                    g_final=final_norm_g if layer == depth - 1 else None)
        if layer % 2 == 0:
            e = layer // 2
            x = _even_layer(x, norm_mix_g[layer], ev_w_in[e], ev_w_out[e], na_rpb[e], tail)
        else:
            o = layer // 2
            x = _odd_layer(x, norm_mix_g[layer], od_w_in[o], od_w_out[o], conv_w[o], conv_b[o],
                           lru_wa[o], lru_ba[o], lru_wx[o], lru_bx[o], lru_lambda[o],
                           lower_bounds[layer, 0], lower_bounds[layer, 1], hgrn_norm_g[o], tail)
        x = _mlp(x, tail)
    return x
```

```python
import functools
import math

import jax
import jax.numpy as jnp
import numpy as np
from jax import lax
from jax.experimental import pallas as pl
from jax.experimental.pallas import tpu as pltpu

F32 = jnp.float32
BF16 = jnp.bfloat16

D_MODEL = 1024
HEAD_DIM = 64
ROT_DIM = 16
ROPE_THETA = 500000.0
N_HEADS_A = 8
N_HEADS_B = 8
GROUP_W = N_HEADS_A * HEAD_DIM
DILATIONS = (1, 4, 16)
WIN_HALF = 64
GRID_W = 64
NA_ROWS = 8
NA_COLS = 16
LRU_W = 512
LRU_BLOCKS = 8
LRU_C = 8.0
HG_HEADS = 4
HG_DK = 128
XA_HEADS = 4
XA_DH = 256
D_FF = 4096
EPS = 1e-6
LOG2E = math.log2(math.e)
F32_MIN_EXPONENT = -126.0

LANES = 128
SUBLANES_BF16 = 16
V7X_VMEM_LIMIT_BYTES = 56 * 1024 * 1024

ROW_TILE_WIDE = 1024
ATT_Q_BLOCK = 512
ATT_SUB = 128
NA_ROW_GROUP = 8
LRU_T = 128
HG_CHUNK = 128
HG_T = 1024


def _cparams(sem):
    return pltpu.CompilerParams(dimension_semantics=sem, vmem_limit_bytes=V7X_VMEM_LIMIT_BYTES)


def _resident(shape, index_map):
    return pl.BlockSpec(shape, index_map, pipeline_mode=pl.Buffered(1))


def _rms(x, g):
    return x * lax.rsqrt(jnp.mean(x * x, axis=-1, keepdims=True) + EPS) * g


def _nt_dot(a, b):
    return lax.dot_general(a, b, (((1,), (1,)), ((), ())), preferred_element_type=F32)


def _dot(a, b):
    return jnp.dot(a, b, preferred_element_type=F32)


def _head_selectors():
    lane = lax.broadcasted_iota(jnp.int32, (1, LANES), 1)
    low = (lane < HEAD_DIM).astype(F32)
    return low.astype(BF16), (1.0 - low).astype(BF16)


def _ev_proj_kernel(x_ref, g_ref, w_ref, rc_ref, rs1_ref, rs2_ref, o_ref, o4_ref, o16_ref, slab):
    h = _rms(x_ref[...], g_ref[...]).astype(BF16)
    scale = HEAD_DIM ** -0.5 * LOG2E
    tm = x_ref.shape[0]
    n_slabs = GROUP_W // LANES
    for c in range(6):
        acc = _dot(h, w_ref[:, c * GROUP_W:(c + 1) * GROUP_W])
        if c in (0, 1):
            rc, rs1, rs2 = rc_ref[...], rs1_ref[...], rs2_ref[...]
            parts = []
            for j in range(n_slabs):
                t = acc[:, j * LANES:(j + 1) * LANES]
                t = t * rc + pltpu.roll(t, LANES - ROT_DIM // 2, 1) * rs1 + pltpu.roll(t, ROT_DIM // 2, 1) * rs2
                parts.append(t)
            acc = jnp.concatenate(parts, axis=-1)
        if c in (0, 3):
            acc = acc * scale
        o_ref[:, c * GROUP_W:(c + 1) * GROUP_W] = acc.astype(BF16)
        if c < 3:
            for j in range(n_slabs):
                slab[j] = acc[:, j * LANES:(j + 1) * LANES]
            for dil, dst in ((DILATIONS[1], o4_ref), (DILATIONS[2], o16_ref)):
                for r in range(dil):
                    for j in range(n_slabs):
                        col = r * 3 * GROUP_W + c * GROUP_W + j * LANES
                        dst[:, col:col + LANES] = slab[j, pl.ds(r, tm // dil, stride=dil), :].astype(BF16)


def _rope_tables(s_len):
    half = ROT_DIM // 2
    inv = jnp.asarray(ROPE_THETA ** (-np.arange(half) * 2.0 / ROT_DIM), F32)
    ang = jnp.arange(s_len, dtype=F32)[:, None] * inv[None, :]
    cos, sin = jnp.cos(ang), jnp.sin(ang)
    ones = jnp.ones((s_len, HEAD_DIM - ROT_DIM), F32)
    zeros = jnp.zeros((s_len, HEAD_DIM - ROT_DIM), F32)
    zh = jnp.zeros((s_len, half), F32)
    rc = jnp.concatenate([cos, cos, ones], axis=-1)
    rs1 = jnp.concatenate([-sin, zh, zeros], axis=-1)
    rs2 = jnp.concatenate([zh, sin, zeros], axis=-1)
    rep = LANES // HEAD_DIM
    return jnp.tile(rc, (1, rep)), jnp.tile(rs1, (1, rep)), jnp.tile(rs2, (1, rep))


def _ev_proj(x, g, w_bf16):
    bn, s_len, d = x.shape
    tm = min(ROW_TILE_WIDE, s_len)
    n_out = w_bf16.shape[1]
    rc, rs1, rs2 = _rope_tables(s_len)
    tbl_spec = pl.BlockSpec((tm, LANES), lambda b, i: (i, 0))
    wa = 3 * GROUP_W
    d4, d16 = DILATIONS[1], DILATIONS[2]
    return pl.pallas_call(
        _ev_proj_kernel,
        out_shape=(jax.ShapeDtypeStruct((bn, s_len, n_out), BF16),
                   jax.ShapeDtypeStruct((bn, s_len // d4, d4 * wa), BF16),
                   jax.ShapeDtypeStruct((bn, s_len // d16, d16 * wa), BF16)),
        grid=(bn, s_len // tm),
        in_specs=[
            pl.BlockSpec((None, tm, d), lambda b, i: (b, i, 0)),
            pl.BlockSpec((1, d), lambda b, i: (0, 0)),
            _resident((d, n_out), lambda b, i: (0, 0)),
            tbl_spec, tbl_spec, tbl_spec,
        ],
        out_specs=(pl.BlockSpec((None, tm, n_out), lambda b, i: (b, i, 0)),
                   pl.BlockSpec((None, tm // d4, d4 * wa), lambda b, i: (b, i, 0)),
                   pl.BlockSpec((None, tm // d16, d16 * wa), lambda b, i: (b, i, 0))),
        scratch_shapes=[pltpu.VMEM((GROUP_W // LANES, tm, LANES), F32)],
        compiler_params=_cparams(("parallel", "parallel")),
        name="ev_proj",
    )(x, g.reshape(1, d), w_bf16, rc, rs1, rs2)


def _dilated_kernel(q_ref, kl_ref, kc_ref, kr_ref, vl_ref, vc_ref, vr_ref, o_ref, stat_ref,
                    kext, vext, s_scr, p_scr, *, lq, l_total):
    i = pl.program_id(2)
    kext[0:WIN_HALF, :] = kl_ref[...]
    kext[WIN_HALF:WIN_HALF + lq, :] = kc_ref[...]
    kext[WIN_HALF + lq:, :] = kr_ref[...]
    vext[0:WIN_HALF, :] = vl_ref[...]
    vext[WIN_HALF:WIN_HALF + lq, :] = vc_ref[...]
    vext[WIN_HALF + lq:, :] = vr_ref[...]

    wk = ATT_SUB + 2 * WIN_HALF
    qi = lax.broadcasted_iota(jnp.int32, (ATT_SUB, wk), 0)
    ci = lax.broadcasted_iota(jnp.int32, (ATT_SUB, wk), 1)
    band_bias = jnp.where((ci - qi >= 0) & (ci - qi <= 2 * WIN_HALF), 0.0, -jnp.inf).astype(F32)
    crow = lax.broadcasted_iota(jnp.int32, (1, wk), 1)
    lane = lax.broadcasted_iota(jnp.int32, (ATT_SUB, LANES), 1)
    low_half = lane < HEAD_DIM
    sel_lo, sel_hi = _head_selectors()
    n_sub = lq // ATT_SUB
    n_hp = GROUP_W // LANES

    for j in range(n_sub):
        base = i * lq + j * ATT_SUB - WIN_HALF
        in_seq = (crow + base >= 0) & (crow + base < l_total)
        bias = band_bias + jnp.where(in_seq, 0.0, -jnp.inf).astype(F32)
        bias2 = jnp.concatenate([bias, bias], axis=0)
        rows = slice(j * ATT_SUB, (j + 1) * ATT_SUB)
        wrows = slice(j * ATT_SUB, j * ATT_SUB + wk)
        for hp in range(n_hp):
            cols = slice(hp * LANES, (hp + 1) * LANES)
            qp = q_ref[rows, cols]
            q2 = jnp.concatenate([qp * sel_lo, qp * sel_hi], axis=0)
            s_scr[j * n_hp + hp] = _nt_dot(q2, kext[wrows, cols]) + bias2
    for j in range(n_sub):
        stat = jnp.ones((ATT_SUB, LANES), F32)
        for hp in range(n_hp):
            t = j * n_hp + hp
            s = s_scr[t]
            m = jnp.max(s, axis=-1, keepdims=True)
            p = jnp.exp2(s - m)
            den = jnp.sum(p, axis=-1, keepdims=True)
            p_scr[t] = p.astype(BF16)
            stat = jnp.where(lane == 2 * hp, m[0:ATT_SUB], jnp.where(lane == 2 * hp + 1, m[ATT_SUB:], stat))
            stat = jnp.where(lane == N_HEADS_A + 2 * hp, den[0:ATT_SUB],
                             jnp.where(lane == N_HEADS_A + 2 * hp + 1, den[ATT_SUB:], stat))
        stat_ref[j * ATT_SUB:(j + 1) * ATT_SUB, :] = stat
    for j in range(n_sub):
        rows = slice(j * ATT_SUB, (j + 1) * ATT_SUB)
        wrows = slice(j * ATT_SUB, j * ATT_SUB + wk)
        for hp in range(n_hp):
            cols = slice(hp * LANES, (hp + 1) * LANES)
            o = _dot(p_scr[j * n_hp + hp], vext[wrows, cols])
            o_ref[rows, cols] = jnp.where(low_half, o[0:ATT_SUB], o[ATT_SUB:]).astype(BF16)


def _dilated_branch(view, dil):
    bn, l_total, width = view.shape
    n_groups = width // (dil * GROUP_W)
    lq = min(ATT_Q_BLOCK, l_total)
    nblk = l_total // lq
    per = lq // WIN_HALF
    n_halo = l_total // WIN_HALF
    n_units = (lq // ATT_SUB) * (GROUP_W // LANES)

    def cur(g):
        return pl.BlockSpec((None, lq, GROUP_W), lambda b, r, i: (b, i, r * n_groups + g))

    def left(g):
        return pl.BlockSpec((None, WIN_HALF, GROUP_W),
                            lambda b, r, i: (b, jnp.maximum(i * per - 1, 0), r * n_groups + g))

    def right(g):
        return pl.BlockSpec((None, WIN_HALF, GROUP_W),
                            lambda b, r, i: (b, jnp.minimum((i + 1) * per, n_halo - 1), r * n_groups + g))

    stat_shape = jax.ShapeDtypeStruct((bn, l_total, dil * LANES), F32)
    stat_spec = pl.BlockSpec((None, lq, LANES), lambda b, r, i: (b, i, r))
    return pl.pallas_call(
        functools.partial(_dilated_kernel, lq=lq, l_total=l_total),
        out_shape=(jax.ShapeDtypeStruct((bn, l_total, dil * GROUP_W), BF16), stat_shape),
        grid=(bn, dil, nblk),
        in_specs=[cur(0), left(1), cur(1), right(1), left(2), cur(2), right(2)],
        out_specs=(pl.BlockSpec((None, lq, GROUP_W), lambda b, r, i: (b, i, r)), stat_spec),
        scratch_shapes=[pltpu.VMEM((lq + 2 * WIN_HALF, GROUP_W), BF16),
                        pltpu.VMEM((lq + 2 * WIN_HALF, GROUP_W), BF16),
                        pltpu.VMEM((n_units, 2 * ATT_SUB, ATT_SUB + 2 * WIN_HALF), F32),
                        pltpu.VMEM((n_units, 2 * ATT_SUB, ATT_SUB + 2 * WIN_HALF), BF16)],
        compiler_params=_cparams(("parallel", "parallel", "parallel")),
        name=f"dilated_attn_d{dil}",
    )(view, view, view, view, view, view, view)


NA_QT = NA_ROW_GROUP * GRID_W
NA_KT = 2 * NA_QT
NA_KW = NA_ROWS * GRID_W


def _na_tables(rpb):
    rpb = rpb.astype(F32)
    nh = rpb.shape[0]
    ext = GRID_W - NA_COLS
    padded = jnp.concatenate([jnp.repeat(rpb[..., :1], ext, axis=-1), rpb,
                              jnp.repeat(rpb[..., -1:], ext, axis=-1)], axis=-1)
    skew = jnp.tile(padded, (1, 1, GRID_W + 1))[..., :2 * GRID_W * GRID_W]
    t1 = skew.reshape(nh, -1, GRID_W, 2 * GRID_W)[..., ::-1, :GRID_W]
    qc = np.arange(GRID_W)[:, None]
    kc = np.arange(GRID_W)[None, :]
    c0 = np.clip(qc - NA_COLS // 2, 0, GRID_W - NA_COLS)
    t1 = jnp.where((kc >= c0) & (kc < c0 + NA_COLS), t1 * LOG2E, -jnp.inf)
    per_delta = [jnp.transpose(t1[:, NA_ROWS - 1 - delta:2 * NA_ROWS - 1 - delta], (0, 2, 1, 3))
                 .reshape(nh, GRID_W, NA_KW) for delta in range(NA_ROWS)]
    return jnp.stack(per_delta).reshape(NA_ROWS, nh // 2, 2 * GRID_W, NA_KW)


def _na_kernel(q_ref, k_ref, v_ref, tb_ref, o_ref, s_scr, p_scr, inv_scr, *, rows):
    i0 = pl.program_id(1) * NA_ROW_GROUP
    w0 = jnp.clip(i0 - NA_ROWS // 2, 0, rows - 2 * NA_ROW_GROUP)
    lane = lax.broadcasted_iota(jnp.int32, (GRID_W, LANES), 1)
    low_half = lane < HEAD_DIM
    sel_lo, sel_hi = _head_selectors()
    n_hp = GROUP_W // LANES

    def key_rows(a):
        r0 = jnp.clip(i0 + a - NA_ROWS // 2, 0, rows - NA_ROWS)
        return pl.ds(pl.multiple_of((r0 - w0) * GRID_W, GRID_W), NA_KW), i0 + a - r0

    for a in range(NA_ROW_GROUP):
        krows, delta = key_rows(a)
        qrows = slice(a * GRID_W, (a + 1) * GRID_W)
        for hp in range(n_hp):
            cols = slice(hp * LANES, (hp + 1) * LANES)
            qp = q_ref[qrows, cols]
            q2 = jnp.concatenate([qp * sel_lo, qp * sel_hi], axis=0)
            s_scr[a * n_hp + hp] = _nt_dot(q2, k_ref[0, krows, cols]) + tb_ref[delta, hp]
    for t in range(NA_ROW_GROUP * n_hp):
        s = s_scr[t]
        p = jnp.exp2(s - jnp.max(s, axis=-1, keepdims=True))
        inv_scr[t] = jnp.broadcast_to(1.0 / jnp.sum(p, axis=-1, keepdims=True), (2 * GRID_W, LANES))
        p_scr[t] = p.astype(BF16)
    for a in range(NA_ROW_GROUP):
        krows, _ = key_rows(a)
        qrows = slice(a * GRID_W, (a + 1) * GRID_W)
        for hp in range(n_hp):
            cols = slice(hp * LANES, (hp + 1) * LANES)
            t = a * n_hp + hp
            o = _dot(p_scr[t], v_ref[0, krows, cols]) * inv_scr[t]
            o_ref[qrows, cols] = jnp.where(low_half, o[0:GRID_W], o[GRID_W:]).astype(BF16)


def _neighborhood(qkv, rpb):
    bn, s_len, _ = qkv.shape
    rows = s_len // GRID_W
    n_groups = rows // NA_ROW_GROUP
    tables = _na_tables(rpb)

    def window(g):
        return pl.BlockSpec(
            (pl.Element(1), pl.Element(NA_KT), pl.Element(GROUP_W)),
            lambda b, i: (b, jnp.clip(i * NA_ROW_GROUP - NA_ROWS // 2, 0, rows - 2 * NA_ROW_GROUP) * GRID_W,
                          g * GROUP_W))

    return pl.pallas_call(
        functools.partial(_na_kernel, rows=rows),
        out_shape=jax.ShapeDtypeStruct((bn, s_len, GROUP_W), BF16),
        grid=(bn, n_groups),
        in_specs=[pl.BlockSpec((None, NA_QT, GROUP_W), lambda b, i: (b, i, 3)), window(4), window(5),
                  pl.BlockSpec(tables.shape, lambda b, i: (0, 0, 0, 0))],
        out_specs=pl.BlockSpec((None, NA_QT, GROUP_W), lambda b, i: (b, i, 0)),
        scratch_shapes=[pltpu.VMEM((NA_ROW_GROUP * GROUP_W // LANES, 2 * GRID_W, NA_KW), F32),
                        pltpu.VMEM((NA_ROW_GROUP * GROUP_W // LANES, 2 * GRID_W, NA_KW), BF16),
                        pltpu.VMEM((NA_ROW_GROUP * GROUP_W // LANES, 2 * GRID_W, LANES), F32)],
        compiler_params=_cparams(("parallel", "parallel")),
        name="neighborhood_attn",
    )(qkv, qkv, qkv, tables)


def _xattn_block(x, g_ref, wq_ref, kv_ref, wo_ref):
    q = (_dot(_rms(x, g_ref[...]).astype(BF16), wq_ref[...]) * (XA_DH ** -0.5 * LOG2E)).astype(BF16)
    heads = []
    for hd in range(XA_HEADS):
        cols = slice(hd * XA_DH, (hd + 1) * XA_DH)
        s = _nt_dot(q[:, cols], kv_ref[:, cols])
        m = jnp.max(s, axis=-1, keepdims=True)
        p = jnp.exp2(s - m)
        den = jnp.sum(p, axis=-1, keepdims=True)
        vcols = slice(D_MODEL + hd * XA_DH, D_MODEL + (hd + 1) * XA_DH)
        heads.append((_dot(p.astype(BF16), kv_ref[:, vcols]) * (1.0 / den)).astype(BF16))
    return x + _dot(jnp.concatenate(heads, axis=-1), wo_ref[...])


def _mlp_block(x, g_ref, w1_ref, w2_ref, gf_ref, final_norm):
    h = _rms(x, g_ref[...]).astype(BF16)
    acc = x
    for c in range(D_FF // D_MODEL):
        cols = slice(c * D_MODEL, (c + 1) * D_MODEL)
        a = jnp.maximum(_dot(h, w1_ref[:, cols]), 0.0)
        acc = acc + _dot((a * a).astype(BF16), w2_ref[cols, :])
    return _rms(acc, gf_ref[...]) if final_norm else acc


def _xattn_operands(tail, d):
    m_len = tail["kv"].shape[1]
    specs = [pl.BlockSpec((1, d), lambda b, i: (0, 0)), _resident((d, d), lambda b, i: (0, 0)),
             pl.BlockSpec((None, m_len, 2 * d), lambda b, i: (b, 0, 0)),
             _resident((d, d), lambda b, i: (0, 0))]
    return specs, [tail["g_xa"].reshape(1, d), tail["wq"], tail["kv"], tail["wo"]]


def _mlp_kernel(x_ref, g_ref, w1_ref, w2_ref, gf_ref, out_ref, *, final_norm):
    out_ref[...] = _mlp_block(x_ref[...], g_ref, w1_ref, w2_ref, gf_ref, final_norm)


def _mlp(x, tail):
    bn, s_len, d = x.shape
    tm = min(ROW_TILE_WIDE, s_len)
    final_norm = tail["g_final"] is not None
    gf = tail["g_final"] if final_norm else tail["g_mlp"]
    x_spec = pl.BlockSpec((None, tm, d), lambda b, i: (b, i, 0))
    vec = pl.BlockSpec((1, d), lambda b, i: (0, 0))
    return pl.pallas_call(
        functools.partial(_mlp_kernel, final_norm=final_norm),
        out_shape=jax.ShapeDtypeStruct(x.shape, F32),
        grid=(bn, s_len // tm),
        in_specs=[x_spec, vec, _resident((d, D_FF), lambda b, i: (0, 0)),
                  _resident((D_FF, d), lambda b, i: (0, 0)), vec],
        out_specs=x_spec,
        compiler_params=_cparams(("parallel", "parallel")),
        name="mlp_final" if final_norm else "mlp",
    )(x, tail["g_mlp"].reshape(1, d), tail["w1"], tail["w2"], gf.reshape(1, d))


def _ev_tail_kernel(o1, o2v, o3v, s1, s2v, s3v, yb_ref, ex_ref, w_ref, x_ref,
                    gxa_ref, wq_ref, kv_ref, wo_ref, out_ref, so2, so3, st2, st3):
    tm = x_ref.shape[0]
    n_slabs = GROUP_W // LANES
    for dil, ov, sv, so, st in ((DILATIONS[1], o2v, s2v, so2, st2), (DILATIONS[2], o3v, s3v, so3, st3)):
        for r in range(dil):
            dst = pl.ds(r, tm // dil, stride=dil)
            for j in range(n_slabs):
                col = r * GROUP_W + j * LANES
                so[j, dst, :] = ov[:, col:col + LANES].astype(F32)
            st[dst, :] = sv[:, r * LANES:(r + 1) * LANES]
    o2 = jnp.concatenate([so2[j] for j in range(n_slabs)], axis=-1)
    o3 = jnp.concatenate([so3[j] for j in range(n_slabs)], axis=-1)
    ma, mb, mc = s1[...], st2[...], st3[...]
    da, db, dc = [pltpu.roll(t, LANES - N_HEADS_A, 1) for t in (ma, mb, mc)]
    m = jnp.maximum(jnp.maximum(ma, mb), mc)
    ea, eb, ec = jnp.exp2(ma - m), jnp.exp2(mb - m), jnp.exp2(mc - m)
    inv = 1.0 / (ea * da + eb * db + ec * dc)
    lane = lax.broadcasted_iota(jnp.int32, (tm, LANES), 1)
    wcat = jnp.where(lane < N_HEADS_A, ea * inv,
                     jnp.where(lane < 2 * N_HEADS_A, pltpu.roll(eb * inv, N_HEADS_A, 1),
                               pltpu.roll(ec * inv, 2 * N_HEADS_A, 1)))
    hi = wcat.astype(BF16)
    lo = (wcat - hi.astype(F32)).astype(BF16)
    wide = _dot(jnp.concatenate([hi, lo], axis=1), ex_ref[...])
    ya = (wide[:, 0:GROUP_W] * o1[...].astype(F32) + wide[:, GROUP_W:2 * GROUP_W] * o2
          + wide[:, 2 * GROUP_W:] * o3)
    y = _dot(ya.astype(BF16), w_ref[0:GROUP_W, :]) + _dot(yb_ref[...], w_ref[GROUP_W:, :])
    out_ref[...] = _xattn_block(x_ref[...] + y, gxa_ref, wq_ref, kv_ref, wo_ref)


def _ev_tail(x, outs, stats, yb, w_bf16, tail):
    bn, s_len, d = x.shape
    tm = min(ROW_TILE_WIDE, s_len)
    n_br = len(DILATIONS)
    ex = np.zeros((LANES, n_br * GROUP_W), np.float32)
    for i in range(n_br):
        for h in range(N_HEADS_A):
            ex[i * N_HEADS_A + h, i * GROUP_W + h * HEAD_DIM:i * GROUP_W + (h + 1) * HEAD_DIM] = 1.0
    ex = np.concatenate([ex, ex], axis=0)

    def o_spec(dil):
        return pl.BlockSpec((None, tm // dil, dil * GROUP_W), lambda b, i: (b, i, 0))

    def l_spec(dil):
        return pl.BlockSpec((None, tm // dil, dil * LANES), lambda b, i: (b, i, 0))

    x_spec = pl.BlockSpec((None, tm, d), lambda b, i: (b, i, 0))
    n_slabs = GROUP_W // LANES
    tail_specs, tail_args = _xattn_operands(tail, d)
    return pl.pallas_call(
        _ev_tail_kernel,
        out_shape=jax.ShapeDtypeStruct(x.shape, F32),
        grid=(bn, s_len // tm),
        in_specs=[o_spec(d_) for d_ in DILATIONS] + [l_spec(d_) for d_ in DILATIONS]
        + [o_spec(1),
           _resident(ex.shape, lambda b, i: (0, 0)),
           _resident((2 * GROUP_W, d), lambda b, i: (0, 0)),
           x_spec] + tail_specs,
        out_specs=x_spec,
        scratch_shapes=[pltpu.VMEM((n_slabs, tm, LANES), F32), pltpu.VMEM((n_slabs, tm, LANES), F32),
                        pltpu.VMEM((tm, LANES), F32), pltpu.VMEM((tm, LANES), F32)],
        compiler_params=_cparams(("parallel", "parallel")),
        name="ev_tail",
    )(*outs, *stats, yb, jnp.asarray(ex, BF16), w_bf16, x, *tail_args)


def _od_tail_kernel(yc_ref, yd_ref, w_ref, x_ref, gxa_ref, wq_ref, kv_ref, wo_ref, out_ref):
    half = yc_ref.shape[-1]
    x0 = x_ref[...] + _dot(yc_ref[...], w_ref[0:half, :]) + _dot(yd_ref[...], w_ref[half:, :])
    out_ref[...] = _xattn_block(x0, gxa_ref, wq_ref, kv_ref, wo_ref)


def _od_tail(x, yc_tm, yd, w_bf16, tail):
    bn, s_len, d = x.shape
    tm = min(ROW_TILE_WIDE, s_len)
    c = yd.shape[-1]
    x_spec = pl.BlockSpec((None, tm, d), lambda b, i: (b, i, 0))
    tail_specs, tail_args = _xattn_operands(tail, d)
    return pl.pallas_call(
        _od_tail_kernel,
        out_shape=jax.ShapeDtypeStruct(x.shape, F32),
        grid=(bn, s_len // tm),
        in_specs=[pl.BlockSpec((tm, c), lambda b, i: (i, b)),
                  pl.BlockSpec((None, tm, c), lambda b, i: (b, i, 0)),
                  _resident((2 * c, d), lambda b, i: (0, 0)),
                  x_spec] + tail_specs,
        out_specs=x_spec,
        compiler_params=_cparams(("parallel", "parallel")),
        name="od_tail",
    )(yc_tm, yd, w_bf16, x, *tail_args)


def _norm_matmul_kernel(x_ref, g_ref, w_ref, o_ref):
    o_ref[...] = _dot(_rms(x_ref[...], g_ref[...]).astype(BF16), w_ref[...]).astype(o_ref.dtype)


def _mem_kv(mem, g, w_bf16):
    bn, m_len, d = mem.shape
    n_out = w_bf16.shape[1]
    return pl.pallas_call(
        _norm_matmul_kernel,
        out_shape=jax.ShapeDtypeStruct((bn, m_len, n_out), BF16),
        grid=(bn,),
        in_specs=[pl.BlockSpec((None, m_len, d), lambda b: (b, 0, 0)),
                  pl.BlockSpec((1, d), lambda b: (0, 0)),
                  pl.BlockSpec((d, n_out), lambda b: (0, 0))],
        out_specs=pl.BlockSpec((None, m_len, n_out), lambda b: (b, 0, 0)),
        compiler_params=_cparams(("parallel",)),
        name="mem_kv",
    )(mem, g.reshape(1, d), w_bf16)


def _od_proj_kernel(x_ref, g_ref, w_ref, ug_ref, *rest):
    hg_refs, slab = rest[:-1], rest[-1]
    bn, tq, d = x_ref.shape
    h = _rms(x_ref[...].reshape(bn * tq, d), g_ref[...]).astype(BF16)
    n_ug = 2 * LRU_W
    n_slabs = GROUP_W // LANES
    for c in range(n_ug // GROUP_W):
        acc = _dot(h, w_ref[:, c * GROUP_W:(c + 1) * GROUP_W])
        for b in range(bn):
            for j in range(n_slabs):
                slab[j, pl.ds(b, tq, stride=bn), :] = acc[b * tq:(b + 1) * tq, j * LANES:(j + 1) * LANES]
        for j in range(n_slabs):
            ug_ref[:, c * GROUP_W + j * LANES:c * GROUP_W + (j + 1) * LANES] = slab[j]
    for c, hg_ref in enumerate(hg_refs):
        acc = _dot(h, w_ref[:, n_ug + c * GROUP_W:n_ug + (c + 1) * GROUP_W])
        hg_ref[...] = acc.reshape(bn, tq, GROUP_W).astype(hg_ref.dtype)


HG_INPUT_DTYPES = (("q", BF16), ("f_fw", F32), ("f_bw", F32), ("i_in", BF16), ("g_out", BF16))


def _od_proj(x, g, w_bf16):
    bn, s_len, d = x.shape
    tq = min(ROW_TILE_WIDE // bn, s_len)
    n_ug = 2 * LRU_W
    n_hg = w_bf16.shape[1] - n_ug
    assert n_hg == len(HG_INPUT_DTYPES) * GROUP_W
    hg_spec = pl.BlockSpec((bn, tq, GROUP_W), lambda i: (0, i, 0))
    ug, *hg = pl.pallas_call(
        _od_proj_kernel,
        out_shape=[jax.ShapeDtypeStruct((s_len * bn, n_ug), F32)]
        + [jax.ShapeDtypeStruct((bn, s_len, GROUP_W), dt) for _, dt in HG_INPUT_DTYPES],
        grid=(s_len // tq,),
        in_specs=[pl.BlockSpec((bn, tq, d), lambda i: (0, i, 0)),
                  pl.BlockSpec((1, d), lambda i: (0, 0)),
                  _resident((d, n_ug + n_hg), lambda i: (0, 0))],
        out_specs=[pl.BlockSpec((tq * bn, n_ug), lambda i: (i, 0))] + [hg_spec] * len(HG_INPUT_DTYPES),
        scratch_shapes=[pltpu.VMEM((GROUP_W // LANES, tq * bn, LANES), F32)],
        compiler_params=_cparams(("parallel",)),
        name="od_proj",
    )(x, g.reshape(1, d), w_bf16)
    return ug.reshape(s_len, bn, n_ug), dict(zip([n for n, _ in HG_INPUT_DTYPES], hg))


def _gelu_tanh(x):
    return 0.5 * x * (1.0 + jnp.tanh(math.sqrt(2.0 / math.pi) * (x + 0.044715 * (x * x * x))))


def _rglru_kernel(*refs, reverse, n_chunks):
    if reverse:
        (ul_ref, u_ref, ur_ref, gate_ref, hf_ref, cw_ref, cb_ref, wa_ref, ba_ref, wx_ref, bx_ref, sp_ref,
         out_ref, ext, a_s, b_s, h_s, carry) = refs
    else:
        (ul_ref, u_ref, ur_ref, cw_ref, cb_ref, wa_ref, ba_ref, wx_ref, bx_ref, sp_ref,
         out_ref, ext, a_s, b_s, carry) = refs
        h_s = out_ref
    step = pl.program_id(0)
    chunk = (n_chunks - 1 - step) if reverse else step
    t_len, bn, c = u_ref.shape

    @pl.when(step == 0)
    def _():
        carry[...] = jnp.zeros_like(carry)

    ext[0:2] = jnp.where(chunk == 0, 0.0, ul_ref[...])
    ext[2:t_len + 2] = u_ref[...]
    ext[t_len + 2:t_len + 3] = jnp.where(chunk == n_chunks - 1, 0.0, ur_ref[...])
    uc = cb_ref[...].reshape(1, 1, c)
    for j in range(4):
        uc = uc + cw_ref[j:j + 1, :].reshape(1, 1, c) * ext[j:j + t_len]

    u2 = uc.reshape(t_len * bn, c)
    ub = u2.astype(BF16)
    half = c // 2

    def gate_tanh(w_ref, b_ref):
        z = jnp.concatenate([_dot(ub[:, :half], w_ref[0]), _dot(ub[:, half:], w_ref[1])], axis=-1)
        return jnp.tanh(z + b_ref[...])

    tr = gate_tanh(wa_ref, ba_ref)
    ti = gate_tanh(wx_ref, bx_ref)
    a = jnp.exp2(sp_ref[...] * tr + sp_ref[...])
    b = jnp.sqrt(1.0 - a * a) * ((0.5 * ti + 0.5) * u2)
    a_s[...] = a.reshape(t_len, bn, c)
    b_s[...] = b.reshape(t_len, bn, c)

    def body(k, h):
        t = (t_len - 1 - k) if reverse else k
        h = a_s[t] * h + b_s[t]
        h_s[t] = h
        return h

    carry[...] = lax.fori_loop(0, t_len, body, carry[...], unroll=8)

    if reverse:
        out_ref[...] = ((hf_ref[...] + h_s[...]) * _gelu_tanh(gate_ref[...])).astype(out_ref.dtype)


def _rglru(ug, hf, conv_w, conv_b, wa, ba, wx, bx, lam, bn, *, reverse):
    s_len = ug.shape[0]
    c = LRU_W
    ug3 = ug
    t_len = min(LRU_T, s_len)
    n_chunks = s_len // t_len

    def ck(i):
        return (n_chunks - 1 - i) if reverse else i

    half = c // 2
    nb = LRU_BLOCKS // 2

    def dense_halves(w):
        w = 0.5 * w.astype(F32).reshape(2, nb, c // LRU_BLOCKS, c // LRU_BLOCKS)
        eye = jnp.eye(nb, dtype=F32)
        return jnp.einsum('gnij,nm->gnimj', w, eye).reshape(2, half, half).astype(BF16)

    decay_scale = ((-0.5 * LRU_C * LOG2E) * jax.nn.softplus(-lam.astype(F32))).reshape(1, c)
    vec = pl.BlockSpec((1, c), lambda i: (0, 0))
    wspec = pl.BlockSpec((2, half, half), lambda i: (0, 0, 0))
    blk = pl.BlockSpec((t_len, bn, c), lambda i: (ck(i), 0, 0))
    in_specs = [
        pl.BlockSpec((2, bn, c), lambda i: (jnp.maximum(ck(i) * (t_len // 2) - 1, 0), 0, 0)),
        blk,
        pl.BlockSpec((1, bn, c), lambda i: (jnp.minimum((ck(i) + 1) * t_len, s_len - 1), 0, 0)),
    ]
    args = [ug3, ug3, ug3]
    scratch = [pltpu.VMEM((t_len + 3, bn, c), F32), pltpu.VMEM((t_len, bn, c), F32),
               pltpu.VMEM((t_len, bn, c), F32)]
    if reverse:
        in_specs += [pl.BlockSpec((t_len, bn, c), lambda i: (ck(i), 0, 1)), blk]
        args += [ug3, hf]
        scratch.append(pltpu.VMEM((t_len, bn, c), F32))
    scratch.append(pltpu.VMEM((bn, c), F32))
    in_specs += [pl.BlockSpec((4, c), lambda i: (0, 0)), vec, wspec, vec, wspec, vec, vec]
    args += [conv_w.astype(F32), conv_b.astype(F32).reshape(1, c), dense_halves(wa),
             0.5 * ba.astype(F32).reshape(1, c), dense_halves(wx), 0.5 * bx.astype(F32).reshape(1, c),
             decay_scale]
    return pl.pallas_call(
        functools.partial(_rglru_kernel, reverse=reverse, n_chunks=n_chunks),
        out_shape=jax.ShapeDtypeStruct((s_len, bn, c), BF16 if reverse else F32),
        grid=(n_chunks,),
        in_specs=in_specs,
        out_specs=blk,
        scratch_shapes=scratch,
        compiler_params=_cparams(("arbitrary",)),
        name="rglru_bw" if reverse else "rglru_fw",
    )(*args)


def _hg_matrices(c, reverse):
    n_levels = int(math.log2(c))
    t = np.arange(c)[:, None]
    r = np.arange(c)[None, :]
    mats, masks, upper = [], [], []
    if not reverse:
        mats.append(r <= t)
        mats.append(r > t)
    else:
        mats.append(r >= t)
        mats.append(r < t)
    for lev in range(n_levels):
        half = c >> (lev + 1)
        parent = 2 * half
        start = (np.arange(c) // parent) * parent
        mid = (start + half)[:, None]
        later = (np.arange(c) % parent >= half)[:, None]
        if not reverse:
            m = np.where(later, (r >= mid) & (r <= t), (r > t) & (r < mid))
            is_q = later
        else:
            m = np.where(later, (r >= mid) & (r < t), (r >= t) & (r < mid))
            is_q = ~later
        mats.append(m)
        same_parent = (start[:, None] == start[None, :])
        masks.append(same_parent & is_q & (~is_q).T)
        upper.append(np.broadcast_to(is_q, (c, HG_DK)))
    mat = np.concatenate(mats, axis=0).astype(np.float32)
    return (np.concatenate([mat, mat], axis=1), np.stack(masks).astype(np.float32),
            np.stack(upper).astype(np.float32))


def _hgrn_kernel(*refs, reverse, n_steps, n_levels):
    if reverse:
        (q_ref, f_ref, v_ref, go_ref, of_ref, lb_ref, gn_ref, mat_ref, mask_ref, isq_ref, out_ref,
         state, x_scr, oi_scr, aux_scr, dec_scr, kk_scr) = refs
    else:
        (q_ref, f_ref, v_ref, lb_ref, mat_ref, mask_ref, isq_ref, out_ref,
         state, x_scr, oi_scr, aux_scr, dec_scr, kk_scr) = refs
    step = pl.program_id(1)
    c = HG_CHUNK
    t_len = q_ref.shape[0]
    n_chunks = t_len // c

    @pl.when(step == 0)
    def _():
        state[...] = jnp.zeros_like(state)

    def chunk_body(ci, carry):
        ch = (n_chunks - 1 - ci) if reverse else ci
        rows = slice(ch * c, (ch + 1) * c)
        edge = (c - 1) if not reverse else 0
        for hp in range(HG_HEADS // 2):
            cols2 = slice(2 * hp * HG_DK, 2 * (hp + 1) * HG_DK)
            fl = f_ref[rows, cols2]
            e = jnp.exp(-jnp.abs(fl))
            r = 1.0 / (1.0 + e)
            er = e * r
            pos = fl > 0.0
            lb = lb_ref[0:1, cols2]
            oml = lb_ref[1:2, cols2]
            g2 = jnp.maximum(jnp.log2(lb + oml * jnp.where(pos, r, er)), F32_MIN_EXPONENT)
            kk_scr[hp] = oml * jnp.where(pos, er, r)
            g_hi = g2.astype(BF16)
            g_lo = (g2 - g_hi.astype(F32)).astype(BF16)
            dec_scr[hp] = jnp.exp2(_dot(mat_ref[...], jnp.concatenate([g_hi, g_lo], axis=0)))

        for hd in range(HG_HEADS):
            cols = slice(hd * HG_DK, (hd + 1) * HG_DK)
            pcols = slice((hd % 2) * HG_DK, (hd % 2 + 1) * HG_DK)
            kk = kk_scr[hd // 2, :, pcols]
            dec = dec_scr.at[hd // 2]
            qv = q_ref[rows, cols].astype(F32)
            qq = qv * jax.nn.sigmoid(qv)
            qb, kb = qq.astype(BF16), kk.astype(BF16)
            d_in = dec[0:c, pcols]
            x_scr[hd, n_levels] = (qq * d_in).astype(BF16)
            x_scr[hd, n_levels + 1] = (kk * dec[c:2 * c, pcols]).astype(BF16)
            aux_scr[hd, 0] = jnp.broadcast_to(d_in[edge:edge + 1, :], (c, HG_DK))
            aux_scr[hd, 1] = jnp.broadcast_to(jnp.sum(qq * kk, axis=-1, keepdims=True), (c, HG_DK))
            for lev in range(n_levels):
                half = c >> (lev + 1)
                if half >= SUBLANES_BF16:
                    first_is_q = bool(reverse)
                    base = jnp.concatenate(
                        [(qb if (blk % 2 == 1) != first_is_q else kb)[blk * half:(blk + 1) * half]
                         for blk in range(c // half)], axis=0)
                else:
                    base = jnp.where(isq_ref[lev] > 0.5, qb, kb)
                x_scr[hd, lev] = base * dec[(2 + lev) * c:(3 + lev) * c, pcols].astype(BF16)

        for hd in range(HG_HEADS):
            cols = slice(hd * HG_DK, (hd + 1) * HG_DK)
            att = None
            for lev in range(n_levels):
                xl = x_scr[hd, lev]
                term = mask_ref[lev] * _nt_dot(xl, xl)
                att = term if att is None else att + term
            st = state[hd]
            oi_scr[hd] = _nt_dot(x_scr[hd, n_levels], st.astype(BF16))
            vt = v_ref[rows, cols].astype(F32).T.astype(BF16)
            state[hd] = st * aux_scr[hd, 0] + _dot(vt, x_scr[hd, n_levels + 1])
            x_scr[hd, 0] = att.astype(BF16)

        for hd in range(HG_HEADS):
            cols = slice(hd * HG_DK, (hd + 1) * HG_DK)
            vb = v_ref[rows, cols]
            o = oi_scr[hd] + _dot(x_scr[hd, 0], vb) + aux_scr[hd, 1] * vb.astype(F32)

            if reverse:
                o = o + of_ref[rows, cols]
                o = o * lax.rsqrt(jnp.mean(o * o, axis=-1, keepdims=True) + EPS) * gn_ref[...]
                gv = go_ref[rows, cols].astype(F32)
                out_ref[rows, cols] = (o * (gv * jax.nn.sigmoid(gv))).astype(out_ref.dtype)
            else:
                out_ref[rows, cols] = o
        return carry

    for ci in range(n_chunks):
        chunk_body(ci, 0)


def _hgrn(hg, o_fw, lb, gnorm_g, *, reverse):
    bn, s_len, _ = hg["q"].shape
    w = HG_HEADS * HG_DK
    t_len = min(HG_T, s_len)
    n_steps = s_len // t_len
    mats, masks, isq = _hg_matrices(HG_CHUNK, reverse)
    n_levels = masks.shape[0]
    lbf = lb.astype(F32)
    lb_tab = jnp.stack([lbf, 1.0 - lbf])

    blk = pl.BlockSpec((None, t_len, w), lambda b, i: (b, (n_steps - 1 - i) if reverse else i, 0))

    def const(shape):
        return pl.BlockSpec(shape, lambda b, i: (0,) * len(shape))

    in_specs = [blk, blk, blk]
    args = [hg["q"], hg["f_bw" if reverse else "f_fw"], hg["i_in"]]
    if reverse:
        in_specs += [blk, blk]
        args += [hg["g_out"], o_fw]
    in_specs.append(const((2, w)))
    args.append(lb_tab)
    if reverse:
        in_specs.append(const((1, HG_DK)))
        args.append(gnorm_g.astype(F32).reshape(1, HG_DK))
    in_specs += [const(mats.shape), const(masks.shape), const(isq.shape)]
    args += [jnp.asarray(mats, BF16), jnp.asarray(masks, F32), jnp.asarray(isq, BF16)]
    return pl.pallas_call(
        functools.partial(_hgrn_kernel, reverse=reverse, n_steps=n_steps, n_levels=n_levels),
        out_shape=jax.ShapeDtypeStruct((bn, s_len, w), BF16 if reverse else F32),
        grid=(bn, n_steps),
        in_specs=in_specs,
        out_specs=blk,
        scratch_shapes=[pltpu.VMEM((HG_HEADS, HG_DK, HG_DK), F32),
                        pltpu.VMEM((HG_HEADS, n_levels + 2, HG_CHUNK, HG_DK), BF16),
                        pltpu.VMEM((HG_HEADS, HG_CHUNK, HG_DK), F32),
                        pltpu.VMEM((HG_HEADS, 2, HG_CHUNK, HG_DK), F32),
                        pltpu.VMEM((HG_HEADS // 2, (n_levels + 2) * HG_CHUNK, 2 * HG_DK), F32),
                        pltpu.VMEM((HG_HEADS // 2, HG_CHUNK, 2 * HG_DK), F32)],
        compiler_params=_cparams(("parallel", "arbitrary")),
        name="hgrn_bw" if reverse else "hgrn_fw",
    )(*args)


def _even_layer(x, g, w_in, w_out, rpb, tail):
    qkv, view4, view16 = _ev_proj(x, g, w_in.astype(BF16))
    outs, stats = zip(*[_dilated_branch(view, dil) for dil, view in zip(DILATIONS, (qkv, view4, view16))])
    yb = _neighborhood(qkv, rpb)
    return _ev_tail(x, outs, stats, yb, w_out.astype(BF16), tail)


def _odd_layer(x, g, w_in, w_out, conv_w, conv_b, wa, ba, wx, bx, lam, lb_f, lb_b, gnorm_g, tail):
    bn, s_len, _ = x.shape
    ug, hg = _od_proj(x, g, w_in.astype(BF16))
    h_fw = _rglru(ug, None, conv_w, conv_b, wa[0], ba[0], wx[0], bx[0], lam[0], bn, reverse=False)
    yc = _rglru(ug, h_fw, conv_w, conv_b, wa[1], ba[1], wx[1], bx[1], lam[1], bn, reverse=True)
    o_fw = _hgrn(hg, None, lb_f, gnorm_g, reverse=False)
    yd = _hgrn(hg, o_fw, lb_b, gnorm_g, reverse=True)
    return _od_tail(x, yc.reshape(s_len, bn * LRU_W), yd, w_out.astype(BF16), tail)


def kernel(x, mem, norm_mix_g, norm_xa_g, norm_mem_g, norm_mlp_g, final_norm_g, ev_w_in, ev_w_out, na_rpb,
           od_w_in, od_w_out, conv_w, conv_b, lru_wa, lru_ba, lru_wx, lru_bx, lru_lambda, hgrn_lb_logits,
           hgrn_norm_g, xa_wq, xa_wkv, xa_wo, mlp_w1, mlp_w2):
    depth = norm_mix_g.shape[0]
    p_lb = jax.nn.softmax(hgrn_lb_logits.astype(F32), axis=0)
    lower_bounds = jnp.cumsum(p_lb, axis=0) - p_lb[0:1]
    for layer in range(depth):
        tail = dict(g_xa=norm_xa_g[layer], wq=xa_wq[layer].astype(BF16),
                    kv=_mem_kv(mem, norm_mem_g[layer], xa_wkv[layer].astype(BF16)),
                    wo=xa_wo[layer].astype(BF16), g_mlp=norm_mlp_g[layer],
                    w1=mlp_w1[layer].astype(BF16), w2=mlp_w2[layer].astype(BF16),
                    g_final=final_norm_g if layer == depth - 1 else None)
        if layer % 2 == 0:
            e = layer // 2
            x = _even_layer(x, norm_mix_g[layer], ev_w_in[e], ev_w_out[e], na_rpb[e], tail)
        else:
            o = layer // 2
            x = _odd_layer(x, norm_mix_g[layer], od_w_in[o], od_w_out[o], conv_w[o], conv_b[o],
                           lru_wa[o], lru_ba[o], lru_wx[o], lru_bx[o], lru_lambda[o],
                           lower_bounds[layer, 0], lower_bounds[layer, 1], hgrn_norm_g[o], tail)
        x = _mlp(x, tail)
    return x
```

```python
import functools
import math

import jax
import jax.numpy as jnp
import numpy as np
from jax import lax
from jax.experimental import pallas as pl
from jax.experimental.pallas import tpu as pltpu

F32 = jnp.float32
BF16 = jnp.bfloat16

D_MODEL = 1024
HEAD_DIM = 64
ROT_DIM = 16
ROPE_THETA = 500000.0
N_HEADS_A = 8
N_HEADS_B = 8
GROUP_W = N_HEADS_A * HEAD_DIM
DILATIONS = (1, 4, 16)
WIN_HALF = 64
GRID_W = 64
NA_ROWS = 8
NA_COLS = 16
LRU_W = 512
LRU_BLOCKS = 8
LRU_C = 8.0
HG_HEADS = 4
HG_DK = 128
XA_HEADS = 4
XA_DH = 256
D_FF = 4096
EPS = 1e-6
LOG2E = math.log2(math.e)
F32_MIN_EXPONENT = -126.0

LANES = 128
SUBLANES_BF16 = 16
V7X_VMEM_LIMIT_BYTES = 56 * 1024 * 1024

ROW_TILE_WIDE = 1024
ATT_Q_BLOCK = 512
ATT_SUB = 128
NA_ROW_GROUP = 16
LRU_T = 256
HG_CHUNK = 128
HG_T = 1024


def _cparams(sem):
    return pltpu.CompilerParams(dimension_semantics=sem, vmem_limit_bytes=V7X_VMEM_LIMIT_BYTES)


def _resident(shape, index_map):
    return pl.BlockSpec(shape, index_map, pipeline_mode=pl.Buffered(1))


def _rms(x, g):
    return x * lax.rsqrt(jnp.mean(x * x, axis=-1, keepdims=True) + EPS) * g


def _nt_dot(a, b):
    return lax.dot_general(a, b, (((1,), (1,)), ((), ())), preferred_element_type=F32)


def _dot(a, b):
    return jnp.dot(a, b, preferred_element_type=F32)


def _head_selectors():
    lane = lax.broadcasted_iota(jnp.int32, (1, LANES), 1)
    low = (lane < HEAD_DIM).astype(F32)
    return low.astype(BF16), (1.0 - low).astype(BF16)


def _ev_proj_kernel(x_ref, g_ref, w_ref, rc_ref, rs1_ref, rs2_ref, o_ref, o4_ref, o16_ref, slab):
    h = _rms(x_ref[...], g_ref[...]).astype(BF16)
    scale = HEAD_DIM ** -0.5 * LOG2E
    tm = x_ref.shape[0]
    n_slabs = GROUP_W // LANES
    for c in range(6):
        acc = _dot(h, w_ref[:, c * GROUP_W:(c + 1) * GROUP_W])
        if c in (0, 1):
            rc, rs1, rs2 = rc_ref[...], rs1_ref[...], rs2_ref[...]
            parts = []
            for j in range(n_slabs):
                t = acc[:, j * LANES:(j + 1) * LANES]
                t = t * rc + pltpu.roll(t, LANES - ROT_DIM // 2, 1) * rs1 + pltpu.roll(t, ROT_DIM // 2, 1) * rs2
                parts.append(t)
            acc = jnp.concatenate(parts, axis=-1)
        if c in (0, 3):
            acc = acc * scale
        o_ref[:, c * GROUP_W:(c + 1) * GROUP_W] = acc.astype(BF16)
        if c < 3:
            for j in range(n_slabs):
                slab[j] = acc[:, j * LANES:(j + 1) * LANES]
            for dil, dst in ((DILATIONS[1], o4_ref), (DILATIONS[2], o16_ref)):
                for r in range(dil):
                    for j in range(n_slabs):
                        col = r * 3 * GROUP_W + c * GROUP_W + j * LANES
                        dst[:, col:col + LANES] = slab[j, pl.ds(r, tm // dil, stride=dil), :].astype(BF16)


def _rope_tables(s_len):
    half = ROT_DIM // 2
    inv = jnp.asarray(ROPE_THETA ** (-np.arange(half) * 2.0 / ROT_DIM), F32)
    ang = jnp.arange(s_len, dtype=F32)[:, None] * inv[None, :]
    cos, sin = jnp.cos(ang), jnp.sin(ang)
    ones = jnp.ones((s_len, HEAD_DIM - ROT_DIM), F32)
    zeros = jnp.zeros((s_len, HEAD_DIM - ROT_DIM), F32)
    zh = jnp.zeros((s_len, half), F32)
    rc = jnp.concatenate([cos, cos, ones], axis=-1)
    rs1 = jnp.concatenate([-sin, zh, zeros], axis=-1)
    rs2 = jnp.concatenate([zh, sin, zeros], axis=-1)
    rep = LANES // HEAD_DIM
    return jnp.tile(rc, (1, rep)), jnp.tile(rs1, (1, rep)), jnp.tile(rs2, (1, rep))


def _ev_proj(x, g, w_bf16):
    bn, s_len, d = x.shape
    tm = min(ROW_TILE_WIDE, s_len)
    n_out = w_bf16.shape[1]
    rc, rs1, rs2 = _rope_tables(s_len)
    tbl_spec = pl.BlockSpec((tm, LANES), lambda b, i: (i, 0))
    wa = 3 * GROUP_W
    d4, d16 = DILATIONS[1], DILATIONS[2]
    return pl.pallas_call(
        _ev_proj_kernel,
        out_shape=(jax.ShapeDtypeStruct((bn, s_len, n_out), BF16),
                   jax.ShapeDtypeStruct((bn, s_len // d4, d4 * wa), BF16),
                   jax.ShapeDtypeStruct((bn, s_len // d16, d16 * wa), BF16)),
        grid=(bn, s_len // tm),
        in_specs=[
            pl.BlockSpec((None, tm, d), lambda b, i: (b, i, 0)),
            pl.BlockSpec((1, d), lambda b, i: (0, 0)),
            _resident((d, n_out), lambda b, i: (0, 0)),
            tbl_spec, tbl_spec, tbl_spec,
        ],
        out_specs=(pl.BlockSpec((None, tm, n_out), lambda b, i: (b, i, 0)),
                   pl.BlockSpec((None, tm // d4, d4 * wa), lambda b, i: (b, i, 0)),
                   pl.BlockSpec((None, tm // d16, d16 * wa), lambda b, i: (b, i, 0))),
        scratch_shapes=[pltpu.VMEM((GROUP_W // LANES, tm, LANES), F32)],
        compiler_params=_cparams(("parallel", "parallel")),
        name="ev_proj",
    )(x, g.reshape(1, d), w_bf16, rc, rs1, rs2)


def _dilated_kernel(q_ref, kl_ref, kc_ref, kr_ref, vl_ref, vc_ref, vr_ref, o_ref, stat_ref,
                    kext, vext, s_scr, p_scr, *, lq, l_total):
    i = pl.program_id(2)
    kext[0:WIN_HALF, :] = kl_ref[...]
    kext[WIN_HALF:WIN_HALF + lq, :] = kc_ref[...]
    kext[WIN_HALF + lq:, :] = kr_ref[...]
    vext[0:WIN_HALF, :] = vl_ref[...]
    vext[WIN_HALF:WIN_HALF + lq, :] = vc_ref[...]
    vext[WIN_HALF + lq:, :] = vr_ref[...]

    wk = ATT_SUB + 2 * WIN_HALF
    qi = lax.broadcasted_iota(jnp.int32, (ATT_SUB, wk), 0)
    ci = lax.broadcasted_iota(jnp.int32, (ATT_SUB, wk), 1)
    band_bias = jnp.where((ci - qi >= 0) & (ci - qi <= 2 * WIN_HALF), 0.0, -jnp.inf).astype(F32)
    crow = lax.broadcasted_iota(jnp.int32, (1, wk), 1)
    lane = lax.broadcasted_iota(jnp.int32, (ATT_SUB, LANES), 1)
    low_half = lane < HEAD_DIM
    sel_lo, sel_hi = _head_selectors()
    n_sub = lq // ATT_SUB
    n_hp = GROUP_W // LANES

    for j in range(n_sub):
        base = i * lq + j * ATT_SUB - WIN_HALF
        in_seq = (crow + base >= 0) & (crow + base < l_total)
        bias = band_bias + jnp.where(in_seq, 0.0, -jnp.inf).astype(F32)
        bias2 = jnp.concatenate([bias, bias], axis=0)
        rows = slice(j * ATT_SUB, (j + 1) * ATT_SUB)
        wrows = slice(j * ATT_SUB, j * ATT_SUB + wk)
        for hp in range(n_hp):
            cols = slice(hp * LANES, (hp + 1) * LANES)
            qp = q_ref[rows, cols]
            q2 = jnp.concatenate([qp * sel_lo, qp * sel_hi], axis=0)
            s_scr[j * n_hp + hp] = _nt_dot(q2, kext[wrows, cols]) + bias2
    for j in range(n_sub):
        stat = jnp.ones((ATT_SUB, LANES), F32)
        for hp in range(n_hp):
            t = j * n_hp + hp
            s = s_scr[t]
            m = jnp.max(s, axis=-1, keepdims=True)
            p = jnp.exp2(s - m)
            den = jnp.sum(p, axis=-1, keepdims=True)
            p_scr[t] = p.astype(BF16)
            stat = jnp.where(lane == 2 * hp, m[0:ATT_SUB], jnp.where(lane == 2 * hp + 1, m[ATT_SUB:], stat))
            stat = jnp.where(lane == N_HEADS_A + 2 * hp, den[0:ATT_SUB],
                             jnp.where(lane == N_HEADS_A + 2 * hp + 1, den[ATT_SUB:], stat))
        stat_ref[j * ATT_SUB:(j + 1) * ATT_SUB, :] = stat
    for j in range(n_sub):
        rows = slice(j * ATT_SUB, (j + 1) * ATT_SUB)
        wrows = slice(j * ATT_SUB, j * ATT_SUB + wk)
        for hp in range(n_hp):
            cols = slice(hp * LANES, (hp + 1) * LANES)
            o = _dot(p_scr[j * n_hp + hp], vext[wrows, cols])
            o_ref[rows, cols] = jnp.where(low_half, o[0:ATT_SUB], o[ATT_SUB:]).astype(BF16)


def _dilated_branch(view, dil):
    bn, l_total, width = view.shape
    n_groups = width // (dil * GROUP_W)
    lq = min(ATT_Q_BLOCK, l_total)
    nblk = l_total // lq
    per = lq // WIN_HALF
    n_halo = l_total // WIN_HALF
    n_units = (lq // ATT_SUB) * (GROUP_W // LANES)

    def cur(g):
        return pl.BlockSpec((None, lq, GROUP_W), lambda b, r, i: (b, i, r * n_groups + g))

    def left(g):
        return pl.BlockSpec((None, WIN_HALF, GROUP_W),
                            lambda b, r, i: (b, jnp.maximum(i * per - 1, 0), r * n_groups + g))

    def right(g):
        return pl.BlockSpec((None, WIN_HALF, GROUP_W),
                            lambda b, r, i: (b, jnp.minimum((i + 1) * per, n_halo - 1), r * n_groups + g))

    stat_shape = jax.ShapeDtypeStruct((bn, l_total, dil * LANES), F32)
    stat_spec = pl.BlockSpec((None, lq, LANES), lambda b, r, i: (b, i, r))
    return pl.pallas_call(
        functools.partial(_dilated_kernel, lq=lq, l_total=l_total),
        out_shape=(jax.ShapeDtypeStruct((bn, l_total, dil * GROUP_W), BF16), stat_shape),
        grid=(bn, dil, nblk),
        in_specs=[cur(0), left(1), cur(1), right(1), left(2), cur(2), right(2)],
        out_specs=(pl.BlockSpec((None, lq, GROUP_W), lambda b, r, i: (b, i, r)), stat_spec),
        scratch_shapes=[pltpu.VMEM((lq + 2 * WIN_HALF, GROUP_W), BF16),
                        pltpu.VMEM((lq + 2 * WIN_HALF, GROUP_W), BF16),
                        pltpu.VMEM((n_units, 2 * ATT_SUB, ATT_SUB + 2 * WIN_HALF), F32),
                        pltpu.VMEM((n_units, 2 * ATT_SUB, ATT_SUB + 2 * WIN_HALF), BF16)],
        compiler_params=_cparams(("parallel", "parallel", "parallel")),
        name=f"dilated_attn_d{dil}",
    )(view, view, view, view, view, view, view)


NA_QT = NA_ROW_GROUP * GRID_W
NA_KT = 2 * NA_QT
NA_KW = NA_ROWS * GRID_W


def _na_tables(rpb):
    rpb = rpb.astype(F32)
    nh = rpb.shape[0]
    ext = GRID_W - NA_COLS
    padded = jnp.concatenate([jnp.repeat(rpb[..., :1], ext, axis=-1), rpb,
                              jnp.repeat(rpb[..., -1:], ext, axis=-1)], axis=-1)
    skew = jnp.tile(padded, (1, 1, GRID_W + 1))[..., :2 * GRID_W * GRID_W]
    t1 = skew.reshape(nh, -1, GRID_W, 2 * GRID_W)[..., ::-1, :GRID_W]
    qc = np.arange(GRID_W)[:, None]
    kc = np.arange(GRID_W)[None, :]
    c0 = np.clip(qc - NA_COLS // 2, 0, GRID_W - NA_COLS)
    t1 = jnp.where((kc >= c0) & (kc < c0 + NA_COLS), t1 * LOG2E, -jnp.inf)
    per_delta = [jnp.transpose(t1[:, NA_ROWS - 1 - delta:2 * NA_ROWS - 1 - delta], (0, 2, 1, 3))
                 .reshape(nh, GRID_W, NA_KW) for delta in range(NA_ROWS)]
    return jnp.stack(per_delta).reshape(NA_ROWS, nh // 2, 2 * GRID_W, NA_KW)


def _na_kernel(q_ref, k_ref, v_ref, tb_ref, o_ref, s_scr, p_scr, inv_scr, *, rows):
    i0 = pl.program_id(1) * NA_ROW_GROUP
    w0 = jnp.clip(i0 - NA_ROWS // 2, 0, rows - 2 * NA_ROW_GROUP)
    lane = lax.broadcasted_iota(jnp.int32, (GRID_W, LANES), 1)
    low_half = lane < HEAD_DIM
    sel_lo, sel_hi = _head_selectors()
    n_hp = GROUP_W // LANES

    def key_rows(a):
        r0 = jnp.clip(i0 + a - NA_ROWS // 2, 0, rows - NA_ROWS)
        return pl.ds(pl.multiple_of((r0 - w0) * GRID_W, GRID_W), NA_KW), i0 + a - r0

    for a in range(NA_ROW_GROUP):
        krows, delta = key_rows(a)
        qrows = slice(a * GRID_W, (a + 1) * GRID_W)
        for hp in range(n_hp):
            cols = slice(hp * LANES, (hp + 1) * LANES)
            qp = q_ref[qrows, cols]
            q2 = jnp.concatenate([qp * sel_lo, qp * sel_hi], axis=0)
            s_scr[a * n_hp + hp] = _nt_dot(q2, k_ref[0, krows, cols]) + tb_ref[delta, hp]
    for t in range(NA_ROW_GROUP * n_hp):
        s = s_scr[t]
        p = jnp.exp2(s - jnp.max(s, axis=-1, keepdims=True))
        inv_scr[t] = jnp.broadcast_to(1.0 / jnp.sum(p, axis=-1, keepdims=True), (2 * GRID_W, LANES))
        p_scr[t] = p.astype(BF16)
    for a in range(NA_ROW_GROUP):
        krows, _ = key_rows(a)
        qrows = slice(a * GRID_W, (a + 1) * GRID_W)
        for hp in range(n_hp):
            cols = slice(hp * LANES, (hp + 1) * LANES)
            t = a * n_hp + hp
            o = _dot(p_scr[t], v_ref[0, krows, cols]) * inv_scr[t]
            o_ref[qrows, cols] = jnp.where(low_half, o[0:GRID_W], o[GRID_W:]).astype(BF16)


def _neighborhood(qkv, rpb):
    bn, s_len, _ = qkv.shape
    rows = s_len // GRID_W
    n_groups = rows // NA_ROW_GROUP
    tables = _na_tables(rpb)

    def window(g):
        return pl.BlockSpec(
            (pl.Element(1), pl.Element(NA_KT), pl.Element(GROUP_W)),
            lambda b, i: (b, jnp.clip(i * NA_ROW_GROUP - NA_ROWS // 2, 0, rows - 2 * NA_ROW_GROUP) * GRID_W,
                          g * GROUP_W))

    return pl.pallas_call(
        functools.partial(_na_kernel, rows=rows),
        out_shape=jax.ShapeDtypeStruct((bn, s_len, GROUP_W), BF16),
        grid=(bn, n_groups),
        in_specs=[pl.BlockSpec((None, NA_QT, GROUP_W), lambda b, i: (b, i, 3)), window(4), window(5),
                  pl.BlockSpec(tables.shape, lambda b, i: (0, 0, 0, 0))],
        out_specs=pl.BlockSpec((None, NA_QT, GROUP_W), lambda b, i: (b, i, 0)),
        scratch_shapes=[pltpu.VMEM((NA_ROW_GROUP * GROUP_W // LANES, 2 * GRID_W, NA_KW), F32),
                        pltpu.VMEM((NA_ROW_GROUP * GROUP_W // LANES, 2 * GRID_W, NA_KW), BF16),
                        pltpu.VMEM((NA_ROW_GROUP * GROUP_W // LANES, 2 * GRID_W, LANES), F32)],
        compiler_params=_cparams(("parallel", "parallel")),
        name="neighborhood_attn",
    )(qkv, qkv, qkv, tables)


def _xattn_block(x, g_ref, wq_ref, kv_ref, wo_ref):
    q = (_dot(_rms(x, g_ref[...]).astype(BF16), wq_ref[...]) * (XA_DH ** -0.5 * LOG2E)).astype(BF16)
    heads = []
    for hd in range(XA_HEADS):
        cols = slice(hd * XA_DH, (hd + 1) * XA_DH)
        s = _nt_dot(q[:, cols], kv_ref[:, cols])
        m = jnp.max(s, axis=-1, keepdims=True)
        p = jnp.exp2(s - m)
        den = jnp.sum(p, axis=-1, keepdims=True)
        vcols = slice(D_MODEL + hd * XA_DH, D_MODEL + (hd + 1) * XA_DH)
        heads.append((_dot(p.astype(BF16), kv_ref[:, vcols]) * (1.0 / den)).astype(BF16))
    return x + _dot(jnp.concatenate(heads, axis=-1), wo_ref[...])


def _mlp_block(x, g_ref, w1_ref, w2_ref, gf_ref, final_norm):
    h = _rms(x, g_ref[...]).astype(BF16)
    acc = x
    for c in range(D_FF // D_MODEL):
        cols = slice(c * D_MODEL, (c + 1) * D_MODEL)
        a = jnp.maximum(_dot(h, w1_ref[:, cols]), 0.0)
        acc = acc + _dot((a * a).astype(BF16), w2_ref[cols, :])
    return _rms(acc, gf_ref[...]) if final_norm else acc


def _xattn_operands(tail, d):
    m_len = tail["kv"].shape[1]
    specs = [pl.BlockSpec((1, d), lambda b, i: (0, 0)), _resident((d, d), lambda b, i: (0, 0)),
             pl.BlockSpec((None, m_len, 2 * d), lambda b, i: (b, 0, 0)),
             _resident((d, d), lambda b, i: (0, 0))]
    return specs, [tail["g_xa"].reshape(1, d), tail["wq"], tail["kv"], tail["wo"]]


def _mlp_kernel(x_ref, g_ref, w1_ref, w2_ref, gf_ref, out_ref, *, final_norm):
    out_ref[...] = _mlp_block(x_ref[...], g_ref, w1_ref, w2_ref, gf_ref, final_norm)


def _mlp(x, tail):
    bn, s_len, d = x.shape
    tm = min(ROW_TILE_WIDE, s_len)
    final_norm = tail["g_final"] is not None
    gf = tail["g_final"] if final_norm else tail["g_mlp"]
    x_spec = pl.BlockSpec((None, tm, d), lambda b, i: (b, i, 0))
    vec = pl.BlockSpec((1, d), lambda b, i: (0, 0))
    return pl.pallas_call(
        functools.partial(_mlp_kernel, final_norm=final_norm),
        out_shape=jax.ShapeDtypeStruct(x.shape, F32),
        grid=(bn, s_len // tm),
        in_specs=[x_spec, vec, _resident((d, D_FF), lambda b, i: (0, 0)),
                  _resident((D_FF, d), lambda b, i: (0, 0)), vec],
        out_specs=x_spec,
        compiler_params=_cparams(("parallel", "parallel")),
        name="mlp_final" if final_norm else "mlp",
    )(x, tail["g_mlp"].reshape(1, d), tail["w1"], tail["w2"], gf.reshape(1, d))


def _ev_tail_kernel(o1, o2v, o3v, s1, s2v, s3v, yb_ref, ex_ref, w_ref, x_ref,
                    gxa_ref, wq_ref, kv_ref, wo_ref, out_ref, so2, so3, st2, st3):
    tm = x_ref.shape[0]
    n_slabs = GROUP_W // LANES
    for dil, ov, sv, so, st in ((DILATIONS[1], o2v, s2v, so2, st2), (DILATIONS[2], o3v, s3v, so3, st3)):
        for r in range(dil):
            dst = pl.ds(r, tm // dil, stride=dil)
            for j in range(n_slabs):
                col = r * GROUP_W + j * LANES
                so[j, dst, :] = ov[:, col:col + LANES].astype(F32)
            st[dst, :] = sv[:, r * LANES:(r + 1) * LANES]
    o2 = jnp.concatenate([so2[j] for j in range(n_slabs)], axis=-1)
    o3 = jnp.concatenate([so3[j] for j in range(n_slabs)], axis=-1)
    ma, mb, mc = s1[...], st2[...], st3[...]
    da, db, dc = [pltpu.roll(t, LANES - N_HEADS_A, 1) for t in (ma, mb, mc)]
    m = jnp.maximum(jnp.maximum(ma, mb), mc)
    ea, eb, ec = jnp.exp2(ma - m), jnp.exp2(mb - m), jnp.exp2(mc - m)
    inv = 1.0 / (ea * da + eb * db + ec * dc)
    lane = lax.broadcasted_iota(jnp.int32, (tm, LANES), 1)
    wcat = jnp.where(lane < N_HEADS_A, ea * inv,
                     jnp.where(lane < 2 * N_HEADS_A, pltpu.roll(eb * inv, N_HEADS_A, 1),
                               pltpu.roll(ec * inv, 2 * N_HEADS_A, 1)))
    hi = wcat.astype(BF16)
    lo = (wcat - hi.astype(F32)).astype(BF16)
    wide = _dot(jnp.concatenate([hi, lo], axis=1), ex_ref[...])
    ya = (wide[:, 0:GROUP_W] * o1[...].astype(F32) + wide[:, GROUP_W:2 * GROUP_W] * o2
          + wide[:, 2 * GROUP_W:] * o3)
    y = _dot(ya.astype(BF16), w_ref[0:GROUP_W, :]) + _dot(yb_ref[...], w_ref[GROUP_W:, :])
    out_ref[...] = _xattn_block(x_ref[...] + y, gxa_ref, wq_ref, kv_ref, wo_ref)


def _ev_tail(x, outs, stats, yb, w_bf16, tail):
    bn, s_len, d = x.shape
    tm = min(ROW_TILE_WIDE, s_len)
    n_br = len(DILATIONS)
    ex = np.zeros((LANES, n_br * GROUP_W), np.float32)
    for i in range(n_br):
        for h in range(N_HEADS_A):
            ex[i * N_HEADS_A + h, i * GROUP_W + h * HEAD_DIM:i * GROUP_W + (h + 1) * HEAD_DIM] = 1.0
    ex = np.concatenate([ex, ex], axis=0)

    def o_spec(dil):
        return pl.BlockSpec((None, tm // dil, dil * GROUP_W), lambda b, i: (b, i, 0))

    def l_spec(dil):
        return pl.BlockSpec((None, tm // dil, dil * LANES), lambda b, i: (b, i, 0))

    x_spec = pl.BlockSpec((None, tm, d), lambda b, i: (b, i, 0))
    n_slabs = GROUP_W // LANES
    tail_specs, tail_args = _xattn_operands(tail, d)
    return pl.pallas_call(
        _ev_tail_kernel,
        out_shape=jax.ShapeDtypeStruct(x.shape, F32),
        grid=(bn, s_len // tm),
        in_specs=[o_spec(d_) for d_ in DILATIONS] + [l_spec(d_) for d_ in DILATIONS]
        + [o_spec(1),
           _resident(ex.shape, lambda b, i: (0, 0)),
           _resident((2 * GROUP_W, d), lambda b, i: (0, 0)),
           x_spec] + tail_specs,
        out_specs=x_spec,
        scratch_shapes=[pltpu.VMEM((n_slabs, tm, LANES), F32), pltpu.VMEM((n_slabs, tm, LANES), F32),
                        pltpu.VMEM((tm, LANES), F32), pltpu.VMEM((tm, LANES), F32)],
        compiler_params=_cparams(("parallel", "parallel")),
        name="ev_tail",
    )(*outs, *stats, yb, jnp.asarray(ex, BF16), w_bf16, x, *tail_args)


def _od_tail_kernel(yc_ref, yd_ref, w_ref, x_ref, gxa_ref, wq_ref, kv_ref, wo_ref, out_ref):
    half = yc_ref.shape[-1]
    x0 = x_ref[...] + _dot(yc_ref[...], w_ref[0:half, :]) + _dot(yd_ref[...], w_ref[half:, :])
    out_ref[...] = _xattn_block(x0, gxa_ref, wq_ref, kv_ref, wo_ref)


def _od_tail(x, yc_tm, yd, w_bf16, tail):
    bn, s_len, d = x.shape
    tm = min(ROW_TILE_WIDE, s_len)
    c = yd.shape[-1]
    x_spec = pl.BlockSpec((None, tm, d), lambda b, i: (b, i, 0))
    tail_specs, tail_args = _xattn_operands(tail, d)
    return pl.pallas_call(
        _od_tail_kernel,
        out_shape=jax.ShapeDtypeStruct(x.shape, F32),
        grid=(bn, s_len // tm),
        in_specs=[pl.BlockSpec((tm, c), lambda b, i: (i, b)),
                  pl.BlockSpec((None, tm, c), lambda b, i: (b, i, 0)),
                  _resident((2 * c, d), lambda b, i: (0, 0)),
                  x_spec] + tail_specs,
        out_specs=x_spec,
        compiler_params=_cparams(("parallel", "parallel")),
        name="od_tail",
    )(yc_tm, yd, w_bf16, x, *tail_args)


def _norm_matmul_kernel(x_ref, g_ref, w_ref, o_ref):
    o_ref[...] = _dot(_rms(x_ref[...], g_ref[...]).astype(BF16), w_ref[...]).astype(o_ref.dtype)


def _mem_kv(mem, g, w_bf16):
    bn, m_len, d = mem.shape
    n_out = w_bf16.shape[1]
    return pl.pallas_call(
        _norm_matmul_kernel,
        out_shape=jax.ShapeDtypeStruct((bn, m_len, n_out), BF16),
        grid=(bn,),
        in_specs=[pl.BlockSpec((None, m_len, d), lambda b: (b, 0, 0)),
                  pl.BlockSpec((1, d), lambda b: (0, 0)),
                  pl.BlockSpec((d, n_out), lambda b: (0, 0))],
        out_specs=pl.BlockSpec((None, m_len, n_out), lambda b: (b, 0, 0)),
        compiler_params=_cparams(("parallel",)),
        name="mem_kv",
    )(mem, g.reshape(1, d), w_bf16)


def _od_proj_kernel(x_ref, g_ref, w_ref, ug_ref, *rest):
    hg_refs, slab = rest[:-1], rest[-1]
    bn, tq, d = x_ref.shape
    h = _rms(x_ref[...].reshape(bn * tq, d), g_ref[...]).astype(BF16)
    n_ug = 2 * LRU_W
    n_slabs = GROUP_W // LANES
    for c in range(n_ug // GROUP_W):
        acc = _dot(h, w_ref[:, c * GROUP_W:(c + 1) * GROUP_W])
        for b in range(bn):
            for j in range(n_slabs):
                slab[j, pl.ds(b, tq, stride=bn), :] = acc[b * tq:(b + 1) * tq, j * LANES:(j + 1) * LANES]
        for j in range(n_slabs):
            ug_ref[:, c * GROUP_W + j * LANES:c * GROUP_W + (j + 1) * LANES] = slab[j]
    for c, hg_ref in enumerate(hg_refs):
        acc = _dot(h, w_ref[:, n_ug + c * GROUP_W:n_ug + (c + 1) * GROUP_W])
        hg_ref[...] = acc.reshape(bn, tq, GROUP_W).astype(hg_ref.dtype)


HG_INPUT_DTYPES = (("q", BF16), ("f_fw", F32), ("f_bw", F32), ("i_in", BF16), ("g_out", BF16))


def _od_proj(x, g, w_bf16):
    bn, s_len, d = x.shape
    tq = min(ROW_TILE_WIDE // bn, s_len)
    n_ug = 2 * LRU_W
    n_hg = w_bf16.shape[1] - n_ug
    assert n_hg == len(HG_INPUT_DTYPES) * GROUP_W
    hg_spec = pl.BlockSpec((bn, tq, GROUP_W), lambda i: (0, i, 0))
    ug, *hg = pl.pallas_call(
        _od_proj_kernel,
        out_shape=[jax.ShapeDtypeStruct((s_len * bn, n_ug), F32)]
        + [jax.ShapeDtypeStruct((bn, s_len, GROUP_W), dt) for _, dt in HG_INPUT_DTYPES],
        grid=(s_len // tq,),
        in_specs=[pl.BlockSpec((bn, tq, d), lambda i: (0, i, 0)),
                  pl.BlockSpec((1, d), lambda i: (0, 0)),
                  _resident((d, n_ug + n_hg), lambda i: (0, 0))],
        out_specs=[pl.BlockSpec((tq * bn, n_ug), lambda i: (i, 0))] + [hg_spec] * len(HG_INPUT_DTYPES),
        scratch_shapes=[pltpu.VMEM((GROUP_W // LANES, tq * bn, LANES), F32)],
        compiler_params=_cparams(("parallel",)),
        name="od_proj",
    )(x, g.reshape(1, d), w_bf16)
    return ug.reshape(s_len, bn, n_ug), dict(zip([n for n, _ in HG_INPUT_DTYPES], hg))


def _gelu_tanh(x):
    return 0.5 * x * (1.0 + jnp.tanh(math.sqrt(2.0 / math.pi) * (x + 0.044715 * (x * x * x))))


def _rglru_kernel(*refs, reverse, n_chunks):
    if reverse:
        (ul_ref, u_ref, ur_ref, gate_ref, hf_ref, cw_ref, cb_ref, wa_ref, ba_ref, wx_ref, bx_ref, sp_ref,
         out_ref, ext, a_s, b_s, h_s, carry) = refs
    else:
        (ul_ref, u_ref, ur_ref, cw_ref, cb_ref, wa_ref, ba_ref, wx_ref, bx_ref, sp_ref,
         out_ref, ext, a_s, b_s, carry) = refs
        h_s = out_ref
    step = pl.program_id(0)
    chunk = (n_chunks - 1 - step) if reverse else step
    t_len, bn, c = u_ref.shape

    @pl.when(step == 0)
    def _():
        carry[...] = jnp.zeros_like(carry)

    ext[0:2] = jnp.where(chunk == 0, 0.0, ul_ref[...])
    ext[2:t_len + 2] = u_ref[...]
    ext[t_len + 2:t_len + 3] = jnp.where(chunk == n_chunks - 1, 0.0, ur_ref[...])
    uc = cb_ref[...].reshape(1, 1, c)
    for j in range(4):
        uc = uc + cw_ref[j:j + 1, :].reshape(1, 1, c) * ext[j:j + t_len]

    u2 = uc.reshape(t_len * bn, c)
    ub = u2.astype(BF16)
    half = c // 2

    def gate_tanh(w_ref, b_ref):
        z = jnp.concatenate([_dot(ub[:, :half], w_ref[0]), _dot(ub[:, half:], w_ref[1])], axis=-1)
        return jnp.tanh(z + b_ref[...])

    tr = gate_tanh(wa_ref, ba_ref)
    ti = gate_tanh(wx_ref, bx_ref)
    a = jnp.exp2(sp_ref[...] * tr + sp_ref[...])
    b = jnp.sqrt(1.0 - a * a) * ((0.5 * ti + 0.5) * u2)
    a_s[...] = a.reshape(t_len, bn, c)
    b_s[...] = b.reshape(t_len, bn, c)

    def body(k, h):
        t = (t_len - 1 - k) if reverse else k
        h = a_s[t] * h + b_s[t]
        h_s[t] = h
        return h

    carry[...] = lax.fori_loop(0, t_len, body, carry[...], unroll=8)

    if reverse:
        out_ref[...] = ((hf_ref[...] + h_s[...]) * _gelu_tanh(gate_ref[...])).astype(out_ref.dtype)


def _rglru(ug, hf, conv_w, conv_b, wa, ba, wx, bx, lam, bn, *, reverse):
    s_len = ug.shape[0]
    c = LRU_W
    ug3 = ug
    t_len = min(LRU_T, s_len)
    n_chunks = s_len // t_len

    def ck(i):
        return (n_chunks - 1 - i) if reverse else i

    half = c // 2
    nb = LRU_BLOCKS // 2

    def dense_halves(w):
        w = 0.5 * w.astype(F32).reshape(2, nb, c // LRU_BLOCKS, c // LRU_BLOCKS)
        eye = jnp.eye(nb, dtype=F32)
        return jnp.einsum('gnij,nm->gnimj', w, eye).reshape(2, half, half).astype(BF16)

    decay_scale = ((-0.5 * LRU_C * LOG2E) * jax.nn.softplus(-lam.astype(F32))).reshape(1, c)
    vec = pl.BlockSpec((1, c), lambda i: (0, 0))
    wspec = pl.BlockSpec((2, half, half), lambda i: (0, 0, 0))
    blk = pl.BlockSpec((t_len, bn, c), lambda i: (ck(i), 0, 0))
    in_specs = [
        pl.BlockSpec((2, bn, c), lambda i: (jnp.maximum(ck(i) * (t_len // 2) - 1, 0), 0, 0)),
        blk,
        pl.BlockSpec((1, bn, c), lambda i: (jnp.minimum((ck(i) + 1) * t_len, s_len - 1), 0, 0)),
    ]
    args = [ug3, ug3, ug3]
    scratch = [pltpu.VMEM((t_len + 3, bn, c), F32), pltpu.VMEM((t_len, bn, c), F32),
               pltpu.VMEM((t_len, bn, c), F32)]
    if reverse:
        in_specs += [pl.BlockSpec((t_len, bn, c), lambda i: (ck(i), 0, 1)), blk]
        args += [ug3, hf]
        scratch.append(pltpu.VMEM((t_len, bn, c), F32))
    scratch.append(pltpu.VMEM((bn, c), F32))
    in_specs += [pl.BlockSpec((4, c), lambda i: (0, 0)), vec, wspec, vec, wspec, vec, vec]
    args += [conv_w.astype(F32), conv_b.astype(F32).reshape(1, c), dense_halves(wa),
             0.5 * ba.astype(F32).reshape(1, c), dense_halves(wx), 0.5 * bx.astype(F32).reshape(1, c),
             decay_scale]
    return pl.pallas_call(
        functools.partial(_rglru_kernel, reverse=reverse, n_chunks=n_chunks),
        out_shape=jax.ShapeDtypeStruct((s_len, bn, c), BF16 if reverse else F32),
        grid=(n_chunks,),
        in_specs=in_specs,
        out_specs=blk,
        scratch_shapes=scratch,
        compiler_params=_cparams(("arbitrary",)),
        name="rglru_bw" if reverse else "rglru_fw",
    )(*args)


def _hg_matrices(c, reverse):
    n_levels = int(math.log2(c))
    t = np.arange(c)[:, None]
    r = np.arange(c)[None, :]
    mats, masks, upper = [], [], []
    if not reverse:
        mats.append(r <= t)
        mats.append(r > t)
    else:
        mats.append(r >= t)
        mats.append(r < t)
    for lev in range(n_levels):
        half = c >> (lev + 1)
        parent = 2 * half
        start = (np.arange(c) // parent) * parent
        mid = (start + half)[:, None]
        later = (np.arange(c) % parent >= half)[:, None]
        if not reverse:
            m = np.where(later, (r >= mid) & (r <= t), (r > t) & (r < mid))
            is_q = later
        else:
            m = np.where(later, (r >= mid) & (r < t), (r >= t) & (r < mid))
            is_q = ~later
        mats.append(m)
        same_parent = (start[:, None] == start[None, :])
        masks.append(same_parent & is_q & (~is_q).T)
        upper.append(np.broadcast_to(is_q, (c, HG_DK)))
    mat = np.concatenate(mats, axis=0).astype(np.float32)
    return (np.concatenate([mat, mat], axis=1), np.stack(masks).astype(np.float32),
            np.stack(upper).astype(np.float32))


def _hgrn_kernel(*refs, reverse, n_steps, n_levels):
    if reverse:
        (q_ref, f_ref, v_ref, go_ref, of_ref, lb_ref, gn_ref, mat_ref, mask_ref, isq_ref, out_ref,
         state, x_scr, oi_scr, aux_scr, dec_scr, kk_scr) = refs
    else:
        (q_ref, f_ref, v_ref, lb_ref, mat_ref, mask_ref, isq_ref, out_ref,
         state, x_scr, oi_scr, aux_scr, dec_scr, kk_scr) = refs
    step = pl.program_id(1)
    c = HG_CHUNK
    t_len = q_ref.shape[0]
    n_chunks = t_len // c

    @pl.when(step == 0)
    def _():
        state[...] = jnp.zeros_like(state)

    def chunk_body(ci, carry):
        ch = (n_chunks - 1 - ci) if reverse else ci
        rows = slice(ch * c, (ch + 1) * c)
        edge = (c - 1) if not reverse else 0
        for hp in range(HG_HEADS // 2):
            cols2 = slice(2 * hp * HG_DK, 2 * (hp + 1) * HG_DK)
            fl = f_ref[rows, cols2]
            e = jnp.exp(-jnp.abs(fl))
            r = 1.0 / (1.0 + e)
            er = e * r
            pos = fl > 0.0
            lb = lb_ref[0:1, cols2]
            oml = lb_ref[1:2, cols2]
            g2 = jnp.maximum(jnp.log2(lb + oml * jnp.where(pos, r, er)), F32_MIN_EXPONENT)
            kk_scr[hp] = oml * jnp.where(pos, er, r)
            g_hi = g2.astype(BF16)
            g_lo = (g2 - g_hi.astype(F32)).astype(BF16)
            dec_scr[hp] = jnp.exp2(_dot(mat_ref[...], jnp.concatenate([g_hi, g_lo], axis=0)))

        for hd in range(HG_HEADS):
            cols = slice(hd * HG_DK, (hd + 1) * HG_DK)
            pcols = slice((hd % 2) * HG_DK, (hd % 2 + 1) * HG_DK)
            kk = kk_scr[hd // 2, :, pcols]
            dec = dec_scr.at[hd // 2]
            qv = q_ref[rows, cols].astype(F32)
            qq = qv * jax.nn.sigmoid(qv)
            qb, kb = qq.astype(BF16), kk.astype(BF16)
            d_in = dec[0:c, pcols]
            x_scr[hd, n_levels] = (qq * d_in).astype(BF16)
            x_scr[hd, n_levels + 1] = (kk * dec[c:2 * c, pcols]).astype(BF16)
            aux_scr[hd, 0] = jnp.broadcast_to(d_in[edge:edge + 1, :], (c, HG_DK))
            aux_scr[hd, 1] = jnp.broadcast_to(jnp.sum(qq * kk, axis=-1, keepdims=True), (c, HG_DK))
            for lev in range(n_levels):
                half = c >> (lev + 1)
                if half >= SUBLANES_BF16:
                    first_is_q = bool(reverse)
                    base = jnp.concatenate(
                        [(qb if (blk % 2 == 1) != first_is_q else kb)[blk * half:(blk + 1) * half]
                         for blk in range(c // half)], axis=0)
                else:
                    base = jnp.where(isq_ref[lev] > 0.5, qb, kb)
                x_scr[hd, lev] = base * dec[(2 + lev) * c:(3 + lev) * c, pcols].astype(BF16)

        for hd in range(HG_HEADS):
            cols = slice(hd * HG_DK, (hd + 1) * HG_DK)
            att = None
            for lev in range(n_levels):
                xl = x_scr[hd, lev]
                term = mask_ref[lev] * _nt_dot(xl, xl)
                att = term if att is None else att + term
            st = state[hd]
            oi_scr[hd] = _nt_dot(x_scr[hd, n_levels], st.astype(BF16))
            vt = v_ref[rows, cols].astype(F32).T.astype(BF16)
            state[hd] = st * aux_scr[hd, 0] + _dot(vt, x_scr[hd, n_levels + 1])
            x_scr[hd, 0] = att.astype(BF16)

        for hd in range(HG_HEADS):
            cols = slice(hd * HG_DK, (hd + 1) * HG_DK)
            vb = v_ref[rows, cols]
            o = oi_scr[hd] + _dot(x_scr[hd, 0], vb) + aux_scr[hd, 1] * vb.astype(F32)

            if reverse:
                o = o + of_ref[rows, cols]
                o = o * lax.rsqrt(jnp.mean(o * o, axis=-1, keepdims=True) + EPS) * gn_ref[...]
                gv = go_ref[rows, cols].astype(F32)
                out_ref[rows, cols] = (o * (gv * jax.nn.sigmoid(gv))).astype(out_ref.dtype)
            else:
                out_ref[rows, cols] = o
        return carry

    for ci in range(n_chunks):
        chunk_body(ci, 0)


def _hgrn(hg, o_fw, lb, gnorm_g, *, reverse):
    bn, s_len, _ = hg["q"].shape
    w = HG_HEADS * HG_DK
    t_len = min(HG_T, s_len)
    n_steps = s_len // t_len
    mats, masks, isq = _hg_matrices(HG_CHUNK, reverse)
    n_levels = masks.shape[0]
    lbf = lb.astype(F32)
    lb_tab = jnp.stack([lbf, 1.0 - lbf])

    blk = pl.BlockSpec((None, t_len, w), lambda b, i: (b, (n_steps - 1 - i) if reverse else i, 0))

    def const(shape):
        return pl.BlockSpec(shape, lambda b, i: (0,) * len(shape))

    in_specs = [blk, blk, blk]
    args = [hg["q"], hg["f_bw" if reverse else "f_fw"], hg["i_in"]]
    if reverse:
        in_specs += [blk, blk]
        args += [hg["g_out"], o_fw]
    in_specs.append(const((2, w)))
    args.append(lb_tab)
    if reverse:
        in_specs.append(const((1, HG_DK)))
        args.append(gnorm_g.astype(F32).reshape(1, HG_DK))
    in_specs += [const(mats.shape), const(masks.shape), const(isq.shape)]
    args += [jnp.asarray(mats, BF16), jnp.asarray(masks, F32), jnp.asarray(isq, BF16)]
    return pl.pallas_call(
        functools.partial(_hgrn_kernel, reverse=reverse, n_steps=n_steps, n_levels=n_levels),
        out_shape=jax.ShapeDtypeStruct((bn, s_len, w), BF16 if reverse else F32),
        grid=(bn, n_steps),
        in_specs=in_specs,
        out_specs=blk,
        scratch_shapes=[pltpu.VMEM((HG_HEADS, HG_DK, HG_DK), F32),
                        pltpu.VMEM((HG_HEADS, n_levels + 2, HG_CHUNK, HG_DK), BF16),
                        pltpu.VMEM((HG_HEADS, HG_CHUNK, HG_DK), F32),
                        pltpu.VMEM((HG_HEADS, 2, HG_CHUNK, HG_DK), F32),
                        pltpu.VMEM((HG_HEADS // 2, (n_levels + 2) * HG_CHUNK, 2 * HG_DK), F32),
                        pltpu.VMEM((HG_HEADS // 2, HG_CHUNK, 2 * HG_DK), F32)],
        compiler_params=_cparams(("parallel", "arbitrary")),
        name="hgrn_bw" if reverse else "hgrn_fw",
    )(*args)


def _even_layer(x, g, w_in, w_out, rpb, tail):
    qkv, view4, view16 = _ev_proj(x, g, w_in.astype(BF16))
    outs, stats = zip(*[_dilated_branch(view, dil) for dil, view in zip(DILATIONS, (qkv, view4, view16))])
    yb = _neighborhood(qkv, rpb)
    return _ev_tail(x, outs, stats, yb, w_out.astype(BF16), tail)


def _odd_layer(x, g, w_in, w_out, conv_w, conv_b, wa, ba, wx, bx, lam, lb_f, lb_b, gnorm_g, tail):
    bn, s_len, _ = x.shape
    ug, hg = _od_proj(x, g, w_in.astype(BF16))
    h_fw = _rglru(ug, None, conv_w, conv_b, wa[0], ba[0], wx[0], bx[0], lam[0], bn, reverse=False)
    yc = _rglru(ug, h_fw, conv_w, conv_b, wa[1], ba[1], wx[1], bx[1], lam[1], bn, reverse=True)
    o_fw = _hgrn(hg, None, lb_f, gnorm_g, reverse=False)
    yd = _hgrn(hg, o_fw, lb_b, gnorm_g, reverse=True)
    return _od_tail(x, yc.reshape(s_len, bn * LRU_W), yd, w_out.astype(BF16), tail)


def kernel(x, mem, norm_mix_g, norm_xa_g, norm_mem_g, norm_mlp_g, final_norm_g, ev_w_in, ev_w_out, na_rpb,
           od_w_in, od_w_out, conv_w, conv_b, lru_wa, lru_ba, lru_wx, lru_bx, lru_lambda, hgrn_lb_logits,
           hgrn_norm_g, xa_wq, xa_wkv, xa_wo, mlp_w1, mlp_w2):
    depth = norm_mix_g.shape[0]
    p_lb = jax.nn.softmax(hgrn_lb_logits.astype(F32), axis=0)
    lower_bounds = jnp.cumsum(p_lb, axis=0) - p_lb[0:1]
    for layer in range(depth):
        tail = dict(g_xa=norm_xa_g[layer], wq=xa_wq[layer].astype(BF16),
                    kv=_mem_kv(mem, norm_mem_g[layer], xa_wkv[layer].astype(BF16)),
                    wo=xa_wo[layer].astype(BF16), g_mlp=norm_mlp_g[layer],
                    w1=mlp_w1[layer].astype(BF16), w2=mlp_w2[layer].astype(BF16),
                    g_final=final_norm_g if layer == depth - 1 else None)
        if layer % 2 == 0:
            e = layer // 2
            x = _even_layer(x, norm_mix_g[layer], ev_w_in[e], ev_w_out[e], na_rpb[e], tail)
        else:
            o = layer // 2
            x = _odd_layer(x, norm_mix_g[layer], od_w_in[o], od_w_out[o], conv_w[o], conv_b[o],
                           lru_wa[o], lru_ba[o], lru_wx[o], lru_bx[o], lru_lambda[o],
                           lower_bounds[layer, 0], lower_bounds[layer, 1], hgrn_norm_g[o], tail)
        x = _mlp(x, tail)
    return x
```

```python
import functools
import math

import jax
import jax.numpy as jnp
import numpy as np
from jax import lax
from jax.experimental import pallas as pl
from jax.experimental.pallas import tpu as pltpu

F32 = jnp.float32
BF16 = jnp.bfloat16

D_MODEL = 1024
HEAD_DIM = 64
ROT_DIM = 16
ROPE_THETA = 500000.0
N_HEADS_A = 8
N_HEADS_B = 8
GROUP_W = N_HEADS_A * HEAD_DIM
DILATIONS = (1, 4, 16)
WIN_HALF = 64
GRID_W = 64
NA_ROWS = 8
NA_COLS = 16
LRU_W = 512
LRU_BLOCKS = 8
LRU_C = 8.0
HG_HEADS = 4
HG_DK = 128
XA_HEADS = 4
XA_DH = 256
D_FF = 4096
EPS = 1e-6
LOG2E = math.log2(math.e)
F32_MIN_EXPONENT = -126.0

LANES = 128
SUBLANES_BF16 = 16
V7X_VMEM_LIMIT_BYTES = 56 * 1024 * 1024

ROW_TILE_WIDE = 1024
ATT_Q_BLOCK = 1024
ATT_SUB = 128
NA_ROW_GROUP = 16
LRU_T = 256
HG_CHUNK = 128
HG_T = 1024


def _cparams(sem):
    return pltpu.CompilerParams(dimension_semantics=sem, vmem_limit_bytes=V7X_VMEM_LIMIT_BYTES)


def _resident(shape, index_map):
    return pl.BlockSpec(shape, index_map, pipeline_mode=pl.Buffered(1))


def _rms(x, g):
    return x * lax.rsqrt(jnp.mean(x * x, axis=-1, keepdims=True) + EPS) * g


def _nt_dot(a, b):
    return lax.dot_general(a, b, (((1,), (1,)), ((), ())), preferred_element_type=F32)


def _dot(a, b):
    return jnp.dot(a, b, preferred_element_type=F32)


def _head_selectors():
    lane = lax.broadcasted_iota(jnp.int32, (1, LANES), 1)
    low = (lane < HEAD_DIM).astype(F32)
    return low.astype(BF16), (1.0 - low).astype(BF16)


def _ev_proj_kernel(x_ref, g_ref, w_ref, rc_ref, rs1_ref, rs2_ref, o_ref, o4_ref, o16_ref, slab):
    h = _rms(x_ref[...], g_ref[...]).astype(BF16)
    scale = HEAD_DIM ** -0.5 * LOG2E
    tm = x_ref.shape[0]
    n_slabs = GROUP_W // LANES
    for c in range(6):
        acc = _dot(h, w_ref[:, c * GROUP_W:(c + 1) * GROUP_W])
        if c in (0, 1):
            rc, rs1, rs2 = rc_ref[...], rs1_ref[...], rs2_ref[...]
            parts = []
            for j in range(n_slabs):
                t = acc[:, j * LANES:(j + 1) * LANES]
                t = t * rc + pltpu.roll(t, LANES - ROT_DIM // 2, 1) * rs1 + pltpu.roll(t, ROT_DIM // 2, 1) * rs2
                parts.append(t)
            acc = jnp.concatenate(parts, axis=-1)
        if c in (0, 3):
            acc = acc * scale
        o_ref[:, c * GROUP_W:(c + 1) * GROUP_W] = acc.astype(BF16)
        if c < 3:
            for j in range(n_slabs):
                slab[j] = acc[:, j * LANES:(j + 1) * LANES]
            for dil, dst in ((DILATIONS[1], o4_ref), (DILATIONS[2], o16_ref)):
                for r in range(dil):
                    for j in range(n_slabs):
                        col = r * 3 * GROUP_W + c * GROUP_W + j * LANES
                        dst[:, col:col + LANES] = slab[j, pl.ds(r, tm // dil, stride=dil), :].astype(BF16)


def _rope_tables(s_len):
    half = ROT_DIM // 2
    inv = jnp.asarray(ROPE_THETA ** (-np.arange(half) * 2.0 / ROT_DIM), F32)
    ang = jnp.arange(s_len, dtype=F32)[:, None] * inv[None, :]
    cos, sin = jnp.cos(ang), jnp.sin(ang)
    ones = jnp.ones((s_len, HEAD_DIM - ROT_DIM), F32)
    zeros = jnp.zeros((s_len, HEAD_DIM - ROT_DIM), F32)
    zh = jnp.zeros((s_len, half), F32)
    rc = jnp.concatenate([cos, cos, ones], axis=-1)
    rs1 = jnp.concatenate([-sin, zh, zeros], axis=-1)
    rs2 = jnp.concatenate([zh, sin, zeros], axis=-1)
    rep = LANES // HEAD_DIM
    return jnp.tile(rc, (1, rep)), jnp.tile(rs1, (1, rep)), jnp.tile(rs2, (1, rep))


def _ev_proj(x, g, w_bf16):
    bn, s_len, d = x.shape
    tm = min(ROW_TILE_WIDE, s_len)
    n_out = w_bf16.shape[1]
    rc, rs1, rs2 = _rope_tables(s_len)
    tbl_spec = pl.BlockSpec((tm, LANES), lambda b, i: (i, 0))
    wa = 3 * GROUP_W
    d4, d16 = DILATIONS[1], DILATIONS[2]
    return pl.pallas_call(
        _ev_proj_kernel,
        out_shape=(jax.ShapeDtypeStruct((bn, s_len, n_out), BF16),
                   jax.ShapeDtypeStruct((bn, s_len // d4, d4 * wa), BF16),
                   jax.ShapeDtypeStruct((bn, s_len // d16, d16 * wa), BF16)),
        grid=(bn, s_len // tm),
        in_specs=[
            pl.BlockSpec((None, tm, d), lambda b, i: (b, i, 0)),
            pl.BlockSpec((1, d), lambda b, i: (0, 0)),
            _resident((d, n_out), lambda b, i: (0, 0)),
            tbl_spec, tbl_spec, tbl_spec,
        ],
        out_specs=(pl.BlockSpec((None, tm, n_out), lambda b, i: (b, i, 0)),
                   pl.BlockSpec((None, tm // d4, d4 * wa), lambda b, i: (b, i, 0)),
                   pl.BlockSpec((None, tm // d16, d16 * wa), lambda b, i: (b, i, 0))),
        scratch_shapes=[pltpu.VMEM((GROUP_W // LANES, tm, LANES), F32)],
        compiler_params=_cparams(("parallel", "parallel")),
        name="ev_proj",
    )(x, g.reshape(1, d), w_bf16, rc, rs1, rs2)


def _dilated_kernel(q_ref, kl_ref, kc_ref, kr_ref, vl_ref, vc_ref, vr_ref, o_ref, stat_ref,
                    kext, vext, s_scr, p_scr, *, lq, l_total):
    i = pl.program_id(2)
    kext[0:WIN_HALF, :] = kl_ref[...]
    kext[WIN_HALF:WIN_HALF + lq, :] = kc_ref[...]
    kext[WIN_HALF + lq:, :] = kr_ref[...]
    vext[0:WIN_HALF, :] = vl_ref[...]
    vext[WIN_HALF:WIN_HALF + lq, :] = vc_ref[...]
    vext[WIN_HALF + lq:, :] = vr_ref[...]

    wk = ATT_SUB + 2 * WIN_HALF
    qi = lax.broadcasted_iota(jnp.int32, (ATT_SUB, wk), 0)
    ci = lax.broadcasted_iota(jnp.int32, (ATT_SUB, wk), 1)
    band_bias = jnp.where((ci - qi >= 0) & (ci - qi <= 2 * WIN_HALF), 0.0, -jnp.inf).astype(F32)
    crow = lax.broadcasted_iota(jnp.int32, (1, wk), 1)
    lane = lax.broadcasted_iota(jnp.int32, (ATT_SUB, LANES), 1)
    low_half = lane < HEAD_DIM
    sel_lo, sel_hi = _head_selectors()
    n_sub = lq // ATT_SUB
    n_hp = GROUP_W // LANES

    for j in range(n_sub):
        base = i * lq + j * ATT_SUB - WIN_HALF
        in_seq = (crow + base >= 0) & (crow + base < l_total)
        bias = band_bias + jnp.where(in_seq, 0.0, -jnp.inf).astype(F32)
        bias2 = jnp.concatenate([bias, bias], axis=0)
        rows = slice(j * ATT_SUB, (j + 1) * ATT_SUB)
        wrows = slice(j * ATT_SUB, j * ATT_SUB + wk)
        for hp in range(n_hp):
            cols = slice(hp * LANES, (hp + 1) * LANES)
            qp = q_ref[rows, cols]
            q2 = jnp.concatenate([qp * sel_lo, qp * sel_hi], axis=0)
            s_scr[j * n_hp + hp] = _nt_dot(q2, kext[wrows, cols]) + bias2
    for j in range(n_sub):
        stat = jnp.ones((ATT_SUB, LANES), F32)
        for hp in range(n_hp):
            t = j * n_hp + hp
            s = s_scr[t]
            m = jnp.max(s, axis=-1, keepdims=True)
            p = jnp.exp2(s - m)
            den = jnp.sum(p, axis=-1, keepdims=True)
            p_scr[t] = p.astype(BF16)
            stat = jnp.where(lane == 2 * hp, m[0:ATT_SUB], jnp.where(lane == 2 * hp + 1, m[ATT_SUB:], stat))
            stat = jnp.where(lane == N_HEADS_A + 2 * hp, den[0:ATT_SUB],
                             jnp.where(lane == N_HEADS_A + 2 * hp + 1, den[ATT_SUB:], stat))
        stat_ref[j * ATT_SUB:(j + 1) * ATT_SUB, :] = stat
    for j in range(n_sub):
        rows = slice(j * ATT_SUB, (j + 1) * ATT_SUB)
        wrows = slice(j * ATT_SUB, j * ATT_SUB + wk)
        for hp in range(n_hp):
            cols = slice(hp * LANES, (hp + 1) * LANES)
            o = _dot(p_scr[j * n_hp + hp], vext[wrows, cols])
            o_ref[rows, cols] = jnp.where(low_half, o[0:ATT_SUB], o[ATT_SUB:]).astype(BF16)


def _dilated_branch(view, dil):
    bn, l_total, width = view.shape
    n_groups = width // (dil * GROUP_W)
    lq = min(ATT_Q_BLOCK, l_total)
    nblk = l_total // lq
    per = lq // WIN_HALF
    n_halo = l_total // WIN_HALF
    n_units = (lq // ATT_SUB) * (GROUP_W // LANES)

    def cur(g):
        return pl.BlockSpec((None, lq, GROUP_W), lambda b, r, i: (b, i, r * n_groups + g))

    def left(g):
        return pl.BlockSpec((None, WIN_HALF, GROUP_W),
                            lambda b, r, i: (b, jnp.maximum(i * per - 1, 0), r * n_groups + g))

    def right(g):
        return pl.BlockSpec((None, WIN_HALF, GROUP_W),
                            lambda b, r, i: (b, jnp.minimum((i + 1) * per, n_halo - 1), r * n_groups + g))

    stat_shape = jax.ShapeDtypeStruct((bn, l_total, dil * LANES), F32)
    stat_spec = pl.BlockSpec((None, lq, LANES), lambda b, r, i: (b, i, r))
    return pl.pallas_call(
        functools.partial(_dilated_kernel, lq=lq, l_total=l_total),
        out_shape=(jax.ShapeDtypeStruct((bn, l_total, dil * GROUP_W), BF16), stat_shape),
        grid=(bn, dil, nblk),
        in_specs=[cur(0), left(1), cur(1), right(1), left(2), cur(2), right(2)],
        out_specs=(pl.BlockSpec((None, lq, GROUP_W), lambda b, r, i: (b, i, r)), stat_spec),
        scratch_shapes=[pltpu.VMEM((lq + 2 * WIN_HALF, GROUP_W), BF16),
                        pltpu.VMEM((lq + 2 * WIN_HALF, GROUP_W), BF16),
                        pltpu.VMEM((n_units, 2 * ATT_SUB, ATT_SUB + 2 * WIN_HALF), F32),
                        pltpu.VMEM((n_units, 2 * ATT_SUB, ATT_SUB + 2 * WIN_HALF), BF16)],
        compiler_params=_cparams(("parallel", "parallel", "parallel")),
        name=f"dilated_attn_d{dil}",
    )(view, view, view, view, view, view, view)


NA_QT = NA_ROW_GROUP * GRID_W
NA_KT = 2 * NA_QT
NA_KW = NA_ROWS * GRID_W


def _na_tables(rpb):
    rpb = rpb.astype(F32)
    nh = rpb.shape[0]
    ext = GRID_W - NA_COLS
    padded = jnp.concatenate([jnp.repeat(rpb[..., :1], ext, axis=-1), rpb,
                              jnp.repeat(rpb[..., -1:], ext, axis=-1)], axis=-1)
    skew = jnp.tile(padded, (1, 1, GRID_W + 1))[..., :2 * GRID_W * GRID_W]
    t1 = skew.reshape(nh, -1, GRID_W, 2 * GRID_W)[..., ::-1, :GRID_W]
    qc = np.arange(GRID_W)[:, None]
    kc = np.arange(GRID_W)[None, :]
    c0 = np.clip(qc - NA_COLS // 2, 0, GRID_W - NA_COLS)
    t1 = jnp.where((kc >= c0) & (kc < c0 + NA_COLS), t1 * LOG2E, -jnp.inf)
    per_delta = [jnp.transpose(t1[:, NA_ROWS - 1 - delta:2 * NA_ROWS - 1 - delta], (0, 2, 1, 3))
                 .reshape(nh, GRID_W, NA_KW) for delta in range(NA_ROWS)]
    return jnp.stack(per_delta).reshape(NA_ROWS, nh // 2, 2 * GRID_W, NA_KW)


def _na_kernel(q_ref, k_ref, v_ref, tb_ref, o_ref, s_scr, p_scr, inv_scr, *, rows):
    i0 = pl.program_id(1) * NA_ROW_GROUP
    w0 = jnp.clip(i0 - NA_ROWS // 2, 0, rows - 2 * NA_ROW_GROUP)
    lane = lax.broadcasted_iota(jnp.int32, (GRID_W, LANES), 1)
    low_half = lane < HEAD_DIM
    sel_lo, sel_hi = _head_selectors()
    n_hp = GROUP_W // LANES

    def key_rows(a):
        r0 = jnp.clip(i0 + a - NA_ROWS // 2, 0, rows - NA_ROWS)
        return pl.ds(pl.multiple_of((r0 - w0) * GRID_W, GRID_W), NA_KW), i0 + a - r0

    for a in range(NA_ROW_GROUP):
        krows, delta = key_rows(a)
        qrows = slice(a * GRID_W, (a + 1) * GRID_W)
        for hp in range(n_hp):
            cols = slice(hp * LANES, (hp + 1) * LANES)
            qp = q_ref[qrows, cols]
            q2 = jnp.concatenate([qp * sel_lo, qp * sel_hi], axis=0)
            s_scr[a * n_hp + hp] = _nt_dot(q2, k_ref[0, krows, cols]) + tb_ref[delta, hp]
    for t in range(NA_ROW_GROUP * n_hp):
        s = s_scr[t]
        p = jnp.exp2(s - jnp.max(s, axis=-1, keepdims=True))
        inv_scr[t] = jnp.broadcast_to(1.0 / jnp.sum(p, axis=-1, keepdims=True), (2 * GRID_W, LANES))
        p_scr[t] = p.astype(BF16)
    for a in range(NA_ROW_GROUP):
        krows, _ = key_rows(a)
        qrows = slice(a * GRID_W, (a + 1) * GRID_W)
        for hp in range(n_hp):
            cols = slice(hp * LANES, (hp + 1) * LANES)
            t = a * n_hp + hp
            o = _dot(p_scr[t], v_ref[0, krows, cols]) * inv_scr[t]
            o_ref[qrows, cols] = jnp.where(low_half, o[0:GRID_W], o[GRID_W:]).astype(BF16)


def _neighborhood(qkv, rpb):
    bn, s_len, _ = qkv.shape
    rows = s_len // GRID_W
    n_groups = rows // NA_ROW_GROUP
    tables = _na_tables(rpb)

    def window(g):
        return pl.BlockSpec(
            (pl.Element(1), pl.Element(NA_KT), pl.Element(GROUP_W)),
            lambda b, i: (b, jnp.clip(i * NA_ROW_GROUP - NA_ROWS // 2, 0, rows - 2 * NA_ROW_GROUP) * GRID_W,
                          g * GROUP_W))

    return pl.pallas_call(
        functools.partial(_na_kernel, rows=rows),
        out_shape=jax.ShapeDtypeStruct((bn, s_len, GROUP_W), BF16),
        grid=(bn, n_groups),
        in_specs=[pl.BlockSpec((None, NA_QT, GROUP_W), lambda b, i: (b, i, 3)), window(4), window(5),
                  pl.BlockSpec(tables.shape, lambda b, i: (0, 0, 0, 0))],
        out_specs=pl.BlockSpec((None, NA_QT, GROUP_W), lambda b, i: (b, i, 0)),
        scratch_shapes=[pltpu.VMEM((NA_ROW_GROUP * GROUP_W // LANES, 2 * GRID_W, NA_KW), F32),
                        pltpu.VMEM((NA_ROW_GROUP * GROUP_W // LANES, 2 * GRID_W, NA_KW), BF16),
                        pltpu.VMEM((NA_ROW_GROUP * GROUP_W // LANES, 2 * GRID_W, LANES), F32)],
        compiler_params=_cparams(("parallel", "parallel")),
        name="neighborhood_attn",
    )(qkv, qkv, qkv, tables)


def _xattn_block(x, g_ref, wq_ref, kv_ref, wo_ref):
    q = (_dot(_rms(x, g_ref[...]).astype(BF16), wq_ref[...]) * (XA_DH ** -0.5 * LOG2E)).astype(BF16)
    heads = []
    for hd in range(XA_HEADS):
        cols = slice(hd * XA_DH, (hd + 1) * XA_DH)
        s = _nt_dot(q[:, cols], kv_ref[:, cols])
        m = jnp.max(s, axis=-1, keepdims=True)
        p = jnp.exp2(s - m)
        den = jnp.sum(p, axis=-1, keepdims=True)
        vcols = slice(D_MODEL + hd * XA_DH, D_MODEL + (hd + 1) * XA_DH)
        heads.append((_dot(p.astype(BF16), kv_ref[:, vcols]) * (1.0 / den)).astype(BF16))
    return x + _dot(jnp.concatenate(heads, axis=-1), wo_ref[...])


def _mlp_block(x, g_ref, w1_ref, w2_ref, gf_ref, final_norm):
    h = _rms(x, g_ref[...]).astype(BF16)
    acc = x
    for c in range(D_FF // D_MODEL):
        cols = slice(c * D_MODEL, (c + 1) * D_MODEL)
        a = jnp.maximum(_dot(h, w1_ref[:, cols]), 0.0)
        acc = acc + _dot((a * a).astype(BF16), w2_ref[cols, :])
    return _rms(acc, gf_ref[...]) if final_norm else acc


def _xattn_operands(tail, d):
    m_len = tail["kv"].shape[1]
    specs = [pl.BlockSpec((1, d), lambda b, i: (0, 0)), _resident((d, d), lambda b, i: (0, 0)),
             pl.BlockSpec((None, m_len, 2 * d), lambda b, i: (b, 0, 0)),
             _resident((d, d), lambda b, i: (0, 0))]
    return specs, [tail["g_xa"].reshape(1, d), tail["wq"], tail["kv"], tail["wo"]]


def _mlp_kernel(x_ref, g_ref, w1_ref, w2_ref, gf_ref, out_ref, *, final_norm):
    out_ref[...] = _mlp_block(x_ref[...], g_ref, w1_ref, w2_ref, gf_ref, final_norm)


def _mlp(x, tail):
    bn, s_len, d = x.shape
    tm = min(ROW_TILE_WIDE, s_len)
    final_norm = tail["g_final"] is not None
    gf = tail["g_final"] if final_norm else tail["g_mlp"]
    x_spec = pl.BlockSpec((None, tm, d), lambda b, i: (b, i, 0))
    vec = pl.BlockSpec((1, d), lambda b, i: (0, 0))
    return pl.pallas_call(
        functools.partial(_mlp_kernel, final_norm=final_norm),
        out_shape=jax.ShapeDtypeStruct(x.shape, F32),
        grid=(bn, s_len // tm),
        in_specs=[x_spec, vec, _resident((d, D_FF), lambda b, i: (0, 0)),
                  _resident((D_FF, d), lambda b, i: (0, 0)), vec],
        out_specs=x_spec,
        compiler_params=_cparams(("parallel", "parallel")),
        name="mlp_final" if final_norm else "mlp",
    )(x, tail["g_mlp"].reshape(1, d), tail["w1"], tail["w2"], gf.reshape(1, d))


def _ev_tail_kernel(o1, o2v, o3v, s1, s2v, s3v, yb_ref, ex_ref, w_ref, x_ref,
                    gxa_ref, wq_ref, kv_ref, wo_ref, out_ref, so2, so3, st2, st3):
    tm = x_ref.shape[0]
    n_slabs = GROUP_W // LANES
    for dil, ov, sv, so, st in ((DILATIONS[1], o2v, s2v, so2, st2), (DILATIONS[2], o3v, s3v, so3, st3)):
        for r in range(dil):
            dst = pl.ds(r, tm // dil, stride=dil)
            for j in range(n_slabs):
                col = r * GROUP_W + j * LANES
                so[j, dst, :] = ov[:, col:col + LANES].astype(F32)
            st[dst, :] = sv[:, r * LANES:(r + 1) * LANES]
    o2 = jnp.concatenate([so2[j] for j in range(n_slabs)], axis=-1)
    o3 = jnp.concatenate([so3[j] for j in range(n_slabs)], axis=-1)
    ma, mb, mc = s1[...], st2[...], st3[...]
    da, db, dc = [pltpu.roll(t, LANES - N_HEADS_A, 1) for t in (ma, mb, mc)]
    m = jnp.maximum(jnp.maximum(ma, mb), mc)
    ea, eb, ec = jnp.exp2(ma - m), jnp.exp2(mb - m), jnp.exp2(mc - m)
    inv = 1.0 / (ea * da + eb * db + ec * dc)
    lane = lax.broadcasted_iota(jnp.int32, (tm, LANES), 1)
    wcat = jnp.where(lane < N_HEADS_A, ea * inv,
                     jnp.where(lane < 2 * N_HEADS_A, pltpu.roll(eb * inv, N_HEADS_A, 1),
                               pltpu.roll(ec * inv, 2 * N_HEADS_A, 1)))
    hi = wcat.astype(BF16)
    lo = (wcat - hi.astype(F32)).astype(BF16)
    wide = _dot(jnp.concatenate([hi, lo], axis=1), ex_ref[...])
    ya = (wide[:, 0:GROUP_W] * o1[...].astype(F32) + wide[:, GROUP_W:2 * GROUP_W] * o2
          + wide[:, 2 * GROUP_W:] * o3)
    y = _dot(ya.astype(BF16), w_ref[0:GROUP_W, :]) + _dot(yb_ref[...], w_ref[GROUP_W:, :])
    out_ref[...] = _xattn_block(x_ref[...] + y, gxa_ref, wq_ref, kv_ref, wo_ref)


def _ev_tail(x, outs, stats, yb, w_bf16, tail):
    bn, s_len, d = x.shape
    tm = min(ROW_TILE_WIDE, s_len)
    n_br = len(DILATIONS)
    ex = np.zeros((LANES, n_br * GROUP_W), np.float32)
    for i in range(n_br):
        for h in range(N_HEADS_A):
            ex[i * N_HEADS_A + h, i * GROUP_W + h * HEAD_DIM:i * GROUP_W + (h + 1) * HEAD_DIM] = 1.0
    ex = np.concatenate([ex, ex], axis=0)

    def o_spec(dil):
        return pl.BlockSpec((None, tm // dil, dil * GROUP_W), lambda b, i: (b, i, 0))

    def l_spec(dil):
        return pl.BlockSpec((None, tm // dil, dil * LANES), lambda b, i: (b, i, 0))

    x_spec = pl.BlockSpec((None, tm, d), lambda b, i: (b, i, 0))
    n_slabs = GROUP_W // LANES
    tail_specs, tail_args = _xattn_operands(tail, d)
    return pl.pallas_call(
        _ev_tail_kernel,
        out_shape=jax.ShapeDtypeStruct(x.shape, F32),
        grid=(bn, s_len // tm),
        in_specs=[o_spec(d_) for d_ in DILATIONS] + [l_spec(d_) for d_ in DILATIONS]
        + [o_spec(1),
           _resident(ex.shape, lambda b, i: (0, 0)),
           _resident((2 * GROUP_W, d), lambda b, i: (0, 0)),
           x_spec] + tail_specs,
        out_specs=x_spec,
        scratch_shapes=[pltpu.VMEM((n_slabs, tm, LANES), F32), pltpu.VMEM((n_slabs, tm, LANES), F32),
                        pltpu.VMEM((tm, LANES), F32), pltpu.VMEM((tm, LANES), F32)],
        compiler_params=_cparams(("parallel", "parallel")),
        name="ev_tail",
    )(*outs, *stats, yb, jnp.asarray(ex, BF16), w_bf16, x, *tail_args)


def _od_tail_kernel(yc_ref, yd_ref, w_ref, x_ref, gxa_ref, wq_ref, kv_ref, wo_ref, out_ref):
    half = yc_ref.shape[-1]
    x0 = x_ref[...] + _dot(yc_ref[...], w_ref[0:half, :]) + _dot(yd_ref[...], w_ref[half:, :])
    out_ref[...] = _xattn_block(x0, gxa_ref, wq_ref, kv_ref, wo_ref)


def _od_tail(x, yc_tm, yd, w_bf16, tail):
    bn, s_len, d = x.shape
    tm = min(ROW_TILE_WIDE, s_len)
    c = yd.shape[-1]
    x_spec = pl.BlockSpec((None, tm, d), lambda b, i: (b, i, 0))
    tail_specs, tail_args = _xattn_operands(tail, d)
    return pl.pallas_call(
        _od_tail_kernel,
        out_shape=jax.ShapeDtypeStruct(x.shape, F32),
        grid=(bn, s_len // tm),
        in_specs=[pl.BlockSpec((tm, c), lambda b, i: (i, b)),
                  pl.BlockSpec((None, tm, c), lambda b, i: (b, i, 0)),
                  _resident((2 * c, d), lambda b, i: (0, 0)),
                  x_spec] + tail_specs,
        out_specs=x_spec,
        compiler_params=_cparams(("parallel", "parallel")),
        name="od_tail",
    )(yc_tm, yd, w_bf16, x, *tail_args)


def _norm_matmul_kernel(x_ref, g_ref, w_ref, o_ref):
    o_ref[...] = _dot(_rms(x_ref[...], g_ref[...]).astype(BF16), w_ref[...]).astype(o_ref.dtype)


def _mem_kv(mem, g, w_bf16):
    bn, m_len, d = mem.shape
    n_out = w_bf16.shape[1]
    return pl.pallas_call(
        _norm_matmul_kernel,
        out_shape=jax.ShapeDtypeStruct((bn, m_len, n_out), BF16),
        grid=(bn,),
        in_specs=[pl.BlockSpec((None, m_len, d), lambda b: (b, 0, 0)),
                  pl.BlockSpec((1, d), lambda b: (0, 0)),
                  pl.BlockSpec((d, n_out), lambda b: (0, 0))],
        out_specs=pl.BlockSpec((None, m_len, n_out), lambda b: (b, 0, 0)),
        compiler_params=_cparams(("parallel",)),
        name="mem_kv",
    )(mem, g.reshape(1, d), w_bf16)


def _od_proj_kernel(x_ref, g_ref, w_ref, ug_ref, *rest):
    hg_refs, slab = rest[:-1], rest[-1]
    bn, tq, d = x_ref.shape
    h = _rms(x_ref[...].reshape(bn * tq, d), g_ref[...]).astype(BF16)
    n_ug = 2 * LRU_W
    n_slabs = GROUP_W // LANES
    for c in range(n_ug // GROUP_W):
        acc = _dot(h, w_ref[:, c * GROUP_W:(c + 1) * GROUP_W])
        for b in range(bn):
            for j in range(n_slabs):
                slab[j, pl.ds(b, tq, stride=bn), :] = acc[b * tq:(b + 1) * tq, j * LANES:(j + 1) * LANES]
        for j in range(n_slabs):
            ug_ref[:, c * GROUP_W + j * LANES:c * GROUP_W + (j + 1) * LANES] = slab[j]
    for c, hg_ref in enumerate(hg_refs):
        acc = _dot(h, w_ref[:, n_ug + c * GROUP_W:n_ug + (c + 1) * GROUP_W])
        hg_ref[...] = acc.reshape(bn, tq, GROUP_W).astype(hg_ref.dtype)


HG_INPUT_DTYPES = (("q", BF16), ("f_fw", F32), ("f_bw", F32), ("i_in", BF16), ("g_out", BF16))


def _od_proj(x, g, w_bf16):
    bn, s_len, d = x.shape
    tq = min(ROW_TILE_WIDE // bn, s_len)
    n_ug = 2 * LRU_W
    n_hg = w_bf16.shape[1] - n_ug
    assert n_hg == len(HG_INPUT_DTYPES) * GROUP_W
    hg_spec = pl.BlockSpec((bn, tq, GROUP_W), lambda i: (0, i, 0))
    ug, *hg = pl.pallas_call(
        _od_proj_kernel,
        out_shape=[jax.ShapeDtypeStruct((s_len * bn, n_ug), F32)]
        + [jax.ShapeDtypeStruct((bn, s_len, GROUP_W), dt) for _, dt in HG_INPUT_DTYPES],
        grid=(s_len // tq,),
        in_specs=[pl.BlockSpec((bn, tq, d), lambda i: (0, i, 0)),
                  pl.BlockSpec((1, d), lambda i: (0, 0)),
                  _resident((d, n_ug + n_hg), lambda i: (0, 0))],
        out_specs=[pl.BlockSpec((tq * bn, n_ug), lambda i: (i, 0))] + [hg_spec] * len(HG_INPUT_DTYPES),
        scratch_shapes=[pltpu.VMEM((GROUP_W // LANES, tq * bn, LANES), F32)],
        compiler_params=_cparams(("parallel",)),
        name="od_proj",
    )(x, g.reshape(1, d), w_bf16)
    return ug.reshape(s_len, bn, n_ug), dict(zip([n for n, _ in HG_INPUT_DTYPES], hg))


def _gelu_tanh(x):
    return 0.5 * x * (1.0 + jnp.tanh(math.sqrt(2.0 / math.pi) * (x + 0.044715 * (x * x * x))))


def _rglru_kernel(*refs, reverse, n_chunks):
    if reverse:
        (ul_ref, u_ref, ur_ref, gate_ref, hf_ref, cw_ref, cb_ref, wa_ref, ba_ref, wx_ref, bx_ref, sp_ref,
         out_ref, ext, a_s, b_s, h_s, carry) = refs
    else:
        (ul_ref, u_ref, ur_ref, cw_ref, cb_ref, wa_ref, ba_ref, wx_ref, bx_ref, sp_ref,
         out_ref, ext, a_s, b_s, carry) = refs
        h_s = out_ref
    step = pl.program_id(0)
    chunk = (n_chunks - 1 - step) if reverse else step
    t_len, bn, c = u_ref.shape

    @pl.when(step == 0)
    def _():
        carry[...] = jnp.zeros_like(carry)

    ext[0:2] = jnp.where(chunk == 0, 0.0, ul_ref[...])
    ext[2:t_len + 2] = u_ref[...]
    ext[t_len + 2:t_len + 3] = jnp.where(chunk == n_chunks - 1, 0.0, ur_ref[...])
    uc = cb_ref[...].reshape(1, 1, c)
    for j in range(4):
        uc = uc + cw_ref[j:j + 1, :].reshape(1, 1, c) * ext[j:j + t_len]

    u2 = uc.reshape(t_len * bn, c)
    ub = u2.astype(BF16)
    half = c // 2

    def gate_tanh(w_ref, b_ref):
        z = jnp.concatenate([_dot(ub[:, :half], w_ref[0]), _dot(ub[:, half:], w_ref[1])], axis=-1)
        return jnp.tanh(z + b_ref[...])

    tr = gate_tanh(wa_ref, ba_ref)
    ti = gate_tanh(wx_ref, bx_ref)
    a = jnp.exp2(sp_ref[...] * tr + sp_ref[...])
    b = jnp.sqrt(1.0 - a * a) * ((0.5 * ti + 0.5) * u2)
    a_s[...] = a.reshape(t_len, bn, c)
    b_s[...] = b.reshape(t_len, bn, c)

    def body(k, h):
        t = (t_len - 1 - k) if reverse else k
        h = a_s[t] * h + b_s[t]
        h_s[t] = h
        return h

    carry[...] = lax.fori_loop(0, t_len, body, carry[...], unroll=8)

    if reverse:
        out_ref[...] = ((hf_ref[...] + h_s[...]) * _gelu_tanh(gate_ref[...])).astype(out_ref.dtype)


def _rglru(ug, hf, conv_w, conv_b, wa, ba, wx, bx, lam, bn, *, reverse):
    s_len = ug.shape[0]
    c = LRU_W
    ug3 = ug
    t_len = min(LRU_T, s_len)
    n_chunks = s_len // t_len

    def ck(i):
        return (n_chunks - 1 - i) if reverse else i

    half = c // 2
    nb = LRU_BLOCKS // 2

    def dense_halves(w):
        w = 0.5 * w.astype(F32).reshape(2, nb, c // LRU_BLOCKS, c // LRU_BLOCKS)
        eye = jnp.eye(nb, dtype=F32)
        return jnp.einsum('gnij,nm->gnimj', w, eye).reshape(2, half, half).astype(BF16)

    decay_scale = ((-0.5 * LRU_C * LOG2E) * jax.nn.softplus(-lam.astype(F32))).reshape(1, c)
    vec = pl.BlockSpec((1, c), lambda i: (0, 0))
    wspec = pl.BlockSpec((2, half, half), lambda i: (0, 0, 0))
    blk = pl.BlockSpec((t_len, bn, c), lambda i: (ck(i), 0, 0))
    in_specs = [
        pl.BlockSpec((2, bn, c), lambda i: (jnp.maximum(ck(i) * (t_len // 2) - 1, 0), 0, 0)),
        blk,
        pl.BlockSpec((1, bn, c), lambda i: (jnp.minimum((ck(i) + 1) * t_len, s_len - 1), 0, 0)),
    ]
    args = [ug3, ug3, ug3]
    scratch = [pltpu.VMEM((t_len + 3, bn, c), F32), pltpu.VMEM((t_len, bn, c), F32),
               pltpu.VMEM((t_len, bn, c), F32)]
    if reverse:
        in_specs += [pl.BlockSpec((t_len, bn, c), lambda i: (ck(i), 0, 1)), blk]
        args += [ug3, hf]
        scratch.append(pltpu.VMEM((t_len, bn, c), F32))
    scratch.append(pltpu.VMEM((bn, c), F32))
    in_specs += [pl.BlockSpec((4, c), lambda i: (0, 0)), vec, wspec, vec, wspec, vec, vec]
    args += [conv_w.astype(F32), conv_b.astype(F32).reshape(1, c), dense_halves(wa),
             0.5 * ba.astype(F32).reshape(1, c), dense_halves(wx), 0.5 * bx.astype(F32).reshape(1, c),
             decay_scale]
    return pl.pallas_call(
        functools.partial(_rglru_kernel, reverse=reverse, n_chunks=n_chunks),
        out_shape=jax.ShapeDtypeStruct((s_len, bn, c), BF16 if reverse else F32),
        grid=(n_chunks,),
        in_specs=in_specs,
        out_specs=blk,
        scratch_shapes=scratch,
        compiler_params=_cparams(("arbitrary",)),
        name="rglru_bw" if reverse else "rglru_fw",
    )(*args)


def _hg_matrices(c, reverse):
    n_levels = int(math.log2(c))
    t = np.arange(c)[:, None]
    r = np.arange(c)[None, :]
    mats, masks, upper = [], [], []
    if not reverse:
        mats.append(r <= t)
        mats.append(r > t)
    else:
        mats.append(r >= t)
        mats.append(r < t)
    for lev in range(n_levels):
        half = c >> (lev + 1)
        parent = 2 * half
        start = (np.arange(c) // parent) * parent
        mid = (start + half)[:, None]
        later = (np.arange(c) % parent >= half)[:, None]
        if not reverse:
            m = np.where(later, (r >= mid) & (r <= t), (r > t) & (r < mid))
            is_q = later
        else:
            m = np.where(later, (r >= mid) & (r < t), (r >= t) & (r < mid))
            is_q = ~later
        mats.append(m)
        same_parent = (start[:, None] == start[None, :])
        masks.append(same_parent & is_q & (~is_q).T)
        upper.append(np.broadcast_to(is_q, (c, HG_DK)))
    mat = np.concatenate(mats, axis=0).astype(np.float32)
    return (np.concatenate([mat, mat], axis=1), np.stack(masks).astype(np.float32),
            np.stack(upper).astype(np.float32))


def _hgrn_kernel(*refs, reverse, n_steps, n_levels):
    if reverse:
        (q_ref, f_ref, v_ref, go_ref, of_ref, lb_ref, gn_ref, mat_ref, mask_ref, isq_ref, out_ref,
         state, x_scr, oi_scr, aux_scr, dec_scr, kk_scr) = refs
    else:
        (q_ref, f_ref, v_ref, lb_ref, mat_ref, mask_ref, isq_ref, out_ref,
         state, x_scr, oi_scr, aux_scr, dec_scr, kk_scr) = refs
    step = pl.program_id(1)
    c = HG_CHUNK
    t_len = q_ref.shape[0]
    n_chunks = t_len // c

    @pl.when(step == 0)
    def _():
        state[...] = jnp.zeros_like(state)

    def chunk_body(ci, carry):
        ch = (n_chunks - 1 - ci) if reverse else ci
        rows = slice(ch * c, (ch + 1) * c)
        edge = (c - 1) if not reverse else 0
        for hp in range(HG_HEADS // 2):
            cols2 = slice(2 * hp * HG_DK, 2 * (hp + 1) * HG_DK)
            fl = f_ref[rows, cols2]
            e = jnp.exp(-jnp.abs(fl))
            r = 1.0 / (1.0 + e)
            er = e * r
            pos = fl > 0.0
            lb = lb_ref[0:1, cols2]
            oml = lb_ref[1:2, cols2]
            g2 = jnp.maximum(jnp.log2(lb + oml * jnp.where(pos, r, er)), F32_MIN_EXPONENT)
            kk_scr[hp] = oml * jnp.where(pos, er, r)
            g_hi = g2.astype(BF16)
            g_lo = (g2 - g_hi.astype(F32)).astype(BF16)
            dec_scr[hp] = jnp.exp2(_dot(mat_ref[...], jnp.concatenate([g_hi, g_lo], axis=0)))

        for hd in range(HG_HEADS):
            cols = slice(hd * HG_DK, (hd + 1) * HG_DK)
            pcols = slice((hd % 2) * HG_DK, (hd % 2 + 1) * HG_DK)
            kk = kk_scr[hd // 2, :, pcols]
            dec = dec_scr.at[hd // 2]
            qv = q_ref[rows, cols].astype(F32)
            qq = qv * jax.nn.sigmoid(qv)
            qb, kb = qq.astype(BF16), kk.astype(BF16)
            d_in = dec[0:c, pcols]
            x_scr[hd, n_levels] = (qq * d_in).astype(BF16)
            x_scr[hd, n_levels + 1] = (kk * dec[c:2 * c, pcols]).astype(BF16)
            aux_scr[hd, 0] = jnp.broadcast_to(d_in[edge:edge + 1, :], (c, HG_DK))
            aux_scr[hd, 1] = jnp.broadcast_to(jnp.sum(qq * kk, axis=-1, keepdims=True), (c, HG_DK))
            for lev in range(n_levels):
                half = c >> (lev + 1)
                if half >= SUBLANES_BF16:
                    first_is_q = bool(reverse)
                    base = jnp.concatenate(
                        [(qb if (blk % 2 == 1) != first_is_q else kb)[blk * half:(blk + 1) * half]
                         for blk in range(c // half)], axis=0)
                else:
                    base = jnp.where(isq_ref[lev] > 0.5, qb, kb)
                x_scr[hd, lev] = base * dec[(2 + lev) * c:(3 + lev) * c, pcols].astype(BF16)

        for hd in range(HG_HEADS):
            cols = slice(hd * HG_DK, (hd + 1) * HG_DK)
            att = None
            for lev in range(n_levels):
                xl = x_scr[hd, lev]
                term = mask_ref[lev] * _nt_dot(xl, xl)
                att = term if att is None else att + term
            st = state[hd]
            oi_scr[hd] = _nt_dot(x_scr[hd, n_levels], st.astype(BF16))
            vt = v_ref[rows, cols].astype(F32).T.astype(BF16)
            state[hd] = st * aux_scr[hd, 0] + _dot(vt, x_scr[hd, n_levels + 1])
            x_scr[hd, 0] = att.astype(BF16)

        for hd in range(HG_HEADS):
            cols = slice(hd * HG_DK, (hd + 1) * HG_DK)
            vb = v_ref[rows, cols]
            o = oi_scr[hd] + _dot(x_scr[hd, 0], vb) + aux_scr[hd, 1] * vb.astype(F32)

            if reverse:
                o = o + of_ref[rows, cols]
                o = o * lax.rsqrt(jnp.mean(o * o, axis=-1, keepdims=True) + EPS) * gn_ref[...]
                gv = go_ref[rows, cols].astype(F32)
                out_ref[rows, cols] = (o * (gv * jax.nn.sigmoid(gv))).astype(out_ref.dtype)
            else:
                out_ref[rows, cols] = o
        return carry

    for ci in range(n_chunks):
        chunk_body(ci, 0)


def _hgrn(hg, o_fw, lb, gnorm_g, *, reverse):
    bn, s_len, _ = hg["q"].shape
    w = HG_HEADS * HG_DK
    t_len = min(HG_T, s_len)
    n_steps = s_len // t_len
    mats, masks, isq = _hg_matrices(HG_CHUNK, reverse)
    n_levels = masks.shape[0]
    lbf = lb.astype(F32)
    lb_tab = jnp.stack([lbf, 1.0 - lbf])

    blk = pl.BlockSpec((None, t_len, w), lambda b, i: (b, (n_steps - 1 - i) if reverse else i, 0))

    def const(shape):
        return pl.BlockSpec(shape, lambda b, i: (0,) * len(shape))

    in_specs = [blk, blk, blk]
    args = [hg["q"], hg["f_bw" if reverse else "f_fw"], hg["i_in"]]
    if reverse:
        in_specs += [blk, blk]
        args += [hg["g_out"], o_fw]
    in_specs.append(const((2, w)))
    args.append(lb_tab)
    if reverse:
        in_specs.append(const((1, HG_DK)))
        args.append(gnorm_g.astype(F32).reshape(1, HG_DK))
    in_specs += [const(mats.shape), const(masks.shape), const(isq.shape)]
    args += [jnp.asarray(mats, BF16), jnp.asarray(masks, F32), jnp.asarray(isq, BF16)]
    return pl.pallas_call(
        functools.partial(_hgrn_kernel, reverse=reverse, n_steps=n_steps, n_levels=n_levels),
        out_shape=jax.ShapeDtypeStruct((bn, s_len, w), BF16 if reverse else F32),
        grid=(bn, n_steps),
        in_specs=in_specs,
        out_specs=blk,
        scratch_shapes=[pltpu.VMEM((HG_HEADS, HG_DK, HG_DK), F32),
                        pltpu.VMEM((HG_HEADS, n_levels + 2, HG_CHUNK, HG_DK), BF16),
                        pltpu.VMEM((HG_HEADS, HG_CHUNK, HG_DK), F32),
                        pltpu.VMEM((HG_HEADS, 2, HG_CHUNK, HG_DK), F32),
                        pltpu.VMEM((HG_HEADS // 2, (n_levels + 2) * HG_CHUNK, 2 * HG_DK), F32),
                        pltpu.VMEM((HG_HEADS // 2, HG_CHUNK, 2 * HG_DK), F32)],
        compiler_params=_cparams(("parallel", "arbitrary")),
        name="hgrn_bw" if reverse else "hgrn_fw",
    )(*args)


def _even_layer(x, g, w_in, w_out, rpb, tail):
    qkv, view4, view16 = _ev_proj(x, g, w_in.astype(BF16))
    outs, stats = zip(*[_dilated_branch(view, dil) for dil, view in zip(DILATIONS, (qkv, view4, view16))])
    yb = _neighborhood(qkv, rpb)
    return _ev_tail(x, outs, stats, yb, w_out.astype(BF16), tail)


def _odd_layer(x, g, w_in, w_out, conv_w, conv_b, wa, ba, wx, bx, lam, lb_f, lb_b, gnorm_g, tail):
    bn, s_len, _ = x.shape
    ug, hg = _od_proj(x, g, w_in.astype(BF16))
    h_fw = _rglru(ug, None, conv_w, conv_b, wa[0], ba[0], wx[0], bx[0], lam[0], bn, reverse=False)
    yc = _rglru(ug, h_fw, conv_w, conv_b, wa[1], ba[1], wx[1], bx[1], lam[1], bn, reverse=True)
    o_fw = _hgrn(hg, None, lb_f, gnorm_g, reverse=False)
    yd = _hgrn(hg, o_fw, lb_b, gnorm_g, reverse=True)
    return _od_tail(x, yc.reshape(s_len, bn * LRU_W), yd, w_out.astype(BF16), tail)


def kernel(x, mem, norm_mix_g, norm_xa_g, norm_mem_g, norm_mlp_g, final_norm_g, ev_w_in, ev_w_out, na_rpb,
           od_w_in, od_w_out, conv_w, conv_b, lru_wa, lru_ba, lru_wx, lru_bx, lru_lambda, hgrn_lb_logits,
           hgrn_norm_g, xa_wq, xa_wkv, xa_wo, mlp_w1, mlp_w2):
    depth = norm_mix_g.shape[0]
    p_lb = jax.nn.softmax(hgrn_lb_logits.astype(F32), axis=0)
    lower_bounds = jnp.cumsum(p_lb, axis=0) - p_lb[0:1]
    for layer in range(depth):
        tail = dict(g_xa=norm_xa_g[layer], wq=xa_wq[layer].astype(BF16),
                    kv=_mem_kv(mem, norm_mem_g[layer], xa_wkv[layer].astype(BF16)),
                    wo=xa_wo[layer].astype(BF16), g_mlp=norm_mlp_g[layer],
                    w1=mlp_w1[layer].astype(BF16), w2=mlp_w2[layer].astype(BF16),
                    g_final=final_norm_g if layer == depth - 1 else None)
        if layer % 2 == 0:
            e = layer // 2
            x = _even_layer(x, norm_mix_g[layer], ev_w_in[e], ev_w_out[e], na_rpb[e], tail)
        else:
            o = layer // 2
            x = _odd_layer(x, norm_mix_g[layer], od_w_in[o], od_w_out[o], conv_w[o], conv_b[o],
                           lru_wa[o], lru_ba[o], lru_wx[o], lru_bx[o], lru_lambda[o],
                           lower_bounds[layer, 0], lower_bounds[layer, 1], hgrn_norm_g[o], tail)
        x = _mlp(x, tail)
    return x
```

```python
import functools
import math

import jax
import jax.numpy as jnp
import numpy as np
from jax import lax
from jax.experimental import pallas as pl
from jax.experimental.pallas import tpu as pltpu

F32 = jnp.float32
BF16 = jnp.bfloat16

D_MODEL = 1024
HEAD_DIM = 64
ROT_DIM = 16
ROPE_THETA = 500000.0
N_HEADS_A = 8
N_HEADS_B = 8
GROUP_W = N_HEADS_A * HEAD_DIM
DILATIONS = (1, 4, 16)
WIN_HALF = 64
GRID_W = 64
NA_ROWS = 8
NA_COLS = 16
LRU_W = 512
LRU_BLOCKS = 8
LRU_C = 8.0
HG_HEADS = 4
HG_DK = 128
XA_HEADS = 4
XA_DH = 256
D_FF = 4096
EPS = 1e-6
LOG2E = math.log2(math.e)
F32_MIN_EXPONENT = -126.0

LANES = 128
SUBLANES_BF16 = 16
V7X_VMEM_LIMIT_BYTES = 56 * 1024 * 1024

ROW_TILE_WIDE = 1024
ATT_Q_BLOCK = 1024
ATT_SUB = 128
NA_ROW_GROUP = 16
LRU_T = 256
HG_CHUNK = 128
HG_T = 1024
HG_GROUP = 2


def _cparams(sem):
    return pltpu.CompilerParams(dimension_semantics=sem, vmem_limit_bytes=V7X_VMEM_LIMIT_BYTES)


def _resident(shape, index_map):
    return pl.BlockSpec(shape, index_map, pipeline_mode=pl.Buffered(1))


def _rms(x, g):
    return x * lax.rsqrt(jnp.mean(x * x, axis=-1, keepdims=True) + EPS) * g


def _nt_dot(a, b):
    return lax.dot_general(a, b, (((1,), (1,)), ((), ())), preferred_element_type=F32)


def _dot(a, b):
    return jnp.dot(a, b, preferred_element_type=F32)


def _head_selectors():
    lane = lax.broadcasted_iota(jnp.int32, (1, LANES), 1)
    low = (lane < HEAD_DIM).astype(F32)
    return low.astype(BF16), (1.0 - low).astype(BF16)


def _ev_proj_kernel(x_ref, g_ref, w_ref, rc_ref, rs1_ref, rs2_ref, o_ref, o4_ref, o16_ref, slab):
    h = _rms(x_ref[...], g_ref[...]).astype(BF16)
    scale = HEAD_DIM ** -0.5 * LOG2E
    tm = x_ref.shape[0]
    n_slabs = GROUP_W // LANES
    for c in range(6):
        acc = _dot(h, w_ref[:, c * GROUP_W:(c + 1) * GROUP_W])
        if c in (0, 1):
            rc, rs1, rs2 = rc_ref[...], rs1_ref[...], rs2_ref[...]
            parts = []
            for j in range(n_slabs):
                t = acc[:, j * LANES:(j + 1) * LANES]
                t = t * rc + pltpu.roll(t, LANES - ROT_DIM // 2, 1) * rs1 + pltpu.roll(t, ROT_DIM // 2, 1) * rs2
                parts.append(t)
            acc = jnp.concatenate(parts, axis=-1)
        if c in (0, 3):
            acc = acc * scale
        o_ref[:, c * GROUP_W:(c + 1) * GROUP_W] = acc.astype(BF16)
        if c < 3:
            for j in range(n_slabs):
                slab[j] = acc[:, j * LANES:(j + 1) * LANES]
            for dil, dst in ((DILATIONS[1], o4_ref), (DILATIONS[2], o16_ref)):
                for r in range(dil):
                    for j in range(n_slabs):
                        col = r * 3 * GROUP_W + c * GROUP_W + j * LANES
                        dst[:, col:col + LANES] = slab[j, pl.ds(r, tm // dil, stride=dil), :].astype(BF16)


def _rope_tables(s_len):
    half = ROT_DIM // 2
    inv = jnp.asarray(ROPE_THETA ** (-np.arange(half) * 2.0 / ROT_DIM), F32)
    ang = jnp.arange(s_len, dtype=F32)[:, None] * inv[None, :]
    cos, sin = jnp.cos(ang), jnp.sin(ang)
    ones = jnp.ones((s_len, HEAD_DIM - ROT_DIM), F32)
    zeros = jnp.zeros((s_len, HEAD_DIM - ROT_DIM), F32)
    zh = jnp.zeros((s_len, half), F32)
    rc = jnp.concatenate([cos, cos, ones], axis=-1)
    rs1 = jnp.concatenate([-sin, zh, zeros], axis=-1)
    rs2 = jnp.concatenate([zh, sin, zeros], axis=-1)
    rep = LANES // HEAD_DIM
    return jnp.tile(rc, (1, rep)), jnp.tile(rs1, (1, rep)), jnp.tile(rs2, (1, rep))


def _ev_proj(x, g, w_bf16):
    bn, s_len, d = x.shape
    tm = min(ROW_TILE_WIDE, s_len)
    n_out = w_bf16.shape[1]
    rc, rs1, rs2 = _rope_tables(s_len)
    tbl_spec = pl.BlockSpec((tm, LANES), lambda b, i: (i, 0))
    wa = 3 * GROUP_W
    d4, d16 = DILATIONS[1], DILATIONS[2]
    return pl.pallas_call(
        _ev_proj_kernel,
        out_shape=(jax.ShapeDtypeStruct((bn, s_len, n_out), BF16),
                   jax.ShapeDtypeStruct((bn, s_len // d4, d4 * wa), BF16),
                   jax.ShapeDtypeStruct((bn, s_len // d16, d16 * wa), BF16)),
        grid=(bn, s_len // tm),
        in_specs=[
            pl.BlockSpec((None, tm, d), lambda b, i: (b, i, 0)),
            pl.BlockSpec((1, d), lambda b, i: (0, 0)),
            _resident((d, n_out), lambda b, i: (0, 0)),
            tbl_spec, tbl_spec, tbl_spec,
        ],
        out_specs=(pl.BlockSpec((None, tm, n_out), lambda b, i: (b, i, 0)),
                   pl.BlockSpec((None, tm // d4, d4 * wa), lambda b, i: (b, i, 0)),
                   pl.BlockSpec((None, tm // d16, d16 * wa), lambda b, i: (b, i, 0))),
        scratch_shapes=[pltpu.VMEM((GROUP_W // LANES, tm, LANES), F32)],
        compiler_params=_cparams(("parallel", "parallel")),
        name="ev_proj",
    )(x, g.reshape(1, d), w_bf16, rc, rs1, rs2)


def _dilated_kernel(q_ref, kl_ref, kc_ref, kr_ref, vl_ref, vc_ref, vr_ref, o_ref, stat_ref,
                    kext, vext, s_scr, p_scr, *, lq, l_total):
    i = pl.program_id(2)
    kext[0:WIN_HALF, :] = kl_ref[...]
    kext[WIN_HALF:WIN_HALF + lq, :] = kc_ref[...]
    kext[WIN_HALF + lq:, :] = kr_ref[...]
    vext[0:WIN_HALF, :] = vl_ref[...]
    vext[WIN_HALF:WIN_HALF + lq, :] = vc_ref[...]
    vext[WIN_HALF + lq:, :] = vr_ref[...]

    wk = ATT_SUB + 2 * WIN_HALF
    qi = lax.broadcasted_iota(jnp.int32, (ATT_SUB, wk), 0)
    ci = lax.broadcasted_iota(jnp.int32, (ATT_SUB, wk), 1)
    band_bias = jnp.where((ci - qi >= 0) & (ci - qi <= 2 * WIN_HALF), 0.0, -jnp.inf).astype(F32)
    crow = lax.broadcasted_iota(jnp.int32, (1, wk), 1)
    lane = lax.broadcasted_iota(jnp.int32, (ATT_SUB, LANES), 1)
    low_half = lane < HEAD_DIM
    sel_lo, sel_hi = _head_selectors()
    n_sub = lq // ATT_SUB
    n_hp = GROUP_W // LANES

    for j in range(n_sub):
        base = i * lq + j * ATT_SUB - WIN_HALF
        in_seq = (crow + base >= 0) & (crow + base < l_total)
        bias = band_bias + jnp.where(in_seq, 0.0, -jnp.inf).astype(F32)
        bias2 = jnp.concatenate([bias, bias], axis=0)
        rows = slice(j * ATT_SUB, (j + 1) * ATT_SUB)
        wrows = slice(j * ATT_SUB, j * ATT_SUB + wk)
        for hp in range(n_hp):
            cols = slice(hp * LANES, (hp + 1) * LANES)
            qp = q_ref[rows, cols]
            q2 = jnp.concatenate([qp * sel_lo, qp * sel_hi], axis=0)
            s_scr[j * n_hp + hp] = _nt_dot(q2, kext[wrows, cols]) + bias2
    for j in range(n_sub):
        stat = jnp.ones((ATT_SUB, LANES), F32)
        for hp in range(n_hp):
            t = j * n_hp + hp
            s = s_scr[t]
            m = jnp.max(s, axis=-1, keepdims=True)
            p = jnp.exp2(s - m)
            den = jnp.sum(p, axis=-1, keepdims=True)
            p_scr[t] = p.astype(BF16)
            stat = jnp.where(lane == 2 * hp, m[0:ATT_SUB], jnp.where(lane == 2 * hp + 1, m[ATT_SUB:], stat))
            stat = jnp.where(lane == N_HEADS_A + 2 * hp, den[0:ATT_SUB],
                             jnp.where(lane == N_HEADS_A + 2 * hp + 1, den[ATT_SUB:], stat))
        stat_ref[j * ATT_SUB:(j + 1) * ATT_SUB, :] = stat
    for j in range(n_sub):
        rows = slice(j * ATT_SUB, (j + 1) * ATT_SUB)
        wrows = slice(j * ATT_SUB, j * ATT_SUB + wk)
        for hp in range(n_hp):
            cols = slice(hp * LANES, (hp + 1) * LANES)
            o = _dot(p_scr[j * n_hp + hp], vext[wrows, cols])
            o_ref[rows, cols] = jnp.where(low_half, o[0:ATT_SUB], o[ATT_SUB:]).astype(BF16)


def _dilated_branch(view, dil):
    bn, l_total, width = view.shape
    n_groups = width // (dil * GROUP_W)
    lq = min(ATT_Q_BLOCK, l_total)
    nblk = l_total // lq
    per = lq // WIN_HALF
    n_halo = l_total // WIN_HALF
    n_units = (lq // ATT_SUB) * (GROUP_W // LANES)

    def cur(g):
        return pl.BlockSpec((None, lq, GROUP_W), lambda b, r, i: (b, i, r * n_groups + g))

    def left(g):
        return pl.BlockSpec((None, WIN_HALF, GROUP_W),
                            lambda b, r, i: (b, jnp.maximum(i * per - 1, 0), r * n_groups + g))

    def right(g):
        return pl.BlockSpec((None, WIN_HALF, GROUP_W),
                            lambda b, r, i: (b, jnp.minimum((i + 1) * per, n_halo - 1), r * n_groups + g))

    stat_shape = jax.ShapeDtypeStruct((bn, l_total, dil * LANES), F32)
    stat_spec = pl.BlockSpec((None, lq, LANES), lambda b, r, i: (b, i, r))
    return pl.pallas_call(
        functools.partial(_dilated_kernel, lq=lq, l_total=l_total),
        out_shape=(jax.ShapeDtypeStruct((bn, l_total, dil * GROUP_W), BF16), stat_shape),
        grid=(bn, dil, nblk),
        in_specs=[cur(0), left(1), cur(1), right(1), left(2), cur(2), right(2)],
        out_specs=(pl.BlockSpec((None, lq, GROUP_W), lambda b, r, i: (b, i, r)), stat_spec),
        scratch_shapes=[pltpu.VMEM((lq + 2 * WIN_HALF, GROUP_W), BF16),
                        pltpu.VMEM((lq + 2 * WIN_HALF, GROUP_W), BF16),
                        pltpu.VMEM((n_units, 2 * ATT_SUB, ATT_SUB + 2 * WIN_HALF), F32),
                        pltpu.VMEM((n_units, 2 * ATT_SUB, ATT_SUB + 2 * WIN_HALF), BF16)],
        compiler_params=_cparams(("parallel", "parallel", "parallel")),
        name=f"dilated_attn_d{dil}",
    )(view, view, view, view, view, view, view)


NA_QT = NA_ROW_GROUP * GRID_W
NA_KT = 2 * NA_QT
NA_KW = NA_ROWS * GRID_W


def _na_tables(rpb):
    rpb = rpb.astype(F32)
    nh = rpb.shape[0]
    ext = GRID_W - NA_COLS
    padded = jnp.concatenate([jnp.repeat(rpb[..., :1], ext, axis=-1), rpb,
                              jnp.repeat(rpb[..., -1:], ext, axis=-1)], axis=-1)
    skew = jnp.tile(padded, (1, 1, GRID_W + 1))[..., :2 * GRID_W * GRID_W]
    t1 = skew.reshape(nh, -1, GRID_W, 2 * GRID_W)[..., ::-1, :GRID_W]
    qc = np.arange(GRID_W)[:, None]
    kc = np.arange(GRID_W)[None, :]
    c0 = np.clip(qc - NA_COLS // 2, 0, GRID_W - NA_COLS)
    t1 = jnp.where((kc >= c0) & (kc < c0 + NA_COLS), t1 * LOG2E, -jnp.inf)
    per_delta = [jnp.transpose(t1[:, NA_ROWS - 1 - delta:2 * NA_ROWS - 1 - delta], (0, 2, 1, 3))
                 .reshape(nh, GRID_W, NA_KW) for delta in range(NA_ROWS)]
    return jnp.stack(per_delta).reshape(NA_ROWS, nh // 2, 2 * GRID_W, NA_KW)


def _na_kernel(q_ref, k_ref, v_ref, tb_ref, o_ref, s_scr, p_scr, inv_scr, *, rows):
    i0 = pl.program_id(1) * NA_ROW_GROUP
    w0 = jnp.clip(i0 - NA_ROWS // 2, 0, rows - 2 * NA_ROW_GROUP)
    lane = lax.broadcasted_iota(jnp.int32, (GRID_W, LANES), 1)
    low_half = lane < HEAD_DIM
    sel_lo, sel_hi = _head_selectors()
    n_hp = GROUP_W // LANES

    def key_rows(a):
        r0 = jnp.clip(i0 + a - NA_ROWS // 2, 0, rows - NA_ROWS)
        return pl.ds(pl.multiple_of((r0 - w0) * GRID_W, GRID_W), NA_KW), i0 + a - r0

    for a in range(NA_ROW_GROUP):
        krows, delta = key_rows(a)
        qrows = slice(a * GRID_W, (a + 1) * GRID_W)
        for hp in range(n_hp):
            cols = slice(hp * LANES, (hp + 1) * LANES)
            qp = q_ref[qrows, cols]
            q2 = jnp.concatenate([qp * sel_lo, qp * sel_hi], axis=0)
            s_scr[a * n_hp + hp] = _nt_dot(q2, k_ref[0, krows, cols]) + tb_ref[delta, hp]
    for t in range(NA_ROW_GROUP * n_hp):
        s = s_scr[t]
        p = jnp.exp2(s - jnp.max(s, axis=-1, keepdims=True))
        inv_scr[t] = jnp.broadcast_to(1.0 / jnp.sum(p, axis=-1, keepdims=True), (2 * GRID_W, LANES))
        p_scr[t] = p.astype(BF16)
    for a in range(NA_ROW_GROUP):
        krows, _ = key_rows(a)
        qrows = slice(a * GRID_W, (a + 1) * GRID_W)
        for hp in range(n_hp):
            cols = slice(hp * LANES, (hp + 1) * LANES)
            t = a * n_hp + hp
            o = _dot(p_scr[t], v_ref[0, krows, cols]) * inv_scr[t]
            o_ref[qrows, cols] = jnp.where(low_half, o[0:GRID_W], o[GRID_W:]).astype(BF16)


def _neighborhood(qkv, rpb):
    bn, s_len, _ = qkv.shape
    rows = s_len // GRID_W
    n_groups = rows // NA_ROW_GROUP
    tables = _na_tables(rpb)

    def window(g):
        return pl.BlockSpec(
            (pl.Element(1), pl.Element(NA_KT), pl.Element(GROUP_W)),
            lambda b, i: (b, jnp.clip(i * NA_ROW_GROUP - NA_ROWS // 2, 0, rows - 2 * NA_ROW_GROUP) * GRID_W,
                          g * GROUP_W))

    return pl.pallas_call(
        functools.partial(_na_kernel, rows=rows),
        out_shape=jax.ShapeDtypeStruct((bn, s_len, GROUP_W), BF16),
        grid=(bn, n_groups),
        in_specs=[pl.BlockSpec((None, NA_QT, GROUP_W), lambda b, i: (b, i, 3)), window(4), window(5),
                  pl.BlockSpec(tables.shape, lambda b, i: (0, 0, 0, 0))],
        out_specs=pl.BlockSpec((None, NA_QT, GROUP_W), lambda b, i: (b, i, 0)),
        scratch_shapes=[pltpu.VMEM((NA_ROW_GROUP * GROUP_W // LANES, 2 * GRID_W, NA_KW), F32),
                        pltpu.VMEM((NA_ROW_GROUP * GROUP_W // LANES, 2 * GRID_W, NA_KW), BF16),
                        pltpu.VMEM((NA_ROW_GROUP * GROUP_W // LANES, 2 * GRID_W, LANES), F32)],
        compiler_params=_cparams(("parallel", "parallel")),
        name="neighborhood_attn",
    )(qkv, qkv, qkv, tables)


def _xattn_block(x, g_ref, wq_ref, kv_ref, wo_ref):
    q = (_dot(_rms(x, g_ref[...]).astype(BF16), wq_ref[...]) * (XA_DH ** -0.5 * LOG2E)).astype(BF16)
    heads = []
    for hd in range(XA_HEADS):
        cols = slice(hd * XA_DH, (hd + 1) * XA_DH)
        s = _nt_dot(q[:, cols], kv_ref[:, cols])
        m = jnp.max(s, axis=-1, keepdims=True)
        p = jnp.exp2(s - m)
        den = jnp.sum(p, axis=-1, keepdims=True)
        vcols = slice(D_MODEL + hd * XA_DH, D_MODEL + (hd + 1) * XA_DH)
        heads.append((_dot(p.astype(BF16), kv_ref[:, vcols]) * (1.0 / den)).astype(BF16))
    return x + _dot(jnp.concatenate(heads, axis=-1), wo_ref[...])


def _mlp_block(x, g_ref, w1_ref, w2_ref, gf_ref, final_norm):
    h = _rms(x, g_ref[...]).astype(BF16)
    acc = x
    for c in range(D_FF // D_MODEL):
        cols = slice(c * D_MODEL, (c + 1) * D_MODEL)
        a = jnp.maximum(_dot(h, w1_ref[:, cols]), 0.0)
        acc = acc + _dot((a * a).astype(BF16), w2_ref[cols, :])
    return _rms(acc, gf_ref[...]) if final_norm else acc


def _xattn_operands(tail, d):
    m_len = tail["kv"].shape[1]
    specs = [pl.BlockSpec((1, d), lambda b, i: (0, 0)), _resident((d, d), lambda b, i: (0, 0)),
             pl.BlockSpec((None, m_len, 2 * d), lambda b, i: (b, 0, 0)),
             _resident((d, d), lambda b, i: (0, 0))]
    return specs, [tail["g_xa"].reshape(1, d), tail["wq"], tail["kv"], tail["wo"]]


def _mlp_kernel(x_ref, g_ref, w1_ref, w2_ref, gf_ref, out_ref, *, final_norm):
    out_ref[...] = _mlp_block(x_ref[...], g_ref, w1_ref, w2_ref, gf_ref, final_norm)


def _mlp(x, tail):
    bn, s_len, d = x.shape
    tm = min(ROW_TILE_WIDE, s_len)
    final_norm = tail["g_final"] is not None
    gf = tail["g_final"] if final_norm else tail["g_mlp"]
    x_spec = pl.BlockSpec((None, tm, d), lambda b, i: (b, i, 0))
    vec = pl.BlockSpec((1, d), lambda b, i: (0, 0))
    return pl.pallas_call(
        functools.partial(_mlp_kernel, final_norm=final_norm),
        out_shape=jax.ShapeDtypeStruct(x.shape, F32),
        grid=(bn, s_len // tm),
        in_specs=[x_spec, vec, _resident((d, D_FF), lambda b, i: (0, 0)),
                  _resident((D_FF, d), lambda b, i: (0, 0)), vec],
        out_specs=x_spec,
        compiler_params=_cparams(("parallel", "parallel")),
        name="mlp_final" if final_norm else "mlp",
    )(x, tail["g_mlp"].reshape(1, d), tail["w1"], tail["w2"], gf.reshape(1, d))


def _ev_tail_kernel(o1, o2v, o3v, s1, s2v, s3v, yb_ref, ex_ref, w_ref, x_ref,
                    gxa_ref, wq_ref, kv_ref, wo_ref, out_ref, so2, so3, st2, st3):
    tm = x_ref.shape[0]
    n_slabs = GROUP_W // LANES
    for dil, ov, sv, so, st in ((DILATIONS[1], o2v, s2v, so2, st2), (DILATIONS[2], o3v, s3v, so3, st3)):
        for r in range(dil):
            dst = pl.ds(r, tm // dil, stride=dil)
            for j in range(n_slabs):
                col = r * GROUP_W + j * LANES
                so[j, dst, :] = ov[:, col:col + LANES].astype(F32)
            st[dst, :] = sv[:, r * LANES:(r + 1) * LANES]
    o2 = jnp.concatenate([so2[j] for j in range(n_slabs)], axis=-1)
    o3 = jnp.concatenate([so3[j] for j in range(n_slabs)], axis=-1)
    ma, mb, mc = s1[...], st2[...], st3[...]
    da, db, dc = [pltpu.roll(t, LANES - N_HEADS_A, 1) for t in (ma, mb, mc)]
    m = jnp.maximum(jnp.maximum(ma, mb), mc)
    ea, eb, ec = jnp.exp2(ma - m), jnp.exp2(mb - m), jnp.exp2(mc - m)
    inv = 1.0 / (ea * da + eb * db + ec * dc)
    lane = lax.broadcasted_iota(jnp.int32, (tm, LANES), 1)
    wcat = jnp.where(lane < N_HEADS_A, ea * inv,
                     jnp.where(lane < 2 * N_HEADS_A, pltpu.roll(eb * inv, N_HEADS_A, 1),
                               pltpu.roll(ec * inv, 2 * N_HEADS_A, 1)))
    hi = wcat.astype(BF16)
    lo = (wcat - hi.astype(F32)).astype(BF16)
    wide = _dot(jnp.concatenate([hi, lo], axis=1), ex_ref[...])
    ya = (wide[:, 0:GROUP_W] * o1[...].astype(F32) + wide[:, GROUP_W:2 * GROUP_W] * o2
          + wide[:, 2 * GROUP_W:] * o3)
    y = _dot(ya.astype(BF16), w_ref[0:GROUP_W, :]) + _dot(yb_ref[...], w_ref[GROUP_W:, :])
    out_ref[...] = _xattn_block(x_ref[...] + y, gxa_ref, wq_ref, kv_ref, wo_ref)


def _ev_tail(x, outs, stats, yb, w_bf16, tail):
    bn, s_len, d = x.shape
    tm = min(ROW_TILE_WIDE, s_len)
    n_br = len(DILATIONS)
    ex = np.zeros((LANES, n_br * GROUP_W), np.float32)
    for i in range(n_br):
        for h in range(N_HEADS_A):
            ex[i * N_HEADS_A + h, i * GROUP_W + h * HEAD_DIM:i * GROUP_W + (h + 1) * HEAD_DIM] = 1.0
    ex = np.concatenate([ex, ex], axis=0)

    def o_spec(dil):
        return pl.BlockSpec((None, tm // dil, dil * GROUP_W), lambda b, i: (b, i, 0))

    def l_spec(dil):
        return pl.BlockSpec((None, tm // dil, dil * LANES), lambda b, i: (b, i, 0))

    x_spec = pl.BlockSpec((None, tm, d), lambda b, i: (b, i, 0))
    n_slabs = GROUP_W // LANES
    tail_specs, tail_args = _xattn_operands(tail, d)
    return pl.pallas_call(
        _ev_tail_kernel,
        out_shape=jax.ShapeDtypeStruct(x.shape, F32),
        grid=(bn, s_len // tm),
        in_specs=[o_spec(d_) for d_ in DILATIONS] + [l_spec(d_) for d_ in DILATIONS]
        + [o_spec(1),
           _resident(ex.shape, lambda b, i: (0, 0)),
           _resident((2 * GROUP_W, d), lambda b, i: (0, 0)),
           x_spec] + tail_specs,
        out_specs=x_spec,
        scratch_shapes=[pltpu.VMEM((n_slabs, tm, LANES), F32), pltpu.VMEM((n_slabs, tm, LANES), F32),
                        pltpu.VMEM((tm, LANES), F32), pltpu.VMEM((tm, LANES), F32)],
        compiler_params=_cparams(("parallel", "parallel")),
        name="ev_tail",
    )(*outs, *stats, yb, jnp.asarray(ex, BF16), w_bf16, x, *tail_args)


def _od_tail_kernel(yc_ref, yd_ref, w_ref, x_ref, gxa_ref, wq_ref, kv_ref, wo_ref, out_ref):
    half = yc_ref.shape[-1]
    x0 = x_ref[...] + _dot(yc_ref[...], w_ref[0:half, :]) + _dot(yd_ref[...], w_ref[half:, :])
    out_ref[...] = _xattn_block(x0, gxa_ref, wq_ref, kv_ref, wo_ref)


def _od_tail(x, yc_tm, yd, w_bf16, tail):
    bn, s_len, d = x.shape
    tm = min(ROW_TILE_WIDE, s_len)
    c = yd.shape[-1]
    x_spec = pl.BlockSpec((None, tm, d), lambda b, i: (b, i, 0))
    tail_specs, tail_args = _xattn_operands(tail, d)
    return pl.pallas_call(
        _od_tail_kernel,
        out_shape=jax.ShapeDtypeStruct(x.shape, F32),
        grid=(bn, s_len // tm),
        in_specs=[pl.BlockSpec((tm, c), lambda b, i: (i, b)),
                  pl.BlockSpec((None, tm, c), lambda b, i: (b, i, 0)),
                  _resident((2 * c, d), lambda b, i: (0, 0)),
                  x_spec] + tail_specs,
        out_specs=x_spec,
        compiler_params=_cparams(("parallel", "parallel")),
        name="od_tail",
    )(yc_tm, yd, w_bf16, x, *tail_args)


def _norm_matmul_kernel(x_ref, g_ref, w_ref, o_ref):
    o_ref[...] = _dot(_rms(x_ref[...], g_ref[...]).astype(BF16), w_ref[...]).astype(o_ref.dtype)


def _mem_kv(mem, g, w_bf16):
    bn, m_len, d = mem.shape
    n_out = w_bf16.shape[1]
    return pl.pallas_call(
        _norm_matmul_kernel,
        out_shape=jax.ShapeDtypeStruct((bn, m_len, n_out), BF16),
        grid=(bn,),
        in_specs=[pl.BlockSpec((None, m_len, d), lambda b: (b, 0, 0)),
                  pl.BlockSpec((1, d), lambda b: (0, 0)),
                  pl.BlockSpec((d, n_out), lambda b: (0, 0))],
        out_specs=pl.BlockSpec((None, m_len, n_out), lambda b: (b, 0, 0)),
        compiler_params=_cparams(("parallel",)),
        name="mem_kv",
    )(mem, g.reshape(1, d), w_bf16)


def _od_proj_kernel(x_ref, g_ref, w_ref, ug_ref, *rest):
    hg_refs, slab = rest[:-1], rest[-1]
    bn, tq, d = x_ref.shape
    h = _rms(x_ref[...].reshape(bn * tq, d), g_ref[...]).astype(BF16)
    n_ug = 2 * LRU_W
    n_slabs = GROUP_W // LANES
    for c in range(n_ug // GROUP_W):
        acc = _dot(h, w_ref[:, c * GROUP_W:(c + 1) * GROUP_W])
        for b in range(bn):
            for j in range(n_slabs):
                slab[j, pl.ds(b, tq, stride=bn), :] = acc[b * tq:(b + 1) * tq, j * LANES:(j + 1) * LANES]
        for j in range(n_slabs):
            ug_ref[:, c * GROUP_W + j * LANES:c * GROUP_W + (j + 1) * LANES] = slab[j]
    for c, hg_ref in enumerate(hg_refs):
        acc = _dot(h, w_ref[:, n_ug + c * GROUP_W:n_ug + (c + 1) * GROUP_W])
        hg_ref[...] = acc.reshape(bn, tq, GROUP_W).astype(hg_ref.dtype)


HG_INPUT_DTYPES = (("q", BF16), ("f_fw", F32), ("f_bw", F32), ("i_in", BF16), ("g_out", BF16))


def _od_proj(x, g, w_bf16):
    bn, s_len, d = x.shape
    tq = min(ROW_TILE_WIDE // bn, s_len)
    n_ug = 2 * LRU_W
    n_hg = w_bf16.shape[1] - n_ug
    assert n_hg == len(HG_INPUT_DTYPES) * GROUP_W
    hg_spec = pl.BlockSpec((bn, tq, GROUP_W), lambda i: (0, i, 0))
    ug, *hg = pl.pallas_call(
        _od_proj_kernel,
        out_shape=[jax.ShapeDtypeStruct((s_len * bn, n_ug), F32)]
        + [jax.ShapeDtypeStruct((bn, s_len, GROUP_W), dt) for _, dt in HG_INPUT_DTYPES],
        grid=(s_len // tq,),
        in_specs=[pl.BlockSpec((bn, tq, d), lambda i: (0, i, 0)),
                  pl.BlockSpec((1, d), lambda i: (0, 0)),
                  _resident((d, n_ug + n_hg), lambda i: (0, 0))],
        out_specs=[pl.BlockSpec((tq * bn, n_ug), lambda i: (i, 0))] + [hg_spec] * len(HG_INPUT_DTYPES),
        scratch_shapes=[pltpu.VMEM((GROUP_W // LANES, tq * bn, LANES), F32)],
        compiler_params=_cparams(("parallel",)),
        name="od_proj",
    )(x, g.reshape(1, d), w_bf16)
    return ug.reshape(s_len, bn, n_ug), dict(zip([n for n, _ in HG_INPUT_DTYPES], hg))


def _gelu_tanh(x):
    return 0.5 * x * (1.0 + jnp.tanh(math.sqrt(2.0 / math.pi) * (x + 0.044715 * (x * x * x))))


def _rglru_kernel(*refs, reverse, n_chunks):
    if reverse:
        (ul_ref, u_ref, ur_ref, gate_ref, hf_ref, cw_ref, cb_ref, wa_ref, ba_ref, wx_ref, bx_ref, sp_ref,
         out_ref, ext, a_s, b_s, h_s, carry) = refs
    else:
        (ul_ref, u_ref, ur_ref, cw_ref, cb_ref, wa_ref, ba_ref, wx_ref, bx_ref, sp_ref,
         out_ref, ext, a_s, b_s, carry) = refs
        h_s = out_ref
    step = pl.program_id(0)
    chunk = (n_chunks - 1 - step) if reverse else step
    t_len, bn, c = u_ref.shape

    @pl.when(step == 0)
    def _():
        carry[...] = jnp.zeros_like(carry)

    ext[0:2] = jnp.where(chunk == 0, 0.0, ul_ref[...])
    ext[2:t_len + 2] = u_ref[...]
    ext[t_len + 2:t_len + 3] = jnp.where(chunk == n_chunks - 1, 0.0, ur_ref[...])
    uc = cb_ref[...].reshape(1, 1, c)
    for j in range(4):
        uc = uc + cw_ref[j:j + 1, :].reshape(1, 1, c) * ext[j:j + t_len]

    u2 = uc.reshape(t_len * bn, c)
    ub = u2.astype(BF16)
    half = c // 2

    def gate_tanh(w_ref, b_ref):
        z = jnp.concatenate([_dot(ub[:, :half], w_ref[0]), _dot(ub[:, half:], w_ref[1])], axis=-1)
        return jnp.tanh(z + b_ref[...])

    tr = gate_tanh(wa_ref, ba_ref)
    ti = gate_tanh(wx_ref, bx_ref)
    a = jnp.exp2(sp_ref[...] * tr + sp_ref[...])
    b = jnp.sqrt(1.0 - a * a) * ((0.5 * ti + 0.5) * u2)
    a_s[...] = a.reshape(t_len, bn, c)
    b_s[...] = b.reshape(t_len, bn, c)

    def body(k, h):
        t = (t_len - 1 - k) if reverse else k
        h = a_s[t] * h + b_s[t]
        h_s[t] = h
        return h

    carry[...] = lax.fori_loop(0, t_len, body, carry[...], unroll=8)

    if reverse:
        out_ref[...] = ((hf_ref[...] + h_s[...]) * _gelu_tanh(gate_ref[...])).astype(out_ref.dtype)


def _rglru(ug, hf, conv_w, conv_b, wa, ba, wx, bx, lam, bn, *, reverse):
    s_len = ug.shape[0]
    c = LRU_W
    ug3 = ug
    t_len = min(LRU_T, s_len)
    n_chunks = s_len // t_len

    def ck(i):
        return (n_chunks - 1 - i) if reverse else i

    half = c // 2
    nb = LRU_BLOCKS // 2

    def dense_halves(w):
        w = 0.5 * w.astype(F32).reshape(2, nb, c // LRU_BLOCKS, c // LRU_BLOCKS)
        eye = jnp.eye(nb, dtype=F32)
        return jnp.einsum('gnij,nm->gnimj', w, eye).reshape(2, half, half).astype(BF16)

    decay_scale = ((-0.5 * LRU_C * LOG2E) * jax.nn.softplus(-lam.astype(F32))).reshape(1, c)
    vec = pl.BlockSpec((1, c), lambda i: (0, 0))
    wspec = pl.BlockSpec((2, half, half), lambda i: (0, 0, 0))
    blk = pl.BlockSpec((t_len, bn, c), lambda i: (ck(i), 0, 0))
    in_specs = [
        pl.BlockSpec((2, bn, c), lambda i: (jnp.maximum(ck(i) * (t_len // 2) - 1, 0), 0, 0)),
        blk,
        pl.BlockSpec((1, bn, c), lambda i: (jnp.minimum((ck(i) + 1) * t_len, s_len - 1), 0, 0)),
    ]
    args = [ug3, ug3, ug3]
    scratch = [pltpu.VMEM((t_len + 3, bn, c), F32), pltpu.VMEM((t_len, bn, c), F32),
               pltpu.VMEM((t_len, bn, c), F32)]
    if reverse:
        in_specs += [pl.BlockSpec((t_len, bn, c), lambda i: (ck(i), 0, 1)), blk]
        args += [ug3, hf]
        scratch.append(pltpu.VMEM((t_len, bn, c), F32))
    scratch.append(pltpu.VMEM((bn, c), F32))
    in_specs += [pl.BlockSpec((4, c), lambda i: (0, 0)), vec, wspec, vec, wspec, vec, vec]
    args += [conv_w.astype(F32), conv_b.astype(F32).reshape(1, c), dense_halves(wa),
             0.5 * ba.astype(F32).reshape(1, c), dense_halves(wx), 0.5 * bx.astype(F32).reshape(1, c),
             decay_scale]
    return pl.pallas_call(
        functools.partial(_rglru_kernel, reverse=reverse, n_chunks=n_chunks),
        out_shape=jax.ShapeDtypeStruct((s_len, bn, c), BF16 if reverse else F32),
        grid=(n_chunks,),
        in_specs=in_specs,
        out_specs=blk,
        scratch_shapes=scratch,
        compiler_params=_cparams(("arbitrary",)),
        name="rglru_bw" if reverse else "rglru_fw",
    )(*args)


def _hg_matrices(c, reverse):
    n_levels = int(math.log2(c))
    t = np.arange(c)[:, None]
    r = np.arange(c)[None, :]
    mats, masks, upper = [], [], []
    if not reverse:
        mats.append(r <= t)
        mats.append(r > t)
    else:
        mats.append(r >= t)
        mats.append(r < t)
    for lev in range(n_levels):
        half = c >> (lev + 1)
        parent = 2 * half
        start = (np.arange(c) // parent) * parent
        mid = (start + half)[:, None]
        later = (np.arange(c) % parent >= half)[:, None]
        if not reverse:
            m = np.where(later, (r >= mid) & (r <= t), (r > t) & (r < mid))
            is_q = later
        else:
            m = np.where(later, (r >= mid) & (r < t), (r >= t) & (r < mid))
            is_q = ~later
        mats.append(m)
        same_parent = (start[:, None] == start[None, :])
        masks.append(same_parent & is_q & (~is_q).T)
        upper.append(np.broadcast_to(is_q, (c, HG_DK)))
    mat = np.concatenate(mats, axis=0).astype(np.float32)
    return (np.concatenate([mat, mat], axis=1), np.stack(masks).astype(np.float32),
            np.stack(upper).astype(np.float32))


def _hgrn_kernel(*refs, reverse, n_steps, n_levels):
    if reverse:
        (q_ref, f_ref, v_ref, go_ref, of_ref, lb_ref, gn_ref, mat_ref, mask_ref, isq_ref, out_ref,
         state, x_scr, oi_scr, aux_scr, dec_scr, kk_scr) = refs
    else:
        (q_ref, f_ref, v_ref, lb_ref, mat_ref, mask_ref, isq_ref, out_ref,
         state, x_scr, oi_scr, aux_scr, dec_scr, kk_scr) = refs
    step = pl.program_id(1)
    c = HG_CHUNK
    t_len = q_ref.shape[0]
    n_chunks = t_len // c

    @pl.when(step == 0)
    def _():
        state[...] = jnp.zeros_like(state)

    def chunk_phases(ci, k):
        ch = (n_chunks - 1 - ci) if reverse else ci
        rows = slice(ch * c, (ch + 1) * c)
        edge = (c - 1) if not reverse else 0
        kk_s, dec_s, x_s, oi_s, aux_s = kk_scr.at[k], dec_scr.at[k], x_scr.at[k], oi_scr.at[k], aux_scr.at[k]
        for hp in range(HG_HEADS // 2):
            cols2 = slice(2 * hp * HG_DK, 2 * (hp + 1) * HG_DK)
            fl = f_ref[rows, cols2]
            e = jnp.exp(-jnp.abs(fl))
            r = 1.0 / (1.0 + e)
            er = e * r
            pos = fl > 0.0
            lb = lb_ref[0:1, cols2]
            oml = lb_ref[1:2, cols2]
            g2 = jnp.maximum(jnp.log2(lb + oml * jnp.where(pos, r, er)), F32_MIN_EXPONENT)
            kk_s[hp] = oml * jnp.where(pos, er, r)
            g_hi = g2.astype(BF16)
            g_lo = (g2 - g_hi.astype(F32)).astype(BF16)
            dec_s[hp] = jnp.exp2(_dot(mat_ref[...], jnp.concatenate([g_hi, g_lo], axis=0)))
        yield

        for hd in range(HG_HEADS):
            cols = slice(hd * HG_DK, (hd + 1) * HG_DK)
            pcols = slice((hd % 2) * HG_DK, (hd % 2 + 1) * HG_DK)
            kk = kk_s[hd // 2, :, pcols]
            dec = dec_s.at[hd // 2]
            qv = q_ref[rows, cols].astype(F32)
            qq = qv * jax.nn.sigmoid(qv)
            qb, kb = qq.astype(BF16), kk.astype(BF16)
            d_in = dec[0:c, pcols]
            x_s[hd, n_levels] = (qq * d_in).astype(BF16)
            x_s[hd, n_levels + 1] = (kk * dec[c:2 * c, pcols]).astype(BF16)
            aux_s[hd, 0] = jnp.broadcast_to(d_in[edge:edge + 1, :], (c, HG_DK))
            aux_s[hd, 1] = jnp.broadcast_to(jnp.sum(qq * kk, axis=-1, keepdims=True), (c, HG_DK))
            for lev in range(n_levels):
                half = c >> (lev + 1)
                if half >= SUBLANES_BF16:
                    first_is_q = bool(reverse)
                    base = jnp.concatenate(
                        [(qb if (blk % 2 == 1) != first_is_q else kb)[blk * half:(blk + 1) * half]
                         for blk in range(c // half)], axis=0)
                else:
                    base = jnp.where(isq_ref[lev] > 0.5, qb, kb)
                x_s[hd, lev] = base * dec[(2 + lev) * c:(3 + lev) * c, pcols].astype(BF16)
        yield

        for hd in range(HG_HEADS):
            cols = slice(hd * HG_DK, (hd + 1) * HG_DK)
            att = None
            for lev in range(n_levels):
                xl = x_s[hd, lev]
                term = mask_ref[lev] * _nt_dot(xl, xl)
                att = term if att is None else att + term
            st = state[hd]
            oi_s[hd] = _nt_dot(x_s[hd, n_levels], st.astype(BF16))
            vt = v_ref[rows, cols].astype(F32).T.astype(BF16)
            state[hd] = st * aux_s[hd, 0] + _dot(vt, x_s[hd, n_levels + 1])
            x_s[hd, 0] = att.astype(BF16)
        yield

        for hd in range(HG_HEADS):
            cols = slice(hd * HG_DK, (hd + 1) * HG_DK)
            vb = v_ref[rows, cols]
            o = oi_s[hd] + _dot(x_s[hd, 0], vb) + aux_s[hd, 1] * vb.astype(F32)

            if reverse:
                o = o + of_ref[rows, cols]
                o = o * lax.rsqrt(jnp.mean(o * o, axis=-1, keepdims=True) + EPS) * gn_ref[...]
                gv = go_ref[rows, cols].astype(F32)
                out_ref[rows, cols] = (o * (gv * jax.nn.sigmoid(gv))).astype(out_ref.dtype)
            else:
                out_ref[rows, cols] = o
        yield

    for g0 in range(0, n_chunks, HG_GROUP):
        group = [chunk_phases(ci, k) for k, ci in enumerate(range(g0, min(g0 + HG_GROUP, n_chunks)))]
        for _ in range(4):
            for phases in group:
                next(phases)


def _hgrn(hg, o_fw, lb, gnorm_g, *, reverse):
    bn, s_len, _ = hg["q"].shape
    w = HG_HEADS * HG_DK
    t_len = min(HG_T, s_len)
    n_steps = s_len // t_len
    mats, masks, isq = _hg_matrices(HG_CHUNK, reverse)
    n_levels = masks.shape[0]
    lbf = lb.astype(F32)
    lb_tab = jnp.stack([lbf, 1.0 - lbf])

    blk = pl.BlockSpec((None, t_len, w), lambda b, i: (b, (n_steps - 1 - i) if reverse else i, 0))

    def const(shape):
        return pl.BlockSpec(shape, lambda b, i: (0,) * len(shape))

    in_specs = [blk, blk, blk]
    args = [hg["q"], hg["f_bw" if reverse else "f_fw"], hg["i_in"]]
    if reverse:
        in_specs += [blk, blk]
        args += [hg["g_out"], o_fw]
    in_specs.append(const((2, w)))
    args.append(lb_tab)
    if reverse:
        in_specs.append(const((1, HG_DK)))
        args.append(gnorm_g.astype(F32).reshape(1, HG_DK))
    in_specs += [const(mats.shape), const(masks.shape), const(isq.shape)]
    args += [jnp.asarray(mats, BF16), jnp.asarray(masks, F32), jnp.asarray(isq, BF16)]
    return pl.pallas_call(
        functools.partial(_hgrn_kernel, reverse=reverse, n_steps=n_steps, n_levels=n_levels),
        out_shape=jax.ShapeDtypeStruct((bn, s_len, w), BF16 if reverse else F32),
        grid=(bn, n_steps),
        in_specs=in_specs,
        out_specs=blk,
        scratch_shapes=[pltpu.VMEM((HG_HEADS, HG_DK, HG_DK), F32),
                        pltpu.VMEM((HG_GROUP, HG_HEADS, n_levels + 2, HG_CHUNK, HG_DK), BF16),
                        pltpu.VMEM((HG_GROUP, HG_HEADS, HG_CHUNK, HG_DK), F32),
                        pltpu.VMEM((HG_GROUP, HG_HEADS, 2, HG_CHUNK, HG_DK), F32),
                        pltpu.VMEM((HG_GROUP, HG_HEADS // 2, (n_levels + 2) * HG_CHUNK, 2 * HG_DK), F32),
                        pltpu.VMEM((HG_GROUP, HG_HEADS // 2, HG_CHUNK, 2 * HG_DK), F32)],
        compiler_params=_cparams(("parallel", "arbitrary")),
        name="hgrn_bw" if reverse else "hgrn_fw",
    )(*args)


def _even_layer(x, g, w_in, w_out, rpb, tail):
    qkv, view4, view16 = _ev_proj(x, g, w_in.astype(BF16))
    outs, stats = zip(*[_dilated_branch(view, dil) for dil, view in zip(DILATIONS, (qkv, view4, view16))])
    yb = _neighborhood(qkv, rpb)
    return _ev_tail(x, outs, stats, yb, w_out.astype(BF16), tail)


def _odd_layer(x, g, w_in, w_out, conv_w, conv_b, wa, ba, wx, bx, lam, lb_f, lb_b, gnorm_g, tail):
    bn, s_len, _ = x.shape
    ug, hg = _od_proj(x, g, w_in.astype(BF16))
    h_fw = _rglru(ug, None, conv_w, conv_b, wa[0], ba[0], wx[0], bx[0], lam[0], bn, reverse=False)
    yc = _rglru(ug, h_fw, conv_w, conv_b, wa[1], ba[1], wx[1], bx[1], lam[1], bn, reverse=True)
    o_fw = _hgrn(hg, None, lb_f, gnorm_g, reverse=False)
    yd = _hgrn(hg, o_fw, lb_b, gnorm_g, reverse=True)
    return _od_tail(x, yc.reshape(s_len, bn * LRU_W), yd, w_out.astype(BF16), tail)


def kernel(x, mem, norm_mix_g, norm_xa_g, norm_mem_g, norm_mlp_g, final_norm_g, ev_w_in, ev_w_out, na_rpb,
           od_w_in, od_w_out, conv_w, conv_b, lru_wa, lru_ba, lru_wx, lru_bx, lru_lambda, hgrn_lb_logits,
           hgrn_norm_g, xa_wq, xa_wkv, xa_wo, mlp_w1, mlp_w2):
    depth = norm_mix_g.shape[0]
    p_lb = jax.nn.softmax(hgrn_lb_logits.astype(F32), axis=0)
    lower_bounds = jnp.cumsum(p_lb, axis=0) - p_lb[0:1]
    for layer in range(depth):
        tail = dict(g_xa=norm_xa_g[layer], wq=xa_wq[layer].astype(BF16),
                    kv=_mem_kv(mem, norm_mem_g[layer], xa_wkv[layer].astype(BF16)),
                    wo=xa_wo[layer].astype(BF16), g_mlp=norm_mlp_g[layer],
                    w1=mlp_w1[layer].astype(BF16), w2=mlp_w2[layer].astype(BF16),
                    g_final=final_norm_g if layer == depth - 1 else None)
        if layer % 2 == 0:
            e = layer // 2
            x = _even_layer(x, norm_mix_g[layer], ev_w_in[e], ev_w_out[e], na_rpb[e], tail)
        else:
            o = layer // 2
            x = _odd_layer(x, norm_mix_g[layer], od_w_in[o], od_w_out[o], conv_w[o], conv_b[o],
                           lru_wa[o], lru_ba[o], lru_wx[o], lru_bx[o], lru_lambda[o],
                           lower_bounds[layer, 0], lower_bounds[layer, 1], hgrn_norm_g[o], tail)
        x = _mlp(x, tail)
    return x
```

```python
import functools
import math

import jax
import jax.numpy as jnp
import numpy as np
from jax import lax
from jax.experimental import pallas as pl
from jax.experimental.pallas import tpu as pltpu

F32 = jnp.float32
BF16 = jnp.bfloat16

D_MODEL = 1024
HEAD_DIM = 64
ROT_DIM = 16
ROPE_THETA = 500000.0
N_HEADS_A = 8
N_HEADS_B = 8
GROUP_W = N_HEADS_A * HEAD_DIM
DILATIONS = (1, 4, 16)
WIN_HALF = 64
GRID_W = 64
NA_ROWS = 8
NA_COLS = 16
LRU_W = 512
LRU_BLOCKS = 8
LRU_C = 8.0
HG_HEADS = 4
HG_DK = 128
XA_HEADS = 4
XA_DH = 256
D_FF = 4096
EPS = 1e-6
LOG2E = math.log2(math.e)
F32_MIN_EXPONENT = -126.0

LANES = 128
SUBLANES_BF16 = 16
V7X_VMEM_LIMIT_BYTES = 56 * 1024 * 1024

ROW_TILE_WIDE = 1024
ATT_Q_BLOCK = 1024
ATT_SUB = 128
NA_ROW_GROUP = 16
LRU_T = 256
HG_CHUNK = 128
HG_T = 1024
HG_GROUP = 4


def _cparams(sem):
    return pltpu.CompilerParams(dimension_semantics=sem, vmem_limit_bytes=V7X_VMEM_LIMIT_BYTES)


def _resident(shape, index_map):
    return pl.BlockSpec(shape, index_map, pipeline_mode=pl.Buffered(1))


def _rms(x, g):
    return x * lax.rsqrt(jnp.mean(x * x, axis=-1, keepdims=True) + EPS) * g


def _nt_dot(a, b):
    return lax.dot_general(a, b, (((1,), (1,)), ((), ())), preferred_element_type=F32)


def _dot(a, b):
    return jnp.dot(a, b, preferred_element_type=F32)


def _head_selectors():
    lane = lax.broadcasted_iota(jnp.int32, (1, LANES), 1)
    low = (lane < HEAD_DIM).astype(F32)
    return low.astype(BF16), (1.0 - low).astype(BF16)


def _ev_proj_kernel(x_ref, g_ref, w_ref, rc_ref, rs1_ref, rs2_ref, o_ref, o4_ref, o16_ref, slab):
    h = _rms(x_ref[...], g_ref[...]).astype(BF16)
    scale = HEAD_DIM ** -0.5 * LOG2E
    tm = x_ref.shape[0]
    n_slabs = GROUP_W // LANES
    for c in range(6):
        acc = _dot(h, w_ref[:, c * GROUP_W:(c + 1) * GROUP_W])
        if c in (0, 1):
            rc, rs1, rs2 = rc_ref[...], rs1_ref[...], rs2_ref[...]
            parts = []
            for j in range(n_slabs):
                t = acc[:, j * LANES:(j + 1) * LANES]
                t = t * rc + pltpu.roll(t, LANES - ROT_DIM // 2, 1) * rs1 + pltpu.roll(t, ROT_DIM // 2, 1) * rs2
                parts.append(t)
            acc = jnp.concatenate(parts, axis=-1)
        if c in (0, 3):
            acc = acc * scale
        o_ref[:, c * GROUP_W:(c + 1) * GROUP_W] = acc.astype(BF16)
        if c < 3:
            for j in range(n_slabs):
                slab[j] = acc[:, j * LANES:(j + 1) * LANES]
            for dil, dst in ((DILATIONS[1], o4_ref), (DILATIONS[2], o16_ref)):
                for r in range(dil):
                    for j in range(n_slabs):
                        col = r * 3 * GROUP_W + c * GROUP_W + j * LANES
                        dst[:, col:col + LANES] = slab[j, pl.ds(r, tm // dil, stride=dil), :].astype(BF16)


def _rope_tables(s_len):
    half = ROT_DIM // 2
    inv = jnp.asarray(ROPE_THETA ** (-np.arange(half) * 2.0 / ROT_DIM), F32)
    ang = jnp.arange(s_len, dtype=F32)[:, None] * inv[None, :]
    cos, sin = jnp.cos(ang), jnp.sin(ang)
    ones = jnp.ones((s_len, HEAD_DIM - ROT_DIM), F32)
    zeros = jnp.zeros((s_len, HEAD_DIM - ROT_DIM), F32)
    zh = jnp.zeros((s_len, half), F32)
    rc = jnp.concatenate([cos, cos, ones], axis=-1)
    rs1 = jnp.concatenate([-sin, zh, zeros], axis=-1)
    rs2 = jnp.concatenate([zh, sin, zeros], axis=-1)
    rep = LANES // HEAD_DIM
    return jnp.tile(rc, (1, rep)), jnp.tile(rs1, (1, rep)), jnp.tile(rs2, (1, rep))


def _ev_proj(x, g, w_bf16):
    bn, s_len, d = x.shape
    tm = min(ROW_TILE_WIDE, s_len)
    n_out = w_bf16.shape[1]
    rc, rs1, rs2 = _rope_tables(s_len)
    tbl_spec = pl.BlockSpec((tm, LANES), lambda b, i: (i, 0))
    wa = 3 * GROUP_W
    d4, d16 = DILATIONS[1], DILATIONS[2]
    return pl.pallas_call(
        _ev_proj_kernel,
        out_shape=(jax.ShapeDtypeStruct((bn, s_len, n_out), BF16),
                   jax.ShapeDtypeStruct((bn, s_len // d4, d4 * wa), BF16),
                   jax.ShapeDtypeStruct((bn, s_len // d16, d16 * wa), BF16)),
        grid=(bn, s_len // tm),
        in_specs=[
            pl.BlockSpec((None, tm, d), lambda b, i: (b, i, 0)),
            pl.BlockSpec((1, d), lambda b, i: (0, 0)),
            _resident((d, n_out), lambda b, i: (0, 0)),
            tbl_spec, tbl_spec, tbl_spec,
        ],
        out_specs=(pl.BlockSpec((None, tm, n_out), lambda b, i: (b, i, 0)),
                   pl.BlockSpec((None, tm // d4, d4 * wa), lambda b, i: (b, i, 0)),
                   pl.BlockSpec((None, tm // d16, d16 * wa), lambda b, i: (b, i, 0))),
        scratch_shapes=[pltpu.VMEM((GROUP_W // LANES, tm, LANES), F32)],
        compiler_params=_cparams(("parallel", "parallel")),
        name="ev_proj",
    )(x, g.reshape(1, d), w_bf16, rc, rs1, rs2)


def _dilated_kernel(q_ref, kl_ref, kc_ref, kr_ref, vl_ref, vc_ref, vr_ref, o_ref, stat_ref,
                    kext, vext, s_scr, p_scr, *, lq, l_total):
    i = pl.program_id(2)
    kext[0:WIN_HALF, :] = kl_ref[...]
    kext[WIN_HALF:WIN_HALF + lq, :] = kc_ref[...]
    kext[WIN_HALF + lq:, :] = kr_ref[...]
    vext[0:WIN_HALF, :] = vl_ref[...]
    vext[WIN_HALF:WIN_HALF + lq, :] = vc_ref[...]
    vext[WIN_HALF + lq:, :] = vr_ref[...]

    wk = ATT_SUB + 2 * WIN_HALF
    qi = lax.broadcasted_iota(jnp.int32, (ATT_SUB, wk), 0)
    ci = lax.broadcasted_iota(jnp.int32, (ATT_SUB, wk), 1)
    band_bias = jnp.where((ci - qi >= 0) & (ci - qi <= 2 * WIN_HALF), 0.0, -jnp.inf).astype(F32)
    crow = lax.broadcasted_iota(jnp.int32, (1, wk), 1)
    lane = lax.broadcasted_iota(jnp.int32, (ATT_SUB, LANES), 1)
    low_half = lane < HEAD_DIM
    sel_lo, sel_hi = _head_selectors()
    n_sub = lq // ATT_SUB
    n_hp = GROUP_W // LANES

    for j in range(n_sub):
        base = i * lq + j * ATT_SUB - WIN_HALF
        in_seq = (crow + base >= 0) & (crow + base < l_total)
        bias = band_bias + jnp.where(in_seq, 0.0, -jnp.inf).astype(F32)
        bias2 = jnp.concatenate([bias, bias], axis=0)
        rows = slice(j * ATT_SUB, (j + 1) * ATT_SUB)
        wrows = slice(j * ATT_SUB, j * ATT_SUB + wk)
        for hp in range(n_hp):
            cols = slice(hp * LANES, (hp + 1) * LANES)
            qp = q_ref[rows, cols]
            q2 = jnp.concatenate([qp * sel_lo, qp * sel_hi], axis=0)
            s_scr[j * n_hp + hp] = _nt_dot(q2, kext[wrows, cols]) + bias2
    for j in range(n_sub):
        stat = jnp.ones((ATT_SUB, LANES), F32)
        for hp in range(n_hp):
            t = j * n_hp + hp
            s = s_scr[t]
            m = jnp.max(s, axis=-1, keepdims=True)
            p = jnp.exp2(s - m)
            den = jnp.sum(p, axis=-1, keepdims=True)
            p_scr[t] = p.astype(BF16)
            stat = jnp.where(lane == 2 * hp, m[0:ATT_SUB], jnp.where(lane == 2 * hp + 1, m[ATT_SUB:], stat))
            stat = jnp.where(lane == N_HEADS_A + 2 * hp, den[0:ATT_SUB],
                             jnp.where(lane == N_HEADS_A + 2 * hp + 1, den[ATT_SUB:], stat))
        stat_ref[j * ATT_SUB:(j + 1) * ATT_SUB, :] = stat
    for j in range(n_sub):
        rows = slice(j * ATT_SUB, (j + 1) * ATT_SUB)
        wrows = slice(j * ATT_SUB, j * ATT_SUB + wk)
        for hp in range(n_hp):
            cols = slice(hp * LANES, (hp + 1) * LANES)
            o = _dot(p_scr[j * n_hp + hp], vext[wrows, cols])
            o_ref[rows, cols] = jnp.where(low_half, o[0:ATT_SUB], o[ATT_SUB:]).astype(BF16)


def _dilated_branch(view, dil):
    bn, l_total, width = view.shape
    n_groups = width // (dil * GROUP_W)
    lq = min(ATT_Q_BLOCK, l_total)
    nblk = l_total // lq
    per = lq // WIN_HALF
    n_halo = l_total // WIN_HALF
    n_units = (lq // ATT_SUB) * (GROUP_W // LANES)

    def cur(g):
        return pl.BlockSpec((None, lq, GROUP_W), lambda b, r, i: (b, i, r * n_groups + g))

    def left(g):
        return pl.BlockSpec((None, WIN_HALF, GROUP_W),
                            lambda b, r, i: (b, jnp.maximum(i * per - 1, 0), r * n_groups + g))

    def right(g):
        return pl.BlockSpec((None, WIN_HALF, GROUP_W),
                            lambda b, r, i: (b, jnp.minimum((i + 1) * per, n_halo - 1), r * n_groups + g))

    stat_shape = jax.ShapeDtypeStruct((bn, l_total, dil * LANES), F32)
    stat_spec = pl.BlockSpec((None, lq, LANES), lambda b, r, i: (b, i, r))
    return pl.pallas_call(
        functools.partial(_dilated_kernel, lq=lq, l_total=l_total),
        out_shape=(jax.ShapeDtypeStruct((bn, l_total, dil * GROUP_W), BF16), stat_shape),
        grid=(bn, dil, nblk),
        in_specs=[cur(0), left(1), cur(1), right(1), left(2), cur(2), right(2)],
        out_specs=(pl.BlockSpec((None, lq, GROUP_W), lambda b, r, i: (b, i, r)), stat_spec),
        scratch_shapes=[pltpu.VMEM((lq + 2 * WIN_HALF, GROUP_W), BF16),
                        pltpu.VMEM((lq + 2 * WIN_HALF, GROUP_W), BF16),
                        pltpu.VMEM((n_units, 2 * ATT_SUB, ATT_SUB + 2 * WIN_HALF), F32),
                        pltpu.VMEM((n_units, 2 * ATT_SUB, ATT_SUB + 2 * WIN_HALF), BF16)],
        compiler_params=_cparams(("parallel", "parallel", "parallel")),
        name=f"dilated_attn_d{dil}",
    )(view, view, view, view, view, view, view)


NA_QT = NA_ROW_GROUP * GRID_W
NA_KT = 2 * NA_QT
NA_KW = NA_ROWS * GRID_W


def _na_tables(rpb):
    rpb = rpb.astype(F32)
    nh = rpb.shape[0]
    ext = GRID_W - NA_COLS
    padded = jnp.concatenate([jnp.repeat(rpb[..., :1], ext, axis=-1), rpb,
                              jnp.repeat(rpb[..., -1:], ext, axis=-1)], axis=-1)
    skew = jnp.tile(padded, (1, 1, GRID_W + 1))[..., :2 * GRID_W * GRID_W]
    t1 = skew.reshape(nh, -1, GRID_W, 2 * GRID_W)[..., ::-1, :GRID_W]
    qc = np.arange(GRID_W)[:, None]
    kc = np.arange(GRID_W)[None, :]
    c0 = np.clip(qc - NA_COLS // 2, 0, GRID_W - NA_COLS)
    t1 = jnp.where((kc >= c0) & (kc < c0 + NA_COLS), t1 * LOG2E, -jnp.inf)
    per_delta = [jnp.transpose(t1[:, NA_ROWS - 1 - delta:2 * NA_ROWS - 1 - delta], (0, 2, 1, 3))
                 .reshape(nh, GRID_W, NA_KW) for delta in range(NA_ROWS)]
    return jnp.stack(per_delta).reshape(NA_ROWS, nh // 2, 2 * GRID_W, NA_KW)


def _na_kernel(q_ref, k_ref, v_ref, tb_ref, o_ref, s_scr, p_scr, inv_scr, *, rows):
    i0 = pl.program_id(1) * NA_ROW_GROUP
    w0 = jnp.clip(i0 - NA_ROWS // 2, 0, rows - 2 * NA_ROW_GROUP)
    lane = lax.broadcasted_iota(jnp.int32, (GRID_W, LANES), 1)
    low_half = lane < HEAD_DIM
    sel_lo, sel_hi = _head_selectors()
    n_hp = GROUP_W // LANES

    def key_rows(a):
        r0 = jnp.clip(i0 + a - NA_ROWS // 2, 0, rows - NA_ROWS)
        return pl.ds(pl.multiple_of((r0 - w0) * GRID_W, GRID_W), NA_KW), i0 + a - r0

    for a in range(NA_ROW_GROUP):
        krows, delta = key_rows(a)
        qrows = slice(a * GRID_W, (a + 1) * GRID_W)
        for hp in range(n_hp):
            cols = slice(hp * LANES, (hp + 1) * LANES)
            qp = q_ref[qrows, cols]
            q2 = jnp.concatenate([qp * sel_lo, qp * sel_hi], axis=0)
            s_scr[a * n_hp + hp] = _nt_dot(q2, k_ref[0, krows, cols]) + tb_ref[delta, hp]
    for t in range(NA_ROW_GROUP * n_hp):
        s = s_scr[t]
        p = jnp.exp2(s - jnp.max(s, axis=-1, keepdims=True))
        inv_scr[t] = jnp.broadcast_to(1.0 / jnp.sum(p, axis=-1, keepdims=True), (2 * GRID_W, LANES))
        p_scr[t] = p.astype(BF16)
    for a in range(NA_ROW_GROUP):
        krows, _ = key_rows(a)
        qrows = slice(a * GRID_W, (a + 1) * GRID_W)
        for hp in range(n_hp):
            cols = slice(hp * LANES, (hp + 1) * LANES)
            t = a * n_hp + hp
            o = _dot(p_scr[t], v_ref[0, krows, cols]) * inv_scr[t]
            o_ref[qrows, cols] = jnp.where(low_half, o[0:GRID_W], o[GRID_W:]).astype(BF16)


def _neighborhood(qkv, rpb):
    bn, s_len, _ = qkv.shape
    rows = s_len // GRID_W
    n_groups = rows // NA_ROW_GROUP
    tables = _na_tables(rpb)

    def window(g):
        return pl.BlockSpec(
            (pl.Element(1), pl.Element(NA_KT), pl.Element(GROUP_W)),
            lambda b, i: (b, jnp.clip(i * NA_ROW_GROUP - NA_ROWS // 2, 0, rows - 2 * NA_ROW_GROUP) * GRID_W,
                          g * GROUP_W))

    return pl.pallas_call(
        functools.partial(_na_kernel, rows=rows),
        out_shape=jax.ShapeDtypeStruct((bn, s_len, GROUP_W), BF16),
        grid=(bn, n_groups),
        in_specs=[pl.BlockSpec((None, NA_QT, GROUP_W), lambda b, i: (b, i, 3)), window(4), window(5),
                  pl.BlockSpec(tables.shape, lambda b, i: (0, 0, 0, 0))],
        out_specs=pl.BlockSpec((None, NA_QT, GROUP_W), lambda b, i: (b, i, 0)),
        scratch_shapes=[pltpu.VMEM((NA_ROW_GROUP * GROUP_W // LANES, 2 * GRID_W, NA_KW), F32),
                        pltpu.VMEM((NA_ROW_GROUP * GROUP_W // LANES, 2 * GRID_W, NA_KW), BF16),
                        pltpu.VMEM((NA_ROW_GROUP * GROUP_W // LANES, 2 * GRID_W, LANES), F32)],
        compiler_params=_cparams(("parallel", "parallel")),
        name="neighborhood_attn",
    )(qkv, qkv, qkv, tables)


def _xattn_block(x, g_ref, wq_ref, kv_ref, wo_ref):
    q = (_dot(_rms(x, g_ref[...]).astype(BF16), wq_ref[...]) * (XA_DH ** -0.5 * LOG2E)).astype(BF16)
    heads = []
    for hd in range(XA_HEADS):
        cols = slice(hd * XA_DH, (hd + 1) * XA_DH)
        s = _nt_dot(q[:, cols], kv_ref[:, cols])
        m = jnp.max(s, axis=-1, keepdims=True)
        p = jnp.exp2(s - m)
        den = jnp.sum(p, axis=-1, keepdims=True)
        vcols = slice(D_MODEL + hd * XA_DH, D_MODEL + (hd + 1) * XA_DH)
        heads.append((_dot(p.astype(BF16), kv_ref[:, vcols]) * (1.0 / den)).astype(BF16))
    return x + _dot(jnp.concatenate(heads, axis=-1), wo_ref[...])


def _mlp_block(x, g_ref, w1_ref, w2_ref, gf_ref, final_norm):
    h = _rms(x, g_ref[...]).astype(BF16)
    acc = x
    for c in range(D_FF // D_MODEL):
        cols = slice(c * D_MODEL, (c + 1) * D_MODEL)
        a = jnp.maximum(_dot(h, w1_ref[:, cols]), 0.0)
        acc = acc + _dot((a * a).astype(BF16), w2_ref[cols, :])
    return _rms(acc, gf_ref[...]) if final_norm else acc


def _xattn_operands(tail, d):
    m_len = tail["kv"].shape[1]
    specs = [pl.BlockSpec((1, d), lambda b, i: (0, 0)), _resident((d, d), lambda b, i: (0, 0)),
             pl.BlockSpec((None, m_len, 2 * d), lambda b, i: (b, 0, 0)),
             _resident((d, d), lambda b, i: (0, 0))]
    return specs, [tail["g_xa"].reshape(1, d), tail["wq"], tail["kv"], tail["wo"]]


def _mlp_kernel(x_ref, g_ref, w1_ref, w2_ref, gf_ref, out_ref, *, final_norm):
    out_ref[...] = _mlp_block(x_ref[...], g_ref, w1_ref, w2_ref, gf_ref, final_norm)


def _mlp(x, tail):
    bn, s_len, d = x.shape
    tm = min(ROW_TILE_WIDE, s_len)
    final_norm = tail["g_final"] is not None
    gf = tail["g_final"] if final_norm else tail["g_mlp"]
    x_spec = pl.BlockSpec((None, tm, d), lambda b, i: (b, i, 0))
    vec = pl.BlockSpec((1, d), lambda b, i: (0, 0))
    return pl.pallas_call(
        functools.partial(_mlp_kernel, final_norm=final_norm),
        out_shape=jax.ShapeDtypeStruct(x.shape, F32),
        grid=(bn, s_len // tm),
        in_specs=[x_spec, vec, _resident((d, D_FF), lambda b, i: (0, 0)),
                  _resident((D_FF, d), lambda b, i: (0, 0)), vec],
        out_specs=x_spec,
        compiler_params=_cparams(("parallel", "parallel")),
        name="mlp_final" if final_norm else "mlp",
    )(x, tail["g_mlp"].reshape(1, d), tail["w1"], tail["w2"], gf.reshape(1, d))


def _ev_tail_kernel(o1, o2v, o3v, s1, s2v, s3v, yb_ref, ex_ref, w_ref, x_ref,
                    gxa_ref, wq_ref, kv_ref, wo_ref, out_ref, so2, so3, st2, st3):
    tm = x_ref.shape[0]
    n_slabs = GROUP_W // LANES
    for dil, ov, sv, so, st in ((DILATIONS[1], o2v, s2v, so2, st2), (DILATIONS[2], o3v, s3v, so3, st3)):
        for r in range(dil):
            dst = pl.ds(r, tm // dil, stride=dil)
            for j in range(n_slabs):
                col = r * GROUP_W + j * LANES
                so[j, dst, :] = ov[:, col:col + LANES].astype(F32)
            st[dst, :] = sv[:, r * LANES:(r + 1) * LANES]
    o2 = jnp.concatenate([so2[j] for j in range(n_slabs)], axis=-1)
    o3 = jnp.concatenate([so3[j] for j in range(n_slabs)], axis=-1)
    ma, mb, mc = s1[...], st2[...], st3[...]
    da, db, dc = [pltpu.roll(t, LANES - N_HEADS_A, 1) for t in (ma, mb, mc)]
    m = jnp.maximum(jnp.maximum(ma, mb), mc)
    ea, eb, ec = jnp.exp2(ma - m), jnp.exp2(mb - m), jnp.exp2(mc - m)
    inv = 1.0 / (ea * da + eb * db + ec * dc)
    lane = lax.broadcasted_iota(jnp.int32, (tm, LANES), 1)
    wcat = jnp.where(lane < N_HEADS_A, ea * inv,
                     jnp.where(lane < 2 * N_HEADS_A, pltpu.roll(eb * inv, N_HEADS_A, 1),
                               pltpu.roll(ec * inv, 2 * N_HEADS_A, 1)))
    hi = wcat.astype(BF16)
    lo = (wcat - hi.astype(F32)).astype(BF16)
    wide = _dot(jnp.concatenate([hi, lo], axis=1), ex_ref[...])
    ya = (wide[:, 0:GROUP_W] * o1[...].astype(F32) + wide[:, GROUP_W:2 * GROUP_W] * o2
          + wide[:, 2 * GROUP_W:] * o3)
    y = _dot(ya.astype(BF16), w_ref[0:GROUP_W, :]) + _dot(yb_ref[...], w_ref[GROUP_W:, :])
    out_ref[...] = _xattn_block(x_ref[...] + y, gxa_ref, wq_ref, kv_ref, wo_ref)


def _ev_tail(x, outs, stats, yb, w_bf16, tail):
    bn, s_len, d = x.shape
    tm = min(ROW_TILE_WIDE, s_len)
    n_br = len(DILATIONS)
    ex = np.zeros((LANES, n_br * GROUP_W), np.float32)
    for i in range(n_br):
        for h in range(N_HEADS_A):
            ex[i * N_HEADS_A + h, i * GROUP_W + h * HEAD_DIM:i * GROUP_W + (h + 1) * HEAD_DIM] = 1.0
    ex = np.concatenate([ex, ex], axis=0)

    def o_spec(dil):
        return pl.BlockSpec((None, tm // dil, dil * GROUP_W), lambda b, i: (b, i, 0))

    def l_spec(dil):
        return pl.BlockSpec((None, tm // dil, dil * LANES), lambda b, i: (b, i, 0))

    x_spec = pl.BlockSpec((None, tm, d), lambda b, i: (b, i, 0))
    n_slabs = GROUP_W // LANES
    tail_specs, tail_args = _xattn_operands(tail, d)
    return pl.pallas_call(
        _ev_tail_kernel,
        out_shape=jax.ShapeDtypeStruct(x.shape, F32),
        grid=(bn, s_len // tm),
        in_specs=[o_spec(d_) for d_ in DILATIONS] + [l_spec(d_) for d_ in DILATIONS]
        + [o_spec(1),
           _resident(ex.shape, lambda b, i: (0, 0)),
           _resident((2 * GROUP_W, d), lambda b, i: (0, 0)),
           x_spec] + tail_specs,
        out_specs=x_spec,
        scratch_shapes=[pltpu.VMEM((n_slabs, tm, LANES), F32), pltpu.VMEM((n_slabs, tm, LANES), F32),
                        pltpu.VMEM((tm, LANES), F32), pltpu.VMEM((tm, LANES), F32)],
        compiler_params=_cparams(("parallel", "parallel")),
        name="ev_tail",
    )(*outs, *stats, yb, jnp.asarray(ex, BF16), w_bf16, x, *tail_args)


def _od_tail_kernel(yc_ref, yd_ref, w_ref, x_ref, gxa_ref, wq_ref, kv_ref, wo_ref, out_ref):
    half = yc_ref.shape[-1]
    x0 = x_ref[...] + _dot(yc_ref[...], w_ref[0:half, :]) + _dot(yd_ref[...], w_ref[half:, :])
    out_ref[...] = _xattn_block(x0, gxa_ref, wq_ref, kv_ref, wo_ref)


def _od_tail(x, yc_tm, yd, w_bf16, tail):
    bn, s_len, d = x.shape
    tm = min(ROW_TILE_WIDE, s_len)
    c = yd.shape[-1]
    x_spec = pl.BlockSpec((None, tm, d), lambda b, i: (b, i, 0))
    tail_specs, tail_args = _xattn_operands(tail, d)
    return pl.pallas_call(
        _od_tail_kernel,
        out_shape=jax.ShapeDtypeStruct(x.shape, F32),
        grid=(bn, s_len // tm),
        in_specs=[pl.BlockSpec((tm, c), lambda b, i: (i, b)),
                  pl.BlockSpec((None, tm, c), lambda b, i: (b, i, 0)),
                  _resident((2 * c, d), lambda b, i: (0, 0)),
                  x_spec] + tail_specs,
        out_specs=x_spec,
        compiler_params=_cparams(("parallel", "parallel")),
        name="od_tail",
    )(yc_tm, yd, w_bf16, x, *tail_args)


def _norm_matmul_kernel(x_ref, g_ref, w_ref, o_ref):
    o_ref[...] = _dot(_rms(x_ref[...], g_ref[...]).astype(BF16), w_ref[...]).astype(o_ref.dtype)


def _mem_kv(mem, g, w_bf16):
    bn, m_len, d = mem.shape
    n_out = w_bf16.shape[1]
    return pl.pallas_call(
        _norm_matmul_kernel,
        out_shape=jax.ShapeDtypeStruct((bn, m_len, n_out), BF16),
        grid=(bn,),
        in_specs=[pl.BlockSpec((None, m_len, d), lambda b: (b, 0, 0)),
                  pl.BlockSpec((1, d), lambda b: (0, 0)),
                  pl.BlockSpec((d, n_out), lambda b: (0, 0))],
        out_specs=pl.BlockSpec((None, m_len, n_out), lambda b: (b, 0, 0)),
        compiler_params=_cparams(("parallel",)),
        name="mem_kv",
    )(mem, g.reshape(1, d), w_bf16)


def _od_proj_kernel(x_ref, g_ref, w_ref, ug_ref, *rest):
    hg_refs, slab = rest[:-1], rest[-1]
    bn, tq, d = x_ref.shape
    h = _rms(x_ref[...].reshape(bn * tq, d), g_ref[...]).astype(BF16)
    n_ug = 2 * LRU_W
    n_slabs = GROUP_W // LANES
    for c in range(n_ug // GROUP_W):
        acc = _dot(h, w_ref[:, c * GROUP_W:(c + 1) * GROUP_W])
        for b in range(bn):
            for j in range(n_slabs):
                slab[j, pl.ds(b, tq, stride=bn), :] = acc[b * tq:(b + 1) * tq, j * LANES:(j + 1) * LANES]
        for j in range(n_slabs):
            ug_ref[:, c * GROUP_W + j * LANES:c * GROUP_W + (j + 1) * LANES] = slab[j]
    for c, hg_ref in enumerate(hg_refs):
        acc = _dot(h, w_ref[:, n_ug + c * GROUP_W:n_ug + (c + 1) * GROUP_W])
        hg_ref[...] = acc.reshape(bn, tq, GROUP_W).astype(hg_ref.dtype)


HG_INPUT_DTYPES = (("q", BF16), ("f_fw", F32), ("f_bw", F32), ("i_in", BF16), ("g_out", BF16))


def _od_proj(x, g, w_bf16):
    bn, s_len, d = x.shape
    tq = min(ROW_TILE_WIDE // bn, s_len)
    n_ug = 2 * LRU_W
    n_hg = w_bf16.shape[1] - n_ug
    assert n_hg == len(HG_INPUT_DTYPES) * GROUP_W
    hg_spec = pl.BlockSpec((bn, tq, GROUP_W), lambda i: (0, i, 0))
    ug, *hg = pl.pallas_call(
        _od_proj_kernel,
        out_shape=[jax.ShapeDtypeStruct((s_len * bn, n_ug), F32)]
        + [jax.ShapeDtypeStruct((bn, s_len, GROUP_W), dt) for _, dt in HG_INPUT_DTYPES],
        grid=(s_len // tq,),
        in_specs=[pl.BlockSpec((bn, tq, d), lambda i: (0, i, 0)),
                  pl.BlockSpec((1, d), lambda i: (0, 0)),
                  _resident((d, n_ug + n_hg), lambda i: (0, 0))],
        out_specs=[pl.BlockSpec((tq * bn, n_ug), lambda i: (i, 0))] + [hg_spec] * len(HG_INPUT_DTYPES),
        scratch_shapes=[pltpu.VMEM((GROUP_W // LANES, tq * bn, LANES), F32)],
        compiler_params=_cparams(("parallel",)),
        name="od_proj",
    )(x, g.reshape(1, d), w_bf16)
    return ug.reshape(s_len, bn, n_ug), dict(zip([n for n, _ in HG_INPUT_DTYPES], hg))


def _gelu_tanh(x):
    return 0.5 * x * (1.0 + jnp.tanh(math.sqrt(2.0 / math.pi) * (x + 0.044715 * (x * x * x))))


def _rglru_kernel(*refs, reverse, n_chunks):
    if reverse:
        (ul_ref, u_ref, ur_ref, gate_ref, hf_ref, cw_ref, cb_ref, wa_ref, ba_ref, wx_ref, bx_ref, sp_ref,
         out_ref, ext, a_s, b_s, h_s, carry) = refs
    else:
        (ul_ref, u_ref, ur_ref, cw_ref, cb_ref, wa_ref, ba_ref, wx_ref, bx_ref, sp_ref,
         out_ref, ext, a_s, b_s, carry) = refs
        h_s = out_ref
    step = pl.program_id(0)
    chunk = (n_chunks - 1 - step) if reverse else step
    t_len, bn, c = u_ref.shape

    @pl.when(step == 0)
    def _():
        carry[...] = jnp.zeros_like(carry)

    ext[0:2] = jnp.where(chunk == 0, 0.0, ul_ref[...])
    ext[2:t_len + 2] = u_ref[...]
    ext[t_len + 2:t_len + 3] = jnp.where(chunk == n_chunks - 1, 0.0, ur_ref[...])
    uc = cb_ref[...].reshape(1, 1, c)
    for j in range(4):
        uc = uc + cw_ref[j:j + 1, :].reshape(1, 1, c) * ext[j:j + t_len]

    u2 = uc.reshape(t_len * bn, c)
    ub = u2.astype(BF16)
    half = c // 2

    def gate_tanh(w_ref, b_ref):
        z = jnp.concatenate([_dot(ub[:, :half], w_ref[0]), _dot(ub[:, half:], w_ref[1])], axis=-1)
        return jnp.tanh(z + b_ref[...])

    tr = gate_tanh(wa_ref, ba_ref)
    ti = gate_tanh(wx_ref, bx_ref)
    a = jnp.exp2(sp_ref[...] * tr + sp_ref[...])
    b = jnp.sqrt(1.0 - a * a) * ((0.5 * ti + 0.5) * u2)
    a_s[...] = a.reshape(t_len, bn, c)
    b_s[...] = b.reshape(t_len, bn, c)

    def body(k, h):
        t = (t_len - 1 - k) if reverse else k
        h = a_s[t] * h + b_s[t]
        h_s[t] = h
        return h

    carry[...] = lax.fori_loop(0, t_len, body, carry[...], unroll=8)

    if reverse:
        out_ref[...] = ((hf_ref[...] + h_s[...]) * _gelu_tanh(gate_ref[...])).astype(out_ref.dtype)


def _rglru(ug, hf, conv_w, conv_b, wa, ba, wx, bx, lam, bn, *, reverse):
    s_len = ug.shape[0]
    c = LRU_W
    ug3 = ug
    t_len = min(LRU_T, s_len)
    n_chunks = s_len // t_len

    def ck(i):
        return (n_chunks - 1 - i) if reverse else i

    half = c // 2
    nb = LRU_BLOCKS // 2

    def dense_halves(w):
        w = 0.5 * w.astype(F32).reshape(2, nb, c // LRU_BLOCKS, c // LRU_BLOCKS)
        eye = jnp.eye(nb, dtype=F32)
        return jnp.einsum('gnij,nm->gnimj', w, eye).reshape(2, half, half).astype(BF16)

    decay_scale = ((-0.5 * LRU_C * LOG2E) * jax.nn.softplus(-lam.astype(F32))).reshape(1, c)
    vec = pl.BlockSpec((1, c), lambda i: (0, 0))
    wspec = pl.BlockSpec((2, half, half), lambda i: (0, 0, 0))
    blk = pl.BlockSpec((t_len, bn, c), lambda i: (ck(i), 0, 0))
    in_specs = [
        pl.BlockSpec((2, bn, c), lambda i: (jnp.maximum(ck(i) * (t_len // 2) - 1, 0), 0, 0)),
        blk,
        pl.BlockSpec((1, bn, c), lambda i: (jnp.minimum((ck(i) + 1) * t_len, s_len - 1), 0, 0)),
    ]
    args = [ug3, ug3, ug3]
    scratch = [pltpu.VMEM((t_len + 3, bn, c), F32), pltpu.VMEM((t_len, bn, c), F32),
               pltpu.VMEM((t_len, bn, c), F32)]
    if reverse:
        in_specs += [pl.BlockSpec((t_len, bn, c), lambda i: (ck(i), 0, 1)), blk]
        args += [ug3, hf]
        scratch.append(pltpu.VMEM((t_len, bn, c), F32))
    scratch.append(pltpu.VMEM((bn, c), F32))
    in_specs += [pl.BlockSpec((4, c), lambda i: (0, 0)), vec, wspec, vec, wspec, vec, vec]
    args += [conv_w.astype(F32), conv_b.astype(F32).reshape(1, c), dense_halves(wa),
             0.5 * ba.astype(F32).reshape(1, c), dense_halves(wx), 0.5 * bx.astype(F32).reshape(1, c),
             decay_scale]
    return pl.pallas_call(
        functools.partial(_rglru_kernel, reverse=reverse, n_chunks=n_chunks),
        out_shape=jax.ShapeDtypeStruct((s_len, bn, c), BF16 if reverse else F32),
        grid=(n_chunks,),
        in_specs=in_specs,
        out_specs=blk,
        scratch_shapes=scratch,
        compiler_params=_cparams(("arbitrary",)),
        name="rglru_bw" if reverse else "rglru_fw",
    )(*args)


def _hg_matrices(c, reverse):
    n_levels = int(math.log2(c))
    t = np.arange(c)[:, None]
    r = np.arange(c)[None, :]
    mats, masks, upper = [], [], []
    if not reverse:
        mats.append(r <= t)
        mats.append(r > t)
    else:
        mats.append(r >= t)
        mats.append(r < t)
    for lev in range(n_levels):
        half = c >> (lev + 1)
        parent = 2 * half
        start = (np.arange(c) // parent) * parent
        mid = (start + half)[:, None]
        later = (np.arange(c) % parent >= half)[:, None]
        if not reverse:
            m = np.where(later, (r >= mid) & (r <= t), (r > t) & (r < mid))
            is_q = later
        else:
            m = np.where(later, (r >= mid) & (r < t), (r >= t) & (r < mid))
            is_q = ~later
        mats.append(m)
        same_parent = (start[:, None] == start[None, :])
        masks.append(same_parent & is_q & (~is_q).T)
        upper.append(np.broadcast_to(is_q, (c, HG_DK)))
    mat = np.concatenate(mats, axis=0).astype(np.float32)
    return (np.concatenate([mat, mat], axis=1), np.stack(masks).astype(np.float32),
            np.stack(upper).astype(np.float32))


def _hgrn_kernel(*refs, reverse, n_steps, n_levels):
    if reverse:
        (q_ref, f_ref, v_ref, go_ref, of_ref, lb_ref, gn_ref, mat_ref, mask_ref, isq_ref, out_ref,
         state, x_scr, oi_scr, aux_scr, dec_scr, kk_scr) = refs
    else:
        (q_ref, f_ref, v_ref, lb_ref, mat_ref, mask_ref, isq_ref, out_ref,
         state, x_scr, oi_scr, aux_scr, dec_scr, kk_scr) = refs
    step = pl.program_id(1)
    c = HG_CHUNK
    t_len = q_ref.shape[0]
    n_chunks = t_len // c

    @pl.when(step == 0)
    def _():
        state[...] = jnp.zeros_like(state)

    def chunk_phases(ci, k):
        ch = (n_chunks - 1 - ci) if reverse else ci
        rows = slice(ch * c, (ch + 1) * c)
        edge = (c - 1) if not reverse else 0
        kk_s, dec_s, x_s, oi_s, aux_s = kk_scr.at[k], dec_scr.at[k], x_scr.at[k], oi_scr.at[k], aux_scr.at[k]
        for hp in range(HG_HEADS // 2):
            cols2 = slice(2 * hp * HG_DK, 2 * (hp + 1) * HG_DK)
            fl = f_ref[rows, cols2]
            e = jnp.exp(-jnp.abs(fl))
            r = 1.0 / (1.0 + e)
            er = e * r
            pos = fl > 0.0
            lb = lb_ref[0:1, cols2]
            oml = lb_ref[1:2, cols2]
            g2 = jnp.maximum(jnp.log2(lb + oml * jnp.where(pos, r, er)), F32_MIN_EXPONENT)
            kk_s[hp] = oml * jnp.where(pos, er, r)
            g_hi = g2.astype(BF16)
            g_lo = (g2 - g_hi.astype(F32)).astype(BF16)
            dec_s[hp] = jnp.exp2(_dot(mat_ref[...], jnp.concatenate([g_hi, g_lo], axis=0)))
        yield

        for hd in range(HG_HEADS):
            cols = slice(hd * HG_DK, (hd + 1) * HG_DK)
            pcols = slice((hd % 2) * HG_DK, (hd % 2 + 1) * HG_DK)
            kk = kk_s[hd // 2, :, pcols]
            dec = dec_s.at[hd // 2]
            qv = q_ref[rows, cols].astype(F32)
            qq = qv * jax.nn.sigmoid(qv)
            qb, kb = qq.astype(BF16), kk.astype(BF16)
            d_in = dec[0:c, pcols]
            x_s[hd, n_levels] = (qq * d_in).astype(BF16)
            x_s[hd, n_levels + 1] = (kk * dec[c:2 * c, pcols]).astype(BF16)
            aux_s[hd, 0] = jnp.broadcast_to(d_in[edge:edge + 1, :], (c, HG_DK))
            aux_s[hd, 1] = jnp.broadcast_to(jnp.sum(qq * kk, axis=-1, keepdims=True), (c, HG_DK))
            for lev in range(n_levels):
                half = c >> (lev + 1)
                if half >= SUBLANES_BF16:
                    first_is_q = bool(reverse)
                    base = jnp.concatenate(
                        [(qb if (blk % 2 == 1) != first_is_q else kb)[blk * half:(blk + 1) * half]
                         for blk in range(c // half)], axis=0)
                else:
                    base = jnp.where(isq_ref[lev] > 0.5, qb, kb)
                x_s[hd, lev] = base * dec[(2 + lev) * c:(3 + lev) * c, pcols].astype(BF16)
        yield

        for hd in range(HG_HEADS):
            cols = slice(hd * HG_DK, (hd + 1) * HG_DK)
            att = None
            for lev in range(n_levels):
                xl = x_s[hd, lev]
                term = mask_ref[lev] * _nt_dot(xl, xl)
                att = term if att is None else att + term
            st = state[hd]
            oi_s[hd] = _nt_dot(x_s[hd, n_levels], st.astype(BF16))
            vt = v_ref[rows, cols].astype(F32).T.astype(BF16)
            state[hd] = st * aux_s[hd, 0] + _dot(vt, x_s[hd, n_levels + 1])
            x_s[hd, 0] = att.astype(BF16)
        yield

        for hd in range(HG_HEADS):
            cols = slice(hd * HG_DK, (hd + 1) * HG_DK)
            vb = v_ref[rows, cols]
            o = oi_s[hd] + _dot(x_s[hd, 0], vb) + aux_s[hd, 1] * vb.astype(F32)

            if reverse:
                o = o + of_ref[rows, cols]
                o = o * lax.rsqrt(jnp.mean(o * o, axis=-1, keepdims=True) + EPS) * gn_ref[...]
                gv = go_ref[rows, cols].astype(F32)
                out_ref[rows, cols] = (o * (gv * jax.nn.sigmoid(gv))).astype(out_ref.dtype)
            else:
                out_ref[rows, cols] = o
        yield

    for g0 in range(0, n_chunks, HG_GROUP):
        group = [chunk_phases(ci, k) for k, ci in enumerate(range(g0, min(g0 + HG_GROUP, n_chunks)))]
        for _ in range(4):
            for phases in group:
                next(phases)


def _hgrn(hg, o_fw, lb, gnorm_g, *, reverse):
    bn, s_len, _ = hg["q"].shape
    w = HG_HEADS * HG_DK
    t_len = min(HG_T, s_len)
    n_steps = s_len // t_len
    mats, masks, isq = _hg_matrices(HG_CHUNK, reverse)
    n_levels = masks.shape[0]
    lbf = lb.astype(F32)
    lb_tab = jnp.stack([lbf, 1.0 - lbf])

    blk = pl.BlockSpec((None, t_len, w), lambda b, i: (b, (n_steps - 1 - i) if reverse else i, 0))

    def const(shape):
        return pl.BlockSpec(shape, lambda b, i: (0,) * len(shape))

    in_specs = [blk, blk, blk]
    args = [hg["q"], hg["f_bw" if reverse else "f_fw"], hg["i_in"]]
    if reverse:
        in_specs += [blk, blk]
        args += [hg["g_out"], o_fw]
    in_specs.append(const((2, w)))
    args.append(lb_tab)
    if reverse:
        in_specs.append(const((1, HG_DK)))
        args.append(gnorm_g.astype(F32).reshape(1, HG_DK))
    in_specs += [const(mats.shape), const(masks.shape), const(isq.shape)]
    args += [jnp.asarray(mats, BF16), jnp.asarray(masks, F32), jnp.asarray(isq, BF16)]
    return pl.pallas_call(
        functools.partial(_hgrn_kernel, reverse=reverse, n_steps=n_steps, n_levels=n_levels),
        out_shape=jax.ShapeDtypeStruct((bn, s_len, w), BF16 if reverse else F32),
        grid=(bn, n_steps),
        in_specs=in_specs,
        out_specs=blk,
        scratch_shapes=[pltpu.VMEM((HG_HEADS, HG_DK, HG_DK), F32),
                        pltpu.VMEM((HG_GROUP, HG_HEADS, n_levels + 2, HG_CHUNK, HG_DK), BF16),
                        pltpu.VMEM((HG_GROUP, HG_HEADS, HG_CHUNK, HG_DK), F32),
                        pltpu.VMEM((HG_GROUP, HG_HEADS, 2, HG_CHUNK, HG_DK), F32),
                        pltpu.VMEM((HG_GROUP, HG_HEADS // 2, (n_levels + 2) * HG_CHUNK, 2 * HG_DK), F32),
                        pltpu.VMEM((HG_GROUP, HG_HEADS // 2, HG_CHUNK, 2 * HG_DK), F32)],
        compiler_params=_cparams(("parallel", "arbitrary")),
        name="hgrn_bw" if reverse else "hgrn_fw",
    )(*args)


def _even_layer(x, g, w_in, w_out, rpb, tail):
    qkv, view4, view16 = _ev_proj(x, g, w_in.astype(BF16))
    outs, stats = zip(*[_dilated_branch(view, dil) for dil, view in zip(DILATIONS, (qkv, view4, view16))])
    yb = _neighborhood(qkv, rpb)
    return _ev_tail(x, outs, stats, yb, w_out.astype(BF16), tail)


def _odd_layer(x, g, w_in, w_out, conv_w, conv_b, wa, ba, wx, bx, lam, lb_f, lb_b, gnorm_g, tail):
    bn, s_len, _ = x.shape
    ug, hg = _od_proj(x, g, w_in.astype(BF16))
    h_fw = _rglru(ug, None, conv_w, conv_b, wa[0], ba[0], wx[0], bx[0], lam[0], bn, reverse=False)
    yc = _rglru(ug, h_fw, conv_w, conv_b, wa[1], ba[1], wx[1], bx[1], lam[1], bn, reverse=True)
    o_fw = _hgrn(hg, None, lb_f, gnorm_g, reverse=False)
    yd = _hgrn(hg, o_fw, lb_b, gnorm_g, reverse=True)
    return _od_tail(x, yc.reshape(s_len, bn * LRU_W), yd, w_out.astype(BF16), tail)


def kernel(x, mem, norm_mix_g, norm_xa_g, norm_mem_g, norm_mlp_g, final_norm_g, ev_w_in, ev_w_out, na_rpb,
           od_w_in, od_w_out, conv_w, conv_b, lru_wa, lru_ba, lru_wx, lru_bx, lru_lambda, hgrn_lb_logits,
           hgrn_norm_g, xa_wq, xa_wkv, xa_wo, mlp_w1, mlp_w2):
    depth = norm_mix_g.shape[0]
    p_lb = jax.nn.softmax(hgrn_lb_logits.astype(F32), axis=0)
    lower_bounds = jnp.cumsum(p_lb, axis=0) - p_lb[0:1]
    for layer in range(depth):
        tail = dict(g_xa=norm_xa_g[layer], wq=xa_wq[layer].astype(BF16),
                    kv=_mem_kv(mem, norm_mem_g[layer], xa_wkv[layer].astype(BF16)),
                    wo=xa_wo[layer].astype(BF16), g_mlp=norm_mlp_g[layer],
                    w1=mlp_w1[layer].astype(BF16), w2=mlp_w2[layer].astype(BF16),
                    g_final=final_norm_g if layer == depth - 1 else None)
        if layer % 2 == 0:
            e = layer // 2
            x = _even_layer(x, norm_mix_g[layer], ev_w_in[e], ev_w_out[e], na_rpb[e], tail)
        else:
            o = layer // 2
            x = _odd_layer(x, norm_mix_g[layer], od_w_in[o], od_w_out[o], conv_w[o], conv_b[o],
                           lru_wa[o], lru_ba[o], lru_wx[o], lru_bx[o], lru_lambda[o],
                           lower_bounds[layer, 0], lower_bounds[layer, 1], hgrn_norm_g[o], tail)
        x = _mlp(x, tail)
    return x
```

```python
import functools
import math

import jax
import jax.numpy as jnp
import numpy as np
from jax import lax
from jax.experimental import pallas as pl
from jax.experimental.pallas import tpu as pltpu

F32 = jnp.float32
BF16 = jnp.bfloat16

D_MODEL = 1024
HEAD_DIM = 64
ROT_DIM = 16
ROPE_THETA = 500000.0
N_HEADS_A = 8
N_HEADS_B = 8
GROUP_W = N_HEADS_A * HEAD_DIM
DILATIONS = (1, 4, 16)
WIN_HALF = 64
GRID_W = 64
NA_ROWS = 8
NA_COLS = 16
LRU_W = 512
LRU_BLOCKS = 8
LRU_C = 8.0
HG_HEADS = 4
HG_DK = 128
XA_HEADS = 4
XA_DH = 256
D_FF = 4096
EPS = 1e-6
LOG2E = math.log2(math.e)
F32_MIN_EXPONENT = -126.0

LANES = 128
SUBLANES_BF16 = 16
V7X_VMEM_LIMIT_BYTES = 56 * 1024 * 1024

ROW_TILE_WIDE = 1024
ATT_Q_BLOCK = 1024
ATT_SUB = 128
NA_ROW_GROUP = 16
LRU_T = 256
HG_CHUNK = 128
HG_T = 1024
HG_GROUP = 2


def _cparams(sem):
    return pltpu.CompilerParams(dimension_semantics=sem, vmem_limit_bytes=V7X_VMEM_LIMIT_BYTES)


def _resident(shape, index_map):
    return pl.BlockSpec(shape, index_map, pipeline_mode=pl.Buffered(1))


def _rms(x, g):
    return x * lax.rsqrt(jnp.mean(x * x, axis=-1, keepdims=True) + EPS) * g


def _nt_dot(a, b):
    return lax.dot_general(a, b, (((1,), (1,)), ((), ())), preferred_element_type=F32)


def _dot(a, b):
    return jnp.dot(a, b, preferred_element_type=F32)


def _head_selectors():
    lane = lax.broadcasted_iota(jnp.int32, (1, LANES), 1)
    low = (lane < HEAD_DIM).astype(F32)
    return low.astype(BF16), (1.0 - low).astype(BF16)


def _ev_proj_kernel(x_ref, g_ref, w_ref, rc_ref, rs1_ref, rs2_ref, o_ref, o4_ref, o16_ref, slab):
    h = _rms(x_ref[...], g_ref[...]).astype(BF16)
    scale = HEAD_DIM ** -0.5 * LOG2E
    tm = x_ref.shape[0]
    n_slabs = GROUP_W // LANES
    for c in range(6):
        acc = _dot(h, w_ref[:, c * GROUP_W:(c + 1) * GROUP_W])
        if c in (0, 1):
            rc, rs1, rs2 = rc_ref[...], rs1_ref[...], rs2_ref[...]
            parts = []
            for j in range(n_slabs):
                t = acc[:, j * LANES:(j + 1) * LANES]
                t = t * rc + pltpu.roll(t, LANES - ROT_DIM // 2, 1) * rs1 + pltpu.roll(t, ROT_DIM // 2, 1) * rs2
                parts.append(t)
            acc = jnp.concatenate(parts, axis=-1)
        if c in (0, 3):
            acc = acc * scale
        o_ref[:, c * GROUP_W:(c + 1) * GROUP_W] = acc.astype(BF16)
        if c < 3:
            for j in range(n_slabs):
                slab[j] = acc[:, j * LANES:(j + 1) * LANES]
            for dil, dst in ((DILATIONS[1], o4_ref), (DILATIONS[2], o16_ref)):
                for r in range(dil):
                    for j in range(n_slabs):
                        col = r * 3 * GROUP_W + c * GROUP_W + j * LANES
                        dst[:, col:col + LANES] = slab[j, pl.ds(r, tm // dil, stride=dil), :].astype(BF16)


def _rope_tables(s_len):
    half = ROT_DIM // 2
    inv = jnp.asarray(ROPE_THETA ** (-np.arange(half) * 2.0 / ROT_DIM), F32)
    ang = jnp.arange(s_len, dtype=F32)[:, None] * inv[None, :]
    cos, sin = jnp.cos(ang), jnp.sin(ang)
    ones = jnp.ones((s_len, HEAD_DIM - ROT_DIM), F32)
    zeros = jnp.zeros((s_len, HEAD_DIM - ROT_DIM), F32)
    zh = jnp.zeros((s_len, half), F32)
    rc = jnp.concatenate([cos, cos, ones], axis=-1)
    rs1 = jnp.concatenate([-sin, zh, zeros], axis=-1)
    rs2 = jnp.concatenate([zh, sin, zeros], axis=-1)
    rep = LANES // HEAD_DIM
    return jnp.tile(rc, (1, rep)), jnp.tile(rs1, (1, rep)), jnp.tile(rs2, (1, rep))


def _ev_proj(x, g, w_bf16):
    bn, s_len, d = x.shape
    tm = min(ROW_TILE_WIDE, s_len)
    n_out = w_bf16.shape[1]
    rc, rs1, rs2 = _rope_tables(s_len)
    tbl_spec = pl.BlockSpec((tm, LANES), lambda b, i: (i, 0))
    wa = 3 * GROUP_W
    d4, d16 = DILATIONS[1], DILATIONS[2]
    return pl.pallas_call(
        _ev_proj_kernel,
        out_shape=(jax.ShapeDtypeStruct((bn, s_len, n_out), BF16),
                   jax.ShapeDtypeStruct((bn, s_len // d4, d4 * wa), BF16),
                   jax.ShapeDtypeStruct((bn, s_len // d16, d16 * wa), BF16)),
        grid=(bn, s_len // tm),
        in_specs=[
            pl.BlockSpec((None, tm, d), lambda b, i: (b, i, 0)),
            pl.BlockSpec((1, d), lambda b, i: (0, 0)),
            _resident((d, n_out), lambda b, i: (0, 0)),
            tbl_spec, tbl_spec, tbl_spec,
        ],
        out_specs=(pl.BlockSpec((None, tm, n_out), lambda b, i: (b, i, 0)),
                   pl.BlockSpec((None, tm // d4, d4 * wa), lambda b, i: (b, i, 0)),
                   pl.BlockSpec((None, tm // d16, d16 * wa), lambda b, i: (b, i, 0))),
        scratch_shapes=[pltpu.VMEM((GROUP_W // LANES, tm, LANES), F32)],
        compiler_params=_cparams(("parallel", "parallel")),
        name="ev_proj",
    )(x, g.reshape(1, d), w_bf16, rc, rs1, rs2)


def _dilated_kernel(q_ref, kl_ref, kc_ref, kr_ref, vl_ref, vc_ref, vr_ref, o_ref, stat_ref,
                    kext, vext, s_scr, p_scr, *, lq, l_total):
    i = pl.program_id(2)
    kext[0:WIN_HALF, :] = kl_ref[...]
    kext[WIN_HALF:WIN_HALF + lq, :] = kc_ref[...]
    kext[WIN_HALF + lq:, :] = kr_ref[...]
    vext[0:WIN_HALF, :] = vl_ref[...]
    vext[WIN_HALF:WIN_HALF + lq, :] = vc_ref[...]
    vext[WIN_HALF + lq:, :] = vr_ref[...]

    wk = ATT_SUB + 2 * WIN_HALF
    qi = lax.broadcasted_iota(jnp.int32, (ATT_SUB, wk), 0)
    ci = lax.broadcasted_iota(jnp.int32, (ATT_SUB, wk), 1)
    band_bias = jnp.where((ci - qi >= 0) & (ci - qi <= 2 * WIN_HALF), 0.0, -jnp.inf).astype(F32)
    crow = lax.broadcasted_iota(jnp.int32, (1, wk), 1)
    lane = lax.broadcasted_iota(jnp.int32, (ATT_SUB, LANES), 1)
    low_half = lane < HEAD_DIM
    sel_lo, sel_hi = _head_selectors()
    n_sub = lq // ATT_SUB
    n_hp = GROUP_W // LANES

    def half_phases(js):
        for j in js:
            base = i * lq + j * ATT_SUB - WIN_HALF
            in_seq = (crow + base >= 0) & (crow + base < l_total)
            bias = band_bias + jnp.where(in_seq, 0.0, -jnp.inf).astype(F32)
            bias2 = jnp.concatenate([bias, bias], axis=0)
            rows = slice(j * ATT_SUB, (j + 1) * ATT_SUB)
            wrows = slice(j * ATT_SUB, j * ATT_SUB + wk)
            for hp in range(n_hp):
                cols = slice(hp * LANES, (hp + 1) * LANES)
                qp = q_ref[rows, cols]
                q2 = jnp.concatenate([qp * sel_lo, qp * sel_hi], axis=0)
                s_scr[j * n_hp + hp] = _nt_dot(q2, kext[wrows, cols]) + bias2
        yield
        for j in js:
            stat = jnp.ones((ATT_SUB, LANES), F32)
            for hp in range(n_hp):
                t = j * n_hp + hp
                s = s_scr[t]
                m = jnp.max(s, axis=-1, keepdims=True)
                p = jnp.exp2(s - m)
                den = jnp.sum(p, axis=-1, keepdims=True)
                p_scr[t] = p.astype(BF16)
                stat = jnp.where(lane == 2 * hp, m[0:ATT_SUB], jnp.where(lane == 2 * hp + 1, m[ATT_SUB:], stat))
                stat = jnp.where(lane == N_HEADS_A + 2 * hp, den[0:ATT_SUB],
                                 jnp.where(lane == N_HEADS_A + 2 * hp + 1, den[ATT_SUB:], stat))
            stat_ref[j * ATT_SUB:(j + 1) * ATT_SUB, :] = stat
        yield
        for j in js:
            rows = slice(j * ATT_SUB, (j + 1) * ATT_SUB)
            wrows = slice(j * ATT_SUB, j * ATT_SUB + wk)
            for hp in range(n_hp):
                cols = slice(hp * LANES, (hp + 1) * LANES)
                o = _dot(p_scr[j * n_hp + hp], vext[wrows, cols])
                o_ref[rows, cols] = jnp.where(low_half, o[0:ATT_SUB], o[ATT_SUB:]).astype(BF16)
        yield

    first, second = half_phases(range(0, n_sub // 2)), half_phases(range(n_sub // 2, n_sub))
    for g in (first, first, second, first, second, second):
        next(g)


def _dilated_branch(view, dil):
    bn, l_total, width = view.shape
    n_groups = width // (dil * GROUP_W)
    lq = min(ATT_Q_BLOCK, l_total)
    nblk = l_total // lq
    per = lq // WIN_HALF
    n_halo = l_total // WIN_HALF
    n_units = (lq // ATT_SUB) * (GROUP_W // LANES)

    def cur(g):
        return pl.BlockSpec((None, lq, GROUP_W), lambda b, r, i: (b, i, r * n_groups + g))

    def left(g):
        return pl.BlockSpec((None, WIN_HALF, GROUP_W),
                            lambda b, r, i: (b, jnp.maximum(i * per - 1, 0), r * n_groups + g))

    def right(g):
        return pl.BlockSpec((None, WIN_HALF, GROUP_W),
                            lambda b, r, i: (b, jnp.minimum((i + 1) * per, n_halo - 1), r * n_groups + g))

    stat_shape = jax.ShapeDtypeStruct((bn, l_total, dil * LANES), F32)
    stat_spec = pl.BlockSpec((None, lq, LANES), lambda b, r, i: (b, i, r))
    return pl.pallas_call(
        functools.partial(_dilated_kernel, lq=lq, l_total=l_total),
        out_shape=(jax.ShapeDtypeStruct((bn, l_total, dil * GROUP_W), BF16), stat_shape),
        grid=(bn, dil, nblk),
        in_specs=[cur(0), left(1), cur(1), right(1), left(2), cur(2), right(2)],
        out_specs=(pl.BlockSpec((None, lq, GROUP_W), lambda b, r, i: (b, i, r)), stat_spec),
        scratch_shapes=[pltpu.VMEM((lq + 2 * WIN_HALF, GROUP_W), BF16),
                        pltpu.VMEM((lq + 2 * WIN_HALF, GROUP_W), BF16),
                        pltpu.VMEM((n_units, 2 * ATT_SUB, ATT_SUB + 2 * WIN_HALF), F32),
                        pltpu.VMEM((n_units, 2 * ATT_SUB, ATT_SUB + 2 * WIN_HALF), BF16)],
        compiler_params=_cparams(("parallel", "parallel", "parallel")),
        name=f"dilated_attn_d{dil}",
    )(view, view, view, view, view, view, view)


NA_QT = NA_ROW_GROUP * GRID_W
NA_KT = 2 * NA_QT
NA_KW = NA_ROWS * GRID_W


def _na_tables(rpb):
    rpb = rpb.astype(F32)
    nh = rpb.shape[0]
    ext = GRID_W - NA_COLS
    padded = jnp.concatenate([jnp.repeat(rpb[..., :1], ext, axis=-1), rpb,
                              jnp.repeat(rpb[..., -1:], ext, axis=-1)], axis=-1)
    skew = jnp.tile(padded, (1, 1, GRID_W + 1))[..., :2 * GRID_W * GRID_W]
    t1 = skew.reshape(nh, -1, GRID_W, 2 * GRID_W)[..., ::-1, :GRID_W]
    qc = np.arange(GRID_W)[:, None]
    kc = np.arange(GRID_W)[None, :]
    c0 = np.clip(qc - NA_COLS // 2, 0, GRID_W - NA_COLS)
    t1 = jnp.where((kc >= c0) & (kc < c0 + NA_COLS), t1 * LOG2E, -jnp.inf)
    per_delta = [jnp.transpose(t1[:, NA_ROWS - 1 - delta:2 * NA_ROWS - 1 - delta], (0, 2, 1, 3))
                 .reshape(nh, GRID_W, NA_KW) for delta in range(NA_ROWS)]
    return jnp.stack(per_delta).reshape(NA_ROWS, nh // 2, 2 * GRID_W, NA_KW)


def _na_kernel(q_ref, k_ref, v_ref, tb_ref, o_ref, s_scr, p_scr, inv_scr, *, rows):
    i0 = pl.program_id(1) * NA_ROW_GROUP
    w0 = jnp.clip(i0 - NA_ROWS // 2, 0, rows - 2 * NA_ROW_GROUP)
    lane = lax.broadcasted_iota(jnp.int32, (GRID_W, LANES), 1)
    low_half = lane < HEAD_DIM
    sel_lo, sel_hi = _head_selectors()
    n_hp = GROUP_W // LANES

    def key_rows(a):
        r0 = jnp.clip(i0 + a - NA_ROWS // 2, 0, rows - NA_ROWS)
        return pl.ds(pl.multiple_of((r0 - w0) * GRID_W, GRID_W), NA_KW), i0 + a - r0

    for a in range(NA_ROW_GROUP):
        krows, delta = key_rows(a)
        qrows = slice(a * GRID_W, (a + 1) * GRID_W)
        for hp in range(n_hp):
            cols = slice(hp * LANES, (hp + 1) * LANES)
            qp = q_ref[qrows, cols]
            q2 = jnp.concatenate([qp * sel_lo, qp * sel_hi], axis=0)
            s_scr[a * n_hp + hp] = _nt_dot(q2, k_ref[0, krows, cols]) + tb_ref[delta, hp]
    for t in range(NA_ROW_GROUP * n_hp):
        s = s_scr[t]
        p = jnp.exp2(s - jnp.max(s, axis=-1, keepdims=True))
        inv_scr[t] = jnp.broadcast_to(1.0 / jnp.sum(p, axis=-1, keepdims=True), (2 * GRID_W, LANES))
        p_scr[t] = p.astype(BF16)
    for a in range(NA_ROW_GROUP):
        krows, _ = key_rows(a)
        qrows = slice(a * GRID_W, (a + 1) * GRID_W)
        for hp in range(n_hp):
            cols = slice(hp * LANES, (hp + 1) * LANES)
            t = a * n_hp + hp
            o = _dot(p_scr[t], v_ref[0, krows, cols]) * inv_scr[t]
            o_ref[qrows, cols] = jnp.where(low_half, o[0:GRID_W], o[GRID_W:]).astype(BF16)


def _neighborhood(qkv, rpb):
    bn, s_len, _ = qkv.shape
    rows = s_len // GRID_W
    n_groups = rows // NA_ROW_GROUP
    tables = _na_tables(rpb)

    def window(g):
        return pl.BlockSpec(
            (pl.Element(1), pl.Element(NA_KT), pl.Element(GROUP_W)),
            lambda b, i: (b, jnp.clip(i * NA_ROW_GROUP - NA_ROWS // 2, 0, rows - 2 * NA_ROW_GROUP) * GRID_W,
                          g * GROUP_W))

    return pl.pallas_call(
        functools.partial(_na_kernel, rows=rows),
        out_shape=jax.ShapeDtypeStruct((bn, s_len, GROUP_W), BF16),
        grid=(bn, n_groups),
        in_specs=[pl.BlockSpec((None, NA_QT, GROUP_W), lambda b, i: (b, i, 3)), window(4), window(5),
                  pl.BlockSpec(tables.shape, lambda b, i: (0, 0, 0, 0))],
        out_specs=pl.BlockSpec((None, NA_QT, GROUP_W), lambda b, i: (b, i, 0)),
        scratch_shapes=[pltpu.VMEM((NA_ROW_GROUP * GROUP_W // LANES, 2 * GRID_W, NA_KW), F32),
                        pltpu.VMEM((NA_ROW_GROUP * GROUP_W // LANES, 2 * GRID_W, NA_KW), BF16),
                        pltpu.VMEM((NA_ROW_GROUP * GROUP_W // LANES, 2 * GRID_W, LANES), F32)],
        compiler_params=_cparams(("parallel", "parallel")),
        name="neighborhood_attn",
    )(qkv, qkv, qkv, tables)


def _xattn_block(x, g_ref, wq_ref, kv_ref, wo_ref):
    q = (_dot(_rms(x, g_ref[...]).astype(BF16), wq_ref[...]) * (XA_DH ** -0.5 * LOG2E)).astype(BF16)
    heads = []
    for hd in range(XA_HEADS):
        cols = slice(hd * XA_DH, (hd + 1) * XA_DH)
        s = _nt_dot(q[:, cols], kv_ref[:, cols])
        m = jnp.max(s, axis=-1, keepdims=True)
        p = jnp.exp2(s - m)
        den = jnp.sum(p, axis=-1, keepdims=True)
        vcols = slice(D_MODEL + hd * XA_DH, D_MODEL + (hd + 1) * XA_DH)
        heads.append((_dot(p.astype(BF16), kv_ref[:, vcols]) * (1.0 / den)).astype(BF16))
    return x + _dot(jnp.concatenate(heads, axis=-1), wo_ref[...])


def _mlp_block(x, g_ref, w1_ref, w2_ref, gf_ref, final_norm):
    h = _rms(x, g_ref[...]).astype(BF16)
    acc = x
    for c in range(D_FF // D_MODEL):
        cols = slice(c * D_MODEL, (c + 1) * D_MODEL)
        a = jnp.maximum(_dot(h, w1_ref[:, cols]), 0.0)
        acc = acc + _dot((a * a).astype(BF16), w2_ref[cols, :])
    return _rms(acc, gf_ref[...]) if final_norm else acc


def _xattn_operands(tail, d):
    m_len = tail["kv"].shape[1]
    specs = [pl.BlockSpec((1, d), lambda b, i: (0, 0)), _resident((d, d), lambda b, i: (0, 0)),
             pl.BlockSpec((None, m_len, 2 * d), lambda b, i: (b, 0, 0)),
             _resident((d, d), lambda b, i: (0, 0))]
    return specs, [tail["g_xa"].reshape(1, d), tail["wq"], tail["kv"], tail["wo"]]


def _mlp_kernel(x_ref, g_ref, w1_ref, w2_ref, gf_ref, out_ref, *, final_norm):
    out_ref[...] = _mlp_block(x_ref[...], g_ref, w1_ref, w2_ref, gf_ref, final_norm)


def _mlp(x, tail):
    bn, s_len, d = x.shape
    tm = min(ROW_TILE_WIDE, s_len)
    final_norm = tail["g_final"] is not None
    gf = tail["g_final"] if final_norm else tail["g_mlp"]
    x_spec = pl.BlockSpec((None, tm, d), lambda b, i: (b, i, 0))
    vec = pl.BlockSpec((1, d), lambda b, i: (0, 0))
    return pl.pallas_call(
        functools.partial(_mlp_kernel, final_norm=final_norm),
        out_shape=jax.ShapeDtypeStruct(x.shape, F32),
        grid=(bn, s_len // tm),
        in_specs=[x_spec, vec, _resident((d, D_FF), lambda b, i: (0, 0)),
                  _resident((D_FF, d), lambda b, i: (0, 0)), vec],
        out_specs=x_spec,
        compiler_params=_cparams(("parallel", "parallel")),
        name="mlp_final" if final_norm else "mlp",
    )(x, tail["g_mlp"].reshape(1, d), tail["w1"], tail["w2"], gf.reshape(1, d))


def _ev_tail_kernel(o1, o2v, o3v, s1, s2v, s3v, yb_ref, ex_ref, w_ref, x_ref,
                    gxa_ref, wq_ref, kv_ref, wo_ref, out_ref, so2, so3, st2, st3):
    tm = x_ref.shape[0]
    n_slabs = GROUP_W // LANES
    for dil, ov, sv, so, st in ((DILATIONS[1], o2v, s2v, so2, st2), (DILATIONS[2], o3v, s3v, so3, st3)):
        for r in range(dil):
            dst = pl.ds(r, tm // dil, stride=dil)
            for j in range(n_slabs):
                col = r * GROUP_W + j * LANES
                so[j, dst, :] = ov[:, col:col + LANES].astype(F32)
            st[dst, :] = sv[:, r * LANES:(r + 1) * LANES]
    o2 = jnp.concatenate([so2[j] for j in range(n_slabs)], axis=-1)
    o3 = jnp.concatenate([so3[j] for j in range(n_slabs)], axis=-1)
    ma, mb, mc = s1[...], st2[...], st3[...]
    da, db, dc = [pltpu.roll(t, LANES - N_HEADS_A, 1) for t in (ma, mb, mc)]
    m = jnp.maximum(jnp.maximum(ma, mb), mc)
    ea, eb, ec = jnp.exp2(ma - m), jnp.exp2(mb - m), jnp.exp2(mc - m)
    inv = 1.0 / (ea * da + eb * db + ec * dc)
    lane = lax.broadcasted_iota(jnp.int32, (tm, LANES), 1)
    wcat = jnp.where(lane < N_HEADS_A, ea * inv,
                     jnp.where(lane < 2 * N_HEADS_A, pltpu.roll(eb * inv, N_HEADS_A, 1),
                               pltpu.roll(ec * inv, 2 * N_HEADS_A, 1)))
    hi = wcat.astype(BF16)
    lo = (wcat - hi.astype(F32)).astype(BF16)
    wide = _dot(jnp.concatenate([hi, lo], axis=1), ex_ref[...])
    ya = (wide[:, 0:GROUP_W] * o1[...].astype(F32) + wide[:, GROUP_W:2 * GROUP_W] * o2
          + wide[:, 2 * GROUP_W:] * o3)
    y = _dot(ya.astype(BF16), w_ref[0:GROUP_W, :]) + _dot(yb_ref[...], w_ref[GROUP_W:, :])
    out_ref[...] = _xattn_block(x_ref[...] + y, gxa_ref, wq_ref, kv_ref, wo_ref)


def _ev_tail(x, outs, stats, yb, w_bf16, tail):
    bn, s_len, d = x.shape
    tm = min(ROW_TILE_WIDE, s_len)
    n_br = len(DILATIONS)
    ex = np.zeros((LANES, n_br * GROUP_W), np.float32)
    for i in range(n_br):
        for h in range(N_HEADS_A):
            ex[i * N_HEADS_A + h, i * GROUP_W + h * HEAD_DIM:i * GROUP_W + (h + 1) * HEAD_DIM] = 1.0
    ex = np.concatenate([ex, ex], axis=0)

    def o_spec(dil):
        return pl.BlockSpec((None, tm // dil, dil * GROUP_W), lambda b, i: (b, i, 0))

    def l_spec(dil):
        return pl.BlockSpec((None, tm // dil, dil * LANES), lambda b, i: (b, i, 0))

    x_spec = pl.BlockSpec((None, tm, d), lambda b, i: (b, i, 0))
    n_slabs = GROUP_W // LANES
    tail_specs, tail_args = _xattn_operands(tail, d)
    return pl.pallas_call(
        _ev_tail_kernel,
        out_shape=jax.ShapeDtypeStruct(x.shape, F32),
        grid=(bn, s_len // tm),
        in_specs=[o_spec(d_) for d_ in DILATIONS] + [l_spec(d_) for d_ in DILATIONS]
        + [o_spec(1),
           _resident(ex.shape, lambda b, i: (0, 0)),
           _resident((2 * GROUP_W, d), lambda b, i: (0, 0)),
           x_spec] + tail_specs,
        out_specs=x_spec,
        scratch_shapes=[pltpu.VMEM((n_slabs, tm, LANES), F32), pltpu.VMEM((n_slabs, tm, LANES), F32),
                        pltpu.VMEM((tm, LANES), F32), pltpu.VMEM((tm, LANES), F32)],
        compiler_params=_cparams(("parallel", "parallel")),
        name="ev_tail",
    )(*outs, *stats, yb, jnp.asarray(ex, BF16), w_bf16, x, *tail_args)


def _od_tail_kernel(yc_ref, yd_ref, w_ref, x_ref, gxa_ref, wq_ref, kv_ref, wo_ref, out_ref):
    half = yc_ref.shape[-1]
    x0 = x_ref[...] + _dot(yc_ref[...], w_ref[0:half, :]) + _dot(yd_ref[...], w_ref[half:, :])
    out_ref[...] = _xattn_block(x0, gxa_ref, wq_ref, kv_ref, wo_ref)


def _od_tail(x, yc_tm, yd, w_bf16, tail):
    bn, s_len, d = x.shape
    tm = min(ROW_TILE_WIDE, s_len)
    c = yd.shape[-1]
    x_spec = pl.BlockSpec((None, tm, d), lambda b, i: (b, i, 0))
    tail_specs, tail_args = _xattn_operands(tail, d)
    return pl.pallas_call(
        _od_tail_kernel,
        out_shape=jax.ShapeDtypeStruct(x.shape, F32),
        grid=(bn, s_len // tm),
        in_specs=[pl.BlockSpec((tm, c), lambda b, i: (i, b)),
                  pl.BlockSpec((None, tm, c), lambda b, i: (b, i, 0)),
                  _resident((2 * c, d), lambda b, i: (0, 0)),
                  x_spec] + tail_specs,
        out_specs=x_spec,
        compiler_params=_cparams(("parallel", "parallel")),
        name="od_tail",
    )(yc_tm, yd, w_bf16, x, *tail_args)


def _norm_matmul_kernel(x_ref, g_ref, w_ref, o_ref):
    o_ref[...] = _dot(_rms(x_ref[...], g_ref[...]).astype(BF16), w_ref[...]).astype(o_ref.dtype)


def _mem_kv(mem, g, w_bf16):
    bn, m_len, d = mem.shape
    n_out = w_bf16.shape[1]
    return pl.pallas_call(
        _norm_matmul_kernel,
        out_shape=jax.ShapeDtypeStruct((bn, m_len, n_out), BF16),
        grid=(bn,),
        in_specs=[pl.BlockSpec((None, m_len, d), lambda b: (b, 0, 0)),
                  pl.BlockSpec((1, d), lambda b: (0, 0)),
                  pl.BlockSpec((d, n_out), lambda b: (0, 0))],
        out_specs=pl.BlockSpec((None, m_len, n_out), lambda b: (b, 0, 0)),
        compiler_params=_cparams(("parallel",)),
        name="mem_kv",
    )(mem, g.reshape(1, d), w_bf16)


def _od_proj_kernel(x_ref, g_ref, w_ref, ug_ref, *rest):
    hg_refs, slab = rest[:-1], rest[-1]
    bn, tq, d = x_ref.shape
    h = _rms(x_ref[...].reshape(bn * tq, d), g_ref[...]).astype(BF16)
    n_ug = 2 * LRU_W
    n_slabs = GROUP_W // LANES
    for c in range(n_ug // GROUP_W):
        acc = _dot(h, w_ref[:, c * GROUP_W:(c + 1) * GROUP_W])
        for b in range(bn):
            for j in range(n_slabs):
                slab[j, pl.ds(b, tq, stride=bn), :] = acc[b * tq:(b + 1) * tq, j * LANES:(j + 1) * LANES]
        for j in range(n_slabs):
            ug_ref[:, c * GROUP_W + j * LANES:c * GROUP_W + (j + 1) * LANES] = slab[j]
    for c, hg_ref in enumerate(hg_refs):
        acc = _dot(h, w_ref[:, n_ug + c * GROUP_W:n_ug + (c + 1) * GROUP_W])
        hg_ref[...] = acc.reshape(bn, tq, GROUP_W).astype(hg_ref.dtype)


HG_INPUT_DTYPES = (("q", BF16), ("f_fw", F32), ("f_bw", F32), ("i_in", BF16), ("g_out", BF16))


def _od_proj(x, g, w_bf16):
    bn, s_len, d = x.shape
    tq = min(ROW_TILE_WIDE // bn, s_len)
    n_ug = 2 * LRU_W
    n_hg = w_bf16.shape[1] - n_ug
    assert n_hg == len(HG_INPUT_DTYPES) * GROUP_W
    hg_spec = pl.BlockSpec((bn, tq, GROUP_W), lambda i: (0, i, 0))
    ug, *hg = pl.pallas_call(
        _od_proj_kernel,
        out_shape=[jax.ShapeDtypeStruct((s_len * bn, n_ug), F32)]
        + [jax.ShapeDtypeStruct((bn, s_len, GROUP_W), dt) for _, dt in HG_INPUT_DTYPES],
        grid=(s_len // tq,),
        in_specs=[pl.BlockSpec((bn, tq, d), lambda i: (0, i, 0)),
                  pl.BlockSpec((1, d), lambda i: (0, 0)),
                  _resident((d, n_ug + n_hg), lambda i: (0, 0))],
        out_specs=[pl.BlockSpec((tq * bn, n_ug), lambda i: (i, 0))] + [hg_spec] * len(HG_INPUT_DTYPES),
        scratch_shapes=[pltpu.VMEM((GROUP_W // LANES, tq * bn, LANES), F32)],
        compiler_params=_cparams(("parallel",)),
        name="od_proj",
    )(x, g.reshape(1, d), w_bf16)
    return ug.reshape(s_len, bn, n_ug), dict(zip([n for n, _ in HG_INPUT_DTYPES], hg))


def _gelu_tanh(x):
    return 0.5 * x * (1.0 + jnp.tanh(math.sqrt(2.0 / math.pi) * (x + 0.044715 * (x * x * x))))


def _rglru_kernel(*refs, reverse, n_chunks):
    if reverse:
        (ul_ref, u_ref, ur_ref, gate_ref, hf_ref, cw_ref, cb_ref, wa_ref, ba_ref, wx_ref, bx_ref, sp_ref,
         out_ref, ext, a_s, b_s, h_s, carry) = refs
    else:
        (ul_ref, u_ref, ur_ref, cw_ref, cb_ref, wa_ref, ba_ref, wx_ref, bx_ref, sp_ref,
         out_ref, ext, a_s, b_s, carry) = refs
        h_s = out_ref
    step = pl.program_id(0)
    chunk = (n_chunks - 1 - step) if reverse else step
    t_len, bn, c = u_ref.shape

    @pl.when(step == 0)
    def _():
        carry[...] = jnp.zeros_like(carry)

    ext[0:2] = jnp.where(chunk == 0, 0.0, ul_ref[...])
    ext[2:t_len + 2] = u_ref[...]
    ext[t_len + 2:t_len + 3] = jnp.where(chunk == n_chunks - 1, 0.0, ur_ref[...])
    uc = cb_ref[...].reshape(1, 1, c)
    for j in range(4):
        uc = uc + cw_ref[j:j + 1, :].reshape(1, 1, c) * ext[j:j + t_len]

    u2 = uc.reshape(t_len * bn, c)
    ub = u2.astype(BF16)
    half = c // 2

    def gate_tanh(w_ref, b_ref):
        z = jnp.concatenate([_dot(ub[:, :half], w_ref[0]), _dot(ub[:, half:], w_ref[1])], axis=-1)
        return jnp.tanh(z + b_ref[...])

    tr = gate_tanh(wa_ref, ba_ref)
    ti = gate_tanh(wx_ref, bx_ref)
    a = jnp.exp2(sp_ref[...] * tr + sp_ref[...])
    b = jnp.sqrt(1.0 - a * a) * ((0.5 * ti + 0.5) * u2)
    a_s[...] = a.reshape(t_len, bn, c)
    b_s[...] = b.reshape(t_len, bn, c)

    def body(k, h):
        t = (t_len - 1 - k) if reverse else k
        h = a_s[t] * h + b_s[t]
        h_s[t] = h
        return h

    carry[...] = lax.fori_loop(0, t_len, body, carry[...], unroll=8)

    if reverse:
        out_ref[...] = ((hf_ref[...] + h_s[...]) * _gelu_tanh(gate_ref[...])).astype(out_ref.dtype)


def _rglru(ug, hf, conv_w, conv_b, wa, ba, wx, bx, lam, bn, *, reverse):
    s_len = ug.shape[0]
    c = LRU_W
    ug3 = ug
    t_len = min(LRU_T, s_len)
    n_chunks = s_len // t_len

    def ck(i):
        return (n_chunks - 1 - i) if reverse else i

    half = c // 2
    nb = LRU_BLOCKS // 2

    def dense_halves(w):
        w = 0.5 * w.astype(F32).reshape(2, nb, c // LRU_BLOCKS, c // LRU_BLOCKS)
        eye = jnp.eye(nb, dtype=F32)
        return jnp.einsum('gnij,nm->gnimj', w, eye).reshape(2, half, half).astype(BF16)

    decay_scale = ((-0.5 * LRU_C * LOG2E) * jax.nn.softplus(-lam.astype(F32))).reshape(1, c)
    vec = pl.BlockSpec((1, c), lambda i: (0, 0))
    wspec = pl.BlockSpec((2, half, half), lambda i: (0, 0, 0))
    blk = pl.BlockSpec((t_len, bn, c), lambda i: (ck(i), 0, 0))
    in_specs = [
        pl.BlockSpec((2, bn, c), lambda i: (jnp.maximum(ck(i) * (t_len // 2) - 1, 0), 0, 0)),
        blk,
        pl.BlockSpec((1, bn, c), lambda i: (jnp.minimum((ck(i) + 1) * t_len, s_len - 1), 0, 0)),
    ]
    args = [ug3, ug3, ug3]
    scratch = [pltpu.VMEM((t_len + 3, bn, c), F32), pltpu.VMEM((t_len, bn, c), F32),
               pltpu.VMEM((t_len, bn, c), F32)]
    if reverse:
        in_specs += [pl.BlockSpec((t_len, bn, c), lambda i: (ck(i), 0, 1)), blk]
        args += [ug3, hf]
        scratch.append(pltpu.VMEM((t_len, bn, c), F32))
    scratch.append(pltpu.VMEM((bn, c), F32))
    in_specs += [pl.BlockSpec((4, c), lambda i: (0, 0)), vec, wspec, vec, wspec, vec, vec]
    args += [conv_w.astype(F32), conv_b.astype(F32).reshape(1, c), dense_halves(wa),
             0.5 * ba.astype(F32).reshape(1, c), dense_halves(wx), 0.5 * bx.astype(F32).reshape(1, c),
             decay_scale]
    return pl.pallas_call(
        functools.partial(_rglru_kernel, reverse=reverse, n_chunks=n_chunks),
        out_shape=jax.ShapeDtypeStruct((s_len, bn, c), BF16 if reverse else F32),
        grid=(n_chunks,),
        in_specs=in_specs,
        out_specs=blk,
        scratch_shapes=scratch,
        compiler_params=_cparams(("arbitrary",)),
        name="rglru_bw" if reverse else "rglru_fw",
    )(*args)


def _hg_matrices(c, reverse):
    n_levels = int(math.log2(c))
    t = np.arange(c)[:, None]
    r = np.arange(c)[None, :]
    mats, masks, upper = [], [], []
    if not reverse:
        mats.append(r <= t)
        mats.append(r > t)
    else:
        mats.append(r >= t)
        mats.append(r < t)
    for lev in range(n_levels):
        half = c >> (lev + 1)
        parent = 2 * half
        start = (np.arange(c) // parent) * parent
        mid = (start + half)[:, None]
        later = (np.arange(c) % parent >= half)[:, None]
        if not reverse:
            m = np.where(later, (r >= mid) & (r <= t), (r > t) & (r < mid))
            is_q = later
        else:
            m = np.where(later, (r >= mid) & (r < t), (r >= t) & (r < mid))
            is_q = ~later
        mats.append(m)
        same_parent = (start[:, None] == start[None, :])
        masks.append(same_parent & is_q & (~is_q).T)
        upper.append(np.broadcast_to(is_q, (c, HG_DK)))
    mat = np.concatenate(mats, axis=0).astype(np.float32)
    return (np.concatenate([mat, mat], axis=1), np.stack(masks).astype(np.float32),
            np.stack(upper).astype(np.float32))


def _hgrn_kernel(*refs, reverse, n_steps, n_levels):
    if reverse:
        (q_ref, f_ref, v_ref, go_ref, of_ref, lb_ref, gn_ref, mat_ref, mask_ref, isq_ref, out_ref,
         state, x_scr, oi_scr, aux_scr, dec_scr, kk_scr) = refs
    else:
        (q_ref, f_ref, v_ref, lb_ref, mat_ref, mask_ref, isq_ref, out_ref,
         state, x_scr, oi_scr, aux_scr, dec_scr, kk_scr) = refs
    step = pl.program_id(1)
    c = HG_CHUNK
    t_len = q_ref.shape[0]
    n_chunks = t_len // c

    @pl.when(step == 0)
    def _():
        state[...] = jnp.zeros_like(state)

    def chunk_phases(ci, k):
        ch = (n_chunks - 1 - ci) if reverse else ci
        rows = slice(ch * c, (ch + 1) * c)
        edge = (c - 1) if not reverse else 0
        kk_s, dec_s, x_s, oi_s, aux_s = kk_scr.at[k], dec_scr.at[k], x_scr.at[k], oi_scr.at[k], aux_scr.at[k]
        for hp in range(HG_HEADS // 2):
            cols2 = slice(2 * hp * HG_DK, 2 * (hp + 1) * HG_DK)
            fl = f_ref[rows, cols2]
            e = jnp.exp(-jnp.abs(fl))
            r = 1.0 / (1.0 + e)
            er = e * r
            pos = fl > 0.0
            lb = lb_ref[0:1, cols2]
            oml = lb_ref[1:2, cols2]
            g2 = jnp.maximum(jnp.log2(lb + oml * jnp.where(pos, r, er)), F32_MIN_EXPONENT)
            kk_s[hp] = oml * jnp.where(pos, er, r)
            g_hi = g2.astype(BF16)
            g_lo = (g2 - g_hi.astype(F32)).astype(BF16)
            dec_s[hp] = jnp.exp2(_dot(mat_ref[...], jnp.concatenate([g_hi, g_lo], axis=0)))
        yield

        for hd in range(HG_HEADS):
            cols = slice(hd * HG_DK, (hd + 1) * HG_DK)
            pcols = slice((hd % 2) * HG_DK, (hd % 2 + 1) * HG_DK)
            kk = kk_s[hd // 2, :, pcols]
            dec = dec_s.at[hd // 2]
            qv = q_ref[rows, cols].astype(F32)
            qq = qv * jax.nn.sigmoid(qv)
            qb, kb = qq.astype(BF16), kk.astype(BF16)
            d_in = dec[0:c, pcols]
            x_s[hd, n_levels] = (qq * d_in).astype(BF16)
            x_s[hd, n_levels + 1] = (kk * dec[c:2 * c, pcols]).astype(BF16)
            aux_s[hd, 0] = jnp.broadcast_to(d_in[edge:edge + 1, :], (c, HG_DK))
            aux_s[hd, 1] = jnp.broadcast_to(jnp.sum(qq * kk, axis=-1, keepdims=True), (c, HG_DK))
            for lev in range(n_levels):
                half = c >> (lev + 1)
                if half >= SUBLANES_BF16:
                    first_is_q = bool(reverse)
                    base = jnp.concatenate(
                        [(qb if (blk % 2 == 1) != first_is_q else kb)[blk * half:(blk + 1) * half]
                         for blk in range(c // half)], axis=0)
                else:
                    base = jnp.where(isq_ref[lev] > 0.5, qb, kb)
                x_s[hd, lev] = base * dec[(2 + lev) * c:(3 + lev) * c, pcols].astype(BF16)
        yield

        for hd in range(HG_HEADS):
            cols = slice(hd * HG_DK, (hd + 1) * HG_DK)
            att = None
            for lev in range(n_levels):
                xl = x_s[hd, lev]
                term = mask_ref[lev] * _nt_dot(xl, xl)
                att = term if att is None else att + term
            st = state[hd]
            oi_s[hd] = _nt_dot(x_s[hd, n_levels], st.astype(BF16))
            vt = v_ref[rows, cols].astype(F32).T.astype(BF16)
            state[hd] = st * aux_s[hd, 0] + _dot(vt, x_s[hd, n_levels + 1])
            x_s[hd, 0] = att.astype(BF16)
        yield

        for hd in range(HG_HEADS):
            cols = slice(hd * HG_DK, (hd + 1) * HG_DK)
            vb = v_ref[rows, cols]
            o = oi_s[hd] + _dot(x_s[hd, 0], vb) + aux_s[hd, 1] * vb.astype(F32)

            if reverse:
                o = o + of_ref[rows, cols]
                o = o * lax.rsqrt(jnp.mean(o * o, axis=-1, keepdims=True) + EPS) * gn_ref[...]
                gv = go_ref[rows, cols].astype(F32)
                out_ref[rows, cols] = (o * (gv * jax.nn.sigmoid(gv))).astype(out_ref.dtype)
            else:
                out_ref[rows, cols] = o
        yield

    for g0 in range(0, n_chunks, HG_GROUP):
        group = [chunk_phases(ci, k) for k, ci in enumerate(range(g0, min(g0 + HG_GROUP, n_chunks)))]
        for _ in range(4):
            for phases in group:
                next(phases)


def _hgrn(hg, o_fw, lb, gnorm_g, *, reverse):
    bn, s_len, _ = hg["q"].shape
    w = HG_HEADS * HG_DK
    t_len = min(HG_T, s_len)
    n_steps = s_len // t_len
    mats, masks, isq = _hg_matrices(HG_CHUNK, reverse)
    n_levels = masks.shape[0]
    lbf = lb.astype(F32)
    lb_tab = jnp.stack([lbf, 1.0 - lbf])

    blk = pl.BlockSpec((None, t_len, w), lambda b, i: (b, (n_steps - 1 - i) if reverse else i, 0))

    def const(shape):
        return pl.BlockSpec(shape, lambda b, i: (0,) * len(shape))

    in_specs = [blk, blk, blk]
    args = [hg["q"], hg["f_bw" if reverse else "f_fw"], hg["i_in"]]
    if reverse:
        in_specs += [blk, blk]
        args += [hg["g_out"], o_fw]
    in_specs.append(const((2, w)))
    args.append(lb_tab)
    if reverse:
        in_specs.append(const((1, HG_DK)))
        args.append(gnorm_g.astype(F32).reshape(1, HG_DK))
    in_specs += [const(mats.shape), const(masks.shape), const(isq.shape)]
    args += [jnp.asarray(mats, BF16), jnp.asarray(masks, F32), jnp.asarray(isq, BF16)]
    return pl.pallas_call(
        functools.partial(_hgrn_kernel, reverse=reverse, n_steps=n_steps, n_levels=n_levels),
        out_shape=jax.ShapeDtypeStruct((bn, s_len, w), BF16 if reverse else F32),
        grid=(bn, n_steps),
        in_specs=in_specs,
        out_specs=blk,
        scratch_shapes=[pltpu.VMEM((HG_HEADS, HG_DK, HG_DK), F32),
                        pltpu.VMEM((HG_GROUP, HG_HEADS, n_levels + 2, HG_CHUNK, HG_DK), BF16),
                        pltpu.VMEM((HG_GROUP, HG_HEADS, HG_CHUNK, HG_DK), F32),
                        pltpu.VMEM((HG_GROUP, HG_HEADS, 2, HG_CHUNK, HG_DK), F32),
                        pltpu.VMEM((HG_GROUP, HG_HEADS // 2, (n_levels + 2) * HG_CHUNK, 2 * HG_DK), F32),
                        pltpu.VMEM((HG_GROUP, HG_HEADS // 2, HG_CHUNK, 2 * HG_DK), F32)],
        compiler_params=_cparams(("parallel", "arbitrary")),
        name="hgrn_bw" if reverse else "hgrn_fw",
    )(*args)


def _even_layer(x, g, w_in, w_out, rpb, tail):
    qkv, view4, view16 = _ev_proj(x, g, w_in.astype(BF16))
    outs, stats = zip(*[_dilated_branch(view, dil) for dil, view in zip(DILATIONS, (qkv, view4, view16))])
    yb = _neighborhood(qkv, rpb)
    return _ev_tail(x, outs, stats, yb, w_out.astype(BF16), tail)


def _odd_layer(x, g, w_in, w_out, conv_w, conv_b, wa, ba, wx, bx, lam, lb_f, lb_b, gnorm_g, tail):
    bn, s_len, _ = x.shape
    ug, hg = _od_proj(x, g, w_in.astype(BF16))
    h_fw = _rglru(ug, None, conv_w, conv_b, wa[0], ba[0], wx[0], bx[0], lam[0], bn, reverse=False)
    yc = _rglru(ug, h_fw, conv_w, conv_b, wa[1], ba[1], wx[1], bx[1], lam[1], bn, reverse=True)
    o_fw = _hgrn(hg, None, lb_f, gnorm_g, reverse=False)
    yd = _hgrn(hg, o_fw, lb_b, gnorm_g, reverse=True)
    return _od_tail(x, yc.reshape(s_len, bn * LRU_W), yd, w_out.astype(BF16), tail)


def kernel(x, mem, norm_mix_g, norm_xa_g, norm_mem_g, norm_mlp_g, final_norm_g, ev_w_in, ev_w_out, na_rpb,
           od_w_in, od_w_out, conv_w, conv_b, lru_wa, lru_ba, lru_wx, lru_bx, lru_lambda, hgrn_lb_logits,
           hgrn_norm_g, xa_wq, xa_wkv, xa_wo, mlp_w1, mlp_w2):
    depth = norm_mix_g.shape[0]
    p_lb = jax.nn.softmax(hgrn_lb_logits.astype(F32), axis=0)
    lower_bounds = jnp.cumsum(p_lb, axis=0) - p_lb[0:1]
    for layer in range(depth):
        tail = dict(g_xa=norm_xa_g[layer], wq=xa_wq[layer].astype(BF16),
                    kv=_mem_kv(mem, norm_mem_g[layer], xa_wkv[layer].astype(BF16)),
                    wo=xa_wo[layer].astype(BF16), g_mlp=norm_mlp_g[layer],
                    w1=mlp_w1[layer].astype(BF16), w2=mlp_w2[layer].astype(BF16),
                    g_final=final_norm_g if layer == depth - 1 else None)
        if layer % 2 == 0:
            e = layer // 2
            x = _even_layer(x, norm_mix_g[layer], ev_w_in[e], ev_w_out[e], na_rpb[e], tail)
        else:
            o = layer // 2
            x = _odd_layer(x, norm_mix_g[layer], od_w_in[o], od_w_out[o], conv_w[o], conv_b[o],
                           lru_wa[o], lru_ba[o], lru_wx[o], lru_bx[o], lru_lambda[o],
                           lower_bounds[layer, 0], lower_bounds[layer, 1], hgrn_norm_g[o], tail)
        x = _mlp(x, tail)
    return x
```
